```python
import math
import jax, jax.numpy as jnp
from jax import lax
import numpy as np

D_MODEL = 1024
BATCH = 16
SEQ = 4096
DEPTH = 4

N_MIXERS = 3
CONV_WIDTH = 31
POOL_WINDOWS = (2, 4, 8, 16)
N_POOL_GROUPS = len(POOL_WINDOWS)
POOL_GROUP_DIM = D_MODEL // N_POOL_GROUPS
N_HEADS = 16
HEAD_DIM = D_MODEL // N_HEADS
Q_BLOCK = 128
D_FF = ((8 * D_MODEL // 3 + 127) // 128) * 128
FFN_CONV_WIDTH = 3
EPS = 1e-6
N_A = (DEPTH + 2) // 3
N_B = (DEPTH + 1) // 3
N_C = DEPTH // 3

kernel_name = "hybrid_conv_pool_fox_trunk"


def rms_norm(x, g):
    x32 = x.astype(jnp.float32)
    y = x32 * lax.rsqrt(jnp.mean(x32 * x32, axis=-1, keepdims=True) + EPS)
    return (y * g.astype(jnp.float32)).astype(x.dtype)


def causal_dwconv(x, w, b):
    k_width, c = w.shape
    y = lax.conv_general_dilated(x, w[:, None, :].astype(x.dtype), window_strides=(1,),
                                 padding=[(k_width - 1, 0)],
                                 dimension_numbers=("NWC", "WIO", "NWC"),
                                 feature_group_count=c)
    return y + b.astype(x.dtype)


def conformer_conv(h, w_in, b_in, dw, dw_b, ln_g, ln_b, w_out, b_out):
    a, g = jnp.split(h @ w_in + b_in, 2, axis=-1)
    u = causal_dwconv(a * jax.nn.sigmoid(g), dw, dw_b)
    u32 = u.astype(jnp.float32)
    mu = jnp.mean(u32, axis=-1, keepdims=True)
    var = jnp.mean(jnp.square(u32 - mu), axis=-1, keepdims=True)
    u = ((u32 - mu) * lax.rsqrt(var + EPS) * ln_g.astype(jnp.float32) + ln_b.astype(jnp.float32)).astype(h.dtype)
    return jax.nn.silu(u) @ w_out + b_out


def multiscale_pool(h, w_grp, b_grp, scale):
    bsz, s, d = h.shape
    hg = h.reshape(bsz, s, N_POOL_GROUPS, POOL_GROUP_DIM).astype(jnp.float32)
    t = jnp.arange(s)
    outs = []
    for gi, w in enumerate(POOL_WINDOWS):
        xg = hg[:, :, gi]
        cs = jnp.cumsum(xg, axis=1)
        lag = jnp.pad(cs[:, :s - w], ((0, 0), (w, 0), (0, 0)))
        cnt = jnp.minimum(t + 1, w).astype(jnp.float32)[None, :, None]
        outs.append((cs - lag) / cnt - xg)
    p = jnp.stack(outs, axis=2).astype(h.dtype)
    y = jnp.einsum("bsgc,gcd->bsgd", p, w_grp) + b_grp
    return y.reshape(bsz, s, d) * scale


def fox_block_attention(q, k, v, c):
    bsz, nh, s, hd = q.shape
    nb = s // Q_BLOCK
    scale = 1.0 / math.sqrt(hd)
    qb = q.reshape(bsz, nh, nb, Q_BLOCK, hd).transpose(2, 0, 1, 3, 4)
    cb = c.reshape(bsz, nh, nb, Q_BLOCK).transpose(2, 0, 1, 3)
    pos = jnp.arange(s)
    qpos = pos.reshape(nb, Q_BLOCK)

    def one_block(args):
        qi, ci, pi = args
        logits = jnp.einsum("bhqd,bhkd->bhqk", qi, k).astype(jnp.float32) * scale
        logits = logits + ci[..., :, None] - c[:, :, None, :]
        mask = pi[:, None] >= pos[None, :]
        logits = jnp.where(mask[None, None], logits, -jnp.inf)
        probs = jax.nn.softmax(logits, axis=-1)
        return jnp.einsum("bhqk,bhkd->bhqd", probs.astype(v.dtype), v)

    out = lax.map(one_block, (qb, cb, qpos))
    return out.transpose(1, 2, 0, 3, 4).reshape(bsz, nh, s, hd)


def forgetting_attention(h, w_in, b_f, q_gain, k_gain, w_o):
    bsz, s, d = h.shape
    proj = h @ w_in
    q, k, v, fl = jnp.split(proj, [d, 2 * d, 3 * d], axis=-1)
    to_heads = lambda z: z.reshape(bsz, s, N_HEADS, HEAD_DIM).transpose(0, 2, 1, 3)
    q = rms_norm(to_heads(q), q_gain)
    k = rms_norm(to_heads(k), k_gain)
    v = to_heads(v)
    logf = jax.nn.log_sigmoid(fl.astype(jnp.float32) + b_f.astype(jnp.float32))
    c = jnp.cumsum(logf, axis=1).transpose(0, 2, 1)
    o = fox_block_attention(q, k, v, c)
    return o.transpose(0, 2, 1, 3).reshape(bsz, s, d) @ w_o


def conv_ffn(h, w_up, dw, dw_b, w_down):
    u = causal_dwconv(h @ w_up, dw, dw_b)
    val, gate = jnp.split(u, 2, axis=-1)
    return (jax.nn.silu(gate) * val) @ w_down


def _fwd_setup_inputs(seed: int = 0) -> dict:
    key = jax.random.key(seed)
    ks = iter(jax.random.split(key, 32))
    nrm = lambda shape, s: jax.random.normal(next(ks), shape, jnp.float32) * s
    D, F = D_MODEL, D_FF
    return {
        "x": nrm((BATCH, SEQ, D), 1.0),
        "norm_mix": 1.0 + nrm((DEPTH, D), 0.02),
        "norm_ffn": 1.0 + nrm((DEPTH, D), 0.02),
        "conv_w_in": nrm((N_A, D, 2 * D), D ** -0.5),
        "conv_b_in": nrm((N_A, 2 * D), 0.02),
        "conv_dw": nrm((N_A, CONV_WIDTH, D), CONV_WIDTH ** -0.5),
        "conv_dw_b": nrm((N_A, D), 0.02),
        "conv_ln_g": 1.0 + nrm((N_A, D), 0.02),
        "conv_ln_b": nrm((N_A, D), 0.02),
        "conv_w_out": nrm((N_A, D, D), D ** -0.5),
        "conv_b_out": nrm((N_A, D), 0.02),
        "pool_w": nrm((N_B, N_POOL_GROUPS, POOL_GROUP_DIM, POOL_GROUP_DIM), POOL_GROUP_DIM ** -0.5),
        "pool_b": nrm((N_B, N_POOL_GROUPS, POOL_GROUP_DIM), 0.02),
        "pool_scale": 0.5 + nrm((N_B, D), 0.05),
        "fox_w_in": nrm((N_C, D, 3 * D + N_HEADS), D ** -0.5),
        "fox_b_f": 2.0 + nrm((N_C, N_HEADS), 0.5),
        "fox_q_gain": 1.0 + nrm((N_C, HEAD_DIM), 0.02),
        "fox_k_gain": 1.0 + nrm((N_C, HEAD_DIM), 0.02),
        "fox_w_o": nrm((N_C, D, D), D ** -0.5),
        "ffn_w_up": nrm((DEPTH, D, 2 * F), D ** -0.5),
        "ffn_dw": nrm((DEPTH, FFN_CONV_WIDTH, 2 * F), FFN_CONV_WIDTH ** -0.5),
        "ffn_dw_b": nrm((DEPTH, 2 * F), 0.02),
        "ffn_w_down": nrm((DEPTH, F, D), F ** -0.5),
    }


def _fwd_reference(x, norm_mix, norm_ffn, conv_w_in, conv_b_in, conv_dw, conv_dw_b, conv_ln_g, conv_ln_b,
              conv_w_out, conv_b_out, pool_w, pool_b, pool_scale, fox_w_in, fox_b_f, fox_q_gain,
              fox_k_gain, fox_w_o, ffn_w_up, ffn_dw, ffn_dw_b, ffn_w_down):
    for i in range(DEPTH):
        j = i // N_MIXERS
        h = rms_norm(x, norm_mix[i])
        kind = i % N_MIXERS
        if kind == 0:
            y = conformer_conv(h, conv_w_in[j], conv_b_in[j], conv_dw[j], conv_dw_b[j],
                               conv_ln_g[j], conv_ln_b[j], conv_w_out[j], conv_b_out[j])
        elif kind == 1:
            y = multiscale_pool(h, pool_w[j], pool_b[j], pool_scale[j])
        else:
            y = forgetting_attention(h, fox_w_in[j], fox_b_f[j], fox_q_gain[j], fox_k_gain[j], fox_w_o[j])
        x = x + y
        x = x + conv_ffn(rms_norm(x, norm_ffn[i]), ffn_w_up[i], ffn_dw[i], ffn_dw_b[i], ffn_w_down[i])
    return x


import jax as _jax
import jax.numpy as _jnp

TWIN_FORMAT = 'train_step'
FWD_PARAMS = ['x', 'norm_mix', 'norm_ffn', 'conv_w_in', 'conv_b_in', 'conv_dw', 'conv_dw_b', 'conv_ln_g', 'conv_ln_b', 'conv_w_out', 'conv_b_out', 'pool_w', 'pool_b', 'pool_scale', 'fox_w_in', 'fox_b_f', 'fox_q_gain', 'fox_k_gain', 'fox_w_o', 'ffn_w_up', 'ffn_dw', 'ffn_dw_b', 'ffn_w_down']
TWIN_WEIGHTS = ['norm_mix', 'norm_ffn', 'conv_w_in', 'conv_b_in', 'conv_dw', 'conv_dw_b', 'conv_ln_g', 'conv_ln_b', 'conv_w_out', 'conv_b_out', 'pool_w', 'pool_b', 'pool_scale', 'fox_w_in', 'fox_b_f', 'fox_q_gain', 'fox_k_gain', 'fox_w_o', 'ffn_w_up', 'ffn_dw', 'ffn_dw_b', 'ffn_w_down']
TWIN_DIFF_INPUT = 'x'
TWIN_INPUTS = ['x', 'norm_mix', 'norm_ffn', 'conv_w_in', 'conv_b_in', 'conv_dw', 'conv_dw_b', 'conv_ln_g', 'conv_ln_b', 'conv_w_out', 'conv_b_out', 'pool_w', 'pool_b', 'pool_scale', 'fox_w_in', 'fox_b_f', 'fox_q_gain', 'fox_k_gain', 'fox_w_o', 'ffn_w_up', 'ffn_dw', 'ffn_dw_b', 'ffn_w_down', 'loss_target', 'm_norm_mix', 'm_norm_ffn', 'm_conv_w_in', 'm_conv_b_in', 'm_conv_dw', 'm_conv_dw_b', 'm_conv_ln_g', 'm_conv_ln_b', 'm_conv_w_out', 'm_conv_b_out', 'm_pool_w', 'm_pool_b', 'm_pool_scale', 'm_fox_w_in', 'm_fox_b_f', 'm_fox_q_gain', 'm_fox_k_gain', 'm_fox_w_o', 'm_ffn_w_up', 'm_ffn_dw', 'm_ffn_dw_b', 'm_ffn_w_down', 'v_norm_mix', 'v_norm_ffn', 'v_conv_w_in', 'v_conv_b_in', 'v_conv_dw', 'v_conv_dw_b', 'v_conv_ln_g', 'v_conv_ln_b', 'v_conv_w_out', 'v_conv_b_out', 'v_pool_w', 'v_pool_b', 'v_pool_scale', 'v_fox_w_in', 'v_fox_b_f', 'v_fox_q_gain', 'v_fox_k_gain', 'v_fox_w_o', 'v_ffn_w_up', 'v_ffn_dw', 'v_ffn_dw_b', 'v_ffn_w_down']
TWIN_OUTPUTS = ['loss', 'grad_x', 'grad_norm_mix', 'grad_norm_ffn', 'grad_conv_w_in', 'grad_conv_b_in', 'grad_conv_dw', 'grad_conv_dw_b', 'grad_conv_ln_g', 'grad_conv_ln_b', 'grad_conv_w_out', 'grad_conv_b_out', 'grad_pool_w', 'grad_pool_b', 'grad_pool_scale', 'grad_fox_w_in', 'grad_fox_b_f', 'grad_fox_q_gain', 'grad_fox_k_gain', 'grad_fox_w_o', 'grad_ffn_w_up', 'grad_ffn_dw', 'grad_ffn_dw_b', 'grad_ffn_w_down', 'delta_norm_mix', 'delta_norm_ffn', 'delta_conv_w_in', 'delta_conv_b_in', 'delta_conv_dw', 'delta_conv_dw_b', 'delta_conv_ln_g', 'delta_conv_ln_b', 'delta_conv_w_out', 'delta_conv_b_out', 'delta_pool_w', 'delta_pool_b', 'delta_pool_scale', 'delta_fox_w_in', 'delta_fox_b_f', 'delta_fox_q_gain', 'delta_fox_k_gain', 'delta_fox_w_o', 'delta_ffn_w_up', 'delta_ffn_dw', 'delta_ffn_dw_b', 'delta_ffn_w_down', 'new_m_norm_mix', 'new_m_norm_ffn', 'new_m_conv_w_in', 'new_m_conv_b_in', 'new_m_conv_dw', 'new_m_conv_dw_b', 'new_m_conv_ln_g', 'new_m_conv_ln_b', 'new_m_conv_w_out', 'new_m_conv_b_out', 'new_m_pool_w', 'new_m_pool_b', 'new_m_pool_scale', 'new_m_fox_w_in', 'new_m_fox_b_f', 'new_m_fox_q_gain', 'new_m_fox_k_gain', 'new_m_fox_w_o', 'new_m_ffn_w_up', 'new_m_ffn_dw', 'new_m_ffn_dw_b', 'new_m_ffn_w_down', 'new_v_norm_mix', 'new_v_norm_ffn', 'new_v_conv_w_in', 'new_v_conv_b_in', 'new_v_conv_dw', 'new_v_conv_dw_b', 'new_v_conv_ln_g', 'new_v_conv_ln_b', 'new_v_conv_w_out', 'new_v_conv_b_out', 'new_v_pool_w', 'new_v_pool_b', 'new_v_pool_scale', 'new_v_fox_w_in', 'new_v_fox_b_f', 'new_v_fox_q_gain', 'new_v_fox_k_gain', 'new_v_fox_w_o', 'new_v_ffn_w_up', 'new_v_ffn_dw', 'new_v_ffn_dw_b', 'new_v_ffn_w_down']
TWIN_LEAF_KINDS = {'loss': 'loss', 'grad_x': 'grad_x', 'grad_norm_mix': 'grad_w', 'grad_norm_ffn': 'grad_w', 'grad_conv_w_in': 'grad_w', 'grad_conv_b_in': 'grad_w', 'grad_conv_dw': 'grad_w', 'grad_conv_dw_b': 'grad_w', 'grad_conv_ln_g': 'grad_w', 'grad_conv_ln_b': 'grad_w', 'grad_conv_w_out': 'grad_w', 'grad_conv_b_out': 'grad_w', 'grad_pool_w': 'grad_w', 'grad_pool_b': 'grad_w', 'grad_pool_scale': 'grad_w', 'grad_fox_w_in': 'grad_w', 'grad_fox_b_f': 'grad_w', 'grad_fox_q_gain': 'grad_w', 'grad_fox_k_gain': 'grad_w', 'grad_fox_w_o': 'grad_w', 'grad_ffn_w_up': 'grad_w', 'grad_ffn_dw': 'grad_w', 'grad_ffn_dw_b': 'grad_w', 'grad_ffn_w_down': 'grad_w', 'delta_norm_mix': 'delta_w', 'delta_norm_ffn': 'delta_w', 'delta_conv_w_in': 'delta_w', 'delta_conv_b_in': 'delta_w', 'delta_conv_dw': 'delta_w', 'delta_conv_dw_b': 'delta_w', 'delta_conv_ln_g': 'delta_w', 'delta_conv_ln_b': 'delta_w', 'delta_conv_w_out': 'delta_w', 'delta_conv_b_out': 'delta_w', 'delta_pool_w': 'delta_w', 'delta_pool_b': 'delta_w', 'delta_pool_scale': 'delta_w', 'delta_fox_w_in': 'delta_w', 'delta_fox_b_f': 'delta_w', 'delta_fox_q_gain': 'delta_w', 'delta_fox_k_gain': 'delta_w', 'delta_fox_w_o': 'delta_w', 'delta_ffn_w_up': 'delta_w', 'delta_ffn_dw': 'delta_w', 'delta_ffn_dw_b': 'delta_w', 'delta_ffn_w_down': 'delta_w', 'new_m_norm_mix': 'new_m', 'new_m_norm_ffn': 'new_m', 'new_m_conv_w_in': 'new_m', 'new_m_conv_b_in': 'new_m', 'new_m_conv_dw': 'new_m', 'new_m_conv_dw_b': 'new_m', 'new_m_conv_ln_g': 'new_m', 'new_m_conv_ln_b': 'new_m', 'new_m_conv_w_out': 'new_m', 'new_m_conv_b_out': 'new_m', 'new_m_pool_w': 'new_m', 'new_m_pool_b': 'new_m', 'new_m_pool_scale': 'new_m', 'new_m_fox_w_in': 'new_m', 'new_m_fox_b_f': 'new_m', 'new_m_fox_q_gain': 'new_m', 'new_m_fox_k_gain': 'new_m', 'new_m_fox_w_o': 'new_m', 'new_m_ffn_w_up': 'new_m', 'new_m_ffn_dw': 'new_m', 'new_m_ffn_dw_b': 'new_m', 'new_m_ffn_w_down': 'new_m', 'new_v_norm_mix': 'new_v', 'new_v_norm_ffn': 'new_v', 'new_v_conv_w_in': 'new_v', 'new_v_conv_b_in': 'new_v', 'new_v_conv_dw': 'new_v', 'new_v_conv_dw_b': 'new_v', 'new_v_conv_ln_g': 'new_v', 'new_v_conv_ln_b': 'new_v', 'new_v_conv_w_out': 'new_v', 'new_v_conv_b_out': 'new_v', 'new_v_pool_w': 'new_v', 'new_v_pool_b': 'new_v', 'new_v_pool_scale': 'new_v', 'new_v_fox_w_in': 'new_v', 'new_v_fox_b_f': 'new_v', 'new_v_fox_q_gain': 'new_v', 'new_v_fox_k_gain': 'new_v', 'new_v_fox_w_o': 'new_v', 'new_v_ffn_w_up': 'new_v', 'new_v_ffn_dw': 'new_v', 'new_v_ffn_dw_b': 'new_v', 'new_v_ffn_w_down': 'new_v'}


def _forward(args):
    return _fwd_reference(*[args[k] for k in FWD_PARAMS])


def _output_shape():
    out = _jax.eval_shape(lambda: _forward(_fwd_setup_inputs(0)))
    return out.shape, out.dtype

N_MICROBATCH = 1
ADAM_LR = 0.001
ADAM_B1 = 0.9
ADAM_B2 = 0.999
ADAM_EPS = 1e-08
ADAM_WD = 0.01
ADAM_STEP = 10
PER_EXAMPLE_BATCH_AXIS = {'x': 0, 'loss_target': 0}
SHARED_INPUTS = []
_WEIGHT_DTYPES = {'norm_mix': _jnp.float32, 'norm_ffn': _jnp.float32, 'conv_w_in': _jnp.float32, 'conv_b_in': _jnp.float32, 'conv_dw': _jnp.float32, 'conv_dw_b': _jnp.float32, 'conv_ln_g': _jnp.float32, 'conv_ln_b': _jnp.float32, 'conv_w_out': _jnp.float32, 'conv_b_out': _jnp.float32, 'pool_w': _jnp.float32, 'pool_b': _jnp.float32, 'pool_scale': _jnp.float32, 'fox_w_in': _jnp.float32, 'fox_b_f': _jnp.float32, 'fox_q_gain': _jnp.float32, 'fox_k_gain': _jnp.float32, 'fox_w_o': _jnp.float32, 'ffn_w_up': _jnp.float32, 'ffn_dw': _jnp.float32, 'ffn_dw_b': _jnp.float32, 'ffn_w_down': _jnp.float32}
MOMENT_SCALE = {'norm_mix': 8.296337e+00, 'norm_ffn': 5.256689e+01, 'conv_w_in': 8.209680e-01, 'conv_b_in': 1.585180e+01, 'conv_dw': 1.920802e+00, 'conv_dw_b': 3.494305e+01, 'conv_ln_g': 3.216481e+01, 'conv_ln_b': 2.819803e+01, 'conv_w_out': 7.989604e+00, 'conv_b_out': 4.324596e+01, 'pool_w': 1.328940e+00, 'pool_b': 2.546460e+01, 'pool_scale': 2.367495e+01, 'fox_w_in': 2.896236e+00, 'fox_b_f': 1.912450e+02, 'fox_q_gain': 5.861146e+01, 'fox_k_gain': 5.880637e+01, 'fox_w_o': 4.041756e+00, 'ffn_w_up': 1.265804e+00, 'ffn_dw': 7.266421e+00, 'ffn_dw_b': 7.727593e+00, 'ffn_w_down': 1.171630e+00}


def _to_microbatches(a, axis):
    t = _jnp.moveaxis(a, axis, 0)
    t = t.reshape((N_MICROBATCH, t.shape[0] // N_MICROBATCH) + t.shape[1:])
    return _jnp.moveaxis(t, 1, axis + 1)


def setup_inputs(seed: int = 0) -> dict:
    inp = _fwd_setup_inputs(seed)
    key = _jax.random.fold_in(_jax.random.key(seed), 7919)
    shape, _ = _output_shape()
    out = dict(inp)
    out["loss_target"] = _jax.random.normal(_jax.random.fold_in(key, 0), shape, _jnp.float32)
    for i, name in enumerate(TWIN_WEIGHTS):
        w = inp[name].astype(_jnp.float32)
        if MOMENT_SCALE is None:
            s = _jnp.sqrt(_jnp.mean(_jnp.square(w)) + 1e-30)
        else:
            s = MOMENT_SCALE[name]
        km, kv = _jax.random.split(_jax.random.fold_in(key, i + 1))
        out[name] = w
        out["m_" + name] = s * _jax.random.normal(km, w.shape, _jnp.float32)
        out["v_" + name] = (s * s) * _jax.random.uniform(kv, w.shape, _jnp.float32, 0.5, 1.5)
    if N_MICROBATCH > 1:
        for name, axis in PER_EXAMPLE_BATCH_AXIS.items():
            out[name] = _to_microbatches(out[name], axis)
    return {'x': out['x'], 'norm_mix': out['norm_mix'], 'norm_ffn': out['norm_ffn'], 'conv_w_in': out['conv_w_in'], 'conv_b_in': out['conv_b_in'], 'conv_dw': out['conv_dw'], 'conv_dw_b': out['conv_dw_b'], 'conv_ln_g': out['conv_ln_g'], 'conv_ln_b': out['conv_ln_b'], 'conv_w_out': out['conv_w_out'], 'conv_b_out': out['conv_b_out'], 'pool_w': out['pool_w'], 'pool_b': out['pool_b'], 'pool_scale': out['pool_scale'], 'fox_w_in': out['fox_w_in'], 'fox_b_f': out['fox_b_f'], 'fox_q_gain': out['fox_q_gain'], 'fox_k_gain': out['fox_k_gain'], 'fox_w_o': out['fox_w_o'], 'ffn_w_up': out['ffn_w_up'], 'ffn_dw': out['ffn_dw'], 'ffn_dw_b': out['ffn_dw_b'], 'ffn_w_down': out['ffn_w_down'], 'loss_target': out['loss_target'], 'm_norm_mix': out['m_norm_mix'], 'm_norm_ffn': out['m_norm_ffn'], 'm_conv_w_in': out['m_conv_w_in'], 'm_conv_b_in': out['m_conv_b_in'], 'm_conv_dw': out['m_conv_dw'], 'm_conv_dw_b': out['m_conv_dw_b'], 'm_conv_ln_g': out['m_conv_ln_g'], 'm_conv_ln_b': out['m_conv_ln_b'], 'm_conv_w_out': out['m_conv_w_out'], 'm_conv_b_out': out['m_conv_b_out'], 'm_pool_w': out['m_pool_w'], 'm_pool_b': out['m_pool_b'], 'm_pool_scale': out['m_pool_scale'], 'm_fox_w_in': out['m_fox_w_in'], 'm_fox_b_f': out['m_fox_b_f'], 'm_fox_q_gain': out['m_fox_q_gain'], 'm_fox_k_gain': out['m_fox_k_gain'], 'm_fox_w_o': out['m_fox_w_o'], 'm_ffn_w_up': out['m_ffn_w_up'], 'm_ffn_dw': out['m_ffn_dw'], 'm_ffn_dw_b': out['m_ffn_dw_b'], 'm_ffn_w_down': out['m_ffn_w_down'], 'v_norm_mix': out['v_norm_mix'], 'v_norm_ffn': out['v_norm_ffn'], 'v_conv_w_in': out['v_conv_w_in'], 'v_conv_b_in': out['v_conv_b_in'], 'v_conv_dw': out['v_conv_dw'], 'v_conv_dw_b': out['v_conv_dw_b'], 'v_conv_ln_g': out['v_conv_ln_g'], 'v_conv_ln_b': out['v_conv_ln_b'], 'v_conv_w_out': out['v_conv_w_out'], 'v_conv_b_out': out['v_conv_b_out'], 'v_pool_w': out['v_pool_w'], 'v_pool_b': out['v_pool_b'], 'v_pool_scale': out['v_pool_scale'], 'v_fox_w_in': out['v_fox_w_in'], 'v_fox_b_f': out['v_fox_b_f'], 'v_fox_q_gain': out['v_fox_q_gain'], 'v_fox_k_gain': out['v_fox_k_gain'], 'v_fox_w_o': out['v_fox_w_o'], 'v_ffn_w_up': out['v_ffn_w_up'], 'v_ffn_dw': out['v_ffn_dw'], 'v_ffn_dw_b': out['v_ffn_dw_b'], 'v_ffn_w_down': out['v_ffn_w_down']}


def _loss(weights, diff, rest, loss_target):
    with _jax.named_scope("forward"):
        args = {**rest, TWIN_DIFF_INPUT: diff, **{k: w.astype(_WEIGHT_DTYPES[k]) for k, w in weights.items()}}
        y = _forward(args)
    with _jax.named_scope("loss_head"):
        err = _jnp.square(y.astype(_jnp.float32) - loss_target)
        return 0.5 * _jnp.sum(_jnp.mean(err, axis=-1)) if err.ndim else 0.5 * err


def _adamw(w, g, m, v):
    m = ADAM_B1 * m + (1.0 - ADAM_B1) * g
    v = ADAM_B2 * v + (1.0 - ADAM_B2) * _jnp.square(g)
    m_hat = m / (1.0 - ADAM_B1 ** ADAM_STEP)
    v_hat = v / (1.0 - ADAM_B2 ** ADAM_STEP)
    delta = -ADAM_LR * (m_hat / (_jnp.sqrt(v_hat) + ADAM_EPS) + ADAM_WD * w)
    return delta, m, v


def reference(x, norm_mix, norm_ffn, conv_w_in, conv_b_in, conv_dw, conv_dw_b, conv_ln_g, conv_ln_b, conv_w_out, conv_b_out, pool_w, pool_b, pool_scale, fox_w_in, fox_b_f, fox_q_gain, fox_k_gain, fox_w_o, ffn_w_up, ffn_dw, ffn_dw_b, ffn_w_down, loss_target, m_norm_mix, m_norm_ffn, m_conv_w_in, m_conv_b_in, m_conv_dw, m_conv_dw_b, m_conv_ln_g, m_conv_ln_b, m_conv_w_out, m_conv_b_out, m_pool_w, m_pool_b, m_pool_scale, m_fox_w_in, m_fox_b_f, m_fox_q_gain, m_fox_k_gain, m_fox_w_o, m_ffn_w_up, m_ffn_dw, m_ffn_dw_b, m_ffn_w_down, v_norm_mix, v_norm_ffn, v_conv_w_in, v_conv_b_in, v_conv_dw, v_conv_dw_b, v_conv_ln_g, v_conv_ln_b, v_conv_w_out, v_conv_b_out, v_pool_w, v_pool_b, v_pool_scale, v_fox_w_in, v_fox_b_f, v_fox_q_gain, v_fox_k_gain, v_fox_w_o, v_ffn_w_up, v_ffn_dw, v_ffn_dw_b, v_ffn_w_down):
    given = dict(x=x, norm_mix=norm_mix, norm_ffn=norm_ffn, conv_w_in=conv_w_in, conv_b_in=conv_b_in, conv_dw=conv_dw, conv_dw_b=conv_dw_b, conv_ln_g=conv_ln_g, conv_ln_b=conv_ln_b, conv_w_out=conv_w_out, conv_b_out=conv_b_out, pool_w=pool_w, pool_b=pool_b, pool_scale=pool_scale, fox_w_in=fox_w_in, fox_b_f=fox_b_f, fox_q_gain=fox_q_gain, fox_k_gain=fox_k_gain, fox_w_o=fox_w_o, ffn_w_up=ffn_w_up, ffn_dw=ffn_dw, ffn_dw_b=ffn_dw_b, ffn_w_down=ffn_w_down, loss_target=loss_target, m_norm_mix=m_norm_mix, m_norm_ffn=m_norm_ffn, m_conv_w_in=m_conv_w_in, m_conv_b_in=m_conv_b_in, m_conv_dw=m_conv_dw, m_conv_dw_b=m_conv_dw_b, m_conv_ln_g=m_conv_ln_g, m_conv_ln_b=m_conv_ln_b, m_conv_w_out=m_conv_w_out, m_conv_b_out=m_conv_b_out, m_pool_w=m_pool_w, m_pool_b=m_pool_b, m_pool_scale=m_pool_scale, m_fox_w_in=m_fox_w_in, m_fox_b_f=m_fox_b_f, m_fox_q_gain=m_fox_q_gain, m_fox_k_gain=m_fox_k_gain, m_fox_w_o=m_fox_w_o, m_ffn_w_up=m_ffn_w_up, m_ffn_dw=m_ffn_dw, m_ffn_dw_b=m_ffn_dw_b, m_ffn_w_down=m_ffn_w_down, v_norm_mix=v_norm_mix, v_norm_ffn=v_norm_ffn, v_conv_w_in=v_conv_w_in, v_conv_b_in=v_conv_b_in, v_conv_dw=v_conv_dw, v_conv_dw_b=v_conv_dw_b, v_conv_ln_g=v_conv_ln_g, v_conv_ln_b=v_conv_ln_b, v_conv_w_out=v_conv_w_out, v_conv_b_out=v_conv_b_out, v_pool_w=v_pool_w, v_pool_b=v_pool_b, v_pool_scale=v_pool_scale, v_fox_w_in=v_fox_w_in, v_fox_b_f=v_fox_b_f, v_fox_q_gain=v_fox_q_gain, v_fox_k_gain=v_fox_k_gain, v_fox_w_o=v_fox_w_o, v_ffn_w_up=v_ffn_w_up, v_ffn_dw=v_ffn_dw, v_ffn_dw_b=v_ffn_dw_b, v_ffn_w_down=v_ffn_w_down)
    weights = {n: given[n] for n in TWIN_WEIGHTS}
    shared = {n: given[n] for n in SHARED_INPUTS}
    per_example = {n: given[n] for n in ['x']}
    grad_fn = _jax.value_and_grad(_loss, argnums=(0, 1))

    def one_microbatch(ex, loss_target):
        ex = dict(ex)
        diff = ex.pop(TWIN_DIFF_INPUT)
        return grad_fn(weights, diff, {**shared, **ex}, loss_target)

    if N_MICROBATCH == 1:
        loss, (grad_w, grad_x) = one_microbatch(per_example, given["loss_target"])
    else:
        def body(carry, xs):
            loss_sum, grad_sum = carry
            l_k, (gw_k, gx_k) = one_microbatch(xs[0], xs[1])
            with _jax.named_scope("update"):
                return (loss_sum + l_k, _jax.tree.map(_jnp.add, grad_sum, gw_k)), gx_k

        init = (_jnp.zeros((), _jnp.float32), _jax.tree.map(_jnp.zeros_like, weights))
        (loss, grad_w), grad_x = _jax.lax.scan(body, init, (per_example, given["loss_target"]))
    with _jax.named_scope("update"):
        delta_w, new_m, new_v = {}, {}, {}
        for n in TWIN_WEIGHTS:
            delta_w[n], new_m[n], new_v[n] = _adamw(weights[n], grad_w[n], given["m_" + n], given["v_" + n])
    return (loss, grad_x, *[grad_w[n] for n in TWIN_WEIGHTS], *[delta_w[n] for n in TWIN_WEIGHTS],
            *[new_m[n] for n in TWIN_WEIGHTS], *[new_v[n] for n in TWIN_WEIGHTS])
```

```python
import math

import jax
import jax.numpy as jnp
from jax import lax
from jax.experimental import pallas as pl
from jax.experimental.pallas import tpu as pltpu

F32 = jnp.float32
BF16 = jnp.bfloat16
HI = lax.Precision.HIGHEST
MESH = pl.DeviceIdType.MESH
ANY = pl.BlockSpec(memory_space=pl.ANY)

EPS = 1e-6
HEAD_DIM = 64
LANES = 128
POOL_WINDOWS = (2, 4, 8, 16)
CONV_WIDTH = 31
CONV_HALO = 32
FFN_HALO = 8
POOL_HALO = 16
N_CHIPS = 4
NEG = -1e30

ADAM_LR = 0.001
ADAM_B1 = 0.9
ADAM_B2 = 0.999
ADAM_EPS = 1e-08
ADAM_WD = 0.01
ADAM_STEP = 10

V7X_VMEM_LIMIT_BYTES = 56 * 1024 * 1024


def _cp(*sem):
    return pltpu.CompilerParams(dimension_semantics=sem or None, vmem_limit_bytes=V7X_VMEM_LIMIT_BYTES)


def _tile(n, pref):
    t = min(n, pref)
    assert n % t == 0, (n, pref)
    return t


def _sig(v):
    return jax.nn.sigmoid(v)


def _roll(v, shift):
    n = v.shape[0]
    shift = shift % n
    return v if shift == 0 else pltpu.roll(v, shift, 0)


def _mm(a, b, *, m, n, k, name, tm=512, tn=512, tk=512, ta=False, tb=False, b_stk=None, b_s0=0,
        o_stk=None, o_s0=0, o_slots=None, o_buf=None, bias=None, res=None, out_dtype=F32):
    tm, tn, tk = _tile(m, tm), _tile(n, tn), _tile(k, tk)
    gi, gj, gk = m // tm, n // tn, k // tk
    a_spec = pl.BlockSpec((tk, tm), lambda j, i, kk: (kk, i)) if ta else pl.BlockSpec((tm, tk), lambda j, i, kk: (i, kk))
    if b_stk is None:
        b_spec = pl.BlockSpec((tn, tk), lambda j, i, kk: (j, kk)) if tb else pl.BlockSpec((tk, tn), lambda j, i, kk: (kk, j))
    elif tb:
        assert b_stk % tk == 0
        per = b_stk // tk
        b_spec = pl.BlockSpec((None, tn, tk), lambda j, i, kk: (b_s0 + kk // per, j, kk % per))
    else:
        assert b_stk % tn == 0
        per = b_stk // tn
        b_spec = pl.BlockSpec((None, tk, tn), lambda j, i, kk: (b_s0 + j // per, kk, j % per))
    ins, in_specs = [a, b], [a_spec, b_spec]
    if bias is not None:
        ins.append(bias)
        in_specs.append(pl.BlockSpec((1, tn), lambda j, i, kk: (0, j)))
    if res is not None:
        ins.append(res)
        in_specs.append(pl.BlockSpec((tm, tn), lambda j, i, kk: (i, j)))
    aliases = {}
    if o_stk is None:
        out_shape = jax.ShapeDtypeStruct((m, n), out_dtype)
        o_spec = pl.BlockSpec((tm, tn), lambda j, i, kk: (i, j))
    else:
        assert o_stk % tn == 0
        pero = o_stk // tn
        out_shape = jax.ShapeDtypeStruct((o_slots, m, o_stk), out_dtype)
        o_spec = pl.BlockSpec((None, tm, tn), lambda j, i, kk: (o_s0 + j // pero, i, j % pero))
        if o_buf is not None:
            aliases = {len(ins): 0}
            ins.append(o_buf)
            in_specs.append(ANY)
    has_bias, has_res, has_buf = bias is not None, res is not None, o_buf is not None
    dn = (((0 if ta else 1,), (1 if tb else 0,)), ((), ()))

    def body(*refs):
        a_ref, b_ref = refs[0], refs[1]
        pos = 2
        bias_ref = refs[pos] if has_bias else None
        pos += has_bias
        res_ref = refs[pos] if has_res else None
        pos += has_res + has_buf
        o_ref = refs[pos]
        p = lax.dot_general(a_ref[...].astype(BF16), b_ref[...].astype(BF16), dn, preferred_element_type=F32)

        def finish(acc):
            if has_bias:
                acc = acc + bias_ref[...]
            if has_res:
                acc = acc + res_ref[...]
            o_ref[...] = acc.astype(o_ref.dtype)

        if gk == 1:
            finish(p)
        else:
            acc_ref = refs[pos + 1]
            kk = pl.program_id(2)

            @pl.when(kk == 0)
            def _():
                acc_ref[...] = p

            @pl.when(kk > 0)
            def _():
                acc_ref[...] += p

            @pl.when(kk == gk - 1)
            def _():
                finish(acc_ref[...])

    return pl.pallas_call(
        body, grid=(gj, gi, gk), in_specs=in_specs, out_specs=o_spec, out_shape=out_shape,
        scratch_shapes=[pltpu.VMEM((tm, tn), F32)] if gk > 1 else [],
        input_output_aliases=aliases, name=name,
        compiler_params=_cp("parallel", "parallel", "arbitrary"),
    )(*ins)


def _rms_fwd(x, g, name):
    n, d = x.shape
    tm = _tile(n, 512)

    def body(x_ref, g_ref, h_ref):
        xv = x_ref[...]
        r = lax.rsqrt(jnp.mean(xv * xv, axis=-1, keepdims=True) + EPS)
        h_ref[...] = (xv * r * g_ref[...]).astype(h_ref.dtype)

    return pl.pallas_call(
        body, grid=(n // tm,),
        in_specs=[pl.BlockSpec((tm, d), lambda i: (i, 0)), pl.BlockSpec((1, d), lambda i: (0, 0))],
        out_specs=pl.BlockSpec((tm, d), lambda i: (i, 0)),
        out_shape=jax.ShapeDtypeStruct((n, d), BF16), name=name, compiler_params=_cp("arbitrary"),
    )(x, g)


def _rms_bwd(x, dh, g, dres, name, colsum=False):
    n, d = x.shape
    tm = _tile(n, 512)

    def body(x_ref, dh_ref, g_ref, dres_ref, dx_ref, dg_ref, *rest):
        i = pl.program_id(0)
        xv, dhv = x_ref[...], dh_ref[...]
        r = lax.rsqrt(jnp.mean(xv * xv, axis=-1, keepdims=True) + EPS)
        xn = xv * r
        dxn = dhv * g_ref[...]
        dx_ref[...] = dres_ref[...] + r * (dxn - xn * jnp.mean(dxn * xn, axis=-1, keepdims=True))
        dg = jnp.sum(dhv * xn, axis=0, keepdims=True)

        @pl.when(i == 0)
        def _():
            dg_ref[...] = jnp.zeros_like(dg_ref)
            if colsum:
                rest[0][...] = jnp.zeros_like(rest[0])

        dg_ref[...] += dg
        if colsum:
            rest[0][...] += jnp.sum(dres_ref[...], axis=0, keepdims=True)

    row = pl.BlockSpec((tm, d), lambda i: (i, 0))
    vec = pl.BlockSpec((1, d), lambda i: (0, 0))
    out_shape = [jax.ShapeDtypeStruct((n, d), F32), jax.ShapeDtypeStruct((1, d), F32)]
    out_specs = [row, vec]
    if colsum:
        out_shape.append(jax.ShapeDtypeStruct((1, d), F32))
        out_specs.append(vec)
    return pl.pallas_call(
        body, grid=(n // tm,), in_specs=[row, row, vec, row], out_specs=out_specs, out_shape=out_shape,
        name=name, compiler_params=_cp("arbitrary"),
    )(x, dh, g, dres)


def _loss(y, tgt, name):
    n, d = y.shape
    tm = _tile(n, 512)

    def body(y_ref, t_ref, dy_ref, l_ref):
        i = pl.program_id(0)
        e = y_ref[...] - t_ref[...]
        dy_ref[...] = e / d
        part = 0.5 * jnp.sum(jnp.mean(e * e, axis=-1, keepdims=True), axis=0, keepdims=True)

        @pl.when(i == 0)
        def _():
            l_ref[...] = jnp.zeros_like(l_ref)

        l_ref[...] += part

    row = pl.BlockSpec((tm, d), lambda i: (i, 0))
    return pl.pallas_call(
        body, grid=(n // tm,), in_specs=[row, row],
        out_specs=[row, pl.BlockSpec((1, 1), lambda i: (0, 0))],
        out_shape=[jax.ShapeDtypeStruct((n, d), F32), jax.ShapeDtypeStruct((1, 1), F32)],
        name=name, compiler_params=_cp("arbitrary"),
    )(y, tgt)


def _ffn_specs(n, f, tm, tc, seq):
    hb = FFN_HALO
    cur = pl.BlockSpec((tm, tc), lambda j, i: (i, j))
    prev = pl.BlockSpec((hb, tc), lambda j, i: (jnp.maximum(i * (tm // hb) - 1, 0), j))
    nxt = pl.BlockSpec((hb, tc), lambda j, i: (jnp.minimum((i + 1) * (tm // hb), n // hb - 1), j))
    taps = pl.BlockSpec((3, tc), lambda j, i: (0, j))
    vec = pl.BlockSpec((1, tc), lambda j, i: (0, j))
    return cur, prev, nxt, taps, vec


def _ffn_glu_fwd(uv, ug, wv, wg, bv, bg, seq, name):
    n, f = uv.shape
    tm, tc = _tile(seq, 512), _tile(f, 256)
    tps = seq // tm
    cur, prev, _, taps, vec = _ffn_specs(n, f, tm, tc, seq)

    def body(uvp, uvc, ugp, ugc, wv_ref, wg_ref, bv_ref, bg_ref, a_ref):
        first = (pl.program_id(1) % tps) == 0

        def conv(p_ref, c_ref, w_ref, b_ref):
            xs = jnp.concatenate([jnp.where(first, 0.0, p_ref[...]), c_ref[...]], axis=0)
            w = w_ref[...]
            y = w[2:3] * xs + w[1:2] * _roll(xs, 1) + w[0:1] * _roll(xs, 2)
            return y[FFN_HALO:] + b_ref[...]

        val = conv(uvp, uvc, wv_ref, bv_ref)
        gate = conv(ugp, ugc, wg_ref, bg_ref)
        a_ref[...] = (gate * _sig(gate) * val).astype(a_ref.dtype)

    return pl.pallas_call(
        body, grid=(f // tc, n // tm), in_specs=[prev, cur, prev, cur, taps, taps, vec, vec],
        out_specs=cur, out_shape=jax.ShapeDtypeStruct((n, f), BF16), name=name,
        compiler_params=_cp("parallel", "arbitrary"),
    )(uv, uv, ug, ug, wv, wg, bv, bg)


def _ffn_glu_bwd(uv, ug, da, wv, wg, bv, bg, seq, name):
    n, f = uv.shape
    tm, tc = _tile(seq, 512), _tile(f, 256)
    tps = seq // tm
    hb = FFN_HALO
    ext = tm + hb
    cur, prev, nxt, taps, vec = _ffn_specs(n, f, tm, tc, seq)

    def body(uvp, uvc, uvn, ugp, ugc, ugn, da_c, da_n, wv_ref, wg_ref, bv_ref, bg_ref,
             duv_ref, dug_ref, dwv_ref, dwg_ref, dbv_ref, dbg_ref):
        i = pl.program_id(1)
        first = (i % tps) == 0
        last = (i % tps) == tps - 1
        da_e = jnp.concatenate([da_c[...], jnp.where(last, 0.0, da_n[...])], axis=0)

        def taps3(p_ref, c_ref, n_ref):
            xs = jnp.concatenate([jnp.where(first, 0.0, p_ref[...]), c_ref[...], n_ref[...]], axis=0)
            return xs, _roll(xs, 1), _roll(xs, 2)

        xv, xg = taps3(uvp, uvc, uvn), taps3(ugp, ugc, ugn)
        wv_, wg_ = wv_ref[...], wg_ref[...]

        def conv(xs, w, b_ref):
            return (w[2:3] * xs[0] + w[1:2] * xs[1] + w[0:1] * xs[2])[hb:] + b_ref[...]

        val, gate = conv(xv, wv_, bv_ref), conv(xg, wg_, bg_ref)
        sg = _sig(gate)
        dval = da_e * (gate * sg)
        dgate = da_e * val * (sg * (1.0 + gate * (1.0 - sg)))

        def conv_t(dv, w):
            return (w[2:3] * dv + w[1:2] * _roll(dv, ext - 1) + w[0:1] * _roll(dv, ext - 2))[:tm]

        duv_ref[...] = conv_t(dval, wv_).astype(duv_ref.dtype)
        dug_ref[...] = conv_t(dgate, wg_).astype(dug_ref.dtype)

        def tap_grads(d_own, xs):
            return jnp.concatenate(
                [jnp.sum(d_own * xs[2 - kk][hb:hb + tm], axis=0, keepdims=True) for kk in range(3)], axis=0)

        dv_own, dg_own = dval[:tm], dgate[:tm]

        @pl.when(i == 0)
        def _():
            for r in (dwv_ref, dwg_ref, dbv_ref, dbg_ref):
                r[...] = jnp.zeros_like(r)

        dwv_ref[...] += tap_grads(dv_own, xv)
        dwg_ref[...] += tap_grads(dg_own, xg)
        dbv_ref[...] += jnp.sum(dv_own, axis=0, keepdims=True)
        dbg_ref[...] += jnp.sum(dg_own, axis=0, keepdims=True)

    return pl.pallas_call(
        body, grid=(f // tc, n // tm),
        in_specs=[prev, cur, nxt, prev, cur, nxt, cur, nxt, taps, taps, vec, vec],
        out_specs=[cur, cur, taps, taps, vec, vec],
        out_shape=[jax.ShapeDtypeStruct((n, f), BF16), jax.ShapeDtypeStruct((n, f), BF16),
                   jax.ShapeDtypeStruct((3, f), F32), jax.ShapeDtypeStruct((3, f), F32),
                   jax.ShapeDtypeStruct((1, f), F32), jax.ShapeDtypeStruct((1, f), F32)],
        name=name, compiler_params=_cp("parallel", "arbitrary"),
    )(uv, uv, uv, ug, ug, ug, da, da, wv, wg, bv, bg)


def _conf_specs(n, d, tm):
    hb = CONV_HALO
    cur = pl.BlockSpec((tm, d), lambda i: (i, 0))
    prev = pl.BlockSpec((hb, d), lambda i: (jnp.maximum(i * (tm // hb) - 1, 0), 0))
    nxt = pl.BlockSpec((hb, d), lambda i: (jnp.minimum((i + 1) * (tm // hb), n // hb - 1), 0))
    taps = pl.BlockSpec((CONV_HALO, d), lambda i: (0, 0))
    vec = pl.BlockSpec((1, d), lambda i: (0, 0))
    return cur, prev, nxt, taps, vec


def _conf_fwd(pa, pg, w, wb, lng, lnb, seq, name):
    n, d = pa.shape
    tm = _tile(seq, 256)
    tps = seq // tm
    hb = CONV_HALO
    cur, prev, _, taps, vec = _conf_specs(n, d, tm)

    def body(pap, pac, pgp, pgc, w_ref, wb_ref, lng_ref, lnb_ref, u_ref, s_ref):
        first = (pl.program_id(0) % tps) == 0
        a = jnp.concatenate([jnp.where(first, 0.0, pap[...]), pac[...]], axis=0)
        g = jnp.concatenate([jnp.where(first, 0.0, pgp[...]), pgc[...]], axis=0)
        z = a * _sig(g)
        acc = w_ref[CONV_WIDTH - 1:CONV_WIDTH, :] * z
        for sh in range(1, CONV_WIDTH):
            acc = acc + w_ref[CONV_WIDTH - 1 - sh:CONV_WIDTH - sh, :] * _roll(z, sh)
        u = acc[hb:] + wb_ref[...]
        mu = jnp.mean(u, axis=-1, keepdims=True)
        uc = u - mu
        var = jnp.mean(uc * uc, axis=-1, keepdims=True)
        ul = uc * lax.rsqrt(var + EPS) * lng_ref[...] + lnb_ref[...]
        u_ref[...] = u
        s_ref[...] = (ul * _sig(ul)).astype(s_ref.dtype)

    return pl.pallas_call(
        body, grid=(n // tm,), in_specs=[prev, cur, prev, cur, taps, vec, vec, vec],
        out_specs=[cur, cur],
        out_shape=[jax.ShapeDtypeStruct((n, d), F32), jax.ShapeDtypeStruct((n, d), BF16)],
        name=name, compiler_params=_cp("arbitrary"),
    )(pa, pa, pg, pg, w, wb, lng, lnb)


def _conf_bwd(u, ds, pa, pg, w, lng, lnb, seq, name):
    n, d = u.shape
    tm = _tile(seq, 256)
    tps = seq // tm
    hb = CONV_HALO
    ext = tm + hb
    cur, prev, nxt, taps, vec = _conf_specs(n, d, tm)

    def body(uc_ref, un_ref, dsc_ref, dsn_ref, pap, pac, pgp, pgc, w_ref, lng_ref, lnb_ref,
             dpa_ref, dpg_ref, dw_ref, dwb_ref, dlng_ref, dlnb_ref, dba_ref, dbg_ref):
        i = pl.program_id(0)
        first = (i % tps) == 0
        last = (i % tps) == tps - 1

        @pl.when(i == 0)
        def _():
            for r in (dw_ref, dwb_ref, dlng_ref, dlnb_ref, dba_ref, dbg_ref):
                r[...] = jnp.zeros_like(r)

        ue = jnp.concatenate([uc_ref[...], un_ref[...]], axis=0)
        dse = jnp.concatenate([dsc_ref[...], jnp.where(last, 0.0, dsn_ref[...])], axis=0)
        mu = jnp.mean(ue, axis=-1, keepdims=True)
        cen = ue - mu
        r = lax.rsqrt(jnp.mean(cen * cen, axis=-1, keepdims=True) + EPS)
        xn = cen * r
        ul = xn * lng_ref[...] + lnb_ref[...]
        sg = _sig(ul)
        dul = dse * (sg * (1.0 + ul * (1.0 - sg)))
        dun = dul * lng_ref[...]
        du = r * (dun - jnp.mean(dun, axis=-1, keepdims=True) - xn * jnp.mean(dun * xn, axis=-1, keepdims=True))
        dlng_ref[...] += jnp.sum((dul * xn)[:tm], axis=0, keepdims=True)
        dlnb_ref[...] += jnp.sum(dul[:tm], axis=0, keepdims=True)
        du_own = du[:tm]
        dwb_ref[...] += jnp.sum(du_own, axis=0, keepdims=True)

        dz = w_ref[CONV_WIDTH - 1:CONV_WIDTH, :] * du
        for sh in range(1, CONV_WIDTH):
            dz = dz + w_ref[CONV_WIDTH - 1 - sh:CONV_WIDTH - sh, :] * _roll(du, ext - sh)
        dz = dz[:tm]

        a = jnp.concatenate([jnp.where(first, 0.0, pap[...]), pac[...]], axis=0)
        g = jnp.concatenate([jnp.where(first, 0.0, pgp[...]), pgc[...]], axis=0)
        sgg = _sig(g)
        z = a * sgg
        for sh in range(CONV_WIDTH):
            kk = CONV_WIDTH - 1 - sh
            dw_ref[kk:kk + 1, :] += jnp.sum(du_own * _roll(z, sh)[hb:], axis=0, keepdims=True)

        a_c, sg_c = a[hb:], sgg[hb:]
        da = dz * sg_c
        dg = dz * a_c * sg_c * (1.0 - sg_c)
        dpa_ref[...] = da.astype(dpa_ref.dtype)
        dpg_ref[...] = dg.astype(dpg_ref.dtype)
        dba_ref[...] += jnp.sum(da, axis=0, keepdims=True)
        dbg_ref[...] += jnp.sum(dg, axis=0, keepdims=True)

    vshape = jax.ShapeDtypeStruct((1, d), F32)
    return pl.pallas_call(
        body, grid=(n // tm,),
        in_specs=[cur, nxt, cur, nxt, prev, cur, prev, cur, taps, vec, vec],
        out_specs=[cur, cur, taps, vec, vec, vec, vec, vec],
        out_shape=[jax.ShapeDtypeStruct((n, d), BF16), jax.ShapeDtypeStruct((n, d), BF16),
                   jax.ShapeDtypeStruct((CONV_HALO, d), F32), vshape, vshape, vshape, vshape, vshape],
        name=name, compiler_params=_cp("arbitrary"),
    )(u, u, ds, ds, pa, pa, pg, pg, w, lng, lnb)


def _pool_specs(n, d, tm, gd):
    hb = POOL_HALO
    cur = pl.BlockSpec((tm, d), lambda i: (i, 0))
    prev = pl.BlockSpec((hb, d), lambda i: (jnp.maximum(i * (tm // hb) - 1, 0), 0))
    nxt = pl.BlockSpec((hb, d), lambda i: (jnp.minimum((i + 1) * (tm // hb), n // hb - 1), 0))
    wsp = pl.BlockSpec((len(POOL_WINDOWS), gd, gd), lambda i: (0, 0, 0))
    vec = pl.BlockSpec((1, d), lambda i: (0, 0))
    return cur, prev, nxt, wsp, vec


def _pool_fwd(x, g, w, b, sc, seq, name):
    n, d = x.shape
    gd = d // len(POOL_WINDOWS)
    tm = _tile(seq, 256)
    tps = seq // tm
    hb = POOL_HALO
    cur, prev, _, wsp, vec = _pool_specs(n, d, tm, gd)

    def body(xp, xc, g_ref, w_ref, b_ref, sc_ref, x1_ref, p_ref):
        i = pl.program_id(0)
        first = (i % tps) == 0
        xe = jnp.concatenate([jnp.where(first, 0.0, xp[...]), xc[...]], axis=0)
        r = lax.rsqrt(jnp.mean(xe * xe, axis=-1, keepdims=True) + EPS)
        h = xe * r * g_ref[...]
        t = ((i % tps) * tm + lax.broadcasted_iota(jnp.int32, (tm, 1), 0) + 1).astype(F32)
        ys = []
        for gi, win in enumerate(POOL_WINDOWS):
            hg = h[:, gi * gd:(gi + 1) * gd]
            s, sh = hg, 1
            while sh < win:
                s = s + _roll(s, sh)
                sh *= 2
            p = (s[hb:] / jnp.minimum(t, float(win)) - hg[hb:]).astype(BF16)
            p_ref[:, gi * gd:(gi + 1) * gd] = p
            ys.append(jnp.dot(p, w_ref[gi], preferred_element_type=F32))
        y = jnp.concatenate(ys, axis=1) + b_ref[...]
        x1_ref[...] = xc[...] + y * sc_ref[...]

    return pl.pallas_call(
        body, grid=(n // tm,), in_specs=[prev, cur, vec, wsp, vec, vec], out_specs=[cur, cur],
        out_shape=[jax.ShapeDtypeStruct((n, d), F32), jax.ShapeDtypeStruct((n, d), BF16)],
        name=name, compiler_params=_cp("arbitrary"),
    )(x, x, g, w, b, sc)


def _pool_bwd(dx1, x, p, g, w, b, sc, seq, name):
    n, d = x.shape
    ng = len(POOL_WINDOWS)
    gd = d // ng
    tm = _tile(seq, 256)
    tps = seq // tm
    hb = POOL_HALO
    ext = tm + hb
    cur, _, nxt, wsp, vec = _pool_specs(n, d, tm, gd)

    def body(dc_ref, dn_ref, x_ref, p_ref, g_ref, w_ref, b_ref, sc_ref, dx_ref, dg_ref, dw_ref, db_ref, dsc_ref):
        i = pl.program_id(0)
        last = (i % tps) == tps - 1

        @pl.when(i == 0)
        def _():
            for r_ in (dg_ref, dw_ref, db_ref, dsc_ref):
                r_[...] = jnp.zeros_like(r_)

        dxc = dc_ref[...]
        dxe = jnp.concatenate([dxc, jnp.where(last, 0.0, dn_ref[...])], axis=0)
        dyg = dxe * sc_ref[...]
        t = ((i % tps) * tm + lax.broadcasted_iota(jnp.int32, (ext, 1), 0) + 1).astype(F32)
        dhs = []
        for gi, win in enumerate(POOL_WINDOWS):
            sl = slice(gi * gd, (gi + 1) * gd)
            dyb = dyg[:, sl].astype(BF16)
            wg = w_ref[gi]
            dp = lax.dot_general(dyb, wg, (((1,), (1,)), ((), ())), preferred_element_type=F32)
            s, sh = dp / jnp.minimum(t, float(win)), 1
            while sh < win:
                s = s + _roll(s, ext - sh)
                sh *= 2
            dhs.append((s - dp)[:tm])
            pg = p_ref[:, sl]
            dw_ref[gi] += lax.dot_general(pg, dyb[:tm], (((0,), (0,)), ((), ())), preferred_element_type=F32)
            ypre = jnp.dot(pg, wg, preferred_element_type=F32) + b_ref[:, sl]
            dsc_ref[:, sl] += jnp.sum(dxc[:, sl] * ypre, axis=0, keepdims=True)
            db_ref[:, sl] += jnp.sum(dyg[:tm, sl], axis=0, keepdims=True)
        dh = jnp.concatenate(dhs, axis=1)
        xv = x_ref[...]
        r = lax.rsqrt(jnp.mean(xv * xv, axis=-1, keepdims=True) + EPS)
        xn = xv * r
        dxn = dh * g_ref[...]
        dx_ref[...] = dxc + r * (dxn - xn * jnp.mean(dxn * xn, axis=-1, keepdims=True))
        dg_ref[...] += jnp.sum(dh * xn, axis=0, keepdims=True)

    vshape = jax.ShapeDtypeStruct((1, d), F32)
    return pl.pallas_call(
        body, grid=(n // tm,), in_specs=[cur, nxt, cur, cur, vec, wsp, vec, vec],
        out_specs=[cur, vec, wsp, vec, vec],
        out_shape=[jax.ShapeDtypeStruct((n, d), F32), vshape, jax.ShapeDtypeStruct((ng, gd, gd), F32), vshape, vshape],
        name=name, compiler_params=_cp("arbitrary"),
    )(dx1, dx1, x, p, g, w, b, sc)


def _head_maps(d):
    hd = lax.broadcasted_iota(jnp.int32, (d, LANES), 0) // HEAD_DIM
    col = lax.broadcasted_iota(jnp.int32, (d, LANES), 1)
    gm = (hd == col).astype(F32)
    hd_t = lax.broadcasted_iota(jnp.int32, (LANES, d), 1) // HEAD_DIM
    row = lax.broadcasted_iota(jnp.int32, (LANES, d), 0)
    gt = (hd_t == row).astype(F32)
    return gm, gt


def _fox_prep_fwd(qkv, fl, bf, qg, kg, seq, name):
    n, d3 = qkv.shape
    d = d3 // 3
    tm = _tile(seq, 256)
    tps = seq // tm

    def body(qkv_ref, fl_ref, bf_ref, qg_ref, kg_ref, q_ref, k_ref, v_ref, c_ref, carry):
        first = (pl.program_id(0) % tps) == 0
        gm, gt = _head_maps(d)

        def head_norm(xr, gain):
            ss = jnp.dot(xr * xr, gm, precision=HI, preferred_element_type=F32)
            r = lax.rsqrt(ss / HEAD_DIM + EPS)
            return xr * jnp.dot(r, gt, precision=HI, preferred_element_type=F32) * gain

        q_ref[...] = head_norm(qkv_ref[:, :d], qg_ref[...]).astype(BF16)
        k_ref[...] = head_norm(qkv_ref[:, d:2 * d], kg_ref[...]).astype(BF16)
        v_ref[...] = qkv_ref[:, 2 * d:].astype(BF16)
        z = fl_ref[...] + bf_ref[...]
        logf = jnp.minimum(z, 0.0) - jnp.log1p(jnp.exp(-jnp.abs(z)))
        tri = (lax.broadcasted_iota(jnp.int32, (tm, tm), 0) >= lax.broadcasted_iota(jnp.int32, (tm, tm), 1)).astype(F32)

        @pl.when(first)
        def _():
            carry[...] = jnp.zeros_like(carry)

        c = jnp.dot(tri, logf, precision=HI, preferred_element_type=F32) + carry[...]
        c_ref[...] = c
        carry[...] = c[tm - 1:tm, :]

    row = lambda w: pl.BlockSpec((tm, w), lambda i: (i, 0))
    vec = lambda w: pl.BlockSpec((1, w), lambda i: (0, 0))
    return pl.pallas_call(
        body, grid=(n // tm,), in_specs=[row(d3), row(LANES), vec(LANES), vec(d), vec(d)],
        out_specs=[row(d), row(d), row(d), row(LANES)],
        out_shape=[jax.ShapeDtypeStruct((n, d), BF16)] * 3 + [jax.ShapeDtypeStruct((n, LANES), F32)],
        scratch_shapes=[pltpu.VMEM((1, LANES), F32)], name=name, compiler_params=_cp("arbitrary"),
    )(qkv, fl, bf, qg, kg)


def _fox_prep_bwd(qkv, dq, dk, dv, dc1, dc2, fl, bf, qg, kg, seq, name):
    n, d3 = qkv.shape
    d = d3 // 3
    tm = _tile(seq, 256)
    tps = seq // tm
    nt = n // tm

    def body(qkv_ref, dq_ref, dk_ref, dv_ref, dc1_ref, dc2_ref, fl_ref, bf_ref, qg_ref, kg_ref,
             dqkv_ref, dfl_ref, dqg_ref, dkg_ref, dbf_ref, carry):
        i = pl.program_id(0)
        tile = nt - 1 - i
        last = (tile % tps) == tps - 1
        gm, gt = _head_maps(d)

        @pl.when(i == 0)
        def _():
            for r_ in (dqg_ref, dkg_ref, dbf_ref):
                r_[...] = jnp.zeros_like(r_)

        @pl.when(last)
        def _():
            carry[...] = jnp.zeros_like(carry)

        def head_norm_bwd(xr, dy, gain, dgain_ref):
            ss = jnp.dot(xr * xr, gm, precision=HI, preferred_element_type=F32)
            rf = jnp.dot(lax.rsqrt(ss / HEAD_DIM + EPS), gt, precision=HI, preferred_element_type=F32)
            xn = xr * rf
            dgain_ref[...] += jnp.sum(dy * xn, axis=0, keepdims=True)
            dyg = dy * gain
            mean = jnp.dot(dyg * xn, gm, precision=HI, preferred_element_type=F32) / HEAD_DIM
            return rf * (dyg - xn * jnp.dot(mean, gt, precision=HI, preferred_element_type=F32))

        dqkv_ref[:, :d] = head_norm_bwd(qkv_ref[:, :d], dq_ref[...], qg_ref[...], dqg_ref).astype(BF16)
        dqkv_ref[:, d:2 * d] = head_norm_bwd(qkv_ref[:, d:2 * d], dk_ref[...], kg_ref[...], dkg_ref).astype(BF16)
        dqkv_ref[:, 2 * d:] = dv_ref[...].astype(BF16)

        dc = dc1_ref[...] + dc2_ref[...]
        tri = (lax.broadcasted_iota(jnp.int32, (tm, tm), 0) <= lax.broadcasted_iota(jnp.int32, (tm, tm), 1)).astype(F32)
        dlog = jnp.dot(tri, dc, precision=HI, preferred_element_type=F32) + carry[...]
        carry[...] = dlog[0:1, :]
        dfl = dlog * (1.0 - _sig(fl_ref[...] + bf_ref[...]))
        dfl_ref[...] = dfl.astype(BF16)
        dbf_ref[...] += jnp.sum(dfl, axis=0, keepdims=True)

    row = lambda w: pl.BlockSpec((tm, w), lambda i: (nt - 1 - i, 0))
    vec = lambda w: pl.BlockSpec((1, w), lambda i: (0, 0))
    return pl.pallas_call(
        body, grid=(nt,),
        in_specs=[row(d3), row(d), row(d), row(d), row(LANES), row(LANES), row(LANES), vec(LANES), vec(d), vec(d)],
        out_specs=[row(d3), row(LANES), vec(d), vec(d), vec(LANES)],
        out_shape=[jax.ShapeDtypeStruct((n, d3), BF16), jax.ShapeDtypeStruct((n, LANES), BF16),
                   jax.ShapeDtypeStruct((1, d), F32), jax.ShapeDtypeStruct((1, d), F32),
                   jax.ShapeDtypeStruct((1, LANES), F32)],
        scratch_shapes=[pltpu.VMEM((1, LANES), F32)], name=name, compiler_params=_cp("arbitrary"),
    )(qkv, dq, dk, dv, dc1, dc2, fl, bf, qg, kg)


def _attn_specs(bsz, seq, t, hp):
    nb = seq // t
    blk = pl.BlockSpec((t, LANES), lambda b, h, i: (b * nb + i, h))
    full = pl.BlockSpec((seq, LANES), lambda b, h, i: (b, h))
    col = pl.BlockSpec((None, None, t, 2), lambda b, h, i: (b, h, i, 0))
    rows = pl.BlockSpec((None, None, nb, 2, t), lambda b, h, i: (b, h, 0, 0, 0))
    return nb, blk, full, col, rows


_NT = (((1,), (1,)), ((), ()))


def _flash_fwd(q, k, v, ccol, crow, bsz, seq, name):
    n, d = q.shape
    hp = d // LANES
    t = _tile(seq, 256)
    nb, blk, full, col, rows = _attn_specs(bsz, seq, t, hp)
    scale = 1.0 / math.sqrt(HEAD_DIM)

    def body(q_ref, k_ref, v_ref, cc_ref, cr_ref, o_ref, lse_ref):
        i = pl.program_id(2)
        causal = lax.broadcasted_iota(jnp.int32, (t, t), 0) >= lax.broadcasted_iota(jnp.int32, (t, t), 1)
        for hh in range(2):
            sl = slice(hh * HEAD_DIM, (hh + 1) * HEAD_DIM)
            qh = q_ref[:, sl]
            cc = cc_ref[:, hh:hh + 1]

            def block(j, carry, masked):
                m, l, acc = carry
                rs = pl.ds(pl.multiple_of(j * t, t), t)
                kj, vj = k_ref[rs, sl], v_ref[rs, sl]
                sc = lax.dot_general(qh, kj, _NT, preferred_element_type=F32) * scale + cc - cr_ref[j, hh:hh + 1, :]
                if masked:
                    sc = jnp.where(causal, sc, NEG)
                mn = jnp.maximum(m, jnp.max(sc, axis=-1, keepdims=True))
                p = jnp.exp(sc - mn)
                al = jnp.exp(m - mn)
                l = al * l + jnp.sum(p, axis=-1, keepdims=True)
                acc = al * acc + jnp.dot(p.astype(BF16), vj, preferred_element_type=F32)
                return mn, l, acc

            init = (jnp.full((t, 1), NEG, F32), jnp.zeros((t, 1), F32), jnp.zeros((t, HEAD_DIM), F32))
            carry = lax.fori_loop(0, i, lambda j, c: block(j, c, False), init)
            m, l, acc = block(i, carry, True)
            o_ref[:, sl] = acc / l
            lse_ref[:, hh:hh + 1] = m + jnp.log(l)

    return pl.pallas_call(
        body, grid=(bsz, hp, nb), in_specs=[blk, full, full, col, rows], out_specs=[blk, col],
        out_shape=[jax.ShapeDtypeStruct((n, d), F32), jax.ShapeDtypeStruct((bsz, hp, seq, 2), F32)],
        name=name, compiler_params=_cp("parallel", "parallel", "arbitrary"),
    )(q, k, v, ccol, crow)


def _flash_dq(q, k, v, ccol, crow, do, o, lse, bsz, seq, name):
    n, d = q.shape
    hp = d // LANES
    t = _tile(seq, 256)
    nb, blk, full, col, rows = _attn_specs(bsz, seq, t, hp)
    scale = 1.0 / math.sqrt(HEAD_DIM)

    def body(q_ref, k_ref, v_ref, cc_ref, cr_ref, do_ref, o_ref, lse_ref, dq_ref, dcc_ref, dl_ref):
        i = pl.program_id(2)
        causal = lax.broadcasted_iota(jnp.int32, (t, t), 0) >= lax.broadcasted_iota(jnp.int32, (t, t), 1)
        for hh in range(2):
            sl = slice(hh * HEAD_DIM, (hh + 1) * HEAD_DIM)
            qh = q_ref[:, sl]
            cc = cc_ref[:, hh:hh + 1]
            lse = lse_ref[:, hh:hh + 1]
            doh = do_ref[:, sl]
            delta = jnp.sum(doh * o_ref[:, sl], axis=-1, keepdims=True)
            dob = doh.astype(BF16)

            def block(j, carry, masked):
                dq, dc = carry
                rs = pl.ds(pl.multiple_of(j * t, t), t)
                kj, vj = k_ref[rs, sl], v_ref[rs, sl]
                sc = lax.dot_general(qh, kj, _NT, preferred_element_type=F32) * scale + cc - cr_ref[j, hh:hh + 1, :]
                if masked:
                    sc = jnp.where(causal, sc, NEG)
                p = jnp.exp(sc - lse)
                dp = lax.dot_general(dob, vj, _NT, preferred_element_type=F32)
                ds = p * (dp - delta)
                dq = dq + jnp.dot(ds.astype(BF16), kj, preferred_element_type=F32)
                dc = dc + jnp.sum(ds, axis=-1, keepdims=True)
                return dq, dc

            init = (jnp.zeros((t, HEAD_DIM), F32), jnp.zeros((t, 1), F32))
            carry = lax.fori_loop(0, i, lambda j, c: block(j, c, False), init)
            dq, dc = block(i, carry, True)
            dq_ref[:, sl] = dq * scale
            dcc_ref[:, hh:hh + 1] = dc
            dl_ref[:, hh:hh + 1] = delta

    cshape = jax.ShapeDtypeStruct((bsz, hp, seq, 2), F32)
    return pl.pallas_call(
        body, grid=(bsz, hp, nb), in_specs=[blk, full, full, col, rows, blk, blk, col],
        out_specs=[blk, col, col], out_shape=[jax.ShapeDtypeStruct((n, d), F32), cshape, cshape],
        name=name, compiler_params=_cp("parallel", "parallel", "arbitrary"),
    )(q, k, v, ccol, crow, do, o, lse)


def _flash_dkv(q, k, v, ccol, crow, do, lse_row, dl_row, bsz, seq, name):
    n, d = q.shape
    hp = d // LANES
    t = _tile(seq, 256)
    nb, blk, full, col, rows = _attn_specs(bsz, seq, t, hp)
    scale = 1.0 / math.sqrt(HEAD_DIM)

    def body(k_ref, v_ref, q_ref, do_ref, cc_ref, cr_ref, lse_ref, dl_ref, dk_ref, dv_ref, dck_ref):
        j = pl.program_id(2)
        causal = lax.broadcasted_iota(jnp.int32, (t, t), 1) >= lax.broadcasted_iota(jnp.int32, (t, t), 0)
        for hh in range(2):
            sl = slice(hh * HEAD_DIM, (hh + 1) * HEAD_DIM)
            kh, vh = k_ref[:, sl], v_ref[:, sl]
            cck = cc_ref[:, hh:hh + 1]

            def block(i, carry, masked):
                dk, dv, dc = carry
                rs = pl.ds(pl.multiple_of(i * t, t), t)
                qi = q_ref[rs, sl]
                doi = do_ref[rs, sl].astype(BF16)
                st = lax.dot_general(kh, qi, _NT, preferred_element_type=F32) * scale + cr_ref[i, hh:hh + 1, :] - cck
                if masked:
                    st = jnp.where(causal, st, NEG)
                pt = jnp.exp(st - lse_ref[i, hh:hh + 1, :])
                dv = dv + jnp.dot(pt.astype(BF16), doi, preferred_element_type=F32)
                dpt = lax.dot_general(vh, doi, _NT, preferred_element_type=F32)
                dst = pt * (dpt - dl_ref[i, hh:hh + 1, :])
                dk = dk + jnp.dot(dst.astype(BF16), qi, preferred_element_type=F32)
                dc = dc - jnp.sum(dst, axis=-1, keepdims=True)
                return dk, dv, dc

            init = (jnp.zeros((t, HEAD_DIM), F32), jnp.zeros((t, HEAD_DIM), F32), jnp.zeros((t, 1), F32))
            carry = block(j, init, True)
            dk, dv, dc = lax.fori_loop(j + 1, nb, lambda i, c: block(i, c, False), carry)
            dk_ref[:, sl] = dk * scale
            dv_ref[:, sl] = dv
            dck_ref[:, hh:hh + 1] = dc

    return pl.pallas_call(
        body, grid=(bsz, hp, nb), in_specs=[blk, blk, full, full, col, rows, rows, rows],
        out_specs=[blk, blk, col],
        out_shape=[jax.ShapeDtypeStruct((n, d), F32), jax.ShapeDtypeStruct((n, d), F32),
                   jax.ShapeDtypeStruct((bsz, hp, seq, 2), F32)],
        name=name, compiler_params=_cp("parallel", "parallel", "arbitrary"),
    )(k, v, q, do, ccol, crow, lse_row, dl_row)


def _adamw(w, g, m, v, name):
    r, c = w.shape
    tr = r
    for cand in (512, 256, 128, 64, 32, 16, 8):
        if r % cand == 0 and r > cand and cand * c * 4 <= 4 * 1024 * 1024:
            tr = cand
            break

    def body(w_ref, g_ref, m_ref, v_ref, d_ref, m2_ref, v2_ref):
        gv = g_ref[...]
        m2 = ADAM_B1 * m_ref[...] + (1.0 - ADAM_B1) * gv
        v2 = ADAM_B2 * v_ref[...] + (1.0 - ADAM_B2) * jnp.square(gv)
        m_hat = m2 / (1.0 - ADAM_B1 ** ADAM_STEP)
        v_hat = v2 / (1.0 - ADAM_B2 ** ADAM_STEP)
        d_ref[...] = -ADAM_LR * (m_hat / (jnp.sqrt(v_hat) + ADAM_EPS) + ADAM_WD * w_ref[...])
        m2_ref[...] = m2
        v2_ref[...] = v2

    blk = pl.BlockSpec((tr, c), lambda i: (i, 0))
    shp = jax.ShapeDtypeStruct((r, c), F32)
    return pl.pallas_call(
        body, grid=(r // tr,), in_specs=[blk] * 4, out_specs=[blk] * 3, out_shape=[shp] * 3,
        name=name, compiler_params=_cp("parallel"),
    )(w, g, m, v)


def _adamw_nd(w, g, m, v, name):
    shape = w.shape
    two = (math.prod(shape[:-1]), shape[-1])
    d_, m_, v_ = _adamw(w.reshape(two), g.reshape(two), m.reshape(two), v.reshape(two), name)
    return g.reshape(shape), d_.reshape(shape), m_.reshape(shape), v_.reshape(shape)


def _place():
    x, y, c = lax.axis_index("x"), lax.axis_index("y"), lax.axis_index("c")
    chips = [(1 - x, y), (x, 1 - y), (1 - x, 1 - y)]
    return x, y, c, chips


def _all_gather(shards, name):
    nt = len(shards)
    halves = [s.shape[0] // 2 for s in shards]

    def body(*refs):
        ins, outs = refs[:nt], refs[nt:2 * nt]
        send_sems, recv_sems, local_sems = refs[2 * nt:]
        x, y, c, chips = _place()
        mine = 2 * x + y
        sibling = (x, y, 1 - c)
        local = [pltpu.make_async_copy(ins[t], outs[t].at[mine], local_sems.at[t]) for t in range(nt)]
        for cp in local:
            cp.start()

        def half(t, slot, hf):
            return outs[t].at[slot, pl.ds(hf * halves[t], halves[t]), :]

        def copy(t, kk, src, dst, to):
            return pltpu.make_async_remote_copy(src_ref=src, dst_ref=dst, send_sem=send_sems.at[6 * t + kk],
                                                recv_sem=recv_sems.at[6 * t + kk], device_id=to, device_id_type=MESH)

        sends = []
        for t in range(nt):
            src = ins[t].at[pl.ds(c * halves[t], halves[t]), :]
            for jj, (cx, cy) in enumerate(chips):
                sends.append(copy(t, jj, src, half(t, mine, c), (cx, cy, c)))
        for cp in sends:
            cp.start()
        for t in range(nt):
            for jj, (cx, cy) in enumerate(chips):
                landed = half(t, 2 * cx + cy, c)
                copy(t, jj, landed, landed, (cx, cy, c)).wait_recv()
                fwd = copy(t, 3 + jj, landed, landed, sibling)
                fwd.start()
                sends.append(fwd)
        for t in range(nt):
            for jj, (cx, cy) in enumerate(chips):
                other = half(t, 2 * cx + cy, 1 - c)
                copy(t, 3 + jj, other, other, sibling).wait_recv()
        for cp in sends:
            cp.wait_send()
        for cp in local:
            cp.wait()

    return pl.pallas_call(
        body, in_specs=[ANY] * nt, out_specs=[ANY] * nt,
        out_shape=[jax.ShapeDtypeStruct((N_CHIPS,) + s.shape, s.dtype) for s in shards],
        scratch_shapes=[pltpu.SemaphoreType.DMA((6 * nt,)), pltpu.SemaphoreType.DMA((6 * nt,)),
                        pltpu.SemaphoreType.DMA((nt,))],
        name=name, compiler_params=pltpu.CompilerParams(has_side_effects=True),
    )(*shards)


def _send_sibling_halves(grads, name):
    nt = len(grads)
    halves = [g.shape[1] // 2 for g in grads]

    def body(*refs):
        ins, outs = refs[:nt], refs[nt:2 * nt]
        send_sems, recv_sems = refs[2 * nt:]
        x, y, c, _ = _place()
        cps = []
        for t in range(nt):
            src = ins[t].at[:, pl.ds((1 - c) * halves[t], halves[t]), :]
            cps.append(pltpu.make_async_remote_copy(src_ref=src, dst_ref=outs[t], send_sem=send_sems.at[t],
                                                    recv_sem=recv_sems.at[t], device_id=(x, y, 1 - c),
                                                    device_id_type=MESH))
        for cp in cps:
            cp.start()
        for cp in cps:
            cp.wait()

    return pl.pallas_call(
        body, in_specs=[ANY] * nt, out_specs=[ANY] * nt,
        out_shape=[jax.ShapeDtypeStruct((N_CHIPS, h, g.shape[2]), g.dtype) for g, h in zip(grads, halves)],
        scratch_shapes=[pltpu.SemaphoreType.DMA((nt,)), pltpu.SemaphoreType.DMA((nt,))],
        name=name, compiler_params=pltpu.CompilerParams(has_side_effects=True),
    )(*grads)


def _add_sibling(g, r1, c_idx, name):
    s, r, w = g.shape
    rh = r // 2
    tr = rh
    for cand in (512, 256, 128, 64, 32, 16, 8):
        if rh % cand == 0 and cand * w * 4 <= 4 * 1024 * 1024:
            tr = cand
            break
    per = rh // tr

    def body(c_ref, g_ref, r_ref, o_ref):
        o_ref[...] = g_ref[...] + r_ref[...]

    return pl.pallas_call(
        body,
        grid_spec=pltpu.PrefetchScalarGridSpec(
            num_scalar_prefetch=1, grid=(s, per),
            in_specs=[pl.BlockSpec((None, tr, w), lambda a, b, c_ref: (a, c_ref[0] * per + b, 0)),
                      pl.BlockSpec((None, tr, w), lambda a, b, c_ref: (a, b, 0))],
            out_specs=pl.BlockSpec((None, tr, w), lambda a, b, c_ref: (a, b, 0))),
        out_shape=jax.ShapeDtypeStruct((s, rh, w), F32), name=name, compiler_params=_cp("parallel", "parallel"),
    )(c_idx, g, r1)


def _send_chip_slots(parts, name):
    nt = len(parts)

    def body(*refs):
        ins, outs = refs[:nt], refs[nt:2 * nt]
        send_sems, recv_sems = refs[2 * nt:]
        x, y, c, chips = _place()
        cps = []
        for t in range(nt):
            for jj, (cx, cy) in enumerate(chips):
                cps.append(pltpu.make_async_remote_copy(
                    src_ref=ins[t].at[2 * cx + cy], dst_ref=outs[t].at[jj], send_sem=send_sems.at[3 * t + jj],
                    recv_sem=recv_sems.at[3 * t + jj], device_id=(cx, cy, c), device_id_type=MESH))
        for cp in cps:
            cp.start()
        for cp in cps:
            cp.wait()

    return pl.pallas_call(
        body, in_specs=[ANY] * nt, out_specs=[ANY] * nt,
        out_shape=[jax.ShapeDtypeStruct((3,) + p.shape[1:], p.dtype) for p in parts],
        scratch_shapes=[pltpu.SemaphoreType.DMA((3 * nt,)), pltpu.SemaphoreType.DMA((3 * nt,))],
        name=name, compiler_params=pltpu.CompilerParams(has_side_effects=True),
    )(*parts)


def _add_chips(part, r2, chip_idx, name):
    s, rh, w = part.shape
    tr = rh
    for cand in (512, 256, 128, 64, 32, 16, 8):
        if rh % cand == 0 and cand * w * 4 <= 4 * 1024 * 1024:
            tr = cand
            break

    def body(b_ref, p_ref, r0_ref, r1_ref, r2_ref, o_ref):
        o_ref[...] = ((p_ref[...] + r0_ref[...]) + r1_ref[...]) + r2_ref[...]

    return pl.pallas_call(
        body,
        grid_spec=pltpu.PrefetchScalarGridSpec(
            num_scalar_prefetch=1, grid=(rh // tr,),
            in_specs=[pl.BlockSpec((None, tr, w), lambda a, b_ref: (b_ref[0], a, 0))]
            + [pl.BlockSpec((None, tr, w), lambda a, b_ref, jj=jj: (jj, a, 0)) for jj in range(3)],
            out_specs=pl.BlockSpec((tr, w), lambda a, b_ref: (a, 0))),
        out_shape=jax.ShapeDtypeStruct((rh, w), F32), name=name, compiler_params=_cp("parallel"),
    )(chip_idx, part, r2, r2, r2)


def _swap_halves(halves_, name):
    nt = len(halves_)

    def body(*refs):
        ins, outs = refs[:nt], refs[nt:2 * nt]
        send_sems, recv_sems, local_sems = refs[2 * nt:]
        x, y, c, _ = _place()
        cps, loc = [], []
        for t in range(nt):
            rh = ins[t].shape[0]
            dst = outs[t].at[pl.ds(c * rh, rh), :]
            loc.append(pltpu.make_async_copy(ins[t], dst, local_sems.at[t]))
            cps.append(pltpu.make_async_remote_copy(src_ref=ins[t], dst_ref=dst, send_sem=send_sems.at[t],
                                                    recv_sem=recv_sems.at[t], device_id=(x, y, 1 - c),
                                                    device_id_type=MESH))
        for cp in loc + cps:
            cp.start()
        for t in range(nt):
            rh = ins[t].shape[0]
            got = outs[t].at[pl.ds((1 - c) * rh, rh), :]
            pltpu.make_async_remote_copy(src_ref=ins[t], dst_ref=got, send_sem=send_sems.at[t],
                                         recv_sem=recv_sems.at[t], device_id=(x, y, 1 - c),
                                         device_id_type=MESH).wait_recv()
        for cp in cps:
            cp.wait_send()
        for cp in loc:
            cp.wait()

    return pl.pallas_call(
        body, in_specs=[ANY] * nt, out_specs=[ANY] * nt,
        out_shape=[jax.ShapeDtypeStruct((2 * h.shape[0], h.shape[1]), h.dtype) for h in halves_],
        scratch_shapes=[pltpu.SemaphoreType.DMA((nt,)), pltpu.SemaphoreType.DMA((nt,)),
                        pltpu.SemaphoreType.DMA((nt,))],
        name=name, compiler_params=pltpu.CompilerParams(has_side_effects=True),
    )(*halves_)


def _reduce_scatter(grads, c_idx, chip_idx, tag):
    r1 = _send_sibling_halves(grads, name=f"rs_sibling_{tag}")
    parts = [_add_sibling(g, r, c_idx, name=f"rs_add_sibling_{tag}_{t}") for t, (g, r) in enumerate(zip(grads, r1))]
    r2 = _send_chip_slots(parts, name=f"rs_chips_{tag}")
    fin = [_add_chips(p, r, chip_idx, name=f"rs_add_chips_{tag}_{t}") for t, (p, r) in enumerate(zip(parts, r2))]
    return _swap_halves(fin, name=f"rs_swap_{tag}")


def _all_reduce_small(v, name):
    r, w = v.shape

    def body(v_ref, o_ref, buf, send_sems, recv_sems):
        x, y, c, _ = _place()
        me = 4 * x + 2 * y + c
        buf[me] = v_ref[...]
        cps = []
        for kk in range(1, 8):
            peer = (x ^ ((kk >> 2) & 1), y ^ ((kk >> 1) & 1), c ^ (kk & 1))
            cps.append(pltpu.make_async_remote_copy(src_ref=v_ref, dst_ref=buf.at[me], send_sem=send_sems.at[kk - 1],
                                                    recv_sem=recv_sems.at[kk - 1], device_id=peer, device_id_type=MESH))
        for cp in cps:
            cp.start()
        for kk in range(1, 8):
            pltpu.make_async_remote_copy(src_ref=v_ref, dst_ref=buf.at[me ^ kk], send_sem=send_sems.at[kk - 1],
                                         recv_sem=recv_sems.at[kk - 1], device_id=(x, y, c),
                                         device_id_type=MESH).wait_recv()
        for cp in cps:
            cp.wait_send()
        acc = buf[0]
        for dev in range(1, 8):
            acc = acc + buf[dev]
        o_ref[...] = acc

    vm = pl.BlockSpec(memory_space=pltpu.VMEM)
    return pl.pallas_call(
        body, in_specs=[vm], out_specs=vm, out_shape=jax.ShapeDtypeStruct((r, w), F32),
        scratch_shapes=[pltpu.VMEM((8, r, w), F32), pltpu.SemaphoreType.DMA((7,)), pltpu.SemaphoreType.DMA((7,))],
        name=name, compiler_params=pltpu.CompilerParams(has_side_effects=True),
    )(v)


def _to_shards(a, axis=-1):
    axis = axis % a.ndim
    shp = a.shape
    a = a.reshape(shp[:axis] + (N_CHIPS, shp[axis] // N_CHIPS) + shp[axis + 1:])
    return jnp.moveaxis(a, axis, 0).reshape(N_CHIPS, -1)


def _from_shards(s, shard_shape, axis=-1):
    axis = axis % len(shard_shape)
    a = jnp.moveaxis(s.reshape((N_CHIPS,) + tuple(shard_shape)), 0, axis)
    return a.reshape(tuple(shard_shape[:axis]) + (N_CHIPS * shard_shape[axis],) + tuple(shard_shape[axis + 1:]))


def _pack(vecs, rows):
    flat = jnp.concatenate([v.reshape(v.shape[0], -1) if v.ndim > 1 else v.reshape(1, -1) for v in vecs], axis=1)
    lead = flat.shape[0]
    flat = jnp.pad(flat, ((0, 0), (0, rows * LANES - flat.shape[1])))
    return flat.reshape(lead, rows, LANES)


def _pack_rows(sizes, mult):
    total = sum(sizes)
    rows = -(-total // LANES)
    return -(-rows // mult) * mult


def _unpack(flat, shapes):
    out, pos = [], 0
    for shp in shapes:
        sz = math.prod(shp)
        out.append(flat[..., pos:pos + sz].reshape(flat.shape[:-1] + tuple(shp)))
        pos += sz
    return out


def _col_layout(a, bsz, seq, hp):
    return a[:, :2 * hp].reshape(bsz, seq, hp, 2).transpose(0, 2, 1, 3)


def _row_layout(col, t):
    bsz, hp, seq, _ = col.shape
    return col.reshape(bsz, hp, seq // t, t, 2).transpose(0, 1, 2, 4, 3)


def _from_col_layout(col):
    bsz, hp, seq, _ = col.shape
    a = col.transpose(0, 2, 1, 3).reshape(bsz * seq, 2 * hp)
    return jnp.pad(a, ((0, 0), (0, LANES - 2 * hp)))


def kernel(x, norm_mix, norm_ffn, conv_w_in, conv_b_in, conv_dw, conv_dw_b, conv_ln_g, conv_ln_b, conv_w_out, conv_b_out, pool_w, pool_b, pool_scale, fox_w_in, fox_b_f, fox_q_gain, fox_k_gain, fox_w_o, ffn_w_up, ffn_dw, ffn_dw_b, ffn_w_down, loss_target, m_norm_mix, m_norm_ffn, m_conv_w_in, m_conv_b_in, m_conv_dw, m_conv_dw_b, m_conv_ln_g, m_conv_ln_b, m_conv_w_out, m_conv_b_out, m_pool_w, m_pool_b, m_pool_scale, m_fox_w_in, m_fox_b_f, m_fox_q_gain, m_fox_k_gain, m_fox_w_o, m_ffn_w_up, m_ffn_dw, m_ffn_dw_b, m_ffn_w_down, v_norm_mix, v_norm_ffn, v_conv_w_in, v_conv_b_in, v_conv_dw, v_conv_dw_b, v_conv_ln_g, v_conv_ln_b, v_conv_w_out, v_conv_b_out, v_pool_w, v_pool_b, v_pool_scale, v_fox_w_in, v_fox_b_f, v_fox_q_gain, v_fox_k_gain, v_fox_w_o, v_ffn_w_up, v_ffn_dw, v_ffn_dw_b, v_ffn_w_down):
    bsz, seq, d = x.shape
    n = bsz * seq
    depth = norm_mix.shape[0]
    n_conv, n_pool, n_fox = conv_w_in.shape[0], pool_w.shape[0], fox_w_in.shape[0]
    f2 = ffn_dw_b.shape[1]
    f = f2 // 2
    nh = d // HEAD_DIM
    hp = d // LANES
    ng = len(POOL_WINDOWS)
    gd = d // ng
    c_idx = lax.axis_index("c").astype(jnp.int32).reshape(1)
    chip_idx = (2 * lax.axis_index("x") + lax.axis_index("y")).astype(jnp.int32).reshape(1)

    small_shapes = [conv_b_in.shape, conv_dw.shape, conv_dw_b.shape, conv_ln_g.shape, conv_ln_b.shape,
                    conv_b_out.shape, pool_b.shape, ffn_dw.shape]
    small_rows = _pack_rows([math.prod(s) for s in small_shapes], 16)
    small = _pack([v.reshape(1, -1) for v in (conv_b_in, conv_dw, conv_dw_b, conv_ln_g, conv_ln_b, conv_b_out,
                                                pool_b, ffn_dw)], small_rows)[0]
    gathered = []
    for i in range(depth):
        kind, j = i % 3, i // 3
        shards = [ffn_w_up[i].astype(BF16), ffn_w_down[i].astype(BF16)]
        if kind == 0:
            shards += [conv_w_in[j].astype(BF16), conv_w_out[j].astype(BF16)]
        elif kind == 1:
            shards += [pool_w[j].reshape(ng * (gd // N_CHIPS), gd).astype(BF16)]
        else:
            shards += [fox_w_in[j].astype(BF16), fox_w_o[j].astype(BF16)]
        if i == 0:
            shards.append(small)
        gathered.append(_all_gather(shards, name=f"all_gather_l{i}"))
    small_all = gathered[0][-1].reshape(N_CHIPS, -1)
    sm = _unpack(small_all, small_shapes)
    axes = [-1] * 8
    b_in_f, dw_f, dw_b_f, ln_g_f, ln_b_f, b_out_f, pool_b_f, ffn_dw_f = [
        _from_shards(s_.reshape(N_CHIPS, -1), shp, ax) for s_, shp, ax in zip(sm, small_shapes, axes)]

    xs = x.reshape(n, d)
    tgt = loss_target.reshape(n, d)
    vec = lambda a: a.reshape(1, -1)

    saved = []
    cur = xs
    for i in range(depth):
        kind, j = i % 3, i // 3
        wts = gathered[i]
        sv = {"x_in": cur}
        if kind == 0:
            w_in, w_out = wts[2], wts[3].reshape(d, d)
            wcol = w_in.shape[2]
            h = _rms_fwd(cur, vec(norm_mix[i]), name=f"rms_mix_l{i}")
            pa = _mm(h, w_in, m=n, n=d, k=d, tk=d, tn=wcol, b_stk=wcol, b_s0=0, bias=vec(b_in_f[j, :d]),
                     name=f"conv_in_a_l{i}")
            pg = _mm(h, w_in, m=n, n=d, k=d, tk=d, tn=wcol, b_stk=wcol, b_s0=2, bias=vec(b_in_f[j, d:]),
                     name=f"conv_in_g_l{i}")
            taps = jnp.pad(dw_f[j], ((0, CONV_HALO - CONV_WIDTH), (0, 0)))
            u, s_ = _conf_fwd(pa, pg, taps, vec(dw_b_f[j]), vec(ln_g_f[j]), vec(ln_b_f[j]), seq, name=f"conf_fwd_l{i}")
            cur = _mm(s_, w_out, m=n, n=d, k=d, tk=d, bias=vec(b_out_f[j]), res=cur, name=f"conv_out_l{i}")
            sv.update(h=h, pa=pa, pg=pg, u=u, s=s_, taps=taps)
        elif kind == 1:
            pw = wts[2].reshape(N_CHIPS, ng, gd // N_CHIPS, gd).transpose(1, 0, 2, 3).reshape(ng, gd, gd)
            cur, p = _pool_fwd(cur, vec(norm_mix[i]), pw, vec(pool_b_f[j]), vec(pool_scale[j]), seq,
                               name=f"pool_fwd_l{i}")
            sv.update(p=p, pw=pw)
        else:
            w_in = wts[2].transpose(1, 0, 2).reshape(d, -1)
            w_qkv = w_in[:, :3 * d]
            w_f = jnp.pad(w_in[:, 3 * d:], ((0, 0), (0, LANES - nh)))
            w_o = wts[3].reshape(d, d)
            bf = jnp.pad(vec(fox_b_f[j]), ((0, 0), (0, LANES - nh)))
            qg, kg = jnp.tile(vec(fox_q_gain[j]), (1, nh)), jnp.tile(vec(fox_k_gain[j]), (1, nh))
            h = _rms_fwd(cur, vec(norm_mix[i]), name=f"rms_mix_l{i}")
            qkv = _mm(h, w_qkv, m=n, n=3 * d, k=d, tk=d, tn=d, name=f"fox_qkv_l{i}")
            fl = _mm(h, w_f, m=n, n=LANES, k=d, tk=d, name=f"fox_fl_l{i}")
            q, k, v, c = _fox_prep_fwd(qkv, fl, bf, qg, kg, seq, name=f"fox_prep_l{i}")
            t_att = _tile(seq, 256)
            ccol = _col_layout(c, bsz, seq, hp)
            crow = _row_layout(ccol, t_att)
            o, lse = _flash_fwd(q, k, v, ccol, crow, bsz, seq, name=f"fox_attn_l{i}")
            cur = _mm(o, w_o, m=n, n=d, k=d, tk=d, res=cur, name=f"fox_out_l{i}")
            sv.update(h=h, qkv=qkv, fl=fl, q=q, k=k, v=v, ccol=ccol, crow=crow, o=o, lse=lse, w_qkv=w_qkv, w_f=w_f,
                      w_o=w_o, bf=bf, qg=qg, kg=kg)
        w_up, w_down = wts[0], wts[1].reshape(f, d)
        ucol = w_up.shape[2]
        sv["x_mid"] = cur
        h2 = _rms_fwd(cur, vec(norm_ffn[i]), name=f"rms_ffn_l{i}")
        uv = _mm(h2, w_up, m=n, n=f, k=d, tk=d, tn=ucol, b_stk=ucol, b_s0=0, name=f"ffn_up_v_l{i}")
        ug = _mm(h2, w_up, m=n, n=f, k=d, tk=d, tn=ucol, b_stk=ucol, b_s0=2, name=f"ffn_up_g_l{i}")
        fdw, fdb = ffn_dw_f[i], ffn_dw_b[i]
        a_ = _ffn_glu_fwd(uv, ug, fdw[:, :f], fdw[:, f:], vec(fdb[:f]), vec(fdb[f:]), seq, name=f"ffn_glu_l{i}")
        cur = _mm(a_, w_down, m=n, n=d, k=f, tk=f // 2, res=cur, name=f"ffn_down_l{i}")
        sv.update(h2=h2, uv=uv, ug=ug, a=a_)
        saved.append(sv)

    dy, loss_part = _loss(cur, tgt, name="loss")
    loss = lax.psum(loss_part[0, 0], ("x", "y", "c"))

    g_norm_mix, g_norm_ffn = [None] * depth, [None] * depth
    g_ffn_dw_b = [None] * depth
    g_up, g_down = [None] * depth, [None] * depth
    g_small = {}
    g_conv_in, g_conv_out = [None] * n_conv, [None] * n_conv
    g_pool_w = g_fox_in = g_fox_o = None
    g_pool_scale = g_bf = g_qg = g_kg = None
    part_small = {"b_in": [None] * n_conv, "dw": [None] * n_conv, "dw_b": [None] * n_conv, "ln_g": [None] * n_conv,
                  "ln_b": [None] * n_conv, "b_out": [None] * n_conv, "pool_b": None, "ffn_dw": [None] * depth}

    dcur = dy
    for i in reversed(range(depth)):
        kind, j = i % 3, i // 3
        wts, sv = gathered[i], saved[i]
        w_up, w_down = wts[0], wts[1].reshape(f, d)
        ucol = w_up.shape[2]
        fdw, fdb = ffn_dw_f[i], ffn_dw_b[i]
        da = _mm(dcur, w_down, m=n, n=f, k=d, tk=d, tn=f // 2 if (f // 2) % LANES == 0 else f, tb=True,
                 name=f"ffn_down_dx_l{i}")
        dw_down = _mm(sv["a"], dcur, m=f, n=d, k=n, tm=f // 2, tn=d, tk=512, ta=True, name=f"ffn_down_dw_l{i}")
        duv, dug, dwv, dwg, dbv, dbg = _ffn_glu_bwd(sv["uv"], sv["ug"], da, fdw[:, :f], fdw[:, f:], vec(fdb[:f]),
                                                    vec(fdb[f:]), seq, name=f"ffn_glu_bwd_l{i}")
        dw_up = _mm(sv["h2"], duv, m=d, n=f, k=n, tm=d, tn=ucol, tk=512, ta=True, o_stk=ucol, o_s0=0,
                    o_slots=N_CHIPS, name=f"ffn_up_dw_v_l{i}")
        dw_up = _mm(sv["h2"], dug, m=d, n=f, k=n, tm=d, tn=ucol, tk=512, ta=True, o_stk=ucol, o_s0=2,
                    o_slots=N_CHIPS, o_buf=dw_up, name=f"ffn_up_dw_g_l{i}")
        dh2 = _mm(duv, w_up, m=n, n=d, k=f, tn=d, tk=ucol, tb=True, b_stk=ucol, b_s0=0, name=f"ffn_up_dx_v_l{i}")
        dh2 = _mm(dug, w_up, m=n, n=d, k=f, tn=d, tk=ucol, tb=True, b_stk=ucol, b_s0=2, res=dh2,
                  name=f"ffn_up_dx_g_l{i}")
        dmid, g_norm_ffn[i] = _rms_bwd(sv["x_mid"], dh2, vec(norm_ffn[i]), dcur, name=f"rms_ffn_bwd_l{i}")
        part_small["ffn_dw"][i] = jnp.concatenate([dwv, dwg], axis=1)
        g_ffn_dw_b[i] = jnp.concatenate([dbv, dbg], axis=1)
        layer_grads = [dw_up, dw_down.reshape(N_CHIPS, f // N_CHIPS, d)]

        if kind == 0:
            w_in, w_out = wts[2], wts[3].reshape(d, d)
            wcol = w_in.shape[2]
            ds = _mm(dmid, w_out, m=n, n=d, k=d, tk=d, tb=True, name=f"conv_out_dx_l{i}")
            dw_out = _mm(sv["s"], dmid, m=d, n=d, k=n, tm=d, tn=d, tk=512, ta=True, name=f"conv_out_dw_l{i}")
            dpa, dpg, ddw, ddwb, dlng, dlnb, dba, dbg_ = _conf_bwd(sv["u"], ds, sv["pa"], sv["pg"], sv["taps"],
                                                                   vec(ln_g_f[j]), vec(ln_b_f[j]), seq,
                                                                   name=f"conf_bwd_l{i}")
            dw_in = _mm(sv["h"], dpa, m=d, n=d, k=n, tm=d, tn=wcol, tk=512, ta=True, o_stk=wcol, o_s0=0,
                        o_slots=N_CHIPS, name=f"conv_in_dw_a_l{i}")
            dw_in = _mm(sv["h"], dpg, m=d, n=d, k=n, tm=d, tn=wcol, tk=512, ta=True, o_stk=wcol, o_s0=2,
                        o_slots=N_CHIPS, o_buf=dw_in, name=f"conv_in_dw_g_l{i}")
            dh = _mm(dpa, w_in, m=n, n=d, k=d, tn=d, tk=wcol, tb=True, b_stk=wcol, b_s0=0, name=f"conv_in_dx_a_l{i}")
            dh = _mm(dpg, w_in, m=n, n=d, k=d, tn=d, tk=wcol, tb=True, b_stk=wcol, b_s0=2, res=dh,
                     name=f"conv_in_dx_g_l{i}")
            dcur, g_norm_mix[i], db_out = _rms_bwd(sv["x_in"], dh, vec(norm_mix[i]), dmid, name=f"rms_mix_bwd_l{i}",
                                                   colsum=True)
            part_small["b_in"][j] = jnp.concatenate([dba, dbg_], axis=1)
            part_small["dw"][j] = ddw[:CONV_WIDTH]
            part_small["dw_b"][j], part_small["ln_g"][j], part_small["ln_b"][j] = ddwb, dlng, dlnb
            part_small["b_out"][j] = db_out
            layer_grads += [dw_in, dw_out.reshape(N_CHIPS, d // N_CHIPS, d)]
        elif kind == 1:
            dcur, g_norm_mix[i], dpw, dpb, dpsc = _pool_bwd(dmid, sv["x_in"], sv["p"], vec(norm_mix[i]), sv["pw"],
                                                            vec(pool_b_f[j]), vec(pool_scale[j]), seq,
                                                            name=f"pool_bwd_l{i}")
            part_small["pool_b"] = dpb
            g_pool_scale = dpsc
            layer_grads += [dpw.reshape(ng, N_CHIPS, gd // N_CHIPS, gd).transpose(1, 0, 2, 3).reshape(N_CHIPS, gd, gd)]
        else:
            t_att = _tile(seq, 256)
            do = _mm(dmid, sv["w_o"], m=n, n=d, k=d, tk=d, tb=True, name=f"fox_out_dx_l{i}")
            dw_o = _mm(sv["o"], dmid, m=d, n=d, k=n, tm=d, tn=d, tk=512, ta=True, name=f"fox_out_dw_l{i}")
            dq, dcc, dl = _flash_dq(sv["q"], sv["k"], sv["v"], sv["ccol"], sv["crow"], do, sv["o"], sv["lse"], bsz, seq,
                                    name=f"fox_attn_dq_l{i}")
            dk, dv, dck = _flash_dkv(sv["q"], sv["k"], sv["v"], sv["ccol"], sv["crow"], do,
                                     _row_layout(sv["lse"], t_att), _row_layout(dl, t_att), bsz, seq,
                                     name=f"fox_attn_dkv_l{i}")
            dqkv, dfl, dqg, dkg, dbf = _fox_prep_bwd(sv["qkv"], dq, dk, dv, _from_col_layout(dcc), _from_col_layout(dck),
                                                     sv["fl"], sv["bf"], sv["qg"], sv["kg"], seq,
                                                     name=f"fox_prep_bwd_l{i}")
            dw_qkv = _mm(sv["h"], dqkv, m=d, n=3 * d, k=n, tm=d, tn=d, tk=512, ta=True, name=f"fox_qkv_dw_l{i}")
            dw_f = _mm(sv["h"], dfl, m=d, n=LANES, k=n, tm=d, tk=512, ta=True, name=f"fox_fl_dw_l{i}")
            dh = _mm(dqkv, sv["w_qkv"], m=n, n=d, k=3 * d, tn=d, tk=d, tb=True, name=f"fox_qkv_dx_l{i}")
            dh = _mm(dfl, sv["w_f"], m=n, n=d, k=LANES, tn=d, tb=True, res=dh, name=f"fox_fl_dx_l{i}")
            dcur, g_norm_mix[i] = _rms_bwd(sv["x_in"], dh, vec(norm_mix[i]), dmid, name=f"rms_mix_bwd_l{i}")
            g_bf = dbf[:, :nh]
            g_qg = dqg.reshape(nh, HEAD_DIM).sum(axis=0, keepdims=True)
            g_kg = dkg.reshape(nh, HEAD_DIM).sum(axis=0, keepdims=True)
            dw_in_full = jnp.concatenate([dw_qkv, dw_f[:, :nh]], axis=1)
            wshard = dw_in_full.shape[1] // N_CHIPS
            layer_grads += [dw_in_full.reshape(d, N_CHIPS, wshard).transpose(1, 0, 2),
                            dw_o.reshape(N_CHIPS, d // N_CHIPS, d)]
        if i == 0:
            sm_parts = [jnp.concatenate(part_small["b_in"]), jnp.stack(part_small["dw"]),
                        jnp.concatenate(part_small["dw_b"]), jnp.concatenate(part_small["ln_g"]),
                        jnp.concatenate(part_small["ln_b"]), jnp.concatenate(part_small["b_out"]),
                        part_small["pool_b"].reshape(n_pool, ng, gd), jnp.stack(part_small["ffn_dw"])]
            layer_grads.append(_pack([_to_shards(p_) for p_ in sm_parts], small_rows))
        red = _reduce_scatter(layer_grads, c_idx, chip_idx, tag=f"l{i}")
        g_up[i], g_down[i] = red[0], red[1]
        if kind == 0:
            g_conv_in[j], g_conv_out[j] = red[2], red[3]
        elif kind == 1:
            g_pool_w = red[2]
        else:
            g_fox_in, g_fox_o = red[2], red[3]
        if i == 0:
            g_small_flat = red[-1].reshape(-1)

    grad_x = dcur.reshape(bsz, seq, d)

    rep_parts = [jnp.concatenate(g_norm_mix), jnp.concatenate(g_norm_ffn), g_pool_scale, g_bf, g_qg, g_kg,
                 jnp.concatenate(g_ffn_dw_b)]
    rep_shapes = [norm_mix.shape, norm_ffn.shape, pool_scale.shape, fox_b_f.shape, fox_q_gain.shape,
                  fox_k_gain.shape, ffn_dw_b.shape]
    rep_rows = _pack_rows([math.prod(s) for s in rep_shapes], 8)
    rep = _all_reduce_small(_pack([p_.reshape(1, -1) for p_ in rep_parts], rep_rows)[0], name="all_reduce_small")
    g_rep = _unpack(rep.reshape(-1), rep_shapes)
    g_sm = _unpack(g_small_flat, small_shapes)

    grads = {
        "norm_mix": g_rep[0], "norm_ffn": g_rep[1],
        "conv_w_in": jnp.stack(g_conv_in), "conv_b_in": g_sm[0], "conv_dw": g_sm[1], "conv_dw_b": g_sm[2],
        "conv_ln_g": g_sm[3], "conv_ln_b": g_sm[4], "conv_w_out": jnp.stack(g_conv_out), "conv_b_out": g_sm[5],
        "pool_w": g_pool_w.reshape(pool_w.shape), "pool_b": g_sm[6], "pool_scale": g_rep[2],
        "fox_w_in": g_fox_in.reshape(fox_w_in.shape), "fox_b_f": g_rep[3], "fox_q_gain": g_rep[4],
        "fox_k_gain": g_rep[5], "fox_w_o": g_fox_o.reshape(fox_w_o.shape),
        "ffn_w_up": jnp.stack(g_up), "ffn_dw": g_sm[7], "ffn_dw_b": g_rep[6], "ffn_w_down": jnp.stack(g_down),
    }
    weights = dict(norm_mix=norm_mix, norm_ffn=norm_ffn, conv_w_in=conv_w_in, conv_b_in=conv_b_in, conv_dw=conv_dw,
                   conv_dw_b=conv_dw_b, conv_ln_g=conv_ln_g, conv_ln_b=conv_ln_b, conv_w_out=conv_w_out,
                   conv_b_out=conv_b_out, pool_w=pool_w, pool_b=pool_b, pool_scale=pool_scale, fox_w_in=fox_w_in,
                   fox_b_f=fox_b_f, fox_q_gain=fox_q_gain, fox_k_gain=fox_k_gain, fox_w_o=fox_w_o, ffn_w_up=ffn_w_up,
                   ffn_dw=ffn_dw, ffn_dw_b=ffn_dw_b, ffn_w_down=ffn_w_down)
    m_in = dict(norm_mix=m_norm_mix, norm_ffn=m_norm_ffn, conv_w_in=m_conv_w_in, conv_b_in=m_conv_b_in,
                conv_dw=m_conv_dw, conv_dw_b=m_conv_dw_b, conv_ln_g=m_conv_ln_g, conv_ln_b=m_conv_ln_b,
                conv_w_out=m_conv_w_out, conv_b_out=m_conv_b_out, pool_w=m_pool_w, pool_b=m_pool_b,
                pool_scale=m_pool_scale, fox_w_in=m_fox_w_in, fox_b_f=m_fox_b_f, fox_q_gain=m_fox_q_gain,
                fox_k_gain=m_fox_k_gain, fox_w_o=m_fox_w_o, ffn_w_up=m_ffn_w_up, ffn_dw=m_ffn_dw,
                ffn_dw_b=m_ffn_dw_b, ffn_w_down=m_ffn_w_down)
    v_in = dict(norm_mix=v_norm_mix, norm_ffn=v_norm_ffn, conv_w_in=v_conv_w_in, conv_b_in=v_conv_b_in,
                conv_dw=v_conv_dw, conv_dw_b=v_conv_dw_b, conv_ln_g=v_conv_ln_g, conv_ln_b=v_conv_ln_b,
                conv_w_out=v_conv_w_out, conv_b_out=v_conv_b_out, pool_w=v_pool_w, pool_b=v_pool_b,
                pool_scale=v_pool_scale, fox_w_in=v_fox_w_in, fox_b_f=v_fox_b_f, fox_q_gain=v_fox_q_gain,
                fox_k_gain=v_fox_k_gain, fox_w_o=v_fox_w_o, ffn_w_up=v_ffn_w_up, ffn_dw=v_ffn_dw,
                ffn_dw_b=v_ffn_dw_b, ffn_w_down=v_ffn_w_down)
    names = list(weights)
    g_out, d_out, m_out, v_out = [], [], [], []
    for nm in names:
        g_, dl_, m_, v_ = _adamw_nd(weights[nm], grads[nm].reshape(weights[nm].shape), m_in[nm], v_in[nm],
                                    name=f"adamw_{nm}")
        g_out.append(g_)
        d_out.append(dl_)
        m_out.append(m_)
        v_out.append(v_)
    return (loss, grad_x, *g_out, *d_out, *m_out, *v_out)
```

```python
import math

import jax
import jax.numpy as jnp
import numpy as np
from jax import lax
from jax.experimental import pallas as pl
from jax.experimental.pallas import tpu as pltpu

F32 = jnp.float32
BF16 = jnp.bfloat16
HI = lax.Precision.HIGHEST
MESH = pl.DeviceIdType.MESH
ANY = pl.BlockSpec(memory_space=pl.ANY)

EPS = 1e-6
HEAD_DIM = 64
LANES = 128
POOL_WINDOWS = (2, 4, 8, 16)
CONV_WIDTH = 31
CONV_HALO = 32
FFN_HALO = 8
POOL_HALO = 16
N_CHIPS = 4
NEG = -1e30

ADAM_LR = 0.001
ADAM_B1 = 0.9
ADAM_B2 = 0.999
ADAM_EPS = 1e-08
ADAM_WD = 0.01
ADAM_STEP = 10

V7X_VMEM_LIMIT_BYTES = 56 * 1024 * 1024


def _cp(*sem):
    return pltpu.CompilerParams(dimension_semantics=sem or None, vmem_limit_bytes=V7X_VMEM_LIMIT_BYTES)


def _tile(n, pref):
    t = min(n, pref)
    assert n % t == 0, (n, pref)
    return t


def _sig(v):
    return jax.nn.sigmoid(v)


def _roll(v, shift):
    n = v.shape[0]
    shift = shift % n
    return v if shift == 0 else pltpu.roll(v, shift, 0)


def _mm(a, b, *, m, n, k, name, tm=512, tn=512, tk=512, ta=False, tb=False, b_stk=None, b_s0=0,
        o_stk=None, o_s0=0, o_slots=None, o_buf=None, bias=None, res=None, out_dtype=F32):
    tm, tn, tk = _tile(m, tm), _tile(n, tn), _tile(k, tk)
    gi, gj, gk = m // tm, n // tn, k // tk
    a_spec = pl.BlockSpec((tk, tm), lambda j, i, kk: (kk, i)) if ta else pl.BlockSpec((tm, tk), lambda j, i, kk: (i, kk))
    if b_stk is None:
        b_spec = pl.BlockSpec((tn, tk), lambda j, i, kk: (j, kk)) if tb else pl.BlockSpec((tk, tn), lambda j, i, kk: (kk, j))
    elif tb:
        assert b_stk % tk == 0
        per = b_stk // tk
        b_spec = pl.BlockSpec((None, tn, tk), lambda j, i, kk: (b_s0 + kk // per, j, kk % per))
    else:
        assert b_stk % tn == 0
        per = b_stk // tn
        b_spec = pl.BlockSpec((None, tk, tn), lambda j, i, kk: (b_s0 + j // per, kk, j % per))
    ins, in_specs = [a, b], [a_spec, b_spec]
    if bias is not None:
        ins.append(bias)
        in_specs.append(pl.BlockSpec((1, tn), lambda j, i, kk: (0, j)))
    if res is not None:
        ins.append(res)
        in_specs.append(pl.BlockSpec((tm, tn), lambda j, i, kk: (i, j)))
    aliases = {}
    if o_stk is None:
        out_shape = jax.ShapeDtypeStruct((m, n), out_dtype)
        o_spec = pl.BlockSpec((tm, tn), lambda j, i, kk: (i, j))
    else:
        assert o_stk % tn == 0
        pero = o_stk // tn
        out_shape = jax.ShapeDtypeStruct((o_slots, m, o_stk), out_dtype)
        o_spec = pl.BlockSpec((None, tm, tn), lambda j, i, kk: (o_s0 + j // pero, i, j % pero))
        if o_buf is not None:
            aliases = {len(ins): 0}
            ins.append(o_buf)
            in_specs.append(ANY)
    has_bias, has_res, has_buf = bias is not None, res is not None, o_buf is not None
    dn = (((0 if ta else 1,), (1 if tb else 0,)), ((), ()))

    def body(*refs):
        a_ref, b_ref = refs[0], refs[1]
        pos = 2
        bias_ref = refs[pos] if has_bias else None
        pos += has_bias
        res_ref = refs[pos] if has_res else None
        pos += has_res + has_buf
        o_ref = refs[pos]
        p = lax.dot_general(a_ref[...].astype(BF16), b_ref[...].astype(BF16), dn, preferred_element_type=F32)

        def finish(acc):
            if has_bias:
                acc = acc + bias_ref[...]
            if has_res:
                acc = acc + res_ref[...]
            o_ref[...] = acc.astype(o_ref.dtype)

        if gk == 1:
            finish(p)
        else:
            acc_ref = refs[pos + 1]
            kk = pl.program_id(2)

            @pl.when(kk == 0)
            def _():
                acc_ref[...] = p

            @pl.when(kk > 0)
            def _():
                acc_ref[...] += p

            @pl.when(kk == gk - 1)
            def _():
                finish(acc_ref[...])

    return pl.pallas_call(
        body, grid=(gj, gi, gk), in_specs=in_specs, out_specs=o_spec, out_shape=out_shape,
        scratch_shapes=[pltpu.VMEM((tm, tn), F32)] if gk > 1 else [],
        input_output_aliases=aliases, name=name,
        compiler_params=_cp("parallel", "parallel", "arbitrary"),
    )(*ins)


def _rms_fwd(x, g, name):
    n, d = x.shape
    tm = _tile(n, 512)

    def body(x_ref, g_ref, h_ref):
        xv = x_ref[...]
        r = lax.rsqrt(jnp.mean(xv * xv, axis=-1, keepdims=True) + EPS)
        h_ref[...] = (xv * r * g_ref[...]).astype(h_ref.dtype)

    return pl.pallas_call(
        body, grid=(n // tm,),
        in_specs=[pl.BlockSpec((tm, d), lambda i: (i, 0)), pl.BlockSpec((1, d), lambda i: (0, 0))],
        out_specs=pl.BlockSpec((tm, d), lambda i: (i, 0)),
        out_shape=jax.ShapeDtypeStruct((n, d), BF16), name=name, compiler_params=_cp("arbitrary"),
    )(x, g)


def _rms_bwd(x, dh, g, dres, name, colsum=False):
    n, d = x.shape
    tm = _tile(n, 512)

    def body(x_ref, dh_ref, g_ref, dres_ref, dx_ref, dg_ref, *rest):
        i = pl.program_id(0)
        xv, dhv = x_ref[...], dh_ref[...]
        r = lax.rsqrt(jnp.mean(xv * xv, axis=-1, keepdims=True) + EPS)
        xn = xv * r
        dxn = dhv * g_ref[...]
        dx_ref[...] = dres_ref[...] + r * (dxn - xn * jnp.mean(dxn * xn, axis=-1, keepdims=True))
        dg = jnp.sum(dhv * xn, axis=0, keepdims=True)

        @pl.when(i == 0)
        def _():
            dg_ref[...] = jnp.zeros_like(dg_ref)
            if colsum:
                rest[0][...] = jnp.zeros_like(rest[0])

        dg_ref[...] += dg
        if colsum:
            rest[0][...] += jnp.sum(dres_ref[...], axis=0, keepdims=True)

    row = pl.BlockSpec((tm, d), lambda i: (i, 0))
    vec = pl.BlockSpec((1, d), lambda i: (0, 0))
    out_shape = [jax.ShapeDtypeStruct((n, d), F32), jax.ShapeDtypeStruct((1, d), F32)]
    out_specs = [row, vec]
    if colsum:
        out_shape.append(jax.ShapeDtypeStruct((1, d), F32))
        out_specs.append(vec)
    return pl.pallas_call(
        body, grid=(n // tm,), in_specs=[row, row, vec, row], out_specs=out_specs, out_shape=out_shape,
        name=name, compiler_params=_cp("arbitrary"),
    )(x, dh, g, dres)


def _loss(y, tgt, name):
    n, d = y.shape
    tm = _tile(n, 512)

    def body(y_ref, t_ref, dy_ref, l_ref):
        i = pl.program_id(0)
        e = y_ref[...] - t_ref[...]
        dy_ref[...] = e / d
        part = 0.5 * jnp.sum(jnp.mean(e * e, axis=-1, keepdims=True), axis=0, keepdims=True)

        @pl.when(i == 0)
        def _():
            l_ref[...] = jnp.zeros_like(l_ref)

        l_ref[...] += part

    row = pl.BlockSpec((tm, d), lambda i: (i, 0))
    return pl.pallas_call(
        body, grid=(n // tm,), in_specs=[row, row],
        out_specs=[row, pl.BlockSpec((1, 1), lambda i: (0, 0))],
        out_shape=[jax.ShapeDtypeStruct((n, d), F32), jax.ShapeDtypeStruct((1, 1), F32)],
        name=name, compiler_params=_cp("arbitrary"),
    )(y, tgt)


def _ffn_specs(n, f, tm, tc, seq):
    hb = FFN_HALO
    cur = pl.BlockSpec((tm, tc), lambda j, i: (i, j))
    prev = pl.BlockSpec((hb, tc), lambda j, i: (jnp.maximum(i * (tm // hb) - 1, 0), j))
    nxt = pl.BlockSpec((hb, tc), lambda j, i: (jnp.minimum((i + 1) * (tm // hb), n // hb - 1), j))
    taps = pl.BlockSpec((3, tc), lambda j, i: (0, j))
    vec = pl.BlockSpec((1, tc), lambda j, i: (0, j))
    return cur, prev, nxt, taps, vec


def _ffn_glu_fwd(uv, ug, wv, wg, bv, bg, seq, name):
    n, f = uv.shape
    tm, tc = _tile(seq, 512), _tile(f, 256)
    tps = seq // tm
    cur, prev, _, taps, vec = _ffn_specs(n, f, tm, tc, seq)

    def body(uvp, uvc, ugp, ugc, wv_ref, wg_ref, bv_ref, bg_ref, a_ref):
        first = (pl.program_id(1) % tps) == 0

        def conv(p_ref, c_ref, w_ref, b_ref):
            xs = jnp.concatenate([jnp.where(first, 0.0, p_ref[...]), c_ref[...]], axis=0)
            w = w_ref[...]
            y = w[2:3] * xs + w[1:2] * _roll(xs, 1) + w[0:1] * _roll(xs, 2)
            return y[FFN_HALO:] + b_ref[...]

        val = conv(uvp, uvc, wv_ref, bv_ref)
        gate = conv(ugp, ugc, wg_ref, bg_ref)
        a_ref[...] = (gate * _sig(gate) * val).astype(a_ref.dtype)

    return pl.pallas_call(
        body, grid=(f // tc, n // tm), in_specs=[prev, cur, prev, cur, taps, taps, vec, vec],
        out_specs=cur, out_shape=jax.ShapeDtypeStruct((n, f), BF16), name=name,
        compiler_params=_cp("parallel", "arbitrary"),
    )(uv, uv, ug, ug, wv, wg, bv, bg)


def _ffn_glu_bwd(uv, ug, da, wv, wg, bv, bg, seq, name):
    n, f = uv.shape
    tm, tc = _tile(seq, 512), _tile(f, 256)
    tps = seq // tm
    hb = FFN_HALO
    ext = tm + hb
    cur, prev, nxt, taps, vec = _ffn_specs(n, f, tm, tc, seq)

    def body(uvp, uvc, uvn, ugp, ugc, ugn, da_c, da_n, wv_ref, wg_ref, bv_ref, bg_ref,
             duv_ref, dug_ref, dwv_ref, dwg_ref, dbv_ref, dbg_ref):
        i = pl.program_id(1)
        first = (i % tps) == 0
        last = (i % tps) == tps - 1
        da_e = jnp.concatenate([da_c[...], jnp.where(last, 0.0, da_n[...])], axis=0)

        def taps3(p_ref, c_ref, n_ref):
            xs = jnp.concatenate([jnp.where(first, 0.0, p_ref[...]), c_ref[...], n_ref[...]], axis=0)
            return xs, _roll(xs, 1), _roll(xs, 2)

        xv, xg = taps3(uvp, uvc, uvn), taps3(ugp, ugc, ugn)
        wv_, wg_ = wv_ref[...], wg_ref[...]

        def conv(xs, w, b_ref):
            return (w[2:3] * xs[0] + w[1:2] * xs[1] + w[0:1] * xs[2])[hb:] + b_ref[...]

        val, gate = conv(xv, wv_, bv_ref), conv(xg, wg_, bg_ref)
        sg = _sig(gate)
        dval = da_e * (gate * sg)
        dgate = da_e * val * (sg * (1.0 + gate * (1.0 - sg)))

        def conv_t(dv, w):
            return (w[2:3] * dv + w[1:2] * _roll(dv, ext - 1) + w[0:1] * _roll(dv, ext - 2))[:tm]

        duv_ref[...] = conv_t(dval, wv_).astype(duv_ref.dtype)
        dug_ref[...] = conv_t(dgate, wg_).astype(dug_ref.dtype)

        def tap_grads(d_own, xs):
            return jnp.concatenate(
                [jnp.sum(d_own * xs[2 - kk][hb:hb + tm], axis=0, keepdims=True) for kk in range(3)], axis=0)

        dv_own, dg_own = dval[:tm], dgate[:tm]

        @pl.when(i == 0)
        def _():
            for r in (dwv_ref, dwg_ref, dbv_ref, dbg_ref):
                r[...] = jnp.zeros_like(r)

        dwv_ref[...] += tap_grads(dv_own, xv)
        dwg_ref[...] += tap_grads(dg_own, xg)
        dbv_ref[...] += jnp.sum(dv_own, axis=0, keepdims=True)
        dbg_ref[...] += jnp.sum(dg_own, axis=0, keepdims=True)

    return pl.pallas_call(
        body, grid=(f // tc, n // tm),
        in_specs=[prev, cur, nxt, prev, cur, nxt, cur, nxt, taps, taps, vec, vec],
        out_specs=[cur, cur, taps, taps, vec, vec],
        out_shape=[jax.ShapeDtypeStruct((n, f), BF16), jax.ShapeDtypeStruct((n, f), BF16),
                   jax.ShapeDtypeStruct((3, f), F32), jax.ShapeDtypeStruct((3, f), F32),
                   jax.ShapeDtypeStruct((1, f), F32), jax.ShapeDtypeStruct((1, f), F32)],
        name=name, compiler_params=_cp("parallel", "arbitrary"),
    )(uv, uv, uv, ug, ug, ug, da, da, wv, wg, bv, bg)


def _conf_specs(n, d, tm):
    hb = CONV_HALO
    cur = pl.BlockSpec((tm, d), lambda i: (i, 0))
    prev = pl.BlockSpec((hb, d), lambda i: (jnp.maximum(i * (tm // hb) - 1, 0), 0))
    nxt = pl.BlockSpec((hb, d), lambda i: (jnp.minimum((i + 1) * (tm // hb), n // hb - 1), 0))
    taps = pl.BlockSpec((CONV_HALO, d), lambda i: (0, 0))
    vec = pl.BlockSpec((1, d), lambda i: (0, 0))
    return cur, prev, nxt, taps, vec


def _conf_fwd(pa, pg, w, wb, lng, lnb, seq, name):
    n, d = pa.shape
    tm = _tile(seq, 256)
    tps = seq // tm
    hb = CONV_HALO
    cur, prev, _, taps, vec = _conf_specs(n, d, tm)

    def body(pap, pac, pgp, pgc, w_ref, wb_ref, lng_ref, lnb_ref, u_ref, s_ref):
        first = (pl.program_id(0) % tps) == 0
        a = jnp.concatenate([jnp.where(first, 0.0, pap[...]), pac[...]], axis=0)
        g = jnp.concatenate([jnp.where(first, 0.0, pgp[...]), pgc[...]], axis=0)
        z = a * _sig(g)
        acc = w_ref[CONV_WIDTH - 1:CONV_WIDTH, :] * z
        for sh in range(1, CONV_WIDTH):
            acc = acc + w_ref[CONV_WIDTH - 1 - sh:CONV_WIDTH - sh, :] * _roll(z, sh)
        u = acc[hb:] + wb_ref[...]
        mu = jnp.mean(u, axis=-1, keepdims=True)
        uc = u - mu
        var = jnp.mean(uc * uc, axis=-1, keepdims=True)
        ul = uc * lax.rsqrt(var + EPS) * lng_ref[...] + lnb_ref[...]
        u_ref[...] = u
        s_ref[...] = (ul * _sig(ul)).astype(s_ref.dtype)

    return pl.pallas_call(
        body, grid=(n // tm,), in_specs=[prev, cur, prev, cur, taps, vec, vec, vec],
        out_specs=[cur, cur],
        out_shape=[jax.ShapeDtypeStruct((n, d), F32), jax.ShapeDtypeStruct((n, d), BF16)],
        name=name, compiler_params=_cp("arbitrary"),
    )(pa, pa, pg, pg, w, wb, lng, lnb)


def _conf_bwd(u, ds, pa, pg, w, lng, lnb, seq, name):
    n, d = u.shape
    tm = _tile(seq, 256)
    tps = seq // tm
    hb = CONV_HALO
    ext = tm + hb
    cur, prev, nxt, taps, vec = _conf_specs(n, d, tm)

    def body(uc_ref, un_ref, dsc_ref, dsn_ref, pap, pac, pgp, pgc, w_ref, lng_ref, lnb_ref,
             dpa_ref, dpg_ref, dw_ref, dwb_ref, dlng_ref, dlnb_ref, dba_ref, dbg_ref):
        i = pl.program_id(0)
        first = (i % tps) == 0
        last = (i % tps) == tps - 1

        @pl.when(i == 0)
        def _():
            for r in (dw_ref, dwb_ref, dlng_ref, dlnb_ref, dba_ref, dbg_ref):
                r[...] = jnp.zeros_like(r)

        ue = jnp.concatenate([uc_ref[...], un_ref[...]], axis=0)
        dse = jnp.concatenate([dsc_ref[...], jnp.where(last, 0.0, dsn_ref[...])], axis=0)
        mu = jnp.mean(ue, axis=-1, keepdims=True)
        cen = ue - mu
        r = lax.rsqrt(jnp.mean(cen * cen, axis=-1, keepdims=True) + EPS)
        xn = cen * r
        ul = xn * lng_ref[...] + lnb_ref[...]
        sg = _sig(ul)
        dul = dse * (sg * (1.0 + ul * (1.0 - sg)))
        dun = dul * lng_ref[...]
        du = r * (dun - jnp.mean(dun, axis=-1, keepdims=True) - xn * jnp.mean(dun * xn, axis=-1, keepdims=True))
        dlng_ref[...] += jnp.sum((dul * xn)[:tm], axis=0, keepdims=True)
        dlnb_ref[...] += jnp.sum(dul[:tm], axis=0, keepdims=True)
        du_own = du[:tm]
        dwb_ref[...] += jnp.sum(du_own, axis=0, keepdims=True)

        dz = w_ref[CONV_WIDTH - 1:CONV_WIDTH, :] * du
        for sh in range(1, CONV_WIDTH):
            dz = dz + w_ref[CONV_WIDTH - 1 - sh:CONV_WIDTH - sh, :] * _roll(du, ext - sh)
        dz = dz[:tm]

        a = jnp.concatenate([jnp.where(first, 0.0, pap[...]), pac[...]], axis=0)
        g = jnp.concatenate([jnp.where(first, 0.0, pgp[...]), pgc[...]], axis=0)
        sgg = _sig(g)
        z = a * sgg
        for sh in range(CONV_WIDTH):
            kk = CONV_WIDTH - 1 - sh
            dw_ref[kk:kk + 1, :] += jnp.sum(du_own * _roll(z, sh)[hb:], axis=0, keepdims=True)

        a_c, sg_c = a[hb:], sgg[hb:]
        da = dz * sg_c
        dg = dz * a_c * sg_c * (1.0 - sg_c)
        dpa_ref[...] = da.astype(dpa_ref.dtype)
        dpg_ref[...] = dg.astype(dpg_ref.dtype)
        dba_ref[...] += jnp.sum(da, axis=0, keepdims=True)
        dbg_ref[...] += jnp.sum(dg, axis=0, keepdims=True)

    vshape = jax.ShapeDtypeStruct((1, d), F32)
    return pl.pallas_call(
        body, grid=(n // tm,),
        in_specs=[cur, nxt, cur, nxt, prev, cur, prev, cur, taps, vec, vec],
        out_specs=[cur, cur, taps, vec, vec, vec, vec, vec],
        out_shape=[jax.ShapeDtypeStruct((n, d), BF16), jax.ShapeDtypeStruct((n, d), BF16),
                   jax.ShapeDtypeStruct((CONV_HALO, d), F32), vshape, vshape, vshape, vshape, vshape],
        name=name, compiler_params=_cp("arbitrary"),
    )(u, u, ds, ds, pa, pa, pg, pg, w, lng, lnb)


def _pool_specs(n, d, tm, gd):
    hb = POOL_HALO
    cur = pl.BlockSpec((tm, d), lambda i: (i, 0))
    prev = pl.BlockSpec((hb, d), lambda i: (jnp.maximum(i * (tm // hb) - 1, 0), 0))
    nxt = pl.BlockSpec((hb, d), lambda i: (jnp.minimum((i + 1) * (tm // hb), n // hb - 1), 0))
    wsp = pl.BlockSpec((len(POOL_WINDOWS), gd, gd), lambda i: (0, 0, 0))
    vec = pl.BlockSpec((1, d), lambda i: (0, 0))
    return cur, prev, nxt, wsp, vec


def _pool_fwd(x, g, w, b, sc, seq, name):
    n, d = x.shape
    gd = d // len(POOL_WINDOWS)
    tm = _tile(seq, 256)
    tps = seq // tm
    hb = POOL_HALO
    cur, prev, _, wsp, vec = _pool_specs(n, d, tm, gd)

    def body(xp, xc, g_ref, w_ref, b_ref, sc_ref, x1_ref, p_ref):
        i = pl.program_id(0)
        first = (i % tps) == 0
        xe = jnp.concatenate([jnp.where(first, 0.0, xp[...]), xc[...]], axis=0)
        r = lax.rsqrt(jnp.mean(xe * xe, axis=-1, keepdims=True) + EPS)
        h = xe * r * g_ref[...]
        t = ((i % tps) * tm + lax.broadcasted_iota(jnp.int32, (tm, 1), 0) + 1).astype(F32)
        ys = []
        for gi, win in enumerate(POOL_WINDOWS):
            hg = h[:, gi * gd:(gi + 1) * gd]
            s, sh = hg, 1
            while sh < win:
                s = s + _roll(s, sh)
                sh *= 2
            p = (s[hb:] / jnp.minimum(t, float(win)) - hg[hb:]).astype(BF16)
            p_ref[:, gi * gd:(gi + 1) * gd] = p
            ys.append(jnp.dot(p, w_ref[gi], preferred_element_type=F32))
        y = jnp.concatenate(ys, axis=1) + b_ref[...]
        x1_ref[...] = xc[...] + y * sc_ref[...]

    return pl.pallas_call(
        body, grid=(n // tm,), in_specs=[prev, cur, vec, wsp, vec, vec], out_specs=[cur, cur],
        out_shape=[jax.ShapeDtypeStruct((n, d), F32), jax.ShapeDtypeStruct((n, d), BF16)],
        name=name, compiler_params=_cp("arbitrary"),
    )(x, x, g, w, b, sc)


def _pool_bwd(dx1, x, p, g, w, b, sc, seq, name):
    n, d = x.shape
    ng = len(POOL_WINDOWS)
    gd = d // ng
    tm = _tile(seq, 256)
    tps = seq // tm
    hb = POOL_HALO
    ext = tm + hb
    cur, _, nxt, wsp, vec = _pool_specs(n, d, tm, gd)

    def body(dc_ref, dn_ref, x_ref, p_ref, g_ref, w_ref, b_ref, sc_ref, dx_ref, dg_ref, dw_ref, db_ref, dsc_ref):
        i = pl.program_id(0)
        last = (i % tps) == tps - 1

        @pl.when(i == 0)
        def _():
            for r_ in (dg_ref, dw_ref, db_ref, dsc_ref):
                r_[...] = jnp.zeros_like(r_)

        dxc = dc_ref[...]
        dxe = jnp.concatenate([dxc, jnp.where(last, 0.0, dn_ref[...])], axis=0)
        dyg = dxe * sc_ref[...]
        t = ((i % tps) * tm + lax.broadcasted_iota(jnp.int32, (ext, 1), 0) + 1).astype(F32)
        dhs = []
        for gi, win in enumerate(POOL_WINDOWS):
            sl = slice(gi * gd, (gi + 1) * gd)
            dyb = dyg[:, sl].astype(BF16)
            wg = w_ref[gi]
            dp = lax.dot_general(dyb, wg, (((1,), (1,)), ((), ())), preferred_element_type=F32)
            s, sh = dp / jnp.minimum(t, float(win)), 1
            while sh < win:
                s = s + _roll(s, ext - sh)
                sh *= 2
            dhs.append((s - dp)[:tm])
            pg = p_ref[:, sl]
            dw_ref[gi] += lax.dot_general(pg, dyb[:tm], (((0,), (0,)), ((), ())), preferred_element_type=F32)
            ypre = jnp.dot(pg, wg, preferred_element_type=F32) + b_ref[:, sl]
            dsc_ref[:, sl] += jnp.sum(dxc[:, sl] * ypre, axis=0, keepdims=True)
            db_ref[:, sl] += jnp.sum(dyg[:tm, sl], axis=0, keepdims=True)
        dh = jnp.concatenate(dhs, axis=1)
        xv = x_ref[...]
        r = lax.rsqrt(jnp.mean(xv * xv, axis=-1, keepdims=True) + EPS)
        xn = xv * r
        dxn = dh * g_ref[...]
        dx_ref[...] = dxc + r * (dxn - xn * jnp.mean(dxn * xn, axis=-1, keepdims=True))
        dg_ref[...] += jnp.sum(dh * xn, axis=0, keepdims=True)

    vshape = jax.ShapeDtypeStruct((1, d), F32)
    return pl.pallas_call(
        body, grid=(n // tm,), in_specs=[cur, nxt, cur, cur, vec, wsp, vec, vec],
        out_specs=[cur, vec, wsp, vec, vec],
        out_shape=[jax.ShapeDtypeStruct((n, d), F32), vshape, jax.ShapeDtypeStruct((ng, gd, gd), F32), vshape, vshape],
        name=name, compiler_params=_cp("arbitrary"),
    )(dx1, dx1, x, p, g, w, b, sc)


def _head_maps(d):
    hd = lax.broadcasted_iota(jnp.int32, (d, LANES), 0) // HEAD_DIM
    col = lax.broadcasted_iota(jnp.int32, (d, LANES), 1)
    gm = (hd == col).astype(F32)
    hd_t = lax.broadcasted_iota(jnp.int32, (LANES, d), 1) // HEAD_DIM
    row = lax.broadcasted_iota(jnp.int32, (LANES, d), 0)
    gt = (hd_t == row).astype(F32)
    return gm, gt


def _bias_placement(nh):
    pq = np.zeros((3 * LANES, nh * HEAD_DIM), np.float32)
    pk = np.zeros((3 * LANES, nh * HEAD_DIM), np.float32)
    oq = np.zeros((1, nh * HEAD_DIM), np.float32)
    ok = np.zeros((1, nh * HEAD_DIM), np.float32)
    for h in range(nh):
        for piece in range(3):
            pq[piece * LANES + h, h * HEAD_DIM + piece] = 1.0
            pk[piece * LANES + h, h * HEAD_DIM + 3 + piece] = -1.0
            oq[0, h * HEAD_DIM + 3 + piece] = 1.0
            ok[0, h * HEAD_DIM + piece] = 1.0
    return jnp.asarray(pq, BF16), jnp.asarray(pk, BF16), jnp.asarray(oq), jnp.asarray(ok)


def _fox_prep_fwd(qkv, fl, bf, qg, kg, seq, name):
    n, d3 = qkv.shape
    d = d3 // 3
    nh = d // HEAD_DIM
    tm = _tile(seq, 256)
    tps = seq // tm
    scale = 1.0 / math.sqrt(HEAD_DIM)
    pq, pk, oq, ok = _bias_placement(nh)

    def body(qkv_ref, fl_ref, bf_ref, qg_ref, kg_ref, pq_ref, pk_ref, oq_ref, ok_ref, q_ref, k_ref, v_ref, carry):
        first = (pl.program_id(0) % tps) == 0
        gm, gt = _head_maps(d)

        def head_norm(xr, gain):
            ss = jnp.dot(xr * xr, gm, precision=HI, preferred_element_type=F32)
            r = lax.rsqrt(ss / HEAD_DIM + EPS)
            return xr * jnp.dot(r, gt, precision=HI, preferred_element_type=F32) * gain

        qs = (head_norm(qkv_ref[:, :d], qg_ref[...]).astype(BF16).astype(F32) * scale).astype(BF16)
        kn = head_norm(qkv_ref[:, d:2 * d], kg_ref[...]).astype(BF16)
        v_ref[...] = qkv_ref[:, 2 * d:].astype(BF16)
        z = fl_ref[...] + bf_ref[...]
        logf = jnp.minimum(z, 0.0) - jnp.log1p(jnp.exp(-jnp.abs(z)))
        tri = (lax.broadcasted_iota(jnp.int32, (tm, tm), 0) >= lax.broadcasted_iota(jnp.int32, (tm, tm), 1)).astype(F32)

        @pl.when(first)
        def _():
            carry[...] = jnp.zeros_like(carry)

        c = jnp.dot(tri, logf, precision=HI, preferred_element_type=F32) + carry[...]
        carry[...] = c[tm - 1:tm, :]
        c1 = c.astype(BF16)
        r1 = c - c1.astype(F32)
        c2 = r1.astype(BF16)
        c3 = (r1 - c2.astype(F32)).astype(BF16)
        pieces = jnp.concatenate([c1, c2, c3], axis=1)
        eq = (jnp.dot(pieces, pq_ref[...], preferred_element_type=F32) + oq_ref[...]).astype(BF16)
        ek = (jnp.dot(pieces, pk_ref[...], preferred_element_type=F32) + ok_ref[...]).astype(BF16)
        for h in range(nh):
            lo, hi = h * HEAD_DIM, (h + 1) * HEAD_DIM
            q_ref[:, 2 * lo:2 * lo + HEAD_DIM] = qs[:, lo:hi]
            q_ref[:, 2 * lo + HEAD_DIM:2 * hi] = eq[:, lo:hi]
            k_ref[:, 2 * lo:2 * lo + HEAD_DIM] = kn[:, lo:hi]
            k_ref[:, 2 * lo + HEAD_DIM:2 * hi] = ek[:, lo:hi]

    row = lambda w: pl.BlockSpec((tm, w), lambda i: (i, 0))
    vec = lambda w: pl.BlockSpec((1, w), lambda i: (0, 0))
    full = lambda a: pl.BlockSpec(a.shape, lambda i: (0, 0))
    return pl.pallas_call(
        body, grid=(n // tm,),
        in_specs=[row(d3), row(LANES), vec(LANES), vec(d), vec(d), full(pq), full(pk), full(oq), full(ok)],
        out_specs=[row(2 * d), row(2 * d), row(d)],
        out_shape=[jax.ShapeDtypeStruct((n, 2 * d), BF16)] * 2 + [jax.ShapeDtypeStruct((n, d), BF16)],
        scratch_shapes=[pltpu.VMEM((1, LANES), F32)], name=name, compiler_params=_cp("arbitrary"),
    )(qkv, fl, bf, qg, kg, pq, pk, oq, ok)


def _fox_prep_bwd(qkv, dq, dk, dv, dc1, dc2, fl, bf, qg, kg, seq, name):
    n, d3 = qkv.shape
    d = d3 // 3
    tm = _tile(seq, 256)
    tps = seq // tm
    nt = n // tm

    def body(qkv_ref, dq_ref, dk_ref, dv_ref, dc1_ref, dc2_ref, fl_ref, bf_ref, qg_ref, kg_ref,
             dqkv_ref, dfl_ref, dqg_ref, dkg_ref, dbf_ref, carry):
        i = pl.program_id(0)
        tile = nt - 1 - i
        last = (tile % tps) == tps - 1
        gm, gt = _head_maps(d)

        @pl.when(i == 0)
        def _():
            for r_ in (dqg_ref, dkg_ref, dbf_ref):
                r_[...] = jnp.zeros_like(r_)

        @pl.when(last)
        def _():
            carry[...] = jnp.zeros_like(carry)

        def head_norm_bwd(xr, dy, gain, dgain_ref):
            ss = jnp.dot(xr * xr, gm, precision=HI, preferred_element_type=F32)
            rf = jnp.dot(lax.rsqrt(ss / HEAD_DIM + EPS), gt, precision=HI, preferred_element_type=F32)
            xn = xr * rf
            dgain_ref[...] += jnp.sum(dy * xn, axis=0, keepdims=True)
            dyg = dy * gain
            mean = jnp.dot(dyg * xn, gm, precision=HI, preferred_element_type=F32) / HEAD_DIM
            return rf * (dyg - xn * jnp.dot(mean, gt, precision=HI, preferred_element_type=F32))

        dqkv_ref[:, :d] = head_norm_bwd(qkv_ref[:, :d], dq_ref[...], qg_ref[...], dqg_ref).astype(BF16)
        dqkv_ref[:, d:2 * d] = head_norm_bwd(qkv_ref[:, d:2 * d], dk_ref[...], kg_ref[...], dkg_ref).astype(BF16)
        dqkv_ref[:, 2 * d:] = dv_ref[...].astype(BF16)

        dc = dc1_ref[...] + dc2_ref[...]
        tri = (lax.broadcasted_iota(jnp.int32, (tm, tm), 0) <= lax.broadcasted_iota(jnp.int32, (tm, tm), 1)).astype(F32)
        dlog = jnp.dot(tri, dc, precision=HI, preferred_element_type=F32) + carry[...]
        carry[...] = dlog[0:1, :]
        dfl = dlog * (1.0 - _sig(fl_ref[...] + bf_ref[...]))
        dfl_ref[...] = dfl.astype(BF16)
        dbf_ref[...] += jnp.sum(dfl, axis=0, keepdims=True)

    row = lambda w: pl.BlockSpec((tm, w), lambda i: (nt - 1 - i, 0))
    vec = lambda w: pl.BlockSpec((1, w), lambda i: (0, 0))
    return pl.pallas_call(
        body, grid=(nt,),
        in_specs=[row(d3), row(d), row(d), row(d), row(LANES), row(LANES), row(LANES), vec(LANES), vec(d), vec(d)],
        out_specs=[row(d3), row(LANES), vec(d), vec(d), vec(LANES)],
        out_shape=[jax.ShapeDtypeStruct((n, d3), BF16), jax.ShapeDtypeStruct((n, LANES), BF16),
                   jax.ShapeDtypeStruct((1, d), F32), jax.ShapeDtypeStruct((1, d), F32),
                   jax.ShapeDtypeStruct((1, LANES), F32)],
        scratch_shapes=[pltpu.VMEM((1, LANES), F32)], name=name, compiler_params=_cp("arbitrary"),
    )(qkv, dq, dk, dv, dc1, dc2, fl, bf, qg, kg)


def _attn_specs(bsz, seq, t):
    nb = seq // t
    blk = lambda w: pl.BlockSpec((t, w), lambda b, h, i: (b * nb + i, h))
    full = lambda w: pl.BlockSpec((seq, w), lambda b, h, i: (b, h))
    col = pl.BlockSpec((None, None, t, 2), lambda b, h, i: (b, h, i, 0))
    rows = pl.BlockSpec((None, None, nb, 2, t), lambda b, h, i: (b, h, 0, 0, 0))
    return nb, blk, full, col, rows


_NT = (((1,), (1,)), ((), ()))
ATTN_TILE = 512


def _head_lanes(t, hh):
    lane = lax.broadcasted_iota(jnp.int32, (t, LANES), 1)
    return (lane < HEAD_DIM) if hh == 0 else (lane >= HEAD_DIM)


def _flash_fwd(qa, ka, v, bsz, seq, name):
    n, d = v.shape
    hp = d // LANES
    t = _tile(seq, ATTN_TILE)
    nb, blk, full, col, _ = _attn_specs(bsz, seq, t)

    def body(q_ref, k_ref, v_ref, o_ref, lse_ref):
        i = pl.program_id(2)
        causal = lax.broadcasted_iota(jnp.int32, (t, t), 0) >= lax.broadcasted_iota(jnp.int32, (t, t), 1)

        def block(j, carry, masked):
            rs = pl.ds(pl.multiple_of(j * t, t), t)
            vj = v_ref[rs, :]
            out = []
            for hh in range(2):
                m, l, acc = carry[hh]
                hs = slice(hh * LANES, (hh + 1) * LANES)
                sc = lax.dot_general(q_ref[:, hs], k_ref[rs, hs], _NT, preferred_element_type=F32)
                if masked:
                    sc = jnp.where(causal, sc, NEG)
                mn = jnp.maximum(m, jnp.max(sc, axis=-1, keepdims=True))
                p = jnp.exp(sc - mn)
                al = jnp.exp(m - mn)
                l = al * l + jnp.sum(p, axis=-1, keepdims=True)
                acc = al * acc + jnp.dot(p.astype(BF16), vj, preferred_element_type=F32)
                out.append((mn, l, acc))
            return tuple(out)

        init = tuple((jnp.full((t, 1), NEG, F32), jnp.zeros((t, 1), F32), jnp.zeros((t, LANES), F32))
                     for _ in range(2))
        carry = lax.fori_loop(0, i, lambda j, c: block(j, c, False), init)
        (m0, l0, a0), (m1, l1, a1) = block(i, carry, True)
        o_ref[...] = jnp.where(_head_lanes(t, 0), a0 / l0, a1 / l1)
        lse_ref[:, 0:1] = m0 + jnp.log(l0)
        lse_ref[:, 1:2] = m1 + jnp.log(l1)

    return pl.pallas_call(
        body, grid=(bsz, hp, nb), in_specs=[blk(2 * LANES), full(2 * LANES), full(LANES)],
        out_specs=[blk(LANES), col],
        out_shape=[jax.ShapeDtypeStruct((n, d), F32), jax.ShapeDtypeStruct((bsz, hp, seq, 2), F32)],
        name=name, compiler_params=_cp("parallel", "parallel", "arbitrary"),
    )(qa, ka, v)


def _flash_bwd(qa, ka, v, do, o, lse_row, bsz, seq, name):
    n, d = v.shape
    hp = d // LANES
    t = _tile(seq, ATTN_TILE)
    nb, blk, full, col, rows = _attn_specs(bsz, seq, t)
    scale = 1.0 / math.sqrt(HEAD_DIM)
    tn_ = (((0,), (0,)), ((), ()))

    def body(k_ref, v_ref, q_ref, do_ref, o_ref, lse_ref, dq_ref, dcc_ref, dk_ref, dv_ref, dck_ref, dqa, dl):
        j = pl.program_id(2)
        causal = lax.broadcasted_iota(jnp.int32, (t, t), 1) >= lax.broadcasted_iota(jnp.int32, (t, t), 0)
        heads = [_head_lanes(t, 0), _head_lanes(t, 1)]

        @pl.when(j == 0)
        def _():
            dqa[...] = jnp.zeros_like(dqa)
            ones = jnp.ones((8, LANES), F32)
            for ib in range(nb):
                rs = slice(ib * t, (ib + 1) * t)
                prod = do_ref[rs, :] * o_ref[rs, :]
                for hh in range(2):
                    dl[ib, hh:hh + 1, :] = lax.dot_general(ones, jnp.where(heads[hh], prod, 0.0), _NT, precision=HI,
                                                           preferred_element_type=F32)[0:1]

        vj = v_ref[...]

        def block(i, carry, masked):
            rs = pl.ds(pl.multiple_of(i * t, t), t)
            doi = do_ref[rs, :]
            dks, dvp = list(carry[:2]), carry[2]
            for hh in range(2):
                hs = slice(hh * LANES, (hh + 1) * LANES)
                kh, qi = k_ref[:, hs], q_ref[rs, hs]
                dom = jnp.where(heads[hh], doi, 0.0).astype(BF16)
                st = lax.dot_general(kh, qi, _NT, preferred_element_type=F32)
                if masked:
                    st = jnp.where(causal, st, NEG)
                pt = jnp.exp(st - lse_ref[i, hh:hh + 1, :])
                dvp = dvp + jnp.dot(pt.astype(BF16), dom, preferred_element_type=F32)
                dpt = lax.dot_general(vj, dom, _NT, preferred_element_type=F32)
                dsb = (pt * (dpt - dl[i, hh:hh + 1, :])).astype(BF16)
                dks[hh] = dks[hh] + jnp.dot(dsb, qi, preferred_element_type=F32)
                dqa[rs, hs] += lax.dot_general(dsb, kh, tn_, preferred_element_type=F32)
            return dks[0], dks[1], dvp

        zero = jnp.zeros((t, LANES), F32)
        carry = block(j, (zero, zero, zero), True)
        dk0, dk1, dvp = lax.fori_loop(j + 1, nb, lambda i, c: block(i, c, False), carry)
        dk_ref[...] = jnp.where(heads[0], dk0, pltpu.roll(dk1, HEAD_DIM, 1))
        dv_ref[...] = dvp
        dck_ref[:, 0:1] = -dk0[:, HEAD_DIM + 3:HEAD_DIM + 4]
        dck_ref[:, 1:2] = -dk1[:, HEAD_DIM + 3:HEAD_DIM + 4]

        @pl.when(j == nb - 1)
        def _():
            first = lax.broadcasted_iota(jnp.int32, (seq, LANES), 1) < HEAD_DIM
            dq_ref[...] = jnp.where(first, dqa[:, :LANES], pltpu.roll(dqa[:, LANES:], HEAD_DIM, 1)) * scale
            for hh in range(2):
                lo = hh * LANES + HEAD_DIM
                dcc_ref[:, hh:hh + 1] = dqa[:, lo:lo + 1]

    whole_col = pl.BlockSpec((None, None, seq, 2), lambda b, h, i: (b, h, 0, 0))
    cshape = jax.ShapeDtypeStruct((bsz, hp, seq, 2), F32)
    nd = jax.ShapeDtypeStruct((n, d), F32)
    return pl.pallas_call(
        body, grid=(bsz, hp, nb),
        in_specs=[blk(2 * LANES), blk(LANES), full(2 * LANES), full(LANES), full(LANES), rows],
        out_specs=[full(LANES), whole_col, blk(LANES), blk(LANES), col],
        out_shape=[nd, cshape, nd, nd, cshape],
        scratch_shapes=[pltpu.VMEM((seq, 2 * LANES), F32), pltpu.VMEM((nb, 2, t), F32)],
        name=name, compiler_params=_cp("parallel", "parallel", "arbitrary"),
    )(ka, v, qa, do, o, lse_row)


def _adamw(w, g, m, v, name):
    r, c = w.shape
    tr = r
    for cand in (512, 256, 128, 64, 32, 16, 8):
        if r % cand == 0 and r > cand and cand * c * 4 <= 4 * 1024 * 1024:
            tr = cand
            break

    def body(w_ref, g_ref, m_ref, v_ref, d_ref, m2_ref, v2_ref):
        gv = g_ref[...]
        m2 = ADAM_B1 * m_ref[...] + (1.0 - ADAM_B1) * gv
        v2 = ADAM_B2 * v_ref[...] + (1.0 - ADAM_B2) * jnp.square(gv)
        m_hat = m2 / (1.0 - ADAM_B1 ** ADAM_STEP)
        v_hat = v2 / (1.0 - ADAM_B2 ** ADAM_STEP)
        d_ref[...] = -ADAM_LR * (m_hat / (jnp.sqrt(v_hat) + ADAM_EPS) + ADAM_WD * w_ref[...])
        m2_ref[...] = m2
        v2_ref[...] = v2

    blk = pl.BlockSpec((tr, c), lambda i: (i, 0))
    shp = jax.ShapeDtypeStruct((r, c), F32)
    return pl.pallas_call(
        body, grid=(r // tr,), in_specs=[blk] * 4, out_specs=[blk] * 3, out_shape=[shp] * 3,
        name=name, compiler_params=_cp("parallel"),
    )(w, g, m, v)


def _adamw_nd(w, g, m, v, name):
    shape = w.shape
    two = (math.prod(shape[:-1]), shape[-1])
    d_, m_, v_ = _adamw(w.reshape(two), g.reshape(two), m.reshape(two), v.reshape(two), name)
    return g.reshape(shape), d_.reshape(shape), m_.reshape(shape), v_.reshape(shape)


def _place():
    x, y, c = lax.axis_index("x"), lax.axis_index("y"), lax.axis_index("c")
    chips = [(1 - x, y), (x, 1 - y), (1 - x, 1 - y)]
    return x, y, c, chips


def _all_gather(shards, name):
    nt = len(shards)
    halves = [s.shape[0] // 2 for s in shards]

    def body(*refs):
        ins, outs = refs[:nt], refs[nt:2 * nt]
        send_sems, recv_sems, local_sems = refs[2 * nt:]
        x, y, c, chips = _place()
        mine = 2 * x + y
        sibling = (x, y, 1 - c)
        local = [pltpu.make_async_copy(ins[t], outs[t].at[mine], local_sems.at[t]) for t in range(nt)]
        for cp in local:
            cp.start()

        def half(t, slot, hf):
            return outs[t].at[slot, pl.ds(hf * halves[t], halves[t]), :]

        def copy(t, kk, src, dst, to):
            return pltpu.make_async_remote_copy(src_ref=src, dst_ref=dst, send_sem=send_sems.at[6 * t + kk],
                                                recv_sem=recv_sems.at[6 * t + kk], device_id=to, device_id_type=MESH)

        sends = []
        for t in range(nt):
            src = ins[t].at[pl.ds(c * halves[t], halves[t]), :]
            for jj, (cx, cy) in enumerate(chips):
                sends.append(copy(t, jj, src, half(t, mine, c), (cx, cy, c)))
        for cp in sends:
            cp.start()
        for t in range(nt):
            for jj, (cx, cy) in enumerate(chips):
                landed = half(t, 2 * cx + cy, c)
                copy(t, jj, landed, landed, (cx, cy, c)).wait_recv()
                fwd = copy(t, 3 + jj, landed, landed, sibling)
                fwd.start()
                sends.append(fwd)
        for t in range(nt):
            for jj, (cx, cy) in enumerate(chips):
                other = half(t, 2 * cx + cy, 1 - c)
                copy(t, 3 + jj, other, other, sibling).wait_recv()
        for cp in sends:
            cp.wait_send()
        for cp in local:
            cp.wait()

    return pl.pallas_call(
        body, in_specs=[ANY] * nt, out_specs=[ANY] * nt,
        out_shape=[jax.ShapeDtypeStruct((N_CHIPS,) + s.shape, s.dtype) for s in shards],
        scratch_shapes=[pltpu.SemaphoreType.DMA((6 * nt,)), pltpu.SemaphoreType.DMA((6 * nt,)),
                        pltpu.SemaphoreType.DMA((nt,))],
        name=name, compiler_params=pltpu.CompilerParams(has_side_effects=True),
    )(*shards)


def _send_sibling_halves(grads, name):
    nt = len(grads)
    halves = [g.shape[1] // 2 for g in grads]

    def body(*refs):
        ins, outs = refs[:nt], refs[nt:2 * nt]
        send_sems, recv_sems = refs[2 * nt:]
        x, y, c, _ = _place()
        cps = []
        for t in range(nt):
            src = ins[t].at[:, pl.ds((1 - c) * halves[t], halves[t]), :]
            cps.append(pltpu.make_async_remote_copy(src_ref=src, dst_ref=outs[t], send_sem=send_sems.at[t],
                                                    recv_sem=recv_sems.at[t], device_id=(x, y, 1 - c),
                                                    device_id_type=MESH))
        for cp in cps:
            cp.start()
        for cp in cps:
            cp.wait()

    return pl.pallas_call(
        body, in_specs=[ANY] * nt, out_specs=[ANY] * nt,
        out_shape=[jax.ShapeDtypeStruct((N_CHIPS, h, g.shape[2]), g.dtype) for g, h in zip(grads, halves)],
        scratch_shapes=[pltpu.SemaphoreType.DMA((nt,)), pltpu.SemaphoreType.DMA((nt,))],
        name=name, compiler_params=pltpu.CompilerParams(has_side_effects=True),
    )(*grads)


def _add_sibling(g, r1, c_idx, out_dtype, name):
    s, r, w = g.shape
    rh = r // 2
    tr = rh
    for cand in (512, 256, 128, 64, 32, 16, 8):
        if rh % cand == 0 and cand * w * 4 <= 4 * 1024 * 1024:
            tr = cand
            break
    per = rh // tr

    def body(c_ref, g_ref, r_ref, o_ref):
        o_ref[...] = (g_ref[...] + r_ref[...]).astype(o_ref.dtype)

    return pl.pallas_call(
        body,
        grid_spec=pltpu.PrefetchScalarGridSpec(
            num_scalar_prefetch=1, grid=(s, per),
            in_specs=[pl.BlockSpec((None, tr, w), lambda a, b, c_ref: (a, c_ref[0] * per + b, 0)),
                      pl.BlockSpec((None, tr, w), lambda a, b, c_ref: (a, b, 0))],
            out_specs=pl.BlockSpec((None, tr, w), lambda a, b, c_ref: (a, b, 0))),
        out_shape=jax.ShapeDtypeStruct((s, rh, w), out_dtype), name=name,
        compiler_params=_cp("parallel", "parallel"),
    )(c_idx, g, r1)


def _send_chip_slots(parts, name):
    nt = len(parts)

    def body(*refs):
        ins, outs = refs[:nt], refs[nt:2 * nt]
        send_sems, recv_sems = refs[2 * nt:]
        x, y, c, chips = _place()
        cps = []
        for t in range(nt):
            for jj, (cx, cy) in enumerate(chips):
                cps.append(pltpu.make_async_remote_copy(
                    src_ref=ins[t].at[2 * cx + cy], dst_ref=outs[t].at[jj], send_sem=send_sems.at[3 * t + jj],
                    recv_sem=recv_sems.at[3 * t + jj], device_id=(cx, cy, c), device_id_type=MESH))
        for cp in cps:
            cp.start()
        for cp in cps:
            cp.wait()

    return pl.pallas_call(
        body, in_specs=[ANY] * nt, out_specs=[ANY] * nt,
        out_shape=[jax.ShapeDtypeStruct((3,) + p.shape[1:], p.dtype) for p in parts],
        scratch_shapes=[pltpu.SemaphoreType.DMA((3 * nt,)), pltpu.SemaphoreType.DMA((3 * nt,))],
        name=name, compiler_params=pltpu.CompilerParams(has_side_effects=True),
    )(*parts)


def _add_chips(part, r2, chip_idx, name):
    s, rh, w = part.shape
    tr = rh
    for cand in (512, 256, 128, 64, 32, 16, 8):
        if rh % cand == 0 and cand * w * 4 <= 4 * 1024 * 1024:
            tr = cand
            break

    def body(b_ref, p_ref, r0_ref, r1_ref, r2_ref, o_ref):
        up = lambda ref: ref[...].astype(F32)
        o_ref[...] = ((up(p_ref) + up(r0_ref)) + up(r1_ref)) + up(r2_ref)

    return pl.pallas_call(
        body,
        grid_spec=pltpu.PrefetchScalarGridSpec(
            num_scalar_prefetch=1, grid=(rh // tr,),
            in_specs=[pl.BlockSpec((None, tr, w), lambda a, b_ref: (b_ref[0], a, 0))]
            + [pl.BlockSpec((None, tr, w), lambda a, b_ref, jj=jj: (jj, a, 0)) for jj in range(3)],
            out_specs=pl.BlockSpec((tr, w), lambda a, b_ref: (a, 0))),
        out_shape=jax.ShapeDtypeStruct((rh, w), F32), name=name, compiler_params=_cp("parallel"),
    )(chip_idx, part, r2, r2, r2)


def _swap_halves(halves_, name):
    nt = len(halves_)

    def body(*refs):
        ins, outs = refs[:nt], refs[nt:2 * nt]
        send_sems, recv_sems, local_sems = refs[2 * nt:]
        x, y, c, _ = _place()
        cps, loc = [], []
        for t in range(nt):
            rh = ins[t].shape[0]
            dst = outs[t].at[pl.ds(c * rh, rh), :]
            loc.append(pltpu.make_async_copy(ins[t], dst, local_sems.at[t]))
            cps.append(pltpu.make_async_remote_copy(src_ref=ins[t], dst_ref=dst, send_sem=send_sems.at[t],
                                                    recv_sem=recv_sems.at[t], device_id=(x, y, 1 - c),
                                                    device_id_type=MESH))
        for cp in loc + cps:
            cp.start()
        for t in range(nt):
            rh = ins[t].shape[0]
            got = outs[t].at[pl.ds((1 - c) * rh, rh), :]
            pltpu.make_async_remote_copy(src_ref=ins[t], dst_ref=got, send_sem=send_sems.at[t],
                                         recv_sem=recv_sems.at[t], device_id=(x, y, 1 - c),
                                         device_id_type=MESH).wait_recv()
        for cp in cps:
            cp.wait_send()
        for cp in loc:
            cp.wait()

    return pl.pallas_call(
        body, in_specs=[ANY] * nt, out_specs=[ANY] * nt,
        out_shape=[jax.ShapeDtypeStruct((2 * h.shape[0], h.shape[1]), h.dtype) for h in halves_],
        scratch_shapes=[pltpu.SemaphoreType.DMA((nt,)), pltpu.SemaphoreType.DMA((nt,)),
                        pltpu.SemaphoreType.DMA((nt,))],
        name=name, compiler_params=pltpu.CompilerParams(has_side_effects=True),
    )(*halves_)


def _reduce_scatter(grads, wire, c_idx, chip_idx, tag):
    r1 = _send_sibling_halves(grads, name=f"rs_sibling_{tag}")
    parts = [_add_sibling(g, r, c_idx, wire[t], name=f"rs_add_sibling_{tag}_{t}")
             for t, (g, r) in enumerate(zip(grads, r1))]
    r2 = _send_chip_slots(parts, name=f"rs_chips_{tag}")
    fin = [_add_chips(p, r, chip_idx, name=f"rs_add_chips_{tag}_{t}") for t, (p, r) in enumerate(zip(parts, r2))]
    return _swap_halves(fin, name=f"rs_swap_{tag}")


def _all_reduce_small(v, name):
    r, w = v.shape

    def body(v_ref, o_ref, buf, send_sems, recv_sems):
        x, y, c, _ = _place()
        me = 4 * x + 2 * y + c
        buf[me] = v_ref[...]
        cps = []
        for kk in range(1, 8):
            peer = (x ^ ((kk >> 2) & 1), y ^ ((kk >> 1) & 1), c ^ (kk & 1))
            cps.append(pltpu.make_async_remote_copy(src_ref=v_ref, dst_ref=buf.at[me], send_sem=send_sems.at[kk - 1],
                                                    recv_sem=recv_sems.at[kk - 1], device_id=peer, device_id_type=MESH))
        for cp in cps:
            cp.start()
        for kk in range(1, 8):
            pltpu.make_async_remote_copy(src_ref=v_ref, dst_ref=buf.at[me ^ kk], send_sem=send_sems.at[kk - 1],
                                         recv_sem=recv_sems.at[kk - 1], device_id=(x, y, c),
                                         device_id_type=MESH).wait_recv()
        for cp in cps:
            cp.wait_send()
        acc = buf[0]
        for dev in range(1, 8):
            acc = acc + buf[dev]
        o_ref[...] = acc

    vm = pl.BlockSpec(memory_space=pltpu.VMEM)
    return pl.pallas_call(
        body, in_specs=[vm], out_specs=vm, out_shape=jax.ShapeDtypeStruct((r, w), F32),
        scratch_shapes=[pltpu.VMEM((8, r, w), F32), pltpu.SemaphoreType.DMA((7,)), pltpu.SemaphoreType.DMA((7,))],
        name=name, compiler_params=pltpu.CompilerParams(has_side_effects=True),
    )(v)


def _to_shards(a, axis=-1):
    axis = axis % a.ndim
    shp = a.shape
    a = a.reshape(shp[:axis] + (N_CHIPS, shp[axis] // N_CHIPS) + shp[axis + 1:])
    return jnp.moveaxis(a, axis, 0).reshape(N_CHIPS, -1)


def _from_shards(s, shard_shape, axis=-1):
    axis = axis % len(shard_shape)
    a = jnp.moveaxis(s.reshape((N_CHIPS,) + tuple(shard_shape)), 0, axis)
    return a.reshape(tuple(shard_shape[:axis]) + (N_CHIPS * shard_shape[axis],) + tuple(shard_shape[axis + 1:]))


def _pack(vecs, rows):
    flat = jnp.concatenate([v.reshape(v.shape[0], -1) if v.ndim > 1 else v.reshape(1, -1) for v in vecs], axis=1)
    lead = flat.shape[0]
    flat = jnp.pad(flat, ((0, 0), (0, rows * LANES - flat.shape[1])))
    return flat.reshape(lead, rows, LANES)


def _pack_rows(sizes, mult):
    total = sum(sizes)
    rows = -(-total // LANES)
    return -(-rows // mult) * mult


def _unpack(flat, shapes):
    out, pos = [], 0
    for shp in shapes:
        sz = math.prod(shp)
        out.append(flat[..., pos:pos + sz].reshape(flat.shape[:-1] + tuple(shp)))
        pos += sz
    return out


def _row_layout(col, t):
    bsz, hp, seq, _ = col.shape
    return col.reshape(bsz, hp, seq // t, t, 2).transpose(0, 1, 2, 4, 3)


def _from_col_layout(col):
    bsz, hp, seq, _ = col.shape
    a = col.transpose(0, 2, 1, 3).reshape(bsz * seq, 2 * hp)
    return jnp.pad(a, ((0, 0), (0, LANES - 2 * hp)))


def kernel(x, norm_mix, norm_ffn, conv_w_in, conv_b_in, conv_dw, conv_dw_b, conv_ln_g, conv_ln_b, conv_w_out, conv_b_out, pool_w, pool_b, pool_scale, fox_w_in, fox_b_f, fox_q_gain, fox_k_gain, fox_w_o, ffn_w_up, ffn_dw, ffn_dw_b, ffn_w_down, loss_target, m_norm_mix, m_norm_ffn, m_conv_w_in, m_conv_b_in, m_conv_dw, m_conv_dw_b, m_conv_ln_g, m_conv_ln_b, m_conv_w_out, m_conv_b_out, m_pool_w, m_pool_b, m_pool_scale, m_fox_w_in, m_fox_b_f, m_fox_q_gain, m_fox_k_gain, m_fox_w_o, m_ffn_w_up, m_ffn_dw, m_ffn_dw_b, m_ffn_w_down, v_norm_mix, v_norm_ffn, v_conv_w_in, v_conv_b_in, v_conv_dw, v_conv_dw_b, v_conv_ln_g, v_conv_ln_b, v_conv_w_out, v_conv_b_out, v_pool_w, v_pool_b, v_pool_scale, v_fox_w_in, v_fox_b_f, v_fox_q_gain, v_fox_k_gain, v_fox_w_o, v_ffn_w_up, v_ffn_dw, v_ffn_dw_b, v_ffn_w_down):
    bsz, seq, d = x.shape
    n = bsz * seq
    depth = norm_mix.shape[0]
    n_conv, n_pool, n_fox = conv_w_in.shape[0], pool_w.shape[0], fox_w_in.shape[0]
    f2 = ffn_dw_b.shape[1]
    f = f2 // 2
    nh = d // HEAD_DIM
    hp = d // LANES
    ng = len(POOL_WINDOWS)
    gd = d // ng
    c_idx = lax.axis_index("c").astype(jnp.int32).reshape(1)
    chip_idx = (2 * lax.axis_index("x") + lax.axis_index("y")).astype(jnp.int32).reshape(1)

    small_shapes = [conv_b_in.shape, conv_dw.shape, conv_dw_b.shape, conv_ln_g.shape, conv_ln_b.shape,
                    conv_b_out.shape, pool_b.shape, ffn_dw.shape]
    small_rows = _pack_rows([math.prod(s) for s in small_shapes], 16)
    small = _pack([v.reshape(1, -1) for v in (conv_b_in, conv_dw, conv_dw_b, conv_ln_g, conv_ln_b, conv_b_out,
                                                pool_b, ffn_dw)], small_rows)[0]
    gathered = []
    for i in range(depth):
        kind, j = i % 3, i // 3
        shards = [ffn_w_up[i].astype(BF16), ffn_w_down[i].astype(BF16)]
        if kind == 0:
            shards += [conv_w_in[j].astype(BF16), conv_w_out[j].astype(BF16)]
        elif kind == 1:
            shards += [pool_w[j].reshape(ng * (gd // N_CHIPS), gd).astype(BF16)]
        else:
            shards += [fox_w_in[j].astype(BF16), fox_w_o[j].astype(BF16)]
        if i == 0:
            shards.append(small)
        gathered.append(_all_gather(shards, name=f"all_gather_l{i}"))
    small_all = gathered[0][-1].reshape(N_CHIPS, -1)
    sm = _unpack(small_all, small_shapes)
    axes = [-1] * 8
    b_in_f, dw_f, dw_b_f, ln_g_f, ln_b_f, b_out_f, pool_b_f, ffn_dw_f = [
        _from_shards(s_.reshape(N_CHIPS, -1), shp, ax) for s_, shp, ax in zip(sm, small_shapes, axes)]

    xs = x.reshape(n, d)
    tgt = loss_target.reshape(n, d)
    vec = lambda a: a.reshape(1, -1)

    saved = []
    cur = xs
    for i in range(depth):
        kind, j = i % 3, i // 3
        wts = gathered[i]
        sv = {"x_in": cur}
        if kind == 0:
            w_in, w_out = wts[2], wts[3].reshape(d, d)
            wcol = w_in.shape[2]
            h = _rms_fwd(cur, vec(norm_mix[i]), name=f"rms_mix_l{i}")
            pa = _mm(h, w_in, m=n, n=d, k=d, tk=d, tn=wcol, b_stk=wcol, b_s0=0, bias=vec(b_in_f[j, :d]),
                     name=f"conv_in_a_l{i}")
            pg = _mm(h, w_in, m=n, n=d, k=d, tk=d, tn=wcol, b_stk=wcol, b_s0=2, bias=vec(b_in_f[j, d:]),
                     name=f"conv_in_g_l{i}")
            taps = jnp.pad(dw_f[j], ((0, CONV_HALO - CONV_WIDTH), (0, 0)))
            u, s_ = _conf_fwd(pa, pg, taps, vec(dw_b_f[j]), vec(ln_g_f[j]), vec(ln_b_f[j]), seq, name=f"conf_fwd_l{i}")
            cur = _mm(s_, w_out, m=n, n=d, k=d, tk=d, bias=vec(b_out_f[j]), res=cur, name=f"conv_out_l{i}")
            sv.update(h=h, pa=pa, pg=pg, u=u, s=s_, taps=taps)
        elif kind == 1:
            pw = wts[2].reshape(N_CHIPS, ng, gd // N_CHIPS, gd).transpose(1, 0, 2, 3).reshape(ng, gd, gd)
            cur, p = _pool_fwd(cur, vec(norm_mix[i]), pw, vec(pool_b_f[j]), vec(pool_scale[j]), seq,
                               name=f"pool_fwd_l{i}")
            sv.update(p=p, pw=pw)
        else:
            w_in = wts[2].transpose(1, 0, 2).reshape(d, -1)
            w_qkv = w_in[:, :3 * d]
            w_f = jnp.pad(w_in[:, 3 * d:], ((0, 0), (0, LANES - nh)))
            w_o = wts[3].reshape(d, d)
            bf = jnp.pad(vec(fox_b_f[j]), ((0, 0), (0, LANES - nh)))
            qg, kg = jnp.tile(vec(fox_q_gain[j]), (1, nh)), jnp.tile(vec(fox_k_gain[j]), (1, nh))
            h = _rms_fwd(cur, vec(norm_mix[i]), name=f"rms_mix_l{i}")
            qkv = _mm(h, w_qkv, m=n, n=3 * d, k=d, tk=d, tn=d, name=f"fox_qkv_l{i}")
            fl = _mm(h, w_f, m=n, n=LANES, k=d, tk=d, name=f"fox_fl_l{i}")
            qa, ka, v = _fox_prep_fwd(qkv, fl, bf, qg, kg, seq, name=f"fox_prep_l{i}")
            o, lse = _flash_fwd(qa, ka, v, bsz, seq, name=f"fox_attn_l{i}")
            cur = _mm(o, w_o, m=n, n=d, k=d, tk=d, res=cur, name=f"fox_out_l{i}")
            sv.update(h=h, qkv=qkv, fl=fl, qa=qa, ka=ka, v=v, o=o, lse=lse, w_qkv=w_qkv, w_f=w_f, w_o=w_o, bf=bf,
                      qg=qg, kg=kg)
        w_up, w_down = wts[0], wts[1].reshape(f, d)
        ucol = w_up.shape[2]
        sv["x_mid"] = cur
        h2 = _rms_fwd(cur, vec(norm_ffn[i]), name=f"rms_ffn_l{i}")
        uv = _mm(h2, w_up, m=n, n=f, k=d, tk=d, tn=ucol, b_stk=ucol, b_s0=0, name=f"ffn_up_v_l{i}")
        ug = _mm(h2, w_up, m=n, n=f, k=d, tk=d, tn=ucol, b_stk=ucol, b_s0=2, name=f"ffn_up_g_l{i}")
        fdw, fdb = ffn_dw_f[i], ffn_dw_b[i]
        a_ = _ffn_glu_fwd(uv, ug, fdw[:, :f], fdw[:, f:], vec(fdb[:f]), vec(fdb[f:]), seq, name=f"ffn_glu_l{i}")
        cur = _mm(a_, w_down, m=n, n=d, k=f, tk=f, tn=d, res=cur, name=f"ffn_down_l{i}")
        sv.update(h2=h2, uv=uv, ug=ug, a=a_)
        saved.append(sv)

    dy, loss_part = _loss(cur, tgt, name="loss")
    loss = lax.psum(loss_part[0, 0], ("x", "y", "c"))

    g_norm_mix, g_norm_ffn = [None] * depth, [None] * depth
    g_ffn_dw_b = [None] * depth
    g_up, g_down = [None] * depth, [None] * depth
    g_small = {}
    g_conv_in, g_conv_out = [None] * n_conv, [None] * n_conv
    g_pool_w = g_fox_in = g_fox_o = None
    g_pool_scale = g_bf = g_qg = g_kg = None
    part_small = {"b_in": [None] * n_conv, "dw": [None] * n_conv, "dw_b": [None] * n_conv, "ln_g": [None] * n_conv,
                  "ln_b": [None] * n_conv, "b_out": [None] * n_conv, "pool_b": None, "ffn_dw": [None] * depth}

    dcur = dy
    for i in reversed(range(depth)):
        kind, j = i % 3, i // 3
        wts, sv = gathered[i], saved[i]
        w_up, w_down = wts[0], wts[1].reshape(f, d)
        ucol = w_up.shape[2]
        fdw, fdb = ffn_dw_f[i], ffn_dw_b[i]
        da = _mm(dcur, w_down, m=n, n=f, k=d, tk=d, tn=f // 2 if (f // 2) % LANES == 0 else f, tb=True,
                 name=f"ffn_down_dx_l{i}")
        dw_down = _mm(sv["a"], dcur, m=f, n=d, k=n, tm=f // 2, tn=d, tk=1024, ta=True, name=f"ffn_down_dw_l{i}")
        duv, dug, dwv, dwg, dbv, dbg = _ffn_glu_bwd(sv["uv"], sv["ug"], da, fdw[:, :f], fdw[:, f:], vec(fdb[:f]),
                                                    vec(fdb[f:]), seq, name=f"ffn_glu_bwd_l{i}")
        dw_up = _mm(sv["h2"], duv, m=d, n=f, k=n, tm=d, tn=ucol, tk=1024, ta=True, o_stk=ucol, o_s0=0,
                    o_slots=N_CHIPS, name=f"ffn_up_dw_v_l{i}")
        dw_up = _mm(sv["h2"], dug, m=d, n=f, k=n, tm=d, tn=ucol, tk=1024, ta=True, o_stk=ucol, o_s0=2,
                    o_slots=N_CHIPS, o_buf=dw_up, name=f"ffn_up_dw_g_l{i}")
        dh2 = _mm(duv, w_up, m=n, n=d, k=f, tn=d, tk=ucol, tb=True, b_stk=ucol, b_s0=0, name=f"ffn_up_dx_v_l{i}")
        dh2 = _mm(dug, w_up, m=n, n=d, k=f, tn=d, tk=ucol, tb=True, b_stk=ucol, b_s0=2, res=dh2,
                  name=f"ffn_up_dx_g_l{i}")
        dmid, g_norm_ffn[i] = _rms_bwd(sv["x_mid"], dh2, vec(norm_ffn[i]), dcur, name=f"rms_ffn_bwd_l{i}")
        part_small["ffn_dw"][i] = jnp.concatenate([dwv, dwg], axis=1)
        g_ffn_dw_b[i] = jnp.concatenate([dbv, dbg], axis=1)
        layer_grads = [dw_up, dw_down.reshape(N_CHIPS, f // N_CHIPS, d)]

        if kind == 0:
            w_in, w_out = wts[2], wts[3].reshape(d, d)
            wcol = w_in.shape[2]
            ds = _mm(dmid, w_out, m=n, n=d, k=d, tk=d, tb=True, name=f"conv_out_dx_l{i}")
            dw_out = _mm(sv["s"], dmid, m=d, n=d, k=n, tm=d, tn=d, tk=1024, ta=True, name=f"conv_out_dw_l{i}")
            dpa, dpg, ddw, ddwb, dlng, dlnb, dba, dbg_ = _conf_bwd(sv["u"], ds, sv["pa"], sv["pg"], sv["taps"],
                                                                   vec(ln_g_f[j]), vec(ln_b_f[j]), seq,
                                                                   name=f"conf_bwd_l{i}")
            dw_in = _mm(sv["h"], dpa, m=d, n=d, k=n, tm=d, tn=wcol, tk=1024, ta=True, o_stk=wcol, o_s0=0,
                        o_slots=N_CHIPS, name=f"conv_in_dw_a_l{i}")
            dw_in = _mm(sv["h"], dpg, m=d, n=d, k=n, tm=d, tn=wcol, tk=1024, ta=True, o_stk=wcol, o_s0=2,
                        o_slots=N_CHIPS, o_buf=dw_in, name=f"conv_in_dw_g_l{i}")
            dh = _mm(dpa, w_in, m=n, n=d, k=d, tn=d, tk=wcol, tb=True, b_stk=wcol, b_s0=0, name=f"conv_in_dx_a_l{i}")
            dh = _mm(dpg, w_in, m=n, n=d, k=d, tn=d, tk=wcol, tb=True, b_stk=wcol, b_s0=2, res=dh,
                     name=f"conv_in_dx_g_l{i}")
            dcur, g_norm_mix[i], db_out = _rms_bwd(sv["x_in"], dh, vec(norm_mix[i]), dmid, name=f"rms_mix_bwd_l{i}",
                                                   colsum=True)
            part_small["b_in"][j] = jnp.concatenate([dba, dbg_], axis=1)
            part_small["dw"][j] = ddw[:CONV_WIDTH]
            part_small["dw_b"][j], part_small["ln_g"][j], part_small["ln_b"][j] = ddwb, dlng, dlnb
            part_small["b_out"][j] = db_out
            layer_grads += [dw_in, dw_out.reshape(N_CHIPS, d // N_CHIPS, d)]
        elif kind == 1:
            dcur, g_norm_mix[i], dpw, dpb, dpsc = _pool_bwd(dmid, sv["x_in"], sv["p"], vec(norm_mix[i]), sv["pw"],
                                                            vec(pool_b_f[j]), vec(pool_scale[j]), seq,
                                                            name=f"pool_bwd_l{i}")
            part_small["pool_b"] = dpb
            g_pool_scale = dpsc
            layer_grads += [dpw.reshape(ng, N_CHIPS, gd // N_CHIPS, gd).transpose(1, 0, 2, 3).reshape(N_CHIPS, gd, gd)]
        else:
            do = _mm(dmid, sv["w_o"], m=n, n=d, k=d, tk=d, tb=True, name=f"fox_out_dx_l{i}")
            dw_o = _mm(sv["o"], dmid, m=d, n=d, k=n, tm=d, tn=d, tk=1024, ta=True, name=f"fox_out_dw_l{i}")
            dq, dcc, dk, dv, dck = _flash_bwd(sv["qa"], sv["ka"], sv["v"], do, sv["o"],
                                              _row_layout(sv["lse"], _tile(seq, ATTN_TILE)), bsz, seq,
                                              name=f"fox_attn_bwd_l{i}")
            dqkv, dfl, dqg, dkg, dbf = _fox_prep_bwd(sv["qkv"], dq, dk, dv, _from_col_layout(dcc), _from_col_layout(dck),
                                                     sv["fl"], sv["bf"], sv["qg"], sv["kg"], seq,
                                                     name=f"fox_prep_bwd_l{i}")
            dw_qkv = _mm(sv["h"], dqkv, m=d, n=3 * d, k=n, tm=d, tn=d, tk=1024, ta=True, name=f"fox_qkv_dw_l{i}")
            dw_f = _mm(sv["h"], dfl, m=d, n=LANES, k=n, tm=d, tk=1024, ta=True, name=f"fox_fl_dw_l{i}")
            dh = _mm(dqkv, sv["w_qkv"], m=n, n=d, k=3 * d, tn=d, tk=d, tb=True, name=f"fox_qkv_dx_l{i}")
            dh = _mm(dfl, sv["w_f"], m=n, n=d, k=LANES, tn=d, tb=True, res=dh, name=f"fox_fl_dx_l{i}")
            dcur, g_norm_mix[i] = _rms_bwd(sv["x_in"], dh, vec(norm_mix[i]), dmid, name=f"rms_mix_bwd_l{i}")
            g_bf = dbf[:, :nh]
            g_qg = dqg.reshape(nh, HEAD_DIM).sum(axis=0, keepdims=True)
            g_kg = dkg.reshape(nh, HEAD_DIM).sum(axis=0, keepdims=True)
            dw_in_full = jnp.concatenate([dw_qkv, dw_f[:, :nh]], axis=1)
            wshard = dw_in_full.shape[1] // N_CHIPS
            layer_grads += [dw_in_full.reshape(d, N_CHIPS, wshard).transpose(1, 0, 2),
                            dw_o.reshape(N_CHIPS, d // N_CHIPS, d)]
        if i == 0:
            sm_parts = [jnp.concatenate(part_small["b_in"]), jnp.stack(part_small["dw"]),
                        jnp.concatenate(part_small["dw_b"]), jnp.concatenate(part_small["ln_g"]),
                        jnp.concatenate(part_small["ln_b"]), jnp.concatenate(part_small["b_out"]),
                        part_small["pool_b"].reshape(n_pool, ng, gd), jnp.stack(part_small["ffn_dw"])]
            layer_grads.append(_pack([_to_shards(p_) for p_ in sm_parts], small_rows))
        wire = [BF16] * len(layer_grads)
        if i == 0:
            wire[-1] = F32
        red = _reduce_scatter(layer_grads, wire, c_idx, chip_idx, tag=f"l{i}")
        g_up[i], g_down[i] = red[0], red[1]
        if kind == 0:
            g_conv_in[j], g_conv_out[j] = red[2], red[3]
        elif kind == 1:
            g_pool_w = red[2]
        else:
            g_fox_in, g_fox_o = red[2], red[3]
        if i == 0:
            g_small_flat = red[-1].reshape(-1)

    grad_x = dcur.reshape(bsz, seq, d)

    rep_parts = [jnp.concatenate(g_norm_mix), jnp.concatenate(g_norm_ffn), g_pool_scale, g_bf, g_qg, g_kg,
                 jnp.concatenate(g_ffn_dw_b)]
    rep_shapes = [norm_mix.shape, norm_ffn.shape, pool_scale.shape, fox_b_f.shape, fox_q_gain.shape,
                  fox_k_gain.shape, ffn_dw_b.shape]
    rep_rows = _pack_rows([math.prod(s) for s in rep_shapes], 8)
    rep = _all_reduce_small(_pack([p_.reshape(1, -1) for p_ in rep_parts], rep_rows)[0], name="all_reduce_small")
    g_rep = _unpack(rep.reshape(-1), rep_shapes)
    g_sm = _unpack(g_small_flat, small_shapes)

    grads = {
        "norm_mix": g_rep[0], "norm_ffn": g_rep[1],
        "conv_w_in": jnp.stack(g_conv_in), "conv_b_in": g_sm[0], "conv_dw": g_sm[1], "conv_dw_b": g_sm[2],
        "conv_ln_g": g_sm[3], "conv_ln_b": g_sm[4], "conv_w_out": jnp.stack(g_conv_out), "conv_b_out": g_sm[5],
        "pool_w": g_pool_w.reshape(pool_w.shape), "pool_b": g_sm[6], "pool_scale": g_rep[2],
        "fox_w_in": g_fox_in.reshape(fox_w_in.shape), "fox_b_f": g_rep[3], "fox_q_gain": g_rep[4],
        "fox_k_gain": g_rep[5], "fox_w_o": g_fox_o.reshape(fox_w_o.shape),
        "ffn_w_up": jnp.stack(g_up), "ffn_dw": g_sm[7], "ffn_dw_b": g_rep[6], "ffn_w_down": jnp.stack(g_down),
    }
    weights = dict(norm_mix=norm_mix, norm_ffn=norm_ffn, conv_w_in=conv_w_in, conv_b_in=conv_b_in, conv_dw=conv_dw,
                   conv_dw_b=conv_dw_b, conv_ln_g=conv_ln_g, conv_ln_b=conv_ln_b, conv_w_out=conv_w_out,
                   conv_b_out=conv_b_out, pool_w=pool_w, pool_b=pool_b, pool_scale=pool_scale, fox_w_in=fox_w_in,
                   fox_b_f=fox_b_f, fox_q_gain=fox_q_gain, fox_k_gain=fox_k_gain, fox_w_o=fox_w_o, ffn_w_up=ffn_w_up,
                   ffn_dw=ffn_dw, ffn_dw_b=ffn_dw_b, ffn_w_down=ffn_w_down)
    m_in = dict(norm_mix=m_norm_mix, norm_ffn=m_norm_ffn, conv_w_in=m_conv_w_in, conv_b_in=m_conv_b_in,
                conv_dw=m_conv_dw, conv_dw_b=m_conv_dw_b, conv_ln_g=m_conv_ln_g, conv_ln_b=m_conv_ln_b,
                conv_w_out=m_conv_w_out, conv_b_out=m_conv_b_out, pool_w=m_pool_w, pool_b=m_pool_b,
                pool_scale=m_pool_scale, fox_w_in=m_fox_w_in, fox_b_f=m_fox_b_f, fox_q_gain=m_fox_q_gain,
                fox_k_gain=m_fox_k_gain, fox_w_o=m_fox_w_o, ffn_w_up=m_ffn_w_up, ffn_dw=m_ffn_dw,
                ffn_dw_b=m_ffn_dw_b, ffn_w_down=m_ffn_w_down)
    v_in = dict(norm_mix=v_norm_mix, norm_ffn=v_norm_ffn, conv_w_in=v_conv_w_in, conv_b_in=v_conv_b_in,
                conv_dw=v_conv_dw, conv_dw_b=v_conv_dw_b, conv_ln_g=v_conv_ln_g, conv_ln_b=v_conv_ln_b,
                conv_w_out=v_conv_w_out, conv_b_out=v_conv_b_out, pool_w=v_pool_w, pool_b=v_pool_b,
                pool_scale=v_pool_scale, fox_w_in=v_fox_w_in, fox_b_f=v_fox_b_f, fox_q_gain=v_fox_q_gain,
                fox_k_gain=v_fox_k_gain, fox_w_o=v_fox_w_o, ffn_w_up=v_ffn_w_up, ffn_dw=v_ffn_dw,
                ffn_dw_b=v_ffn_dw_b, ffn_w_down=v_ffn_w_down)
    names = list(weights)
    g_out, d_out, m_out, v_out = [], [], [], []
    for nm in names:
        g_, dl_, m_, v_ = _adamw_nd(weights[nm], grads[nm].reshape(weights[nm].shape), m_in[nm], v_in[nm],
                                    name=f"adamw_{nm}")
        g_out.append(g_)
        d_out.append(dl_)
        m_out.append(m_)
        v_out.append(v_)
    return (loss, grad_x, *g_out, *d_out, *m_out, *v_out)
```

```python
import math

import jax
import jax.numpy as jnp
import numpy as np
from jax import lax
from jax.experimental import pallas as pl
from jax.experimental.pallas import tpu as pltpu

F32 = jnp.float32
BF16 = jnp.bfloat16
HI = lax.Precision.HIGHEST
MESH = pl.DeviceIdType.MESH
ANY = pl.BlockSpec(memory_space=pl.ANY)

EPS = 1e-6
HEAD_DIM = 64
LANES = 128
POOL_WINDOWS = (2, 4, 8, 16)
CONV_WIDTH = 31
CONV_HALO = 32
FFN_HALO = 8
FFN_ROWS, FFN_COLS = 256, 1408
POOL_HALO = 16
N_CHIPS = 4
NEG = -1e30

ADAM_LR = 0.001
ADAM_B1 = 0.9
ADAM_B2 = 0.999
ADAM_EPS = 1e-08
ADAM_WD = 0.01
ADAM_STEP = 10

V7X_VMEM_LIMIT_BYTES = 56 * 1024 * 1024


def _cp(*sem):
    return pltpu.CompilerParams(dimension_semantics=sem or None, vmem_limit_bytes=V7X_VMEM_LIMIT_BYTES)


def _tile(n, pref):
    t = min(n, pref)
    assert n % t == 0, (n, pref)
    return t


def _sig(v):
    return jax.nn.sigmoid(v)


def _roll(v, shift):
    n = v.shape[0]
    shift = shift % n
    return v if shift == 0 else pltpu.roll(v, shift, 0)


class _Comm:
    def __init__(self, ins, out_shapes, n_sems, start, wait):
        self.ins, self.out_shapes, self.n_sems, self.start, self.wait = list(ins), list(out_shapes), n_sems, start, wait


def _hosted(body, comm, *, grid, in_specs, out_specs, out_shape, scratch_shapes, sem, name, ins):
    if comm is None:
        return pl.pallas_call(body, grid=grid, in_specs=in_specs, out_specs=out_specs, out_shape=out_shape,
                              scratch_shapes=scratch_shapes, name=name, compiler_params=_cp(*sem))(*ins)
    n_in, n_out, n_scr = len(in_specs), len(out_specs), len(scratch_shapes)
    nci, nco = len(comm.ins), len(comm.out_shapes)

    def wrapped(*refs):
        pos = [0]

        def take(cnt):
            pos[0] += cnt
            return refs[pos[0] - cnt:pos[0]]

        r_in, c_in, r_out, c_out, r_scr = take(n_in), take(nci), take(n_out), take(nco), take(n_scr)
        send_sems, recv_sems = take(2)
        ids = [pl.program_id(ax) for ax in range(len(grid))]
        first, last = ids[0] == 0, ids[0] == grid[0] - 1
        for ax in range(1, len(grid)):
            first = jnp.logical_and(first, ids[ax] == 0)
            last = jnp.logical_and(last, ids[ax] == grid[ax] - 1)

        @pl.when(first)
        def _():
            comm.start(c_in, c_out, send_sems, recv_sems)

        body(*r_in, *r_out, *r_scr)

        @pl.when(last)
        def _():
            comm.wait(c_in, c_out, send_sems, recv_sems)

    outs = pl.pallas_call(
        wrapped, grid=grid, in_specs=list(in_specs) + [ANY] * nci, out_specs=list(out_specs) + [ANY] * nco,
        out_shape=list(out_shape) + comm.out_shapes,
        scratch_shapes=list(scratch_shapes) + [pltpu.SemaphoreType.DMA((comm.n_sems,))] * 2, name=name,
        compiler_params=pltpu.CompilerParams(dimension_semantics=sem, vmem_limit_bytes=V7X_VMEM_LIMIT_BYTES,
                                             has_side_effects=True),
    )(*ins, *comm.ins)
    return list(outs[:n_out]), list(outs[n_out:])


def _run_comm(comm, name):
    def body(*refs):
        nci, nco = len(comm.ins), len(comm.out_shapes)
        c_in, c_out, send_sems, recv_sems = refs[:nci], refs[nci:nci + nco], refs[-2], refs[-1]
        comm.start(c_in, c_out, send_sems, recv_sems)
        comm.wait(c_in, c_out, send_sems, recv_sems)

    return pl.pallas_call(
        body, in_specs=[ANY] * len(comm.ins), out_specs=[ANY] * len(comm.out_shapes), out_shape=comm.out_shapes,
        scratch_shapes=[pltpu.SemaphoreType.DMA((comm.n_sems,))] * 2, name=name,
        compiler_params=pltpu.CompilerParams(has_side_effects=True),
    )(*comm.ins)


def _mm(a, b, *, m, n, k, name, tm=512, tn=512, tk=512, ta=False, tb=False, b_stk=None, b_s0=0,
        o_stk=None, o_s0=0, o_slots=None, o_buf=None, bias=None, res=None, out_dtype=F32):
    tm, tn, tk = _tile(m, tm), _tile(n, tn), _tile(k, tk)
    gi, gj, gk = m // tm, n // tn, k // tk
    a_spec = pl.BlockSpec((tk, tm), lambda j, i, kk: (kk, i)) if ta else pl.BlockSpec((tm, tk), lambda j, i, kk: (i, kk))
    if b_stk is None:
        b_spec = pl.BlockSpec((tn, tk), lambda j, i, kk: (j, kk)) if tb else pl.BlockSpec((tk, tn), lambda j, i, kk: (kk, j))
    elif tb:
        assert b_stk % tk == 0
        per = b_stk // tk
        b_spec = pl.BlockSpec((None, tn, tk), lambda j, i, kk: (b_s0 + kk // per, j, kk % per))
    else:
        assert b_stk % tn == 0
        per = b_stk // tn
        b_spec = pl.BlockSpec((None, tk, tn), lambda j, i, kk: (b_s0 + j // per, kk, j % per))
    ins, in_specs = [a, b], [a_spec, b_spec]
    if bias is not None:
        ins.append(bias)
        in_specs.append(pl.BlockSpec((1, tn), lambda j, i, kk: (0, j)))
    if res is not None:
        ins.append(res)
        in_specs.append(pl.BlockSpec((tm, tn), lambda j, i, kk: (i, j)))
    aliases = {}
    if o_stk is None:
        out_shape = jax.ShapeDtypeStruct((m, n), out_dtype)
        o_spec = pl.BlockSpec((tm, tn), lambda j, i, kk: (i, j))
    else:
        assert o_stk % tn == 0
        pero = o_stk // tn
        out_shape = jax.ShapeDtypeStruct((o_slots, m, o_stk), out_dtype)
        o_spec = pl.BlockSpec((None, tm, tn), lambda j, i, kk: (o_s0 + j // pero, i, j % pero))
        if o_buf is not None:
            aliases = {len(ins): 0}
            ins.append(o_buf)
            in_specs.append(ANY)
    has_bias, has_res, has_buf = bias is not None, res is not None, o_buf is not None
    dn = (((0 if ta else 1,), (1 if tb else 0,)), ((), ()))

    def body(*refs):
        a_ref, b_ref = refs[0], refs[1]
        pos = 2
        bias_ref = refs[pos] if has_bias else None
        pos += has_bias
        res_ref = refs[pos] if has_res else None
        pos += has_res + has_buf
        o_ref = refs[pos]
        p = lax.dot_general(a_ref[...].astype(BF16), b_ref[...].astype(BF16), dn, preferred_element_type=F32)

        def finish(acc):
            if has_bias:
                acc = acc + bias_ref[...]
            if has_res:
                acc = acc + res_ref[...]
            o_ref[...] = acc.astype(o_ref.dtype)

        if gk == 1:
            finish(p)
        else:
            acc_ref = refs[pos + 1]
            kk = pl.program_id(2)

            @pl.when(kk == 0)
            def _():
                acc_ref[...] = p

            @pl.when(kk > 0)
            def _():
                acc_ref[...] += p

            @pl.when(kk == gk - 1)
            def _():
                finish(acc_ref[...])

    return pl.pallas_call(
        body, grid=(gj, gi, gk), in_specs=in_specs, out_specs=o_spec, out_shape=out_shape,
        scratch_shapes=[pltpu.VMEM((tm, tn), F32)] if gk > 1 else [],
        input_output_aliases=aliases, name=name,
        compiler_params=_cp("parallel", "parallel", "arbitrary"),
    )(*ins)


def _rms_fwd(x, g, name):
    n, d = x.shape
    tm = _tile(n, 512)

    def body(x_ref, g_ref, h_ref):
        xv = x_ref[...]
        r = lax.rsqrt(jnp.mean(xv * xv, axis=-1, keepdims=True) + EPS)
        h_ref[...] = (xv * r * g_ref[...]).astype(h_ref.dtype)

    return pl.pallas_call(
        body, grid=(n // tm,),
        in_specs=[pl.BlockSpec((tm, d), lambda i: (i, 0)), pl.BlockSpec((1, d), lambda i: (0, 0))],
        out_specs=pl.BlockSpec((tm, d), lambda i: (i, 0)),
        out_shape=jax.ShapeDtypeStruct((n, d), BF16), name=name, compiler_params=_cp("arbitrary"),
    )(x, g)


def _rms_bwd(x, dh, g, dres, name, colsum=False):
    n, d = x.shape
    tm = _tile(n, 512)

    def body(x_ref, dh_ref, g_ref, dres_ref, dx_ref, dg_ref, *rest):
        i = pl.program_id(0)
        xv, dhv = x_ref[...], dh_ref[...]
        r = lax.rsqrt(jnp.mean(xv * xv, axis=-1, keepdims=True) + EPS)
        xn = xv * r
        dxn = dhv * g_ref[...]
        dx_ref[...] = dres_ref[...] + r * (dxn - xn * jnp.mean(dxn * xn, axis=-1, keepdims=True))
        dg = jnp.sum(dhv * xn, axis=0, keepdims=True)

        @pl.when(i == 0)
        def _():
            dg_ref[...] = jnp.zeros_like(dg_ref)
            if colsum:
                rest[0][...] = jnp.zeros_like(rest[0])

        dg_ref[...] += dg
        if colsum:
            rest[0][...] += jnp.sum(dres_ref[...], axis=0, keepdims=True)

    row = pl.BlockSpec((tm, d), lambda i: (i, 0))
    vec = pl.BlockSpec((1, d), lambda i: (0, 0))
    out_shape = [jax.ShapeDtypeStruct((n, d), F32), jax.ShapeDtypeStruct((1, d), F32)]
    out_specs = [row, vec]
    if colsum:
        out_shape.append(jax.ShapeDtypeStruct((1, d), F32))
        out_specs.append(vec)
    return pl.pallas_call(
        body, grid=(n // tm,), in_specs=[row, row, vec, row], out_specs=out_specs, out_shape=out_shape,
        name=name, compiler_params=_cp("arbitrary"),
    )(x, dh, g, dres)


def _loss(y, tgt, name):
    n, d = y.shape
    tm = _tile(n, 512)

    def body(y_ref, t_ref, dy_ref, l_ref):
        i = pl.program_id(0)
        e = y_ref[...] - t_ref[...]
        dy_ref[...] = e / d
        part = 0.5 * jnp.sum(jnp.mean(e * e, axis=-1, keepdims=True), axis=0, keepdims=True)

        @pl.when(i == 0)
        def _():
            l_ref[...] = jnp.zeros_like(l_ref)

        l_ref[...] += part

    row = pl.BlockSpec((tm, d), lambda i: (i, 0))
    return pl.pallas_call(
        body, grid=(n // tm,), in_specs=[row, row],
        out_specs=[row, pl.BlockSpec((1, 1), lambda i: (0, 0))],
        out_shape=[jax.ShapeDtypeStruct((n, d), F32), jax.ShapeDtypeStruct((1, 1), F32)],
        name=name, compiler_params=_cp("arbitrary"),
    )(y, tgt)


def _ffn_specs(n, f, tm, tc, seq):
    hb = FFN_HALO
    cur = pl.BlockSpec((tm, tc), lambda j, i: (i, j))
    prev = pl.BlockSpec((hb, tc), lambda j, i: (jnp.maximum(i * (tm // hb) - 1, 0), j))
    nxt = pl.BlockSpec((hb, tc), lambda j, i: (jnp.minimum((i + 1) * (tm // hb), n // hb - 1), j))
    taps = pl.BlockSpec((3, tc), lambda j, i: (0, j))
    vec = pl.BlockSpec((1, tc), lambda j, i: (0, j))
    return cur, prev, nxt, taps, vec


def _ffn_glu_fwd(uv, ug, wv, wg, bv, bg, seq, name, comm=None):
    n, f = uv.shape
    tm, tc = _tile(seq, FFN_ROWS), _tile(f, FFN_COLS)
    tps = seq // tm
    cur, prev, _, taps, vec = _ffn_specs(n, f, tm, tc, seq)

    def body(uvp, uvc, ugp, ugc, wv_ref, wg_ref, bv_ref, bg_ref, a_ref):
        first = (pl.program_id(1) % tps) == 0

        def conv(p_ref, c_ref, w_ref, b_ref):
            xs = jnp.concatenate([jnp.where(first, 0.0, p_ref[...]), c_ref[...]], axis=0)
            w = w_ref[...]
            y = w[2:3] * xs + w[1:2] * _roll(xs, 1) + w[0:1] * _roll(xs, 2)
            return y[FFN_HALO:] + b_ref[...]

        val = conv(uvp, uvc, wv_ref, bv_ref)
        gate = conv(ugp, ugc, wg_ref, bg_ref)
        a_ref[...] = (gate * _sig(gate) * val).astype(a_ref.dtype)

    return _hosted(
        body, comm, grid=(f // tc, n // tm), in_specs=[prev, cur, prev, cur, taps, taps, vec, vec],
        out_specs=[cur], out_shape=[jax.ShapeDtypeStruct((n, f), BF16)], scratch_shapes=[],
        sem=("parallel", "arbitrary"), name=name, ins=(uv, uv, ug, ug, wv, wg, bv, bg))


def _ffn_glu_bwd(uv, ug, da, wv, wg, bv, bg, seq, name, comm=None):
    n, f = uv.shape
    tm, tc = _tile(seq, FFN_ROWS), _tile(f, FFN_COLS)
    tps = seq // tm
    hb = FFN_HALO
    ext = tm + hb
    cur, prev, nxt, taps, vec = _ffn_specs(n, f, tm, tc, seq)

    def body(uvp, uvc, uvn, ugp, ugc, ugn, da_c, da_n, wv_ref, wg_ref, bv_ref, bg_ref,
             duv_ref, dug_ref, dwv_ref, dwg_ref, dbv_ref, dbg_ref):
        i = pl.program_id(1)
        first = (i % tps) == 0
        last = (i % tps) == tps - 1
        da_e = jnp.concatenate([da_c[...], jnp.where(last, 0.0, da_n[...])], axis=0)

        def taps3(p_ref, c_ref, n_ref):
            xs = jnp.concatenate([jnp.where(first, 0.0, p_ref[...]), c_ref[...], n_ref[...]], axis=0)
            return xs, _roll(xs, 1), _roll(xs, 2)

        xv, xg = taps3(uvp, uvc, uvn), taps3(ugp, ugc, ugn)
        wv_, wg_ = wv_ref[...], wg_ref[...]

        def conv(xs, w, b_ref):
            return (w[2:3] * xs[0] + w[1:2] * xs[1] + w[0:1] * xs[2])[hb:] + b_ref[...]

        val, gate = conv(xv, wv_, bv_ref), conv(xg, wg_, bg_ref)
        sg = _sig(gate)
        dval = da_e * (gate * sg)
        dgate = da_e * val * (sg * (1.0 + gate * (1.0 - sg)))

        def conv_t(dv, w):
            return (w[2:3] * dv + w[1:2] * _roll(dv, ext - 1) + w[0:1] * _roll(dv, ext - 2))[:tm]

        duv_ref[...] = conv_t(dval, wv_).astype(duv_ref.dtype)
        dug_ref[...] = conv_t(dgate, wg_).astype(dug_ref.dtype)

        def tap_grads(d_own, xs):
            return jnp.concatenate(
                [jnp.sum(d_own * xs[2 - kk][hb:hb + tm], axis=0, keepdims=True) for kk in range(3)], axis=0)

        dv_own, dg_own = dval[:tm], dgate[:tm]

        @pl.when(i == 0)
        def _():
            for r in (dwv_ref, dwg_ref, dbv_ref, dbg_ref):
                r[...] = jnp.zeros_like(r)

        dwv_ref[...] += tap_grads(dv_own, xv)
        dwg_ref[...] += tap_grads(dg_own, xg)
        dbv_ref[...] += jnp.sum(dv_own, axis=0, keepdims=True)
        dbg_ref[...] += jnp.sum(dg_own, axis=0, keepdims=True)

    return _hosted(
        body, comm, grid=(f // tc, n // tm),
        in_specs=[prev, cur, nxt, prev, cur, nxt, cur, nxt, taps, taps, vec, vec],
        out_specs=[cur, cur, taps, taps, vec, vec],
        out_shape=[jax.ShapeDtypeStruct((n, f), BF16), jax.ShapeDtypeStruct((n, f), BF16),
                   jax.ShapeDtypeStruct((3, f), F32), jax.ShapeDtypeStruct((3, f), F32),
                   jax.ShapeDtypeStruct((1, f), F32), jax.ShapeDtypeStruct((1, f), F32)],
        scratch_shapes=[], sem=("parallel", "arbitrary"), name=name,
        ins=(uv, uv, uv, ug, ug, ug, da, da, wv, wg, bv, bg))


def _conf_specs(n, d, tm):
    hb = CONV_HALO
    cur = pl.BlockSpec((tm, d), lambda i: (i, 0))
    prev = pl.BlockSpec((hb, d), lambda i: (jnp.maximum(i * (tm // hb) - 1, 0), 0))
    nxt = pl.BlockSpec((hb, d), lambda i: (jnp.minimum((i + 1) * (tm // hb), n // hb - 1), 0))
    taps = pl.BlockSpec((CONV_HALO, d), lambda i: (0, 0))
    vec = pl.BlockSpec((1, d), lambda i: (0, 0))
    return cur, prev, nxt, taps, vec


def _conf_fwd(pa, pg, w, wb, lng, lnb, seq, name):
    n, d = pa.shape
    tm = _tile(seq, 256)
    tps = seq // tm
    hb = CONV_HALO
    cur, prev, _, taps, vec = _conf_specs(n, d, tm)

    def body(pap, pac, pgp, pgc, w_ref, wb_ref, lng_ref, lnb_ref, u_ref, s_ref):
        first = (pl.program_id(0) % tps) == 0
        a = jnp.concatenate([jnp.where(first, 0.0, pap[...]), pac[...]], axis=0)
        g = jnp.concatenate([jnp.where(first, 0.0, pgp[...]), pgc[...]], axis=0)
        z = a * _sig(g)
        acc = w_ref[CONV_WIDTH - 1:CONV_WIDTH, :] * z
        for sh in range(1, CONV_WIDTH):
            acc = acc + w_ref[CONV_WIDTH - 1 - sh:CONV_WIDTH - sh, :] * _roll(z, sh)
        u = acc[hb:] + wb_ref[...]
        mu = jnp.mean(u, axis=-1, keepdims=True)
        uc = u - mu
        var = jnp.mean(uc * uc, axis=-1, keepdims=True)
        ul = uc * lax.rsqrt(var + EPS) * lng_ref[...] + lnb_ref[...]
        u_ref[...] = u
        s_ref[...] = (ul * _sig(ul)).astype(s_ref.dtype)

    return pl.pallas_call(
        body, grid=(n // tm,), in_specs=[prev, cur, prev, cur, taps, vec, vec, vec],
        out_specs=[cur, cur],
        out_shape=[jax.ShapeDtypeStruct((n, d), F32), jax.ShapeDtypeStruct((n, d), BF16)],
        name=name, compiler_params=_cp("arbitrary"),
    )(pa, pa, pg, pg, w, wb, lng, lnb)


def _conf_bwd(u, ds, pa, pg, w, lng, lnb, seq, name, comm=None):
    n, d = u.shape
    tm = _tile(seq, 256)
    tps = seq // tm
    hb = CONV_HALO
    ext = tm + hb
    cur, prev, nxt, taps, vec = _conf_specs(n, d, tm)

    def body(uc_ref, un_ref, dsc_ref, dsn_ref, pap, pac, pgp, pgc, w_ref, lng_ref, lnb_ref,
             dpa_ref, dpg_ref, dw_ref, dwb_ref, dlng_ref, dlnb_ref, dba_ref, dbg_ref):
        i = pl.program_id(0)
        first = (i % tps) == 0
        last = (i % tps) == tps - 1

        @pl.when(i == 0)
        def _():
            for r in (dw_ref, dwb_ref, dlng_ref, dlnb_ref, dba_ref, dbg_ref):
                r[...] = jnp.zeros_like(r)

        ue = jnp.concatenate([uc_ref[...], un_ref[...]], axis=0)
        dse = jnp.concatenate([dsc_ref[...], jnp.where(last, 0.0, dsn_ref[...])], axis=0)
        mu = jnp.mean(ue, axis=-1, keepdims=True)
        cen = ue - mu
        r = lax.rsqrt(jnp.mean(cen * cen, axis=-1, keepdims=True) + EPS)
        xn = cen * r
        ul = xn * lng_ref[...] + lnb_ref[...]
        sg = _sig(ul)
        dul = dse * (sg * (1.0 + ul * (1.0 - sg)))
        dun = dul * lng_ref[...]
        du = r * (dun - jnp.mean(dun, axis=-1, keepdims=True) - xn * jnp.mean(dun * xn, axis=-1, keepdims=True))
        dlng_ref[...] += jnp.sum((dul * xn)[:tm], axis=0, keepdims=True)
        dlnb_ref[...] += jnp.sum(dul[:tm], axis=0, keepdims=True)
        du_own = du[:tm]
        dwb_ref[...] += jnp.sum(du_own, axis=0, keepdims=True)

        dz = w_ref[CONV_WIDTH - 1:CONV_WIDTH, :] * du
        for sh in range(1, CONV_WIDTH):
            dz = dz + w_ref[CONV_WIDTH - 1 - sh:CONV_WIDTH - sh, :] * _roll(du, ext - sh)
        dz = dz[:tm]

        a = jnp.concatenate([jnp.where(first, 0.0, pap[...]), pac[...]], axis=0)
        g = jnp.concatenate([jnp.where(first, 0.0, pgp[...]), pgc[...]], axis=0)
        sgg = _sig(g)
        z = a * sgg
        for sh in range(CONV_WIDTH):
            kk = CONV_WIDTH - 1 - sh
            dw_ref[kk:kk + 1, :] += jnp.sum(du_own * _roll(z, sh)[hb:], axis=0, keepdims=True)

        a_c, sg_c = a[hb:], sgg[hb:]
        da = dz * sg_c
        dg = dz * a_c * sg_c * (1.0 - sg_c)
        dpa_ref[...] = da.astype(dpa_ref.dtype)
        dpg_ref[...] = dg.astype(dpg_ref.dtype)
        dba_ref[...] += jnp.sum(da, axis=0, keepdims=True)
        dbg_ref[...] += jnp.sum(dg, axis=0, keepdims=True)

    vshape = jax.ShapeDtypeStruct((1, d), F32)
    return _hosted(
        body, comm, grid=(n // tm,),
        in_specs=[cur, nxt, cur, nxt, prev, cur, prev, cur, taps, vec, vec],
        out_specs=[cur, cur, taps, vec, vec, vec, vec, vec],
        out_shape=[jax.ShapeDtypeStruct((n, d), BF16), jax.ShapeDtypeStruct((n, d), BF16),
                   jax.ShapeDtypeStruct((CONV_HALO, d), F32), vshape, vshape, vshape, vshape, vshape],
        scratch_shapes=[], sem=("arbitrary",), name=name, ins=(u, u, ds, ds, pa, pa, pg, pg, w, lng, lnb))


def _pool_specs(n, d, tm, gd):
    hb = POOL_HALO
    cur = pl.BlockSpec((tm, d), lambda i: (i, 0))
    prev = pl.BlockSpec((hb, d), lambda i: (jnp.maximum(i * (tm // hb) - 1, 0), 0))
    nxt = pl.BlockSpec((hb, d), lambda i: (jnp.minimum((i + 1) * (tm // hb), n // hb - 1), 0))
    wsp = pl.BlockSpec((len(POOL_WINDOWS), gd, gd), lambda i: (0, 0, 0))
    vec = pl.BlockSpec((1, d), lambda i: (0, 0))
    return cur, prev, nxt, wsp, vec


def _pool_fwd(x, g, w, b, sc, seq, name):
    n, d = x.shape
    gd = d // len(POOL_WINDOWS)
    tm = _tile(seq, 256)
    tps = seq // tm
    hb = POOL_HALO
    cur, prev, _, wsp, vec = _pool_specs(n, d, tm, gd)

    def body(xp, xc, g_ref, w_ref, b_ref, sc_ref, x1_ref, p_ref):
        i = pl.program_id(0)
        first = (i % tps) == 0
        xe = jnp.concatenate([jnp.where(first, 0.0, xp[...]), xc[...]], axis=0)
        r = lax.rsqrt(jnp.mean(xe * xe, axis=-1, keepdims=True) + EPS)
        h = xe * r * g_ref[...]
        t = ((i % tps) * tm + lax.broadcasted_iota(jnp.int32, (tm, 1), 0) + 1).astype(F32)
        ys = []
        for gi, win in enumerate(POOL_WINDOWS):
            hg = h[:, gi * gd:(gi + 1) * gd]
            s, sh = hg, 1
            while sh < win:
                s = s + _roll(s, sh)
                sh *= 2
            p = (s[hb:] / jnp.minimum(t, float(win)) - hg[hb:]).astype(BF16)
            p_ref[:, gi * gd:(gi + 1) * gd] = p
            ys.append(jnp.dot(p, w_ref[gi], preferred_element_type=F32))
        y = jnp.concatenate(ys, axis=1) + b_ref[...]
        x1_ref[...] = xc[...] + y * sc_ref[...]

    return pl.pallas_call(
        body, grid=(n // tm,), in_specs=[prev, cur, vec, wsp, vec, vec], out_specs=[cur, cur],
        out_shape=[jax.ShapeDtypeStruct((n, d), F32), jax.ShapeDtypeStruct((n, d), BF16)],
        name=name, compiler_params=_cp("arbitrary"),
    )(x, x, g, w, b, sc)


def _pool_bwd(dx1, x, p, g, w, b, sc, seq, name):
    n, d = x.shape
    ng = len(POOL_WINDOWS)
    gd = d // ng
    tm = _tile(seq, 256)
    tps = seq // tm
    hb = POOL_HALO
    ext = tm + hb
    cur, _, nxt, wsp, vec = _pool_specs(n, d, tm, gd)

    def body(dc_ref, dn_ref, x_ref, p_ref, g_ref, w_ref, b_ref, sc_ref, dx_ref, dg_ref, dw_ref, db_ref, dsc_ref):
        i = pl.program_id(0)
        last = (i % tps) == tps - 1

        @pl.when(i == 0)
        def _():
            for r_ in (dg_ref, dw_ref, db_ref, dsc_ref):
                r_[...] = jnp.zeros_like(r_)

        dxc = dc_ref[...]
        dxe = jnp.concatenate([dxc, jnp.where(last, 0.0, dn_ref[...])], axis=0)
        dyg = dxe * sc_ref[...]
        t = ((i % tps) * tm + lax.broadcasted_iota(jnp.int32, (ext, 1), 0) + 1).astype(F32)
        dhs = []
        for gi, win in enumerate(POOL_WINDOWS):
            sl = slice(gi * gd, (gi + 1) * gd)
            dyb = dyg[:, sl].astype(BF16)
            wg = w_ref[gi]
            dp = lax.dot_general(dyb, wg, (((1,), (1,)), ((), ())), preferred_element_type=F32)
            s, sh = dp / jnp.minimum(t, float(win)), 1
            while sh < win:
                s = s + _roll(s, ext - sh)
                sh *= 2
            dhs.append((s - dp)[:tm])
            pg = p_ref[:, sl]
            dw_ref[gi] += lax.dot_general(pg, dyb[:tm], (((0,), (0,)), ((), ())), preferred_element_type=F32)
            ypre = jnp.dot(pg, wg, preferred_element_type=F32) + b_ref[:, sl]
            dsc_ref[:, sl] += jnp.sum(dxc[:, sl] * ypre, axis=0, keepdims=True)
            db_ref[:, sl] += jnp.sum(dyg[:tm, sl], axis=0, keepdims=True)
        dh = jnp.concatenate(dhs, axis=1)
        xv = x_ref[...]
        r = lax.rsqrt(jnp.mean(xv * xv, axis=-1, keepdims=True) + EPS)
        xn = xv * r
        dxn = dh * g_ref[...]
        dx_ref[...] = dxc + r * (dxn - xn * jnp.mean(dxn * xn, axis=-1, keepdims=True))
        dg_ref[...] += jnp.sum(dh * xn, axis=0, keepdims=True)

    vshape = jax.ShapeDtypeStruct((1, d), F32)
    return pl.pallas_call(
        body, grid=(n // tm,), in_specs=[cur, nxt, cur, cur, vec, wsp, vec, vec],
        out_specs=[cur, vec, wsp, vec, vec],
        out_shape=[jax.ShapeDtypeStruct((n, d), F32), vshape, jax.ShapeDtypeStruct((ng, gd, gd), F32), vshape, vshape],
        name=name, compiler_params=_cp("arbitrary"),
    )(dx1, dx1, x, p, g, w, b, sc)


def _head_maps(d):
    hd = lax.broadcasted_iota(jnp.int32, (d, LANES), 0) // HEAD_DIM
    col = lax.broadcasted_iota(jnp.int32, (d, LANES), 1)
    gm = (hd == col).astype(F32)
    hd_t = lax.broadcasted_iota(jnp.int32, (LANES, d), 1) // HEAD_DIM
    row = lax.broadcasted_iota(jnp.int32, (LANES, d), 0)
    gt = (hd_t == row).astype(F32)
    return gm, gt


def _bias_placement(nh):
    pq = np.zeros((3 * LANES, nh * HEAD_DIM), np.float32)
    pk = np.zeros((3 * LANES, nh * HEAD_DIM), np.float32)
    oq = np.zeros((1, nh * HEAD_DIM), np.float32)
    ok = np.zeros((1, nh * HEAD_DIM), np.float32)
    for h in range(nh):
        for piece in range(3):
            pq[piece * LANES + h, h * HEAD_DIM + piece] = 1.0
            pk[piece * LANES + h, h * HEAD_DIM + 3 + piece] = -1.0
            oq[0, h * HEAD_DIM + 3 + piece] = 1.0
            ok[0, h * HEAD_DIM + piece] = 1.0
    return jnp.asarray(pq, BF16), jnp.asarray(pk, BF16), jnp.asarray(oq), jnp.asarray(ok)


def _fox_prep_fwd(qkv, fl, bf, qg, kg, seq, name):
    n, d3 = qkv.shape
    d = d3 // 3
    nh = d // HEAD_DIM
    tm = _tile(seq, 256)
    tps = seq // tm
    scale = 1.0 / math.sqrt(HEAD_DIM)
    pq, pk, oq, ok = _bias_placement(nh)

    def body(qkv_ref, fl_ref, bf_ref, qg_ref, kg_ref, pq_ref, pk_ref, oq_ref, ok_ref, q_ref, k_ref, v_ref, carry):
        first = (pl.program_id(0) % tps) == 0
        gm, gt = _head_maps(d)

        def head_norm(xr, gain):
            ss = jnp.dot(xr * xr, gm, precision=HI, preferred_element_type=F32)
            r = lax.rsqrt(ss / HEAD_DIM + EPS)
            return xr * jnp.dot(r, gt, precision=HI, preferred_element_type=F32) * gain

        qs = (head_norm(qkv_ref[:, :d], qg_ref[...]).astype(BF16).astype(F32) * scale).astype(BF16)
        kn = head_norm(qkv_ref[:, d:2 * d], kg_ref[...]).astype(BF16)
        v_ref[...] = qkv_ref[:, 2 * d:].astype(BF16)
        z = fl_ref[...] + bf_ref[...]
        logf = jnp.minimum(z, 0.0) - jnp.log1p(jnp.exp(-jnp.abs(z)))
        tri = (lax.broadcasted_iota(jnp.int32, (tm, tm), 0) >= lax.broadcasted_iota(jnp.int32, (tm, tm), 1)).astype(F32)

        @pl.when(first)
        def _():
            carry[...] = jnp.zeros_like(carry)

        c = jnp.dot(tri, logf, precision=HI, preferred_element_type=F32) + carry[...]
        carry[...] = c[tm - 1:tm, :]
        c1 = c.astype(BF16)
        r1 = c - c1.astype(F32)
        c2 = r1.astype(BF16)
        c3 = (r1 - c2.astype(F32)).astype(BF16)
        pieces = jnp.concatenate([c1, c2, c3], axis=1)
        eq = (jnp.dot(pieces, pq_ref[...], preferred_element_type=F32) + oq_ref[...]).astype(BF16)
        ek = (jnp.dot(pieces, pk_ref[...], preferred_element_type=F32) + ok_ref[...]).astype(BF16)
        for h in range(nh):
            lo, hi = h * HEAD_DIM, (h + 1) * HEAD_DIM
            q_ref[:, 2 * lo:2 * lo + HEAD_DIM] = qs[:, lo:hi]
            q_ref[:, 2 * lo + HEAD_DIM:2 * hi] = eq[:, lo:hi]
            k_ref[:, 2 * lo:2 * lo + HEAD_DIM] = kn[:, lo:hi]
            k_ref[:, 2 * lo + HEAD_DIM:2 * hi] = ek[:, lo:hi]

    row = lambda w: pl.BlockSpec((tm, w), lambda i: (i, 0))
    vec = lambda w: pl.BlockSpec((1, w), lambda i: (0, 0))
    full = lambda a: pl.BlockSpec(a.shape, lambda i: (0, 0))
    return pl.pallas_call(
        body, grid=(n // tm,),
        in_specs=[row(d3), row(LANES), vec(LANES), vec(d), vec(d), full(pq), full(pk), full(oq), full(ok)],
        out_specs=[row(2 * d), row(2 * d), row(d)],
        out_shape=[jax.ShapeDtypeStruct((n, 2 * d), BF16)] * 2 + [jax.ShapeDtypeStruct((n, d), BF16)],
        scratch_shapes=[pltpu.VMEM((1, LANES), F32)], name=name, compiler_params=_cp("arbitrary"),
    )(qkv, fl, bf, qg, kg, pq, pk, oq, ok)


def _fox_prep_bwd(qkv, dq, dk, dv, dc1, dc2, fl, bf, qg, kg, seq, name, comm=None):
    n, d3 = qkv.shape
    d = d3 // 3
    tm = _tile(seq, 256)
    tps = seq // tm
    nt = n // tm

    def body(qkv_ref, dq_ref, dk_ref, dv_ref, dc1_ref, dc2_ref, fl_ref, bf_ref, qg_ref, kg_ref,
             dqkv_ref, dfl_ref, dqg_ref, dkg_ref, dbf_ref, carry):
        i = pl.program_id(0)
        tile = nt - 1 - i
        last = (tile % tps) == tps - 1
        gm, gt = _head_maps(d)

        @pl.when(i == 0)
        def _():
            for r_ in (dqg_ref, dkg_ref, dbf_ref):
                r_[...] = jnp.zeros_like(r_)

        @pl.when(last)
        def _():
            carry[...] = jnp.zeros_like(carry)

        def head_norm_bwd(xr, dy, gain, dgain_ref):
            ss = jnp.dot(xr * xr, gm, precision=HI, preferred_element_type=F32)
            rf = jnp.dot(lax.rsqrt(ss / HEAD_DIM + EPS), gt, precision=HI, preferred_element_type=F32)
            xn = xr * rf
            dgain_ref[...] += jnp.sum(dy * xn, axis=0, keepdims=True)
            dyg = dy * gain
            mean = jnp.dot(dyg * xn, gm, precision=HI, preferred_element_type=F32) / HEAD_DIM
            return rf * (dyg - xn * jnp.dot(mean, gt, precision=HI, preferred_element_type=F32))

        dqkv_ref[:, :d] = head_norm_bwd(qkv_ref[:, :d], dq_ref[...], qg_ref[...], dqg_ref).astype(BF16)
        dqkv_ref[:, d:2 * d] = head_norm_bwd(qkv_ref[:, d:2 * d], dk_ref[...], kg_ref[...], dkg_ref).astype(BF16)
        dqkv_ref[:, 2 * d:] = dv_ref[...].astype(BF16)

        dc = dc1_ref[...] + dc2_ref[...]
        tri = (lax.broadcasted_iota(jnp.int32, (tm, tm), 0) <= lax.broadcasted_iota(jnp.int32, (tm, tm), 1)).astype(F32)
        dlog = jnp.dot(tri, dc, precision=HI, preferred_element_type=F32) + carry[...]
        carry[...] = dlog[0:1, :]
        dfl = dlog * (1.0 - _sig(fl_ref[...] + bf_ref[...]))
        dfl_ref[...] = dfl.astype(BF16)
        dbf_ref[...] += jnp.sum(dfl, axis=0, keepdims=True)

    row = lambda w: pl.BlockSpec((tm, w), lambda i: (nt - 1 - i, 0))
    vec = lambda w: pl.BlockSpec((1, w), lambda i: (0, 0))
    return _hosted(
        body, comm, grid=(nt,),
        in_specs=[row(d3), row(d), row(d), row(d), row(LANES), row(LANES), row(LANES), vec(LANES), vec(d), vec(d)],
        out_specs=[row(d3), row(LANES), vec(d), vec(d), vec(LANES)],
        out_shape=[jax.ShapeDtypeStruct((n, d3), BF16), jax.ShapeDtypeStruct((n, LANES), BF16),
                   jax.ShapeDtypeStruct((1, d), F32), jax.ShapeDtypeStruct((1, d), F32),
                   jax.ShapeDtypeStruct((1, LANES), F32)],
        scratch_shapes=[pltpu.VMEM((1, LANES), F32)], sem=("arbitrary",), name=name,
        ins=(qkv, dq, dk, dv, dc1, dc2, fl, bf, qg, kg))


def _attn_specs(bsz, seq, t):
    nb = seq // t
    blk = lambda w: pl.BlockSpec((t, w), lambda b, h, i: (b * nb + i, h))
    full = lambda w: pl.BlockSpec((seq, w), lambda b, h, i: (b, h))
    col = pl.BlockSpec((None, None, t, 2), lambda b, h, i: (b, h, i, 0))
    rows = pl.BlockSpec((None, None, nb, 2, t), lambda b, h, i: (b, h, 0, 0, 0))
    return nb, blk, full, col, rows


_NT = (((1,), (1,)), ((), ()))
ATTN_TILE = 512


def _head_lanes(t, hh):
    lane = lax.broadcasted_iota(jnp.int32, (t, LANES), 1)
    return (lane < HEAD_DIM) if hh == 0 else (lane >= HEAD_DIM)


def _flash_fwd(qa, ka, v, bsz, seq, name):
    n, d = v.shape
    hp = d // LANES
    t = _tile(seq, ATTN_TILE)
    nb, blk, full, col, _ = _attn_specs(bsz, seq, t)

    def body(q_ref, k_ref, v_ref, o_ref, lse_ref):
        i = pl.program_id(2)
        causal = lax.broadcasted_iota(jnp.int32, (t, t), 0) >= lax.broadcasted_iota(jnp.int32, (t, t), 1)

        def block(j, carry, masked):
            rs = pl.ds(pl.multiple_of(j * t, t), t)
            vj = v_ref[rs, :]
            out = []
            for hh in range(2):
                m, l, acc = carry[hh]
                hs = slice(hh * LANES, (hh + 1) * LANES)
                sc = lax.dot_general(q_ref[:, hs], k_ref[rs, hs], _NT, preferred_element_type=F32)
                if masked:
                    sc = jnp.where(causal, sc, NEG)
                mn = jnp.maximum(m, jnp.max(sc, axis=-1, keepdims=True))
                p = jnp.exp(sc - mn)
                al = jnp.exp(m - mn)
                l = al * l + jnp.sum(p, axis=-1, keepdims=True)
                acc = al * acc + jnp.dot(p.astype(BF16), vj, preferred_element_type=F32)
                out.append((mn, l, acc))
            return tuple(out)

        init = tuple((jnp.full((t, 1), NEG, F32), jnp.zeros((t, 1), F32), jnp.zeros((t, LANES), F32))
                     for _ in range(2))
        carry = lax.fori_loop(0, i, lambda j, c: block(j, c, False), init)
        (m0, l0, a0), (m1, l1, a1) = block(i, carry, True)
        o_ref[...] = jnp.where(_head_lanes(t, 0), a0 / l0, a1 / l1)
        lse_ref[:, 0:1] = m0 + jnp.log(l0)
        lse_ref[:, 1:2] = m1 + jnp.log(l1)

    return pl.pallas_call(
        body, grid=(bsz, hp, nb), in_specs=[blk(2 * LANES), full(2 * LANES), full(LANES)],
        out_specs=[blk(LANES), col],
        out_shape=[jax.ShapeDtypeStruct((n, d), F32), jax.ShapeDtypeStruct((bsz, hp, seq, 2), F32)],
        name=name, compiler_params=_cp("parallel", "parallel", "arbitrary"),
    )(qa, ka, v)


def _flash_bwd(qa, ka, v, do, o, lse_row, bsz, seq, name):
    n, d = v.shape
    hp = d // LANES
    t = _tile(seq, ATTN_TILE)
    nb, blk, full, col, rows = _attn_specs(bsz, seq, t)
    scale = 1.0 / math.sqrt(HEAD_DIM)
    tn_ = (((0,), (0,)), ((), ()))

    def body(k_ref, v_ref, q_ref, do_ref, o_ref, lse_ref, dq_ref, dcc_ref, dk_ref, dv_ref, dck_ref, dqa, dl):
        j = pl.program_id(2)
        causal = lax.broadcasted_iota(jnp.int32, (t, t), 1) >= lax.broadcasted_iota(jnp.int32, (t, t), 0)
        heads = [_head_lanes(t, 0), _head_lanes(t, 1)]

        @pl.when(j == 0)
        def _():
            dqa[...] = jnp.zeros_like(dqa)
            ones = jnp.ones((8, LANES), F32)
            for ib in range(nb):
                rs = slice(ib * t, (ib + 1) * t)
                prod = do_ref[rs, :] * o_ref[rs, :]
                for hh in range(2):
                    dl[ib, hh:hh + 1, :] = lax.dot_general(ones, jnp.where(heads[hh], prod, 0.0), _NT, precision=HI,
                                                           preferred_element_type=F32)[0:1]

        vj = v_ref[...]

        def block(i, carry, masked):
            rs = pl.ds(pl.multiple_of(i * t, t), t)
            doi = do_ref[rs, :]
            dks, dvp = list(carry[:2]), carry[2]
            for hh in range(2):
                hs = slice(hh * LANES, (hh + 1) * LANES)
                kh, qi = k_ref[:, hs], q_ref[rs, hs]
                dom = jnp.where(heads[hh], doi, 0.0).astype(BF16)
                st = lax.dot_general(kh, qi, _NT, preferred_element_type=F32)
                if masked:
                    st = jnp.where(causal, st, NEG)
                pt = jnp.exp(st - lse_ref[i, hh:hh + 1, :])
                dvp = dvp + jnp.dot(pt.astype(BF16), dom, preferred_element_type=F32)
                dpt = lax.dot_general(vj, dom, _NT, preferred_element_type=F32)
                dsb = (pt * (dpt - dl[i, hh:hh + 1, :])).astype(BF16)
                dks[hh] = dks[hh] + jnp.dot(dsb, qi, preferred_element_type=F32)
                dqa[rs, hs] += lax.dot_general(dsb, kh, tn_, preferred_element_type=F32)
            return dks[0], dks[1], dvp

        zero = jnp.zeros((t, LANES), F32)
        carry = block(j, (zero, zero, zero), True)
        dk0, dk1, dvp = lax.fori_loop(j + 1, nb, lambda i, c: block(i, c, False), carry)
        dk_ref[...] = jnp.where(heads[0], dk0, pltpu.roll(dk1, HEAD_DIM, 1))
        dv_ref[...] = dvp
        dck_ref[:, 0:1] = -dk0[:, HEAD_DIM + 3:HEAD_DIM + 4]
        dck_ref[:, 1:2] = -dk1[:, HEAD_DIM + 3:HEAD_DIM + 4]

        @pl.when(j == nb - 1)
        def _():
            first = lax.broadcasted_iota(jnp.int32, (seq, LANES), 1) < HEAD_DIM
            dq_ref[...] = jnp.where(first, dqa[:, :LANES], pltpu.roll(dqa[:, LANES:], HEAD_DIM, 1)) * scale
            for hh in range(2):
                lo = hh * LANES + HEAD_DIM
                dcc_ref[:, hh:hh + 1] = dqa[:, lo:lo + 1]

    whole_col = pl.BlockSpec((None, None, seq, 2), lambda b, h, i: (b, h, 0, 0))
    cshape = jax.ShapeDtypeStruct((bsz, hp, seq, 2), F32)
    nd = jax.ShapeDtypeStruct((n, d), F32)
    return pl.pallas_call(
        body, grid=(bsz, hp, nb),
        in_specs=[blk(2 * LANES), blk(LANES), full(2 * LANES), full(LANES), full(LANES), rows],
        out_specs=[full(LANES), whole_col, blk(LANES), blk(LANES), col],
        out_shape=[nd, cshape, nd, nd, cshape],
        scratch_shapes=[pltpu.VMEM((seq, 2 * LANES), F32), pltpu.VMEM((nb, 2, t), F32)],
        name=name, compiler_params=_cp("parallel", "parallel", "arbitrary"),
    )(ka, v, qa, do, o, lse_row)


def _adamw(w, g, m, v, name):
    r, c = w.shape
    tr = r
    for cand in (512, 256, 128, 64, 32, 16, 8):
        if r % cand == 0 and r > cand and cand * c * 4 <= 4 * 1024 * 1024:
            tr = cand
            break

    def body(w_ref, g_ref, m_ref, v_ref, d_ref, m2_ref, v2_ref):
        gv = g_ref[...]
        m2 = ADAM_B1 * m_ref[...] + (1.0 - ADAM_B1) * gv
        v2 = ADAM_B2 * v_ref[...] + (1.0 - ADAM_B2) * jnp.square(gv)
        m_hat = m2 / (1.0 - ADAM_B1 ** ADAM_STEP)
        v_hat = v2 / (1.0 - ADAM_B2 ** ADAM_STEP)
        d_ref[...] = -ADAM_LR * (m_hat / (jnp.sqrt(v_hat) + ADAM_EPS) + ADAM_WD * w_ref[...])
        m2_ref[...] = m2
        v2_ref[...] = v2

    blk = pl.BlockSpec((tr, c), lambda i: (i, 0))
    shp = jax.ShapeDtypeStruct((r, c), F32)
    return pl.pallas_call(
        body, grid=(r // tr,), in_specs=[blk] * 4, out_specs=[blk] * 3, out_shape=[shp] * 3,
        name=name, compiler_params=_cp("parallel"),
    )(w, g, m, v)


def _adamw_nd(w, g, m, v, name):
    shape = w.shape
    two = (math.prod(shape[:-1]), shape[-1])
    d_, m_, v_ = _adamw(w.reshape(two), g.reshape(two), m.reshape(two), v.reshape(two), name)
    return g.reshape(shape), d_.reshape(shape), m_.reshape(shape), v_.reshape(shape)


def _place():
    x, y, c = lax.axis_index("x"), lax.axis_index("y"), lax.axis_index("c")
    chips = [(1 - x, y), (x, 1 - y), (1 - x, 1 - y)]
    return x, y, c, chips


def _gather_chips(shards):
    nt = len(shards)
    halves = [s.shape[0] // 2 for s in shards]

    def copies(ins, outs, send_sems, recv_sems):
        x, y, c, chips = _place()
        cps = []
        for t in range(nt):
            rows = pl.ds(c * halves[t], halves[t])
            for jj, (cx, cy) in enumerate(chips):
                cps.append(pltpu.make_async_remote_copy(
                    src_ref=ins[t].at[rows, :], dst_ref=outs[t].at[2 * x + y, rows, :], send_sem=send_sems.at[3 * t + jj],
                    recv_sem=recv_sems.at[3 * t + jj], device_id=(cx, cy, c), device_id_type=MESH))
        return cps

    def start(*refs):
        for cp in copies(*refs):
            cp.start()

    def wait(*refs):
        for cp in copies(*refs):
            cp.wait()

    return _Comm(shards, [jax.ShapeDtypeStruct((N_CHIPS,) + s.shape, s.dtype) for s in shards], 3 * nt, start, wait)


def _gather_sibling(shards, bufs, name):
    nt = len(shards)
    halves = [s.shape[0] // 2 for s in shards]

    def body(*refs):
        ins, outs = refs[:nt], refs[2 * nt:3 * nt]
        send_sems, recv_sems, local_sems = refs[3 * nt:]
        x, y, c, chips = _place()
        sibling = (x, y, 1 - c)
        local = [pltpu.make_async_copy(ins[t], outs[t].at[2 * x + y], local_sems.at[t]) for t in range(nt)]

        def copy(t, jj, hf):
            cx, cy = chips[jj]
            region = outs[t].at[2 * cx + cy, pl.ds(hf * halves[t], halves[t]), :]
            return pltpu.make_async_remote_copy(src_ref=region, dst_ref=region, send_sem=send_sems.at[3 * t + jj],
                                                recv_sem=recv_sems.at[3 * t + jj], device_id=sibling,
                                                device_id_type=MESH)

        fwd = [copy(t, jj, c) for t in range(nt) for jj in range(3)]
        for cp in local + fwd:
            cp.start()
        for t in range(nt):
            for jj in range(3):
                copy(t, jj, 1 - c).wait_recv()
        for cp in fwd:
            cp.wait_send()
        for cp in local:
            cp.wait()

    return pl.pallas_call(
        body, in_specs=[ANY] * (2 * nt), out_specs=[ANY] * nt,
        out_shape=[jax.ShapeDtypeStruct(b.shape, b.dtype) for b in bufs],
        scratch_shapes=[pltpu.SemaphoreType.DMA((3 * nt,)), pltpu.SemaphoreType.DMA((3 * nt,)),
                        pltpu.SemaphoreType.DMA((nt,))],
        input_output_aliases={nt + t: t for t in range(nt)}, name=name,
        compiler_params=pltpu.CompilerParams(has_side_effects=True),
    )(*shards, *bufs)


def _send_sibling_halves(grads, name):
    nt = len(grads)
    halves = [g.shape[1] // 2 for g in grads]

    def body(*refs):
        ins, outs = refs[:nt], refs[nt:2 * nt]
        send_sems, recv_sems = refs[2 * nt:]
        x, y, c, _ = _place()
        cps = []
        for t in range(nt):
            src = ins[t].at[:, pl.ds((1 - c) * halves[t], halves[t]), :]
            cps.append(pltpu.make_async_remote_copy(src_ref=src, dst_ref=outs[t], send_sem=send_sems.at[t],
                                                    recv_sem=recv_sems.at[t], device_id=(x, y, 1 - c),
                                                    device_id_type=MESH))
        for cp in cps:
            cp.start()
        for cp in cps:
            cp.wait()

    return pl.pallas_call(
        body, in_specs=[ANY] * nt, out_specs=[ANY] * nt,
        out_shape=[jax.ShapeDtypeStruct((N_CHIPS, h, g.shape[2]), g.dtype) for g, h in zip(grads, halves)],
        scratch_shapes=[pltpu.SemaphoreType.DMA((nt,)), pltpu.SemaphoreType.DMA((nt,))],
        name=name, compiler_params=pltpu.CompilerParams(has_side_effects=True),
    )(*grads)


def _add_sibling(g, r1, c_idx, out_dtype, name):
    s, r, w = g.shape
    rh = r // 2
    tr = rh
    for cand in (512, 256, 128, 64, 32, 16, 8):
        if rh % cand == 0 and cand * w * 4 <= 4 * 1024 * 1024:
            tr = cand
            break
    per = rh // tr

    def body(c_ref, g_ref, r_ref, o_ref):
        o_ref[...] = (g_ref[...] + r_ref[...]).astype(o_ref.dtype)

    return pl.pallas_call(
        body,
        grid_spec=pltpu.PrefetchScalarGridSpec(
            num_scalar_prefetch=1, grid=(s, per),
            in_specs=[pl.BlockSpec((None, tr, w), lambda a, b, c_ref: (a, c_ref[0] * per + b, 0)),
                      pl.BlockSpec((None, tr, w), lambda a, b, c_ref: (a, b, 0))],
            out_specs=pl.BlockSpec((None, tr, w), lambda a, b, c_ref: (a, b, 0))),
        out_shape=jax.ShapeDtypeStruct((s, rh, w), out_dtype), name=name,
        compiler_params=_cp("parallel", "parallel"),
    )(c_idx, g, r1)


def _chip_slots(parts):
    nt = len(parts)

    def copies(ins, outs, send_sems, recv_sems):
        x, y, c, chips = _place()
        return [pltpu.make_async_remote_copy(
            src_ref=ins[t].at[2 * cx + cy], dst_ref=outs[t].at[jj], send_sem=send_sems.at[3 * t + jj],
            recv_sem=recv_sems.at[3 * t + jj], device_id=(cx, cy, c), device_id_type=MESH)
            for t in range(nt) for jj, (cx, cy) in enumerate(chips)]

    def start(*refs):
        for cp in copies(*refs):
            cp.start()

    def wait(*refs):
        for cp in copies(*refs):
            cp.wait()

    return _Comm(parts, [jax.ShapeDtypeStruct((3,) + p.shape[1:], p.dtype) for p in parts], 3 * nt, start, wait)


def _add_chips(part, r2, chip_idx, name):
    s, rh, w = part.shape
    tr = rh
    for cand in (512, 256, 128, 64, 32, 16, 8):
        if rh % cand == 0 and cand * w * 4 <= 4 * 1024 * 1024:
            tr = cand
            break

    def body(b_ref, p_ref, r0_ref, r1_ref, r2_ref, o_ref):
        up = lambda ref: ref[...].astype(F32)
        o_ref[...] = ((up(p_ref) + up(r0_ref)) + up(r1_ref)) + up(r2_ref)

    return pl.pallas_call(
        body,
        grid_spec=pltpu.PrefetchScalarGridSpec(
            num_scalar_prefetch=1, grid=(rh // tr,),
            in_specs=[pl.BlockSpec((None, tr, w), lambda a, b_ref: (b_ref[0], a, 0))]
            + [pl.BlockSpec((None, tr, w), lambda a, b_ref, jj=jj: (jj, a, 0)) for jj in range(3)],
            out_specs=pl.BlockSpec((tr, w), lambda a, b_ref: (a, 0))),
        out_shape=jax.ShapeDtypeStruct((rh, w), F32), name=name, compiler_params=_cp("parallel"),
    )(chip_idx, part, r2, r2, r2)


def _swap_halves(halves_, name):
    nt = len(halves_)

    def body(*refs):
        ins, outs = refs[:nt], refs[nt:2 * nt]
        send_sems, recv_sems, local_sems = refs[2 * nt:]
        x, y, c, _ = _place()
        cps, loc = [], []
        for t in range(nt):
            rh = ins[t].shape[0]
            dst = outs[t].at[pl.ds(c * rh, rh), :]
            loc.append(pltpu.make_async_copy(ins[t], dst, local_sems.at[t]))
            cps.append(pltpu.make_async_remote_copy(src_ref=ins[t], dst_ref=dst, send_sem=send_sems.at[t],
                                                    recv_sem=recv_sems.at[t], device_id=(x, y, 1 - c),
                                                    device_id_type=MESH))
        for cp in loc + cps:
            cp.start()
        for t in range(nt):
            rh = ins[t].shape[0]
            got = outs[t].at[pl.ds((1 - c) * rh, rh), :]
            pltpu.make_async_remote_copy(src_ref=ins[t], dst_ref=got, send_sem=send_sems.at[t],
                                         recv_sem=recv_sems.at[t], device_id=(x, y, 1 - c),
                                         device_id_type=MESH).wait_recv()
        for cp in cps:
            cp.wait_send()
        for cp in loc:
            cp.wait()

    return pl.pallas_call(
        body, in_specs=[ANY] * nt, out_specs=[ANY] * nt,
        out_shape=[jax.ShapeDtypeStruct((2 * h.shape[0], h.shape[1]), h.dtype) for h in halves_],
        scratch_shapes=[pltpu.SemaphoreType.DMA((nt,)), pltpu.SemaphoreType.DMA((nt,)),
                        pltpu.SemaphoreType.DMA((nt,))],
        name=name, compiler_params=pltpu.CompilerParams(has_side_effects=True),
    )(*halves_)


def _rs_begin(grads, wire, c_idx, tag):
    r1 = _send_sibling_halves(grads, name=f"rs_sibling_{tag}")
    return [_add_sibling(g, r, c_idx, wire[t], name=f"rs_add_sibling_{tag}_{t}")
            for t, (g, r) in enumerate(zip(grads, r1))]


def _rs_finish(parts, r2, chip_idx, tag):
    fin = [_add_chips(p, r, chip_idx, name=f"rs_add_chips_{tag}_{t}") for t, (p, r) in enumerate(zip(parts, r2))]
    return _swap_halves(fin, name=f"rs_swap_{tag}")


def _all_reduce_small(v, name):
    r, w = v.shape

    def body(v_ref, o_ref, buf, send_sems, recv_sems):
        x, y, c, _ = _place()
        me = 4 * x + 2 * y + c
        buf[me] = v_ref[...]
        cps = []
        for kk in range(1, 8):
            peer = (x ^ ((kk >> 2) & 1), y ^ ((kk >> 1) & 1), c ^ (kk & 1))
            cps.append(pltpu.make_async_remote_copy(src_ref=v_ref, dst_ref=buf.at[me], send_sem=send_sems.at[kk - 1],
                                                    recv_sem=recv_sems.at[kk - 1], device_id=peer, device_id_type=MESH))
        for cp in cps:
            cp.start()
        for kk in range(1, 8):
            pltpu.make_async_remote_copy(src_ref=v_ref, dst_ref=buf.at[me ^ kk], send_sem=send_sems.at[kk - 1],
                                         recv_sem=recv_sems.at[kk - 1], device_id=(x, y, c),
                                         device_id_type=MESH).wait_recv()
        for cp in cps:
            cp.wait_send()
        acc = buf[0]
        for dev in range(1, 8):
            acc = acc + buf[dev]
        o_ref[...] = acc

    vm = pl.BlockSpec(memory_space=pltpu.VMEM)
    return pl.pallas_call(
        body, in_specs=[vm], out_specs=vm, out_shape=jax.ShapeDtypeStruct((r, w), F32),
        scratch_shapes=[pltpu.VMEM((8, r, w), F32), pltpu.SemaphoreType.DMA((7,)), pltpu.SemaphoreType.DMA((7,))],
        name=name, compiler_params=pltpu.CompilerParams(has_side_effects=True),
    )(v)


def _to_shards(a, axis=-1):
    axis = axis % a.ndim
    shp = a.shape
    a = a.reshape(shp[:axis] + (N_CHIPS, shp[axis] // N_CHIPS) + shp[axis + 1:])
    return jnp.moveaxis(a, axis, 0).reshape(N_CHIPS, -1)


def _from_shards(s, shard_shape, axis=-1):
    axis = axis % len(shard_shape)
    a = jnp.moveaxis(s.reshape((N_CHIPS,) + tuple(shard_shape)), 0, axis)
    return a.reshape(tuple(shard_shape[:axis]) + (N_CHIPS * shard_shape[axis],) + tuple(shard_shape[axis + 1:]))


def _pack(vecs, rows):
    flat = jnp.concatenate([v.reshape(v.shape[0], -1) if v.ndim > 1 else v.reshape(1, -1) for v in vecs], axis=1)
    lead = flat.shape[0]
    flat = jnp.pad(flat, ((0, 0), (0, rows * LANES - flat.shape[1])))
    return flat.reshape(lead, rows, LANES)


def _pack_rows(sizes, mult):
    total = sum(sizes)
    rows = -(-total // LANES)
    return -(-rows // mult) * mult


def _unpack(flat, shapes):
    out, pos = [], 0
    for shp in shapes:
        sz = math.prod(shp)
        out.append(flat[..., pos:pos + sz].reshape(flat.shape[:-1] + tuple(shp)))
        pos += sz
    return out


def _row_layout(col, t):
    bsz, hp, seq, _ = col.shape
    return col.reshape(bsz, hp, seq // t, t, 2).transpose(0, 1, 2, 4, 3)


def _from_col_layout(col):
    bsz, hp, seq, _ = col.shape
    a = col.transpose(0, 2, 1, 3).reshape(bsz * seq, 2 * hp)
    return jnp.pad(a, ((0, 0), (0, LANES - 2 * hp)))


def kernel(x, norm_mix, norm_ffn, conv_w_in, conv_b_in, conv_dw, conv_dw_b, conv_ln_g, conv_ln_b, conv_w_out, conv_b_out, pool_w, pool_b, pool_scale, fox_w_in, fox_b_f, fox_q_gain, fox_k_gain, fox_w_o, ffn_w_up, ffn_dw, ffn_dw_b, ffn_w_down, loss_target, m_norm_mix, m_norm_ffn, m_conv_w_in, m_conv_b_in, m_conv_dw, m_conv_dw_b, m_conv_ln_g, m_conv_ln_b, m_conv_w_out, m_conv_b_out, m_pool_w, m_pool_b, m_pool_scale, m_fox_w_in, m_fox_b_f, m_fox_q_gain, m_fox_k_gain, m_fox_w_o, m_ffn_w_up, m_ffn_dw, m_ffn_dw_b, m_ffn_w_down, v_norm_mix, v_norm_ffn, v_conv_w_in, v_conv_b_in, v_conv_dw, v_conv_dw_b, v_conv_ln_g, v_conv_ln_b, v_conv_w_out, v_conv_b_out, v_pool_w, v_pool_b, v_pool_scale, v_fox_w_in, v_fox_b_f, v_fox_q_gain, v_fox_k_gain, v_fox_w_o, v_ffn_w_up, v_ffn_dw, v_ffn_dw_b, v_ffn_w_down):
    bsz, seq, d = x.shape
    n = bsz * seq
    depth = norm_mix.shape[0]
    n_conv, n_pool, n_fox = conv_w_in.shape[0], pool_w.shape[0], fox_w_in.shape[0]
    f2 = ffn_dw_b.shape[1]
    f = f2 // 2
    nh = d // HEAD_DIM
    hp = d // LANES
    ng = len(POOL_WINDOWS)
    gd = d // ng
    c_idx = lax.axis_index("c").astype(jnp.int32).reshape(1)
    chip_idx = (2 * lax.axis_index("x") + lax.axis_index("y")).astype(jnp.int32).reshape(1)

    small_shapes = [conv_b_in.shape, conv_dw.shape, conv_dw_b.shape, conv_ln_g.shape, conv_ln_b.shape,
                    conv_b_out.shape, pool_b.shape, ffn_dw.shape]
    small_rows = _pack_rows([math.prod(s) for s in small_shapes], 16)
    small = _pack([v.reshape(1, -1) for v in (conv_b_in, conv_dw, conv_dw_b, conv_ln_g, conv_ln_b, conv_b_out,
                                                pool_b, ffn_dw)], small_rows)[0]
    def layer_shards(i):
        kind, j = i % 3, i // 3
        shards = [ffn_w_up[i].astype(BF16), ffn_w_down[i].astype(BF16)]
        if kind == 0:
            shards += [conv_w_in[j].astype(BF16), conv_w_out[j].astype(BF16)]
        elif kind == 1:
            shards += [pool_w[j].reshape(ng * (gd // N_CHIPS), gd).astype(BF16)]
        else:
            shards += [fox_w_in[j].astype(BF16), fox_w_o[j].astype(BF16)]
        if i == 0:
            shards.append(small)
        return shards

    gathered = [None] * depth
    gathered[0] = _gather_sibling(layer_shards(0), _run_comm(_gather_chips(layer_shards(0)), name="gather_chips_l0"),
                                  name="gather_sibling_l0")
    small_all = gathered[0][-1].reshape(N_CHIPS, -1)
    sm = _unpack(small_all, small_shapes)
    axes = [-1] * 8
    b_in_f, dw_f, dw_b_f, ln_g_f, ln_b_f, b_out_f, pool_b_f, ffn_dw_f = [
        _from_shards(s_.reshape(N_CHIPS, -1), shp, ax) for s_, shp, ax in zip(sm, small_shapes, axes)]

    xs = x.reshape(n, d)
    tgt = loss_target.reshape(n, d)
    vec = lambda a: a.reshape(1, -1)

    saved = []
    cur = xs
    for i in range(depth):
        kind, j = i % 3, i // 3
        wts = gathered[i]
        sv = {"x_in": cur}
        if kind == 0:
            w_in, w_out = wts[2], wts[3].reshape(d, d)
            wcol = w_in.shape[2]
            h = _rms_fwd(cur, vec(norm_mix[i]), name=f"rms_mix_l{i}")
            pa = _mm(h, w_in, m=n, n=d, k=d, tk=d, tn=wcol, b_stk=wcol, b_s0=0, bias=vec(b_in_f[j, :d]),
                     name=f"conv_in_a_l{i}")
            pg = _mm(h, w_in, m=n, n=d, k=d, tk=d, tn=wcol, b_stk=wcol, b_s0=2, bias=vec(b_in_f[j, d:]),
                     name=f"conv_in_g_l{i}")
            taps = jnp.pad(dw_f[j], ((0, CONV_HALO - CONV_WIDTH), (0, 0)))
            u, s_ = _conf_fwd(pa, pg, taps, vec(dw_b_f[j]), vec(ln_g_f[j]), vec(ln_b_f[j]), seq, name=f"conf_fwd_l{i}")
            cur = _mm(s_, w_out, m=n, n=d, k=d, tk=d, bias=vec(b_out_f[j]), res=cur, name=f"conv_out_l{i}")
            sv.update(h=h, pa=pa, pg=pg, u=u, s=s_, taps=taps)
        elif kind == 1:
            pw = wts[2].reshape(N_CHIPS, ng, gd // N_CHIPS, gd).transpose(1, 0, 2, 3).reshape(ng, gd, gd)
            cur, p = _pool_fwd(cur, vec(norm_mix[i]), pw, vec(pool_b_f[j]), vec(pool_scale[j]), seq,
                               name=f"pool_fwd_l{i}")
            sv.update(p=p, pw=pw)
        else:
            w_in = wts[2].transpose(1, 0, 2).reshape(d, -1)
            w_qkv = w_in[:, :3 * d]
            w_f = jnp.pad(w_in[:, 3 * d:], ((0, 0), (0, LANES - nh)))
            w_o = wts[3].reshape(d, d)
            bf = jnp.pad(vec(fox_b_f[j]), ((0, 0), (0, LANES - nh)))
            qg, kg = jnp.tile(vec(fox_q_gain[j]), (1, nh)), jnp.tile(vec(fox_k_gain[j]), (1, nh))
            h = _rms_fwd(cur, vec(norm_mix[i]), name=f"rms_mix_l{i}")
            qkv = _mm(h, w_qkv, m=n, n=3 * d, k=d, tk=d, tn=d, name=f"fox_qkv_l{i}")
            fl = _mm(h, w_f, m=n, n=LANES, k=d, tk=d, name=f"fox_fl_l{i}")
            qa, ka, v = _fox_prep_fwd(qkv, fl, bf, qg, kg, seq, name=f"fox_prep_l{i}")
            o, lse = _flash_fwd(qa, ka, v, bsz, seq, name=f"fox_attn_l{i}")
            cur = _mm(o, w_o, m=n, n=d, k=d, tk=d, res=cur, name=f"fox_out_l{i}")
            sv.update(h=h, qkv=qkv, fl=fl, qa=qa, ka=ka, v=v, o=o, lse=lse, w_qkv=w_qkv, w_f=w_f, w_o=w_o, bf=bf,
                      qg=qg, kg=kg)
        w_up, w_down = wts[0], wts[1].reshape(f, d)
        ucol = w_up.shape[2]
        sv["x_mid"] = cur
        h2 = _rms_fwd(cur, vec(norm_ffn[i]), name=f"rms_ffn_l{i}")
        uv = _mm(h2, w_up, m=n, n=f, k=d, tk=d, tn=ucol, b_stk=ucol, b_s0=0, name=f"ffn_up_v_l{i}")
        ug = _mm(h2, w_up, m=n, n=f, k=d, tk=d, tn=ucol, b_stk=ucol, b_s0=2, name=f"ffn_up_g_l{i}")
        fdw, fdb = ffn_dw_f[i], ffn_dw_b[i]
        glu_args = (uv, ug, fdw[:, :f], fdw[:, f:], vec(fdb[:f]), vec(fdb[f:]), seq)
        if i + 1 < depth:
            nxt_shards = layer_shards(i + 1)
            (a_,), landed = _ffn_glu_fwd(*glu_args, name=f"ffn_glu_l{i}", comm=_gather_chips(nxt_shards))
            gathered[i + 1] = _gather_sibling(nxt_shards, landed, name=f"gather_sibling_l{i + 1}")
        else:
            (a_,) = _ffn_glu_fwd(*glu_args, name=f"ffn_glu_l{i}")
        cur = _mm(a_, w_down, m=n, n=d, k=f, tk=f, tn=d, res=cur, name=f"ffn_down_l{i}")
        sv.update(h2=h2, uv=uv, ug=ug, a=a_)
        saved.append(sv)

    dy, loss_part = _loss(cur, tgt, name="loss")
    loss = lax.psum(loss_part[0, 0], ("x", "y", "c"))

    g_norm_mix, g_norm_ffn = [None] * depth, [None] * depth
    g_ffn_dw_b = [None] * depth
    g_up, g_down = [None] * depth, [None] * depth
    g_small = {}
    g_conv_in, g_conv_out = [None] * n_conv, [None] * n_conv
    g_pool_w = g_fox_in = g_fox_o = None
    g_pool_scale = g_bf = g_qg = g_kg = None
    part_small = {"b_in": [None] * n_conv, "dw": [None] * n_conv, "dw_b": [None] * n_conv, "ln_g": [None] * n_conv,
                  "ln_b": [None] * n_conv, "b_out": [None] * n_conv, "pool_b": None, "ffn_dw": [None] * depth}

    pending = []
    done = {}

    def take_pending():
        groups = list(pending)
        pending.clear()
        parts = [p_ for g_ in groups for p_ in g_[0]]
        return groups, (_chip_slots(parts) if parts else None)

    def finish_groups(groups, r2):
        pos = 0
        for parts, tag, sink in groups:
            sink(_rs_finish(parts, r2[pos:pos + len(parts)], chip_idx, tag))
            pos += len(parts)

    def carried(fn, args, name):
        groups, comm = take_pending()
        if comm is None:
            return fn(*args, name=name)
        outs, r2 = fn(*args, name=name, comm=comm)
        finish_groups(groups, r2)
        return outs

    dcur = dy
    for i in reversed(range(depth)):
        kind, j = i % 3, i // 3
        wts, sv = gathered[i], saved[i]
        w_up, w_down = wts[0], wts[1].reshape(f, d)
        ucol = w_up.shape[2]
        fdw, fdb = ffn_dw_f[i], ffn_dw_b[i]
        da = _mm(dcur, w_down, m=n, n=f, k=d, tk=d, tn=f // 2 if (f // 2) % LANES == 0 else f, tb=True,
                 name=f"ffn_down_dx_l{i}")
        dw_down = _mm(sv["a"], dcur, m=f, n=d, k=n, tm=f // 2, tn=d, tk=1024, ta=True, name=f"ffn_down_dw_l{i}")
        duv, dug, dwv, dwg, dbv, dbg = carried(
            _ffn_glu_bwd, (sv["uv"], sv["ug"], da, fdw[:, :f], fdw[:, f:], vec(fdb[:f]), vec(fdb[f:]), seq),
            name=f"ffn_glu_bwd_l{i}")
        dw_up = _mm(sv["h2"], duv, m=d, n=f, k=n, tm=d, tn=ucol, tk=1024, ta=True, o_stk=ucol, o_s0=0,
                    o_slots=N_CHIPS, name=f"ffn_up_dw_v_l{i}")
        dw_up = _mm(sv["h2"], dug, m=d, n=f, k=n, tm=d, tn=ucol, tk=1024, ta=True, o_stk=ucol, o_s0=2,
                    o_slots=N_CHIPS, o_buf=dw_up, name=f"ffn_up_dw_g_l{i}")
        dh2 = _mm(duv, w_up, m=n, n=d, k=f, tn=d, tk=ucol, tb=True, b_stk=ucol, b_s0=0, name=f"ffn_up_dx_v_l{i}")
        dh2 = _mm(dug, w_up, m=n, n=d, k=f, tn=d, tk=ucol, tb=True, b_stk=ucol, b_s0=2, res=dh2,
                  name=f"ffn_up_dx_g_l{i}")
        dmid, g_norm_ffn[i] = _rms_bwd(sv["x_mid"], dh2, vec(norm_ffn[i]), dcur, name=f"rms_ffn_bwd_l{i}")
        part_small["ffn_dw"][i] = jnp.concatenate([dwv, dwg], axis=1)
        g_ffn_dw_b[i] = jnp.concatenate([dbv, dbg], axis=1)

        def ffn_sink(red, i=i):
            g_up[i], g_down[i] = red[0], red[1]

        pending.append((_rs_begin([dw_up, dw_down.reshape(N_CHIPS, f // N_CHIPS, d)], [BF16, BF16], c_idx,
                                  tag=f"ffn_l{i}"), f"ffn_l{i}", ffn_sink))

        if kind == 0:
            w_in, w_out = wts[2], wts[3].reshape(d, d)
            wcol = w_in.shape[2]
            ds = _mm(dmid, w_out, m=n, n=d, k=d, tk=d, tb=True, name=f"conv_out_dx_l{i}")
            dw_out = _mm(sv["s"], dmid, m=d, n=d, k=n, tm=d, tn=d, tk=1024, ta=True, name=f"conv_out_dw_l{i}")
            dpa, dpg, ddw, ddwb, dlng, dlnb, dba, dbg_ = carried(
                _conf_bwd, (sv["u"], ds, sv["pa"], sv["pg"], sv["taps"], vec(ln_g_f[j]), vec(ln_b_f[j]), seq),
                name=f"conf_bwd_l{i}")
            dw_in = _mm(sv["h"], dpa, m=d, n=d, k=n, tm=d, tn=wcol, tk=1024, ta=True, o_stk=wcol, o_s0=0,
                        o_slots=N_CHIPS, name=f"conv_in_dw_a_l{i}")
            dw_in = _mm(sv["h"], dpg, m=d, n=d, k=n, tm=d, tn=wcol, tk=1024, ta=True, o_stk=wcol, o_s0=2,
                        o_slots=N_CHIPS, o_buf=dw_in, name=f"conv_in_dw_g_l{i}")
            dh = _mm(dpa, w_in, m=n, n=d, k=d, tn=d, tk=wcol, tb=True, b_stk=wcol, b_s0=0, name=f"conv_in_dx_a_l{i}")
            dh = _mm(dpg, w_in, m=n, n=d, k=d, tn=d, tk=wcol, tb=True, b_stk=wcol, b_s0=2, res=dh,
                     name=f"conv_in_dx_g_l{i}")
            dcur, g_norm_mix[i], db_out = _rms_bwd(sv["x_in"], dh, vec(norm_mix[i]), dmid, name=f"rms_mix_bwd_l{i}",
                                                   colsum=True)
            part_small["b_in"][j] = jnp.concatenate([dba, dbg_], axis=1)
            part_small["dw"][j] = ddw[:CONV_WIDTH]
            part_small["dw_b"][j], part_small["ln_g"][j], part_small["ln_b"][j] = ddwb, dlng, dlnb
            part_small["b_out"][j] = db_out
            mix_grads = [dw_in, dw_out.reshape(N_CHIPS, d // N_CHIPS, d)]
        elif kind == 1:
            dcur, g_norm_mix[i], dpw, dpb, dpsc = _pool_bwd(dmid, sv["x_in"], sv["p"], vec(norm_mix[i]), sv["pw"],
                                                            vec(pool_b_f[j]), vec(pool_scale[j]), seq,
                                                            name=f"pool_bwd_l{i}")
            part_small["pool_b"] = dpb
            g_pool_scale = dpsc
            mix_grads = [dpw.reshape(ng, N_CHIPS, gd // N_CHIPS, gd).transpose(1, 0, 2, 3).reshape(N_CHIPS, gd, gd)]
        else:
            do = _mm(dmid, sv["w_o"], m=n, n=d, k=d, tk=d, tb=True, name=f"fox_out_dx_l{i}")
            dw_o = _mm(sv["o"], dmid, m=d, n=d, k=n, tm=d, tn=d, tk=1024, ta=True, name=f"fox_out_dw_l{i}")
            dq, dcc, dk, dv, dck = _flash_bwd(sv["qa"], sv["ka"], sv["v"], do, sv["o"],
                                              _row_layout(sv["lse"], _tile(seq, ATTN_TILE)), bsz, seq,
                                              name=f"fox_attn_bwd_l{i}")
            dqkv, dfl, dqg, dkg, dbf = carried(
                _fox_prep_bwd, (sv["qkv"], dq, dk, dv, _from_col_layout(dcc), _from_col_layout(dck), sv["fl"], sv["bf"],
                                sv["qg"], sv["kg"], seq), name=f"fox_prep_bwd_l{i}")
            dw_qkv = _mm(sv["h"], dqkv, m=d, n=3 * d, k=n, tm=d, tn=d, tk=1024, ta=True, name=f"fox_qkv_dw_l{i}")
            dw_f = _mm(sv["h"], dfl, m=d, n=LANES, k=n, tm=d, tk=1024, ta=True, name=f"fox_fl_dw_l{i}")
            dh = _mm(dqkv, sv["w_qkv"], m=n, n=d, k=3 * d, tn=d, tk=d, tb=True, name=f"fox_qkv_dx_l{i}")
            dh = _mm(dfl, sv["w_f"], m=n, n=d, k=LANES, tn=d, tb=True, res=dh, name=f"fox_fl_dx_l{i}")
            dcur, g_norm_mix[i] = _rms_bwd(sv["x_in"], dh, vec(norm_mix[i]), dmid, name=f"rms_mix_bwd_l{i}")
            g_bf = dbf[:, :nh]
            g_qg = dqg.reshape(nh, HEAD_DIM).sum(axis=0, keepdims=True)
            g_kg = dkg.reshape(nh, HEAD_DIM).sum(axis=0, keepdims=True)
            dw_in_full = jnp.concatenate([dw_qkv, dw_f[:, :nh]], axis=1)
            wshard = dw_in_full.shape[1] // N_CHIPS
            mix_grads = [dw_in_full.reshape(d, N_CHIPS, wshard).transpose(1, 0, 2),
                         dw_o.reshape(N_CHIPS, d // N_CHIPS, d)]
        wire = [BF16] * len(mix_grads)
        if i == 0:
            sm_parts = [jnp.concatenate(part_small["b_in"]), jnp.stack(part_small["dw"]),
                        jnp.concatenate(part_small["dw_b"]), jnp.concatenate(part_small["ln_g"]),
                        jnp.concatenate(part_small["ln_b"]), jnp.concatenate(part_small["b_out"]),
                        part_small["pool_b"].reshape(n_pool, ng, gd), jnp.stack(part_small["ffn_dw"])]
            mix_grads.append(_pack([_to_shards(p_) for p_ in sm_parts], small_rows))
            wire.append(F32)

        def mix_sink(red, i=i, kind=kind, j=j):
            if kind == 0:
                g_conv_in[j], g_conv_out[j] = red[0], red[1]
            elif kind == 1:
                done["pool_w"] = red[0]
            else:
                done["fox_in"], done["fox_o"] = red[0], red[1]
            if i == 0:
                done["small"] = red[-1].reshape(-1)

        pending.append((_rs_begin(mix_grads, wire, c_idx, tag=f"mix_l{i}"), f"mix_l{i}", mix_sink))

    groups, comm = take_pending()
    finish_groups(groups, _run_comm(comm, name="rs_chips_tail"))
    g_pool_w, g_fox_in, g_fox_o, g_small_flat = done["pool_w"], done["fox_in"], done["fox_o"], done["small"]

    grad_x = dcur.reshape(bsz, seq, d)

    rep_parts = [jnp.concatenate(g_norm_mix), jnp.concatenate(g_norm_ffn), g_pool_scale, g_bf, g_qg, g_kg,
                 jnp.concatenate(g_ffn_dw_b)]
    rep_shapes = [norm_mix.shape, norm_ffn.shape, pool_scale.shape, fox_b_f.shape, fox_q_gain.shape,
                  fox_k_gain.shape, ffn_dw_b.shape]
    rep_rows = _pack_rows([math.prod(s) for s in rep_shapes], 8)
    rep = _all_reduce_small(_pack([p_.reshape(1, -1) for p_ in rep_parts], rep_rows)[0], name="all_reduce_small")
    g_rep = _unpack(rep.reshape(-1), rep_shapes)
    g_sm = _unpack(g_small_flat, small_shapes)

    grads = {
        "norm_mix": g_rep[0], "norm_ffn": g_rep[1],
        "conv_w_in": jnp.stack(g_conv_in), "conv_b_in": g_sm[0], "conv_dw": g_sm[1], "conv_dw_b": g_sm[2],
        "conv_ln_g": g_sm[3], "conv_ln_b": g_sm[4], "conv_w_out": jnp.stack(g_conv_out), "conv_b_out": g_sm[5],
        "pool_w": g_pool_w.reshape(pool_w.shape), "pool_b": g_sm[6], "pool_scale": g_rep[2],
        "fox_w_in": g_fox_in.reshape(fox_w_in.shape), "fox_b_f": g_rep[3], "fox_q_gain": g_rep[4],
        "fox_k_gain": g_rep[5], "fox_w_o": g_fox_o.reshape(fox_w_o.shape),
        "ffn_w_up": jnp.stack(g_up), "ffn_dw": g_sm[7], "ffn_dw_b": g_rep[6], "ffn_w_down": jnp.stack(g_down),
    }
    weights = dict(norm_mix=norm_mix, norm_ffn=norm_ffn, conv_w_in=conv_w_in, conv_b_in=conv_b_in, conv_dw=conv_dw,
                   conv_dw_b=conv_dw_b, conv_ln_g=conv_ln_g, conv_ln_b=conv_ln_b, conv_w_out=conv_w_out,
                   conv_b_out=conv_b_out, pool_w=pool_w, pool_b=pool_b, pool_scale=pool_scale, fox_w_in=fox_w_in,
                   fox_b_f=fox_b_f, fox_q_gain=fox_q_gain, fox_k_gain=fox_k_gain, fox_w_o=fox_w_o, ffn_w_up=ffn_w_up,
                   ffn_dw=ffn_dw, ffn_dw_b=ffn_dw_b, ffn_w_down=ffn_w_down)
    m_in = dict(norm_mix=m_norm_mix, norm_ffn=m_norm_ffn, conv_w_in=m_conv_w_in, conv_b_in=m_conv_b_in,
                conv_dw=m_conv_dw, conv_dw_b=m_conv_dw_b, conv_ln_g=m_conv_ln_g, conv_ln_b=m_conv_ln_b,
                conv_w_out=m_conv_w_out, conv_b_out=m_conv_b_out, pool_w=m_pool_w, pool_b=m_pool_b,
                pool_scale=m_pool_scale, fox_w_in=m_fox_w_in, fox_b_f=m_fox_b_f, fox_q_gain=m_fox_q_gain,
                fox_k_gain=m_fox_k_gain, fox_w_o=m_fox_w_o, ffn_w_up=m_ffn_w_up, ffn_dw=m_ffn_dw,
                ffn_dw_b=m_ffn_dw_b, ffn_w_down=m_ffn_w_down)
    v_in = dict(norm_mix=v_norm_mix, norm_ffn=v_norm_ffn, conv_w_in=v_conv_w_in, conv_b_in=v_conv_b_in,
                conv_dw=v_conv_dw, conv_dw_b=v_conv_dw_b, conv_ln_g=v_conv_ln_g, conv_ln_b=v_conv_ln_b,
                conv_w_out=v_conv_w_out, conv_b_out=v_conv_b_out, pool_w=v_pool_w, pool_b=v_pool_b,
                pool_scale=v_pool_scale, fox_w_in=v_fox_w_in, fox_b_f=v_fox_b_f, fox_q_gain=v_fox_q_gain,
                fox_k_gain=v_fox_k_gain, fox_w_o=v_fox_w_o, ffn_w_up=v_ffn_w_up, ffn_dw=v_ffn_dw,
                ffn_dw_b=v_ffn_dw_b, ffn_w_down=v_ffn_w_down)
    names = list(weights)
    g_out, d_out, m_out, v_out = [], [], [], []
    for nm in names:
        g_, dl_, m_, v_ = _adamw_nd(weights[nm], grads[nm].reshape(weights[nm].shape), m_in[nm], v_in[nm],
                                    name=f"adamw_{nm}")
        g_out.append(g_)
        d_out.append(dl_)
        m_out.append(m_)
        v_out.append(v_)
    return (loss, grad_x, *g_out, *d_out, *m_out, *v_out)
```

```python
import math

import jax
import jax.numpy as jnp
import numpy as np
from jax import lax
from jax.experimental import pallas as pl
from jax.experimental.pallas import tpu as pltpu

F32 = jnp.float32
BF16 = jnp.bfloat16
HI = lax.Precision.HIGHEST
MESH = pl.DeviceIdType.MESH
ANY = pl.BlockSpec(memory_space=pl.ANY)

EPS = 1e-6
HEAD_DIM = 64
LANES = 128
POOL_WINDOWS = (2, 4, 8, 16)
CONV_WIDTH = 31
CONV_HALO = 32
FFN_HALO = 8
FFN_ROWS, FFN_COLS = 256, 1408
POOL_HALO = 16
N_CHIPS = 4
NEG = -1e30

ADAM_LR = 0.001
ADAM_B1 = 0.9
ADAM_B2 = 0.999
ADAM_EPS = 1e-08
ADAM_WD = 0.01
ADAM_STEP = 10

V7X_VMEM_LIMIT_BYTES = 56 * 1024 * 1024


def _cp(*sem):
    return pltpu.CompilerParams(dimension_semantics=sem or None, vmem_limit_bytes=V7X_VMEM_LIMIT_BYTES)


def _tile(n, pref):
    t = min(n, pref)
    assert n % t == 0, (n, pref)
    return t


def _sig(v):
    return jax.nn.sigmoid(v)


def _roll(v, shift):
    n = v.shape[0]
    shift = shift % n
    return v if shift == 0 else pltpu.roll(v, shift, 0)


class _Comm:
    def __init__(self, ins, out_shapes, n_sems, start, wait):
        self.ins, self.out_shapes, self.n_sems, self.start, self.wait = list(ins), list(out_shapes), n_sems, start, wait


def _hosted(body, comm, *, grid, in_specs, out_specs, out_shape, scratch_shapes, sem, name, ins):
    if comm is None:
        return pl.pallas_call(body, grid=grid, in_specs=in_specs, out_specs=out_specs, out_shape=out_shape,
                              scratch_shapes=scratch_shapes, name=name, compiler_params=_cp(*sem))(*ins)
    n_in, n_out, n_scr = len(in_specs), len(out_specs), len(scratch_shapes)
    nci, nco = len(comm.ins), len(comm.out_shapes)

    def wrapped(*refs):
        pos = [0]

        def take(cnt):
            pos[0] += cnt
            return refs[pos[0] - cnt:pos[0]]

        r_in, c_in, r_out, c_out, r_scr = take(n_in), take(nci), take(n_out), take(nco), take(n_scr)
        send_sems, recv_sems = take(2)
        ids = [pl.program_id(ax) for ax in range(len(grid))]
        first, last = ids[0] == 0, ids[0] == grid[0] - 1
        for ax in range(1, len(grid)):
            first = jnp.logical_and(first, ids[ax] == 0)
            last = jnp.logical_and(last, ids[ax] == grid[ax] - 1)

        @pl.when(first)
        def _():
            comm.start(c_in, c_out, send_sems, recv_sems)

        body(*r_in, *r_out, *r_scr)

        @pl.when(last)
        def _():
            comm.wait(c_in, c_out, send_sems, recv_sems)

    outs = pl.pallas_call(
        wrapped, grid=grid, in_specs=list(in_specs) + [ANY] * nci, out_specs=list(out_specs) + [ANY] * nco,
        out_shape=list(out_shape) + comm.out_shapes,
        scratch_shapes=list(scratch_shapes) + [pltpu.SemaphoreType.DMA((comm.n_sems,))] * 2, name=name,
        compiler_params=pltpu.CompilerParams(dimension_semantics=sem, vmem_limit_bytes=V7X_VMEM_LIMIT_BYTES,
                                             has_side_effects=True),
    )(*ins, *comm.ins)
    return list(outs[:n_out]), list(outs[n_out:])


def _run_comm(comm, name):
    def body(*refs):
        nci, nco = len(comm.ins), len(comm.out_shapes)
        c_in, c_out, send_sems, recv_sems = refs[:nci], refs[nci:nci + nco], refs[-2], refs[-1]
        comm.start(c_in, c_out, send_sems, recv_sems)
        comm.wait(c_in, c_out, send_sems, recv_sems)

    return pl.pallas_call(
        body, in_specs=[ANY] * len(comm.ins), out_specs=[ANY] * len(comm.out_shapes), out_shape=comm.out_shapes,
        scratch_shapes=[pltpu.SemaphoreType.DMA((comm.n_sems,))] * 2, name=name,
        compiler_params=pltpu.CompilerParams(has_side_effects=True),
    )(*comm.ins)


def _mm(a, b, *, m, n, k, name, tm=512, tn=512, tk=512, ta=False, tb=False, b_stk=None, b_s0=0,
        o_stk=None, o_s0=0, o_slots=None, o_buf=None, bias=None, res=None, out_dtype=F32):
    tm, tn, tk = _tile(m, tm), _tile(n, tn), _tile(k, tk)
    gi, gj, gk = m // tm, n // tn, k // tk
    a_spec = pl.BlockSpec((tk, tm), lambda j, i, kk: (kk, i)) if ta else pl.BlockSpec((tm, tk), lambda j, i, kk: (i, kk))
    if b_stk is None:
        b_spec = pl.BlockSpec((tn, tk), lambda j, i, kk: (j, kk)) if tb else pl.BlockSpec((tk, tn), lambda j, i, kk: (kk, j))
    elif tb:
        assert b_stk % tk == 0
        per = b_stk // tk
        b_spec = pl.BlockSpec((None, tn, tk), lambda j, i, kk: (b_s0 + kk // per, j, kk % per))
    else:
        assert b_stk % tn == 0
        per = b_stk // tn
        b_spec = pl.BlockSpec((None, tk, tn), lambda j, i, kk: (b_s0 + j // per, kk, j % per))
    ins, in_specs = [a, b], [a_spec, b_spec]
    if bias is not None:
        ins.append(bias)
        in_specs.append(pl.BlockSpec((1, tn), lambda j, i, kk: (0, j)))
    if res is not None:
        ins.append(res)
        in_specs.append(pl.BlockSpec((tm, tn), lambda j, i, kk: (i, j)))
    aliases = {}
    if o_stk is None:
        out_shape = jax.ShapeDtypeStruct((m, n), out_dtype)
        o_spec = pl.BlockSpec((tm, tn), lambda j, i, kk: (i, j))
    else:
        assert o_stk % tn == 0
        pero = o_stk // tn
        out_shape = jax.ShapeDtypeStruct((o_slots, m, o_stk), out_dtype)
        o_spec = pl.BlockSpec((None, tm, tn), lambda j, i, kk: (o_s0 + j // pero, i, j % pero))
        if o_buf is not None:
            aliases = {len(ins): 0}
            ins.append(o_buf)
            in_specs.append(ANY)
    has_bias, has_res, has_buf = bias is not None, res is not None, o_buf is not None
    dn = (((0 if ta else 1,), (1 if tb else 0,)), ((), ()))

    def body(*refs):
        a_ref, b_ref = refs[0], refs[1]
        pos = 2
        bias_ref = refs[pos] if has_bias else None
        pos += has_bias
        res_ref = refs[pos] if has_res else None
        pos += has_res + has_buf
        o_ref = refs[pos]
        p = lax.dot_general(a_ref[...].astype(BF16), b_ref[...].astype(BF16), dn, preferred_element_type=F32)

        def finish(acc):
            if has_bias:
                acc = acc + bias_ref[...]
            if has_res:
                acc = acc + res_ref[...]
            o_ref[...] = acc.astype(o_ref.dtype)

        if gk == 1:
            finish(p)
        else:
            acc_ref = refs[pos + 1]
            kk = pl.program_id(2)

            @pl.when(kk == 0)
            def _():
                acc_ref[...] = p

            @pl.when(kk > 0)
            def _():
                acc_ref[...] += p

            @pl.when(kk == gk - 1)
            def _():
                finish(acc_ref[...])

    return pl.pallas_call(
        body, grid=(gj, gi, gk), in_specs=in_specs, out_specs=o_spec, out_shape=out_shape,
        scratch_shapes=[pltpu.VMEM((tm, tn), F32)] if gk > 1 else [],
        input_output_aliases=aliases, name=name,
        compiler_params=_cp("parallel", "parallel", "arbitrary"),
    )(*ins)


def _rms_fwd(x, g, name):
    n, d = x.shape
    tm = _tile(n, 512)

    def body(x_ref, g_ref, h_ref):
        xv = x_ref[...]
        r = lax.rsqrt(jnp.mean(xv * xv, axis=-1, keepdims=True) + EPS)
        h_ref[...] = (xv * r * g_ref[...]).astype(h_ref.dtype)

    return pl.pallas_call(
        body, grid=(n // tm,),
        in_specs=[pl.BlockSpec((tm, d), lambda i: (i, 0)), pl.BlockSpec((1, d), lambda i: (0, 0))],
        out_specs=pl.BlockSpec((tm, d), lambda i: (i, 0)),
        out_shape=jax.ShapeDtypeStruct((n, d), BF16), name=name, compiler_params=_cp("arbitrary"),
    )(x, g)


def _rms_bwd(x, dh, g, dres, name, colsum=False):
    n, d = x.shape
    tm = _tile(n, 512)

    def body(x_ref, dh_ref, g_ref, dres_ref, dx_ref, dg_ref, *rest):
        i = pl.program_id(0)
        xv, dhv = x_ref[...], dh_ref[...]
        r = lax.rsqrt(jnp.mean(xv * xv, axis=-1, keepdims=True) + EPS)
        xn = xv * r
        dxn = dhv * g_ref[...]
        dx_ref[...] = dres_ref[...] + r * (dxn - xn * jnp.mean(dxn * xn, axis=-1, keepdims=True))
        dg = jnp.sum(dhv * xn, axis=0, keepdims=True)

        @pl.when(i == 0)
        def _():
            dg_ref[...] = jnp.zeros_like(dg_ref)
            if colsum:
                rest[0][...] = jnp.zeros_like(rest[0])

        dg_ref[...] += dg
        if colsum:
            rest[0][...] += jnp.sum(dres_ref[...], axis=0, keepdims=True)

    row = pl.BlockSpec((tm, d), lambda i: (i, 0))
    vec = pl.BlockSpec((1, d), lambda i: (0, 0))
    out_shape = [jax.ShapeDtypeStruct((n, d), F32), jax.ShapeDtypeStruct((1, d), F32)]
    out_specs = [row, vec]
    if colsum:
        out_shape.append(jax.ShapeDtypeStruct((1, d), F32))
        out_specs.append(vec)
    return pl.pallas_call(
        body, grid=(n // tm,), in_specs=[row, row, vec, row], out_specs=out_specs, out_shape=out_shape,
        name=name, compiler_params=_cp("arbitrary"),
    )(x, dh, g, dres)


def _loss(y, tgt, name):
    n, d = y.shape
    tm = _tile(n, 512)

    def body(y_ref, t_ref, dy_ref, l_ref):
        i = pl.program_id(0)
        e = y_ref[...] - t_ref[...]
        dy_ref[...] = e / d
        part = 0.5 * jnp.sum(jnp.mean(e * e, axis=-1, keepdims=True), axis=0, keepdims=True)

        @pl.when(i == 0)
        def _():
            l_ref[...] = jnp.zeros_like(l_ref)

        l_ref[...] += part

    row = pl.BlockSpec((tm, d), lambda i: (i, 0))
    return pl.pallas_call(
        body, grid=(n // tm,), in_specs=[row, row],
        out_specs=[row, pl.BlockSpec((1, 1), lambda i: (0, 0))],
        out_shape=[jax.ShapeDtypeStruct((n, d), F32), jax.ShapeDtypeStruct((1, 1), F32)],
        name=name, compiler_params=_cp("arbitrary"),
    )(y, tgt)


def _ffn_specs(n, f, tm, tc, seq):
    hb = FFN_HALO
    cur = pl.BlockSpec((tm, tc), lambda j, i: (i, j))
    prev = pl.BlockSpec((hb, tc), lambda j, i: (jnp.maximum(i * (tm // hb) - 1, 0), j))
    nxt = pl.BlockSpec((hb, tc), lambda j, i: (jnp.minimum((i + 1) * (tm // hb), n // hb - 1), j))
    taps = pl.BlockSpec((3, tc), lambda j, i: (0, j))
    vec = pl.BlockSpec((1, tc), lambda j, i: (0, j))
    return cur, prev, nxt, taps, vec


def _ffn_glu_fwd(uv, ug, wv, wg, bv, bg, seq, name, comm=None):
    n, f = uv.shape
    tm, tc = _tile(seq, FFN_ROWS), _tile(f, FFN_COLS)
    tps = seq // tm
    cur, prev, _, taps, vec = _ffn_specs(n, f, tm, tc, seq)

    def body(uvp, uvc, ugp, ugc, wv_ref, wg_ref, bv_ref, bg_ref, a_ref):
        first = (pl.program_id(1) % tps) == 0

        def conv(p_ref, c_ref, w_ref, b_ref):
            xs = jnp.concatenate([jnp.where(first, 0.0, p_ref[...]), c_ref[...]], axis=0)
            w = w_ref[...]
            y = w[2:3] * xs + w[1:2] * _roll(xs, 1) + w[0:1] * _roll(xs, 2)
            return y[FFN_HALO:] + b_ref[...]

        val = conv(uvp, uvc, wv_ref, bv_ref)
        gate = conv(ugp, ugc, wg_ref, bg_ref)
        a_ref[...] = (gate * _sig(gate) * val).astype(a_ref.dtype)

    return _hosted(
        body, comm, grid=(f // tc, n // tm), in_specs=[prev, cur, prev, cur, taps, taps, vec, vec],
        out_specs=[cur], out_shape=[jax.ShapeDtypeStruct((n, f), BF16)], scratch_shapes=[],
        sem=("parallel", "arbitrary"), name=name, ins=(uv, uv, ug, ug, wv, wg, bv, bg))


def _ffn_glu_bwd(uv, ug, da, wv, wg, bv, bg, seq, name, comm=None):
    n, f = uv.shape
    tm, tc = _tile(seq, FFN_ROWS), _tile(f, FFN_COLS)
    tps = seq // tm
    hb = FFN_HALO
    ext = tm + hb
    cur, prev, nxt, taps, vec = _ffn_specs(n, f, tm, tc, seq)

    def body(uvp, uvc, uvn, ugp, ugc, ugn, da_c, da_n, wv_ref, wg_ref, bv_ref, bg_ref,
             duv_ref, dug_ref, dwv_ref, dwg_ref, dbv_ref, dbg_ref):
        i = pl.program_id(1)
        first = (i % tps) == 0
        last = (i % tps) == tps - 1
        da_e = jnp.concatenate([da_c[...], jnp.where(last, 0.0, da_n[...])], axis=0)

        def taps3(p_ref, c_ref, n_ref):
            xs = jnp.concatenate([jnp.where(first, 0.0, p_ref[...]), c_ref[...], n_ref[...]], axis=0)
            return xs, _roll(xs, 1), _roll(xs, 2)

        xv, xg = taps3(uvp, uvc, uvn), taps3(ugp, ugc, ugn)
        wv_, wg_ = wv_ref[...], wg_ref[...]

        def conv(xs, w, b_ref):
            return (w[2:3] * xs[0] + w[1:2] * xs[1] + w[0:1] * xs[2])[hb:] + b_ref[...]

        val, gate = conv(xv, wv_, bv_ref), conv(xg, wg_, bg_ref)
        sg = _sig(gate)
        dval = da_e * (gate * sg)
        dgate = da_e * val * (sg * (1.0 + gate * (1.0 - sg)))

        def conv_t(dv, w):
            return (w[2:3] * dv + w[1:2] * _roll(dv, ext - 1) + w[0:1] * _roll(dv, ext - 2))[:tm]

        duv_ref[...] = conv_t(dval, wv_).astype(duv_ref.dtype)
        dug_ref[...] = conv_t(dgate, wg_).astype(dug_ref.dtype)

        def tap_grads(d_own, xs):
            return jnp.concatenate(
                [jnp.sum(d_own * xs[2 - kk][hb:hb + tm], axis=0, keepdims=True) for kk in range(3)], axis=0)

        dv_own, dg_own = dval[:tm], dgate[:tm]

        @pl.when(i == 0)
        def _():
            for r in (dwv_ref, dwg_ref, dbv_ref, dbg_ref):
                r[...] = jnp.zeros_like(r)

        dwv_ref[...] += tap_grads(dv_own, xv)
        dwg_ref[...] += tap_grads(dg_own, xg)
        dbv_ref[...] += jnp.sum(dv_own, axis=0, keepdims=True)
        dbg_ref[...] += jnp.sum(dg_own, axis=0, keepdims=True)

    return _hosted(
        body, comm, grid=(f // tc, n // tm),
        in_specs=[prev, cur, nxt, prev, cur, nxt, cur, nxt, taps, taps, vec, vec],
        out_specs=[cur, cur, taps, taps, vec, vec],
        out_shape=[jax.ShapeDtypeStruct((n, f), BF16), jax.ShapeDtypeStruct((n, f), BF16),
                   jax.ShapeDtypeStruct((3, f), F32), jax.ShapeDtypeStruct((3, f), F32),
                   jax.ShapeDtypeStruct((1, f), F32), jax.ShapeDtypeStruct((1, f), F32)],
        scratch_shapes=[], sem=("parallel", "arbitrary"), name=name,
        ins=(uv, uv, uv, ug, ug, ug, da, da, wv, wg, bv, bg))


def _conf_specs(n, d, tm):
    hb = CONV_HALO
    cur = pl.BlockSpec((tm, d), lambda i: (i, 0))
    prev = pl.BlockSpec((hb, d), lambda i: (jnp.maximum(i * (tm // hb) - 1, 0), 0))
    nxt = pl.BlockSpec((hb, d), lambda i: (jnp.minimum((i + 1) * (tm // hb), n // hb - 1), 0))
    taps = pl.BlockSpec((CONV_HALO, d), lambda i: (0, 0))
    vec = pl.BlockSpec((1, d), lambda i: (0, 0))
    return cur, prev, nxt, taps, vec


def _conf_fwd(pa, pg, w, wb, lng, lnb, seq, name):
    n, d = pa.shape
    tm = _tile(seq, 256)
    tps = seq // tm
    hb = CONV_HALO
    cur, prev, _, taps, vec = _conf_specs(n, d, tm)

    def body(pap, pac, pgp, pgc, w_ref, wb_ref, lng_ref, lnb_ref, u_ref, s_ref):
        first = (pl.program_id(0) % tps) == 0
        a = jnp.concatenate([jnp.where(first, 0.0, pap[...]), pac[...]], axis=0)
        g = jnp.concatenate([jnp.where(first, 0.0, pgp[...]), pgc[...]], axis=0)
        z = a * _sig(g)
        acc = w_ref[CONV_WIDTH - 1:CONV_WIDTH, :] * z
        for sh in range(1, CONV_WIDTH):
            acc = acc + w_ref[CONV_WIDTH - 1 - sh:CONV_WIDTH - sh, :] * _roll(z, sh)
        u = acc[hb:] + wb_ref[...]
        mu = jnp.mean(u, axis=-1, keepdims=True)
        uc = u - mu
        var = jnp.mean(uc * uc, axis=-1, keepdims=True)
        ul = uc * lax.rsqrt(var + EPS) * lng_ref[...] + lnb_ref[...]
        u_ref[...] = u
        s_ref[...] = (ul * _sig(ul)).astype(s_ref.dtype)

    return pl.pallas_call(
        body, grid=(n // tm,), in_specs=[prev, cur, prev, cur, taps, vec, vec, vec],
        out_specs=[cur, cur],
        out_shape=[jax.ShapeDtypeStruct((n, d), F32), jax.ShapeDtypeStruct((n, d), BF16)],
        name=name, compiler_params=_cp("arbitrary"),
    )(pa, pa, pg, pg, w, wb, lng, lnb)


def _conf_bwd(u, ds, pa, pg, w, lng, lnb, seq, name, comm=None):
    n, d = u.shape
    tm = _tile(seq, 256)
    tps = seq // tm
    hb = CONV_HALO
    ext = tm + hb
    cur, prev, nxt, taps, vec = _conf_specs(n, d, tm)

    def body(uc_ref, un_ref, dsc_ref, dsn_ref, pap, pac, pgp, pgc, w_ref, lng_ref, lnb_ref,
             dpa_ref, dpg_ref, dw_ref, dwb_ref, dlng_ref, dlnb_ref, dba_ref, dbg_ref):
        i = pl.program_id(0)
        first = (i % tps) == 0
        last = (i % tps) == tps - 1

        @pl.when(i == 0)
        def _():
            for r in (dw_ref, dwb_ref, dlng_ref, dlnb_ref, dba_ref, dbg_ref):
                r[...] = jnp.zeros_like(r)

        ue = jnp.concatenate([uc_ref[...], un_ref[...]], axis=0)
        dse = jnp.concatenate([dsc_ref[...], jnp.where(last, 0.0, dsn_ref[...])], axis=0)
        mu = jnp.mean(ue, axis=-1, keepdims=True)
        cen = ue - mu
        r = lax.rsqrt(jnp.mean(cen * cen, axis=-1, keepdims=True) + EPS)
        xn = cen * r
        ul = xn * lng_ref[...] + lnb_ref[...]
        sg = _sig(ul)
        dul = dse * (sg * (1.0 + ul * (1.0 - sg)))
        dun = dul * lng_ref[...]
        du = r * (dun - jnp.mean(dun, axis=-1, keepdims=True) - xn * jnp.mean(dun * xn, axis=-1, keepdims=True))
        dlng_ref[...] += jnp.sum((dul * xn)[:tm], axis=0, keepdims=True)
        dlnb_ref[...] += jnp.sum(dul[:tm], axis=0, keepdims=True)
        du_own = du[:tm]
        dwb_ref[...] += jnp.sum(du_own, axis=0, keepdims=True)

        dz = w_ref[CONV_WIDTH - 1:CONV_WIDTH, :] * du
        for sh in range(1, CONV_WIDTH):
            dz = dz + w_ref[CONV_WIDTH - 1 - sh:CONV_WIDTH - sh, :] * _roll(du, ext - sh)
        dz = dz[:tm]

        a = jnp.concatenate([jnp.where(first, 0.0, pap[...]), pac[...]], axis=0)
        g = jnp.concatenate([jnp.where(first, 0.0, pgp[...]), pgc[...]], axis=0)
        sgg = _sig(g)
        z = a * sgg
        for sh in range(CONV_WIDTH):
            kk = CONV_WIDTH - 1 - sh
            dw_ref[kk:kk + 1, :] += jnp.sum(du_own * _roll(z, sh)[hb:], axis=0, keepdims=True)

        a_c, sg_c = a[hb:], sgg[hb:]
        da = dz * sg_c
        dg = dz * a_c * sg_c * (1.0 - sg_c)
        dpa_ref[...] = da.astype(dpa_ref.dtype)
        dpg_ref[...] = dg.astype(dpg_ref.dtype)
        dba_ref[...] += jnp.sum(da, axis=0, keepdims=True)
        dbg_ref[...] += jnp.sum(dg, axis=0, keepdims=True)

    vshape = jax.ShapeDtypeStruct((1, d), F32)
    return _hosted(
        body, comm, grid=(n // tm,),
        in_specs=[cur, nxt, cur, nxt, prev, cur, prev, cur, taps, vec, vec],
        out_specs=[cur, cur, taps, vec, vec, vec, vec, vec],
        out_shape=[jax.ShapeDtypeStruct((n, d), BF16), jax.ShapeDtypeStruct((n, d), BF16),
                   jax.ShapeDtypeStruct((CONV_HALO, d), F32), vshape, vshape, vshape, vshape, vshape],
        scratch_shapes=[], sem=("arbitrary",), name=name, ins=(u, u, ds, ds, pa, pa, pg, pg, w, lng, lnb))


def _pool_specs(n, d, tm, gd):
    hb = POOL_HALO
    cur = pl.BlockSpec((tm, d), lambda i: (i, 0))
    prev = pl.BlockSpec((hb, d), lambda i: (jnp.maximum(i * (tm // hb) - 1, 0), 0))
    nxt = pl.BlockSpec((hb, d), lambda i: (jnp.minimum((i + 1) * (tm // hb), n // hb - 1), 0))
    wsp = pl.BlockSpec((len(POOL_WINDOWS), gd, gd), lambda i: (0, 0, 0))
    vec = pl.BlockSpec((1, d), lambda i: (0, 0))
    return cur, prev, nxt, wsp, vec


def _pool_fwd(x, g, w, b, sc, seq, name):
    n, d = x.shape
    gd = d // len(POOL_WINDOWS)
    tm = _tile(seq, 256)
    tps = seq // tm
    hb = POOL_HALO
    cur, prev, _, wsp, vec = _pool_specs(n, d, tm, gd)

    def body(xp, xc, g_ref, w_ref, b_ref, sc_ref, x1_ref, p_ref):
        i = pl.program_id(0)
        first = (i % tps) == 0
        xe = jnp.concatenate([jnp.where(first, 0.0, xp[...]), xc[...]], axis=0)
        r = lax.rsqrt(jnp.mean(xe * xe, axis=-1, keepdims=True) + EPS)
        h = xe * r * g_ref[...]
        t = ((i % tps) * tm + lax.broadcasted_iota(jnp.int32, (tm, 1), 0) + 1).astype(F32)
        ys = []
        for gi, win in enumerate(POOL_WINDOWS):
            hg = h[:, gi * gd:(gi + 1) * gd]
            s, sh = hg, 1
            while sh < win:
                s = s + _roll(s, sh)
                sh *= 2
            p = (s[hb:] / jnp.minimum(t, float(win)) - hg[hb:]).astype(BF16)
            p_ref[:, gi * gd:(gi + 1) * gd] = p
            ys.append(jnp.dot(p, w_ref[gi], preferred_element_type=F32))
        y = jnp.concatenate(ys, axis=1) + b_ref[...]
        x1_ref[...] = xc[...] + y * sc_ref[...]

    return pl.pallas_call(
        body, grid=(n // tm,), in_specs=[prev, cur, vec, wsp, vec, vec], out_specs=[cur, cur],
        out_shape=[jax.ShapeDtypeStruct((n, d), F32), jax.ShapeDtypeStruct((n, d), BF16)],
        name=name, compiler_params=_cp("arbitrary"),
    )(x, x, g, w, b, sc)


def _pool_bwd(dx1, x, p, g, w, b, sc, seq, name):
    n, d = x.shape
    ng = len(POOL_WINDOWS)
    gd = d // ng
    tm = _tile(seq, 256)
    tps = seq // tm
    hb = POOL_HALO
    ext = tm + hb
    cur, _, nxt, wsp, vec = _pool_specs(n, d, tm, gd)

    def body(dc_ref, dn_ref, x_ref, p_ref, g_ref, w_ref, b_ref, sc_ref, dx_ref, dg_ref, dw_ref, db_ref, dsc_ref):
        i = pl.program_id(0)
        last = (i % tps) == tps - 1

        @pl.when(i == 0)
        def _():
            for r_ in (dg_ref, dw_ref, db_ref, dsc_ref):
                r_[...] = jnp.zeros_like(r_)

        dxc = dc_ref[...]
        dxe = jnp.concatenate([dxc, jnp.where(last, 0.0, dn_ref[...])], axis=0)
        dyg = dxe * sc_ref[...]
        t = ((i % tps) * tm + lax.broadcasted_iota(jnp.int32, (ext, 1), 0) + 1).astype(F32)
        dhs = []
        for gi, win in enumerate(POOL_WINDOWS):
            sl = slice(gi * gd, (gi + 1) * gd)
            dyb = dyg[:, sl].astype(BF16)
            wg = w_ref[gi]
            dp = lax.dot_general(dyb, wg, (((1,), (1,)), ((), ())), preferred_element_type=F32)
            s, sh = dp / jnp.minimum(t, float(win)), 1
            while sh < win:
                s = s + _roll(s, ext - sh)
                sh *= 2
            dhs.append((s - dp)[:tm])
            pg = p_ref[:, sl]
            dw_ref[gi] += lax.dot_general(pg, dyb[:tm], (((0,), (0,)), ((), ())), preferred_element_type=F32)
            ypre = jnp.dot(pg, wg, preferred_element_type=F32) + b_ref[:, sl]
            dsc_ref[:, sl] += jnp.sum(dxc[:, sl] * ypre, axis=0, keepdims=True)
            db_ref[:, sl] += jnp.sum(dyg[:tm, sl], axis=0, keepdims=True)
        dh = jnp.concatenate(dhs, axis=1)
        xv = x_ref[...]
        r = lax.rsqrt(jnp.mean(xv * xv, axis=-1, keepdims=True) + EPS)
        xn = xv * r
        dxn = dh * g_ref[...]
        dx_ref[...] = dxc + r * (dxn - xn * jnp.mean(dxn * xn, axis=-1, keepdims=True))
        dg_ref[...] += jnp.sum(dh * xn, axis=0, keepdims=True)

    vshape = jax.ShapeDtypeStruct((1, d), F32)
    return pl.pallas_call(
        body, grid=(n // tm,), in_specs=[cur, nxt, cur, cur, vec, wsp, vec, vec],
        out_specs=[cur, vec, wsp, vec, vec],
        out_shape=[jax.ShapeDtypeStruct((n, d), F32), vshape, jax.ShapeDtypeStruct((ng, gd, gd), F32), vshape, vshape],
        name=name, compiler_params=_cp("arbitrary"),
    )(dx1, dx1, x, p, g, w, b, sc)


def _head_maps(d):
    hd = lax.broadcasted_iota(jnp.int32, (d, LANES), 0) // HEAD_DIM
    col = lax.broadcasted_iota(jnp.int32, (d, LANES), 1)
    gm = (hd == col).astype(F32)
    hd_t = lax.broadcasted_iota(jnp.int32, (LANES, d), 1) // HEAD_DIM
    row = lax.broadcasted_iota(jnp.int32, (LANES, d), 0)
    gt = (hd_t == row).astype(F32)
    return gm, gt


def _bias_placement(nh):
    pq = np.zeros((3 * LANES, nh * HEAD_DIM), np.float32)
    pk = np.zeros((3 * LANES, nh * HEAD_DIM), np.float32)
    oq = np.zeros((1, nh * HEAD_DIM), np.float32)
    ok = np.zeros((1, nh * HEAD_DIM), np.float32)
    for h in range(nh):
        for piece in range(3):
            pq[piece * LANES + h, h * HEAD_DIM + piece] = 1.0
            pk[piece * LANES + h, h * HEAD_DIM + 3 + piece] = -1.0
            oq[0, h * HEAD_DIM + 3 + piece] = 1.0
            ok[0, h * HEAD_DIM + piece] = 1.0
    return jnp.asarray(pq, BF16), jnp.asarray(pk, BF16), jnp.asarray(oq), jnp.asarray(ok)


def _fox_prep_fwd(qkv, fl, bf, qg, kg, seq, name):
    n, d3 = qkv.shape
    d = d3 // 3
    nh = d // HEAD_DIM
    tm = _tile(seq, 256)
    tps = seq // tm
    scale = 1.0 / math.sqrt(HEAD_DIM)
    pq, pk, oq, ok = _bias_placement(nh)

    def body(qkv_ref, fl_ref, bf_ref, qg_ref, kg_ref, pq_ref, pk_ref, oq_ref, ok_ref, q_ref, k_ref, v_ref, carry):
        first = (pl.program_id(0) % tps) == 0
        gm, gt = _head_maps(d)

        def head_norm(xr, gain):
            ss = jnp.dot(xr * xr, gm, precision=HI, preferred_element_type=F32)
            r = lax.rsqrt(ss / HEAD_DIM + EPS)
            return xr * jnp.dot(r, gt, precision=HI, preferred_element_type=F32) * gain

        qs = (head_norm(qkv_ref[:, :d], qg_ref[...]).astype(BF16).astype(F32) * scale).astype(BF16)
        kn = head_norm(qkv_ref[:, d:2 * d], kg_ref[...]).astype(BF16)
        v_ref[...] = qkv_ref[:, 2 * d:].astype(BF16)
        z = fl_ref[...] + bf_ref[...]
        logf = jnp.minimum(z, 0.0) - jnp.log1p(jnp.exp(-jnp.abs(z)))
        tri = (lax.broadcasted_iota(jnp.int32, (tm, tm), 0) >= lax.broadcasted_iota(jnp.int32, (tm, tm), 1)).astype(F32)

        @pl.when(first)
        def _():
            carry[...] = jnp.zeros_like(carry)

        c = jnp.dot(tri, logf, precision=HI, preferred_element_type=F32) + carry[...]
        carry[...] = c[tm - 1:tm, :]
        c1 = c.astype(BF16)
        r1 = c - c1.astype(F32)
        c2 = r1.astype(BF16)
        c3 = (r1 - c2.astype(F32)).astype(BF16)
        pieces = jnp.concatenate([c1, c2, c3], axis=1)
        eq = (jnp.dot(pieces, pq_ref[...], preferred_element_type=F32) + oq_ref[...]).astype(BF16)
        ek = (jnp.dot(pieces, pk_ref[...], preferred_element_type=F32) + ok_ref[...]).astype(BF16)
        for h in range(nh):
            lo, hi = h * HEAD_DIM, (h + 1) * HEAD_DIM
            q_ref[:, 2 * lo:2 * lo + HEAD_DIM] = qs[:, lo:hi]
            q_ref[:, 2 * lo + HEAD_DIM:2 * hi] = eq[:, lo:hi]
            k_ref[:, 2 * lo:2 * lo + HEAD_DIM] = kn[:, lo:hi]
            k_ref[:, 2 * lo + HEAD_DIM:2 * hi] = ek[:, lo:hi]

    row = lambda w: pl.BlockSpec((tm, w), lambda i: (i, 0))
    vec = lambda w: pl.BlockSpec((1, w), lambda i: (0, 0))
    full = lambda a: pl.BlockSpec(a.shape, lambda i: (0, 0))
    return pl.pallas_call(
        body, grid=(n // tm,),
        in_specs=[row(d3), row(LANES), vec(LANES), vec(d), vec(d), full(pq), full(pk), full(oq), full(ok)],
        out_specs=[row(2 * d), row(2 * d), row(d)],
        out_shape=[jax.ShapeDtypeStruct((n, 2 * d), BF16)] * 2 + [jax.ShapeDtypeStruct((n, d), BF16)],
        scratch_shapes=[pltpu.VMEM((1, LANES), F32)], name=name, compiler_params=_cp("arbitrary"),
    )(qkv, fl, bf, qg, kg, pq, pk, oq, ok)


def _fox_prep_bwd(qkv, dq, dk, dv, dc1, dc2, fl, bf, qg, kg, seq, name, comm=None):
    n, d3 = qkv.shape
    d = d3 // 3
    tm = _tile(seq, 256)
    tps = seq // tm
    nt = n // tm

    def body(qkv_ref, dq_ref, dk_ref, dv_ref, dc1_ref, dc2_ref, fl_ref, bf_ref, qg_ref, kg_ref,
             dqkv_ref, dfl_ref, dqg_ref, dkg_ref, dbf_ref, carry):
        i = pl.program_id(0)
        tile = nt - 1 - i
        last = (tile % tps) == tps - 1
        gm, gt = _head_maps(d)

        @pl.when(i == 0)
        def _():
            for r_ in (dqg_ref, dkg_ref, dbf_ref):
                r_[...] = jnp.zeros_like(r_)

        @pl.when(last)
        def _():
            carry[...] = jnp.zeros_like(carry)

        def head_norm_bwd(xr, dy, gain, dgain_ref):
            ss = jnp.dot(xr * xr, gm, precision=HI, preferred_element_type=F32)
            rf = jnp.dot(lax.rsqrt(ss / HEAD_DIM + EPS), gt, precision=HI, preferred_element_type=F32)
            xn = xr * rf
            dgain_ref[...] += jnp.sum(dy * xn, axis=0, keepdims=True)
            dyg = dy * gain
            mean = jnp.dot(dyg * xn, gm, precision=HI, preferred_element_type=F32) / HEAD_DIM
            return rf * (dyg - xn * jnp.dot(mean, gt, precision=HI, preferred_element_type=F32))

        dqkv_ref[:, :d] = head_norm_bwd(qkv_ref[:, :d], dq_ref[...], qg_ref[...], dqg_ref).astype(BF16)
        dqkv_ref[:, d:2 * d] = head_norm_bwd(qkv_ref[:, d:2 * d], dk_ref[...], kg_ref[...], dkg_ref).astype(BF16)
        dqkv_ref[:, 2 * d:] = dv_ref[...].astype(BF16)

        dc = dc1_ref[...] + dc2_ref[...]
        tri = (lax.broadcasted_iota(jnp.int32, (tm, tm), 0) <= lax.broadcasted_iota(jnp.int32, (tm, tm), 1)).astype(F32)
        dlog = jnp.dot(tri, dc, precision=HI, preferred_element_type=F32) + carry[...]
        carry[...] = dlog[0:1, :]
        dfl = dlog * (1.0 - _sig(fl_ref[...] + bf_ref[...]))
        dfl_ref[...] = dfl.astype(BF16)
        dbf_ref[...] += jnp.sum(dfl, axis=0, keepdims=True)

    row = lambda w: pl.BlockSpec((tm, w), lambda i: (nt - 1 - i, 0))
    vec = lambda w: pl.BlockSpec((1, w), lambda i: (0, 0))
    return _hosted(
        body, comm, grid=(nt,),
        in_specs=[row(d3), row(d), row(d), row(d), row(LANES), row(LANES), row(LANES), vec(LANES), vec(d), vec(d)],
        out_specs=[row(d3), row(LANES), vec(d), vec(d), vec(LANES)],
        out_shape=[jax.ShapeDtypeStruct((n, d3), BF16), jax.ShapeDtypeStruct((n, LANES), BF16),
                   jax.ShapeDtypeStruct((1, d), F32), jax.ShapeDtypeStruct((1, d), F32),
                   jax.ShapeDtypeStruct((1, LANES), F32)],
        scratch_shapes=[pltpu.VMEM((1, LANES), F32)], sem=("arbitrary",), name=name,
        ins=(qkv, dq, dk, dv, dc1, dc2, fl, bf, qg, kg))


def _attn_specs(bsz, seq, t):
    nb = seq // t
    blk = lambda w: pl.BlockSpec((t, w), lambda b, h, i: (b * nb + i, h))
    full = lambda w: pl.BlockSpec((seq, w), lambda b, h, i: (b, h))
    col = pl.BlockSpec((None, None, t, 2), lambda b, h, i: (b, h, i, 0))
    rows = pl.BlockSpec((None, None, nb, 2, t), lambda b, h, i: (b, h, 0, 0, 0))
    return nb, blk, full, col, rows


_NT = (((1,), (1,)), ((), ()))
ATTN_TILE = 512


def _head_lanes(t, hh):
    lane = lax.broadcasted_iota(jnp.int32, (t, LANES), 1)
    return (lane < HEAD_DIM) if hh == 0 else (lane >= HEAD_DIM)


def _flash_fwd(qa, ka, v, bsz, seq, name):
    n, d = v.shape
    hp = d // LANES
    t = _tile(seq, ATTN_TILE)
    nb, blk, full, col, _ = _attn_specs(bsz, seq, t)

    def body(q_ref, k_ref, v_ref, o_ref, lse_ref):
        i = pl.program_id(2)
        causal = lax.broadcasted_iota(jnp.int32, (t, t), 0) >= lax.broadcasted_iota(jnp.int32, (t, t), 1)

        def block(j, carry, masked):
            rs = pl.ds(pl.multiple_of(j * t, t), t)
            vj = v_ref[rs, :]
            out = []
            for hh in range(2):
                m, l, acc = carry[hh]
                hs = slice(hh * LANES, (hh + 1) * LANES)
                sc = lax.dot_general(q_ref[:, hs], k_ref[rs, hs], _NT, preferred_element_type=F32)
                if masked:
                    sc = jnp.where(causal, sc, NEG)
                mn = jnp.maximum(m, jnp.max(sc, axis=-1, keepdims=True))
                p = jnp.exp(sc - mn)
                al = jnp.exp(m - mn)
                l = al * l + jnp.sum(p, axis=-1, keepdims=True)
                acc = al * acc + jnp.dot(p.astype(BF16), vj, preferred_element_type=F32)
                out.append((mn, l, acc))
            return tuple(out)

        init = tuple((jnp.full((t, 1), NEG, F32), jnp.zeros((t, 1), F32), jnp.zeros((t, LANES), F32))
                     for _ in range(2))
        carry = lax.fori_loop(0, i, lambda j, c: block(j, c, False), init)
        (m0, l0, a0), (m1, l1, a1) = block(i, carry, True)
        o_ref[...] = jnp.where(_head_lanes(t, 0), a0 / l0, a1 / l1)
        lse_ref[:, 0:1] = m0 + jnp.log(l0)
        lse_ref[:, 1:2] = m1 + jnp.log(l1)

    return pl.pallas_call(
        body, grid=(bsz, hp, nb), in_specs=[blk(2 * LANES), full(2 * LANES), full(LANES)],
        out_specs=[blk(LANES), col],
        out_shape=[jax.ShapeDtypeStruct((n, d), F32), jax.ShapeDtypeStruct((bsz, hp, seq, 2), F32)],
        name=name, compiler_params=_cp("parallel", "parallel", "arbitrary"),
    )(qa, ka, v)


def _flash_bwd(qa, ka, v, do, o, lse_row, bsz, seq, name):
    n, d = v.shape
    hp = d // LANES
    t = _tile(seq, ATTN_TILE)
    nb, blk, full, col, rows = _attn_specs(bsz, seq, t)
    scale = 1.0 / math.sqrt(HEAD_DIM)
    tn_ = (((0,), (0,)), ((), ()))

    def body(k_ref, v_ref, q_ref, do_ref, o_ref, lse_ref, dq_ref, dcc_ref, dk_ref, dv_ref, dck_ref, dqa, dl):
        j = pl.program_id(2)
        causal = lax.broadcasted_iota(jnp.int32, (t, t), 1) >= lax.broadcasted_iota(jnp.int32, (t, t), 0)
        heads = [_head_lanes(t, 0), _head_lanes(t, 1)]

        @pl.when(j == 0)
        def _():
            dqa[...] = jnp.zeros_like(dqa)
            ones = jnp.ones((8, LANES), F32)
            for ib in range(nb):
                rs = slice(ib * t, (ib + 1) * t)
                prod = do_ref[rs, :] * o_ref[rs, :]
                for hh in range(2):
                    dl[ib, hh:hh + 1, :] = lax.dot_general(ones, jnp.where(heads[hh], prod, 0.0), _NT, precision=HI,
                                                           preferred_element_type=F32)[0:1]

        vj = v_ref[...]

        def block(i, carry, masked):
            rs = pl.ds(pl.multiple_of(i * t, t), t)
            doi = do_ref[rs, :]
            dks, dvp = list(carry[:2]), carry[2]
            for hh in range(2):
                hs = slice(hh * LANES, (hh + 1) * LANES)
                kh, qi = k_ref[:, hs], q_ref[rs, hs]
                dom = jnp.where(heads[hh], doi, 0.0).astype(BF16)
                st = lax.dot_general(kh, qi, _NT, preferred_element_type=F32)
                if masked:
                    st = jnp.where(causal, st, NEG)
                pt = jnp.exp(st - lse_ref[i, hh:hh + 1, :])
                dvp = dvp + jnp.dot(pt.astype(BF16), dom, preferred_element_type=F32)
                dpt = lax.dot_general(vj, dom, _NT, preferred_element_type=F32)
                dsb = (pt * (dpt - dl[i, hh:hh + 1, :])).astype(BF16)
                dks[hh] = dks[hh] + jnp.dot(dsb, qi, preferred_element_type=F32)
                dqa[rs, hs] += lax.dot_general(dsb, kh, tn_, preferred_element_type=F32)
            return dks[0], dks[1], dvp

        zero = jnp.zeros((t, LANES), F32)
        carry = block(j, (zero, zero, zero), True)
        dk0, dk1, dvp = lax.fori_loop(j + 1, nb, lambda i, c: block(i, c, False), carry)
        dk_ref[...] = jnp.where(heads[0], dk0, pltpu.roll(dk1, HEAD_DIM, 1))
        dv_ref[...] = dvp
        dck_ref[:, 0:1] = -dk0[:, HEAD_DIM + 3:HEAD_DIM + 4]
        dck_ref[:, 1:2] = -dk1[:, HEAD_DIM + 3:HEAD_DIM + 4]

        @pl.when(j == nb - 1)
        def _():
            first = lax.broadcasted_iota(jnp.int32, (seq, LANES), 1) < HEAD_DIM
            dq_ref[...] = jnp.where(first, dqa[:, :LANES], pltpu.roll(dqa[:, LANES:], HEAD_DIM, 1)) * scale
            for hh in range(2):
                lo = hh * LANES + HEAD_DIM
                dcc_ref[:, hh:hh + 1] = dqa[:, lo:lo + 1]

    whole_col = pl.BlockSpec((None, None, seq, 2), lambda b, h, i: (b, h, 0, 0))
    cshape = jax.ShapeDtypeStruct((bsz, hp, seq, 2), F32)
    nd = jax.ShapeDtypeStruct((n, d), F32)
    return pl.pallas_call(
        body, grid=(bsz, hp, nb),
        in_specs=[blk(2 * LANES), blk(LANES), full(2 * LANES), full(LANES), full(LANES), rows],
        out_specs=[full(LANES), whole_col, blk(LANES), blk(LANES), col],
        out_shape=[nd, cshape, nd, nd, cshape],
        scratch_shapes=[pltpu.VMEM((seq, 2 * LANES), F32), pltpu.VMEM((nb, 2, t), F32)],
        name=name, compiler_params=_cp("parallel", "parallel", "arbitrary"),
    )(ka, v, qa, do, o, lse_row)


def _adamw(w, g, m, v, name):
    r, c = w.shape
    tr = r
    for cand in (512, 256, 128, 64, 32, 16, 8):
        if r % cand == 0 and r > cand and cand * c * 4 <= 4 * 1024 * 1024:
            tr = cand
            break

    def body(w_ref, g_ref, m_ref, v_ref, d_ref, m2_ref, v2_ref):
        gv = g_ref[...]
        m2 = ADAM_B1 * m_ref[...] + (1.0 - ADAM_B1) * gv
        v2 = ADAM_B2 * v_ref[...] + (1.0 - ADAM_B2) * jnp.square(gv)
        m_hat = m2 / (1.0 - ADAM_B1 ** ADAM_STEP)
        v_hat = v2 / (1.0 - ADAM_B2 ** ADAM_STEP)
        d_ref[...] = -ADAM_LR * (m_hat / (jnp.sqrt(v_hat) + ADAM_EPS) + ADAM_WD * w_ref[...])
        m2_ref[...] = m2
        v2_ref[...] = v2

    blk = pl.BlockSpec((tr, c), lambda i: (i, 0))
    shp = jax.ShapeDtypeStruct((r, c), F32)
    return pl.pallas_call(
        body, grid=(r // tr,), in_specs=[blk] * 4, out_specs=[blk] * 3, out_shape=[shp] * 3,
        name=name, compiler_params=_cp("parallel"),
    )(w, g, m, v)


def _adamw_nd(w, g, m, v, name):
    shape = w.shape
    two = (math.prod(shape[:-1]), shape[-1])
    d_, m_, v_ = _adamw(w.reshape(two), g.reshape(two), m.reshape(two), v.reshape(two), name)
    return g.reshape(shape), d_.reshape(shape), m_.reshape(shape), v_.reshape(shape)


def _place():
    x, y, c = lax.axis_index("x"), lax.axis_index("y"), lax.axis_index("c")
    chips = [(1 - x, y), (x, 1 - y), (1 - x, 1 - y)]
    return x, y, c, chips


def _gather_chips(shards):
    nt = len(shards)
    halves = [s.shape[0] // 2 for s in shards]

    def copies(ins, outs, send_sems, recv_sems):
        x, y, c, chips = _place()
        cps = []
        for t in range(nt):
            rows = pl.ds(c * halves[t], halves[t])
            for jj, (cx, cy) in enumerate(chips):
                cps.append(pltpu.make_async_remote_copy(
                    src_ref=ins[t].at[rows, :], dst_ref=outs[t].at[2 * x + y, rows, :], send_sem=send_sems.at[3 * t + jj],
                    recv_sem=recv_sems.at[3 * t + jj], device_id=(cx, cy, c), device_id_type=MESH))
        return cps

    def start(*refs):
        for cp in copies(*refs):
            cp.start()

    def wait(*refs):
        for cp in copies(*refs):
            cp.wait()

    return _Comm(shards, [jax.ShapeDtypeStruct((N_CHIPS,) + s.shape, s.dtype) for s in shards], 3 * nt, start, wait)


def _gather_sibling(shards, bufs, name):
    nt = len(shards)
    halves = [s.shape[0] // 2 for s in shards]

    def body(*refs):
        ins, outs = refs[:nt], refs[2 * nt:3 * nt]
        send_sems, recv_sems = refs[3 * nt:]
        x, y, c, chips = _place()
        sibling = (x, y, 1 - c)

        def copy(t, jj, hf):
            cx, cy = chips[jj]
            region = outs[t].at[2 * cx + cy, pl.ds(hf * halves[t], halves[t]), :]
            return pltpu.make_async_remote_copy(src_ref=region, dst_ref=region, send_sem=send_sems.at[4 * t + jj],
                                                recv_sem=recv_sems.at[4 * t + jj], device_id=sibling,
                                                device_id_type=MESH)

        def own(t):
            return pltpu.make_async_remote_copy(src_ref=ins[t], dst_ref=outs[t].at[2 * x + y],
                                                send_sem=send_sems.at[4 * t + 3], recv_sem=recv_sems.at[4 * t + 3],
                                                device_id=sibling, device_id_type=MESH)

        sends = [copy(t, jj, c) for t in range(nt) for jj in range(3)] + [own(t) for t in range(nt)]
        for cp in sends:
            cp.start()
        for t in range(nt):
            for jj in range(3):
                copy(t, jj, 1 - c).wait_recv()
            own(t).wait_recv()
        for cp in sends:
            cp.wait_send()

    return pl.pallas_call(
        body, in_specs=[ANY] * (2 * nt), out_specs=[ANY] * nt,
        out_shape=[jax.ShapeDtypeStruct(b.shape, b.dtype) for b in bufs],
        scratch_shapes=[pltpu.SemaphoreType.DMA((4 * nt,)), pltpu.SemaphoreType.DMA((4 * nt,))],
        input_output_aliases={nt + t: t for t in range(nt)}, name=name,
        compiler_params=pltpu.CompilerParams(has_side_effects=True),
    )(*shards, *bufs)


def _send_sibling_halves(grads, name):
    nt = len(grads)
    halves = [g.shape[1] // 2 for g in grads]

    def body(*refs):
        ins, outs = refs[:nt], refs[nt:2 * nt]
        send_sems, recv_sems = refs[2 * nt:]
        x, y, c, _ = _place()
        cps = []
        for t in range(nt):
            src = ins[t].at[:, pl.ds((1 - c) * halves[t], halves[t]), :]
            cps.append(pltpu.make_async_remote_copy(src_ref=src, dst_ref=outs[t], send_sem=send_sems.at[t],
                                                    recv_sem=recv_sems.at[t], device_id=(x, y, 1 - c),
                                                    device_id_type=MESH))
        for cp in cps:
            cp.start()
        for cp in cps:
            cp.wait()

    return pl.pallas_call(
        body, in_specs=[ANY] * nt, out_specs=[ANY] * nt,
        out_shape=[jax.ShapeDtypeStruct((N_CHIPS, h, g.shape[2]), g.dtype) for g, h in zip(grads, halves)],
        scratch_shapes=[pltpu.SemaphoreType.DMA((nt,)), pltpu.SemaphoreType.DMA((nt,))],
        name=name, compiler_params=pltpu.CompilerParams(has_side_effects=True),
    )(*grads)


def _add_sibling(g, r1, c_idx, out_dtype, name):
    s, r, w = g.shape
    rh = r // 2
    tr = rh
    for cand in (512, 256, 128, 64, 32, 16, 8):
        if rh % cand == 0 and cand * w * 4 <= 4 * 1024 * 1024:
            tr = cand
            break
    per = rh // tr

    def body(c_ref, g_ref, r_ref, o_ref):
        o_ref[...] = (g_ref[...] + r_ref[...]).astype(o_ref.dtype)

    return pl.pallas_call(
        body,
        grid_spec=pltpu.PrefetchScalarGridSpec(
            num_scalar_prefetch=1, grid=(s, per),
            in_specs=[pl.BlockSpec((None, tr, w), lambda a, b, c_ref: (a, c_ref[0] * per + b, 0)),
                      pl.BlockSpec((None, tr, w), lambda a, b, c_ref: (a, b, 0))],
            out_specs=pl.BlockSpec((None, tr, w), lambda a, b, c_ref: (a, b, 0))),
        out_shape=jax.ShapeDtypeStruct((s, rh, w), out_dtype), name=name,
        compiler_params=_cp("parallel", "parallel"),
    )(c_idx, g, r1)


def _chip_slots(parts):
    nt = len(parts)

    def copies(ins, outs, send_sems, recv_sems):
        x, y, c, chips = _place()
        return [pltpu.make_async_remote_copy(
            src_ref=ins[t].at[2 * cx + cy], dst_ref=outs[t].at[jj], send_sem=send_sems.at[3 * t + jj],
            recv_sem=recv_sems.at[3 * t + jj], device_id=(cx, cy, c), device_id_type=MESH)
            for t in range(nt) for jj, (cx, cy) in enumerate(chips)]

    def start(*refs):
        for cp in copies(*refs):
            cp.start()

    def wait(*refs):
        for cp in copies(*refs):
            cp.wait()

    return _Comm(parts, [jax.ShapeDtypeStruct((3,) + p.shape[1:], p.dtype) for p in parts], 3 * nt, start, wait)


def _add_chips(part, r2, place_idx, name):
    s, rh, w = part.shape
    tr = rh
    for cand in (512, 256, 128, 64, 32, 16, 8):
        if rh % cand == 0 and cand * w * 4 <= 4 * 1024 * 1024:
            tr = cand
            break
    per = rh // tr

    def body(b_ref, p_ref, r0_ref, r1_ref, r2_ref, o_ref):
        up = lambda ref: ref[...].astype(F32)
        o_ref[...] = ((up(p_ref) + up(r0_ref)) + up(r1_ref)) + up(r2_ref)

    return pl.pallas_call(
        body,
        grid_spec=pltpu.PrefetchScalarGridSpec(
            num_scalar_prefetch=1, grid=(per,),
            in_specs=[pl.BlockSpec((None, tr, w), lambda a, b_ref: (b_ref[0], a, 0))]
            + [pl.BlockSpec((None, tr, w), lambda a, b_ref, jj=jj: (jj, a, 0)) for jj in range(3)],
            out_specs=pl.BlockSpec((tr, w), lambda a, b_ref: (b_ref[1] * per + a, 0))),
        out_shape=jax.ShapeDtypeStruct((2 * rh, w), F32), name=name, compiler_params=_cp("parallel"),
    )(place_idx, part, r2, r2, r2)


def _swap_halves(bufs, name):
    nt = len(bufs)

    def body(*refs):
        outs = refs[nt:2 * nt]
        send_sems, recv_sems = refs[2 * nt:]
        x, y, c, _ = _place()

        def copy(t, hf):
            rh = outs[t].shape[0] // 2
            region = outs[t].at[pl.ds(hf * rh, rh), :]
            return pltpu.make_async_remote_copy(src_ref=region, dst_ref=region, send_sem=send_sems.at[t],
                                                recv_sem=recv_sems.at[t], device_id=(x, y, 1 - c),
                                                device_id_type=MESH)

        sends = [copy(t, c) for t in range(nt)]
        for cp in sends:
            cp.start()
        for t in range(nt):
            copy(t, 1 - c).wait_recv()
        for cp in sends:
            cp.wait_send()

    return pl.pallas_call(
        body, in_specs=[ANY] * nt, out_specs=[ANY] * nt,
        out_shape=[jax.ShapeDtypeStruct(b.shape, b.dtype) for b in bufs],
        scratch_shapes=[pltpu.SemaphoreType.DMA((nt,)), pltpu.SemaphoreType.DMA((nt,))],
        input_output_aliases={t: t for t in range(nt)}, name=name,
        compiler_params=pltpu.CompilerParams(has_side_effects=True),
    )(*bufs)


def _rs_begin(grads, wire, c_idx, tag):
    r1 = _send_sibling_halves(grads, name=f"rs_sibling_{tag}")
    return [_add_sibling(g, r, c_idx, wire[t], name=f"rs_add_sibling_{tag}_{t}")
            for t, (g, r) in enumerate(zip(grads, r1))]


def _rs_finish(parts, r2, chip_idx, tag):
    fin = [_add_chips(p, r, chip_idx, name=f"rs_add_chips_{tag}_{t}") for t, (p, r) in enumerate(zip(parts, r2))]
    return _swap_halves(fin, name=f"rs_swap_{tag}")


def _all_reduce_small(v, name):
    r, w = v.shape

    def body(v_ref, o_ref, buf, send_sems, recv_sems):
        x, y, c, _ = _place()
        me = 4 * x + 2 * y + c
        buf[me] = v_ref[...]
        cps = []
        for kk in range(1, 8):
            peer = (x ^ ((kk >> 2) & 1), y ^ ((kk >> 1) & 1), c ^ (kk & 1))
            cps.append(pltpu.make_async_remote_copy(src_ref=v_ref, dst_ref=buf.at[me], send_sem=send_sems.at[kk - 1],
                                                    recv_sem=recv_sems.at[kk - 1], device_id=peer, device_id_type=MESH))
        for cp in cps:
            cp.start()
        for kk in range(1, 8):
            pltpu.make_async_remote_copy(src_ref=v_ref, dst_ref=buf.at[me ^ kk], send_sem=send_sems.at[kk - 1],
                                         recv_sem=recv_sems.at[kk - 1], device_id=(x, y, c),
                                         device_id_type=MESH).wait_recv()
        for cp in cps:
            cp.wait_send()
        acc = buf[0]
        for dev in range(1, 8):
            acc = acc + buf[dev]
        o_ref[...] = acc

    vm = pl.BlockSpec(memory_space=pltpu.VMEM)
    return pl.pallas_call(
        body, in_specs=[vm], out_specs=vm, out_shape=jax.ShapeDtypeStruct((r, w), F32),
        scratch_shapes=[pltpu.VMEM((8, r, w), F32), pltpu.SemaphoreType.DMA((7,)), pltpu.SemaphoreType.DMA((7,))],
        name=name, compiler_params=pltpu.CompilerParams(has_side_effects=True),
    )(v)


def _to_shards(a, axis=-1):
    axis = axis % a.ndim
    shp = a.shape
    a = a.reshape(shp[:axis] + (N_CHIPS, shp[axis] // N_CHIPS) + shp[axis + 1:])
    return jnp.moveaxis(a, axis, 0).reshape(N_CHIPS, -1)


def _from_shards(s, shard_shape, axis=-1):
    axis = axis % len(shard_shape)
    a = jnp.moveaxis(s.reshape((N_CHIPS,) + tuple(shard_shape)), 0, axis)
    return a.reshape(tuple(shard_shape[:axis]) + (N_CHIPS * shard_shape[axis],) + tuple(shard_shape[axis + 1:]))


def _pack(vecs, rows):
    flat = jnp.concatenate([v.reshape(v.shape[0], -1) if v.ndim > 1 else v.reshape(1, -1) for v in vecs], axis=1)
    lead = flat.shape[0]
    flat = jnp.pad(flat, ((0, 0), (0, rows * LANES - flat.shape[1])))
    return flat.reshape(lead, rows, LANES)


def _pack_rows(sizes, mult):
    total = sum(sizes)
    rows = -(-total // LANES)
    return -(-rows // mult) * mult


def _unpack(flat, shapes):
    out, pos = [], 0
    for shp in shapes:
        sz = math.prod(shp)
        out.append(flat[..., pos:pos + sz].reshape(flat.shape[:-1] + tuple(shp)))
        pos += sz
    return out


def _row_layout(col, t):
    bsz, hp, seq, _ = col.shape
    return col.reshape(bsz, hp, seq // t, t, 2).transpose(0, 1, 2, 4, 3)


def _from_col_layout(col):
    bsz, hp, seq, _ = col.shape
    a = col.transpose(0, 2, 1, 3).reshape(bsz * seq, 2 * hp)
    return jnp.pad(a, ((0, 0), (0, LANES - 2 * hp)))


def kernel(x, norm_mix, norm_ffn, conv_w_in, conv_b_in, conv_dw, conv_dw_b, conv_ln_g, conv_ln_b, conv_w_out, conv_b_out, pool_w, pool_b, pool_scale, fox_w_in, fox_b_f, fox_q_gain, fox_k_gain, fox_w_o, ffn_w_up, ffn_dw, ffn_dw_b, ffn_w_down, loss_target, m_norm_mix, m_norm_ffn, m_conv_w_in, m_conv_b_in, m_conv_dw, m_conv_dw_b, m_conv_ln_g, m_conv_ln_b, m_conv_w_out, m_conv_b_out, m_pool_w, m_pool_b, m_pool_scale, m_fox_w_in, m_fox_b_f, m_fox_q_gain, m_fox_k_gain, m_fox_w_o, m_ffn_w_up, m_ffn_dw, m_ffn_dw_b, m_ffn_w_down, v_norm_mix, v_norm_ffn, v_conv_w_in, v_conv_b_in, v_conv_dw, v_conv_dw_b, v_conv_ln_g, v_conv_ln_b, v_conv_w_out, v_conv_b_out, v_pool_w, v_pool_b, v_pool_scale, v_fox_w_in, v_fox_b_f, v_fox_q_gain, v_fox_k_gain, v_fox_w_o, v_ffn_w_up, v_ffn_dw, v_ffn_dw_b, v_ffn_w_down):
    bsz, seq, d = x.shape
    n = bsz * seq
    depth = norm_mix.shape[0]
    n_conv, n_pool, n_fox = conv_w_in.shape[0], pool_w.shape[0], fox_w_in.shape[0]
    f2 = ffn_dw_b.shape[1]
    f = f2 // 2
    nh = d // HEAD_DIM
    hp = d // LANES
    ng = len(POOL_WINDOWS)
    gd = d // ng
    c_idx = lax.axis_index("c").astype(jnp.int32).reshape(1)
    chip_idx = jnp.stack([2 * lax.axis_index("x") + lax.axis_index("y"), lax.axis_index("c")]).astype(jnp.int32)

    small_shapes = [conv_b_in.shape, conv_dw.shape, conv_dw_b.shape, conv_ln_g.shape, conv_ln_b.shape,
                    conv_b_out.shape, pool_b.shape, ffn_dw.shape]
    small_rows = _pack_rows([math.prod(s) for s in small_shapes], 16)
    small = _pack([v.reshape(1, -1) for v in (conv_b_in, conv_dw, conv_dw_b, conv_ln_g, conv_ln_b, conv_b_out,
                                                pool_b, ffn_dw)], small_rows)[0]
    def layer_shards(i):
        kind, j = i % 3, i // 3
        shards = [ffn_w_up[i].astype(BF16), ffn_w_down[i].astype(BF16)]
        if kind == 0:
            shards += [conv_w_in[j].astype(BF16), conv_w_out[j].astype(BF16)]
        elif kind == 1:
            shards += [pool_w[j].reshape(ng * (gd // N_CHIPS), gd).astype(BF16)]
        else:
            shards += [fox_w_in[j].astype(BF16), fox_w_o[j].astype(BF16)]
        if i == 0:
            shards.append(small)
        return shards

    gathered = [None] * depth
    gathered[0] = _gather_sibling(layer_shards(0), _run_comm(_gather_chips(layer_shards(0)), name="gather_chips_l0"),
                                  name="gather_sibling_l0")
    small_all = gathered[0][-1].reshape(N_CHIPS, -1)
    sm = _unpack(small_all, small_shapes)
    axes = [-1] * 8
    b_in_f, dw_f, dw_b_f, ln_g_f, ln_b_f, b_out_f, pool_b_f, ffn_dw_f = [
        _from_shards(s_.reshape(N_CHIPS, -1), shp, ax) for s_, shp, ax in zip(sm, small_shapes, axes)]

    xs = x.reshape(n, d)
    tgt = loss_target.reshape(n, d)
    vec = lambda a: a.reshape(1, -1)

    saved = []
    cur = xs
    for i in range(depth):
        kind, j = i % 3, i // 3
        wts = gathered[i]
        sv = {"x_in": cur}
        if kind == 0:
            w_in, w_out = wts[2], wts[3].reshape(d, d)
            wcol = w_in.shape[2]
            h = _rms_fwd(cur, vec(norm_mix[i]), name=f"rms_mix_l{i}")
            pa = _mm(h, w_in, m=n, n=d, k=d, tk=d, tn=wcol, b_stk=wcol, b_s0=0, bias=vec(b_in_f[j, :d]),
                     name=f"conv_in_a_l{i}")
            pg = _mm(h, w_in, m=n, n=d, k=d, tk=d, tn=wcol, b_stk=wcol, b_s0=2, bias=vec(b_in_f[j, d:]),
                     name=f"conv_in_g_l{i}")
            taps = jnp.pad(dw_f[j], ((0, CONV_HALO - CONV_WIDTH), (0, 0)))
            u, s_ = _conf_fwd(pa, pg, taps, vec(dw_b_f[j]), vec(ln_g_f[j]), vec(ln_b_f[j]), seq, name=f"conf_fwd_l{i}")
            cur = _mm(s_, w_out, m=n, n=d, k=d, tk=d, bias=vec(b_out_f[j]), res=cur, name=f"conv_out_l{i}")
            sv.update(h=h, pa=pa, pg=pg, u=u, s=s_, taps=taps)
        elif kind == 1:
            pw = wts[2].reshape(N_CHIPS, ng, gd // N_CHIPS, gd).transpose(1, 0, 2, 3).reshape(ng, gd, gd)
            cur, p = _pool_fwd(cur, vec(norm_mix[i]), pw, vec(pool_b_f[j]), vec(pool_scale[j]), seq,
                               name=f"pool_fwd_l{i}")
            sv.update(p=p, pw=pw)
        else:
            w_in = wts[2].transpose(1, 0, 2).reshape(d, -1)
            w_qkv = w_in[:, :3 * d]
            w_f = jnp.pad(w_in[:, 3 * d:], ((0, 0), (0, LANES - nh)))
            w_o = wts[3].reshape(d, d)
            bf = jnp.pad(vec(fox_b_f[j]), ((0, 0), (0, LANES - nh)))
            qg, kg = jnp.tile(vec(fox_q_gain[j]), (1, nh)), jnp.tile(vec(fox_k_gain[j]), (1, nh))
            h = _rms_fwd(cur, vec(norm_mix[i]), name=f"rms_mix_l{i}")
            qkv = _mm(h, w_qkv, m=n, n=3 * d, k=d, tk=d, tn=d, name=f"fox_qkv_l{i}")
            fl = _mm(h, w_f, m=n, n=LANES, k=d, tk=d, name=f"fox_fl_l{i}")
            qa, ka, v = _fox_prep_fwd(qkv, fl, bf, qg, kg, seq, name=f"fox_prep_l{i}")
            o, lse = _flash_fwd(qa, ka, v, bsz, seq, name=f"fox_attn_l{i}")
            cur = _mm(o, w_o, m=n, n=d, k=d, tk=d, res=cur, name=f"fox_out_l{i}")
            sv.update(h=h, qkv=qkv, fl=fl, qa=qa, ka=ka, v=v, o=o, lse=lse, w_qkv=w_qkv, w_f=w_f, w_o=w_o, bf=bf,
                      qg=qg, kg=kg)
        w_up, w_down = wts[0], wts[1].reshape(f, d)
        ucol = w_up.shape[2]
        sv["x_mid"] = cur
        h2 = _rms_fwd(cur, vec(norm_ffn[i]), name=f"rms_ffn_l{i}")
        uv = _mm(h2, w_up, m=n, n=f, k=d, tk=d, tn=ucol, b_stk=ucol, b_s0=0, name=f"ffn_up_v_l{i}")
        ug = _mm(h2, w_up, m=n, n=f, k=d, tk=d, tn=ucol, b_stk=ucol, b_s0=2, name=f"ffn_up_g_l{i}")
        fdw, fdb = ffn_dw_f[i], ffn_dw_b[i]
        glu_args = (uv, ug, fdw[:, :f], fdw[:, f:], vec(fdb[:f]), vec(fdb[f:]), seq)
        if i + 1 < depth:
            nxt_shards = layer_shards(i + 1)
            (a_,), landed = _ffn_glu_fwd(*glu_args, name=f"ffn_glu_l{i}", comm=_gather_chips(nxt_shards))
            gathered[i + 1] = _gather_sibling(nxt_shards, landed, name=f"gather_sibling_l{i + 1}")
        else:
            (a_,) = _ffn_glu_fwd(*glu_args, name=f"ffn_glu_l{i}")
        cur = _mm(a_, w_down, m=n, n=d, k=f, tk=f, tn=d, res=cur, name=f"ffn_down_l{i}")
        sv.update(h2=h2, uv=uv, ug=ug, a=a_)
        saved.append(sv)

    dy, loss_part = _loss(cur, tgt, name="loss")
    loss = lax.psum(loss_part[0, 0], ("x", "y", "c"))

    g_norm_mix, g_norm_ffn = [None] * depth, [None] * depth
    g_ffn_dw_b = [None] * depth
    g_up, g_down = [None] * depth, [None] * depth
    g_small = {}
    g_conv_in, g_conv_out = [None] * n_conv, [None] * n_conv
    g_pool_w = g_fox_in = g_fox_o = None
    g_pool_scale = g_bf = g_qg = g_kg = None
    part_small = {"b_in": [None] * n_conv, "dw": [None] * n_conv, "dw_b": [None] * n_conv, "ln_g": [None] * n_conv,
                  "ln_b": [None] * n_conv, "b_out": [None] * n_conv, "pool_b": None, "ffn_dw": [None] * depth}

    pending = []
    done = {}

    def take_pending():
        groups = list(pending)
        pending.clear()
        parts = [p_ for g_ in groups for p_ in g_[0]]
        return groups, (_chip_slots(parts) if parts else None)

    def finish_groups(groups, r2):
        pos = 0
        for parts, tag, sink in groups:
            sink(_rs_finish(parts, r2[pos:pos + len(parts)], chip_idx, tag))
            pos += len(parts)

    def carried(fn, args, name):
        groups, comm = take_pending()
        if comm is None:
            return fn(*args, name=name)
        outs, r2 = fn(*args, name=name, comm=comm)
        finish_groups(groups, r2)
        return outs

    dcur = dy
    for i in reversed(range(depth)):
        kind, j = i % 3, i // 3
        wts, sv = gathered[i], saved[i]
        w_up, w_down = wts[0], wts[1].reshape(f, d)
        ucol = w_up.shape[2]
        fdw, fdb = ffn_dw_f[i], ffn_dw_b[i]
        da = _mm(dcur, w_down, m=n, n=f, k=d, tk=d, tn=f // 2 if (f // 2) % LANES == 0 else f, tb=True,
                 name=f"ffn_down_dx_l{i}")
        dw_down = _mm(sv["a"], dcur, m=f, n=d, k=n, tm=f // 2, tn=d, tk=1024, ta=True, name=f"ffn_down_dw_l{i}")
        duv, dug, dwv, dwg, dbv, dbg = carried(
            _ffn_glu_bwd, (sv["uv"], sv["ug"], da, fdw[:, :f], fdw[:, f:], vec(fdb[:f]), vec(fdb[f:]), seq),
            name=f"ffn_glu_bwd_l{i}")
        dw_up = _mm(sv["h2"], duv, m=d, n=f, k=n, tm=d, tn=ucol, tk=1024, ta=True, o_stk=ucol, o_s0=0,
                    o_slots=N_CHIPS, name=f"ffn_up_dw_v_l{i}")
        dw_up = _mm(sv["h2"], dug, m=d, n=f, k=n, tm=d, tn=ucol, tk=1024, ta=True, o_stk=ucol, o_s0=2,
                    o_slots=N_CHIPS, o_buf=dw_up, name=f"ffn_up_dw_g_l{i}")
        dh2 = _mm(duv, w_up, m=n, n=d, k=f, tn=d, tk=ucol, tb=True, b_stk=ucol, b_s0=0, name=f"ffn_up_dx_v_l{i}")
        dh2 = _mm(dug, w_up, m=n, n=d, k=f, tn=d, tk=ucol, tb=True, b_stk=ucol, b_s0=2, res=dh2,
                  name=f"ffn_up_dx_g_l{i}")
        dmid, g_norm_ffn[i] = _rms_bwd(sv["x_mid"], dh2, vec(norm_ffn[i]), dcur, name=f"rms_ffn_bwd_l{i}")
        part_small["ffn_dw"][i] = jnp.concatenate([dwv, dwg], axis=1)
        g_ffn_dw_b[i] = jnp.concatenate([dbv, dbg], axis=1)

        def ffn_sink(red, i=i):
            g_up[i], g_down[i] = red[0], red[1]

        pending.append((_rs_begin([dw_up, dw_down.reshape(N_CHIPS, f // N_CHIPS, d)], [BF16, BF16], c_idx,
                                  tag=f"ffn_l{i}"), f"ffn_l{i}", ffn_sink))

        if kind == 0:
            w_in, w_out = wts[2], wts[3].reshape(d, d)
            wcol = w_in.shape[2]
            ds = _mm(dmid, w_out, m=n, n=d, k=d, tk=d, tb=True, name=f"conv_out_dx_l{i}")
            dw_out = _mm(sv["s"], dmid, m=d, n=d, k=n, tm=d, tn=d, tk=1024, ta=True, name=f"conv_out_dw_l{i}")
            dpa, dpg, ddw, ddwb, dlng, dlnb, dba, dbg_ = carried(
                _conf_bwd, (sv["u"], ds, sv["pa"], sv["pg"], sv["taps"], vec(ln_g_f[j]), vec(ln_b_f[j]), seq),
                name=f"conf_bwd_l{i}")
            dw_in = _mm(sv["h"], dpa, m=d, n=d, k=n, tm=d, tn=wcol, tk=1024, ta=True, o_stk=wcol, o_s0=0,
                        o_slots=N_CHIPS, name=f"conv_in_dw_a_l{i}")
            dw_in = _mm(sv["h"], dpg, m=d, n=d, k=n, tm=d, tn=wcol, tk=1024, ta=True, o_stk=wcol, o_s0=2,
                        o_slots=N_CHIPS, o_buf=dw_in, name=f"conv_in_dw_g_l{i}")
            dh = _mm(dpa, w_in, m=n, n=d, k=d, tn=d, tk=wcol, tb=True, b_stk=wcol, b_s0=0, name=f"conv_in_dx_a_l{i}")
            dh = _mm(dpg, w_in, m=n, n=d, k=d, tn=d, tk=wcol, tb=True, b_stk=wcol, b_s0=2, res=dh,
                     name=f"conv_in_dx_g_l{i}")
            dcur, g_norm_mix[i], db_out = _rms_bwd(sv["x_in"], dh, vec(norm_mix[i]), dmid, name=f"rms_mix_bwd_l{i}",
                                                   colsum=True)
            part_small["b_in"][j] = jnp.concatenate([dba, dbg_], axis=1)
            part_small["dw"][j] = ddw[:CONV_WIDTH]
            part_small["dw_b"][j], part_small["ln_g"][j], part_small["ln_b"][j] = ddwb, dlng, dlnb
            part_small["b_out"][j] = db_out
            mix_grads = [dw_in, dw_out.reshape(N_CHIPS, d // N_CHIPS, d)]
        elif kind == 1:
            dcur, g_norm_mix[i], dpw, dpb, dpsc = _pool_bwd(dmid, sv["x_in"], sv["p"], vec(norm_mix[i]), sv["pw"],
                                                            vec(pool_b_f[j]), vec(pool_scale[j]), seq,
                                                            name=f"pool_bwd_l{i}")
            part_small["pool_b"] = dpb
            g_pool_scale = dpsc
            mix_grads = [dpw.reshape(ng, N_CHIPS, gd // N_CHIPS, gd).transpose(1, 0, 2, 3).reshape(N_CHIPS, gd, gd)]
        else:
            do = _mm(dmid, sv["w_o"], m=n, n=d, k=d, tk=d, tb=True, name=f"fox_out_dx_l{i}")
            dw_o = _mm(sv["o"], dmid, m=d, n=d, k=n, tm=d, tn=d, tk=1024, ta=True, name=f"fox_out_dw_l{i}")
            dq, dcc, dk, dv, dck = _flash_bwd(sv["qa"], sv["ka"], sv["v"], do, sv["o"],
                                              _row_layout(sv["lse"], _tile(seq, ATTN_TILE)), bsz, seq,
                                              name=f"fox_attn_bwd_l{i}")
            dqkv, dfl, dqg, dkg, dbf = carried(
                _fox_prep_bwd, (sv["qkv"], dq, dk, dv, _from_col_layout(dcc), _from_col_layout(dck), sv["fl"], sv["bf"],
                                sv["qg"], sv["kg"], seq), name=f"fox_prep_bwd_l{i}")
            dw_qkv = _mm(sv["h"], dqkv, m=d, n=3 * d, k=n, tm=d, tn=d, tk=1024, ta=True, name=f"fox_qkv_dw_l{i}")
            dw_f = _mm(sv["h"], dfl, m=d, n=LANES, k=n, tm=d, tk=1024, ta=True, name=f"fox_fl_dw_l{i}")
            dh = _mm(dqkv, sv["w_qkv"], m=n, n=d, k=3 * d, tn=d, tk=d, tb=True, name=f"fox_qkv_dx_l{i}")
            dh = _mm(dfl, sv["w_f"], m=n, n=d, k=LANES, tn=d, tb=True, res=dh, name=f"fox_fl_dx_l{i}")
            dcur, g_norm_mix[i] = _rms_bwd(sv["x_in"], dh, vec(norm_mix[i]), dmid, name=f"rms_mix_bwd_l{i}")
            g_bf = dbf[:, :nh]
            g_qg = dqg.reshape(nh, HEAD_DIM).sum(axis=0, keepdims=True)
            g_kg = dkg.reshape(nh, HEAD_DIM).sum(axis=0, keepdims=True)
            dw_in_full = jnp.concatenate([dw_qkv, dw_f[:, :nh]], axis=1)
            wshard = dw_in_full.shape[1] // N_CHIPS
            mix_grads = [dw_in_full.reshape(d, N_CHIPS, wshard).transpose(1, 0, 2),
                         dw_o.reshape(N_CHIPS, d // N_CHIPS, d)]
        wire = [BF16] * len(mix_grads)
        if i == 0:
            sm_parts = [jnp.concatenate(part_small["b_in"]), jnp.stack(part_small["dw"]),
                        jnp.concatenate(part_small["dw_b"]), jnp.concatenate(part_small["ln_g"]),
                        jnp.concatenate(part_small["ln_b"]), jnp.concatenate(part_small["b_out"]),
                        part_small["pool_b"].reshape(n_pool, ng, gd), jnp.stack(part_small["ffn_dw"])]
            mix_grads.append(_pack([_to_shards(p_) for p_ in sm_parts], small_rows))
            wire.append(F32)

        def mix_sink(red, i=i, kind=kind, j=j):
            if kind == 0:
                g_conv_in[j], g_conv_out[j] = red[0], red[1]
            elif kind == 1:
                done["pool_w"] = red[0]
            else:
                done["fox_in"], done["fox_o"] = red[0], red[1]
            if i == 0:
                done["small"] = red[-1].reshape(-1)

        pending.append((_rs_begin(mix_grads, wire, c_idx, tag=f"mix_l{i}"), f"mix_l{i}", mix_sink))

    groups, comm = take_pending()
    finish_groups(groups, _run_comm(comm, name="rs_chips_tail"))
    g_pool_w, g_fox_in, g_fox_o, g_small_flat = done["pool_w"], done["fox_in"], done["fox_o"], done["small"]

    grad_x = dcur.reshape(bsz, seq, d)

    rep_parts = [jnp.concatenate(g_norm_mix), jnp.concatenate(g_norm_ffn), g_pool_scale, g_bf, g_qg, g_kg,
                 jnp.concatenate(g_ffn_dw_b)]
    rep_shapes = [norm_mix.shape, norm_ffn.shape, pool_scale.shape, fox_b_f.shape, fox_q_gain.shape,
                  fox_k_gain.shape, ffn_dw_b.shape]
    rep_rows = _pack_rows([math.prod(s) for s in rep_shapes], 8)
    rep = _all_reduce_small(_pack([p_.reshape(1, -1) for p_ in rep_parts], rep_rows)[0], name="all_reduce_small")
    g_rep = _unpack(rep.reshape(-1), rep_shapes)
    g_sm = _unpack(g_small_flat, small_shapes)

    grads = {
        "norm_mix": g_rep[0], "norm_ffn": g_rep[1],
        "conv_w_in": jnp.stack(g_conv_in), "conv_b_in": g_sm[0], "conv_dw": g_sm[1], "conv_dw_b": g_sm[2],
        "conv_ln_g": g_sm[3], "conv_ln_b": g_sm[4], "conv_w_out": jnp.stack(g_conv_out), "conv_b_out": g_sm[5],
        "pool_w": g_pool_w.reshape(pool_w.shape), "pool_b": g_sm[6], "pool_scale": g_rep[2],
        "fox_w_in": g_fox_in.reshape(fox_w_in.shape), "fox_b_f": g_rep[3], "fox_q_gain": g_rep[4],
        "fox_k_gain": g_rep[5], "fox_w_o": g_fox_o.reshape(fox_w_o.shape),
        "ffn_w_up": jnp.stack(g_up), "ffn_dw": g_sm[7], "ffn_dw_b": g_rep[6], "ffn_w_down": jnp.stack(g_down),
    }
    weights = dict(norm_mix=norm_mix, norm_ffn=norm_ffn, conv_w_in=conv_w_in, conv_b_in=conv_b_in, conv_dw=conv_dw,
                   conv_dw_b=conv_dw_b, conv_ln_g=conv_ln_g, conv_ln_b=conv_ln_b, conv_w_out=conv_w_out,
                   conv_b_out=conv_b_out, pool_w=pool_w, pool_b=pool_b, pool_scale=pool_scale, fox_w_in=fox_w_in,
                   fox_b_f=fox_b_f, fox_q_gain=fox_q_gain, fox_k_gain=fox_k_gain, fox_w_o=fox_w_o, ffn_w_up=ffn_w_up,
                   ffn_dw=ffn_dw, ffn_dw_b=ffn_dw_b, ffn_w_down=ffn_w_down)
    m_in = dict(norm_mix=m_norm_mix, norm_ffn=m_norm_ffn, conv_w_in=m_conv_w_in, conv_b_in=m_conv_b_in,
                conv_dw=m_conv_dw, conv_dw_b=m_conv_dw_b, conv_ln_g=m_conv_ln_g, conv_ln_b=m_conv_ln_b,
                conv_w_out=m_conv_w_out, conv_b_out=m_conv_b_out, pool_w=m_pool_w, pool_b=m_pool_b,
                pool_scale=m_pool_scale, fox_w_in=m_fox_w_in, fox_b_f=m_fox_b_f, fox_q_gain=m_fox_q_gain,
                fox_k_gain=m_fox_k_gain, fox_w_o=m_fox_w_o, ffn_w_up=m_ffn_w_up, ffn_dw=m_ffn_dw,
                ffn_dw_b=m_ffn_dw_b, ffn_w_down=m_ffn_w_down)
    v_in = dict(norm_mix=v_norm_mix, norm_ffn=v_norm_ffn, conv_w_in=v_conv_w_in, conv_b_in=v_conv_b_in,
                conv_dw=v_conv_dw, conv_dw_b=v_conv_dw_b, conv_ln_g=v_conv_ln_g, conv_ln_b=v_conv_ln_b,
                conv_w_out=v_conv_w_out, conv_b_out=v_conv_b_out, pool_w=v_pool_w, pool_b=v_pool_b,
                pool_scale=v_pool_scale, fox_w_in=v_fox_w_in, fox_b_f=v_fox_b_f, fox_q_gain=v_fox_q_gain,
                fox_k_gain=v_fox_k_gain, fox_w_o=v_fox_w_o, ffn_w_up=v_ffn_w_up, ffn_dw=v_ffn_dw,
                ffn_dw_b=v_ffn_dw_b, ffn_w_down=v_ffn_w_down)
    names = list(weights)
    g_out, d_out, m_out, v_out = [], [], [], []
    for nm in names:
        g_, dl_, m_, v_ = _adamw_nd(weights[nm], grads[nm].reshape(weights[nm].shape), m_in[nm], v_in[nm],
                                    name=f"adamw_{nm}")
        g_out.append(g_)
        d_out.append(dl_)
        m_out.append(m_)
        v_out.append(v_)
    return (loss, grad_x, *g_out, *d_out, *m_out, *v_out)
```

```python
import math

import jax
import jax.numpy as jnp
import numpy as np
from jax import lax
from jax.experimental import pallas as pl
from jax.experimental.pallas import tpu as pltpu

F32 = jnp.float32
BF16 = jnp.bfloat16
HI = lax.Precision.HIGHEST
MESH = pl.DeviceIdType.MESH
ANY = pl.BlockSpec(memory_space=pl.ANY)

EPS = 1e-6
HEAD_DIM = 64
LANES = 128
POOL_WINDOWS = (2, 4, 8, 16)
CONV_WIDTH = 31
CONV_HALO = 32
FFN_HALO = 8
FFN_ROWS, FFN_COLS = 256, 1408
POOL_HALO = 16
N_CHIPS = 4
NEG = -1e30

ADAM_LR = 0.001
ADAM_B1 = 0.9
ADAM_B2 = 0.999
ADAM_EPS = 1e-08
ADAM_WD = 0.01
ADAM_STEP = 10

V7X_VMEM_LIMIT_BYTES = 56 * 1024 * 1024


def _cp(*sem):
    return pltpu.CompilerParams(dimension_semantics=sem or None, vmem_limit_bytes=V7X_VMEM_LIMIT_BYTES)


def _tile(n, pref):
    t = min(n, pref)
    assert n % t == 0, (n, pref)
    return t


def _sig(v):
    return jax.nn.sigmoid(v)


def _roll(v, shift):
    n = v.shape[0]
    shift = shift % n
    return v if shift == 0 else pltpu.roll(v, shift, 0)


SUBLANES = 8


def _shifted_back(v, tm, halo):
    rot = [_roll(v, b) for b in range(SUBLANES)]

    def at(s):
        whole = (s // SUBLANES) * SUBLANES
        return rot[s % SUBLANES][halo - whole:halo - whole + tm]

    return at


def _shifted_ahead(v, tm):
    rot = [_roll(v, v.shape[0] - b) for b in range(SUBLANES)]

    def at(s):
        whole = (s // SUBLANES) * SUBLANES
        return rot[s % SUBLANES][whole:whole + tm]

    return at


class _Comm:
    def __init__(self, ins, out_shapes, n_sems, start, wait):
        self.ins, self.out_shapes, self.n_sems, self.start, self.wait = list(ins), list(out_shapes), n_sems, start, wait


def _hosted(body, comm, *, grid, in_specs, out_specs, out_shape, scratch_shapes, sem, name, ins):
    if comm is None:
        return pl.pallas_call(body, grid=grid, in_specs=in_specs, out_specs=out_specs, out_shape=out_shape,
                              scratch_shapes=scratch_shapes, name=name, compiler_params=_cp(*sem))(*ins)
    n_in, n_out, n_scr = len(in_specs), len(out_specs), len(scratch_shapes)
    nci, nco = len(comm.ins), len(comm.out_shapes)

    def wrapped(*refs):
        pos = [0]

        def take(cnt):
            pos[0] += cnt
            return refs[pos[0] - cnt:pos[0]]

        r_in, c_in, r_out, c_out, r_scr = take(n_in), take(nci), take(n_out), take(nco), take(n_scr)
        send_sems, recv_sems = take(2)
        ids = [pl.program_id(ax) for ax in range(len(grid))]
        first, last = ids[0] == 0, ids[0] == grid[0] - 1
        for ax in range(1, len(grid)):
            first = jnp.logical_and(first, ids[ax] == 0)
            last = jnp.logical_and(last, ids[ax] == grid[ax] - 1)

        @pl.when(first)
        def _():
            comm.start(c_in, c_out, send_sems, recv_sems)

        body(*r_in, *r_out, *r_scr)

        @pl.when(last)
        def _():
            comm.wait(c_in, c_out, send_sems, recv_sems)

    outs = pl.pallas_call(
        wrapped, grid=grid, in_specs=list(in_specs) + [ANY] * nci, out_specs=list(out_specs) + [ANY] * nco,
        out_shape=list(out_shape) + comm.out_shapes,
        scratch_shapes=list(scratch_shapes) + [pltpu.SemaphoreType.DMA((comm.n_sems,))] * 2, name=name,
        compiler_params=pltpu.CompilerParams(dimension_semantics=sem, vmem_limit_bytes=V7X_VMEM_LIMIT_BYTES,
                                             has_side_effects=True),
    )(*ins, *comm.ins)
    return list(outs[:n_out]), list(outs[n_out:])


def _run_comm(comm, name):
    def body(*refs):
        nci, nco = len(comm.ins), len(comm.out_shapes)
        c_in, c_out, send_sems, recv_sems = refs[:nci], refs[nci:nci + nco], refs[-2], refs[-1]
        comm.start(c_in, c_out, send_sems, recv_sems)
        comm.wait(c_in, c_out, send_sems, recv_sems)

    return pl.pallas_call(
        body, in_specs=[ANY] * len(comm.ins), out_specs=[ANY] * len(comm.out_shapes), out_shape=comm.out_shapes,
        scratch_shapes=[pltpu.SemaphoreType.DMA((comm.n_sems,))] * 2, name=name,
        compiler_params=pltpu.CompilerParams(has_side_effects=True),
    )(*comm.ins)


def _mm(a, b, *, m, n, k, name, tm=1024, tn=1024, tk=512, ta=False, tb=False, b_stk=None, b_s0=0,
        o_stk=None, o_s0=0, o_slots=None, o_buf=None, bias=None, res=None, out_dtype=F32):
    tm, tn, tk = _tile(m, tm), _tile(n, tn), _tile(k, tk)
    gi, gj, gk = m // tm, n // tn, k // tk
    a_spec = pl.BlockSpec((tk, tm), lambda j, i, kk: (kk, i)) if ta else pl.BlockSpec((tm, tk), lambda j, i, kk: (i, kk))
    if b_stk is None:
        b_spec = pl.BlockSpec((tn, tk), lambda j, i, kk: (j, kk)) if tb else pl.BlockSpec((tk, tn), lambda j, i, kk: (kk, j))
    elif tb:
        assert b_stk % tk == 0
        per = b_stk // tk
        b_spec = pl.BlockSpec((None, tn, tk), lambda j, i, kk: (b_s0 + kk // per, j, kk % per))
    else:
        assert b_stk % tn == 0
        per = b_stk // tn
        b_spec = pl.BlockSpec((None, tk, tn), lambda j, i, kk: (b_s0 + j // per, kk, j % per))
    ins, in_specs = [a, b], [a_spec, b_spec]
    if bias is not None:
        ins.append(bias)
        in_specs.append(pl.BlockSpec((1, tn), lambda j, i, kk: (0, j)))
    if res is not None:
        ins.append(res)
        in_specs.append(pl.BlockSpec((tm, tn), lambda j, i, kk: (i, j)))
    aliases = {}
    if o_stk is None:
        out_shape = jax.ShapeDtypeStruct((m, n), out_dtype)
        o_spec = pl.BlockSpec((tm, tn), lambda j, i, kk: (i, j))
    else:
        assert o_stk % tn == 0
        pero = o_stk // tn
        out_shape = jax.ShapeDtypeStruct((o_slots, m, o_stk), out_dtype)
        o_spec = pl.BlockSpec((None, tm, tn), lambda j, i, kk: (o_s0 + j // pero, i, j % pero))
        if o_buf is not None:
            aliases = {len(ins): 0}
            ins.append(o_buf)
            in_specs.append(ANY)
    has_bias, has_res, has_buf = bias is not None, res is not None, o_buf is not None
    dn = (((0 if ta else 1,), (1 if tb else 0,)), ((), ()))

    def body(*refs):
        a_ref, b_ref = refs[0], refs[1]
        pos = 2
        bias_ref = refs[pos] if has_bias else None
        pos += has_bias
        res_ref = refs[pos] if has_res else None
        pos += has_res + has_buf
        o_ref = refs[pos]
        p = lax.dot_general(a_ref[...].astype(BF16), b_ref[...].astype(BF16), dn, preferred_element_type=F32)

        def finish(acc):
            if has_bias:
                acc = acc + bias_ref[...]
            if has_res:
                acc = acc + res_ref[...]
            o_ref[...] = acc.astype(o_ref.dtype)

        if gk == 1:
            finish(p)
        else:
            acc_ref = refs[pos + 1]
            kk = pl.program_id(2)

            @pl.when(kk == 0)
            def _():
                acc_ref[...] = p

            @pl.when(kk > 0)
            def _():
                acc_ref[...] += p

            @pl.when(kk == gk - 1)
            def _():
                finish(acc_ref[...])

    return pl.pallas_call(
        body, grid=(gj, gi, gk), in_specs=in_specs, out_specs=o_spec, out_shape=out_shape,
        scratch_shapes=[pltpu.VMEM((tm, tn), F32)] if gk > 1 else [],
        input_output_aliases=aliases, name=name,
        compiler_params=_cp("parallel", "parallel", "arbitrary"),
    )(*ins)


def _rms_fwd(x, g, name):
    n, d = x.shape
    tm = _tile(n, 512)

    def body(x_ref, g_ref, h_ref):
        xv = x_ref[...]
        r = lax.rsqrt(jnp.mean(xv * xv, axis=-1, keepdims=True) + EPS)
        h_ref[...] = (xv * r * g_ref[...]).astype(h_ref.dtype)

    return pl.pallas_call(
        body, grid=(n // tm,),
        in_specs=[pl.BlockSpec((tm, d), lambda i: (i, 0)), pl.BlockSpec((1, d), lambda i: (0, 0))],
        out_specs=pl.BlockSpec((tm, d), lambda i: (i, 0)),
        out_shape=jax.ShapeDtypeStruct((n, d), BF16), name=name, compiler_params=_cp("arbitrary"),
    )(x, g)


def _rms_bwd(x, dh, g, dres, name, colsum=False):
    n, d = x.shape
    tm = _tile(n, 512)

    def body(x_ref, dh_ref, g_ref, dres_ref, dx_ref, dg_ref, *rest):
        i = pl.program_id(0)
        xv, dhv = x_ref[...], dh_ref[...]
        r = lax.rsqrt(jnp.mean(xv * xv, axis=-1, keepdims=True) + EPS)
        xn = xv * r
        dxn = dhv * g_ref[...]
        dx_ref[...] = dres_ref[...] + r * (dxn - xn * jnp.mean(dxn * xn, axis=-1, keepdims=True))
        dg = jnp.sum(dhv * xn, axis=0, keepdims=True)

        @pl.when(i == 0)
        def _():
            dg_ref[...] = jnp.zeros_like(dg_ref)
            if colsum:
                rest[0][...] = jnp.zeros_like(rest[0])

        dg_ref[...] += dg
        if colsum:
            rest[0][...] += jnp.sum(dres_ref[...], axis=0, keepdims=True)

    row = pl.BlockSpec((tm, d), lambda i: (i, 0))
    vec = pl.BlockSpec((1, d), lambda i: (0, 0))
    out_shape = [jax.ShapeDtypeStruct((n, d), F32), jax.ShapeDtypeStruct((1, d), F32)]
    out_specs = [row, vec]
    if colsum:
        out_shape.append(jax.ShapeDtypeStruct((1, d), F32))
        out_specs.append(vec)
    return pl.pallas_call(
        body, grid=(n // tm,), in_specs=[row, row, vec, row], out_specs=out_specs, out_shape=out_shape,
        name=name, compiler_params=_cp("arbitrary"),
    )(x, dh, g, dres)


def _loss(y, tgt, name):
    n, d = y.shape
    tm = _tile(n, 512)

    def body(y_ref, t_ref, dy_ref, l_ref):
        i = pl.program_id(0)
        e = y_ref[...] - t_ref[...]
        dy_ref[...] = e / d
        part = 0.5 * jnp.sum(jnp.mean(e * e, axis=-1, keepdims=True), axis=0, keepdims=True)

        @pl.when(i == 0)
        def _():
            l_ref[...] = jnp.zeros_like(l_ref)

        l_ref[...] += part

    row = pl.BlockSpec((tm, d), lambda i: (i, 0))
    return pl.pallas_call(
        body, grid=(n // tm,), in_specs=[row, row],
        out_specs=[row, pl.BlockSpec((1, 1), lambda i: (0, 0))],
        out_shape=[jax.ShapeDtypeStruct((n, d), F32), jax.ShapeDtypeStruct((1, 1), F32)],
        name=name, compiler_params=_cp("arbitrary"),
    )(y, tgt)


def _ffn_specs(n, f, tm, tc, seq):
    hb = FFN_HALO
    cur = pl.BlockSpec((tm, tc), lambda j, i: (i, j))
    prev = pl.BlockSpec((hb, tc), lambda j, i: (jnp.maximum(i * (tm // hb) - 1, 0), j))
    nxt = pl.BlockSpec((hb, tc), lambda j, i: (jnp.minimum((i + 1) * (tm // hb), n // hb - 1), j))
    taps = pl.BlockSpec((3, tc), lambda j, i: (0, j))
    vec = pl.BlockSpec((1, tc), lambda j, i: (0, j))
    return cur, prev, nxt, taps, vec


def _ffn_glu_fwd(uv, ug, wv, wg, bv, bg, seq, name, comm=None):
    n, f = uv.shape
    tm, tc = _tile(seq, FFN_ROWS), _tile(f, FFN_COLS)
    tps = seq // tm
    cur, prev, _, taps, vec = _ffn_specs(n, f, tm, tc, seq)

    def body(uvp, uvc, ugp, ugc, wv_ref, wg_ref, bv_ref, bg_ref, a_ref):
        first = (pl.program_id(1) % tps) == 0

        def conv(p_ref, c_ref, w_ref, b_ref):
            xs = jnp.concatenate([jnp.where(first, 0.0, p_ref[...]), c_ref[...]], axis=0)
            w = w_ref[...]
            y = w[2:3] * xs + w[1:2] * _roll(xs, 1) + w[0:1] * _roll(xs, 2)
            return y[FFN_HALO:] + b_ref[...]

        val = conv(uvp, uvc, wv_ref, bv_ref)
        gate = conv(ugp, ugc, wg_ref, bg_ref)
        a_ref[...] = (gate * _sig(gate) * val).astype(a_ref.dtype)

    return _hosted(
        body, comm, grid=(f // tc, n // tm), in_specs=[prev, cur, prev, cur, taps, taps, vec, vec],
        out_specs=[cur], out_shape=[jax.ShapeDtypeStruct((n, f), BF16)], scratch_shapes=[],
        sem=("parallel", "arbitrary"), name=name, ins=(uv, uv, ug, ug, wv, wg, bv, bg))


def _ffn_glu_bwd(uv, ug, da, wv, wg, bv, bg, seq, name, comm=None):
    n, f = uv.shape
    tm, tc = _tile(seq, FFN_ROWS), _tile(f, FFN_COLS)
    tps = seq // tm
    hb = FFN_HALO
    ext = tm + hb
    cur, prev, nxt, taps, vec = _ffn_specs(n, f, tm, tc, seq)

    def body(uvp, uvc, uvn, ugp, ugc, ugn, da_c, da_n, wv_ref, wg_ref, bv_ref, bg_ref,
             duv_ref, dug_ref, dwv_ref, dwg_ref, dbv_ref, dbg_ref):
        i = pl.program_id(1)
        first = (i % tps) == 0
        last = (i % tps) == tps - 1
        da_e = jnp.concatenate([da_c[...], jnp.where(last, 0.0, da_n[...])], axis=0)

        def taps3(p_ref, c_ref, n_ref):
            xs = jnp.concatenate([jnp.where(first, 0.0, p_ref[...]), c_ref[...], n_ref[...]], axis=0)
            return xs, _roll(xs, 1), _roll(xs, 2)

        xv, xg = taps3(uvp, uvc, uvn), taps3(ugp, ugc, ugn)
        wv_, wg_ = wv_ref[...], wg_ref[...]

        def conv(xs, w, b_ref):
            return (w[2:3] * xs[0] + w[1:2] * xs[1] + w[0:1] * xs[2])[hb:] + b_ref[...]

        val, gate = conv(xv, wv_, bv_ref), conv(xg, wg_, bg_ref)
        sg = _sig(gate)
        dval = da_e * (gate * sg)
        dgate = da_e * val * (sg * (1.0 + gate * (1.0 - sg)))

        def conv_t(dv, w):
            return (w[2:3] * dv + w[1:2] * _roll(dv, ext - 1) + w[0:1] * _roll(dv, ext - 2))[:tm]

        duv_ref[...] = conv_t(dval, wv_).astype(duv_ref.dtype)
        dug_ref[...] = conv_t(dgate, wg_).astype(dug_ref.dtype)

        def tap_grads(d_own, xs):
            return jnp.concatenate(
                [jnp.sum(d_own * xs[2 - kk][hb:hb + tm], axis=0, keepdims=True) for kk in range(3)], axis=0)

        dv_own, dg_own = dval[:tm], dgate[:tm]

        @pl.when(i == 0)
        def _():
            for r in (dwv_ref, dwg_ref, dbv_ref, dbg_ref):
                r[...] = jnp.zeros_like(r)

        dwv_ref[...] += tap_grads(dv_own, xv)
        dwg_ref[...] += tap_grads(dg_own, xg)
        dbv_ref[...] += jnp.sum(dv_own, axis=0, keepdims=True)
        dbg_ref[...] += jnp.sum(dg_own, axis=0, keepdims=True)

    return _hosted(
        body, comm, grid=(f // tc, n // tm),
        in_specs=[prev, cur, nxt, prev, cur, nxt, cur, nxt, taps, taps, vec, vec],
        out_specs=[cur, cur, taps, taps, vec, vec],
        out_shape=[jax.ShapeDtypeStruct((n, f), BF16), jax.ShapeDtypeStruct((n, f), BF16),
                   jax.ShapeDtypeStruct((3, f), F32), jax.ShapeDtypeStruct((3, f), F32),
                   jax.ShapeDtypeStruct((1, f), F32), jax.ShapeDtypeStruct((1, f), F32)],
        scratch_shapes=[], sem=("parallel", "arbitrary"), name=name,
        ins=(uv, uv, uv, ug, ug, ug, da, da, wv, wg, bv, bg))


def _conf_specs(n, d, tm):
    hb = CONV_HALO
    cur = pl.BlockSpec((tm, d), lambda i: (i, 0))
    prev = pl.BlockSpec((hb, d), lambda i: (jnp.maximum(i * (tm // hb) - 1, 0), 0))
    nxt = pl.BlockSpec((hb, d), lambda i: (jnp.minimum((i + 1) * (tm // hb), n // hb - 1), 0))
    taps = pl.BlockSpec((CONV_HALO, d), lambda i: (0, 0))
    vec = pl.BlockSpec((1, d), lambda i: (0, 0))
    return cur, prev, nxt, taps, vec


def _conf_fwd(pa, pg, w, wb, lng, lnb, seq, name):
    n, d = pa.shape
    tm = _tile(seq, 256)
    tps = seq // tm
    hb = CONV_HALO
    cur, prev, _, taps, vec = _conf_specs(n, d, tm)

    def body(pap, pac, pgp, pgc, w_ref, wb_ref, lng_ref, lnb_ref, u_ref, s_ref):
        first = (pl.program_id(0) % tps) == 0
        a = jnp.concatenate([jnp.where(first, 0.0, pap[...]), pac[...]], axis=0)
        g = jnp.concatenate([jnp.where(first, 0.0, pgp[...]), pgc[...]], axis=0)
        z = a * _sig(g)
        back = _shifted_back(z, tm, hb)
        u = w_ref[CONV_WIDTH - 1:CONV_WIDTH, :] * back(0)
        for sh in range(1, CONV_WIDTH):
            u = u + w_ref[CONV_WIDTH - 1 - sh:CONV_WIDTH - sh, :] * back(sh)
        u = u + wb_ref[...]
        mu = jnp.mean(u, axis=-1, keepdims=True)
        uc = u - mu
        var = jnp.mean(uc * uc, axis=-1, keepdims=True)
        ul = uc * lax.rsqrt(var + EPS) * lng_ref[...] + lnb_ref[...]
        u_ref[...] = u
        s_ref[...] = (ul * _sig(ul)).astype(s_ref.dtype)

    return pl.pallas_call(
        body, grid=(n // tm,), in_specs=[prev, cur, prev, cur, taps, vec, vec, vec],
        out_specs=[cur, cur],
        out_shape=[jax.ShapeDtypeStruct((n, d), F32), jax.ShapeDtypeStruct((n, d), BF16)],
        name=name, compiler_params=_cp("arbitrary"),
    )(pa, pa, pg, pg, w, wb, lng, lnb)


def _conf_bwd(u, ds, pa, pg, w, lng, lnb, seq, name, comm=None):
    n, d = u.shape
    tm = _tile(seq, 256)
    tps = seq // tm
    hb = CONV_HALO
    ext = tm + hb
    cur, prev, nxt, taps, vec = _conf_specs(n, d, tm)

    def body(uc_ref, un_ref, dsc_ref, dsn_ref, pap, pac, pgp, pgc, w_ref, lng_ref, lnb_ref,
             dpa_ref, dpg_ref, dw_ref, dwb_ref, dlng_ref, dlnb_ref, dba_ref, dbg_ref):
        i = pl.program_id(0)
        first = (i % tps) == 0
        last = (i % tps) == tps - 1

        @pl.when(i == 0)
        def _():
            for r in (dw_ref, dwb_ref, dlng_ref, dlnb_ref, dba_ref, dbg_ref):
                r[...] = jnp.zeros_like(r)

        ue = jnp.concatenate([uc_ref[...], un_ref[...]], axis=0)
        dse = jnp.concatenate([dsc_ref[...], jnp.where(last, 0.0, dsn_ref[...])], axis=0)
        mu = jnp.mean(ue, axis=-1, keepdims=True)
        cen = ue - mu
        r = lax.rsqrt(jnp.mean(cen * cen, axis=-1, keepdims=True) + EPS)
        xn = cen * r
        ul = xn * lng_ref[...] + lnb_ref[...]
        sg = _sig(ul)
        dul = dse * (sg * (1.0 + ul * (1.0 - sg)))
        dun = dul * lng_ref[...]
        du = r * (dun - jnp.mean(dun, axis=-1, keepdims=True) - xn * jnp.mean(dun * xn, axis=-1, keepdims=True))
        dlng_ref[...] += jnp.sum((dul * xn)[:tm], axis=0, keepdims=True)
        dlnb_ref[...] += jnp.sum(dul[:tm], axis=0, keepdims=True)
        du_own = du[:tm]
        dwb_ref[...] += jnp.sum(du_own, axis=0, keepdims=True)

        ahead = _shifted_ahead(du, tm)
        dz = w_ref[CONV_WIDTH - 1:CONV_WIDTH, :] * ahead(0)
        for sh in range(1, CONV_WIDTH):
            dz = dz + w_ref[CONV_WIDTH - 1 - sh:CONV_WIDTH - sh, :] * ahead(sh)

        a = jnp.concatenate([jnp.where(first, 0.0, pap[...]), pac[...]], axis=0)
        g = jnp.concatenate([jnp.where(first, 0.0, pgp[...]), pgc[...]], axis=0)
        sgg = _sig(g)
        z = a * sgg
        back = _shifted_back(z, tm, hb)
        for sh in range(CONV_WIDTH):
            kk = CONV_WIDTH - 1 - sh
            dw_ref[kk:kk + 1, :] += jnp.sum(du_own * back(sh), axis=0, keepdims=True)

        a_c, sg_c = a[hb:], sgg[hb:]
        da = dz * sg_c
        dg = dz * a_c * sg_c * (1.0 - sg_c)
        dpa_ref[...] = da.astype(dpa_ref.dtype)
        dpg_ref[...] = dg.astype(dpg_ref.dtype)
        dba_ref[...] += jnp.sum(da, axis=0, keepdims=True)
        dbg_ref[...] += jnp.sum(dg, axis=0, keepdims=True)

    vshape = jax.ShapeDtypeStruct((1, d), F32)
    return _hosted(
        body, comm, grid=(n // tm,),
        in_specs=[cur, nxt, cur, nxt, prev, cur, prev, cur, taps, vec, vec],
        out_specs=[cur, cur, taps, vec, vec, vec, vec, vec],
        out_shape=[jax.ShapeDtypeStruct((n, d), BF16), jax.ShapeDtypeStruct((n, d), BF16),
                   jax.ShapeDtypeStruct((CONV_HALO, d), F32), vshape, vshape, vshape, vshape, vshape],
        scratch_shapes=[], sem=("arbitrary",), name=name, ins=(u, u, ds, ds, pa, pa, pg, pg, w, lng, lnb))


def _pool_specs(n, d, tm, gd):
    hb = POOL_HALO
    cur = pl.BlockSpec((tm, d), lambda i: (i, 0))
    prev = pl.BlockSpec((hb, d), lambda i: (jnp.maximum(i * (tm // hb) - 1, 0), 0))
    nxt = pl.BlockSpec((hb, d), lambda i: (jnp.minimum((i + 1) * (tm // hb), n // hb - 1), 0))
    wsp = pl.BlockSpec((len(POOL_WINDOWS), gd, gd), lambda i: (0, 0, 0))
    vec = pl.BlockSpec((1, d), lambda i: (0, 0))
    return cur, prev, nxt, wsp, vec


def _pool_fwd(x, g, w, b, sc, seq, name):
    n, d = x.shape
    gd = d // len(POOL_WINDOWS)
    tm = _tile(seq, 256)
    tps = seq // tm
    hb = POOL_HALO
    cur, prev, _, wsp, vec = _pool_specs(n, d, tm, gd)

    def body(xp, xc, g_ref, w_ref, b_ref, sc_ref, x1_ref, p_ref):
        i = pl.program_id(0)
        first = (i % tps) == 0
        xe = jnp.concatenate([jnp.where(first, 0.0, xp[...]), xc[...]], axis=0)
        r = lax.rsqrt(jnp.mean(xe * xe, axis=-1, keepdims=True) + EPS)
        h = xe * r * g_ref[...]
        t = ((i % tps) * tm + lax.broadcasted_iota(jnp.int32, (tm, 1), 0) + 1).astype(F32)
        ys = []
        for gi, win in enumerate(POOL_WINDOWS):
            hg = h[:, gi * gd:(gi + 1) * gd]
            s, sh = hg, 1
            while sh < win:
                s = s + _roll(s, sh)
                sh *= 2
            p = (s[hb:] / jnp.minimum(t, float(win)) - hg[hb:]).astype(BF16)
            p_ref[:, gi * gd:(gi + 1) * gd] = p
            ys.append(jnp.dot(p, w_ref[gi], preferred_element_type=F32))
        y = jnp.concatenate(ys, axis=1) + b_ref[...]
        x1_ref[...] = xc[...] + y * sc_ref[...]

    return pl.pallas_call(
        body, grid=(n // tm,), in_specs=[prev, cur, vec, wsp, vec, vec], out_specs=[cur, cur],
        out_shape=[jax.ShapeDtypeStruct((n, d), F32), jax.ShapeDtypeStruct((n, d), BF16)],
        name=name, compiler_params=_cp("arbitrary"),
    )(x, x, g, w, b, sc)


def _pool_bwd(dx1, x, p, g, w, b, sc, seq, name):
    n, d = x.shape
    ng = len(POOL_WINDOWS)
    gd = d // ng
    tm = _tile(seq, 256)
    tps = seq // tm
    hb = POOL_HALO
    ext = tm + hb
    cur, _, nxt, wsp, vec = _pool_specs(n, d, tm, gd)

    def body(dc_ref, dn_ref, x_ref, p_ref, g_ref, w_ref, b_ref, sc_ref, dx_ref, dg_ref, dw_ref, db_ref, dsc_ref):
        i = pl.program_id(0)
        last = (i % tps) == tps - 1

        @pl.when(i == 0)
        def _():
            for r_ in (dg_ref, dw_ref, db_ref, dsc_ref):
                r_[...] = jnp.zeros_like(r_)

        dxc = dc_ref[...]
        dxe = jnp.concatenate([dxc, jnp.where(last, 0.0, dn_ref[...])], axis=0)
        dyg = dxe * sc_ref[...]
        t = ((i % tps) * tm + lax.broadcasted_iota(jnp.int32, (ext, 1), 0) + 1).astype(F32)
        dhs = []
        for gi, win in enumerate(POOL_WINDOWS):
            sl = slice(gi * gd, (gi + 1) * gd)
            dyb = dyg[:, sl].astype(BF16)
            wg = w_ref[gi]
            dp = lax.dot_general(dyb, wg, (((1,), (1,)), ((), ())), preferred_element_type=F32)
            s, sh = dp / jnp.minimum(t, float(win)), 1
            while sh < win:
                s = s + _roll(s, ext - sh)
                sh *= 2
            dhs.append((s - dp)[:tm])
            pg = p_ref[:, sl]
            dw_ref[gi] += lax.dot_general(pg, dyb[:tm], (((0,), (0,)), ((), ())), preferred_element_type=F32)
            ypre = jnp.dot(pg, wg, preferred_element_type=F32) + b_ref[:, sl]
            dsc_ref[:, sl] += jnp.sum(dxc[:, sl] * ypre, axis=0, keepdims=True)
            db_ref[:, sl] += jnp.sum(dyg[:tm, sl], axis=0, keepdims=True)
        dh = jnp.concatenate(dhs, axis=1)
        xv = x_ref[...]
        r = lax.rsqrt(jnp.mean(xv * xv, axis=-1, keepdims=True) + EPS)
        xn = xv * r
        dxn = dh * g_ref[...]
        dx_ref[...] = dxc + r * (dxn - xn * jnp.mean(dxn * xn, axis=-1, keepdims=True))
        dg_ref[...] += jnp.sum(dh * xn, axis=0, keepdims=True)

    vshape = jax.ShapeDtypeStruct((1, d), F32)
    return pl.pallas_call(
        body, grid=(n // tm,), in_specs=[cur, nxt, cur, cur, vec, wsp, vec, vec],
        out_specs=[cur, vec, wsp, vec, vec],
        out_shape=[jax.ShapeDtypeStruct((n, d), F32), vshape, jax.ShapeDtypeStruct((ng, gd, gd), F32), vshape, vshape],
        name=name, compiler_params=_cp("arbitrary"),
    )(dx1, dx1, x, p, g, w, b, sc)


def _head_maps(d):
    hd = lax.broadcasted_iota(jnp.int32, (d, LANES), 0) // HEAD_DIM
    col = lax.broadcasted_iota(jnp.int32, (d, LANES), 1)
    gm = (hd == col).astype(F32)
    hd_t = lax.broadcasted_iota(jnp.int32, (LANES, d), 1) // HEAD_DIM
    row = lax.broadcasted_iota(jnp.int32, (LANES, d), 0)
    gt = (hd_t == row).astype(F32)
    return gm, gt


def _bias_placement(nh):
    pq = np.zeros((3 * LANES, nh * HEAD_DIM), np.float32)
    pk = np.zeros((3 * LANES, nh * HEAD_DIM), np.float32)
    oq = np.zeros((1, nh * HEAD_DIM), np.float32)
    ok = np.zeros((1, nh * HEAD_DIM), np.float32)
    for h in range(nh):
        for piece in range(3):
            pq[piece * LANES + h, h * HEAD_DIM + piece] = 1.0
            pk[piece * LANES + h, h * HEAD_DIM + 3 + piece] = -1.0
            oq[0, h * HEAD_DIM + 3 + piece] = 1.0
            ok[0, h * HEAD_DIM + piece] = 1.0
    return jnp.asarray(pq, BF16), jnp.asarray(pk, BF16), jnp.asarray(oq), jnp.asarray(ok)


def _fox_prep_fwd(qkv, fl, bf, qg, kg, seq, name):
    n, d3 = qkv.shape
    d = d3 // 3
    nh = d // HEAD_DIM
    tm = _tile(seq, 256)
    tps = seq // tm
    scale = 1.0 / math.sqrt(HEAD_DIM)
    pq, pk, oq, ok = _bias_placement(nh)

    def body(qkv_ref, fl_ref, bf_ref, qg_ref, kg_ref, pq_ref, pk_ref, oq_ref, ok_ref, q_ref, k_ref, v_ref, carry):
        first = (pl.program_id(0) % tps) == 0
        gm, gt = _head_maps(d)

        def head_norm(xr, gain):
            ss = jnp.dot(xr * xr, gm, precision=HI, preferred_element_type=F32)
            r = lax.rsqrt(ss / HEAD_DIM + EPS)
            return xr * jnp.dot(r, gt, precision=HI, preferred_element_type=F32) * gain

        qs = (head_norm(qkv_ref[:, :d], qg_ref[...]).astype(BF16).astype(F32) * scale).astype(BF16)
        kn = head_norm(qkv_ref[:, d:2 * d], kg_ref[...]).astype(BF16)
        v_ref[...] = qkv_ref[:, 2 * d:].astype(BF16)
        z = fl_ref[...] + bf_ref[...]
        logf = jnp.minimum(z, 0.0) - jnp.log1p(jnp.exp(-jnp.abs(z)))
        tri = (lax.broadcasted_iota(jnp.int32, (tm, tm), 0) >= lax.broadcasted_iota(jnp.int32, (tm, tm), 1)).astype(F32)

        @pl.when(first)
        def _():
            carry[...] = jnp.zeros_like(carry)

        c = jnp.dot(tri, logf, precision=HI, preferred_element_type=F32) + carry[...]
        carry[...] = c[tm - 1:tm, :]
        c1 = c.astype(BF16)
        r1 = c - c1.astype(F32)
        c2 = r1.astype(BF16)
        c3 = (r1 - c2.astype(F32)).astype(BF16)
        pieces = jnp.concatenate([c1, c2, c3], axis=1)
        eq = (jnp.dot(pieces, pq_ref[...], preferred_element_type=F32) + oq_ref[...]).astype(BF16)
        ek = (jnp.dot(pieces, pk_ref[...], preferred_element_type=F32) + ok_ref[...]).astype(BF16)
        for h in range(nh):
            lo, hi = h * HEAD_DIM, (h + 1) * HEAD_DIM
            q_ref[:, 2 * lo:2 * lo + HEAD_DIM] = qs[:, lo:hi]
            q_ref[:, 2 * lo + HEAD_DIM:2 * hi] = eq[:, lo:hi]
            k_ref[:, 2 * lo:2 * lo + HEAD_DIM] = kn[:, lo:hi]
            k_ref[:, 2 * lo + HEAD_DIM:2 * hi] = ek[:, lo:hi]

    row = lambda w: pl.BlockSpec((tm, w), lambda i: (i, 0))
    vec = lambda w: pl.BlockSpec((1, w), lambda i: (0, 0))
    full = lambda a: pl.BlockSpec(a.shape, lambda i: (0, 0))
    return pl.pallas_call(
        body, grid=(n // tm,),
        in_specs=[row(d3), row(LANES), vec(LANES), vec(d), vec(d), full(pq), full(pk), full(oq), full(ok)],
        out_specs=[row(2 * d), row(2 * d), row(d)],
        out_shape=[jax.ShapeDtypeStruct((n, 2 * d), BF16)] * 2 + [jax.ShapeDtypeStruct((n, d), BF16)],
        scratch_shapes=[pltpu.VMEM((1, LANES), F32)], name=name, compiler_params=_cp("arbitrary"),
    )(qkv, fl, bf, qg, kg, pq, pk, oq, ok)


def _fox_prep_bwd(qkv, dq, dk, dv, dc1, dc2, fl, bf, qg, kg, seq, name, comm=None):
    n, d3 = qkv.shape
    d = d3 // 3
    tm = _tile(seq, 256)
    tps = seq // tm
    nt = n // tm

    def body(qkv_ref, dq_ref, dk_ref, dv_ref, dc1_ref, dc2_ref, fl_ref, bf_ref, qg_ref, kg_ref,
             dqkv_ref, dfl_ref, dqg_ref, dkg_ref, dbf_ref, carry):
        i = pl.program_id(0)
        tile = nt - 1 - i
        last = (tile % tps) == tps - 1
        gm, gt = _head_maps(d)

        @pl.when(i == 0)
        def _():
            for r_ in (dqg_ref, dkg_ref, dbf_ref):
                r_[...] = jnp.zeros_like(r_)

        @pl.when(last)
        def _():
            carry[...] = jnp.zeros_like(carry)

        def head_norm_bwd(xr, dy, gain, dgain_ref):
            ss = jnp.dot(xr * xr, gm, precision=HI, preferred_element_type=F32)
            rf = jnp.dot(lax.rsqrt(ss / HEAD_DIM + EPS), gt, precision=HI, preferred_element_type=F32)
            xn = xr * rf
            dgain_ref[...] += jnp.sum(dy * xn, axis=0, keepdims=True)
            dyg = dy * gain
            mean = jnp.dot(dyg * xn, gm, precision=HI, preferred_element_type=F32) / HEAD_DIM
            return rf * (dyg - xn * jnp.dot(mean, gt, precision=HI, preferred_element_type=F32))

        dqkv_ref[:, :d] = head_norm_bwd(qkv_ref[:, :d], dq_ref[...], qg_ref[...], dqg_ref).astype(BF16)
        dqkv_ref[:, d:2 * d] = head_norm_bwd(qkv_ref[:, d:2 * d], dk_ref[...], kg_ref[...], dkg_ref).astype(BF16)
        dqkv_ref[:, 2 * d:] = dv_ref[...].astype(BF16)

        dc = dc1_ref[...] + dc2_ref[...]
        tri = (lax.broadcasted_iota(jnp.int32, (tm, tm), 0) <= lax.broadcasted_iota(jnp.int32, (tm, tm), 1)).astype(F32)
        dlog = jnp.dot(tri, dc, precision=HI, preferred_element_type=F32) + carry[...]
        carry[...] = dlog[0:1, :]
        dfl = dlog * (1.0 - _sig(fl_ref[...] + bf_ref[...]))
        dfl_ref[...] = dfl.astype(BF16)
        dbf_ref[...] += jnp.sum(dfl, axis=0, keepdims=True)

    row = lambda w: pl.BlockSpec((tm, w), lambda i: (nt - 1 - i, 0))
    vec = lambda w: pl.BlockSpec((1, w), lambda i: (0, 0))
    return _hosted(
        body, comm, grid=(nt,),
        in_specs=[row(d3), row(d), row(d), row(d), row(LANES), row(LANES), row(LANES), vec(LANES), vec(d), vec(d)],
        out_specs=[row(d3), row(LANES), vec(d), vec(d), vec(LANES)],
        out_shape=[jax.ShapeDtypeStruct((n, d3), BF16), jax.ShapeDtypeStruct((n, LANES), BF16),
                   jax.ShapeDtypeStruct((1, d), F32), jax.ShapeDtypeStruct((1, d), F32),
                   jax.ShapeDtypeStruct((1, LANES), F32)],
        scratch_shapes=[pltpu.VMEM((1, LANES), F32)], sem=("arbitrary",), name=name,
        ins=(qkv, dq, dk, dv, dc1, dc2, fl, bf, qg, kg))


def _attn_specs(bsz, seq, t):
    nb = seq // t
    blk = lambda w: pl.BlockSpec((t, w), lambda b, h, i: (b * nb + i, h))
    full = lambda w: pl.BlockSpec((seq, w), lambda b, h, i: (b, h))
    col = pl.BlockSpec((None, None, t, 2), lambda b, h, i: (b, h, i, 0))
    rows = pl.BlockSpec((None, None, nb, 2, t), lambda b, h, i: (b, h, 0, 0, 0))
    return nb, blk, full, col, rows


_NT = (((1,), (1,)), ((), ()))
ATTN_TILE = 512


def _head_lanes(t, hh):
    lane = lax.broadcasted_iota(jnp.int32, (t, LANES), 1)
    return (lane < HEAD_DIM) if hh == 0 else (lane >= HEAD_DIM)


def _flash_fwd(qa, ka, v, bsz, seq, name):
    n, d = v.shape
    hp = d // LANES
    t = _tile(seq, ATTN_TILE)
    nb, blk, full, col, _ = _attn_specs(bsz, seq, t)

    def body(q_ref, k_ref, v_ref, o_ref, lse_ref):
        i = pl.program_id(2)
        causal = lax.broadcasted_iota(jnp.int32, (t, t), 0) >= lax.broadcasted_iota(jnp.int32, (t, t), 1)

        def block(j, carry, masked):
            rs = pl.ds(pl.multiple_of(j * t, t), t)
            vj = v_ref[rs, :]
            out = []
            for hh in range(2):
                m, l, acc = carry[hh]
                hs = slice(hh * LANES, (hh + 1) * LANES)
                sc = lax.dot_general(q_ref[:, hs], k_ref[rs, hs], _NT, preferred_element_type=F32)
                if masked:
                    sc = jnp.where(causal, sc, NEG)
                mn = jnp.maximum(m, jnp.max(sc, axis=-1, keepdims=True))
                p = jnp.exp(sc - mn)
                al = jnp.exp(m - mn)
                l = al * l + jnp.sum(p, axis=-1, keepdims=True)
                acc = al * acc + jnp.dot(p.astype(BF16), vj, preferred_element_type=F32)
                out.append((mn, l, acc))
            return tuple(out)

        init = tuple((jnp.full((t, 1), NEG, F32), jnp.zeros((t, 1), F32), jnp.zeros((t, LANES), F32))
                     for _ in range(2))
        carry = lax.fori_loop(0, i, lambda j, c: block(j, c, False), init)
        (m0, l0, a0), (m1, l1, a1) = block(i, carry, True)
        o_ref[...] = jnp.where(_head_lanes(t, 0), a0 / l0, a1 / l1)
        lse_ref[:, 0:1] = m0 + jnp.log(l0)
        lse_ref[:, 1:2] = m1 + jnp.log(l1)

    return pl.pallas_call(
        body, grid=(bsz, hp, nb), in_specs=[blk(2 * LANES), full(2 * LANES), full(LANES)],
        out_specs=[blk(LANES), col],
        out_shape=[jax.ShapeDtypeStruct((n, d), F32), jax.ShapeDtypeStruct((bsz, hp, seq, 2), F32)],
        name=name, compiler_params=_cp("parallel", "parallel", "arbitrary"),
    )(qa, ka, v)


def _flash_bwd(qa, ka, v, do, o, lse_row, bsz, seq, name):
    n, d = v.shape
    hp = d // LANES
    t = _tile(seq, ATTN_TILE)
    nb, blk, full, col, rows = _attn_specs(bsz, seq, t)
    scale = 1.0 / math.sqrt(HEAD_DIM)
    tn_ = (((0,), (0,)), ((), ()))

    def body(k_ref, v_ref, q_ref, do_ref, o_ref, lse_ref, dq_ref, dcc_ref, dk_ref, dv_ref, dck_ref, dqa, dl):
        j = pl.program_id(2)
        causal = lax.broadcasted_iota(jnp.int32, (t, t), 1) >= lax.broadcasted_iota(jnp.int32, (t, t), 0)
        heads = [_head_lanes(t, 0), _head_lanes(t, 1)]

        @pl.when(j == 0)
        def _():
            dqa[...] = jnp.zeros_like(dqa)
            ones = jnp.ones((8, LANES), F32)
            for ib in range(nb):
                rs = slice(ib * t, (ib + 1) * t)
                prod = do_ref[rs, :] * o_ref[rs, :]
                for hh in range(2):
                    dl[ib, hh:hh + 1, :] = lax.dot_general(ones, jnp.where(heads[hh], prod, 0.0), _NT, precision=HI,
                                                           preferred_element_type=F32)[0:1]

        vj = v_ref[...]

        def block(i, carry, masked):
            rs = pl.ds(pl.multiple_of(i * t, t), t)
            doi = do_ref[rs, :]
            dks, dvp = list(carry[:2]), carry[2]
            for hh in range(2):
                hs = slice(hh * LANES, (hh + 1) * LANES)
                kh, qi = k_ref[:, hs], q_ref[rs, hs]
                dom = jnp.where(heads[hh], doi, 0.0).astype(BF16)
                st = lax.dot_general(kh, qi, _NT, preferred_element_type=F32)
                if masked:
                    st = jnp.where(causal, st, NEG)
                pt = jnp.exp(st - lse_ref[i, hh:hh + 1, :])
                dvp = dvp + jnp.dot(pt.astype(BF16), dom, preferred_element_type=F32)
                dpt = lax.dot_general(vj, dom, _NT, preferred_element_type=F32)
                dsb = (pt * (dpt - dl[i, hh:hh + 1, :])).astype(BF16)
                dks[hh] = dks[hh] + jnp.dot(dsb, qi, preferred_element_type=F32)
                dqa[rs, hs] += lax.dot_general(dsb, kh, tn_, preferred_element_type=F32)
            return dks[0], dks[1], dvp

        zero = jnp.zeros((t, LANES), F32)
        carry = block(j, (zero, zero, zero), True)
        dk0, dk1, dvp = lax.fori_loop(j + 1, nb, lambda i, c: block(i, c, False), carry)
        dk_ref[...] = jnp.where(heads[0], dk0, pltpu.roll(dk1, HEAD_DIM, 1))
        dv_ref[...] = dvp
        dck_ref[:, 0:1] = -dk0[:, HEAD_DIM + 3:HEAD_DIM + 4]
        dck_ref[:, 1:2] = -dk1[:, HEAD_DIM + 3:HEAD_DIM + 4]

        @pl.when(j == nb - 1)
        def _():
            first = lax.broadcasted_iota(jnp.int32, (seq, LANES), 1) < HEAD_DIM
            dq_ref[...] = jnp.where(first, dqa[:, :LANES], pltpu.roll(dqa[:, LANES:], HEAD_DIM, 1)) * scale
            for hh in range(2):
                lo = hh * LANES + HEAD_DIM
                dcc_ref[:, hh:hh + 1] = dqa[:, lo:lo + 1]

    whole_col = pl.BlockSpec((None, None, seq, 2), lambda b, h, i: (b, h, 0, 0))
    cshape = jax.ShapeDtypeStruct((bsz, hp, seq, 2), F32)
    nd = jax.ShapeDtypeStruct((n, d), F32)
    return pl.pallas_call(
        body, grid=(bsz, hp, nb),
        in_specs=[blk(2 * LANES), blk(LANES), full(2 * LANES), full(LANES), full(LANES), rows],
        out_specs=[full(LANES), whole_col, blk(LANES), blk(LANES), col],
        out_shape=[nd, cshape, nd, nd, cshape],
        scratch_shapes=[pltpu.VMEM((seq, 2 * LANES), F32), pltpu.VMEM((nb, 2, t), F32)],
        name=name, compiler_params=_cp("parallel", "parallel", "arbitrary"),
    )(ka, v, qa, do, o, lse_row)


def _adamw(w, g, m, v, name):
    r, c = w.shape
    tr = r
    for cand in (512, 256, 128, 64, 32, 16, 8):
        if r % cand == 0 and r > cand and cand * c * 4 <= 4 * 1024 * 1024:
            tr = cand
            break

    def body(w_ref, g_ref, m_ref, v_ref, d_ref, m2_ref, v2_ref):
        gv = g_ref[...]
        m2 = ADAM_B1 * m_ref[...] + (1.0 - ADAM_B1) * gv
        v2 = ADAM_B2 * v_ref[...] + (1.0 - ADAM_B2) * jnp.square(gv)
        m_hat = m2 / (1.0 - ADAM_B1 ** ADAM_STEP)
        v_hat = v2 / (1.0 - ADAM_B2 ** ADAM_STEP)
        d_ref[...] = -ADAM_LR * (m_hat / (jnp.sqrt(v_hat) + ADAM_EPS) + ADAM_WD * w_ref[...])
        m2_ref[...] = m2
        v2_ref[...] = v2

    blk = pl.BlockSpec((tr, c), lambda i: (i, 0))
    shp = jax.ShapeDtypeStruct((r, c), F32)
    return pl.pallas_call(
        body, grid=(r // tr,), in_specs=[blk] * 4, out_specs=[blk] * 3, out_shape=[shp] * 3,
        name=name, compiler_params=_cp("parallel"),
    )(w, g, m, v)


def _adamw_nd(w, g, m, v, name):
    shape = w.shape
    two = (math.prod(shape[:-1]), shape[-1])
    d_, m_, v_ = _adamw(w.reshape(two), g.reshape(two), m.reshape(two), v.reshape(two), name)
    return g.reshape(shape), d_.reshape(shape), m_.reshape(shape), v_.reshape(shape)


def _place():
    x, y, c = lax.axis_index("x"), lax.axis_index("y"), lax.axis_index("c")
    chips = [(1 - x, y), (x, 1 - y), (1 - x, 1 - y)]
    return x, y, c, chips


def _gather_chips(shards):
    nt = len(shards)
    halves = [s.shape[0] // 2 for s in shards]

    def copies(ins, outs, send_sems, recv_sems):
        x, y, c, chips = _place()
        cps = []
        for t in range(nt):
            rows = pl.ds(c * halves[t], halves[t])
            for jj, (cx, cy) in enumerate(chips):
                cps.append(pltpu.make_async_remote_copy(
                    src_ref=ins[t].at[rows, :], dst_ref=outs[t].at[2 * x + y, rows, :], send_sem=send_sems.at[3 * t + jj],
                    recv_sem=recv_sems.at[3 * t + jj], device_id=(cx, cy, c), device_id_type=MESH))
        return cps

    def start(*refs):
        for cp in copies(*refs):
            cp.start()

    def wait(*refs):
        for cp in copies(*refs):
            cp.wait()

    return _Comm(shards, [jax.ShapeDtypeStruct((N_CHIPS,) + s.shape, s.dtype) for s in shards], 3 * nt, start, wait)


def _gather_sibling(shards, bufs, name):
    nt = len(shards)
    halves = [s.shape[0] // 2 for s in shards]

    def body(*refs):
        ins, outs = refs[:nt], refs[2 * nt:3 * nt]
        send_sems, recv_sems = refs[3 * nt:]
        x, y, c, chips = _place()
        sibling = (x, y, 1 - c)

        def copy(t, jj, hf):
            cx, cy = chips[jj]
            region = outs[t].at[2 * cx + cy, pl.ds(hf * halves[t], halves[t]), :]
            return pltpu.make_async_remote_copy(src_ref=region, dst_ref=region, send_sem=send_sems.at[4 * t + jj],
                                                recv_sem=recv_sems.at[4 * t + jj], device_id=sibling,
                                                device_id_type=MESH)

        def own(t):
            return pltpu.make_async_remote_copy(src_ref=ins[t], dst_ref=outs[t].at[2 * x + y],
                                                send_sem=send_sems.at[4 * t + 3], recv_sem=recv_sems.at[4 * t + 3],
                                                device_id=sibling, device_id_type=MESH)

        sends = [copy(t, jj, c) for t in range(nt) for jj in range(3)] + [own(t) for t in range(nt)]
        for cp in sends:
            cp.start()
        for t in range(nt):
            for jj in range(3):
                copy(t, jj, 1 - c).wait_recv()
            own(t).wait_recv()
        for cp in sends:
            cp.wait_send()

    return pl.pallas_call(
        body, in_specs=[ANY] * (2 * nt), out_specs=[ANY] * nt,
        out_shape=[jax.ShapeDtypeStruct(b.shape, b.dtype) for b in bufs],
        scratch_shapes=[pltpu.SemaphoreType.DMA((4 * nt,)), pltpu.SemaphoreType.DMA((4 * nt,))],
        input_output_aliases={nt + t: t for t in range(nt)}, name=name,
        compiler_params=pltpu.CompilerParams(has_side_effects=True),
    )(*shards, *bufs)


def _send_sibling_halves(grads, name):
    nt = len(grads)
    halves = [g.shape[1] // 2 for g in grads]

    def body(*refs):
        ins, outs = refs[:nt], refs[nt:2 * nt]
        send_sems, recv_sems = refs[2 * nt:]
        x, y, c, _ = _place()
        cps = []
        for t in range(nt):
            src = ins[t].at[:, pl.ds((1 - c) * halves[t], halves[t]), :]
            cps.append(pltpu.make_async_remote_copy(src_ref=src, dst_ref=outs[t], send_sem=send_sems.at[t],
                                                    recv_sem=recv_sems.at[t], device_id=(x, y, 1 - c),
                                                    device_id_type=MESH))
        for cp in cps:
            cp.start()
        for cp in cps:
            cp.wait()

    return pl.pallas_call(
        body, in_specs=[ANY] * nt, out_specs=[ANY] * nt,
        out_shape=[jax.ShapeDtypeStruct((N_CHIPS, h, g.shape[2]), g.dtype) for g, h in zip(grads, halves)],
        scratch_shapes=[pltpu.SemaphoreType.DMA((nt,)), pltpu.SemaphoreType.DMA((nt,))],
        name=name, compiler_params=pltpu.CompilerParams(has_side_effects=True),
    )(*grads)


def _add_sibling(g, r1, c_idx, out_dtype, name):
    s, r, w = g.shape
    rh = r // 2
    tr = rh
    for cand in (512, 256, 128, 64, 32, 16, 8):
        if rh % cand == 0 and cand * w * 4 <= 4 * 1024 * 1024:
            tr = cand
            break
    per = rh // tr

    def body(c_ref, g_ref, r_ref, o_ref):
        o_ref[...] = (g_ref[...] + r_ref[...]).astype(o_ref.dtype)

    return pl.pallas_call(
        body,
        grid_spec=pltpu.PrefetchScalarGridSpec(
            num_scalar_prefetch=1, grid=(s, per),
            in_specs=[pl.BlockSpec((None, tr, w), lambda a, b, c_ref: (a, c_ref[0] * per + b, 0)),
                      pl.BlockSpec((None, tr, w), lambda a, b, c_ref: (a, b, 0))],
            out_specs=pl.BlockSpec((None, tr, w), lambda a, b, c_ref: (a, b, 0))),
        out_shape=jax.ShapeDtypeStruct((s, rh, w), out_dtype), name=name,
        compiler_params=_cp("parallel", "parallel"),
    )(c_idx, g, r1)


def _chip_slots(parts):
    nt = len(parts)

    def copies(ins, outs, send_sems, recv_sems):
        x, y, c, chips = _place()
        return [pltpu.make_async_remote_copy(
            src_ref=ins[t].at[2 * cx + cy], dst_ref=outs[t].at[jj], send_sem=send_sems.at[3 * t + jj],
            recv_sem=recv_sems.at[3 * t + jj], device_id=(cx, cy, c), device_id_type=MESH)
            for t in range(nt) for jj, (cx, cy) in enumerate(chips)]

    def start(*refs):
        for cp in copies(*refs):
            cp.start()

    def wait(*refs):
        for cp in copies(*refs):
            cp.wait()

    return _Comm(parts, [jax.ShapeDtypeStruct((3,) + p.shape[1:], p.dtype) for p in parts], 3 * nt, start, wait)


def _add_chips(part, r2, place_idx, name):
    s, rh, w = part.shape
    tr = rh
    for cand in (512, 256, 128, 64, 32, 16, 8):
        if rh % cand == 0 and cand * w * 4 <= 4 * 1024 * 1024:
            tr = cand
            break
    per = rh // tr

    def body(b_ref, p_ref, r0_ref, r1_ref, r2_ref, o_ref):
        up = lambda ref: ref[...].astype(F32)
        o_ref[...] = ((up(p_ref) + up(r0_ref)) + up(r1_ref)) + up(r2_ref)

    return pl.pallas_call(
        body,
        grid_spec=pltpu.PrefetchScalarGridSpec(
            num_scalar_prefetch=1, grid=(per,),
            in_specs=[pl.BlockSpec((None, tr, w), lambda a, b_ref: (b_ref[0], a, 0))]
            + [pl.BlockSpec((None, tr, w), lambda a, b_ref, jj=jj: (jj, a, 0)) for jj in range(3)],
            out_specs=pl.BlockSpec((tr, w), lambda a, b_ref: (b_ref[1] * per + a, 0))),
        out_shape=jax.ShapeDtypeStruct((2 * rh, w), F32), name=name, compiler_params=_cp("parallel"),
    )(place_idx, part, r2, r2, r2)


def _swap_halves(bufs, name):
    nt = len(bufs)

    def body(*refs):
        outs = refs[nt:2 * nt]
        send_sems, recv_sems = refs[2 * nt:]
        x, y, c, _ = _place()

        def copy(t, hf):
            rh = outs[t].shape[0] // 2
            region = outs[t].at[pl.ds(hf * rh, rh), :]
            return pltpu.make_async_remote_copy(src_ref=region, dst_ref=region, send_sem=send_sems.at[t],
                                                recv_sem=recv_sems.at[t], device_id=(x, y, 1 - c),
                                                device_id_type=MESH)

        sends = [copy(t, c) for t in range(nt)]
        for cp in sends:
            cp.start()
        for t in range(nt):
            copy(t, 1 - c).wait_recv()
        for cp in sends:
            cp.wait_send()

    return pl.pallas_call(
        body, in_specs=[ANY] * nt, out_specs=[ANY] * nt,
        out_shape=[jax.ShapeDtypeStruct(b.shape, b.dtype) for b in bufs],
        scratch_shapes=[pltpu.SemaphoreType.DMA((nt,)), pltpu.SemaphoreType.DMA((nt,))],
        input_output_aliases={t: t for t in range(nt)}, name=name,
        compiler_params=pltpu.CompilerParams(has_side_effects=True),
    )(*bufs)


def _rs_begin(grads, wire, c_idx, tag):
    r1 = _send_sibling_halves(grads, name=f"rs_sibling_{tag}")
    return [_add_sibling(g, r, c_idx, wire[t], name=f"rs_add_sibling_{tag}_{t}")
            for t, (g, r) in enumerate(zip(grads, r1))]


def _rs_finish(parts, r2, chip_idx, tag):
    fin = [_add_chips(p, r, chip_idx, name=f"rs_add_chips_{tag}_{t}") for t, (p, r) in enumerate(zip(parts, r2))]
    return _swap_halves(fin, name=f"rs_swap_{tag}")


def _all_reduce_small(v, name):
    r, w = v.shape

    def body(v_ref, o_ref, buf, send_sems, recv_sems):
        x, y, c, _ = _place()
        me = 4 * x + 2 * y + c
        buf[me] = v_ref[...]
        cps = []
        for kk in range(1, 8):
            peer = (x ^ ((kk >> 2) & 1), y ^ ((kk >> 1) & 1), c ^ (kk & 1))
            cps.append(pltpu.make_async_remote_copy(src_ref=v_ref, dst_ref=buf.at[me], send_sem=send_sems.at[kk - 1],
                                                    recv_sem=recv_sems.at[kk - 1], device_id=peer, device_id_type=MESH))
        for cp in cps:
            cp.start()
        for kk in range(1, 8):
            pltpu.make_async_remote_copy(src_ref=v_ref, dst_ref=buf.at[me ^ kk], send_sem=send_sems.at[kk - 1],
                                         recv_sem=recv_sems.at[kk - 1], device_id=(x, y, c),
                                         device_id_type=MESH).wait_recv()
        for cp in cps:
            cp.wait_send()
        acc = buf[0]
        for dev in range(1, 8):
            acc = acc + buf[dev]
        o_ref[...] = acc

    vm = pl.BlockSpec(memory_space=pltpu.VMEM)
    return pl.pallas_call(
        body, in_specs=[vm], out_specs=vm, out_shape=jax.ShapeDtypeStruct((r, w), F32),
        scratch_shapes=[pltpu.VMEM((8, r, w), F32), pltpu.SemaphoreType.DMA((7,)), pltpu.SemaphoreType.DMA((7,))],
        name=name, compiler_params=pltpu.CompilerParams(has_side_effects=True),
    )(v)


def _to_shards(a, axis=-1):
    axis = axis % a.ndim
    shp = a.shape
    a = a.reshape(shp[:axis] + (N_CHIPS, shp[axis] // N_CHIPS) + shp[axis + 1:])
    return jnp.moveaxis(a, axis, 0).reshape(N_CHIPS, -1)


def _from_shards(s, shard_shape, axis=-1):
    axis = axis % len(shard_shape)
    a = jnp.moveaxis(s.reshape((N_CHIPS,) + tuple(shard_shape)), 0, axis)
    return a.reshape(tuple(shard_shape[:axis]) + (N_CHIPS * shard_shape[axis],) + tuple(shard_shape[axis + 1:]))


def _pack(vecs, rows):
    flat = jnp.concatenate([v.reshape(v.shape[0], -1) if v.ndim > 1 else v.reshape(1, -1) for v in vecs], axis=1)
    lead = flat.shape[0]
    flat = jnp.pad(flat, ((0, 0), (0, rows * LANES - flat.shape[1])))
    return flat.reshape(lead, rows, LANES)


def _pack_rows(sizes, mult):
    total = sum(sizes)
    rows = -(-total // LANES)
    return -(-rows // mult) * mult


def _unpack(flat, shapes):
    out, pos = [], 0
    for shp in shapes:
        sz = math.prod(shp)
        out.append(flat[..., pos:pos + sz].reshape(flat.shape[:-1] + tuple(shp)))
        pos += sz
    return out


def _row_layout(col, t):
    bsz, hp, seq, _ = col.shape
    return col.reshape(bsz, hp, seq // t, t, 2).transpose(0, 1, 2, 4, 3)


def _from_col_layout(col):
    bsz, hp, seq, _ = col.shape
    a = col.transpose(0, 2, 1, 3).reshape(bsz * seq, 2 * hp)
    return jnp.pad(a, ((0, 0), (0, LANES - 2 * hp)))


def kernel(x, norm_mix, norm_ffn, conv_w_in, conv_b_in, conv_dw, conv_dw_b, conv_ln_g, conv_ln_b, conv_w_out, conv_b_out, pool_w, pool_b, pool_scale, fox_w_in, fox_b_f, fox_q_gain, fox_k_gain, fox_w_o, ffn_w_up, ffn_dw, ffn_dw_b, ffn_w_down, loss_target, m_norm_mix, m_norm_ffn, m_conv_w_in, m_conv_b_in, m_conv_dw, m_conv_dw_b, m_conv_ln_g, m_conv_ln_b, m_conv_w_out, m_conv_b_out, m_pool_w, m_pool_b, m_pool_scale, m_fox_w_in, m_fox_b_f, m_fox_q_gain, m_fox_k_gain, m_fox_w_o, m_ffn_w_up, m_ffn_dw, m_ffn_dw_b, m_ffn_w_down, v_norm_mix, v_norm_ffn, v_conv_w_in, v_conv_b_in, v_conv_dw, v_conv_dw_b, v_conv_ln_g, v_conv_ln_b, v_conv_w_out, v_conv_b_out, v_pool_w, v_pool_b, v_pool_scale, v_fox_w_in, v_fox_b_f, v_fox_q_gain, v_fox_k_gain, v_fox_w_o, v_ffn_w_up, v_ffn_dw, v_ffn_dw_b, v_ffn_w_down):
    bsz, seq, d = x.shape
    n = bsz * seq
    depth = norm_mix.shape[0]
    n_conv, n_pool, n_fox = conv_w_in.shape[0], pool_w.shape[0], fox_w_in.shape[0]
    f2 = ffn_dw_b.shape[1]
    f = f2 // 2
    nh = d // HEAD_DIM
    hp = d // LANES
    ng = len(POOL_WINDOWS)
    gd = d // ng
    c_idx = lax.axis_index("c").astype(jnp.int32).reshape(1)
    chip_idx = jnp.stack([2 * lax.axis_index("x") + lax.axis_index("y"), lax.axis_index("c")]).astype(jnp.int32)

    small_shapes = [conv_b_in.shape, conv_dw.shape, conv_dw_b.shape, conv_ln_g.shape, conv_ln_b.shape,
                    conv_b_out.shape, pool_b.shape, ffn_dw.shape]
    small_rows = _pack_rows([math.prod(s) for s in small_shapes], 16)
    small = _pack([v.reshape(1, -1) for v in (conv_b_in, conv_dw, conv_dw_b, conv_ln_g, conv_ln_b, conv_b_out,
                                                pool_b, ffn_dw)], small_rows)[0]
    def layer_shards(i):
        kind, j = i % 3, i // 3
        shards = [ffn_w_up[i].astype(BF16), ffn_w_down[i].astype(BF16)]
        if kind == 0:
            shards += [conv_w_in[j].astype(BF16), conv_w_out[j].astype(BF16)]
        elif kind == 1:
            shards += [pool_w[j].reshape(ng * (gd // N_CHIPS), gd).astype(BF16)]
        else:
            shards += [fox_w_in[j].astype(BF16), fox_w_o[j].astype(BF16)]
        if i == 0:
            shards.append(small)
        return shards

    gathered = [None] * depth
    gathered[0] = _gather_sibling(layer_shards(0), _run_comm(_gather_chips(layer_shards(0)), name="gather_chips_l0"),
                                  name="gather_sibling_l0")
    small_all = gathered[0][-1].reshape(N_CHIPS, -1)
    sm = _unpack(small_all, small_shapes)
    axes = [-1] * 8
    b_in_f, dw_f, dw_b_f, ln_g_f, ln_b_f, b_out_f, pool_b_f, ffn_dw_f = [
        _from_shards(s_.reshape(N_CHIPS, -1), shp, ax) for s_, shp, ax in zip(sm, small_shapes, axes)]

    xs = x.reshape(n, d)
    tgt = loss_target.reshape(n, d)
    vec = lambda a: a.reshape(1, -1)

    saved = []
    cur = xs
    for i in range(depth):
        kind, j = i % 3, i // 3
        wts = gathered[i]
        sv = {"x_in": cur}
        if kind == 0:
            w_in, w_out = wts[2], wts[3].reshape(d, d)
            wcol = w_in.shape[2]
            h = _rms_fwd(cur, vec(norm_mix[i]), name=f"rms_mix_l{i}")
            pa = _mm(h, w_in, m=n, n=d, k=d, tk=d, tn=wcol, b_stk=wcol, b_s0=0, bias=vec(b_in_f[j, :d]),
                     name=f"conv_in_a_l{i}")
            pg = _mm(h, w_in, m=n, n=d, k=d, tk=d, tn=wcol, b_stk=wcol, b_s0=2, bias=vec(b_in_f[j, d:]),
                     name=f"conv_in_g_l{i}")
            taps = jnp.pad(dw_f[j], ((0, CONV_HALO - CONV_WIDTH), (0, 0)))
            u, s_ = _conf_fwd(pa, pg, taps, vec(dw_b_f[j]), vec(ln_g_f[j]), vec(ln_b_f[j]), seq, name=f"conf_fwd_l{i}")
            cur = _mm(s_, w_out, m=n, n=d, k=d, tk=d, bias=vec(b_out_f[j]), res=cur, name=f"conv_out_l{i}")
            sv.update(h=h, pa=pa, pg=pg, u=u, s=s_, taps=taps)
        elif kind == 1:
            pw = wts[2].reshape(N_CHIPS, ng, gd // N_CHIPS, gd).transpose(1, 0, 2, 3).reshape(ng, gd, gd)
            cur, p = _pool_fwd(cur, vec(norm_mix[i]), pw, vec(pool_b_f[j]), vec(pool_scale[j]), seq,
                               name=f"pool_fwd_l{i}")
            sv.update(p=p, pw=pw)
        else:
            w_in = wts[2].transpose(1, 0, 2).reshape(d, -1)
            w_qkv = w_in[:, :3 * d]
            w_f = jnp.pad(w_in[:, 3 * d:], ((0, 0), (0, LANES - nh)))
            w_o = wts[3].reshape(d, d)
            bf = jnp.pad(vec(fox_b_f[j]), ((0, 0), (0, LANES - nh)))
            qg, kg = jnp.tile(vec(fox_q_gain[j]), (1, nh)), jnp.tile(vec(fox_k_gain[j]), (1, nh))
            h = _rms_fwd(cur, vec(norm_mix[i]), name=f"rms_mix_l{i}")
            qkv = _mm(h, w_qkv, m=n, n=3 * d, k=d, tk=d, tn=d, name=f"fox_qkv_l{i}")
            fl = _mm(h, w_f, m=n, n=LANES, k=d, tk=d, name=f"fox_fl_l{i}")
            qa, ka, v = _fox_prep_fwd(qkv, fl, bf, qg, kg, seq, name=f"fox_prep_l{i}")
            o, lse = _flash_fwd(qa, ka, v, bsz, seq, name=f"fox_attn_l{i}")
            cur = _mm(o, w_o, m=n, n=d, k=d, tk=d, res=cur, name=f"fox_out_l{i}")
            sv.update(h=h, qkv=qkv, fl=fl, qa=qa, ka=ka, v=v, o=o, lse=lse, w_qkv=w_qkv, w_f=w_f, w_o=w_o, bf=bf,
                      qg=qg, kg=kg)
        w_up, w_down = wts[0], wts[1].reshape(f, d)
        ucol = w_up.shape[2]
        sv["x_mid"] = cur
        h2 = _rms_fwd(cur, vec(norm_ffn[i]), name=f"rms_ffn_l{i}")
        uv = _mm(h2, w_up, m=n, n=f, k=d, tk=d, tn=ucol, b_stk=ucol, b_s0=0, name=f"ffn_up_v_l{i}")
        ug = _mm(h2, w_up, m=n, n=f, k=d, tk=d, tn=ucol, b_stk=ucol, b_s0=2, name=f"ffn_up_g_l{i}")
        fdw, fdb = ffn_dw_f[i], ffn_dw_b[i]
        glu_args = (uv, ug, fdw[:, :f], fdw[:, f:], vec(fdb[:f]), vec(fdb[f:]), seq)
        if i + 1 < depth:
            nxt_shards = layer_shards(i + 1)
            (a_,), landed = _ffn_glu_fwd(*glu_args, name=f"ffn_glu_l{i}", comm=_gather_chips(nxt_shards))
            gathered[i + 1] = _gather_sibling(nxt_shards, landed, name=f"gather_sibling_l{i + 1}")
        else:
            (a_,) = _ffn_glu_fwd(*glu_args, name=f"ffn_glu_l{i}")
        cur = _mm(a_, w_down, m=n, n=d, k=f, tk=f, tn=d, res=cur, name=f"ffn_down_l{i}")
        sv.update(h2=h2, uv=uv, ug=ug, a=a_)
        saved.append(sv)

    dy, loss_part = _loss(cur, tgt, name="loss")
    loss = lax.psum(loss_part[0, 0], ("x", "y", "c"))

    g_norm_mix, g_norm_ffn = [None] * depth, [None] * depth
    g_ffn_dw_b = [None] * depth
    g_up, g_down = [None] * depth, [None] * depth
    g_small = {}
    g_conv_in, g_conv_out = [None] * n_conv, [None] * n_conv
    g_pool_w = g_fox_in = g_fox_o = None
    g_pool_scale = g_bf = g_qg = g_kg = None
    part_small = {"b_in": [None] * n_conv, "dw": [None] * n_conv, "dw_b": [None] * n_conv, "ln_g": [None] * n_conv,
                  "ln_b": [None] * n_conv, "b_out": [None] * n_conv, "pool_b": None, "ffn_dw": [None] * depth}

    pending = []
    done = {}

    def take_pending():
        groups = list(pending)
        pending.clear()
        parts = [p_ for g_ in groups for p_ in g_[0]]
        return groups, (_chip_slots(parts) if parts else None)

    def finish_groups(groups, r2):
        pos = 0
        for parts, tag, sink in groups:
            sink(_rs_finish(parts, r2[pos:pos + len(parts)], chip_idx, tag))
            pos += len(parts)

    def carried(fn, args, name):
        groups, comm = take_pending()
        if comm is None:
            return fn(*args, name=name)
        outs, r2 = fn(*args, name=name, comm=comm)
        finish_groups(groups, r2)
        return outs

    dcur = dy
    for i in reversed(range(depth)):
        kind, j = i % 3, i // 3
        wts, sv = gathered[i], saved[i]
        w_up, w_down = wts[0], wts[1].reshape(f, d)
        ucol = w_up.shape[2]
        fdw, fdb = ffn_dw_f[i], ffn_dw_b[i]
        da = _mm(dcur, w_down, m=n, n=f, k=d, tk=d, tn=f, tb=True, name=f"ffn_down_dx_l{i}")
        dw_down = _mm(sv["a"], dcur, m=f, n=d, k=n, tm=f // 2, tn=d, tk=2048, ta=True, name=f"ffn_down_dw_l{i}")
        duv, dug, dwv, dwg, dbv, dbg = carried(
            _ffn_glu_bwd, (sv["uv"], sv["ug"], da, fdw[:, :f], fdw[:, f:], vec(fdb[:f]), vec(fdb[f:]), seq),
            name=f"ffn_glu_bwd_l{i}")
        dw_up = _mm(sv["h2"], duv, m=d, n=f, k=n, tm=d, tn=ucol, tk=2048, ta=True, o_stk=ucol, o_s0=0,
                    o_slots=N_CHIPS, name=f"ffn_up_dw_v_l{i}")
        dw_up = _mm(sv["h2"], dug, m=d, n=f, k=n, tm=d, tn=ucol, tk=2048, ta=True, o_stk=ucol, o_s0=2,
                    o_slots=N_CHIPS, o_buf=dw_up, name=f"ffn_up_dw_g_l{i}")
        dh2 = _mm(duv, w_up, m=n, n=d, k=f, tn=d, tk=ucol, tb=True, b_stk=ucol, b_s0=0, name=f"ffn_up_dx_v_l{i}")
        dh2 = _mm(dug, w_up, m=n, n=d, k=f, tn=d, tk=ucol, tb=True, b_stk=ucol, b_s0=2, res=dh2,
                  name=f"ffn_up_dx_g_l{i}")
        dmid, g_norm_ffn[i] = _rms_bwd(sv["x_mid"], dh2, vec(norm_ffn[i]), dcur, name=f"rms_ffn_bwd_l{i}")
        part_small["ffn_dw"][i] = jnp.concatenate([dwv, dwg], axis=1)
        g_ffn_dw_b[i] = jnp.concatenate([dbv, dbg], axis=1)

        def ffn_sink(red, i=i):
            g_up[i], g_down[i] = red[0], red[1]

        pending.append((_rs_begin([dw_up, dw_down.reshape(N_CHIPS, f // N_CHIPS, d)], [BF16, BF16], c_idx,
                                  tag=f"ffn_l{i}"), f"ffn_l{i}", ffn_sink))

        if kind == 0:
            w_in, w_out = wts[2], wts[3].reshape(d, d)
            wcol = w_in.shape[2]
            ds = _mm(dmid, w_out, m=n, n=d, k=d, tk=d, tb=True, name=f"conv_out_dx_l{i}")
            dw_out = _mm(sv["s"], dmid, m=d, n=d, k=n, tm=d, tn=d, tk=2048, ta=True, name=f"conv_out_dw_l{i}")
            dpa, dpg, ddw, ddwb, dlng, dlnb, dba, dbg_ = carried(
                _conf_bwd, (sv["u"], ds, sv["pa"], sv["pg"], sv["taps"], vec(ln_g_f[j]), vec(ln_b_f[j]), seq),
                name=f"conf_bwd_l{i}")
            dw_in = _mm(sv["h"], dpa, m=d, n=d, k=n, tm=d, tn=wcol, tk=2048, ta=True, o_stk=wcol, o_s0=0,
                        o_slots=N_CHIPS, name=f"conv_in_dw_a_l{i}")
            dw_in = _mm(sv["h"], dpg, m=d, n=d, k=n, tm=d, tn=wcol, tk=2048, ta=True, o_stk=wcol, o_s0=2,
                        o_slots=N_CHIPS, o_buf=dw_in, name=f"conv_in_dw_g_l{i}")
            dh = _mm(dpa, w_in, m=n, n=d, k=d, tn=d, tk=wcol, tb=True, b_stk=wcol, b_s0=0, name=f"conv_in_dx_a_l{i}")
            dh = _mm(dpg, w_in, m=n, n=d, k=d, tn=d, tk=wcol, tb=True, b_stk=wcol, b_s0=2, res=dh,
                     name=f"conv_in_dx_g_l{i}")
            dcur, g_norm_mix[i], db_out = _rms_bwd(sv["x_in"], dh, vec(norm_mix[i]), dmid, name=f"rms_mix_bwd_l{i}",
                                                   colsum=True)
            part_small["b_in"][j] = jnp.concatenate([dba, dbg_], axis=1)
            part_small["dw"][j] = ddw[:CONV_WIDTH]
            part_small["dw_b"][j], part_small["ln_g"][j], part_small["ln_b"][j] = ddwb, dlng, dlnb
            part_small["b_out"][j] = db_out
            mix_grads = [dw_in, dw_out.reshape(N_CHIPS, d // N_CHIPS, d)]
        elif kind == 1:
            dcur, g_norm_mix[i], dpw, dpb, dpsc = _pool_bwd(dmid, sv["x_in"], sv["p"], vec(norm_mix[i]), sv["pw"],
                                                            vec(pool_b_f[j]), vec(pool_scale[j]), seq,
                                                            name=f"pool_bwd_l{i}")
            part_small["pool_b"] = dpb
            g_pool_scale = dpsc
            mix_grads = [dpw.reshape(ng, N_CHIPS, gd // N_CHIPS, gd).transpose(1, 0, 2, 3).reshape(N_CHIPS, gd, gd)]
        else:
            do = _mm(dmid, sv["w_o"], m=n, n=d, k=d, tk=d, tb=True, name=f"fox_out_dx_l{i}")
            dw_o = _mm(sv["o"], dmid, m=d, n=d, k=n, tm=d, tn=d, tk=2048, ta=True, name=f"fox_out_dw_l{i}")
            dq, dcc, dk, dv, dck = _flash_bwd(sv["qa"], sv["ka"], sv["v"], do, sv["o"],
                                              _row_layout(sv["lse"], _tile(seq, ATTN_TILE)), bsz, seq,
                                              name=f"fox_attn_bwd_l{i}")
            dqkv, dfl, dqg, dkg, dbf = carried(
                _fox_prep_bwd, (sv["qkv"], dq, dk, dv, _from_col_layout(dcc), _from_col_layout(dck), sv["fl"], sv["bf"],
                                sv["qg"], sv["kg"], seq), name=f"fox_prep_bwd_l{i}")
            dw_qkv = _mm(sv["h"], dqkv, m=d, n=3 * d, k=n, tm=d, tn=d, tk=2048, ta=True, name=f"fox_qkv_dw_l{i}")
            dw_f = _mm(sv["h"], dfl, m=d, n=LANES, k=n, tm=d, tk=2048, ta=True, name=f"fox_fl_dw_l{i}")
            dh = _mm(dqkv, sv["w_qkv"], m=n, n=d, k=3 * d, tn=d, tk=d, tb=True, name=f"fox_qkv_dx_l{i}")
            dh = _mm(dfl, sv["w_f"], m=n, n=d, k=LANES, tn=d, tb=True, res=dh, name=f"fox_fl_dx_l{i}")
            dcur, g_norm_mix[i] = _rms_bwd(sv["x_in"], dh, vec(norm_mix[i]), dmid, name=f"rms_mix_bwd_l{i}")
            g_bf = dbf[:, :nh]
            g_qg = dqg.reshape(nh, HEAD_DIM).sum(axis=0, keepdims=True)
            g_kg = dkg.reshape(nh, HEAD_DIM).sum(axis=0, keepdims=True)
            dw_in_full = jnp.concatenate([dw_qkv, dw_f[:, :nh]], axis=1)
            wshard = dw_in_full.shape[1] // N_CHIPS
            mix_grads = [dw_in_full.reshape(d, N_CHIPS, wshard).transpose(1, 0, 2),
                         dw_o.reshape(N_CHIPS, d // N_CHIPS, d)]
        wire = [BF16] * len(mix_grads)
        if i == 0:
            sm_parts = [jnp.concatenate(part_small["b_in"]), jnp.stack(part_small["dw"]),
                        jnp.concatenate(part_small["dw_b"]), jnp.concatenate(part_small["ln_g"]),
                        jnp.concatenate(part_small["ln_b"]), jnp.concatenate(part_small["b_out"]),
                        part_small["pool_b"].reshape(n_pool, ng, gd), jnp.stack(part_small["ffn_dw"])]
            mix_grads.append(_pack([_to_shards(p_) for p_ in sm_parts], small_rows))
            wire.append(F32)

        def mix_sink(red, i=i, kind=kind, j=j):
            if kind == 0:
                g_conv_in[j], g_conv_out[j] = red[0], red[1]
            elif kind == 1:
                done["pool_w"] = red[0]
            else:
                done["fox_in"], done["fox_o"] = red[0], red[1]
            if i == 0:
                done["small"] = red[-1].reshape(-1)

        pending.append((_rs_begin(mix_grads, wire, c_idx, tag=f"mix_l{i}"), f"mix_l{i}", mix_sink))

    groups, comm = take_pending()
    finish_groups(groups, _run_comm(comm, name="rs_chips_tail"))
    g_pool_w, g_fox_in, g_fox_o, g_small_flat = done["pool_w"], done["fox_in"], done["fox_o"], done["small"]

    grad_x = dcur.reshape(bsz, seq, d)

    rep_parts = [jnp.concatenate(g_norm_mix), jnp.concatenate(g_norm_ffn), g_pool_scale, g_bf, g_qg, g_kg,
                 jnp.concatenate(g_ffn_dw_b)]
    rep_shapes = [norm_mix.shape, norm_ffn.shape, pool_scale.shape, fox_b_f.shape, fox_q_gain.shape,
                  fox_k_gain.shape, ffn_dw_b.shape]
    rep_rows = _pack_rows([math.prod(s) for s in rep_shapes], 8)
    rep = _all_reduce_small(_pack([p_.reshape(1, -1) for p_ in rep_parts], rep_rows)[0], name="all_reduce_small")
    g_rep = _unpack(rep.reshape(-1), rep_shapes)
    g_sm = _unpack(g_small_flat, small_shapes)

    grads = {
        "norm_mix": g_rep[0], "norm_ffn": g_rep[1],
        "conv_w_in": jnp.stack(g_conv_in), "conv_b_in": g_sm[0], "conv_dw": g_sm[1], "conv_dw_b": g_sm[2],
        "conv_ln_g": g_sm[3], "conv_ln_b": g_sm[4], "conv_w_out": jnp.stack(g_conv_out), "conv_b_out": g_sm[5],
        "pool_w": g_pool_w.reshape(pool_w.shape), "pool_b": g_sm[6], "pool_scale": g_rep[2],
        "fox_w_in": g_fox_in.reshape(fox_w_in.shape), "fox_b_f": g_rep[3], "fox_q_gain": g_rep[4],
        "fox_k_gain": g_rep[5], "fox_w_o": g_fox_o.reshape(fox_w_o.shape),
        "ffn_w_up": jnp.stack(g_up), "ffn_dw": g_sm[7], "ffn_dw_b": g_rep[6], "ffn_w_down": jnp.stack(g_down),
    }
    weights = dict(norm_mix=norm_mix, norm_ffn=norm_ffn, conv_w_in=conv_w_in, conv_b_in=conv_b_in, conv_dw=conv_dw,
                   conv_dw_b=conv_dw_b, conv_ln_g=conv_ln_g, conv_ln_b=conv_ln_b, conv_w_out=conv_w_out,
                   conv_b_out=conv_b_out, pool_w=pool_w, pool_b=pool_b, pool_scale=pool_scale, fox_w_in=fox_w_in,
                   fox_b_f=fox_b_f, fox_q_gain=fox_q_gain, fox_k_gain=fox_k_gain, fox_w_o=fox_w_o, ffn_w_up=ffn_w_up,
                   ffn_dw=ffn_dw, ffn_dw_b=ffn_dw_b, ffn_w_down=ffn_w_down)
    m_in = dict(norm_mix=m_norm_mix, norm_ffn=m_norm_ffn, conv_w_in=m_conv_w_in, conv_b_in=m_conv_b_in,
                conv_dw=m_conv_dw, conv_dw_b=m_conv_dw_b, conv_ln_g=m_conv_ln_g, conv_ln_b=m_conv_ln_b,
                conv_w_out=m_conv_w_out, conv_b_out=m_conv_b_out, pool_w=m_pool_w, pool_b=m_pool_b,
                pool_scale=m_pool_scale, fox_w_in=m_fox_w_in, fox_b_f=m_fox_b_f, fox_q_gain=m_fox_q_gain,
                fox_k_gain=m_fox_k_gain, fox_w_o=m_fox_w_o, ffn_w_up=m_ffn_w_up, ffn_dw=m_ffn_dw,
                ffn_dw_b=m_ffn_dw_b, ffn_w_down=m_ffn_w_down)
    v_in = dict(norm_mix=v_norm_mix, norm_ffn=v_norm_ffn, conv_w_in=v_conv_w_in, conv_b_in=v_conv_b_in,
                conv_dw=v_conv_dw, conv_dw_b=v_conv_dw_b, conv_ln_g=v_conv_ln_g, conv_ln_b=v_conv_ln_b,
                conv_w_out=v_conv_w_out, conv_b_out=v_conv_b_out, pool_w=v_pool_w, pool_b=v_pool_b,
                pool_scale=v_pool_scale, fox_w_in=v_fox_w_in, fox_b_f=v_fox_b_f, fox_q_gain=v_fox_q_gain,
                fox_k_gain=v_fox_k_gain, fox_w_o=v_fox_w_o, ffn_w_up=v_ffn_w_up, ffn_dw=v_ffn_dw,
                ffn_dw_b=v_ffn_dw_b, ffn_w_down=v_ffn_w_down)
    names = list(weights)
    g_out, d_out, m_out, v_out = [], [], [], []
    for nm in names:
        g_, dl_, m_, v_ = _adamw_nd(weights[nm], grads[nm].reshape(weights[nm].shape), m_in[nm], v_in[nm],
                                    name=f"adamw_{nm}")
        g_out.append(g_)
        d_out.append(dl_)
        m_out.append(m_)
        v_out.append(v_)
    return (loss, grad_x, *g_out, *d_out, *m_out, *v_out)
```

```python
import math

import jax
import jax.numpy as jnp
import numpy as np
from jax import lax
from jax.experimental import pallas as pl
from jax.experimental.pallas import tpu as pltpu

F32 = jnp.float32
BF16 = jnp.bfloat16
HI = lax.Precision.HIGHEST
MESH = pl.DeviceIdType.MESH
ANY = pl.BlockSpec(memory_space=pl.ANY)

EPS = 1e-6
HEAD_DIM = 64
LANES = 128
POOL_WINDOWS = (2, 4, 8, 16)
CONV_WIDTH = 31
CONV_HALO = 32
FFN_HALO = 8
FFN_ROWS, FFN_COLS = 256, 1408
POOL_HALO = 16
N_CHIPS = 4
NEG = -1e30

ADAM_LR = 0.001
ADAM_B1 = 0.9
ADAM_B2 = 0.999
ADAM_EPS = 1e-08
ADAM_WD = 0.01
ADAM_STEP = 10

V7X_VMEM_LIMIT_BYTES = 56 * 1024 * 1024


def _cp(*sem):
    return pltpu.CompilerParams(dimension_semantics=sem or None, vmem_limit_bytes=V7X_VMEM_LIMIT_BYTES)


def _tile(n, pref):
    t = min(n, pref)
    assert n % t == 0, (n, pref)
    return t


def _sig(v):
    return jax.nn.sigmoid(v)


def _roll(v, shift):
    n = v.shape[0]
    shift = shift % n
    return v if shift == 0 else pltpu.roll(v, shift, 0)


SUBLANES = 8


CONV_ROWS = 64


def _conv_taps(rot_ref, w_ref, out_ref, tm, start_of):
    d = out_ref.shape[1]
    for lc in range(d // LANES):
        ls = slice(lc * LANES, (lc + 1) * LANES)

        def rows(ri, carry, ls=ls):
            r0 = pl.multiple_of(ri * CONV_ROWS, CONV_ROWS)
            acc = jnp.zeros((CONV_ROWS, LANES), F32)
            for sh in range(CONV_WIDTH):
                kk = CONV_WIDTH - 1 - sh
                acc = acc + w_ref[kk:kk + 1, ls] * rot_ref[sh % SUBLANES, pl.ds(r0 + start_of(sh), CONV_ROWS), ls]
            out_ref[pl.ds(r0, CONV_ROWS), ls] = acc
            return carry

        lax.fori_loop(0, tm // CONV_ROWS, rows, 0)


def _tap_grads(rotz_ref, rotd_ref, dw_ref, tm, halo):
    d = dw_ref.shape[1]
    for lc in range(d // LANES):
        ls = slice(lc * LANES, (lc + 1) * LANES)

        def rows(ri, acc, ls=ls):
            r0 = pl.multiple_of(ri * CONV_ROWS, CONV_ROWS)
            duc = rotd_ref[0, pl.ds(r0, CONV_ROWS), ls]
            parts = []
            for sh in range(CONV_WIDTH):
                start = halo - (sh // SUBLANES) * SUBLANES
                prod = duc * rotz_ref[sh % SUBLANES, pl.ds(r0 + start, CONV_ROWS), ls]
                parts.append(prod.reshape(CONV_ROWS // SUBLANES, SUBLANES, LANES).sum(axis=0))
            return acc + jnp.concatenate(parts, axis=0)

        acc = lax.fori_loop(0, tm // CONV_ROWS, rows, jnp.zeros((CONV_WIDTH * SUBLANES, LANES), F32))
        for sh in range(CONV_WIDTH):
            kk = CONV_WIDTH - 1 - sh
            dw_ref[kk:kk + 1, ls] += jnp.sum(acc[sh * SUBLANES:(sh + 1) * SUBLANES], axis=0, keepdims=True)


class _Comm:
    def __init__(self, ins, out_shapes, n_sems, start, wait):
        self.ins, self.out_shapes, self.n_sems, self.start, self.wait = list(ins), list(out_shapes), n_sems, start, wait


def _hosted(body, comm, *, grid, in_specs, out_specs, out_shape, scratch_shapes, sem, name, ins):
    if comm is None:
        return pl.pallas_call(body, grid=grid, in_specs=in_specs, out_specs=out_specs, out_shape=out_shape,
                              scratch_shapes=scratch_shapes, name=name, compiler_params=_cp(*sem))(*ins)
    n_in, n_out, n_scr = len(in_specs), len(out_specs), len(scratch_shapes)
    nci, nco = len(comm.ins), len(comm.out_shapes)

    def wrapped(*refs):
        pos = [0]

        def take(cnt):
            pos[0] += cnt
            return refs[pos[0] - cnt:pos[0]]

        r_in, c_in, r_out, c_out, r_scr = take(n_in), take(nci), take(n_out), take(nco), take(n_scr)
        send_sems, recv_sems = take(2)
        ids = [pl.program_id(ax) for ax in range(len(grid))]
        first, last = ids[0] == 0, ids[0] == grid[0] - 1
        for ax in range(1, len(grid)):
            first = jnp.logical_and(first, ids[ax] == 0)
            last = jnp.logical_and(last, ids[ax] == grid[ax] - 1)

        @pl.when(first)
        def _():
            comm.start(c_in, c_out, send_sems, recv_sems)

        body(*r_in, *r_out, *r_scr)

        @pl.when(last)
        def _():
            comm.wait(c_in, c_out, send_sems, recv_sems)

    outs = pl.pallas_call(
        wrapped, grid=grid, in_specs=list(in_specs) + [ANY] * nci, out_specs=list(out_specs) + [ANY] * nco,
        out_shape=list(out_shape) + comm.out_shapes,
        scratch_shapes=list(scratch_shapes) + [pltpu.SemaphoreType.DMA((comm.n_sems,))] * 2, name=name,
        compiler_params=pltpu.CompilerParams(dimension_semantics=sem, vmem_limit_bytes=V7X_VMEM_LIMIT_BYTES,
                                             has_side_effects=True),
    )(*ins, *comm.ins)
    return list(outs[:n_out]), list(outs[n_out:])


def _run_comm(comm, name):
    def body(*refs):
        nci, nco = len(comm.ins), len(comm.out_shapes)
        c_in, c_out, send_sems, recv_sems = refs[:nci], refs[nci:nci + nco], refs[-2], refs[-1]
        comm.start(c_in, c_out, send_sems, recv_sems)
        comm.wait(c_in, c_out, send_sems, recv_sems)

    return pl.pallas_call(
        body, in_specs=[ANY] * len(comm.ins), out_specs=[ANY] * len(comm.out_shapes), out_shape=comm.out_shapes,
        scratch_shapes=[pltpu.SemaphoreType.DMA((comm.n_sems,))] * 2, name=name,
        compiler_params=pltpu.CompilerParams(has_side_effects=True),
    )(*comm.ins)


def _mm(a, b, *, m, n, k, name, tm=1024, tn=1024, tk=512, ta=False, tb=False, b_stk=None, b_s0=0,
        o_stk=None, o_s0=0, o_slots=None, o_buf=None, bias=None, res=None, out_dtype=F32):
    tm, tn, tk = _tile(m, tm), _tile(n, tn), _tile(k, tk)
    gi, gj, gk = m // tm, n // tn, k // tk
    a_spec = pl.BlockSpec((tk, tm), lambda j, i, kk: (kk, i)) if ta else pl.BlockSpec((tm, tk), lambda j, i, kk: (i, kk))
    if b_stk is None:
        b_spec = pl.BlockSpec((tn, tk), lambda j, i, kk: (j, kk)) if tb else pl.BlockSpec((tk, tn), lambda j, i, kk: (kk, j))
    elif tb:
        assert b_stk % tk == 0
        per = b_stk // tk
        b_spec = pl.BlockSpec((None, tn, tk), lambda j, i, kk: (b_s0 + kk // per, j, kk % per))
    else:
        assert b_stk % tn == 0
        per = b_stk // tn
        b_spec = pl.BlockSpec((None, tk, tn), lambda j, i, kk: (b_s0 + j // per, kk, j % per))
    ins, in_specs = [a, b], [a_spec, b_spec]
    if bias is not None:
        ins.append(bias)
        in_specs.append(pl.BlockSpec((1, tn), lambda j, i, kk: (0, j)))
    if res is not None:
        ins.append(res)
        in_specs.append(pl.BlockSpec((tm, tn), lambda j, i, kk: (i, j)))
    aliases = {}
    if o_stk is None:
        out_shape = jax.ShapeDtypeStruct((m, n), out_dtype)
        o_spec = pl.BlockSpec((tm, tn), lambda j, i, kk: (i, j))
    else:
        assert o_stk % tn == 0
        pero = o_stk // tn
        out_shape = jax.ShapeDtypeStruct((o_slots, m, o_stk), out_dtype)
        o_spec = pl.BlockSpec((None, tm, tn), lambda j, i, kk: (o_s0 + j // pero, i, j % pero))
        if o_buf is not None:
            aliases = {len(ins): 0}
            ins.append(o_buf)
            in_specs.append(ANY)
    has_bias, has_res, has_buf = bias is not None, res is not None, o_buf is not None
    dn = (((0 if ta else 1,), (1 if tb else 0,)), ((), ()))

    def body(*refs):
        a_ref, b_ref = refs[0], refs[1]
        pos = 2
        bias_ref = refs[pos] if has_bias else None
        pos += has_bias
        res_ref = refs[pos] if has_res else None
        pos += has_res + has_buf
        o_ref = refs[pos]
        p = lax.dot_general(a_ref[...].astype(BF16), b_ref[...].astype(BF16), dn, preferred_element_type=F32)

        def finish(acc):
            if has_bias:
                acc = acc + bias_ref[...]
            if has_res:
                acc = acc + res_ref[...]
            o_ref[...] = acc.astype(o_ref.dtype)

        if gk == 1:
            finish(p)
        else:
            acc_ref = refs[pos + 1]
            kk = pl.program_id(2)

            @pl.when(kk == 0)
            def _():
                acc_ref[...] = p

            @pl.when(kk > 0)
            def _():
                acc_ref[...] += p

            @pl.when(kk == gk - 1)
            def _():
                finish(acc_ref[...])

    return pl.pallas_call(
        body, grid=(gj, gi, gk), in_specs=in_specs, out_specs=o_spec, out_shape=out_shape,
        scratch_shapes=[pltpu.VMEM((tm, tn), F32)] if gk > 1 else [],
        input_output_aliases=aliases, name=name,
        compiler_params=_cp("parallel", "parallel", "arbitrary"),
    )(*ins)


def _rms_fwd(x, g, name):
    n, d = x.shape
    tm = _tile(n, 512)

    def body(x_ref, g_ref, h_ref):
        xv = x_ref[...]
        r = lax.rsqrt(jnp.mean(xv * xv, axis=-1, keepdims=True) + EPS)
        h_ref[...] = (xv * r * g_ref[...]).astype(h_ref.dtype)

    return pl.pallas_call(
        body, grid=(n // tm,),
        in_specs=[pl.BlockSpec((tm, d), lambda i: (i, 0)), pl.BlockSpec((1, d), lambda i: (0, 0))],
        out_specs=pl.BlockSpec((tm, d), lambda i: (i, 0)),
        out_shape=jax.ShapeDtypeStruct((n, d), BF16), name=name, compiler_params=_cp("arbitrary"),
    )(x, g)


def _rms_bwd(x, dh, g, dres, name, colsum=False):
    n, d = x.shape
    tm = _tile(n, 512)

    def body(x_ref, dh_ref, g_ref, dres_ref, dx_ref, dg_ref, *rest):
        i = pl.program_id(0)
        xv, dhv = x_ref[...], dh_ref[...]
        r = lax.rsqrt(jnp.mean(xv * xv, axis=-1, keepdims=True) + EPS)
        xn = xv * r
        dxn = dhv * g_ref[...]
        dx_ref[...] = dres_ref[...] + r * (dxn - xn * jnp.mean(dxn * xn, axis=-1, keepdims=True))
        dg = jnp.sum(dhv * xn, axis=0, keepdims=True)

        @pl.when(i == 0)
        def _():
            dg_ref[...] = jnp.zeros_like(dg_ref)
            if colsum:
                rest[0][...] = jnp.zeros_like(rest[0])

        dg_ref[...] += dg
        if colsum:
            rest[0][...] += jnp.sum(dres_ref[...], axis=0, keepdims=True)

    row = pl.BlockSpec((tm, d), lambda i: (i, 0))
    vec = pl.BlockSpec((1, d), lambda i: (0, 0))
    out_shape = [jax.ShapeDtypeStruct((n, d), F32), jax.ShapeDtypeStruct((1, d), F32)]
    out_specs = [row, vec]
    if colsum:
        out_shape.append(jax.ShapeDtypeStruct((1, d), F32))
        out_specs.append(vec)
    return pl.pallas_call(
        body, grid=(n // tm,), in_specs=[row, row, vec, row], out_specs=out_specs, out_shape=out_shape,
        name=name, compiler_params=_cp("arbitrary"),
    )(x, dh, g, dres)


def _loss(y, tgt, name):
    n, d = y.shape
    tm = _tile(n, 512)

    def body(y_ref, t_ref, dy_ref, l_ref):
        i = pl.program_id(0)
        e = y_ref[...] - t_ref[...]
        dy_ref[...] = e / d
        part = 0.5 * jnp.sum(jnp.mean(e * e, axis=-1, keepdims=True), axis=0, keepdims=True)

        @pl.when(i == 0)
        def _():
            l_ref[...] = jnp.zeros_like(l_ref)

        l_ref[...] += part

    row = pl.BlockSpec((tm, d), lambda i: (i, 0))
    return pl.pallas_call(
        body, grid=(n // tm,), in_specs=[row, row],
        out_specs=[row, pl.BlockSpec((1, 1), lambda i: (0, 0))],
        out_shape=[jax.ShapeDtypeStruct((n, d), F32), jax.ShapeDtypeStruct((1, 1), F32)],
        name=name, compiler_params=_cp("arbitrary"),
    )(y, tgt)


def _ffn_specs(n, f, tm, tc, seq):
    hb = FFN_HALO
    cur = pl.BlockSpec((tm, tc), lambda j, i: (i, j))
    prev = pl.BlockSpec((hb, tc), lambda j, i: (jnp.maximum(i * (tm // hb) - 1, 0), j))
    nxt = pl.BlockSpec((hb, tc), lambda j, i: (jnp.minimum((i + 1) * (tm // hb), n // hb - 1), j))
    taps = pl.BlockSpec((3, tc), lambda j, i: (0, j))
    vec = pl.BlockSpec((1, tc), lambda j, i: (0, j))
    return cur, prev, nxt, taps, vec


def _ffn_glu_fwd(uv, ug, wv, wg, bv, bg, seq, name, comm=None):
    n, f = uv.shape
    tm, tc = _tile(seq, FFN_ROWS), _tile(f, FFN_COLS)
    tps = seq // tm
    cur, prev, _, taps, vec = _ffn_specs(n, f, tm, tc, seq)

    def body(uvp, uvc, ugp, ugc, wv_ref, wg_ref, bv_ref, bg_ref, a_ref):
        first = (pl.program_id(1) % tps) == 0

        def conv(p_ref, c_ref, w_ref, b_ref):
            xs = jnp.concatenate([jnp.where(first, 0.0, p_ref[...]), c_ref[...]], axis=0)
            w = w_ref[...]
            y = w[2:3] * xs + w[1:2] * _roll(xs, 1) + w[0:1] * _roll(xs, 2)
            return y[FFN_HALO:] + b_ref[...]

        val = conv(uvp, uvc, wv_ref, bv_ref)
        gate = conv(ugp, ugc, wg_ref, bg_ref)
        a_ref[...] = (gate * _sig(gate) * val).astype(a_ref.dtype)

    return _hosted(
        body, comm, grid=(f // tc, n // tm), in_specs=[prev, cur, prev, cur, taps, taps, vec, vec],
        out_specs=[cur], out_shape=[jax.ShapeDtypeStruct((n, f), BF16)], scratch_shapes=[],
        sem=("parallel", "arbitrary"), name=name, ins=(uv, uv, ug, ug, wv, wg, bv, bg))


def _ffn_glu_bwd(uv, ug, da, wv, wg, bv, bg, seq, name, comm=None):
    n, f = uv.shape
    tm, tc = _tile(seq, FFN_ROWS), _tile(f, FFN_COLS)
    tps = seq // tm
    hb = FFN_HALO
    ext = tm + hb
    cur, prev, nxt, taps, vec = _ffn_specs(n, f, tm, tc, seq)

    def body(uvp, uvc, uvn, ugp, ugc, ugn, da_c, da_n, wv_ref, wg_ref, bv_ref, bg_ref,
             duv_ref, dug_ref, dwv_ref, dwg_ref, dbv_ref, dbg_ref):
        i = pl.program_id(1)
        first = (i % tps) == 0
        last = (i % tps) == tps - 1
        da_e = jnp.concatenate([da_c[...], jnp.where(last, 0.0, da_n[...])], axis=0)

        def taps3(p_ref, c_ref, n_ref):
            xs = jnp.concatenate([jnp.where(first, 0.0, p_ref[...]), c_ref[...], n_ref[...]], axis=0)
            return xs, _roll(xs, 1), _roll(xs, 2)

        xv, xg = taps3(uvp, uvc, uvn), taps3(ugp, ugc, ugn)
        wv_, wg_ = wv_ref[...], wg_ref[...]

        def conv(xs, w, b_ref):
            return (w[2:3] * xs[0] + w[1:2] * xs[1] + w[0:1] * xs[2])[hb:] + b_ref[...]

        val, gate = conv(xv, wv_, bv_ref), conv(xg, wg_, bg_ref)
        sg = _sig(gate)
        dval = da_e * (gate * sg)
        dgate = da_e * val * (sg * (1.0 + gate * (1.0 - sg)))

        def conv_t(dv, w):
            return (w[2:3] * dv + w[1:2] * _roll(dv, ext - 1) + w[0:1] * _roll(dv, ext - 2))[:tm]

        duv_ref[...] = conv_t(dval, wv_).astype(duv_ref.dtype)
        dug_ref[...] = conv_t(dgate, wg_).astype(dug_ref.dtype)

        def tap_grads(d_own, xs):
            return jnp.concatenate(
                [jnp.sum(d_own * xs[2 - kk][hb:hb + tm], axis=0, keepdims=True) for kk in range(3)], axis=0)

        dv_own, dg_own = dval[:tm], dgate[:tm]

        @pl.when(i == 0)
        def _():
            for r in (dwv_ref, dwg_ref, dbv_ref, dbg_ref):
                r[...] = jnp.zeros_like(r)

        dwv_ref[...] += tap_grads(dv_own, xv)
        dwg_ref[...] += tap_grads(dg_own, xg)
        dbv_ref[...] += jnp.sum(dv_own, axis=0, keepdims=True)
        dbg_ref[...] += jnp.sum(dg_own, axis=0, keepdims=True)

    return _hosted(
        body, comm, grid=(f // tc, n // tm),
        in_specs=[prev, cur, nxt, prev, cur, nxt, cur, nxt, taps, taps, vec, vec],
        out_specs=[cur, cur, taps, taps, vec, vec],
        out_shape=[jax.ShapeDtypeStruct((n, f), BF16), jax.ShapeDtypeStruct((n, f), BF16),
                   jax.ShapeDtypeStruct((3, f), F32), jax.ShapeDtypeStruct((3, f), F32),
                   jax.ShapeDtypeStruct((1, f), F32), jax.ShapeDtypeStruct((1, f), F32)],
        scratch_shapes=[], sem=("parallel", "arbitrary"), name=name,
        ins=(uv, uv, uv, ug, ug, ug, da, da, wv, wg, bv, bg))


def _conf_specs(n, d, tm):
    hb = CONV_HALO
    cur = pl.BlockSpec((tm, d), lambda i: (i, 0))
    prev = pl.BlockSpec((hb, d), lambda i: (jnp.maximum(i * (tm // hb) - 1, 0), 0))
    nxt = pl.BlockSpec((hb, d), lambda i: (jnp.minimum((i + 1) * (tm // hb), n // hb - 1), 0))
    taps = pl.BlockSpec((CONV_HALO, d), lambda i: (0, 0))
    vec = pl.BlockSpec((1, d), lambda i: (0, 0))
    return cur, prev, nxt, taps, vec


def _conf_fwd(pa, pg, w, wb, lng, lnb, seq, name):
    n, d = pa.shape
    tm = _tile(seq, 256)
    tps = seq // tm
    hb = CONV_HALO
    cur, prev, _, taps, vec = _conf_specs(n, d, tm)

    def body(pap, pac, pgp, pgc, w_ref, wb_ref, lng_ref, lnb_ref, u_ref, s_ref, rot_ref):
        first = (pl.program_id(0) % tps) == 0
        a = jnp.concatenate([jnp.where(first, 0.0, pap[...]), pac[...]], axis=0)
        g = jnp.concatenate([jnp.where(first, 0.0, pgp[...]), pgc[...]], axis=0)
        z = a * _sig(g)
        for b in range(SUBLANES):
            rot_ref[b] = _roll(z, b)
        _conv_taps(rot_ref, w_ref, u_ref, tm, lambda sh: hb - (sh // SUBLANES) * SUBLANES)
        u = u_ref[...] + wb_ref[...]
        mu = jnp.mean(u, axis=-1, keepdims=True)
        uc = u - mu
        var = jnp.mean(uc * uc, axis=-1, keepdims=True)
        ul = uc * lax.rsqrt(var + EPS) * lng_ref[...] + lnb_ref[...]
        u_ref[...] = u
        s_ref[...] = (ul * _sig(ul)).astype(s_ref.dtype)

    return pl.pallas_call(
        body, grid=(n // tm,), in_specs=[prev, cur, prev, cur, taps, vec, vec, vec],
        out_specs=[cur, cur],
        out_shape=[jax.ShapeDtypeStruct((n, d), F32), jax.ShapeDtypeStruct((n, d), BF16)],
        scratch_shapes=[pltpu.VMEM((SUBLANES, tm + hb, d), F32)],
        name=name, compiler_params=_cp("arbitrary"),
    )(pa, pa, pg, pg, w, wb, lng, lnb)


def _conf_bwd(u, ds, pa, pg, w, lng, lnb, seq, name, comm=None):
    n, d = u.shape
    tm = _tile(seq, 256)
    tps = seq // tm
    hb = CONV_HALO
    ext = tm + hb
    cur, prev, nxt, taps, vec = _conf_specs(n, d, tm)

    def body(uc_ref, un_ref, dsc_ref, dsn_ref, pap, pac, pgp, pgc, w_ref, lng_ref, lnb_ref,
             dpa_ref, dpg_ref, dw_ref, dwb_ref, dlng_ref, dlnb_ref, dba_ref, dbg_ref, rotz_ref, rotd_ref, dz_ref):
        i = pl.program_id(0)
        first = (i % tps) == 0
        last = (i % tps) == tps - 1

        @pl.when(i == 0)
        def _():
            for r in (dw_ref, dwb_ref, dlng_ref, dlnb_ref, dba_ref, dbg_ref):
                r[...] = jnp.zeros_like(r)

        ue = jnp.concatenate([uc_ref[...], un_ref[...]], axis=0)
        dse = jnp.concatenate([dsc_ref[...], jnp.where(last, 0.0, dsn_ref[...])], axis=0)
        mu = jnp.mean(ue, axis=-1, keepdims=True)
        cen = ue - mu
        r = lax.rsqrt(jnp.mean(cen * cen, axis=-1, keepdims=True) + EPS)
        xn = cen * r
        ul = xn * lng_ref[...] + lnb_ref[...]
        sg = _sig(ul)
        dul = dse * (sg * (1.0 + ul * (1.0 - sg)))
        dun = dul * lng_ref[...]
        du = r * (dun - jnp.mean(dun, axis=-1, keepdims=True) - xn * jnp.mean(dun * xn, axis=-1, keepdims=True))
        dlng_ref[...] += jnp.sum((dul * xn)[:tm], axis=0, keepdims=True)
        dlnb_ref[...] += jnp.sum(dul[:tm], axis=0, keepdims=True)
        dwb_ref[...] += jnp.sum(du[:tm], axis=0, keepdims=True)
        for b in range(SUBLANES):
            rotd_ref[b] = _roll(du, ext - b)
        _conv_taps(rotd_ref, w_ref, dz_ref, tm, lambda sh: (sh // SUBLANES) * SUBLANES)
        dz = dz_ref[...]

        a = jnp.concatenate([jnp.where(first, 0.0, pap[...]), pac[...]], axis=0)
        g = jnp.concatenate([jnp.where(first, 0.0, pgp[...]), pgc[...]], axis=0)
        sgg = _sig(g)
        z = a * sgg
        for b in range(SUBLANES):
            rotz_ref[b] = _roll(z, b)
        _tap_grads(rotz_ref, rotd_ref, dw_ref, tm, hb)

        a_c, sg_c = a[hb:], sgg[hb:]
        da = dz * sg_c
        dg = dz * a_c * sg_c * (1.0 - sg_c)
        dpa_ref[...] = da.astype(dpa_ref.dtype)
        dpg_ref[...] = dg.astype(dpg_ref.dtype)
        dba_ref[...] += jnp.sum(da, axis=0, keepdims=True)
        dbg_ref[...] += jnp.sum(dg, axis=0, keepdims=True)

    vshape = jax.ShapeDtypeStruct((1, d), F32)
    return _hosted(
        body, comm, grid=(n // tm,),
        in_specs=[cur, nxt, cur, nxt, prev, cur, prev, cur, taps, vec, vec],
        out_specs=[cur, cur, taps, vec, vec, vec, vec, vec],
        out_shape=[jax.ShapeDtypeStruct((n, d), BF16), jax.ShapeDtypeStruct((n, d), BF16),
                   jax.ShapeDtypeStruct((CONV_HALO, d), F32), vshape, vshape, vshape, vshape, vshape],
        scratch_shapes=[pltpu.VMEM((SUBLANES, ext, d), F32), pltpu.VMEM((SUBLANES, ext, d), F32),
                        pltpu.VMEM((tm, d), F32)],
        sem=("arbitrary",), name=name, ins=(u, u, ds, ds, pa, pa, pg, pg, w, lng, lnb))


def _pool_specs(n, d, tm, gd):
    hb = POOL_HALO
    cur = pl.BlockSpec((tm, d), lambda i: (i, 0))
    prev = pl.BlockSpec((hb, d), lambda i: (jnp.maximum(i * (tm // hb) - 1, 0), 0))
    nxt = pl.BlockSpec((hb, d), lambda i: (jnp.minimum((i + 1) * (tm // hb), n // hb - 1), 0))
    wsp = pl.BlockSpec((len(POOL_WINDOWS), gd, gd), lambda i: (0, 0, 0))
    vec = pl.BlockSpec((1, d), lambda i: (0, 0))
    return cur, prev, nxt, wsp, vec


def _pool_fwd(x, g, w, b, sc, seq, name):
    n, d = x.shape
    gd = d // len(POOL_WINDOWS)
    tm = _tile(seq, 256)
    tps = seq // tm
    hb = POOL_HALO
    cur, prev, _, wsp, vec = _pool_specs(n, d, tm, gd)

    def body(xp, xc, g_ref, w_ref, b_ref, sc_ref, x1_ref, p_ref):
        i = pl.program_id(0)
        first = (i % tps) == 0
        xe = jnp.concatenate([jnp.where(first, 0.0, xp[...]), xc[...]], axis=0)
        r = lax.rsqrt(jnp.mean(xe * xe, axis=-1, keepdims=True) + EPS)
        h = xe * r * g_ref[...]
        t = ((i % tps) * tm + lax.broadcasted_iota(jnp.int32, (tm, 1), 0) + 1).astype(F32)
        ys = []
        for gi, win in enumerate(POOL_WINDOWS):
            hg = h[:, gi * gd:(gi + 1) * gd]
            s, sh = hg, 1
            while sh < win:
                s = s + _roll(s, sh)
                sh *= 2
            p = (s[hb:] / jnp.minimum(t, float(win)) - hg[hb:]).astype(BF16)
            p_ref[:, gi * gd:(gi + 1) * gd] = p
            ys.append(jnp.dot(p, w_ref[gi], preferred_element_type=F32))
        y = jnp.concatenate(ys, axis=1) + b_ref[...]
        x1_ref[...] = xc[...] + y * sc_ref[...]

    return pl.pallas_call(
        body, grid=(n // tm,), in_specs=[prev, cur, vec, wsp, vec, vec], out_specs=[cur, cur],
        out_shape=[jax.ShapeDtypeStruct((n, d), F32), jax.ShapeDtypeStruct((n, d), BF16)],
        name=name, compiler_params=_cp("arbitrary"),
    )(x, x, g, w, b, sc)


def _pool_bwd(dx1, x, p, g, w, b, sc, seq, name):
    n, d = x.shape
    ng = len(POOL_WINDOWS)
    gd = d // ng
    tm = _tile(seq, 256)
    tps = seq // tm
    hb = POOL_HALO
    ext = tm + hb
    cur, _, nxt, wsp, vec = _pool_specs(n, d, tm, gd)

    def body(dc_ref, dn_ref, x_ref, p_ref, g_ref, w_ref, b_ref, sc_ref, dx_ref, dg_ref, dw_ref, db_ref, dsc_ref):
        i = pl.program_id(0)
        last = (i % tps) == tps - 1

        @pl.when(i == 0)
        def _():
            for r_ in (dg_ref, dw_ref, db_ref, dsc_ref):
                r_[...] = jnp.zeros_like(r_)

        dxc = dc_ref[...]
        dxe = jnp.concatenate([dxc, jnp.where(last, 0.0, dn_ref[...])], axis=0)
        dyg = dxe * sc_ref[...]
        t = ((i % tps) * tm + lax.broadcasted_iota(jnp.int32, (ext, 1), 0) + 1).astype(F32)
        dhs = []
        for gi, win in enumerate(POOL_WINDOWS):
            sl = slice(gi * gd, (gi + 1) * gd)
            dyb = dyg[:, sl].astype(BF16)
            wg = w_ref[gi]
            dp = lax.dot_general(dyb, wg, (((1,), (1,)), ((), ())), preferred_element_type=F32)
            s, sh = dp / jnp.minimum(t, float(win)), 1
            while sh < win:
                s = s + _roll(s, ext - sh)
                sh *= 2
            dhs.append((s - dp)[:tm])
            pg = p_ref[:, sl]
            dw_ref[gi] += lax.dot_general(pg, dyb[:tm], (((0,), (0,)), ((), ())), preferred_element_type=F32)
            ypre = jnp.dot(pg, wg, preferred_element_type=F32) + b_ref[:, sl]
            dsc_ref[:, sl] += jnp.sum(dxc[:, sl] * ypre, axis=0, keepdims=True)
            db_ref[:, sl] += jnp.sum(dyg[:tm, sl], axis=0, keepdims=True)
        dh = jnp.concatenate(dhs, axis=1)
        xv = x_ref[...]
        r = lax.rsqrt(jnp.mean(xv * xv, axis=-1, keepdims=True) + EPS)
        xn = xv * r
        dxn = dh * g_ref[...]
        dx_ref[...] = dxc + r * (dxn - xn * jnp.mean(dxn * xn, axis=-1, keepdims=True))
        dg_ref[...] += jnp.sum(dh * xn, axis=0, keepdims=True)

    vshape = jax.ShapeDtypeStruct((1, d), F32)
    return pl.pallas_call(
        body, grid=(n // tm,), in_specs=[cur, nxt, cur, cur, vec, wsp, vec, vec],
        out_specs=[cur, vec, wsp, vec, vec],
        out_shape=[jax.ShapeDtypeStruct((n, d), F32), vshape, jax.ShapeDtypeStruct((ng, gd, gd), F32), vshape, vshape],
        name=name, compiler_params=_cp("arbitrary"),
    )(dx1, dx1, x, p, g, w, b, sc)


def _head_maps(d):
    hd = lax.broadcasted_iota(jnp.int32, (d, LANES), 0) // HEAD_DIM
    col = lax.broadcasted_iota(jnp.int32, (d, LANES), 1)
    gm = (hd == col).astype(BF16)
    hd_t = lax.broadcasted_iota(jnp.int32, (LANES, d), 1) // HEAD_DIM
    row = lax.broadcasted_iota(jnp.int32, (LANES, d), 0)
    gt = (hd_t == row).astype(BF16)
    return gm, gt


def _dot_split(v, onehot):
    hi = v.astype(BF16)
    lo = (v - hi.astype(F32)).astype(BF16)
    return jnp.dot(hi, onehot, preferred_element_type=F32) + jnp.dot(lo, onehot, preferred_element_type=F32)


def _bias_placement(nh):
    pq = np.zeros((3 * LANES, nh * HEAD_DIM), np.float32)
    pk = np.zeros((3 * LANES, nh * HEAD_DIM), np.float32)
    oq = np.zeros((1, nh * HEAD_DIM), np.float32)
    ok = np.zeros((1, nh * HEAD_DIM), np.float32)
    for h in range(nh):
        for piece in range(3):
            pq[piece * LANES + h, h * HEAD_DIM + piece] = 1.0
            pk[piece * LANES + h, h * HEAD_DIM + 3 + piece] = -1.0
            oq[0, h * HEAD_DIM + 3 + piece] = 1.0
            ok[0, h * HEAD_DIM + piece] = 1.0
    return jnp.asarray(pq, BF16), jnp.asarray(pk, BF16), jnp.asarray(oq), jnp.asarray(ok)


def _fox_prep_fwd(qkv, fl, bf, qg, kg, seq, name):
    n, d3 = qkv.shape
    d = d3 // 3
    nh = d // HEAD_DIM
    tm = _tile(seq, 256)
    tps = seq // tm
    scale = 1.0 / math.sqrt(HEAD_DIM)
    pq, pk, oq, ok = _bias_placement(nh)

    def body(qkv_ref, fl_ref, bf_ref, qg_ref, kg_ref, pq_ref, pk_ref, oq_ref, ok_ref, q_ref, k_ref, v_ref, carry):
        first = (pl.program_id(0) % tps) == 0
        gm, gt = _head_maps(d)

        def head_norm(xr, gain):
            r = lax.rsqrt(_dot_split(xr * xr, gm) / HEAD_DIM + EPS)
            return xr * _dot_split(r, gt) * gain

        qs = (head_norm(qkv_ref[:, :d], qg_ref[...]).astype(BF16).astype(F32) * scale).astype(BF16)
        kn = head_norm(qkv_ref[:, d:2 * d], kg_ref[...]).astype(BF16)
        v_ref[...] = qkv_ref[:, 2 * d:].astype(BF16)
        z = fl_ref[...] + bf_ref[...]
        logf = jnp.minimum(z, 0.0) - jnp.log1p(jnp.exp(-jnp.abs(z)))
        tri = (lax.broadcasted_iota(jnp.int32, (tm, tm), 0) >= lax.broadcasted_iota(jnp.int32, (tm, tm), 1)).astype(F32)

        @pl.when(first)
        def _():
            carry[...] = jnp.zeros_like(carry)

        c = jnp.dot(tri, logf, precision=HI, preferred_element_type=F32) + carry[...]
        carry[...] = c[tm - 1:tm, :]
        c1 = c.astype(BF16)
        r1 = c - c1.astype(F32)
        c2 = r1.astype(BF16)
        c3 = (r1 - c2.astype(F32)).astype(BF16)
        pieces = jnp.concatenate([c1, c2, c3], axis=1)
        eq = (jnp.dot(pieces, pq_ref[...], preferred_element_type=F32) + oq_ref[...]).astype(BF16)
        ek = (jnp.dot(pieces, pk_ref[...], preferred_element_type=F32) + ok_ref[...]).astype(BF16)
        for h in range(nh):
            lo, hi = h * HEAD_DIM, (h + 1) * HEAD_DIM
            q_ref[:, 2 * lo:2 * lo + HEAD_DIM] = qs[:, lo:hi]
            q_ref[:, 2 * lo + HEAD_DIM:2 * hi] = eq[:, lo:hi]
            k_ref[:, 2 * lo:2 * lo + HEAD_DIM] = kn[:, lo:hi]
            k_ref[:, 2 * lo + HEAD_DIM:2 * hi] = ek[:, lo:hi]

    row = lambda w: pl.BlockSpec((tm, w), lambda i: (i, 0))
    vec = lambda w: pl.BlockSpec((1, w), lambda i: (0, 0))
    full = lambda a: pl.BlockSpec(a.shape, lambda i: (0, 0))
    return pl.pallas_call(
        body, grid=(n // tm,),
        in_specs=[row(d3), row(LANES), vec(LANES), vec(d), vec(d), full(pq), full(pk), full(oq), full(ok)],
        out_specs=[row(2 * d), row(2 * d), row(d)],
        out_shape=[jax.ShapeDtypeStruct((n, 2 * d), BF16)] * 2 + [jax.ShapeDtypeStruct((n, d), BF16)],
        scratch_shapes=[pltpu.VMEM((1, LANES), F32)], name=name, compiler_params=_cp("arbitrary"),
    )(qkv, fl, bf, qg, kg, pq, pk, oq, ok)


def _fox_prep_bwd(qkv, dq, dk, dv, dc1, dc2, fl, bf, qg, kg, seq, name, comm=None):
    n, d3 = qkv.shape
    d = d3 // 3
    tm = _tile(seq, 256)
    tps = seq // tm
    nt = n // tm

    def body(qkv_ref, dq_ref, dk_ref, dv_ref, dc1_ref, dc2_ref, fl_ref, bf_ref, qg_ref, kg_ref,
             dqkv_ref, dfl_ref, dqg_ref, dkg_ref, dbf_ref, carry):
        i = pl.program_id(0)
        tile = nt - 1 - i
        last = (tile % tps) == tps - 1
        gm, gt = _head_maps(d)

        @pl.when(i == 0)
        def _():
            for r_ in (dqg_ref, dkg_ref, dbf_ref):
                r_[...] = jnp.zeros_like(r_)

        @pl.when(last)
        def _():
            carry[...] = jnp.zeros_like(carry)

        def head_norm_bwd(xr, dy, gain, dgain_ref):
            rf = _dot_split(lax.rsqrt(_dot_split(xr * xr, gm) / HEAD_DIM + EPS), gt)
            xn = xr * rf
            dgain_ref[...] += jnp.sum(dy * xn, axis=0, keepdims=True)
            dyg = dy * gain
            mean = _dot_split(dyg * xn, gm) / HEAD_DIM
            return rf * (dyg - xn * _dot_split(mean, gt))

        dqkv_ref[:, :d] = head_norm_bwd(qkv_ref[:, :d], dq_ref[...], qg_ref[...], dqg_ref).astype(BF16)
        dqkv_ref[:, d:2 * d] = head_norm_bwd(qkv_ref[:, d:2 * d], dk_ref[...], kg_ref[...], dkg_ref).astype(BF16)
        dqkv_ref[:, 2 * d:] = dv_ref[...].astype(BF16)

        dc = dc1_ref[...] + dc2_ref[...]
        tri = (lax.broadcasted_iota(jnp.int32, (tm, tm), 0) <= lax.broadcasted_iota(jnp.int32, (tm, tm), 1)).astype(F32)
        dlog = jnp.dot(tri, dc, precision=HI, preferred_element_type=F32) + carry[...]
        carry[...] = dlog[0:1, :]
        dfl = dlog * (1.0 - _sig(fl_ref[...] + bf_ref[...]))
        dfl_ref[...] = dfl.astype(BF16)
        dbf_ref[...] += jnp.sum(dfl, axis=0, keepdims=True)

    row = lambda w: pl.BlockSpec((tm, w), lambda i: (nt - 1 - i, 0))
    vec = lambda w: pl.BlockSpec((1, w), lambda i: (0, 0))
    return _hosted(
        body, comm, grid=(nt,),
        in_specs=[row(d3), row(d), row(d), row(d), row(LANES), row(LANES), row(LANES), vec(LANES), vec(d), vec(d)],
        out_specs=[row(d3), row(LANES), vec(d), vec(d), vec(LANES)],
        out_shape=[jax.ShapeDtypeStruct((n, d3), BF16), jax.ShapeDtypeStruct((n, LANES), BF16),
                   jax.ShapeDtypeStruct((1, d), F32), jax.ShapeDtypeStruct((1, d), F32),
                   jax.ShapeDtypeStruct((1, LANES), F32)],
        scratch_shapes=[pltpu.VMEM((1, LANES), F32)], sem=("arbitrary",), name=name,
        ins=(qkv, dq, dk, dv, dc1, dc2, fl, bf, qg, kg))


def _attn_specs(bsz, seq, t):
    nb = seq // t
    blk = lambda w: pl.BlockSpec((t, w), lambda b, h, i: (b * nb + i, h))
    full = lambda w: pl.BlockSpec((seq, w), lambda b, h, i: (b, h))
    col = pl.BlockSpec((None, None, t, 2), lambda b, h, i: (b, h, i, 0))
    rows = pl.BlockSpec((None, None, nb, 2, t), lambda b, h, i: (b, h, 0, 0, 0))
    return nb, blk, full, col, rows


_NT = (((1,), (1,)), ((), ()))
ATTN_TILE = 512


def _head_lanes(t, hh):
    lane = lax.broadcasted_iota(jnp.int32, (t, LANES), 1)
    return (lane < HEAD_DIM) if hh == 0 else (lane >= HEAD_DIM)


def _flash_fwd(qa, ka, v, bsz, seq, name):
    n, d = v.shape
    hp = d // LANES
    t = _tile(seq, ATTN_TILE)
    nb, blk, full, col, _ = _attn_specs(bsz, seq, t)

    def body(q_ref, k_ref, v_ref, o_ref, lse_ref):
        i = pl.program_id(2)
        causal = lax.broadcasted_iota(jnp.int32, (t, t), 0) >= lax.broadcasted_iota(jnp.int32, (t, t), 1)

        def block(j, carry, masked):
            rs = pl.ds(pl.multiple_of(j * t, t), t)
            vj = v_ref[rs, :]
            out = []
            for hh in range(2):
                m, l, acc = carry[hh]
                hs = slice(hh * LANES, (hh + 1) * LANES)
                sc = lax.dot_general(q_ref[:, hs], k_ref[rs, hs], _NT, preferred_element_type=F32)
                if masked:
                    sc = jnp.where(causal, sc, NEG)
                mn = jnp.maximum(m, jnp.max(sc, axis=-1, keepdims=True))
                p = jnp.exp(sc - mn)
                al = jnp.exp(m - mn)
                l = al * l + jnp.sum(p, axis=-1, keepdims=True)
                acc = al * acc + jnp.dot(p.astype(BF16), vj, preferred_element_type=F32)
                out.append((mn, l, acc))
            return tuple(out)

        init = tuple((jnp.full((t, 1), NEG, F32), jnp.zeros((t, 1), F32), jnp.zeros((t, LANES), F32))
                     for _ in range(2))
        carry = lax.fori_loop(0, i, lambda j, c: block(j, c, False), init)
        (m0, l0, a0), (m1, l1, a1) = block(i, carry, True)
        o_ref[...] = jnp.where(_head_lanes(t, 0), a0 / l0, a1 / l1)
        lse_ref[:, 0:1] = m0 + jnp.log(l0)
        lse_ref[:, 1:2] = m1 + jnp.log(l1)

    return pl.pallas_call(
        body, grid=(bsz, hp, nb), in_specs=[blk(2 * LANES), full(2 * LANES), full(LANES)],
        out_specs=[blk(LANES), col],
        out_shape=[jax.ShapeDtypeStruct((n, d), F32), jax.ShapeDtypeStruct((bsz, hp, seq, 2), F32)],
        name=name, compiler_params=_cp("parallel", "parallel", "arbitrary"),
    )(qa, ka, v)


def _flash_bwd(qa, ka, v, do, o, lse_row, bsz, seq, name):
    n, d = v.shape
    hp = d // LANES
    t = _tile(seq, ATTN_TILE)
    nb, blk, full, col, rows = _attn_specs(bsz, seq, t)
    scale = 1.0 / math.sqrt(HEAD_DIM)
    tn_ = (((0,), (0,)), ((), ()))

    def body(k_ref, v_ref, q_ref, do_ref, o_ref, lse_ref, dq_ref, dcc_ref, dk_ref, dv_ref, dck_ref, dqa, dl):
        j = pl.program_id(2)
        causal = lax.broadcasted_iota(jnp.int32, (t, t), 1) >= lax.broadcasted_iota(jnp.int32, (t, t), 0)
        heads = [_head_lanes(t, 0), _head_lanes(t, 1)]

        @pl.when(j == 0)
        def _():
            dqa[...] = jnp.zeros_like(dqa)
            ones = jnp.ones((8, LANES), F32)
            for ib in range(nb):
                rs = slice(ib * t, (ib + 1) * t)
                prod = do_ref[rs, :] * o_ref[rs, :]
                for hh in range(2):
                    dl[ib, hh:hh + 1, :] = lax.dot_general(ones, jnp.where(heads[hh], prod, 0.0), _NT, precision=HI,
                                                           preferred_element_type=F32)[0:1]

        vj = v_ref[...]

        def block(i, carry, masked):
            rs = pl.ds(pl.multiple_of(i * t, t), t)
            doi = do_ref[rs, :]
            dks, dvp = list(carry[:2]), carry[2]
            for hh in range(2):
                hs = slice(hh * LANES, (hh + 1) * LANES)
                kh, qi = k_ref[:, hs], q_ref[rs, hs]
                dom = jnp.where(heads[hh], doi, 0.0).astype(BF16)
                st = lax.dot_general(kh, qi, _NT, preferred_element_type=F32)
                if masked:
                    st = jnp.where(causal, st, NEG)
                pt = jnp.exp(st - lse_ref[i, hh:hh + 1, :])
                dvp = dvp + jnp.dot(pt.astype(BF16), dom, preferred_element_type=F32)
                dpt = lax.dot_general(vj, dom, _NT, preferred_element_type=F32)
                dsb = (pt * (dpt - dl[i, hh:hh + 1, :])).astype(BF16)
                dks[hh] = dks[hh] + jnp.dot(dsb, qi, preferred_element_type=F32)
                dqa[rs, hs] += lax.dot_general(dsb, kh, tn_, preferred_element_type=F32)
            return dks[0], dks[1], dvp

        zero = jnp.zeros((t, LANES), F32)
        carry = block(j, (zero, zero, zero), True)
        dk0, dk1, dvp = lax.fori_loop(j + 1, nb, lambda i, c: block(i, c, False), carry)
        dk_ref[...] = jnp.where(heads[0], dk0, pltpu.roll(dk1, HEAD_DIM, 1))
        dv_ref[...] = dvp
        dck_ref[:, 0:1] = -dk0[:, HEAD_DIM + 3:HEAD_DIM + 4]
        dck_ref[:, 1:2] = -dk1[:, HEAD_DIM + 3:HEAD_DIM + 4]

        @pl.when(j == nb - 1)
        def _():
            first = lax.broadcasted_iota(jnp.int32, (seq, LANES), 1) < HEAD_DIM
            dq_ref[...] = jnp.where(first, dqa[:, :LANES], pltpu.roll(dqa[:, LANES:], HEAD_DIM, 1)) * scale
            for hh in range(2):
                lo = hh * LANES + HEAD_DIM
                dcc_ref[:, hh:hh + 1] = dqa[:, lo:lo + 1]

    whole_col = pl.BlockSpec((None, None, seq, 2), lambda b, h, i: (b, h, 0, 0))
    cshape = jax.ShapeDtypeStruct((bsz, hp, seq, 2), F32)
    nd = jax.ShapeDtypeStruct((n, d), F32)
    return pl.pallas_call(
        body, grid=(bsz, hp, nb),
        in_specs=[blk(2 * LANES), blk(LANES), full(2 * LANES), full(LANES), full(LANES), rows],
        out_specs=[full(LANES), whole_col, blk(LANES), blk(LANES), col],
        out_shape=[nd, cshape, nd, nd, cshape],
        scratch_shapes=[pltpu.VMEM((seq, 2 * LANES), F32), pltpu.VMEM((nb, 2, t), F32)],
        name=name, compiler_params=_cp("parallel", "parallel", "arbitrary"),
    )(ka, v, qa, do, o, lse_row)


def _adamw(w, g, m, v, name):
    r, c = w.shape
    tr = r
    for cand in (512, 256, 128, 64, 32, 16, 8):
        if r % cand == 0 and r > cand and cand * c * 4 <= 4 * 1024 * 1024:
            tr = cand
            break

    def body(w_ref, g_ref, m_ref, v_ref, d_ref, m2_ref, v2_ref):
        gv = g_ref[...]
        m2 = ADAM_B1 * m_ref[...] + (1.0 - ADAM_B1) * gv
        v2 = ADAM_B2 * v_ref[...] + (1.0 - ADAM_B2) * jnp.square(gv)
        m_hat = m2 / (1.0 - ADAM_B1 ** ADAM_STEP)
        v_hat = v2 / (1.0 - ADAM_B2 ** ADAM_STEP)
        d_ref[...] = -ADAM_LR * (m_hat / (jnp.sqrt(v_hat) + ADAM_EPS) + ADAM_WD * w_ref[...])
        m2_ref[...] = m2
        v2_ref[...] = v2

    blk = pl.BlockSpec((tr, c), lambda i: (i, 0))
    shp = jax.ShapeDtypeStruct((r, c), F32)
    return pl.pallas_call(
        body, grid=(r // tr,), in_specs=[blk] * 4, out_specs=[blk] * 3, out_shape=[shp] * 3,
        name=name, compiler_params=_cp("parallel"),
    )(w, g, m, v)


def _adamw_nd(w, g, m, v, name):
    shape = w.shape
    two = (math.prod(shape[:-1]), shape[-1])
    d_, m_, v_ = _adamw(w.reshape(two), g.reshape(two), m.reshape(two), v.reshape(two), name)
    return g.reshape(shape), d_.reshape(shape), m_.reshape(shape), v_.reshape(shape)


def _place():
    x, y, c = lax.axis_index("x"), lax.axis_index("y"), lax.axis_index("c")
    chips = [(1 - x, y), (x, 1 - y), (1 - x, 1 - y)]
    return x, y, c, chips


def _gather_chips(shards):
    nt = len(shards)
    halves = [s.shape[0] // 2 for s in shards]

    def copies(ins, outs, send_sems, recv_sems):
        x, y, c, chips = _place()
        cps = []
        for t in range(nt):
            rows = pl.ds(c * halves[t], halves[t])
            for jj, (cx, cy) in enumerate(chips):
                cps.append(pltpu.make_async_remote_copy(
                    src_ref=ins[t].at[rows, :], dst_ref=outs[t].at[2 * x + y, rows, :], send_sem=send_sems.at[3 * t + jj],
                    recv_sem=recv_sems.at[3 * t + jj], device_id=(cx, cy, c), device_id_type=MESH))
        return cps

    def start(*refs):
        for cp in copies(*refs):
            cp.start()

    def wait(*refs):
        for cp in copies(*refs):
            cp.wait()

    return _Comm(shards, [jax.ShapeDtypeStruct((N_CHIPS,) + s.shape, s.dtype) for s in shards], 3 * nt, start, wait)


def _gather_sibling(shards, bufs, name):
    nt = len(shards)
    halves = [s.shape[0] // 2 for s in shards]

    def body(*refs):
        ins, outs = refs[:nt], refs[2 * nt:3 * nt]
        send_sems, recv_sems = refs[3 * nt:]
        x, y, c, chips = _place()
        sibling = (x, y, 1 - c)

        def copy(t, jj, hf):
            cx, cy = chips[jj]
            region = outs[t].at[2 * cx + cy, pl.ds(hf * halves[t], halves[t]), :]
            return pltpu.make_async_remote_copy(src_ref=region, dst_ref=region, send_sem=send_sems.at[4 * t + jj],
                                                recv_sem=recv_sems.at[4 * t + jj], device_id=sibling,
                                                device_id_type=MESH)

        def own(t):
            return pltpu.make_async_remote_copy(src_ref=ins[t], dst_ref=outs[t].at[2 * x + y],
                                                send_sem=send_sems.at[4 * t + 3], recv_sem=recv_sems.at[4 * t + 3],
                                                device_id=sibling, device_id_type=MESH)

        sends = [copy(t, jj, c) for t in range(nt) for jj in range(3)] + [own(t) for t in range(nt)]
        for cp in sends:
            cp.start()
        for t in range(nt):
            for jj in range(3):
                copy(t, jj, 1 - c).wait_recv()
            own(t).wait_recv()
        for cp in sends:
            cp.wait_send()

    return pl.pallas_call(
        body, in_specs=[ANY] * (2 * nt), out_specs=[ANY] * nt,
        out_shape=[jax.ShapeDtypeStruct(b.shape, b.dtype) for b in bufs],
        scratch_shapes=[pltpu.SemaphoreType.DMA((4 * nt,)), pltpu.SemaphoreType.DMA((4 * nt,))],
        input_output_aliases={nt + t: t for t in range(nt)}, name=name,
        compiler_params=pltpu.CompilerParams(has_side_effects=True),
    )(*shards, *bufs)


def _send_sibling_halves(grads, name):
    nt = len(grads)
    halves = [g.shape[1] // 2 for g in grads]

    def body(*refs):
        ins, outs = refs[:nt], refs[nt:2 * nt]
        send_sems, recv_sems = refs[2 * nt:]
        x, y, c, _ = _place()
        cps = []
        for t in range(nt):
            src = ins[t].at[:, pl.ds((1 - c) * halves[t], halves[t]), :]
            cps.append(pltpu.make_async_remote_copy(src_ref=src, dst_ref=outs[t], send_sem=send_sems.at[t],
                                                    recv_sem=recv_sems.at[t], device_id=(x, y, 1 - c),
                                                    device_id_type=MESH))
        for cp in cps:
            cp.start()
        for cp in cps:
            cp.wait()

    return pl.pallas_call(
        body, in_specs=[ANY] * nt, out_specs=[ANY] * nt,
        out_shape=[jax.ShapeDtypeStruct((N_CHIPS, h, g.shape[2]), g.dtype) for g, h in zip(grads, halves)],
        scratch_shapes=[pltpu.SemaphoreType.DMA((nt,)), pltpu.SemaphoreType.DMA((nt,))],
        name=name, compiler_params=pltpu.CompilerParams(has_side_effects=True),
    )(*grads)


def _add_sibling(g, r1, c_idx, out_dtype, name):
    s, r, w = g.shape
    rh = r // 2
    tr = rh
    for cand in (512, 256, 128, 64, 32, 16, 8):
        if rh % cand == 0 and cand * w * 4 <= 4 * 1024 * 1024:
            tr = cand
            break
    per = rh // tr

    def body(c_ref, g_ref, r_ref, o_ref):
        o_ref[...] = (g_ref[...] + r_ref[...]).astype(o_ref.dtype)

    return pl.pallas_call(
        body,
        grid_spec=pltpu.PrefetchScalarGridSpec(
            num_scalar_prefetch=1, grid=(s, per),
            in_specs=[pl.BlockSpec((None, tr, w), lambda a, b, c_ref: (a, c_ref[0] * per + b, 0)),
                      pl.BlockSpec((None, tr, w), lambda a, b, c_ref: (a, b, 0))],
            out_specs=pl.BlockSpec((None, tr, w), lambda a, b, c_ref: (a, b, 0))),
        out_shape=jax.ShapeDtypeStruct((s, rh, w), out_dtype), name=name,
        compiler_params=_cp("parallel", "parallel"),
    )(c_idx, g, r1)


def _chip_slots(parts):
    nt = len(parts)

    def copies(ins, outs, send_sems, recv_sems):
        x, y, c, chips = _place()
        return [pltpu.make_async_remote_copy(
            src_ref=ins[t].at[2 * cx + cy], dst_ref=outs[t].at[jj], send_sem=send_sems.at[3 * t + jj],
            recv_sem=recv_sems.at[3 * t + jj], device_id=(cx, cy, c), device_id_type=MESH)
            for t in range(nt) for jj, (cx, cy) in enumerate(chips)]

    def start(*refs):
        for cp in copies(*refs):
            cp.start()

    def wait(*refs):
        for cp in copies(*refs):
            cp.wait()

    return _Comm(parts, [jax.ShapeDtypeStruct((3,) + p.shape[1:], p.dtype) for p in parts], 3 * nt, start, wait)


def _add_chips(part, r2, place_idx, name):
    s, rh, w = part.shape
    tr = rh
    for cand in (512, 256, 128, 64, 32, 16, 8):
        if rh % cand == 0 and cand * w * 4 <= 4 * 1024 * 1024:
            tr = cand
            break
    per = rh // tr

    def body(b_ref, p_ref, r0_ref, r1_ref, r2_ref, o_ref):
        up = lambda ref: ref[...].astype(F32)
        o_ref[...] = ((up(p_ref) + up(r0_ref)) + up(r1_ref)) + up(r2_ref)

    return pl.pallas_call(
        body,
        grid_spec=pltpu.PrefetchScalarGridSpec(
            num_scalar_prefetch=1, grid=(per,),
            in_specs=[pl.BlockSpec((None, tr, w), lambda a, b_ref: (b_ref[0], a, 0))]
            + [pl.BlockSpec((None, tr, w), lambda a, b_ref, jj=jj: (jj, a, 0)) for jj in range(3)],
            out_specs=pl.BlockSpec((tr, w), lambda a, b_ref: (b_ref[1] * per + a, 0))),
        out_shape=jax.ShapeDtypeStruct((2 * rh, w), F32), name=name, compiler_params=_cp("parallel"),
    )(place_idx, part, r2, r2, r2)


def _swap_halves(bufs, name):
    nt = len(bufs)

    def body(*refs):
        outs = refs[nt:2 * nt]
        send_sems, recv_sems = refs[2 * nt:]
        x, y, c, _ = _place()

        def copy(t, hf):
            rh = outs[t].shape[0] // 2
            region = outs[t].at[pl.ds(hf * rh, rh), :]
            return pltpu.make_async_remote_copy(src_ref=region, dst_ref=region, send_sem=send_sems.at[t],
                                                recv_sem=recv_sems.at[t], device_id=(x, y, 1 - c),
                                                device_id_type=MESH)

        sends = [copy(t, c) for t in range(nt)]
        for cp in sends:
            cp.start()
        for t in range(nt):
            copy(t, 1 - c).wait_recv()
        for cp in sends:
            cp.wait_send()

    return pl.pallas_call(
        body, in_specs=[ANY] * nt, out_specs=[ANY] * nt,
        out_shape=[jax.ShapeDtypeStruct(b.shape, b.dtype) for b in bufs],
        scratch_shapes=[pltpu.SemaphoreType.DMA((nt,)), pltpu.SemaphoreType.DMA((nt,))],
        input_output_aliases={t: t for t in range(nt)}, name=name,
        compiler_params=pltpu.CompilerParams(has_side_effects=True),
    )(*bufs)


def _rs_begin(grads, wire, c_idx, tag):
    r1 = _send_sibling_halves(grads, name=f"rs_sibling_{tag}")
    return [_add_sibling(g, r, c_idx, wire[t], name=f"rs_add_sibling_{tag}_{t}")
            for t, (g, r) in enumerate(zip(grads, r1))]


def _rs_finish(parts, r2, chip_idx, tag):
    fin = [_add_chips(p, r, chip_idx, name=f"rs_add_chips_{tag}_{t}") for t, (p, r) in enumerate(zip(parts, r2))]
    return _swap_halves(fin, name=f"rs_swap_{tag}")


def _all_reduce_small(v, name):
    r, w = v.shape

    def body(v_ref, o_ref, buf, send_sems, recv_sems):
        x, y, c, _ = _place()
        me = 4 * x + 2 * y + c
        buf[me] = v_ref[...]
        cps = []
        for kk in range(1, 8):
            peer = (x ^ ((kk >> 2) & 1), y ^ ((kk >> 1) & 1), c ^ (kk & 1))
            cps.append(pltpu.make_async_remote_copy(src_ref=v_ref, dst_ref=buf.at[me], send_sem=send_sems.at[kk - 1],
                                                    recv_sem=recv_sems.at[kk - 1], device_id=peer, device_id_type=MESH))
        for cp in cps:
            cp.start()
        for kk in range(1, 8):
            pltpu.make_async_remote_copy(src_ref=v_ref, dst_ref=buf.at[me ^ kk], send_sem=send_sems.at[kk - 1],
                                         recv_sem=recv_sems.at[kk - 1], device_id=(x, y, c),
                                         device_id_type=MESH).wait_recv()
        for cp in cps:
            cp.wait_send()
        acc = buf[0]
        for dev in range(1, 8):
            acc = acc + buf[dev]
        o_ref[...] = acc

    vm = pl.BlockSpec(memory_space=pltpu.VMEM)
    return pl.pallas_call(
        body, in_specs=[vm], out_specs=vm, out_shape=jax.ShapeDtypeStruct((r, w), F32),
        scratch_shapes=[pltpu.VMEM((8, r, w), F32), pltpu.SemaphoreType.DMA((7,)), pltpu.SemaphoreType.DMA((7,))],
        name=name, compiler_params=pltpu.CompilerParams(has_side_effects=True),
    )(v)


def _to_shards(a, axis=-1):
    axis = axis % a.ndim
    shp = a.shape
    a = a.reshape(shp[:axis] + (N_CHIPS, shp[axis] // N_CHIPS) + shp[axis + 1:])
    return jnp.moveaxis(a, axis, 0).reshape(N_CHIPS, -1)


def _from_shards(s, shard_shape, axis=-1):
    axis = axis % len(shard_shape)
    a = jnp.moveaxis(s.reshape((N_CHIPS,) + tuple(shard_shape)), 0, axis)
    return a.reshape(tuple(shard_shape[:axis]) + (N_CHIPS * shard_shape[axis],) + tuple(shard_shape[axis + 1:]))


def _pack(vecs, rows):
    flat = jnp.concatenate([v.reshape(v.shape[0], -1) if v.ndim > 1 else v.reshape(1, -1) for v in vecs], axis=1)
    lead = flat.shape[0]
    flat = jnp.pad(flat, ((0, 0), (0, rows * LANES - flat.shape[1])))
    return flat.reshape(lead, rows, LANES)


def _pack_rows(sizes, mult):
    total = sum(sizes)
    rows = -(-total // LANES)
    return -(-rows // mult) * mult


def _unpack(flat, shapes):
    out, pos = [], 0
    for shp in shapes:
        sz = math.prod(shp)
        out.append(flat[..., pos:pos + sz].reshape(flat.shape[:-1] + tuple(shp)))
        pos += sz
    return out


def _row_layout(col, t):
    bsz, hp, seq, _ = col.shape
    return col.reshape(bsz, hp, seq // t, t, 2).transpose(0, 1, 2, 4, 3)


def _from_col_layout(col):
    bsz, hp, seq, _ = col.shape
    a = col.transpose(0, 2, 1, 3).reshape(bsz * seq, 2 * hp)
    return jnp.pad(a, ((0, 0), (0, LANES - 2 * hp)))


def kernel(x, norm_mix, norm_ffn, conv_w_in, conv_b_in, conv_dw, conv_dw_b, conv_ln_g, conv_ln_b, conv_w_out, conv_b_out, pool_w, pool_b, pool_scale, fox_w_in, fox_b_f, fox_q_gain, fox_k_gain, fox_w_o, ffn_w_up, ffn_dw, ffn_dw_b, ffn_w_down, loss_target, m_norm_mix, m_norm_ffn, m_conv_w_in, m_conv_b_in, m_conv_dw, m_conv_dw_b, m_conv_ln_g, m_conv_ln_b, m_conv_w_out, m_conv_b_out, m_pool_w, m_pool_b, m_pool_scale, m_fox_w_in, m_fox_b_f, m_fox_q_gain, m_fox_k_gain, m_fox_w_o, m_ffn_w_up, m_ffn_dw, m_ffn_dw_b, m_ffn_w_down, v_norm_mix, v_norm_ffn, v_conv_w_in, v_conv_b_in, v_conv_dw, v_conv_dw_b, v_conv_ln_g, v_conv_ln_b, v_conv_w_out, v_conv_b_out, v_pool_w, v_pool_b, v_pool_scale, v_fox_w_in, v_fox_b_f, v_fox_q_gain, v_fox_k_gain, v_fox_w_o, v_ffn_w_up, v_ffn_dw, v_ffn_dw_b, v_ffn_w_down):
    bsz, seq, d = x.shape
    n = bsz * seq
    depth = norm_mix.shape[0]
    n_conv, n_pool, n_fox = conv_w_in.shape[0], pool_w.shape[0], fox_w_in.shape[0]
    f2 = ffn_dw_b.shape[1]
    f = f2 // 2
    nh = d // HEAD_DIM
    hp = d // LANES
    ng = len(POOL_WINDOWS)
    gd = d // ng
    c_idx = lax.axis_index("c").astype(jnp.int32).reshape(1)
    chip_idx = jnp.stack([2 * lax.axis_index("x") + lax.axis_index("y"), lax.axis_index("c")]).astype(jnp.int32)

    small_shapes = [conv_b_in.shape, conv_dw.shape, conv_dw_b.shape, conv_ln_g.shape, conv_ln_b.shape,
                    conv_b_out.shape, pool_b.shape, ffn_dw.shape]
    small_rows = _pack_rows([math.prod(s) for s in small_shapes], 16)
    small = _pack([v.reshape(1, -1) for v in (conv_b_in, conv_dw, conv_dw_b, conv_ln_g, conv_ln_b, conv_b_out,
                                                pool_b, ffn_dw)], small_rows)[0]
    def layer_shards(i):
        kind, j = i % 3, i // 3
        shards = [ffn_w_up[i].astype(BF16), ffn_w_down[i].astype(BF16)]
        if kind == 0:
            shards += [conv_w_in[j].astype(BF16), conv_w_out[j].astype(BF16)]
        elif kind == 1:
            shards += [pool_w[j].reshape(ng * (gd // N_CHIPS), gd).astype(BF16)]
        else:
            shards += [fox_w_in[j].astype(BF16), fox_w_o[j].astype(BF16)]
        if i == 0:
            shards.append(small)
        return shards

    gathered = [None] * depth
    gathered[0] = _gather_sibling(layer_shards(0), _run_comm(_gather_chips(layer_shards(0)), name="gather_chips_l0"),
                                  name="gather_sibling_l0")
    small_all = gathered[0][-1].reshape(N_CHIPS, -1)
    sm = _unpack(small_all, small_shapes)
    axes = [-1] * 8
    b_in_f, dw_f, dw_b_f, ln_g_f, ln_b_f, b_out_f, pool_b_f, ffn_dw_f = [
        _from_shards(s_.reshape(N_CHIPS, -1), shp, ax) for s_, shp, ax in zip(sm, small_shapes, axes)]

    xs = x.reshape(n, d)
    tgt = loss_target.reshape(n, d)
    vec = lambda a: a.reshape(1, -1)

    saved = []
    cur = xs
    for i in range(depth):
        kind, j = i % 3, i // 3
        wts = gathered[i]
        sv = {"x_in": cur}
        if kind == 0:
            w_in, w_out = wts[2], wts[3].reshape(d, d)
            wcol = w_in.shape[2]
            h = _rms_fwd(cur, vec(norm_mix[i]), name=f"rms_mix_l{i}")
            pa = _mm(h, w_in, m=n, n=d, k=d, tk=d, tn=wcol, b_stk=wcol, b_s0=0, bias=vec(b_in_f[j, :d]),
                     name=f"conv_in_a_l{i}")
            pg = _mm(h, w_in, m=n, n=d, k=d, tk=d, tn=wcol, b_stk=wcol, b_s0=2, bias=vec(b_in_f[j, d:]),
                     name=f"conv_in_g_l{i}")
            taps = jnp.pad(dw_f[j], ((0, CONV_HALO - CONV_WIDTH), (0, 0)))
            u, s_ = _conf_fwd(pa, pg, taps, vec(dw_b_f[j]), vec(ln_g_f[j]), vec(ln_b_f[j]), seq, name=f"conf_fwd_l{i}")
            cur = _mm(s_, w_out, m=n, n=d, k=d, tk=d, bias=vec(b_out_f[j]), res=cur, name=f"conv_out_l{i}")
            sv.update(h=h, pa=pa, pg=pg, u=u, s=s_, taps=taps)
        elif kind == 1:
            pw = wts[2].reshape(N_CHIPS, ng, gd // N_CHIPS, gd).transpose(1, 0, 2, 3).reshape(ng, gd, gd)
            cur, p = _pool_fwd(cur, vec(norm_mix[i]), pw, vec(pool_b_f[j]), vec(pool_scale[j]), seq,
                               name=f"pool_fwd_l{i}")
            sv.update(p=p, pw=pw)
        else:
            w_in = wts[2].transpose(1, 0, 2).reshape(d, -1)
            w_qkv = w_in[:, :3 * d]
            w_f = jnp.pad(w_in[:, 3 * d:], ((0, 0), (0, LANES - nh)))
            w_o = wts[3].reshape(d, d)
            bf = jnp.pad(vec(fox_b_f[j]), ((0, 0), (0, LANES - nh)))
            qg, kg = jnp.tile(vec(fox_q_gain[j]), (1, nh)), jnp.tile(vec(fox_k_gain[j]), (1, nh))
            h = _rms_fwd(cur, vec(norm_mix[i]), name=f"rms_mix_l{i}")
            qkv = _mm(h, w_qkv, m=n, n=3 * d, k=d, tk=d, tn=d, name=f"fox_qkv_l{i}")
            fl = _mm(h, w_f, m=n, n=LANES, k=d, tk=d, name=f"fox_fl_l{i}")
            qa, ka, v = _fox_prep_fwd(qkv, fl, bf, qg, kg, seq, name=f"fox_prep_l{i}")
            o, lse = _flash_fwd(qa, ka, v, bsz, seq, name=f"fox_attn_l{i}")
            cur = _mm(o, w_o, m=n, n=d, k=d, tk=d, res=cur, name=f"fox_out_l{i}")
            sv.update(h=h, qkv=qkv, fl=fl, qa=qa, ka=ka, v=v, o=o, lse=lse, w_qkv=w_qkv, w_f=w_f, w_o=w_o, bf=bf,
                      qg=qg, kg=kg)
        w_up, w_down = wts[0], wts[1].reshape(f, d)
        ucol = w_up.shape[2]
        sv["x_mid"] = cur
        h2 = _rms_fwd(cur, vec(norm_ffn[i]), name=f"rms_ffn_l{i}")
        uv = _mm(h2, w_up, m=n, n=f, k=d, tk=d, tn=ucol, b_stk=ucol, b_s0=0, name=f"ffn_up_v_l{i}")
        ug = _mm(h2, w_up, m=n, n=f, k=d, tk=d, tn=ucol, b_stk=ucol, b_s0=2, name=f"ffn_up_g_l{i}")
        fdw, fdb = ffn_dw_f[i], ffn_dw_b[i]
        glu_args = (uv, ug, fdw[:, :f], fdw[:, f:], vec(fdb[:f]), vec(fdb[f:]), seq)
        if i + 1 < depth:
            nxt_shards = layer_shards(i + 1)
            (a_,), landed = _ffn_glu_fwd(*glu_args, name=f"ffn_glu_l{i}", comm=_gather_chips(nxt_shards))
            gathered[i + 1] = _gather_sibling(nxt_shards, landed, name=f"gather_sibling_l{i + 1}")
        else:
            (a_,) = _ffn_glu_fwd(*glu_args, name=f"ffn_glu_l{i}")
        cur = _mm(a_, w_down, m=n, n=d, k=f, tk=f, tn=d, res=cur, name=f"ffn_down_l{i}")
        sv.update(h2=h2, uv=uv, ug=ug, a=a_)
        saved.append(sv)

    dy, loss_part = _loss(cur, tgt, name="loss")
    loss = lax.psum(loss_part[0, 0], ("x", "y", "c"))

    g_norm_mix, g_norm_ffn = [None] * depth, [None] * depth
    g_ffn_dw_b = [None] * depth
    g_up, g_down = [None] * depth, [None] * depth
    g_small = {}
    g_conv_in, g_conv_out = [None] * n_conv, [None] * n_conv
    g_pool_w = g_fox_in = g_fox_o = None
    g_pool_scale = g_bf = g_qg = g_kg = None
    part_small = {"b_in": [None] * n_conv, "dw": [None] * n_conv, "dw_b": [None] * n_conv, "ln_g": [None] * n_conv,
                  "ln_b": [None] * n_conv, "b_out": [None] * n_conv, "pool_b": None, "ffn_dw": [None] * depth}

    pending = []
    done = {}

    def take_pending():
        groups = list(pending)
        pending.clear()
        parts = [p_ for g_ in groups for p_ in g_[0]]
        return groups, (_chip_slots(parts) if parts else None)

    def finish_groups(groups, r2):
        pos = 0
        for parts, tag, sink in groups:
            sink(_rs_finish(parts, r2[pos:pos + len(parts)], chip_idx, tag))
            pos += len(parts)

    def carried(fn, args, name):
        groups, comm = take_pending()
        if comm is None:
            return fn(*args, name=name)
        outs, r2 = fn(*args, name=name, comm=comm)
        finish_groups(groups, r2)
        return outs

    dcur = dy
    for i in reversed(range(depth)):
        kind, j = i % 3, i // 3
        wts, sv = gathered[i], saved[i]
        w_up, w_down = wts[0], wts[1].reshape(f, d)
        ucol = w_up.shape[2]
        fdw, fdb = ffn_dw_f[i], ffn_dw_b[i]
        da = _mm(dcur, w_down, m=n, n=f, k=d, tk=d, tn=f, tb=True, name=f"ffn_down_dx_l{i}")
        dw_down = _mm(sv["a"], dcur, m=f, n=d, k=n, tm=f // 2, tn=d, tk=2048, ta=True, name=f"ffn_down_dw_l{i}")
        duv, dug, dwv, dwg, dbv, dbg = carried(
            _ffn_glu_bwd, (sv["uv"], sv["ug"], da, fdw[:, :f], fdw[:, f:], vec(fdb[:f]), vec(fdb[f:]), seq),
            name=f"ffn_glu_bwd_l{i}")
        dw_up = _mm(sv["h2"], duv, m=d, n=f, k=n, tm=d, tn=ucol, tk=2048, ta=True, o_stk=ucol, o_s0=0,
                    o_slots=N_CHIPS, name=f"ffn_up_dw_v_l{i}")
        dw_up = _mm(sv["h2"], dug, m=d, n=f, k=n, tm=d, tn=ucol, tk=2048, ta=True, o_stk=ucol, o_s0=2,
                    o_slots=N_CHIPS, o_buf=dw_up, name=f"ffn_up_dw_g_l{i}")
        dh2 = _mm(duv, w_up, m=n, n=d, k=f, tn=d, tk=ucol, tb=True, b_stk=ucol, b_s0=0, name=f"ffn_up_dx_v_l{i}")
        dh2 = _mm(dug, w_up, m=n, n=d, k=f, tn=d, tk=ucol, tb=True, b_stk=ucol, b_s0=2, res=dh2,
                  name=f"ffn_up_dx_g_l{i}")
        dmid, g_norm_ffn[i] = _rms_bwd(sv["x_mid"], dh2, vec(norm_ffn[i]), dcur, name=f"rms_ffn_bwd_l{i}")
        part_small["ffn_dw"][i] = jnp.concatenate([dwv, dwg], axis=1)
        g_ffn_dw_b[i] = jnp.concatenate([dbv, dbg], axis=1)

        def ffn_sink(red, i=i):
            g_up[i], g_down[i] = red[0], red[1]

        pending.append((_rs_begin([dw_up, dw_down.reshape(N_CHIPS, f // N_CHIPS, d)], [BF16, BF16], c_idx,
                                  tag=f"ffn_l{i}"), f"ffn_l{i}", ffn_sink))

        if kind == 0:
            w_in, w_out = wts[2], wts[3].reshape(d, d)
            wcol = w_in.shape[2]
            ds = _mm(dmid, w_out, m=n, n=d, k=d, tk=d, tb=True, name=f"conv_out_dx_l{i}")
            dw_out = _mm(sv["s"], dmid, m=d, n=d, k=n, tm=d, tn=d, tk=2048, ta=True, name=f"conv_out_dw_l{i}")
            dpa, dpg, ddw, ddwb, dlng, dlnb, dba, dbg_ = carried(
                _conf_bwd, (sv["u"], ds, sv["pa"], sv["pg"], sv["taps"], vec(ln_g_f[j]), vec(ln_b_f[j]), seq),
                name=f"conf_bwd_l{i}")
            dw_in = _mm(sv["h"], dpa, m=d, n=d, k=n, tm=d, tn=wcol, tk=2048, ta=True, o_stk=wcol, o_s0=0,
                        o_slots=N_CHIPS, name=f"conv_in_dw_a_l{i}")
            dw_in = _mm(sv["h"], dpg, m=d, n=d, k=n, tm=d, tn=wcol, tk=2048, ta=True, o_stk=wcol, o_s0=2,
                        o_slots=N_CHIPS, o_buf=dw_in, name=f"conv_in_dw_g_l{i}")
            dh = _mm(dpa, w_in, m=n, n=d, k=d, tn=d, tk=wcol, tb=True, b_stk=wcol, b_s0=0, name=f"conv_in_dx_a_l{i}")
            dh = _mm(dpg, w_in, m=n, n=d, k=d, tn=d, tk=wcol, tb=True, b_stk=wcol, b_s0=2, res=dh,
                     name=f"conv_in_dx_g_l{i}")
            dcur, g_norm_mix[i], db_out = _rms_bwd(sv["x_in"], dh, vec(norm_mix[i]), dmid, name=f"rms_mix_bwd_l{i}",
                                                   colsum=True)
            part_small["b_in"][j] = jnp.concatenate([dba, dbg_], axis=1)
            part_small["dw"][j] = ddw[:CONV_WIDTH]
            part_small["dw_b"][j], part_small["ln_g"][j], part_small["ln_b"][j] = ddwb, dlng, dlnb
            part_small["b_out"][j] = db_out
            mix_grads = [dw_in, dw_out.reshape(N_CHIPS, d // N_CHIPS, d)]
        elif kind == 1:
            dcur, g_norm_mix[i], dpw, dpb, dpsc = _pool_bwd(dmid, sv["x_in"], sv["p"], vec(norm_mix[i]), sv["pw"],
                                                            vec(pool_b_f[j]), vec(pool_scale[j]), seq,
                                                            name=f"pool_bwd_l{i}")
            part_small["pool_b"] = dpb
            g_pool_scale = dpsc
            mix_grads = [dpw.reshape(ng, N_CHIPS, gd // N_CHIPS, gd).transpose(1, 0, 2, 3).reshape(N_CHIPS, gd, gd)]
        else:
            do = _mm(dmid, sv["w_o"], m=n, n=d, k=d, tk=d, tb=True, name=f"fox_out_dx_l{i}")
            dw_o = _mm(sv["o"], dmid, m=d, n=d, k=n, tm=d, tn=d, tk=2048, ta=True, name=f"fox_out_dw_l{i}")
            dq, dcc, dk, dv, dck = _flash_bwd(sv["qa"], sv["ka"], sv["v"], do, sv["o"],
                                              _row_layout(sv["lse"], _tile(seq, ATTN_TILE)), bsz, seq,
                                              name=f"fox_attn_bwd_l{i}")
            dqkv, dfl, dqg, dkg, dbf = carried(
                _fox_prep_bwd, (sv["qkv"], dq, dk, dv, _from_col_layout(dcc), _from_col_layout(dck), sv["fl"], sv["bf"],
                                sv["qg"], sv["kg"], seq), name=f"fox_prep_bwd_l{i}")
            dw_qkv = _mm(sv["h"], dqkv, m=d, n=3 * d, k=n, tm=d, tn=d, tk=2048, ta=True, name=f"fox_qkv_dw_l{i}")
            dw_f = _mm(sv["h"], dfl, m=d, n=LANES, k=n, tm=d, tk=2048, ta=True, name=f"fox_fl_dw_l{i}")
            dh = _mm(dqkv, sv["w_qkv"], m=n, n=d, k=3 * d, tn=d, tk=d, tb=True, name=f"fox_qkv_dx_l{i}")
            dh = _mm(dfl, sv["w_f"], m=n, n=d, k=LANES, tn=d, tb=True, res=dh, name=f"fox_fl_dx_l{i}")
            dcur, g_norm_mix[i] = _rms_bwd(sv["x_in"], dh, vec(norm_mix[i]), dmid, name=f"rms_mix_bwd_l{i}")
            g_bf = dbf[:, :nh]
            g_qg = dqg.reshape(nh, HEAD_DIM).sum(axis=0, keepdims=True)
            g_kg = dkg.reshape(nh, HEAD_DIM).sum(axis=0, keepdims=True)
            dw_in_full = jnp.concatenate([dw_qkv, dw_f[:, :nh]], axis=1)
            wshard = dw_in_full.shape[1] // N_CHIPS
            mix_grads = [dw_in_full.reshape(d, N_CHIPS, wshard).transpose(1, 0, 2),
                         dw_o.reshape(N_CHIPS, d // N_CHIPS, d)]
        wire = [BF16] * len(mix_grads)
        if i == 0:
            sm_parts = [jnp.concatenate(part_small["b_in"]), jnp.stack(part_small["dw"]),
                        jnp.concatenate(part_small["dw_b"]), jnp.concatenate(part_small["ln_g"]),
                        jnp.concatenate(part_small["ln_b"]), jnp.concatenate(part_small["b_out"]),
                        part_small["pool_b"].reshape(n_pool, ng, gd), jnp.stack(part_small["ffn_dw"])]
            mix_grads.append(_pack([_to_shards(p_) for p_ in sm_parts], small_rows))
            wire.append(F32)

        def mix_sink(red, i=i, kind=kind, j=j):
            if kind == 0:
                g_conv_in[j], g_conv_out[j] = red[0], red[1]
            elif kind == 1:
                done["pool_w"] = red[0]
            else:
                done["fox_in"], done["fox_o"] = red[0], red[1]
            if i == 0:
                done["small"] = red[-1].reshape(-1)

        pending.append((_rs_begin(mix_grads, wire, c_idx, tag=f"mix_l{i}"), f"mix_l{i}", mix_sink))

    groups, comm = take_pending()
    finish_groups(groups, _run_comm(comm, name="rs_chips_tail"))
    g_pool_w, g_fox_in, g_fox_o, g_small_flat = done["pool_w"], done["fox_in"], done["fox_o"], done["small"]

    grad_x = dcur.reshape(bsz, seq, d)

    rep_parts = [jnp.concatenate(g_norm_mix), jnp.concatenate(g_norm_ffn), g_pool_scale, g_bf, g_qg, g_kg,
                 jnp.concatenate(g_ffn_dw_b)]
    rep_shapes = [norm_mix.shape, norm_ffn.shape, pool_scale.shape, fox_b_f.shape, fox_q_gain.shape,
                  fox_k_gain.shape, ffn_dw_b.shape]
    rep_rows = _pack_rows([math.prod(s) for s in rep_shapes], 8)
    rep = _all_reduce_small(_pack([p_.reshape(1, -1) for p_ in rep_parts], rep_rows)[0], name="all_reduce_small")
    g_rep = _unpack(rep.reshape(-1), rep_shapes)
    g_sm = _unpack(g_small_flat, small_shapes)

    grads = {
        "norm_mix": g_rep[0], "norm_ffn": g_rep[1],
        "conv_w_in": jnp.stack(g_conv_in), "conv_b_in": g_sm[0], "conv_dw": g_sm[1], "conv_dw_b": g_sm[2],
        "conv_ln_g": g_sm[3], "conv_ln_b": g_sm[4], "conv_w_out": jnp.stack(g_conv_out), "conv_b_out": g_sm[5],
        "pool_w": g_pool_w.reshape(pool_w.shape), "pool_b": g_sm[6], "pool_scale": g_rep[2],
        "fox_w_in": g_fox_in.reshape(fox_w_in.shape), "fox_b_f": g_rep[3], "fox_q_gain": g_rep[4],
        "fox_k_gain": g_rep[5], "fox_w_o": g_fox_o.reshape(fox_w_o.shape),
        "ffn_w_up": jnp.stack(g_up), "ffn_dw": g_sm[7], "ffn_dw_b": g_rep[6], "ffn_w_down": jnp.stack(g_down),
    }
    weights = dict(norm_mix=norm_mix, norm_ffn=norm_ffn, conv_w_in=conv_w_in, conv_b_in=conv_b_in, conv_dw=conv_dw,
                   conv_dw_b=conv_dw_b, conv_ln_g=conv_ln_g, conv_ln_b=conv_ln_b, conv_w_out=conv_w_out,
                   conv_b_out=conv_b_out, pool_w=pool_w, pool_b=pool_b, pool_scale=pool_scale, fox_w_in=fox_w_in,
                   fox_b_f=fox_b_f, fox_q_gain=fox_q_gain, fox_k_gain=fox_k_gain, fox_w_o=fox_w_o, ffn_w_up=ffn_w_up,
                   ffn_dw=ffn_dw, ffn_dw_b=ffn_dw_b, ffn_w_down=ffn_w_down)
    m_in = dict(norm_mix=m_norm_mix, norm_ffn=m_norm_ffn, conv_w_in=m_conv_w_in, conv_b_in=m_conv_b_in,
                conv_dw=m_conv_dw, conv_dw_b=m_conv_dw_b, conv_ln_g=m_conv_ln_g, conv_ln_b=m_conv_ln_b,
                conv_w_out=m_conv_w_out, conv_b_out=m_conv_b_out, pool_w=m_pool_w, pool_b=m_pool_b,
                pool_scale=m_pool_scale, fox_w_in=m_fox_w_in, fox_b_f=m_fox_b_f, fox_q_gain=m_fox_q_gain,
                fox_k_gain=m_fox_k_gain, fox_w_o=m_fox_w_o, ffn_w_up=m_ffn_w_up, ffn_dw=m_ffn_dw,
                ffn_dw_b=m_ffn_dw_b, ffn_w_down=m_ffn_w_down)
    v_in = dict(norm_mix=v_norm_mix, norm_ffn=v_norm_ffn, conv_w_in=v_conv_w_in, conv_b_in=v_conv_b_in,
                conv_dw=v_conv_dw, conv_dw_b=v_conv_dw_b, conv_ln_g=v_conv_ln_g, conv_ln_b=v_conv_ln_b,
                conv_w_out=v_conv_w_out, conv_b_out=v_conv_b_out, pool_w=v_pool_w, pool_b=v_pool_b,
                pool_scale=v_pool_scale, fox_w_in=v_fox_w_in, fox_b_f=v_fox_b_f, fox_q_gain=v_fox_q_gain,
                fox_k_gain=v_fox_k_gain, fox_w_o=v_fox_w_o, ffn_w_up=v_ffn_w_up, ffn_dw=v_ffn_dw,
                ffn_dw_b=v_ffn_dw_b, ffn_w_down=v_ffn_w_down)
    names = list(weights)
    g_out, d_out, m_out, v_out = [], [], [], []
    for nm in names:
        g_, dl_, m_, v_ = _adamw_nd(weights[nm], grads[nm].reshape(weights[nm].shape), m_in[nm], v_in[nm],
                                    name=f"adamw_{nm}")
        g_out.append(g_)
        d_out.append(dl_)
        m_out.append(m_)
        v_out.append(v_)
    return (loss, grad_x, *g_out, *d_out, *m_out, *v_out)
```

```python
import math

import jax
import jax.numpy as jnp
import numpy as np
from jax import lax
from jax.experimental import pallas as pl
from jax.experimental.pallas import tpu as pltpu

F32 = jnp.float32
BF16 = jnp.bfloat16
HI = lax.Precision.HIGHEST
MESH = pl.DeviceIdType.MESH
ANY = pl.BlockSpec(memory_space=pl.ANY)

EPS = 1e-6
HEAD_DIM = 64
LANES = 128
POOL_WINDOWS = (2, 4, 8, 16)
CONV_WIDTH = 31
CONV_HALO = 32
FFN_HALO = 8
FFN_ROWS, FFN_COLS = 256, 1408
POOL_HALO = 16
N_CHIPS = 4
NEG = -1e30

ADAM_LR = 0.001
ADAM_B1 = 0.9
ADAM_B2 = 0.999
ADAM_EPS = 1e-08
ADAM_WD = 0.01
ADAM_STEP = 10

V7X_VMEM_LIMIT_BYTES = 56 * 1024 * 1024


def _cp(*sem):
    return pltpu.CompilerParams(dimension_semantics=sem or None, vmem_limit_bytes=V7X_VMEM_LIMIT_BYTES)


def _tile(n, pref):
    t = min(n, pref)
    assert n % t == 0, (n, pref)
    return t


def _sig(v):
    return jax.nn.sigmoid(v)


def _roll(v, shift):
    n = v.shape[0]
    shift = shift % n
    return v if shift == 0 else pltpu.roll(v, shift, 0)


SUBLANES = 8


CONV_ROWS = 64


def _conv_taps(rot_ref, w_ref, out_ref, tm, start_of):
    d = out_ref.shape[1]
    for lc in range(d // LANES):
        ls = slice(lc * LANES, (lc + 1) * LANES)

        for r0 in range(0, tm, CONV_ROWS):
            acc = None
            for sh in range(CONV_WIDTH):
                kk = CONV_WIDTH - 1 - sh
                lo = r0 + start_of(sh)
                term = w_ref[kk:kk + 1, ls] * rot_ref[sh % SUBLANES, lo:lo + CONV_ROWS, ls]
                acc = term if acc is None else acc + term
            out_ref[r0:r0 + CONV_ROWS, ls] = acc


def _tap_grads(rotz_ref, rotd_ref, dw_ref, tm, halo):
    d = dw_ref.shape[1]
    for lc in range(d // LANES):
        ls = slice(lc * LANES, (lc + 1) * LANES)

        acc = [None] * CONV_WIDTH
        for r0 in range(0, tm, CONV_ROWS):
            duc = rotd_ref[0, r0:r0 + CONV_ROWS, ls]
            for sh in range(CONV_WIDTH):
                lo = r0 + halo - (sh // SUBLANES) * SUBLANES
                prod = duc * rotz_ref[sh % SUBLANES, lo:lo + CONV_ROWS, ls]
                part = prod.reshape(CONV_ROWS // SUBLANES, SUBLANES, LANES).sum(axis=0)
                acc[sh] = part if acc[sh] is None else acc[sh] + part
        for sh in range(CONV_WIDTH):
            kk = CONV_WIDTH - 1 - sh
            dw_ref[kk:kk + 1, ls] += jnp.sum(acc[sh], axis=0, keepdims=True)


class _Comm:
    def __init__(self, ins, out_shapes, n_sems, start, wait):
        self.ins, self.out_shapes, self.n_sems, self.start, self.wait = list(ins), list(out_shapes), n_sems, start, wait


def _hosted(body, comm, *, grid, in_specs, out_specs, out_shape, scratch_shapes, sem, name, ins):
    if comm is None:
        return pl.pallas_call(body, grid=grid, in_specs=in_specs, out_specs=out_specs, out_shape=out_shape,
                              scratch_shapes=scratch_shapes, name=name, compiler_params=_cp(*sem))(*ins)
    n_in, n_out, n_scr = len(in_specs), len(out_specs), len(scratch_shapes)
    nci, nco = len(comm.ins), len(comm.out_shapes)

    def wrapped(*refs):
        pos = [0]

        def take(cnt):
            pos[0] += cnt
            return refs[pos[0] - cnt:pos[0]]

        r_in, c_in, r_out, c_out, r_scr = take(n_in), take(nci), take(n_out), take(nco), take(n_scr)
        send_sems, recv_sems = take(2)
        ids = [pl.program_id(ax) for ax in range(len(grid))]
        first, last = ids[0] == 0, ids[0] == grid[0] - 1
        for ax in range(1, len(grid)):
            first = jnp.logical_and(first, ids[ax] == 0)
            last = jnp.logical_and(last, ids[ax] == grid[ax] - 1)

        @pl.when(first)
        def _():
            comm.start(c_in, c_out, send_sems, recv_sems)

        body(*r_in, *r_out, *r_scr)

        @pl.when(last)
        def _():
            comm.wait(c_in, c_out, send_sems, recv_sems)

    outs = pl.pallas_call(
        wrapped, grid=grid, in_specs=list(in_specs) + [ANY] * nci, out_specs=list(out_specs) + [ANY] * nco,
        out_shape=list(out_shape) + comm.out_shapes,
        scratch_shapes=list(scratch_shapes) + [pltpu.SemaphoreType.DMA((comm.n_sems,))] * 2, name=name,
        compiler_params=pltpu.CompilerParams(dimension_semantics=sem, vmem_limit_bytes=V7X_VMEM_LIMIT_BYTES,
                                             has_side_effects=True),
    )(*ins, *comm.ins)
    return list(outs[:n_out]), list(outs[n_out:])


def _run_comm(comm, name):
    def body(*refs):
        nci, nco = len(comm.ins), len(comm.out_shapes)
        c_in, c_out, send_sems, recv_sems = refs[:nci], refs[nci:nci + nco], refs[-2], refs[-1]
        comm.start(c_in, c_out, send_sems, recv_sems)
        comm.wait(c_in, c_out, send_sems, recv_sems)

    return pl.pallas_call(
        body, in_specs=[ANY] * len(comm.ins), out_specs=[ANY] * len(comm.out_shapes), out_shape=comm.out_shapes,
        scratch_shapes=[pltpu.SemaphoreType.DMA((comm.n_sems,))] * 2, name=name,
        compiler_params=pltpu.CompilerParams(has_side_effects=True),
    )(*comm.ins)


def _mm(a, b, *, m, n, k, name, tm=1024, tn=1024, tk=512, ta=False, tb=False, b_stk=None, b_s0=0,
        o_stk=None, o_s0=0, o_slots=None, o_buf=None, bias=None, res=None, out_dtype=F32):
    tm, tn, tk = _tile(m, tm), _tile(n, tn), _tile(k, tk)
    gi, gj, gk = m // tm, n // tn, k // tk
    a_spec = pl.BlockSpec((tk, tm), lambda j, i, kk: (kk, i)) if ta else pl.BlockSpec((tm, tk), lambda j, i, kk: (i, kk))
    if b_stk is None:
        b_spec = pl.BlockSpec((tn, tk), lambda j, i, kk: (j, kk)) if tb else pl.BlockSpec((tk, tn), lambda j, i, kk: (kk, j))
    elif tb:
        assert b_stk % tk == 0
        per = b_stk // tk
        b_spec = pl.BlockSpec((None, tn, tk), lambda j, i, kk: (b_s0 + kk // per, j, kk % per))
    else:
        assert b_stk % tn == 0
        per = b_stk // tn
        b_spec = pl.BlockSpec((None, tk, tn), lambda j, i, kk: (b_s0 + j // per, kk, j % per))
    ins, in_specs = [a, b], [a_spec, b_spec]
    if bias is not None:
        ins.append(bias)
        in_specs.append(pl.BlockSpec((1, tn), lambda j, i, kk: (0, j)))
    if res is not None:
        ins.append(res)
        in_specs.append(pl.BlockSpec((tm, tn), lambda j, i, kk: (i, j)))
    aliases = {}
    if o_stk is None:
        out_shape = jax.ShapeDtypeStruct((m, n), out_dtype)
        o_spec = pl.BlockSpec((tm, tn), lambda j, i, kk: (i, j))
    else:
        assert o_stk % tn == 0
        pero = o_stk // tn
        out_shape = jax.ShapeDtypeStruct((o_slots, m, o_stk), out_dtype)
        o_spec = pl.BlockSpec((None, tm, tn), lambda j, i, kk: (o_s0 + j // pero, i, j % pero))
        if o_buf is not None:
            aliases = {len(ins): 0}
            ins.append(o_buf)
            in_specs.append(ANY)
    has_bias, has_res, has_buf = bias is not None, res is not None, o_buf is not None
    dn = (((0 if ta else 1,), (1 if tb else 0,)), ((), ()))

    def body(*refs):
        a_ref, b_ref = refs[0], refs[1]
        pos = 2
        bias_ref = refs[pos] if has_bias else None
        pos += has_bias
        res_ref = refs[pos] if has_res else None
        pos += has_res + has_buf
        o_ref = refs[pos]
        p = lax.dot_general(a_ref[...].astype(BF16), b_ref[...].astype(BF16), dn, preferred_element_type=F32)

        def finish(acc):
            if has_bias:
                acc = acc + bias_ref[...]
            if has_res:
                acc = acc + res_ref[...]
            o_ref[...] = acc.astype(o_ref.dtype)

        if gk == 1:
            finish(p)
        else:
            acc_ref = refs[pos + 1]
            kk = pl.program_id(2)

            @pl.when(kk == 0)
            def _():
                acc_ref[...] = p

            @pl.when(kk > 0)
            def _():
                acc_ref[...] += p

            @pl.when(kk == gk - 1)
            def _():
                finish(acc_ref[...])

    return pl.pallas_call(
        body, grid=(gj, gi, gk), in_specs=in_specs, out_specs=o_spec, out_shape=out_shape,
        scratch_shapes=[pltpu.VMEM((tm, tn), F32)] if gk > 1 else [],
        input_output_aliases=aliases, name=name,
        compiler_params=_cp("parallel", "parallel", "arbitrary"),
    )(*ins)


def _rms_fwd(x, g, name):
    n, d = x.shape
    tm = _tile(n, 512)

    def body(x_ref, g_ref, h_ref):
        xv = x_ref[...]
        r = lax.rsqrt(jnp.mean(xv * xv, axis=-1, keepdims=True) + EPS)
        h_ref[...] = (xv * r * g_ref[...]).astype(h_ref.dtype)

    return pl.pallas_call(
        body, grid=(n // tm,),
        in_specs=[pl.BlockSpec((tm, d), lambda i: (i, 0)), pl.BlockSpec((1, d), lambda i: (0, 0))],
        out_specs=pl.BlockSpec((tm, d), lambda i: (i, 0)),
        out_shape=jax.ShapeDtypeStruct((n, d), BF16), name=name, compiler_params=_cp("arbitrary"),
    )(x, g)


def _rms_bwd(x, dh, g, dres, name, colsum=False):
    n, d = x.shape
    tm = _tile(n, 512)

    def body(x_ref, dh_ref, g_ref, dres_ref, dx_ref, dg_ref, *rest):
        i = pl.program_id(0)
        xv, dhv = x_ref[...], dh_ref[...]
        r = lax.rsqrt(jnp.mean(xv * xv, axis=-1, keepdims=True) + EPS)
        xn = xv * r
        dxn = dhv * g_ref[...]
        dx_ref[...] = dres_ref[...] + r * (dxn - xn * jnp.mean(dxn * xn, axis=-1, keepdims=True))
        dg = jnp.sum(dhv * xn, axis=0, keepdims=True)

        @pl.when(i == 0)
        def _():
            dg_ref[...] = jnp.zeros_like(dg_ref)
            if colsum:
                rest[0][...] = jnp.zeros_like(rest[0])

        dg_ref[...] += dg
        if colsum:
            rest[0][...] += jnp.sum(dres_ref[...], axis=0, keepdims=True)

    row = pl.BlockSpec((tm, d), lambda i: (i, 0))
    vec = pl.BlockSpec((1, d), lambda i: (0, 0))
    out_shape = [jax.ShapeDtypeStruct((n, d), F32), jax.ShapeDtypeStruct((1, d), F32)]
    out_specs = [row, vec]
    if colsum:
        out_shape.append(jax.ShapeDtypeStruct((1, d), F32))
        out_specs.append(vec)
    return pl.pallas_call(
        body, grid=(n // tm,), in_specs=[row, row, vec, row], out_specs=out_specs, out_shape=out_shape,
        name=name, compiler_params=_cp("arbitrary"),
    )(x, dh, g, dres)


def _loss(y, tgt, name):
    n, d = y.shape
    tm = _tile(n, 512)

    def body(y_ref, t_ref, dy_ref, l_ref):
        i = pl.program_id(0)
        e = y_ref[...] - t_ref[...]
        dy_ref[...] = e / d
        part = 0.5 * jnp.sum(jnp.mean(e * e, axis=-1, keepdims=True), axis=0, keepdims=True)

        @pl.when(i == 0)
        def _():
            l_ref[...] = jnp.zeros_like(l_ref)

        l_ref[...] += part

    row = pl.BlockSpec((tm, d), lambda i: (i, 0))
    return pl.pallas_call(
        body, grid=(n // tm,), in_specs=[row, row],
        out_specs=[row, pl.BlockSpec((1, 1), lambda i: (0, 0))],
        out_shape=[jax.ShapeDtypeStruct((n, d), F32), jax.ShapeDtypeStruct((1, 1), F32)],
        name=name, compiler_params=_cp("arbitrary"),
    )(y, tgt)


def _ffn_specs(n, f, tm, tc, seq):
    hb = FFN_HALO
    cur = pl.BlockSpec((tm, tc), lambda j, i: (i, j))
    prev = pl.BlockSpec((hb, tc), lambda j, i: (jnp.maximum(i * (tm // hb) - 1, 0), j))
    nxt = pl.BlockSpec((hb, tc), lambda j, i: (jnp.minimum((i + 1) * (tm // hb), n // hb - 1), j))
    taps = pl.BlockSpec((3, tc), lambda j, i: (0, j))
    vec = pl.BlockSpec((1, tc), lambda j, i: (0, j))
    return cur, prev, nxt, taps, vec


def _ffn_glu_fwd(uv, ug, wv, wg, bv, bg, seq, name, comm=None):
    n, f = uv.shape
    tm, tc = _tile(seq, FFN_ROWS), _tile(f, FFN_COLS)
    tps = seq // tm
    cur, prev, _, taps, vec = _ffn_specs(n, f, tm, tc, seq)

    def body(uvp, uvc, ugp, ugc, wv_ref, wg_ref, bv_ref, bg_ref, a_ref):
        first = (pl.program_id(1) % tps) == 0

        def conv(p_ref, c_ref, w_ref, b_ref):
            xs = jnp.concatenate([jnp.where(first, 0.0, p_ref[...]), c_ref[...]], axis=0)
            w = w_ref[...]
            y = w[2:3] * xs + w[1:2] * _roll(xs, 1) + w[0:1] * _roll(xs, 2)
            return y[FFN_HALO:] + b_ref[...]

        val = conv(uvp, uvc, wv_ref, bv_ref)
        gate = conv(ugp, ugc, wg_ref, bg_ref)
        a_ref[...] = (gate * _sig(gate) * val).astype(a_ref.dtype)

    return _hosted(
        body, comm, grid=(f // tc, n // tm), in_specs=[prev, cur, prev, cur, taps, taps, vec, vec],
        out_specs=[cur], out_shape=[jax.ShapeDtypeStruct((n, f), BF16)], scratch_shapes=[],
        sem=("parallel", "arbitrary"), name=name, ins=(uv, uv, ug, ug, wv, wg, bv, bg))


def _ffn_glu_bwd(uv, ug, da, wv, wg, bv, bg, seq, name, comm=None):
    n, f = uv.shape
    tm, tc = _tile(seq, FFN_ROWS), _tile(f, FFN_COLS)
    tps = seq // tm
    hb = FFN_HALO
    ext = tm + hb
    cur, prev, nxt, taps, vec = _ffn_specs(n, f, tm, tc, seq)

    def body(uvp, uvc, uvn, ugp, ugc, ugn, da_c, da_n, wv_ref, wg_ref, bv_ref, bg_ref,
             duv_ref, dug_ref, dwv_ref, dwg_ref, dbv_ref, dbg_ref):
        i = pl.program_id(1)
        first = (i % tps) == 0
        last = (i % tps) == tps - 1
        da_e = jnp.concatenate([da_c[...], jnp.where(last, 0.0, da_n[...])], axis=0)

        def taps3(p_ref, c_ref, n_ref):
            xs = jnp.concatenate([jnp.where(first, 0.0, p_ref[...]), c_ref[...], n_ref[...]], axis=0)
            return xs, _roll(xs, 1), _roll(xs, 2)

        xv, xg = taps3(uvp, uvc, uvn), taps3(ugp, ugc, ugn)
        wv_, wg_ = wv_ref[...], wg_ref[...]

        def conv(xs, w, b_ref):
            return (w[2:3] * xs[0] + w[1:2] * xs[1] + w[0:1] * xs[2])[hb:] + b_ref[...]

        val, gate = conv(xv, wv_, bv_ref), conv(xg, wg_, bg_ref)
        sg = _sig(gate)
        dval = da_e * (gate * sg)
        dgate = da_e * val * (sg * (1.0 + gate * (1.0 - sg)))

        def conv_t(dv, w):
            return (w[2:3] * dv + w[1:2] * _roll(dv, ext - 1) + w[0:1] * _roll(dv, ext - 2))[:tm]

        duv_ref[...] = conv_t(dval, wv_).astype(duv_ref.dtype)
        dug_ref[...] = conv_t(dgate, wg_).astype(dug_ref.dtype)

        def tap_grads(d_own, xs):
            return jnp.concatenate(
                [jnp.sum(d_own * xs[2 - kk][hb:hb + tm], axis=0, keepdims=True) for kk in range(3)], axis=0)

        dv_own, dg_own = dval[:tm], dgate[:tm]

        @pl.when(i == 0)
        def _():
            for r in (dwv_ref, dwg_ref, dbv_ref, dbg_ref):
                r[...] = jnp.zeros_like(r)

        dwv_ref[...] += tap_grads(dv_own, xv)
        dwg_ref[...] += tap_grads(dg_own, xg)
        dbv_ref[...] += jnp.sum(dv_own, axis=0, keepdims=True)
        dbg_ref[...] += jnp.sum(dg_own, axis=0, keepdims=True)

    return _hosted(
        body, comm, grid=(f // tc, n // tm),
        in_specs=[prev, cur, nxt, prev, cur, nxt, cur, nxt, taps, taps, vec, vec],
        out_specs=[cur, cur, taps, taps, vec, vec],
        out_shape=[jax.ShapeDtypeStruct((n, f), BF16), jax.ShapeDtypeStruct((n, f), BF16),
                   jax.ShapeDtypeStruct((3, f), F32), jax.ShapeDtypeStruct((3, f), F32),
                   jax.ShapeDtypeStruct((1, f), F32), jax.ShapeDtypeStruct((1, f), F32)],
        scratch_shapes=[], sem=("parallel", "arbitrary"), name=name,
        ins=(uv, uv, uv, ug, ug, ug, da, da, wv, wg, bv, bg))


def _conf_specs(n, d, tm):
    hb = CONV_HALO
    cur = pl.BlockSpec((tm, d), lambda i: (i, 0))
    prev = pl.BlockSpec((hb, d), lambda i: (jnp.maximum(i * (tm // hb) - 1, 0), 0))
    nxt = pl.BlockSpec((hb, d), lambda i: (jnp.minimum((i + 1) * (tm // hb), n // hb - 1), 0))
    taps = pl.BlockSpec((CONV_HALO, d), lambda i: (0, 0))
    vec = pl.BlockSpec((1, d), lambda i: (0, 0))
    return cur, prev, nxt, taps, vec


def _conf_fwd(pa, pg, w, wb, lng, lnb, seq, name):
    n, d = pa.shape
    tm = _tile(seq, 256)
    tps = seq // tm
    hb = CONV_HALO
    cur, prev, _, taps, vec = _conf_specs(n, d, tm)

    def body(pap, pac, pgp, pgc, w_ref, wb_ref, lng_ref, lnb_ref, u_ref, s_ref, rot_ref):
        first = (pl.program_id(0) % tps) == 0
        a = jnp.concatenate([jnp.where(first, 0.0, pap[...]), pac[...]], axis=0)
        g = jnp.concatenate([jnp.where(first, 0.0, pgp[...]), pgc[...]], axis=0)
        z = a * _sig(g)
        for b in range(SUBLANES):
            rot_ref[b] = _roll(z, b)
        _conv_taps(rot_ref, w_ref, u_ref, tm, lambda sh: hb - (sh // SUBLANES) * SUBLANES)
        u = u_ref[...] + wb_ref[...]
        mu = jnp.mean(u, axis=-1, keepdims=True)
        uc = u - mu
        var = jnp.mean(uc * uc, axis=-1, keepdims=True)
        ul = uc * lax.rsqrt(var + EPS) * lng_ref[...] + lnb_ref[...]
        u_ref[...] = u
        s_ref[...] = (ul * _sig(ul)).astype(s_ref.dtype)

    return pl.pallas_call(
        body, grid=(n // tm,), in_specs=[prev, cur, prev, cur, taps, vec, vec, vec],
        out_specs=[cur, cur],
        out_shape=[jax.ShapeDtypeStruct((n, d), F32), jax.ShapeDtypeStruct((n, d), BF16)],
        scratch_shapes=[pltpu.VMEM((SUBLANES, tm + hb, d), F32)],
        name=name, compiler_params=_cp("arbitrary"),
    )(pa, pa, pg, pg, w, wb, lng, lnb)


def _conf_bwd(u, ds, pa, pg, w, lng, lnb, seq, name, comm=None):
    n, d = u.shape
    tm = _tile(seq, 256)
    tps = seq // tm
    hb = CONV_HALO
    ext = tm + hb
    cur, prev, nxt, taps, vec = _conf_specs(n, d, tm)

    def body(uc_ref, un_ref, dsc_ref, dsn_ref, pap, pac, pgp, pgc, w_ref, lng_ref, lnb_ref,
             dpa_ref, dpg_ref, dw_ref, dwb_ref, dlng_ref, dlnb_ref, dba_ref, dbg_ref, rotz_ref, rotd_ref, dz_ref):
        i = pl.program_id(0)
        first = (i % tps) == 0
        last = (i % tps) == tps - 1

        @pl.when(i == 0)
        def _():
            for r in (dw_ref, dwb_ref, dlng_ref, dlnb_ref, dba_ref, dbg_ref):
                r[...] = jnp.zeros_like(r)

        ue = jnp.concatenate([uc_ref[...], un_ref[...]], axis=0)
        dse = jnp.concatenate([dsc_ref[...], jnp.where(last, 0.0, dsn_ref[...])], axis=0)
        mu = jnp.mean(ue, axis=-1, keepdims=True)
        cen = ue - mu
        r = lax.rsqrt(jnp.mean(cen * cen, axis=-1, keepdims=True) + EPS)
        xn = cen * r
        ul = xn * lng_ref[...] + lnb_ref[...]
        sg = _sig(ul)
        dul = dse * (sg * (1.0 + ul * (1.0 - sg)))
        dun = dul * lng_ref[...]
        du = r * (dun - jnp.mean(dun, axis=-1, keepdims=True) - xn * jnp.mean(dun * xn, axis=-1, keepdims=True))
        dlng_ref[...] += jnp.sum((dul * xn)[:tm], axis=0, keepdims=True)
        dlnb_ref[...] += jnp.sum(dul[:tm], axis=0, keepdims=True)
        dwb_ref[...] += jnp.sum(du[:tm], axis=0, keepdims=True)
        for b in range(SUBLANES):
            rotd_ref[b] = _roll(du, ext - b)
        _conv_taps(rotd_ref, w_ref, dz_ref, tm, lambda sh: (sh // SUBLANES) * SUBLANES)
        dz = dz_ref[...]

        a = jnp.concatenate([jnp.where(first, 0.0, pap[...]), pac[...]], axis=0)
        g = jnp.concatenate([jnp.where(first, 0.0, pgp[...]), pgc[...]], axis=0)
        sgg = _sig(g)
        z = a * sgg
        for b in range(SUBLANES):
            rotz_ref[b] = _roll(z, b)
        _tap_grads(rotz_ref, rotd_ref, dw_ref, tm, hb)

        a_c, sg_c = a[hb:], sgg[hb:]
        da = dz * sg_c
        dg = dz * a_c * sg_c * (1.0 - sg_c)
        dpa_ref[...] = da.astype(dpa_ref.dtype)
        dpg_ref[...] = dg.astype(dpg_ref.dtype)
        dba_ref[...] += jnp.sum(da, axis=0, keepdims=True)
        dbg_ref[...] += jnp.sum(dg, axis=0, keepdims=True)

    vshape = jax.ShapeDtypeStruct((1, d), F32)
    return _hosted(
        body, comm, grid=(n // tm,),
        in_specs=[cur, nxt, cur, nxt, prev, cur, prev, cur, taps, vec, vec],
        out_specs=[cur, cur, taps, vec, vec, vec, vec, vec],
        out_shape=[jax.ShapeDtypeStruct((n, d), BF16), jax.ShapeDtypeStruct((n, d), BF16),
                   jax.ShapeDtypeStruct((CONV_HALO, d), F32), vshape, vshape, vshape, vshape, vshape],
        scratch_shapes=[pltpu.VMEM((SUBLANES, ext, d), F32), pltpu.VMEM((SUBLANES, ext, d), F32),
                        pltpu.VMEM((tm, d), F32)],
        sem=("arbitrary",), name=name, ins=(u, u, ds, ds, pa, pa, pg, pg, w, lng, lnb))


def _pool_specs(n, d, tm, gd):
    hb = POOL_HALO
    cur = pl.BlockSpec((tm, d), lambda i: (i, 0))
    prev = pl.BlockSpec((hb, d), lambda i: (jnp.maximum(i * (tm // hb) - 1, 0), 0))
    nxt = pl.BlockSpec((hb, d), lambda i: (jnp.minimum((i + 1) * (tm // hb), n // hb - 1), 0))
    wsp = pl.BlockSpec((len(POOL_WINDOWS), gd, gd), lambda i: (0, 0, 0))
    vec = pl.BlockSpec((1, d), lambda i: (0, 0))
    return cur, prev, nxt, wsp, vec


def _pool_fwd(x, g, w, b, sc, seq, name):
    n, d = x.shape
    gd = d // len(POOL_WINDOWS)
    tm = _tile(seq, 256)
    tps = seq // tm
    hb = POOL_HALO
    cur, prev, _, wsp, vec = _pool_specs(n, d, tm, gd)

    def body(xp, xc, g_ref, w_ref, b_ref, sc_ref, x1_ref, p_ref):
        i = pl.program_id(0)
        first = (i % tps) == 0
        xe = jnp.concatenate([jnp.where(first, 0.0, xp[...]), xc[...]], axis=0)
        r = lax.rsqrt(jnp.mean(xe * xe, axis=-1, keepdims=True) + EPS)
        h = xe * r * g_ref[...]
        t = ((i % tps) * tm + lax.broadcasted_iota(jnp.int32, (tm, 1), 0) + 1).astype(F32)
        ys = []
        for gi, win in enumerate(POOL_WINDOWS):
            hg = h[:, gi * gd:(gi + 1) * gd]
            s, sh = hg, 1
            while sh < win:
                s = s + _roll(s, sh)
                sh *= 2
            p = (s[hb:] / jnp.minimum(t, float(win)) - hg[hb:]).astype(BF16)
            p_ref[:, gi * gd:(gi + 1) * gd] = p
            ys.append(jnp.dot(p, w_ref[gi], preferred_element_type=F32))
        y = jnp.concatenate(ys, axis=1) + b_ref[...]
        x1_ref[...] = xc[...] + y * sc_ref[...]

    return pl.pallas_call(
        body, grid=(n // tm,), in_specs=[prev, cur, vec, wsp, vec, vec], out_specs=[cur, cur],
        out_shape=[jax.ShapeDtypeStruct((n, d), F32), jax.ShapeDtypeStruct((n, d), BF16)],
        name=name, compiler_params=_cp("arbitrary"),
    )(x, x, g, w, b, sc)


def _pool_bwd(dx1, x, p, g, w, b, sc, seq, name):
    n, d = x.shape
    ng = len(POOL_WINDOWS)
    gd = d // ng
    tm = _tile(seq, 256)
    tps = seq // tm
    hb = POOL_HALO
    ext = tm + hb
    cur, _, nxt, wsp, vec = _pool_specs(n, d, tm, gd)

    def body(dc_ref, dn_ref, x_ref, p_ref, g_ref, w_ref, b_ref, sc_ref, dx_ref, dg_ref, dw_ref, db_ref, dsc_ref):
        i = pl.program_id(0)
        last = (i % tps) == tps - 1

        @pl.when(i == 0)
        def _():
            for r_ in (dg_ref, dw_ref, db_ref, dsc_ref):
                r_[...] = jnp.zeros_like(r_)

        dxc = dc_ref[...]
        dxe = jnp.concatenate([dxc, jnp.where(last, 0.0, dn_ref[...])], axis=0)
        dyg = dxe * sc_ref[...]
        t = ((i % tps) * tm + lax.broadcasted_iota(jnp.int32, (ext, 1), 0) + 1).astype(F32)
        dhs = []
        for gi, win in enumerate(POOL_WINDOWS):
            sl = slice(gi * gd, (gi + 1) * gd)
            dyb = dyg[:, sl].astype(BF16)
            wg = w_ref[gi]
            dp = lax.dot_general(dyb, wg, (((1,), (1,)), ((), ())), preferred_element_type=F32)
            s, sh = dp / jnp.minimum(t, float(win)), 1
            while sh < win:
                s = s + _roll(s, ext - sh)
                sh *= 2
            dhs.append((s - dp)[:tm])
            pg = p_ref[:, sl]
            dw_ref[gi] += lax.dot_general(pg, dyb[:tm], (((0,), (0,)), ((), ())), preferred_element_type=F32)
            ypre = jnp.dot(pg, wg, preferred_element_type=F32) + b_ref[:, sl]
            dsc_ref[:, sl] += jnp.sum(dxc[:, sl] * ypre, axis=0, keepdims=True)
            db_ref[:, sl] += jnp.sum(dyg[:tm, sl], axis=0, keepdims=True)
        dh = jnp.concatenate(dhs, axis=1)
        xv = x_ref[...]
        r = lax.rsqrt(jnp.mean(xv * xv, axis=-1, keepdims=True) + EPS)
        xn = xv * r
        dxn = dh * g_ref[...]
        dx_ref[...] = dxc + r * (dxn - xn * jnp.mean(dxn * xn, axis=-1, keepdims=True))
        dg_ref[...] += jnp.sum(dh * xn, axis=0, keepdims=True)

    vshape = jax.ShapeDtypeStruct((1, d), F32)
    return pl.pallas_call(
        body, grid=(n // tm,), in_specs=[cur, nxt, cur, cur, vec, wsp, vec, vec],
        out_specs=[cur, vec, wsp, vec, vec],
        out_shape=[jax.ShapeDtypeStruct((n, d), F32), vshape, jax.ShapeDtypeStruct((ng, gd, gd), F32), vshape, vshape],
        name=name, compiler_params=_cp("arbitrary"),
    )(dx1, dx1, x, p, g, w, b, sc)


def _head_maps(d):
    hd = lax.broadcasted_iota(jnp.int32, (d, LANES), 0) // HEAD_DIM
    col = lax.broadcasted_iota(jnp.int32, (d, LANES), 1)
    gm = (hd == col).astype(BF16)
    hd_t = lax.broadcasted_iota(jnp.int32, (LANES, d), 1) // HEAD_DIM
    row = lax.broadcasted_iota(jnp.int32, (LANES, d), 0)
    gt = (hd_t == row).astype(BF16)
    return gm, gt


def _dot_split(v, onehot):
    hi = v.astype(BF16)
    lo = (v - hi.astype(F32)).astype(BF16)
    return jnp.dot(hi, onehot, preferred_element_type=F32) + jnp.dot(lo, onehot, preferred_element_type=F32)


def _bias_placement(nh):
    pq = np.zeros((3 * LANES, nh * HEAD_DIM), np.float32)
    pk = np.zeros((3 * LANES, nh * HEAD_DIM), np.float32)
    oq = np.zeros((1, nh * HEAD_DIM), np.float32)
    ok = np.zeros((1, nh * HEAD_DIM), np.float32)
    for h in range(nh):
        for piece in range(3):
            pq[piece * LANES + h, h * HEAD_DIM + piece] = 1.0
            pk[piece * LANES + h, h * HEAD_DIM + 3 + piece] = -1.0
            oq[0, h * HEAD_DIM + 3 + piece] = 1.0
            ok[0, h * HEAD_DIM + piece] = 1.0
    return jnp.asarray(pq, BF16), jnp.asarray(pk, BF16), jnp.asarray(oq), jnp.asarray(ok)


def _fox_prep_fwd(qkv, fl, bf, qg, kg, seq, name):
    n, d3 = qkv.shape
    d = d3 // 3
    nh = d // HEAD_DIM
    tm = _tile(seq, 256)
    tps = seq // tm
    scale = 1.0 / math.sqrt(HEAD_DIM)
    pq, pk, oq, ok = _bias_placement(nh)

    def body(qkv_ref, fl_ref, bf_ref, qg_ref, kg_ref, pq_ref, pk_ref, oq_ref, ok_ref, q_ref, k_ref, v_ref, carry):
        first = (pl.program_id(0) % tps) == 0
        gm, gt = _head_maps(d)

        def head_norm(xr, gain):
            r = lax.rsqrt(_dot_split(xr * xr, gm) / HEAD_DIM + EPS)
            return xr * _dot_split(r, gt) * gain

        qs = (head_norm(qkv_ref[:, :d], qg_ref[...]).astype(BF16).astype(F32) * scale).astype(BF16)
        kn = head_norm(qkv_ref[:, d:2 * d], kg_ref[...]).astype(BF16)
        v_ref[...] = qkv_ref[:, 2 * d:].astype(BF16)
        z = fl_ref[...] + bf_ref[...]
        logf = jnp.minimum(z, 0.0) - jnp.log1p(jnp.exp(-jnp.abs(z)))
        tri = (lax.broadcasted_iota(jnp.int32, (tm, tm), 0) >= lax.broadcasted_iota(jnp.int32, (tm, tm), 1)).astype(F32)

        @pl.when(first)
        def _():
            carry[...] = jnp.zeros_like(carry)

        c = jnp.dot(tri, logf, precision=HI, preferred_element_type=F32) + carry[...]
        carry[...] = c[tm - 1:tm, :]
        c1 = c.astype(BF16)
        r1 = c - c1.astype(F32)
        c2 = r1.astype(BF16)
        c3 = (r1 - c2.astype(F32)).astype(BF16)
        pieces = jnp.concatenate([c1, c2, c3], axis=1)
        eq = (jnp.dot(pieces, pq_ref[...], preferred_element_type=F32) + oq_ref[...]).astype(BF16)
        ek = (jnp.dot(pieces, pk_ref[...], preferred_element_type=F32) + ok_ref[...]).astype(BF16)
        for h in range(nh):
            lo, hi = h * HEAD_DIM, (h + 1) * HEAD_DIM
            q_ref[:, 2 * lo:2 * lo + HEAD_DIM] = qs[:, lo:hi]
            q_ref[:, 2 * lo + HEAD_DIM:2 * hi] = eq[:, lo:hi]
            k_ref[:, 2 * lo:2 * lo + HEAD_DIM] = kn[:, lo:hi]
            k_ref[:, 2 * lo + HEAD_DIM:2 * hi] = ek[:, lo:hi]

    row = lambda w: pl.BlockSpec((tm, w), lambda i: (i, 0))
    vec = lambda w: pl.BlockSpec((1, w), lambda i: (0, 0))
    full = lambda a: pl.BlockSpec(a.shape, lambda i: (0, 0))
    return pl.pallas_call(
        body, grid=(n // tm,),
        in_specs=[row(d3), row(LANES), vec(LANES), vec(d), vec(d), full(pq), full(pk), full(oq), full(ok)],
        out_specs=[row(2 * d), row(2 * d), row(d)],
        out_shape=[jax.ShapeDtypeStruct((n, 2 * d), BF16)] * 2 + [jax.ShapeDtypeStruct((n, d), BF16)],
        scratch_shapes=[pltpu.VMEM((1, LANES), F32)], name=name, compiler_params=_cp("arbitrary"),
    )(qkv, fl, bf, qg, kg, pq, pk, oq, ok)


def _fox_prep_bwd(qkv, dq, dk, dv, dc1, dc2, fl, bf, qg, kg, seq, name, comm=None):
    n, d3 = qkv.shape
    d = d3 // 3
    tm = _tile(seq, 256)
    tps = seq // tm
    nt = n // tm

    def body(qkv_ref, dq_ref, dk_ref, dv_ref, dc1_ref, dc2_ref, fl_ref, bf_ref, qg_ref, kg_ref,
             dqkv_ref, dfl_ref, dqg_ref, dkg_ref, dbf_ref, carry):
        i = pl.program_id(0)
        tile = nt - 1 - i
        last = (tile % tps) == tps - 1
        gm, gt = _head_maps(d)

        @pl.when(i == 0)
        def _():
            for r_ in (dqg_ref, dkg_ref, dbf_ref):
                r_[...] = jnp.zeros_like(r_)

        @pl.when(last)
        def _():
            carry[...] = jnp.zeros_like(carry)

        def head_norm_bwd(xr, dy, gain, dgain_ref):
            rf = _dot_split(lax.rsqrt(_dot_split(xr * xr, gm) / HEAD_DIM + EPS), gt)
            xn = xr * rf
            dgain_ref[...] += jnp.sum(dy * xn, axis=0, keepdims=True)
            dyg = dy * gain
            mean = _dot_split(dyg * xn, gm) / HEAD_DIM
            return rf * (dyg - xn * _dot_split(mean, gt))

        dqkv_ref[:, :d] = head_norm_bwd(qkv_ref[:, :d], dq_ref[...], qg_ref[...], dqg_ref).astype(BF16)
        dqkv_ref[:, d:2 * d] = head_norm_bwd(qkv_ref[:, d:2 * d], dk_ref[...], kg_ref[...], dkg_ref).astype(BF16)
        dqkv_ref[:, 2 * d:] = dv_ref[...].astype(BF16)

        dc = dc1_ref[...] + dc2_ref[...]
        tri = (lax.broadcasted_iota(jnp.int32, (tm, tm), 0) <= lax.broadcasted_iota(jnp.int32, (tm, tm), 1)).astype(F32)
        dlog = jnp.dot(tri, dc, precision=HI, preferred_element_type=F32) + carry[...]
        carry[...] = dlog[0:1, :]
        dfl = dlog * (1.0 - _sig(fl_ref[...] + bf_ref[...]))
        dfl_ref[...] = dfl.astype(BF16)
        dbf_ref[...] += jnp.sum(dfl, axis=0, keepdims=True)

    row = lambda w: pl.BlockSpec((tm, w), lambda i: (nt - 1 - i, 0))
    vec = lambda w: pl.BlockSpec((1, w), lambda i: (0, 0))
    return _hosted(
        body, comm, grid=(nt,),
        in_specs=[row(d3), row(d), row(d), row(d), row(LANES), row(LANES), row(LANES), vec(LANES), vec(d), vec(d)],
        out_specs=[row(d3), row(LANES), vec(d), vec(d), vec(LANES)],
        out_shape=[jax.ShapeDtypeStruct((n, d3), BF16), jax.ShapeDtypeStruct((n, LANES), BF16),
                   jax.ShapeDtypeStruct((1, d), F32), jax.ShapeDtypeStruct((1, d), F32),
                   jax.ShapeDtypeStruct((1, LANES), F32)],
        scratch_shapes=[pltpu.VMEM((1, LANES), F32)], sem=("arbitrary",), name=name,
        ins=(qkv, dq, dk, dv, dc1, dc2, fl, bf, qg, kg))


def _attn_specs(bsz, seq, t):
    nb = seq // t
    blk = lambda w: pl.BlockSpec((t, w), lambda b, h, i: (b * nb + i, h))
    full = lambda w: pl.BlockSpec((seq, w), lambda b, h, i: (b, h))
    col = pl.BlockSpec((None, None, t, 2), lambda b, h, i: (b, h, i, 0))
    rows = pl.BlockSpec((None, None, nb, 2, t), lambda b, h, i: (b, h, 0, 0, 0))
    return nb, blk, full, col, rows


_NT = (((1,), (1,)), ((), ()))
ATTN_TILE = 512


def _head_lanes(t, hh):
    lane = lax.broadcasted_iota(jnp.int32, (t, LANES), 1)
    return (lane < HEAD_DIM) if hh == 0 else (lane >= HEAD_DIM)


def _flash_fwd(qa, ka, v, bsz, seq, name):
    n, d = v.shape
    hp = d // LANES
    t = _tile(seq, ATTN_TILE)
    nb, blk, full, col, _ = _attn_specs(bsz, seq, t)

    def body(q_ref, k_ref, v_ref, o_ref, lse_ref):
        i = pl.program_id(2)
        causal = lax.broadcasted_iota(jnp.int32, (t, t), 0) >= lax.broadcasted_iota(jnp.int32, (t, t), 1)

        def block(j, carry, masked):
            rs = pl.ds(pl.multiple_of(j * t, t), t)
            vj = v_ref[rs, :]
            out = []
            for hh in range(2):
                m, l, acc = carry[hh]
                hs = slice(hh * LANES, (hh + 1) * LANES)
                sc = lax.dot_general(q_ref[:, hs], k_ref[rs, hs], _NT, preferred_element_type=F32)
                if masked:
                    sc = jnp.where(causal, sc, NEG)
                mn = jnp.maximum(m, jnp.max(sc, axis=-1, keepdims=True))
                p = jnp.exp(sc - mn)
                al = jnp.exp(m - mn)
                l = al * l + jnp.sum(p, axis=-1, keepdims=True)
                acc = al * acc + jnp.dot(p.astype(BF16), vj, preferred_element_type=F32)
                out.append((mn, l, acc))
            return tuple(out)

        init = tuple((jnp.full((t, 1), NEG, F32), jnp.zeros((t, 1), F32), jnp.zeros((t, LANES), F32))
                     for _ in range(2))
        carry = lax.fori_loop(0, i, lambda j, c: block(j, c, False), init)
        (m0, l0, a0), (m1, l1, a1) = block(i, carry, True)
        o_ref[...] = jnp.where(_head_lanes(t, 0), a0 / l0, a1 / l1)
        lse_ref[:, 0:1] = m0 + jnp.log(l0)
        lse_ref[:, 1:2] = m1 + jnp.log(l1)

    return pl.pallas_call(
        body, grid=(bsz, hp, nb), in_specs=[blk(2 * LANES), full(2 * LANES), full(LANES)],
        out_specs=[blk(LANES), col],
        out_shape=[jax.ShapeDtypeStruct((n, d), F32), jax.ShapeDtypeStruct((bsz, hp, seq, 2), F32)],
        name=name, compiler_params=_cp("parallel", "parallel", "arbitrary"),
    )(qa, ka, v)


def _flash_bwd(qa, ka, v, do, o, lse_row, bsz, seq, name):
    n, d = v.shape
    hp = d // LANES
    t = _tile(seq, ATTN_TILE)
    nb, blk, full, col, rows = _attn_specs(bsz, seq, t)
    scale = 1.0 / math.sqrt(HEAD_DIM)
    tn_ = (((0,), (0,)), ((), ()))

    def body(k_ref, v_ref, q_ref, do_ref, o_ref, lse_ref, dq_ref, dcc_ref, dk_ref, dv_ref, dck_ref, dqa, dl):
        j = pl.program_id(2)
        causal = lax.broadcasted_iota(jnp.int32, (t, t), 1) >= lax.broadcasted_iota(jnp.int32, (t, t), 0)
        heads = [_head_lanes(t, 0), _head_lanes(t, 1)]

        @pl.when(j == 0)
        def _():
            dqa[...] = jnp.zeros_like(dqa)
            ones = jnp.ones((8, LANES), F32)
            for ib in range(nb):
                rs = slice(ib * t, (ib + 1) * t)
                prod = do_ref[rs, :] * o_ref[rs, :]
                for hh in range(2):
                    dl[ib, hh:hh + 1, :] = lax.dot_general(ones, jnp.where(heads[hh], prod, 0.0), _NT, precision=HI,
                                                           preferred_element_type=F32)[0:1]

        vj = v_ref[...]

        def block(i, carry, masked):
            rs = pl.ds(pl.multiple_of(i * t, t), t)
            doi = do_ref[rs, :]
            dks, dvp = list(carry[:2]), carry[2]
            for hh in range(2):
                hs = slice(hh * LANES, (hh + 1) * LANES)
                kh, qi = k_ref[:, hs], q_ref[rs, hs]
                dom = jnp.where(heads[hh], doi, 0.0).astype(BF16)
                st = lax.dot_general(kh, qi, _NT, preferred_element_type=F32)
                if masked:
                    st = jnp.where(causal, st, NEG)
                pt = jnp.exp(st - lse_ref[i, hh:hh + 1, :])
                dvp = dvp + jnp.dot(pt.astype(BF16), dom, preferred_element_type=F32)
                dpt = lax.dot_general(vj, dom, _NT, preferred_element_type=F32)
                dsb = (pt * (dpt - dl[i, hh:hh + 1, :])).astype(BF16)
                dks[hh] = dks[hh] + jnp.dot(dsb, qi, preferred_element_type=F32)
                dqa[rs, hs] += lax.dot_general(dsb, kh, tn_, preferred_element_type=F32)
            return dks[0], dks[1], dvp

        zero = jnp.zeros((t, LANES), F32)
        carry = block(j, (zero, zero, zero), True)
        dk0, dk1, dvp = lax.fori_loop(j + 1, nb, lambda i, c: block(i, c, False), carry)
        dk_ref[...] = jnp.where(heads[0], dk0, pltpu.roll(dk1, HEAD_DIM, 1))
        dv_ref[...] = dvp
        dck_ref[:, 0:1] = -dk0[:, HEAD_DIM + 3:HEAD_DIM + 4]
        dck_ref[:, 1:2] = -dk1[:, HEAD_DIM + 3:HEAD_DIM + 4]

        @pl.when(j == nb - 1)
        def _():
            first = lax.broadcasted_iota(jnp.int32, (seq, LANES), 1) < HEAD_DIM
            dq_ref[...] = jnp.where(first, dqa[:, :LANES], pltpu.roll(dqa[:, LANES:], HEAD_DIM, 1)) * scale
            for hh in range(2):
                lo = hh * LANES + HEAD_DIM
                dcc_ref[:, hh:hh + 1] = dqa[:, lo:lo + 1]

    whole_col = pl.BlockSpec((None, None, seq, 2), lambda b, h, i: (b, h, 0, 0))
    cshape = jax.ShapeDtypeStruct((bsz, hp, seq, 2), F32)
    nd = jax.ShapeDtypeStruct((n, d), F32)
    return pl.pallas_call(
        body, grid=(bsz, hp, nb),
        in_specs=[blk(2 * LANES), blk(LANES), full(2 * LANES), full(LANES), full(LANES), rows],
        out_specs=[full(LANES), whole_col, blk(LANES), blk(LANES), col],
        out_shape=[nd, cshape, nd, nd, cshape],
        scratch_shapes=[pltpu.VMEM((seq, 2 * LANES), F32), pltpu.VMEM((nb, 2, t), F32)],
        name=name, compiler_params=_cp("parallel", "parallel", "arbitrary"),
    )(ka, v, qa, do, o, lse_row)


def _adamw(w, g, m, v, name):
    r, c = w.shape
    tr = r
    for cand in (512, 256, 128, 64, 32, 16, 8):
        if r % cand == 0 and r > cand and cand * c * 4 <= 4 * 1024 * 1024:
            tr = cand
            break

    def body(w_ref, g_ref, m_ref, v_ref, d_ref, m2_ref, v2_ref):
        gv = g_ref[...]
        m2 = ADAM_B1 * m_ref[...] + (1.0 - ADAM_B1) * gv
        v2 = ADAM_B2 * v_ref[...] + (1.0 - ADAM_B2) * jnp.square(gv)
        m_hat = m2 / (1.0 - ADAM_B1 ** ADAM_STEP)
        v_hat = v2 / (1.0 - ADAM_B2 ** ADAM_STEP)
        d_ref[...] = -ADAM_LR * (m_hat / (jnp.sqrt(v_hat) + ADAM_EPS) + ADAM_WD * w_ref[...])
        m2_ref[...] = m2
        v2_ref[...] = v2

    blk = pl.BlockSpec((tr, c), lambda i: (i, 0))
    shp = jax.ShapeDtypeStruct((r, c), F32)
    return pl.pallas_call(
        body, grid=(r // tr,), in_specs=[blk] * 4, out_specs=[blk] * 3, out_shape=[shp] * 3,
        name=name, compiler_params=_cp("parallel"),
    )(w, g, m, v)


def _adamw_nd(w, g, m, v, name):
    shape = w.shape
    two = (math.prod(shape[:-1]), shape[-1])
    d_, m_, v_ = _adamw(w.reshape(two), g.reshape(two), m.reshape(two), v.reshape(two), name)
    return g.reshape(shape), d_.reshape(shape), m_.reshape(shape), v_.reshape(shape)


def _place():
    x, y, c = lax.axis_index("x"), lax.axis_index("y"), lax.axis_index("c")
    chips = [(1 - x, y), (x, 1 - y), (1 - x, 1 - y)]
    return x, y, c, chips


def _gather_chips(shards):
    nt = len(shards)
    halves = [s.shape[0] // 2 for s in shards]

    def copies(ins, outs, send_sems, recv_sems):
        x, y, c, chips = _place()
        cps = []
        for t in range(nt):
            rows = pl.ds(c * halves[t], halves[t])
            for jj, (cx, cy) in enumerate(chips):
                cps.append(pltpu.make_async_remote_copy(
                    src_ref=ins[t].at[rows, :], dst_ref=outs[t].at[2 * x + y, rows, :], send_sem=send_sems.at[3 * t + jj],
                    recv_sem=recv_sems.at[3 * t + jj], device_id=(cx, cy, c), device_id_type=MESH))
        return cps

    def start(*refs):
        for cp in copies(*refs):
            cp.start()

    def wait(*refs):
        for cp in copies(*refs):
            cp.wait()

    return _Comm(shards, [jax.ShapeDtypeStruct((N_CHIPS,) + s.shape, s.dtype) for s in shards], 3 * nt, start, wait)


def _gather_sibling(shards, bufs, name):
    nt = len(shards)
    halves = [s.shape[0] // 2 for s in shards]

    def body(*refs):
        ins, outs = refs[:nt], refs[2 * nt:3 * nt]
        send_sems, recv_sems = refs[3 * nt:]
        x, y, c, chips = _place()
        sibling = (x, y, 1 - c)

        def copy(t, jj, hf):
            cx, cy = chips[jj]
            region = outs[t].at[2 * cx + cy, pl.ds(hf * halves[t], halves[t]), :]
            return pltpu.make_async_remote_copy(src_ref=region, dst_ref=region, send_sem=send_sems.at[4 * t + jj],
                                                recv_sem=recv_sems.at[4 * t + jj], device_id=sibling,
                                                device_id_type=MESH)

        def own(t):
            return pltpu.make_async_remote_copy(src_ref=ins[t], dst_ref=outs[t].at[2 * x + y],
                                                send_sem=send_sems.at[4 * t + 3], recv_sem=recv_sems.at[4 * t + 3],
                                                device_id=sibling, device_id_type=MESH)

        sends = [copy(t, jj, c) for t in range(nt) for jj in range(3)] + [own(t) for t in range(nt)]
        for cp in sends:
            cp.start()
        for t in range(nt):
            for jj in range(3):
                copy(t, jj, 1 - c).wait_recv()
            own(t).wait_recv()
        for cp in sends:
            cp.wait_send()

    return pl.pallas_call(
        body, in_specs=[ANY] * (2 * nt), out_specs=[ANY] * nt,
        out_shape=[jax.ShapeDtypeStruct(b.shape, b.dtype) for b in bufs],
        scratch_shapes=[pltpu.SemaphoreType.DMA((4 * nt,)), pltpu.SemaphoreType.DMA((4 * nt,))],
        input_output_aliases={nt + t: t for t in range(nt)}, name=name,
        compiler_params=pltpu.CompilerParams(has_side_effects=True),
    )(*shards, *bufs)


def _send_sibling_halves(grads, name):
    nt = len(grads)
    halves = [g.shape[1] // 2 for g in grads]

    def body(*refs):
        ins, outs = refs[:nt], refs[nt:2 * nt]
        send_sems, recv_sems = refs[2 * nt:]
        x, y, c, _ = _place()
        cps = []
        for t in range(nt):
            src = ins[t].at[:, pl.ds((1 - c) * halves[t], halves[t]), :]
            cps.append(pltpu.make_async_remote_copy(src_ref=src, dst_ref=outs[t], send_sem=send_sems.at[t],
                                                    recv_sem=recv_sems.at[t], device_id=(x, y, 1 - c),
                                                    device_id_type=MESH))
        for cp in cps:
            cp.start()
        for cp in cps:
            cp.wait()

    return pl.pallas_call(
        body, in_specs=[ANY] * nt, out_specs=[ANY] * nt,
        out_shape=[jax.ShapeDtypeStruct((N_CHIPS, h, g.shape[2]), g.dtype) for g, h in zip(grads, halves)],
        scratch_shapes=[pltpu.SemaphoreType.DMA((nt,)), pltpu.SemaphoreType.DMA((nt,))],
        name=name, compiler_params=pltpu.CompilerParams(has_side_effects=True),
    )(*grads)


def _add_sibling(g, r1, c_idx, out_dtype, name):
    s, r, w = g.shape
    rh = r // 2
    tr = rh
    for cand in (512, 256, 128, 64, 32, 16, 8):
        if rh % cand == 0 and cand * w * 4 <= 4 * 1024 * 1024:
            tr = cand
            break
    per = rh // tr

    def body(c_ref, g_ref, r_ref, o_ref):
        o_ref[...] = (g_ref[...] + r_ref[...]).astype(o_ref.dtype)

    return pl.pallas_call(
        body,
        grid_spec=pltpu.PrefetchScalarGridSpec(
            num_scalar_prefetch=1, grid=(s, per),
            in_specs=[pl.BlockSpec((None, tr, w), lambda a, b, c_ref: (a, c_ref[0] * per + b, 0)),
                      pl.BlockSpec((None, tr, w), lambda a, b, c_ref: (a, b, 0))],
            out_specs=pl.BlockSpec((None, tr, w), lambda a, b, c_ref: (a, b, 0))),
        out_shape=jax.ShapeDtypeStruct((s, rh, w), out_dtype), name=name,
        compiler_params=_cp("parallel", "parallel"),
    )(c_idx, g, r1)


def _chip_slots(parts):
    nt = len(parts)

    def copies(ins, outs, send_sems, recv_sems):
        x, y, c, chips = _place()
        return [pltpu.make_async_remote_copy(
            src_ref=ins[t].at[2 * cx + cy], dst_ref=outs[t].at[jj], send_sem=send_sems.at[3 * t + jj],
            recv_sem=recv_sems.at[3 * t + jj], device_id=(cx, cy, c), device_id_type=MESH)
            for t in range(nt) for jj, (cx, cy) in enumerate(chips)]

    def start(*refs):
        for cp in copies(*refs):
            cp.start()

    def wait(*refs):
        for cp in copies(*refs):
            cp.wait()

    return _Comm(parts, [jax.ShapeDtypeStruct((3,) + p.shape[1:], p.dtype) for p in parts], 3 * nt, start, wait)


def _add_chips(part, r2, place_idx, name):
    s, rh, w = part.shape
    tr = rh
    for cand in (512, 256, 128, 64, 32, 16, 8):
        if rh % cand == 0 and cand * w * 4 <= 4 * 1024 * 1024:
            tr = cand
            break
    per = rh // tr

    def body(b_ref, p_ref, r0_ref, r1_ref, r2_ref, o_ref):
        up = lambda ref: ref[...].astype(F32)
        o_ref[...] = ((up(p_ref) + up(r0_ref)) + up(r1_ref)) + up(r2_ref)

    return pl.pallas_call(
        body,
        grid_spec=pltpu.PrefetchScalarGridSpec(
            num_scalar_prefetch=1, grid=(per,),
            in_specs=[pl.BlockSpec((None, tr, w), lambda a, b_ref: (b_ref[0], a, 0))]
            + [pl.BlockSpec((None, tr, w), lambda a, b_ref, jj=jj: (jj, a, 0)) for jj in range(3)],
            out_specs=pl.BlockSpec((tr, w), lambda a, b_ref: (b_ref[1] * per + a, 0))),
        out_shape=jax.ShapeDtypeStruct((2 * rh, w), F32), name=name, compiler_params=_cp("parallel"),
    )(place_idx, part, r2, r2, r2)


def _swap_halves(bufs, name):
    nt = len(bufs)

    def body(*refs):
        outs = refs[nt:2 * nt]
        send_sems, recv_sems = refs[2 * nt:]
        x, y, c, _ = _place()

        def copy(t, hf):
            rh = outs[t].shape[0] // 2
            region = outs[t].at[pl.ds(hf * rh, rh), :]
            return pltpu.make_async_remote_copy(src_ref=region, dst_ref=region, send_sem=send_sems.at[t],
                                                recv_sem=recv_sems.at[t], device_id=(x, y, 1 - c),
                                                device_id_type=MESH)

        sends = [copy(t, c) for t in range(nt)]
        for cp in sends:
            cp.start()
        for t in range(nt):
            copy(t, 1 - c).wait_recv()
        for cp in sends:
            cp.wait_send()

    return pl.pallas_call(
        body, in_specs=[ANY] * nt, out_specs=[ANY] * nt,
        out_shape=[jax.ShapeDtypeStruct(b.shape, b.dtype) for b in bufs],
        scratch_shapes=[pltpu.SemaphoreType.DMA((nt,)), pltpu.SemaphoreType.DMA((nt,))],
        input_output_aliases={t: t for t in range(nt)}, name=name,
        compiler_params=pltpu.CompilerParams(has_side_effects=True),
    )(*bufs)


def _rs_begin(grads, wire, c_idx, tag):
    r1 = _send_sibling_halves(grads, name=f"rs_sibling_{tag}")
    return [_add_sibling(g, r, c_idx, wire[t], name=f"rs_add_sibling_{tag}_{t}")
            for t, (g, r) in enumerate(zip(grads, r1))]


def _rs_finish(parts, r2, chip_idx, tag):
    fin = [_add_chips(p, r, chip_idx, name=f"rs_add_chips_{tag}_{t}") for t, (p, r) in enumerate(zip(parts, r2))]
    return _swap_halves(fin, name=f"rs_swap_{tag}")


def _all_reduce_small(v, name):
    r, w = v.shape

    def body(v_ref, o_ref, buf, send_sems, recv_sems):
        x, y, c, _ = _place()
        me = 4 * x + 2 * y + c
        buf[me] = v_ref[...]
        cps = []
        for kk in range(1, 8):
            peer = (x ^ ((kk >> 2) & 1), y ^ ((kk >> 1) & 1), c ^ (kk & 1))
            cps.append(pltpu.make_async_remote_copy(src_ref=v_ref, dst_ref=buf.at[me], send_sem=send_sems.at[kk - 1],
                                                    recv_sem=recv_sems.at[kk - 1], device_id=peer, device_id_type=MESH))
        for cp in cps:
            cp.start()
        for kk in range(1, 8):
            pltpu.make_async_remote_copy(src_ref=v_ref, dst_ref=buf.at[me ^ kk], send_sem=send_sems.at[kk - 1],
                                         recv_sem=recv_sems.at[kk - 1], device_id=(x, y, c),
                                         device_id_type=MESH).wait_recv()
        for cp in cps:
            cp.wait_send()
        acc = buf[0]
        for dev in range(1, 8):
            acc = acc + buf[dev]
        o_ref[...] = acc

    vm = pl.BlockSpec(memory_space=pltpu.VMEM)
    return pl.pallas_call(
        body, in_specs=[vm], out_specs=vm, out_shape=jax.ShapeDtypeStruct((r, w), F32),
        scratch_shapes=[pltpu.VMEM((8, r, w), F32), pltpu.SemaphoreType.DMA((7,)), pltpu.SemaphoreType.DMA((7,))],
        name=name, compiler_params=pltpu.CompilerParams(has_side_effects=True),
    )(v)


def _to_shards(a, axis=-1):
    axis = axis % a.ndim
    shp = a.shape
    a = a.reshape(shp[:axis] + (N_CHIPS, shp[axis] // N_CHIPS) + shp[axis + 1:])
    return jnp.moveaxis(a, axis, 0).reshape(N_CHIPS, -1)


def _from_shards(s, shard_shape, axis=-1):
    axis = axis % len(shard_shape)
    a = jnp.moveaxis(s.reshape((N_CHIPS,) + tuple(shard_shape)), 0, axis)
    return a.reshape(tuple(shard_shape[:axis]) + (N_CHIPS * shard_shape[axis],) + tuple(shard_shape[axis + 1:]))


def _pack(vecs, rows):
    flat = jnp.concatenate([v.reshape(v.shape[0], -1) if v.ndim > 1 else v.reshape(1, -1) for v in vecs], axis=1)
    lead = flat.shape[0]
    flat = jnp.pad(flat, ((0, 0), (0, rows * LANES - flat.shape[1])))
    return flat.reshape(lead, rows, LANES)


def _pack_rows(sizes, mult):
    total = sum(sizes)
    rows = -(-total // LANES)
    return -(-rows // mult) * mult


def _unpack(flat, shapes):
    out, pos = [], 0
    for shp in shapes:
        sz = math.prod(shp)
        out.append(flat[..., pos:pos + sz].reshape(flat.shape[:-1] + tuple(shp)))
        pos += sz
    return out


def _row_layout(col, t):
    bsz, hp, seq, _ = col.shape
    return col.reshape(bsz, hp, seq // t, t, 2).transpose(0, 1, 2, 4, 3)


def _from_col_layout(col):
    bsz, hp, seq, _ = col.shape
    a = col.transpose(0, 2, 1, 3).reshape(bsz * seq, 2 * hp)
    return jnp.pad(a, ((0, 0), (0, LANES - 2 * hp)))


def kernel(x, norm_mix, norm_ffn, conv_w_in, conv_b_in, conv_dw, conv_dw_b, conv_ln_g, conv_ln_b, conv_w_out, conv_b_out, pool_w, pool_b, pool_scale, fox_w_in, fox_b_f, fox_q_gain, fox_k_gain, fox_w_o, ffn_w_up, ffn_dw, ffn_dw_b, ffn_w_down, loss_target, m_norm_mix, m_norm_ffn, m_conv_w_in, m_conv_b_in, m_conv_dw, m_conv_dw_b, m_conv_ln_g, m_conv_ln_b, m_conv_w_out, m_conv_b_out, m_pool_w, m_pool_b, m_pool_scale, m_fox_w_in, m_fox_b_f, m_fox_q_gain, m_fox_k_gain, m_fox_w_o, m_ffn_w_up, m_ffn_dw, m_ffn_dw_b, m_ffn_w_down, v_norm_mix, v_norm_ffn, v_conv_w_in, v_conv_b_in, v_conv_dw, v_conv_dw_b, v_conv_ln_g, v_conv_ln_b, v_conv_w_out, v_conv_b_out, v_pool_w, v_pool_b, v_pool_scale, v_fox_w_in, v_fox_b_f, v_fox_q_gain, v_fox_k_gain, v_fox_w_o, v_ffn_w_up, v_ffn_dw, v_ffn_dw_b, v_ffn_w_down):
    bsz, seq, d = x.shape
    n = bsz * seq
    depth = norm_mix.shape[0]
    n_conv, n_pool, n_fox = conv_w_in.shape[0], pool_w.shape[0], fox_w_in.shape[0]
    f2 = ffn_dw_b.shape[1]
    f = f2 // 2
    nh = d // HEAD_DIM
    hp = d // LANES
    ng = len(POOL_WINDOWS)
    gd = d // ng
    c_idx = lax.axis_index("c").astype(jnp.int32).reshape(1)
    chip_idx = jnp.stack([2 * lax.axis_index("x") + lax.axis_index("y"), lax.axis_index("c")]).astype(jnp.int32)

    small_shapes = [conv_b_in.shape, conv_dw.shape, conv_dw_b.shape, conv_ln_g.shape, conv_ln_b.shape,
                    conv_b_out.shape, pool_b.shape, ffn_dw.shape]
    small_rows = _pack_rows([math.prod(s) for s in small_shapes], 16)
    small = _pack([v.reshape(1, -1) for v in (conv_b_in, conv_dw, conv_dw_b, conv_ln_g, conv_ln_b, conv_b_out,
                                                pool_b, ffn_dw)], small_rows)[0]
    def layer_shards(i):
        kind, j = i % 3, i // 3
        shards = [ffn_w_up[i].astype(BF16), ffn_w_down[i].astype(BF16)]
        if kind == 0:
            shards += [conv_w_in[j].astype(BF16), conv_w_out[j].astype(BF16)]
        elif kind == 1:
            shards += [pool_w[j].reshape(ng * (gd // N_CHIPS), gd).astype(BF16)]
        else:
            shards += [fox_w_in[j].astype(BF16), fox_w_o[j].astype(BF16)]
        if i == 0:
            shards.append(small)
        return shards

    gathered = [None] * depth
    gathered[0] = _gather_sibling(layer_shards(0), _run_comm(_gather_chips(layer_shards(0)), name="gather_chips_l0"),
                                  name="gather_sibling_l0")
    small_all = gathered[0][-1].reshape(N_CHIPS, -1)
    sm = _unpack(small_all, small_shapes)
    axes = [-1] * 8
    b_in_f, dw_f, dw_b_f, ln_g_f, ln_b_f, b_out_f, pool_b_f, ffn_dw_f = [
        _from_shards(s_.reshape(N_CHIPS, -1), shp, ax) for s_, shp, ax in zip(sm, small_shapes, axes)]

    xs = x.reshape(n, d)
    tgt = loss_target.reshape(n, d)
    vec = lambda a: a.reshape(1, -1)

    saved = []
    cur = xs
    for i in range(depth):
        kind, j = i % 3, i // 3
        wts = gathered[i]
        sv = {"x_in": cur}
        if kind == 0:
            w_in, w_out = wts[2], wts[3].reshape(d, d)
            wcol = w_in.shape[2]
            h = _rms_fwd(cur, vec(norm_mix[i]), name=f"rms_mix_l{i}")
            pa = _mm(h, w_in, m=n, n=d, k=d, tk=d, tn=wcol, b_stk=wcol, b_s0=0, bias=vec(b_in_f[j, :d]),
                     name=f"conv_in_a_l{i}")
            pg = _mm(h, w_in, m=n, n=d, k=d, tk=d, tn=wcol, b_stk=wcol, b_s0=2, bias=vec(b_in_f[j, d:]),
                     name=f"conv_in_g_l{i}")
            taps = jnp.pad(dw_f[j], ((0, CONV_HALO - CONV_WIDTH), (0, 0)))
            u, s_ = _conf_fwd(pa, pg, taps, vec(dw_b_f[j]), vec(ln_g_f[j]), vec(ln_b_f[j]), seq, name=f"conf_fwd_l{i}")
            cur = _mm(s_, w_out, m=n, n=d, k=d, tk=d, bias=vec(b_out_f[j]), res=cur, name=f"conv_out_l{i}")
            sv.update(h=h, pa=pa, pg=pg, u=u, s=s_, taps=taps)
        elif kind == 1:
            pw = wts[2].reshape(N_CHIPS, ng, gd // N_CHIPS, gd).transpose(1, 0, 2, 3).reshape(ng, gd, gd)
            cur, p = _pool_fwd(cur, vec(norm_mix[i]), pw, vec(pool_b_f[j]), vec(pool_scale[j]), seq,
                               name=f"pool_fwd_l{i}")
            sv.update(p=p, pw=pw)
        else:
            w_in = wts[2].transpose(1, 0, 2).reshape(d, -1)
            w_qkv = w_in[:, :3 * d]
            w_f = jnp.pad(w_in[:, 3 * d:], ((0, 0), (0, LANES - nh)))
            w_o = wts[3].reshape(d, d)
            bf = jnp.pad(vec(fox_b_f[j]), ((0, 0), (0, LANES - nh)))
            qg, kg = jnp.tile(vec(fox_q_gain[j]), (1, nh)), jnp.tile(vec(fox_k_gain[j]), (1, nh))
            h = _rms_fwd(cur, vec(norm_mix[i]), name=f"rms_mix_l{i}")
            qkv = _mm(h, w_qkv, m=n, n=3 * d, k=d, tk=d, tn=d, name=f"fox_qkv_l{i}")
            fl = _mm(h, w_f, m=n, n=LANES, k=d, tk=d, name=f"fox_fl_l{i}")
            qa, ka, v = _fox_prep_fwd(qkv, fl, bf, qg, kg, seq, name=f"fox_prep_l{i}")
            o, lse = _flash_fwd(qa, ka, v, bsz, seq, name=f"fox_attn_l{i}")
            cur = _mm(o, w_o, m=n, n=d, k=d, tk=d, res=cur, name=f"fox_out_l{i}")
            sv.update(h=h, qkv=qkv, fl=fl, qa=qa, ka=ka, v=v, o=o, lse=lse, w_qkv=w_qkv, w_f=w_f, w_o=w_o, bf=bf,
                      qg=qg, kg=kg)
        w_up, w_down = wts[0], wts[1].reshape(f, d)
        ucol = w_up.shape[2]
        sv["x_mid"] = cur
        h2 = _rms_fwd(cur, vec(norm_ffn[i]), name=f"rms_ffn_l{i}")
        uv = _mm(h2, w_up, m=n, n=f, k=d, tk=d, tn=ucol, b_stk=ucol, b_s0=0, name=f"ffn_up_v_l{i}")
        ug = _mm(h2, w_up, m=n, n=f, k=d, tk=d, tn=ucol, b_stk=ucol, b_s0=2, name=f"ffn_up_g_l{i}")
        fdw, fdb = ffn_dw_f[i], ffn_dw_b[i]
        glu_args = (uv, ug, fdw[:, :f], fdw[:, f:], vec(fdb[:f]), vec(fdb[f:]), seq)
        if i + 1 < depth:
            nxt_shards = layer_shards(i + 1)
            (a_,), landed = _ffn_glu_fwd(*glu_args, name=f"ffn_glu_l{i}", comm=_gather_chips(nxt_shards))
            gathered[i + 1] = _gather_sibling(nxt_shards, landed, name=f"gather_sibling_l{i + 1}")
        else:
            (a_,) = _ffn_glu_fwd(*glu_args, name=f"ffn_glu_l{i}")
        cur = _mm(a_, w_down, m=n, n=d, k=f, tk=f, tn=d, res=cur, name=f"ffn_down_l{i}")
        sv.update(h2=h2, uv=uv, ug=ug, a=a_)
        saved.append(sv)

    dy, loss_part = _loss(cur, tgt, name="loss")
    loss = lax.psum(loss_part[0, 0], ("x", "y", "c"))

    g_norm_mix, g_norm_ffn = [None] * depth, [None] * depth
    g_ffn_dw_b = [None] * depth
    g_up, g_down = [None] * depth, [None] * depth
    g_small = {}
    g_conv_in, g_conv_out = [None] * n_conv, [None] * n_conv
    g_pool_w = g_fox_in = g_fox_o = None
    g_pool_scale = g_bf = g_qg = g_kg = None
    part_small = {"b_in": [None] * n_conv, "dw": [None] * n_conv, "dw_b": [None] * n_conv, "ln_g": [None] * n_conv,
                  "ln_b": [None] * n_conv, "b_out": [None] * n_conv, "pool_b": None, "ffn_dw": [None] * depth}

    pending = []
    done = {}

    def take_pending():
        groups = list(pending)
        pending.clear()
        parts = [p_ for g_ in groups for p_ in g_[0]]
        return groups, (_chip_slots(parts) if parts else None)

    def finish_groups(groups, r2):
        pos = 0
        for parts, tag, sink in groups:
            sink(_rs_finish(parts, r2[pos:pos + len(parts)], chip_idx, tag))
            pos += len(parts)

    def carried(fn, args, name):
        groups, comm = take_pending()
        if comm is None:
            return fn(*args, name=name)
        outs, r2 = fn(*args, name=name, comm=comm)
        finish_groups(groups, r2)
        return outs

    dcur = dy
    for i in reversed(range(depth)):
        kind, j = i % 3, i // 3
        wts, sv = gathered[i], saved[i]
        w_up, w_down = wts[0], wts[1].reshape(f, d)
        ucol = w_up.shape[2]
        fdw, fdb = ffn_dw_f[i], ffn_dw_b[i]
        da = _mm(dcur, w_down, m=n, n=f, k=d, tk=d, tn=f, tb=True, name=f"ffn_down_dx_l{i}")
        dw_down = _mm(sv["a"], dcur, m=f, n=d, k=n, tm=f // 2, tn=d, tk=2048, ta=True, name=f"ffn_down_dw_l{i}")
        duv, dug, dwv, dwg, dbv, dbg = carried(
            _ffn_glu_bwd, (sv["uv"], sv["ug"], da, fdw[:, :f], fdw[:, f:], vec(fdb[:f]), vec(fdb[f:]), seq),
            name=f"ffn_glu_bwd_l{i}")
        dw_up = _mm(sv["h2"], duv, m=d, n=f, k=n, tm=d, tn=ucol, tk=2048, ta=True, o_stk=ucol, o_s0=0,
                    o_slots=N_CHIPS, name=f"ffn_up_dw_v_l{i}")
        dw_up = _mm(sv["h2"], dug, m=d, n=f, k=n, tm=d, tn=ucol, tk=2048, ta=True, o_stk=ucol, o_s0=2,
                    o_slots=N_CHIPS, o_buf=dw_up, name=f"ffn_up_dw_g_l{i}")
        dh2 = _mm(duv, w_up, m=n, n=d, k=f, tn=d, tk=ucol, tb=True, b_stk=ucol, b_s0=0, name=f"ffn_up_dx_v_l{i}")
        dh2 = _mm(dug, w_up, m=n, n=d, k=f, tn=d, tk=ucol, tb=True, b_stk=ucol, b_s0=2, res=dh2,
                  name=f"ffn_up_dx_g_l{i}")
        dmid, g_norm_ffn[i] = _rms_bwd(sv["x_mid"], dh2, vec(norm_ffn[i]), dcur, name=f"rms_ffn_bwd_l{i}")
        part_small["ffn_dw"][i] = jnp.concatenate([dwv, dwg], axis=1)
        g_ffn_dw_b[i] = jnp.concatenate([dbv, dbg], axis=1)

        def ffn_sink(red, i=i):
            g_up[i], g_down[i] = red[0], red[1]

        pending.append((_rs_begin([dw_up, dw_down.reshape(N_CHIPS, f // N_CHIPS, d)], [BF16, BF16], c_idx,
                                  tag=f"ffn_l{i}"), f"ffn_l{i}", ffn_sink))

        if kind == 0:
            w_in, w_out = wts[2], wts[3].reshape(d, d)
            wcol = w_in.shape[2]
            ds = _mm(dmid, w_out, m=n, n=d, k=d, tk=d, tb=True, name=f"conv_out_dx_l{i}")
            dw_out = _mm(sv["s"], dmid, m=d, n=d, k=n, tm=d, tn=d, tk=2048, ta=True, name=f"conv_out_dw_l{i}")
            dpa, dpg, ddw, ddwb, dlng, dlnb, dba, dbg_ = carried(
                _conf_bwd, (sv["u"], ds, sv["pa"], sv["pg"], sv["taps"], vec(ln_g_f[j]), vec(ln_b_f[j]), seq),
                name=f"conf_bwd_l{i}")
            dw_in = _mm(sv["h"], dpa, m=d, n=d, k=n, tm=d, tn=wcol, tk=2048, ta=True, o_stk=wcol, o_s0=0,
                        o_slots=N_CHIPS, name=f"conv_in_dw_a_l{i}")
            dw_in = _mm(sv["h"], dpg, m=d, n=d, k=n, tm=d, tn=wcol, tk=2048, ta=True, o_stk=wcol, o_s0=2,
                        o_slots=N_CHIPS, o_buf=dw_in, name=f"conv_in_dw_g_l{i}")
            dh = _mm(dpa, w_in, m=n, n=d, k=d, tn=d, tk=wcol, tb=True, b_stk=wcol, b_s0=0, name=f"conv_in_dx_a_l{i}")
            dh = _mm(dpg, w_in, m=n, n=d, k=d, tn=d, tk=wcol, tb=True, b_stk=wcol, b_s0=2, res=dh,
                     name=f"conv_in_dx_g_l{i}")
            dcur, g_norm_mix[i], db_out = _rms_bwd(sv["x_in"], dh, vec(norm_mix[i]), dmid, name=f"rms_mix_bwd_l{i}",
                                                   colsum=True)
            part_small["b_in"][j] = jnp.concatenate([dba, dbg_], axis=1)
            part_small["dw"][j] = ddw[:CONV_WIDTH]
            part_small["dw_b"][j], part_small["ln_g"][j], part_small["ln_b"][j] = ddwb, dlng, dlnb
            part_small["b_out"][j] = db_out
            mix_grads = [dw_in, dw_out.reshape(N_CHIPS, d // N_CHIPS, d)]
        elif kind == 1:
            dcur, g_norm_mix[i], dpw, dpb, dpsc = _pool_bwd(dmid, sv["x_in"], sv["p"], vec(norm_mix[i]), sv["pw"],
                                                            vec(pool_b_f[j]), vec(pool_scale[j]), seq,
                                                            name=f"pool_bwd_l{i}")
            part_small["pool_b"] = dpb
            g_pool_scale = dpsc
            mix_grads = [dpw.reshape(ng, N_CHIPS, gd // N_CHIPS, gd).transpose(1, 0, 2, 3).reshape(N_CHIPS, gd, gd)]
        else:
            do = _mm(dmid, sv["w_o"], m=n, n=d, k=d, tk=d, tb=True, name=f"fox_out_dx_l{i}")
            dw_o = _mm(sv["o"], dmid, m=d, n=d, k=n, tm=d, tn=d, tk=2048, ta=True, name=f"fox_out_dw_l{i}")
            dq, dcc, dk, dv, dck = _flash_bwd(sv["qa"], sv["ka"], sv["v"], do, sv["o"],
                                              _row_layout(sv["lse"], _tile(seq, ATTN_TILE)), bsz, seq,
                                              name=f"fox_attn_bwd_l{i}")
            dqkv, dfl, dqg, dkg, dbf = carried(
                _fox_prep_bwd, (sv["qkv"], dq, dk, dv, _from_col_layout(dcc), _from_col_layout(dck), sv["fl"], sv["bf"],
                                sv["qg"], sv["kg"], seq), name=f"fox_prep_bwd_l{i}")
            dw_qkv = _mm(sv["h"], dqkv, m=d, n=3 * d, k=n, tm=d, tn=d, tk=2048, ta=True, name=f"fox_qkv_dw_l{i}")
            dw_f = _mm(sv["h"], dfl, m=d, n=LANES, k=n, tm=d, tk=2048, ta=True, name=f"fox_fl_dw_l{i}")
            dh = _mm(dqkv, sv["w_qkv"], m=n, n=d, k=3 * d, tn=d, tk=d, tb=True, name=f"fox_qkv_dx_l{i}")
            dh = _mm(dfl, sv["w_f"], m=n, n=d, k=LANES, tn=d, tb=True, res=dh, name=f"fox_fl_dx_l{i}")
            dcur, g_norm_mix[i] = _rms_bwd(sv["x_in"], dh, vec(norm_mix[i]), dmid, name=f"rms_mix_bwd_l{i}")
            g_bf = dbf[:, :nh]
            g_qg = dqg.reshape(nh, HEAD_DIM).sum(axis=0, keepdims=True)
            g_kg = dkg.reshape(nh, HEAD_DIM).sum(axis=0, keepdims=True)
            dw_in_full = jnp.concatenate([dw_qkv, dw_f[:, :nh]], axis=1)
            wshard = dw_in_full.shape[1] // N_CHIPS
            mix_grads = [dw_in_full.reshape(d, N_CHIPS, wshard).transpose(1, 0, 2),
                         dw_o.reshape(N_CHIPS, d // N_CHIPS, d)]
        wire = [BF16] * len(mix_grads)
        if i == 0:
            sm_parts = [jnp.concatenate(part_small["b_in"]), jnp.stack(part_small["dw"]),
                        jnp.concatenate(part_small["dw_b"]), jnp.concatenate(part_small["ln_g"]),
                        jnp.concatenate(part_small["ln_b"]), jnp.concatenate(part_small["b_out"]),
                        part_small["pool_b"].reshape(n_pool, ng, gd), jnp.stack(part_small["ffn_dw"])]
            mix_grads.append(_pack([_to_shards(p_) for p_ in sm_parts], small_rows))
            wire.append(F32)

        def mix_sink(red, i=i, kind=kind, j=j):
            if kind == 0:
                g_conv_in[j], g_conv_out[j] = red[0], red[1]
            elif kind == 1:
                done["pool_w"] = red[0]
            else:
                done["fox_in"], done["fox_o"] = red[0], red[1]
            if i == 0:
                done["small"] = red[-1].reshape(-1)

        pending.append((_rs_begin(mix_grads, wire, c_idx, tag=f"mix_l{i}"), f"mix_l{i}", mix_sink))

    groups, comm = take_pending()
    finish_groups(groups, _run_comm(comm, name="rs_chips_tail"))
    g_pool_w, g_fox_in, g_fox_o, g_small_flat = done["pool_w"], done["fox_in"], done["fox_o"], done["small"]

    grad_x = dcur.reshape(bsz, seq, d)

    rep_parts = [jnp.concatenate(g_norm_mix), jnp.concatenate(g_norm_ffn), g_pool_scale, g_bf, g_qg, g_kg,
                 jnp.concatenate(g_ffn_dw_b)]
    rep_shapes = [norm_mix.shape, norm_ffn.shape, pool_scale.shape, fox_b_f.shape, fox_q_gain.shape,
                  fox_k_gain.shape, ffn_dw_b.shape]
    rep_rows = _pack_rows([math.prod(s) for s in rep_shapes], 8)
    rep = _all_reduce_small(_pack([p_.reshape(1, -1) for p_ in rep_parts], rep_rows)[0], name="all_reduce_small")
    g_rep = _unpack(rep.reshape(-1), rep_shapes)
    g_sm = _unpack(g_small_flat, small_shapes)

    grads = {
        "norm_mix": g_rep[0], "norm_ffn": g_rep[1],
        "conv_w_in": jnp.stack(g_conv_in), "conv_b_in": g_sm[0], "conv_dw": g_sm[1], "conv_dw_b": g_sm[2],
        "conv_ln_g": g_sm[3], "conv_ln_b": g_sm[4], "conv_w_out": jnp.stack(g_conv_out), "conv_b_out": g_sm[5],
        "pool_w": g_pool_w.reshape(pool_w.shape), "pool_b": g_sm[6], "pool_scale": g_rep[2],
        "fox_w_in": g_fox_in.reshape(fox_w_in.shape), "fox_b_f": g_rep[3], "fox_q_gain": g_rep[4],
        "fox_k_gain": g_rep[5], "fox_w_o": g_fox_o.reshape(fox_w_o.shape),
        "ffn_w_up": jnp.stack(g_up), "ffn_dw": g_sm[7], "ffn_dw_b": g_rep[6], "ffn_w_down": jnp.stack(g_down),
    }
    weights = dict(norm_mix=norm_mix, norm_ffn=norm_ffn, conv_w_in=conv_w_in, conv_b_in=conv_b_in, conv_dw=conv_dw,
                   conv_dw_b=conv_dw_b, conv_ln_g=conv_ln_g, conv_ln_b=conv_ln_b, conv_w_out=conv_w_out,
                   conv_b_out=conv_b_out, pool_w=pool_w, pool_b=pool_b, pool_scale=pool_scale, fox_w_in=fox_w_in,
                   fox_b_f=fox_b_f, fox_q_gain=fox_q_gain, fox_k_gain=fox_k_gain, fox_w_o=fox_w_o, ffn_w_up=ffn_w_up,
                   ffn_dw=ffn_dw, ffn_dw_b=ffn_dw_b, ffn_w_down=ffn_w_down)
    m_in = dict(norm_mix=m_norm_mix, norm_ffn=m_norm_ffn, conv_w_in=m_conv_w_in, conv_b_in=m_conv_b_in,
                conv_dw=m_conv_dw, conv_dw_b=m_conv_dw_b, conv_ln_g=m_conv_ln_g, conv_ln_b=m_conv_ln_b,
                conv_w_out=m_conv_w_out, conv_b_out=m_conv_b_out, pool_w=m_pool_w, pool_b=m_pool_b,
                pool_scale=m_pool_scale, fox_w_in=m_fox_w_in, fox_b_f=m_fox_b_f, fox_q_gain=m_fox_q_gain,
                fox_k_gain=m_fox_k_gain, fox_w_o=m_fox_w_o, ffn_w_up=m_ffn_w_up, ffn_dw=m_ffn_dw,
                ffn_dw_b=m_ffn_dw_b, ffn_w_down=m_ffn_w_down)
    v_in = dict(norm_mix=v_norm_mix, norm_ffn=v_norm_ffn, conv_w_in=v_conv_w_in, conv_b_in=v_conv_b_in,
                conv_dw=v_conv_dw, conv_dw_b=v_conv_dw_b, conv_ln_g=v_conv_ln_g, conv_ln_b=v_conv_ln_b,
                conv_w_out=v_conv_w_out, conv_b_out=v_conv_b_out, pool_w=v_pool_w, pool_b=v_pool_b,
                pool_scale=v_pool_scale, fox_w_in=v_fox_w_in, fox_b_f=v_fox_b_f, fox_q_gain=v_fox_q_gain,
                fox_k_gain=v_fox_k_gain, fox_w_o=v_fox_w_o, ffn_w_up=v_ffn_w_up, ffn_dw=v_ffn_dw,
                ffn_dw_b=v_ffn_dw_b, ffn_w_down=v_ffn_w_down)
    names = list(weights)
    g_out, d_out, m_out, v_out = [], [], [], []
    for nm in names:
        g_, dl_, m_, v_ = _adamw_nd(weights[nm], grads[nm].reshape(weights[nm].shape), m_in[nm], v_in[nm],
                                    name=f"adamw_{nm}")
        g_out.append(g_)
        d_out.append(dl_)
        m_out.append(m_)
        v_out.append(v_)
    return (loss, grad_x, *g_out, *d_out, *m_out, *v_out)
```

```python
import math

import jax
import jax.numpy as jnp
import numpy as np
from jax import lax
from jax.experimental import pallas as pl
from jax.experimental.pallas import tpu as pltpu

F32 = jnp.float32
BF16 = jnp.bfloat16
HI = lax.Precision.HIGHEST
MESH = pl.DeviceIdType.MESH
ANY = pl.BlockSpec(memory_space=pl.ANY)

EPS = 1e-6
HEAD_DIM = 64
LANES = 128
POOL_WINDOWS = (2, 4, 8, 16)
CONV_WIDTH = 31
CONV_HALO = 32
FFN_HALO = 8
FFN_ROWS, FFN_COLS = 256, 1408
POOL_HALO = 16
N_CHIPS = 4
NEG = -1e30

ADAM_LR = 0.001
ADAM_B1 = 0.9
ADAM_B2 = 0.999
ADAM_EPS = 1e-08
ADAM_WD = 0.01
ADAM_STEP = 10

V7X_VMEM_LIMIT_BYTES = 56 * 1024 * 1024


def _cp(*sem):
    return pltpu.CompilerParams(dimension_semantics=sem or None, vmem_limit_bytes=V7X_VMEM_LIMIT_BYTES)


def _tile(n, pref):
    t = min(n, pref)
    assert n % t == 0, (n, pref)
    return t


def _sig(v):
    return jax.nn.sigmoid(v)


def _roll(v, shift):
    n = v.shape[0]
    shift = shift % n
    return v if shift == 0 else pltpu.roll(v, shift, 0)


SUBLANES = 8


CONV_ROWS = 64


def _conv_taps(rot_ref, w_ref, out_ref, tm, start_of):
    d = out_ref.shape[1]
    for lc in range(d // LANES):
        ls = slice(lc * LANES, (lc + 1) * LANES)

        for r0 in range(0, tm, CONV_ROWS):
            acc = None
            for sh in range(CONV_WIDTH):
                kk = CONV_WIDTH - 1 - sh
                lo = r0 + start_of(sh)
                term = w_ref[kk:kk + 1, ls] * rot_ref[sh % SUBLANES, lo:lo + CONV_ROWS, ls]
                acc = term if acc is None else acc + term
            out_ref[r0:r0 + CONV_ROWS, ls] = acc


def _tap_grads(rotz_ref, rotd_ref, dw_ref, tm, halo):
    d = dw_ref.shape[1]
    for lc in range(d // LANES):
        ls = slice(lc * LANES, (lc + 1) * LANES)

        acc = [None] * CONV_WIDTH
        for r0 in range(0, tm, CONV_ROWS):
            duc = rotd_ref[0, r0:r0 + CONV_ROWS, ls]
            for sh in range(CONV_WIDTH):
                lo = r0 + halo - (sh // SUBLANES) * SUBLANES
                prod = duc * rotz_ref[sh % SUBLANES, lo:lo + CONV_ROWS, ls]
                part = prod.reshape(CONV_ROWS // SUBLANES, SUBLANES, LANES).sum(axis=0)
                acc[sh] = part if acc[sh] is None else acc[sh] + part
        for sh in range(CONV_WIDTH):
            kk = CONV_WIDTH - 1 - sh
            dw_ref[kk:kk + 1, ls] += jnp.sum(acc[sh], axis=0, keepdims=True)


class _Comm:
    def __init__(self, ins, out_shapes, n_sems, start, wait):
        self.ins, self.out_shapes, self.n_sems, self.start, self.wait = list(ins), list(out_shapes), n_sems, start, wait


def _hosted(body, comm, *, grid, in_specs, out_specs, out_shape, scratch_shapes, sem, name, ins):
    if comm is None:
        return pl.pallas_call(body, grid=grid, in_specs=in_specs, out_specs=out_specs, out_shape=out_shape,
                              scratch_shapes=scratch_shapes, name=name, compiler_params=_cp(*sem))(*ins)
    n_in, n_out, n_scr = len(in_specs), len(out_specs), len(scratch_shapes)
    nci, nco = len(comm.ins), len(comm.out_shapes)

    def wrapped(*refs):
        pos = [0]

        def take(cnt):
            pos[0] += cnt
            return refs[pos[0] - cnt:pos[0]]

        r_in, c_in, r_out, c_out, r_scr = take(n_in), take(nci), take(n_out), take(nco), take(n_scr)
        send_sems, recv_sems = take(2)
        ids = [pl.program_id(ax) for ax in range(len(grid))]
        first, last = ids[0] == 0, ids[0] == grid[0] - 1
        for ax in range(1, len(grid)):
            first = jnp.logical_and(first, ids[ax] == 0)
            last = jnp.logical_and(last, ids[ax] == grid[ax] - 1)

        @pl.when(first)
        def _():
            comm.start(c_in, c_out, send_sems, recv_sems)

        body(*r_in, *r_out, *r_scr)

        @pl.when(last)
        def _():
            comm.wait(c_in, c_out, send_sems, recv_sems)

    outs = pl.pallas_call(
        wrapped, grid=grid, in_specs=list(in_specs) + [ANY] * nci, out_specs=list(out_specs) + [ANY] * nco,
        out_shape=list(out_shape) + comm.out_shapes,
        scratch_shapes=list(scratch_shapes) + [pltpu.SemaphoreType.DMA((comm.n_sems,))] * 2, name=name,
        compiler_params=pltpu.CompilerParams(dimension_semantics=sem, vmem_limit_bytes=V7X_VMEM_LIMIT_BYTES,
                                             has_side_effects=True),
    )(*ins, *comm.ins)
    return list(outs[:n_out]), list(outs[n_out:])


def _run_comm(comm, name):
    def body(*refs):
        nci, nco = len(comm.ins), len(comm.out_shapes)
        c_in, c_out, send_sems, recv_sems = refs[:nci], refs[nci:nci + nco], refs[-2], refs[-1]
        comm.start(c_in, c_out, send_sems, recv_sems)
        comm.wait(c_in, c_out, send_sems, recv_sems)

    return pl.pallas_call(
        body, in_specs=[ANY] * len(comm.ins), out_specs=[ANY] * len(comm.out_shapes), out_shape=comm.out_shapes,
        scratch_shapes=[pltpu.SemaphoreType.DMA((comm.n_sems,))] * 2, name=name,
        compiler_params=pltpu.CompilerParams(has_side_effects=True),
    )(*comm.ins)


def _mm(a, b, *, m, n, k, name, tm=1024, tn=1024, tk=512, ta=False, tb=False, b_stk=None, b_s0=0,
        o_stk=None, o_s0=0, o_slots=None, o_buf=None, bias=None, res=None, out_dtype=F32):
    tm, tn, tk = _tile(m, tm), _tile(n, tn), _tile(k, tk)
    gi, gj, gk = m // tm, n // tn, k // tk
    a_spec = pl.BlockSpec((tk, tm), lambda j, i, kk: (kk, i)) if ta else pl.BlockSpec((tm, tk), lambda j, i, kk: (i, kk))
    if b_stk is None:
        b_spec = pl.BlockSpec((tn, tk), lambda j, i, kk: (j, kk)) if tb else pl.BlockSpec((tk, tn), lambda j, i, kk: (kk, j))
    elif tb:
        assert b_stk % tk == 0
        per = b_stk // tk
        b_spec = pl.BlockSpec((None, tn, tk), lambda j, i, kk: (b_s0 + kk // per, j, kk % per))
    else:
        assert b_stk % tn == 0
        per = b_stk // tn
        b_spec = pl.BlockSpec((None, tk, tn), lambda j, i, kk: (b_s0 + j // per, kk, j % per))
    ins, in_specs = [a, b], [a_spec, b_spec]
    if bias is not None:
        ins.append(bias)
        in_specs.append(pl.BlockSpec((1, tn), lambda j, i, kk: (0, j)))
    if res is not None:
        ins.append(res)
        in_specs.append(pl.BlockSpec((tm, tn), lambda j, i, kk: (i, j)))
    aliases = {}
    if o_stk is None:
        out_shape = jax.ShapeDtypeStruct((m, n), out_dtype)
        o_spec = pl.BlockSpec((tm, tn), lambda j, i, kk: (i, j))
    else:
        assert o_stk % tn == 0
        pero = o_stk // tn
        out_shape = jax.ShapeDtypeStruct((o_slots, m, o_stk), out_dtype)
        o_spec = pl.BlockSpec((None, tm, tn), lambda j, i, kk: (o_s0 + j // pero, i, j % pero))
        if o_buf is not None:
            aliases = {len(ins): 0}
            ins.append(o_buf)
            in_specs.append(ANY)
    has_bias, has_res, has_buf = bias is not None, res is not None, o_buf is not None
    dn = (((0 if ta else 1,), (1 if tb else 0,)), ((), ()))

    def body(*refs):
        a_ref, b_ref = refs[0], refs[1]
        pos = 2
        bias_ref = refs[pos] if has_bias else None
        pos += has_bias
        res_ref = refs[pos] if has_res else None
        pos += has_res + has_buf
        o_ref = refs[pos]
        p = lax.dot_general(a_ref[...].astype(BF16), b_ref[...].astype(BF16), dn, preferred_element_type=F32)

        def finish(acc):
            if has_bias:
                acc = acc + bias_ref[...]
            if has_res:
                acc = acc + res_ref[...]
            o_ref[...] = acc.astype(o_ref.dtype)

        if gk == 1:
            finish(p)
        else:
            acc_ref = refs[pos + 1]
            kk = pl.program_id(2)

            @pl.when(kk == 0)
            def _():
                acc_ref[...] = p

            @pl.when(kk > 0)
            def _():
                acc_ref[...] += p

            @pl.when(kk == gk - 1)
            def _():
                finish(acc_ref[...])

    return pl.pallas_call(
        body, grid=(gj, gi, gk), in_specs=in_specs, out_specs=o_spec, out_shape=out_shape,
        scratch_shapes=[pltpu.VMEM((tm, tn), F32)] if gk > 1 else [],
        input_output_aliases=aliases, name=name,
        compiler_params=_cp("parallel", "parallel", "arbitrary"),
    )(*ins)


def _rms_fwd(x, g, name):
    n, d = x.shape
    tm = _tile(n, 512)

    def body(x_ref, g_ref, h_ref):
        xv = x_ref[...]
        r = lax.rsqrt(jnp.mean(xv * xv, axis=-1, keepdims=True) + EPS)
        h_ref[...] = (xv * r * g_ref[...]).astype(h_ref.dtype)

    return pl.pallas_call(
        body, grid=(n // tm,),
        in_specs=[pl.BlockSpec((tm, d), lambda i: (i, 0)), pl.BlockSpec((1, d), lambda i: (0, 0))],
        out_specs=pl.BlockSpec((tm, d), lambda i: (i, 0)),
        out_shape=jax.ShapeDtypeStruct((n, d), BF16), name=name, compiler_params=_cp("arbitrary"),
    )(x, g)


def _rms_bwd(x, dh, g, dres, name, colsum=False):
    n, d = x.shape
    tm = _tile(n, 512)

    def body(x_ref, dh_ref, g_ref, dres_ref, dx_ref, dg_ref, *rest):
        i = pl.program_id(0)
        xv, dhv = x_ref[...], dh_ref[...]
        r = lax.rsqrt(jnp.mean(xv * xv, axis=-1, keepdims=True) + EPS)
        xn = xv * r
        dxn = dhv * g_ref[...]
        dx_ref[...] = dres_ref[...] + r * (dxn - xn * jnp.mean(dxn * xn, axis=-1, keepdims=True))
        dg = jnp.sum(dhv * xn, axis=0, keepdims=True)

        @pl.when(i == 0)
        def _():
            dg_ref[...] = jnp.zeros_like(dg_ref)
            if colsum:
                rest[0][...] = jnp.zeros_like(rest[0])

        dg_ref[...] += dg
        if colsum:
            rest[0][...] += jnp.sum(dres_ref[...], axis=0, keepdims=True)

    row = pl.BlockSpec((tm, d), lambda i: (i, 0))
    vec = pl.BlockSpec((1, d), lambda i: (0, 0))
    out_shape = [jax.ShapeDtypeStruct((n, d), F32), jax.ShapeDtypeStruct((1, d), F32)]
    out_specs = [row, vec]
    if colsum:
        out_shape.append(jax.ShapeDtypeStruct((1, d), F32))
        out_specs.append(vec)
    return pl.pallas_call(
        body, grid=(n // tm,), in_specs=[row, row, vec, row], out_specs=out_specs, out_shape=out_shape,
        name=name, compiler_params=_cp("arbitrary"),
    )(x, dh, g, dres)


def _loss(y, tgt, name):
    n, d = y.shape
    tm = _tile(n, 512)

    def body(y_ref, t_ref, dy_ref, l_ref):
        i = pl.program_id(0)
        e = y_ref[...] - t_ref[...]
        dy_ref[...] = e / d
        part = 0.5 * jnp.sum(jnp.mean(e * e, axis=-1, keepdims=True), axis=0, keepdims=True)

        @pl.when(i == 0)
        def _():
            l_ref[...] = jnp.zeros_like(l_ref)

        l_ref[...] += part

    row = pl.BlockSpec((tm, d), lambda i: (i, 0))
    return pl.pallas_call(
        body, grid=(n // tm,), in_specs=[row, row],
        out_specs=[row, pl.BlockSpec((1, 1), lambda i: (0, 0))],
        out_shape=[jax.ShapeDtypeStruct((n, d), F32), jax.ShapeDtypeStruct((1, 1), F32)],
        name=name, compiler_params=_cp("arbitrary"),
    )(y, tgt)


def _ffn_specs(n, f, tm, tc, seq):
    hb = FFN_HALO
    cur = pl.BlockSpec((tm, tc), lambda j, i: (i, j))
    prev = pl.BlockSpec((hb, tc), lambda j, i: (jnp.maximum(i * (tm // hb) - 1, 0), j))
    nxt = pl.BlockSpec((hb, tc), lambda j, i: (jnp.minimum((i + 1) * (tm // hb), n // hb - 1), j))
    taps = pl.BlockSpec((3, tc), lambda j, i: (0, j))
    vec = pl.BlockSpec((1, tc), lambda j, i: (0, j))
    return cur, prev, nxt, taps, vec


def _ffn_glu_down(uv, ug, wv, wg, bv, bg, w_down, res, seq, name, comm=None):
    n, f = uv.shape
    d = w_down.shape[1]
    tm, tc = _tile(seq, FFN_ROWS), _tile(f, FFN_COLS)
    tps = seq // tm
    nj = f // tc
    hb = FFN_HALO
    cur = pl.BlockSpec((tm, tc), lambda i, j: (i, j))
    prev = pl.BlockSpec((hb, tc), lambda i, j: (jnp.maximum(i * (tm // hb) - 1, 0), j))
    taps = pl.BlockSpec((3, tc), lambda i, j: (0, j))
    vec = pl.BlockSpec((1, tc), lambda i, j: (0, j))
    wblk = pl.BlockSpec((tc, d), lambda i, j: (j, 0))
    row = pl.BlockSpec((tm, d), lambda i, j: (i, 0))

    def body(uvp, uvc, ugp, ugc, wv_ref, wg_ref, bv_ref, bg_ref, wd_ref, res_ref, a_ref, x_ref, acc_ref):
        first = (pl.program_id(0) % tps) == 0
        j = pl.program_id(1)

        def conv(p_ref, c_ref, w_ref, b_ref):
            xs = jnp.concatenate([jnp.where(first, 0.0, p_ref[...]), c_ref[...]], axis=0)
            w = w_ref[...]
            y = w[2:3] * xs + w[1:2] * _roll(xs, 1) + w[0:1] * _roll(xs, 2)
            return y[FFN_HALO:] + b_ref[...]

        val = conv(uvp, uvc, wv_ref, bv_ref)
        gate = conv(ugp, ugc, wg_ref, bg_ref)
        a = (gate * _sig(gate) * val).astype(BF16)
        a_ref[...] = a
        p = jnp.dot(a, wd_ref[...], preferred_element_type=F32)

        @pl.when(j == 0)
        def _():
            acc_ref[...] = res_ref[...] + p

        @pl.when(j > 0)
        def _():
            acc_ref[...] += p

        @pl.when(j == nj - 1)
        def _():
            x_ref[...] = acc_ref[...]

    return _hosted(
        body, comm, grid=(n // tm, nj), in_specs=[prev, cur, prev, cur, taps, taps, vec, vec, wblk, row],
        out_specs=[cur, row], out_shape=[jax.ShapeDtypeStruct((n, f), BF16), jax.ShapeDtypeStruct((n, d), F32)],
        scratch_shapes=[pltpu.VMEM((tm, d), F32)], sem=("parallel", "arbitrary"), name=name,
        ins=(uv, uv, ug, ug, wv, wg, bv, bg, w_down, res))


def _ffn_glu_bwd(uv, ug, dx, w_down, wv, wg, bv, bg, seq, name, comm=None):
    n, f = uv.shape
    d = dx.shape[1]
    tm, tc = _tile(seq, FFN_ROWS), _tile(f, FFN_COLS)
    tps = seq // tm
    hb = FFN_HALO
    ext = tm + hb
    cur, prev, nxt, taps, vec = _ffn_specs(n, f, tm, tc, seq)
    dx_cur = pl.BlockSpec((tm, d), lambda j, i: (i, 0))
    dx_nxt = pl.BlockSpec((hb, d), lambda j, i: (jnp.minimum((i + 1) * (tm // hb), n // hb - 1), 0))
    wblk = pl.BlockSpec((tc, d), lambda j, i: (j, 0))

    def body(uvp, uvc, uvn, ugp, ugc, ugn, dx_c, dx_n, wd_ref, wv_ref, wg_ref, bv_ref, bg_ref,
             duv_ref, dug_ref, dwv_ref, dwg_ref, dbv_ref, dbg_ref):
        i = pl.program_id(1)
        first = (i % tps) == 0
        last = (i % tps) == tps - 1
        dx_e = jnp.concatenate([dx_c[...], jnp.where(last, 0.0, dx_n[...])], axis=0).astype(BF16)
        da_e = lax.dot_general(dx_e, wd_ref[...], _NT, preferred_element_type=F32)

        def taps3(p_ref, c_ref, n_ref):
            xs = jnp.concatenate([jnp.where(first, 0.0, p_ref[...]), c_ref[...], n_ref[...]], axis=0)
            return xs, _roll(xs, 1), _roll(xs, 2)

        xv, xg = taps3(uvp, uvc, uvn), taps3(ugp, ugc, ugn)
        wv_, wg_ = wv_ref[...], wg_ref[...]

        def conv(xs, w, b_ref):
            return (w[2:3] * xs[0] + w[1:2] * xs[1] + w[0:1] * xs[2])[hb:] + b_ref[...]

        val, gate = conv(xv, wv_, bv_ref), conv(xg, wg_, bg_ref)
        sg = _sig(gate)
        dval = da_e * (gate * sg)
        dgate = da_e * val * (sg * (1.0 + gate * (1.0 - sg)))

        def conv_t(dv, w):
            return (w[2:3] * dv + w[1:2] * _roll(dv, ext - 1) + w[0:1] * _roll(dv, ext - 2))[:tm]

        duv_ref[...] = conv_t(dval, wv_).astype(duv_ref.dtype)
        dug_ref[...] = conv_t(dgate, wg_).astype(dug_ref.dtype)

        def tap_grads(d_own, xs):
            return jnp.concatenate(
                [jnp.sum(d_own * xs[2 - kk][hb:hb + tm], axis=0, keepdims=True) for kk in range(3)], axis=0)

        dv_own, dg_own = dval[:tm], dgate[:tm]

        @pl.when(i == 0)
        def _():
            for r in (dwv_ref, dwg_ref, dbv_ref, dbg_ref):
                r[...] = jnp.zeros_like(r)

        dwv_ref[...] += tap_grads(dv_own, xv)
        dwg_ref[...] += tap_grads(dg_own, xg)
        dbv_ref[...] += jnp.sum(dv_own, axis=0, keepdims=True)
        dbg_ref[...] += jnp.sum(dg_own, axis=0, keepdims=True)

    return _hosted(
        body, comm, grid=(f // tc, n // tm),
        in_specs=[prev, cur, nxt, prev, cur, nxt, dx_cur, dx_nxt, wblk, taps, taps, vec, vec],
        out_specs=[cur, cur, taps, taps, vec, vec],
        out_shape=[jax.ShapeDtypeStruct((n, f), BF16), jax.ShapeDtypeStruct((n, f), BF16),
                   jax.ShapeDtypeStruct((3, f), F32), jax.ShapeDtypeStruct((3, f), F32),
                   jax.ShapeDtypeStruct((1, f), F32), jax.ShapeDtypeStruct((1, f), F32)],
        scratch_shapes=[], sem=("parallel", "arbitrary"), name=name,
        ins=(uv, uv, uv, ug, ug, ug, dx, dx, w_down, wv, wg, bv, bg))


def _conf_specs(n, d, tm):
    hb = CONV_HALO
    cur = pl.BlockSpec((tm, d), lambda i: (i, 0))
    prev = pl.BlockSpec((hb, d), lambda i: (jnp.maximum(i * (tm // hb) - 1, 0), 0))
    nxt = pl.BlockSpec((hb, d), lambda i: (jnp.minimum((i + 1) * (tm // hb), n // hb - 1), 0))
    taps = pl.BlockSpec((CONV_HALO, d), lambda i: (0, 0))
    vec = pl.BlockSpec((1, d), lambda i: (0, 0))
    return cur, prev, nxt, taps, vec


def _conf_fwd(pa, pg, w, wb, lng, lnb, seq, name):
    n, d = pa.shape
    tm = _tile(seq, 256)
    tps = seq // tm
    hb = CONV_HALO
    cur, prev, _, taps, vec = _conf_specs(n, d, tm)

    def body(pap, pac, pgp, pgc, w_ref, wb_ref, lng_ref, lnb_ref, u_ref, s_ref, rot_ref):
        first = (pl.program_id(0) % tps) == 0
        a = jnp.concatenate([jnp.where(first, 0.0, pap[...]), pac[...]], axis=0)
        g = jnp.concatenate([jnp.where(first, 0.0, pgp[...]), pgc[...]], axis=0)
        z = a * _sig(g)
        for b in range(SUBLANES):
            rot_ref[b] = _roll(z, b)
        _conv_taps(rot_ref, w_ref, u_ref, tm, lambda sh: hb - (sh // SUBLANES) * SUBLANES)
        u = u_ref[...] + wb_ref[...]
        mu = jnp.mean(u, axis=-1, keepdims=True)
        uc = u - mu
        var = jnp.mean(uc * uc, axis=-1, keepdims=True)
        ul = uc * lax.rsqrt(var + EPS) * lng_ref[...] + lnb_ref[...]
        u_ref[...] = u
        s_ref[...] = (ul * _sig(ul)).astype(s_ref.dtype)

    return pl.pallas_call(
        body, grid=(n // tm,), in_specs=[prev, cur, prev, cur, taps, vec, vec, vec],
        out_specs=[cur, cur],
        out_shape=[jax.ShapeDtypeStruct((n, d), F32), jax.ShapeDtypeStruct((n, d), BF16)],
        scratch_shapes=[pltpu.VMEM((SUBLANES, tm + hb, d), F32)],
        name=name, compiler_params=_cp("arbitrary"),
    )(pa, pa, pg, pg, w, wb, lng, lnb)


def _conf_bwd(u, ds, pa, pg, w, lng, lnb, seq, name, comm=None):
    n, d = u.shape
    tm = _tile(seq, 256)
    tps = seq // tm
    hb = CONV_HALO
    ext = tm + hb
    cur, prev, nxt, taps, vec = _conf_specs(n, d, tm)

    def body(uc_ref, un_ref, dsc_ref, dsn_ref, pap, pac, pgp, pgc, w_ref, lng_ref, lnb_ref,
             dpa_ref, dpg_ref, dw_ref, dwb_ref, dlng_ref, dlnb_ref, dba_ref, dbg_ref, rotz_ref, rotd_ref, dz_ref):
        i = pl.program_id(0)
        first = (i % tps) == 0
        last = (i % tps) == tps - 1

        @pl.when(i == 0)
        def _():
            for r in (dw_ref, dwb_ref, dlng_ref, dlnb_ref, dba_ref, dbg_ref):
                r[...] = jnp.zeros_like(r)

        ue = jnp.concatenate([uc_ref[...], un_ref[...]], axis=0)
        dse = jnp.concatenate([dsc_ref[...], jnp.where(last, 0.0, dsn_ref[...])], axis=0)
        mu = jnp.mean(ue, axis=-1, keepdims=True)
        cen = ue - mu
        r = lax.rsqrt(jnp.mean(cen * cen, axis=-1, keepdims=True) + EPS)
        xn = cen * r
        ul = xn * lng_ref[...] + lnb_ref[...]
        sg = _sig(ul)
        dul = dse * (sg * (1.0 + ul * (1.0 - sg)))
        dun = dul * lng_ref[...]
        du = r * (dun - jnp.mean(dun, axis=-1, keepdims=True) - xn * jnp.mean(dun * xn, axis=-1, keepdims=True))
        dlng_ref[...] += jnp.sum((dul * xn)[:tm], axis=0, keepdims=True)
        dlnb_ref[...] += jnp.sum(dul[:tm], axis=0, keepdims=True)
        dwb_ref[...] += jnp.sum(du[:tm], axis=0, keepdims=True)
        for b in range(SUBLANES):
            rotd_ref[b] = _roll(du, ext - b)
        _conv_taps(rotd_ref, w_ref, dz_ref, tm, lambda sh: (sh // SUBLANES) * SUBLANES)
        dz = dz_ref[...]

        a = jnp.concatenate([jnp.where(first, 0.0, pap[...]), pac[...]], axis=0)
        g = jnp.concatenate([jnp.where(first, 0.0, pgp[...]), pgc[...]], axis=0)
        sgg = _sig(g)
        z = a * sgg
        for b in range(SUBLANES):
            rotz_ref[b] = _roll(z, b)
        _tap_grads(rotz_ref, rotd_ref, dw_ref, tm, hb)

        a_c, sg_c = a[hb:], sgg[hb:]
        da = dz * sg_c
        dg = dz * a_c * sg_c * (1.0 - sg_c)
        dpa_ref[...] = da.astype(dpa_ref.dtype)
        dpg_ref[...] = dg.astype(dpg_ref.dtype)
        dba_ref[...] += jnp.sum(da, axis=0, keepdims=True)
        dbg_ref[...] += jnp.sum(dg, axis=0, keepdims=True)

    vshape = jax.ShapeDtypeStruct((1, d), F32)
    return _hosted(
        body, comm, grid=(n // tm,),
        in_specs=[cur, nxt, cur, nxt, prev, cur, prev, cur, taps, vec, vec],
        out_specs=[cur, cur, taps, vec, vec, vec, vec, vec],
        out_shape=[jax.ShapeDtypeStruct((n, d), BF16), jax.ShapeDtypeStruct((n, d), BF16),
                   jax.ShapeDtypeStruct((CONV_HALO, d), F32), vshape, vshape, vshape, vshape, vshape],
        scratch_shapes=[pltpu.VMEM((SUBLANES, ext, d), F32), pltpu.VMEM((SUBLANES, ext, d), F32),
                        pltpu.VMEM((tm, d), F32)],
        sem=("arbitrary",), name=name, ins=(u, u, ds, ds, pa, pa, pg, pg, w, lng, lnb))


def _pool_specs(n, d, tm, gd):
    hb = POOL_HALO
    cur = pl.BlockSpec((tm, d), lambda i: (i, 0))
    prev = pl.BlockSpec((hb, d), lambda i: (jnp.maximum(i * (tm // hb) - 1, 0), 0))
    nxt = pl.BlockSpec((hb, d), lambda i: (jnp.minimum((i + 1) * (tm // hb), n // hb - 1), 0))
    wsp = pl.BlockSpec((len(POOL_WINDOWS), gd, gd), lambda i: (0, 0, 0))
    vec = pl.BlockSpec((1, d), lambda i: (0, 0))
    return cur, prev, nxt, wsp, vec


def _pool_fwd(x, g, w, b, sc, seq, name):
    n, d = x.shape
    gd = d // len(POOL_WINDOWS)
    tm = _tile(seq, 256)
    tps = seq // tm
    hb = POOL_HALO
    cur, prev, _, wsp, vec = _pool_specs(n, d, tm, gd)

    def body(xp, xc, g_ref, w_ref, b_ref, sc_ref, x1_ref, p_ref):
        i = pl.program_id(0)
        first = (i % tps) == 0
        xe = jnp.concatenate([jnp.where(first, 0.0, xp[...]), xc[...]], axis=0)
        r = lax.rsqrt(jnp.mean(xe * xe, axis=-1, keepdims=True) + EPS)
        h = xe * r * g_ref[...]
        t = ((i % tps) * tm + lax.broadcasted_iota(jnp.int32, (tm, 1), 0) + 1).astype(F32)
        ys = []
        for gi, win in enumerate(POOL_WINDOWS):
            hg = h[:, gi * gd:(gi + 1) * gd]
            s, sh = hg, 1
            while sh < win:
                s = s + _roll(s, sh)
                sh *= 2
            p = (s[hb:] / jnp.minimum(t, float(win)) - hg[hb:]).astype(BF16)
            p_ref[:, gi * gd:(gi + 1) * gd] = p
            ys.append(jnp.dot(p, w_ref[gi], preferred_element_type=F32))
        y = jnp.concatenate(ys, axis=1) + b_ref[...]
        x1_ref[...] = xc[...] + y * sc_ref[...]

    return pl.pallas_call(
        body, grid=(n // tm,), in_specs=[prev, cur, vec, wsp, vec, vec], out_specs=[cur, cur],
        out_shape=[jax.ShapeDtypeStruct((n, d), F32), jax.ShapeDtypeStruct((n, d), BF16)],
        name=name, compiler_params=_cp("arbitrary"),
    )(x, x, g, w, b, sc)


def _pool_bwd(dx1, x, p, g, w, b, sc, seq, name):
    n, d = x.shape
    ng = len(POOL_WINDOWS)
    gd = d // ng
    tm = _tile(seq, 256)
    tps = seq // tm
    hb = POOL_HALO
    ext = tm + hb
    cur, _, nxt, wsp, vec = _pool_specs(n, d, tm, gd)

    def body(dc_ref, dn_ref, x_ref, p_ref, g_ref, w_ref, b_ref, sc_ref, dx_ref, dg_ref, dw_ref, db_ref, dsc_ref):
        i = pl.program_id(0)
        last = (i % tps) == tps - 1

        @pl.when(i == 0)
        def _():
            for r_ in (dg_ref, dw_ref, db_ref, dsc_ref):
                r_[...] = jnp.zeros_like(r_)

        dxc = dc_ref[...]
        dxe = jnp.concatenate([dxc, jnp.where(last, 0.0, dn_ref[...])], axis=0)
        dyg = dxe * sc_ref[...]
        t = ((i % tps) * tm + lax.broadcasted_iota(jnp.int32, (ext, 1), 0) + 1).astype(F32)
        dhs = []
        for gi, win in enumerate(POOL_WINDOWS):
            sl = slice(gi * gd, (gi + 1) * gd)
            dyb = dyg[:, sl].astype(BF16)
            wg = w_ref[gi]
            dp = lax.dot_general(dyb, wg, (((1,), (1,)), ((), ())), preferred_element_type=F32)
            s, sh = dp / jnp.minimum(t, float(win)), 1
            while sh < win:
                s = s + _roll(s, ext - sh)
                sh *= 2
            dhs.append((s - dp)[:tm])
            pg = p_ref[:, sl]
            dw_ref[gi] += lax.dot_general(pg, dyb[:tm], (((0,), (0,)), ((), ())), preferred_element_type=F32)
            ypre = jnp.dot(pg, wg, preferred_element_type=F32) + b_ref[:, sl]
            dsc_ref[:, sl] += jnp.sum(dxc[:, sl] * ypre, axis=0, keepdims=True)
            db_ref[:, sl] += jnp.sum(dyg[:tm, sl], axis=0, keepdims=True)
        dh = jnp.concatenate(dhs, axis=1)
        xv = x_ref[...]
        r = lax.rsqrt(jnp.mean(xv * xv, axis=-1, keepdims=True) + EPS)
        xn = xv * r
        dxn = dh * g_ref[...]
        dx_ref[...] = dxc + r * (dxn - xn * jnp.mean(dxn * xn, axis=-1, keepdims=True))
        dg_ref[...] += jnp.sum(dh * xn, axis=0, keepdims=True)

    vshape = jax.ShapeDtypeStruct((1, d), F32)
    return pl.pallas_call(
        body, grid=(n // tm,), in_specs=[cur, nxt, cur, cur, vec, wsp, vec, vec],
        out_specs=[cur, vec, wsp, vec, vec],
        out_shape=[jax.ShapeDtypeStruct((n, d), F32), vshape, jax.ShapeDtypeStruct((ng, gd, gd), F32), vshape, vshape],
        name=name, compiler_params=_cp("arbitrary"),
    )(dx1, dx1, x, p, g, w, b, sc)


def _head_maps(d):
    hd = lax.broadcasted_iota(jnp.int32, (d, LANES), 0) // HEAD_DIM
    col = lax.broadcasted_iota(jnp.int32, (d, LANES), 1)
    gm = (hd == col).astype(BF16)
    hd_t = lax.broadcasted_iota(jnp.int32, (LANES, d), 1) // HEAD_DIM
    row = lax.broadcasted_iota(jnp.int32, (LANES, d), 0)
    gt = (hd_t == row).astype(BF16)
    return gm, gt


def _dot_split(v, onehot):
    hi = v.astype(BF16)
    lo = (v - hi.astype(F32)).astype(BF16)
    return jnp.dot(hi, onehot, preferred_element_type=F32) + jnp.dot(lo, onehot, preferred_element_type=F32)


def _bias_placement(nh):
    pq = np.zeros((3 * LANES, nh * HEAD_DIM), np.float32)
    pk = np.zeros((3 * LANES, nh * HEAD_DIM), np.float32)
    oq = np.zeros((1, nh * HEAD_DIM), np.float32)
    ok = np.zeros((1, nh * HEAD_DIM), np.float32)
    for h in range(nh):
        for piece in range(3):
            pq[piece * LANES + h, h * HEAD_DIM + piece] = 1.0
            pk[piece * LANES + h, h * HEAD_DIM + 3 + piece] = -1.0
            oq[0, h * HEAD_DIM + 3 + piece] = 1.0
            ok[0, h * HEAD_DIM + piece] = 1.0
    return jnp.asarray(pq, BF16), jnp.asarray(pk, BF16), jnp.asarray(oq), jnp.asarray(ok)


def _fox_prep_fwd(qkv, fl, bf, qg, kg, seq, name):
    n, d3 = qkv.shape
    d = d3 // 3
    nh = d // HEAD_DIM
    tm = _tile(seq, 256)
    tps = seq // tm
    scale = 1.0 / math.sqrt(HEAD_DIM)
    pq, pk, oq, ok = _bias_placement(nh)

    def body(qkv_ref, fl_ref, bf_ref, qg_ref, kg_ref, pq_ref, pk_ref, oq_ref, ok_ref, q_ref, k_ref, v_ref, carry):
        first = (pl.program_id(0) % tps) == 0
        gm, gt = _head_maps(d)

        def head_norm(xr, gain):
            r = lax.rsqrt(_dot_split(xr * xr, gm) / HEAD_DIM + EPS)
            return xr * _dot_split(r, gt) * gain

        qs = (head_norm(qkv_ref[:, :d], qg_ref[...]).astype(BF16).astype(F32) * scale).astype(BF16)
        kn = head_norm(qkv_ref[:, d:2 * d], kg_ref[...]).astype(BF16)
        v_ref[...] = qkv_ref[:, 2 * d:].astype(BF16)
        z = fl_ref[...] + bf_ref[...]
        logf = jnp.minimum(z, 0.0) - jnp.log1p(jnp.exp(-jnp.abs(z)))
        tri = (lax.broadcasted_iota(jnp.int32, (tm, tm), 0) >= lax.broadcasted_iota(jnp.int32, (tm, tm), 1)).astype(F32)

        @pl.when(first)
        def _():
            carry[...] = jnp.zeros_like(carry)

        c = jnp.dot(tri, logf, precision=HI, preferred_element_type=F32) + carry[...]
        carry[...] = c[tm - 1:tm, :]
        c1 = c.astype(BF16)
        r1 = c - c1.astype(F32)
        c2 = r1.astype(BF16)
        c3 = (r1 - c2.astype(F32)).astype(BF16)
        pieces = jnp.concatenate([c1, c2, c3], axis=1)
        eq = (jnp.dot(pieces, pq_ref[...], preferred_element_type=F32) + oq_ref[...]).astype(BF16)
        ek = (jnp.dot(pieces, pk_ref[...], preferred_element_type=F32) + ok_ref[...]).astype(BF16)
        for h in range(nh):
            lo, hi = h * HEAD_DIM, (h + 1) * HEAD_DIM
            q_ref[:, 2 * lo:2 * lo + HEAD_DIM] = qs[:, lo:hi]
            q_ref[:, 2 * lo + HEAD_DIM:2 * hi] = eq[:, lo:hi]
            k_ref[:, 2 * lo:2 * lo + HEAD_DIM] = kn[:, lo:hi]
            k_ref[:, 2 * lo + HEAD_DIM:2 * hi] = ek[:, lo:hi]

    row = lambda w: pl.BlockSpec((tm, w), lambda i: (i, 0))
    vec = lambda w: pl.BlockSpec((1, w), lambda i: (0, 0))
    full = lambda a: pl.BlockSpec(a.shape, lambda i: (0, 0))
    return pl.pallas_call(
        body, grid=(n // tm,),
        in_specs=[row(d3), row(LANES), vec(LANES), vec(d), vec(d), full(pq), full(pk), full(oq), full(ok)],
        out_specs=[row(2 * d), row(2 * d), row(d)],
        out_shape=[jax.ShapeDtypeStruct((n, 2 * d), BF16)] * 2 + [jax.ShapeDtypeStruct((n, d), BF16)],
        scratch_shapes=[pltpu.VMEM((1, LANES), F32)], name=name, compiler_params=_cp("arbitrary"),
    )(qkv, fl, bf, qg, kg, pq, pk, oq, ok)


def _fox_prep_bwd(qkv, dq, dk, dv, dc1, dc2, fl, bf, qg, kg, seq, name, comm=None):
    n, d3 = qkv.shape
    d = d3 // 3
    tm = _tile(seq, 256)
    tps = seq // tm
    nt = n // tm

    def body(qkv_ref, dq_ref, dk_ref, dv_ref, dc1_ref, dc2_ref, fl_ref, bf_ref, qg_ref, kg_ref,
             dqkv_ref, dfl_ref, dqg_ref, dkg_ref, dbf_ref, carry):
        i = pl.program_id(0)
        tile = nt - 1 - i
        last = (tile % tps) == tps - 1
        gm, gt = _head_maps(d)

        @pl.when(i == 0)
        def _():
            for r_ in (dqg_ref, dkg_ref, dbf_ref):
                r_[...] = jnp.zeros_like(r_)

        @pl.when(last)
        def _():
            carry[...] = jnp.zeros_like(carry)

        def head_norm_bwd(xr, dy, gain, dgain_ref):
            rf = _dot_split(lax.rsqrt(_dot_split(xr * xr, gm) / HEAD_DIM + EPS), gt)
            xn = xr * rf
            dgain_ref[...] += jnp.sum(dy * xn, axis=0, keepdims=True)
            dyg = dy * gain
            mean = _dot_split(dyg * xn, gm) / HEAD_DIM
            return rf * (dyg - xn * _dot_split(mean, gt))

        dqkv_ref[:, :d] = head_norm_bwd(qkv_ref[:, :d], dq_ref[...], qg_ref[...], dqg_ref).astype(BF16)
        dqkv_ref[:, d:2 * d] = head_norm_bwd(qkv_ref[:, d:2 * d], dk_ref[...], kg_ref[...], dkg_ref).astype(BF16)
        dqkv_ref[:, 2 * d:] = dv_ref[...].astype(BF16)

        dc = dc1_ref[...] + dc2_ref[...]
        tri = (lax.broadcasted_iota(jnp.int32, (tm, tm), 0) <= lax.broadcasted_iota(jnp.int32, (tm, tm), 1)).astype(F32)
        dlog = jnp.dot(tri, dc, precision=HI, preferred_element_type=F32) + carry[...]
        carry[...] = dlog[0:1, :]
        dfl = dlog * (1.0 - _sig(fl_ref[...] + bf_ref[...]))
        dfl_ref[...] = dfl.astype(BF16)
        dbf_ref[...] += jnp.sum(dfl, axis=0, keepdims=True)

    row = lambda w: pl.BlockSpec((tm, w), lambda i: (nt - 1 - i, 0))
    vec = lambda w: pl.BlockSpec((1, w), lambda i: (0, 0))
    return _hosted(
        body, comm, grid=(nt,),
        in_specs=[row(d3), row(d), row(d), row(d), row(LANES), row(LANES), row(LANES), vec(LANES), vec(d), vec(d)],
        out_specs=[row(d3), row(LANES), vec(d), vec(d), vec(LANES)],
        out_shape=[jax.ShapeDtypeStruct((n, d3), BF16), jax.ShapeDtypeStruct((n, LANES), BF16),
                   jax.ShapeDtypeStruct((1, d), F32), jax.ShapeDtypeStruct((1, d), F32),
                   jax.ShapeDtypeStruct((1, LANES), F32)],
        scratch_shapes=[pltpu.VMEM((1, LANES), F32)], sem=("arbitrary",), name=name,
        ins=(qkv, dq, dk, dv, dc1, dc2, fl, bf, qg, kg))


def _attn_specs(bsz, seq, t):
    nb = seq // t
    blk = lambda w: pl.BlockSpec((t, w), lambda b, h, i: (b * nb + i, h))
    full = lambda w: pl.BlockSpec((seq, w), lambda b, h, i: (b, h))
    col = pl.BlockSpec((None, None, t, 2), lambda b, h, i: (b, h, i, 0))
    rows = pl.BlockSpec((None, None, nb, 2, t), lambda b, h, i: (b, h, 0, 0, 0))
    return nb, blk, full, col, rows


_NT = (((1,), (1,)), ((), ()))
ATTN_TILE = 512


def _head_lanes(t, hh):
    lane = lax.broadcasted_iota(jnp.int32, (t, LANES), 1)
    return (lane < HEAD_DIM) if hh == 0 else (lane >= HEAD_DIM)


def _flash_fwd(qa, ka, v, bsz, seq, name):
    n, d = v.shape
    hp = d // LANES
    t = _tile(seq, ATTN_TILE)
    nb, blk, full, col, _ = _attn_specs(bsz, seq, t)

    def body(q_ref, k_ref, v_ref, o_ref, lse_ref):
        i = pl.program_id(2)
        causal = lax.broadcasted_iota(jnp.int32, (t, t), 0) >= lax.broadcasted_iota(jnp.int32, (t, t), 1)

        def block(j, carry, masked):
            rs = pl.ds(pl.multiple_of(j * t, t), t)
            vj = v_ref[rs, :]
            out = []
            for hh in range(2):
                m, l, acc = carry[hh]
                hs = slice(hh * LANES, (hh + 1) * LANES)
                sc = lax.dot_general(q_ref[:, hs], k_ref[rs, hs], _NT, preferred_element_type=F32)
                if masked:
                    sc = jnp.where(causal, sc, NEG)
                mn = jnp.maximum(m, jnp.max(sc, axis=-1, keepdims=True))
                p = jnp.exp(sc - mn)
                al = jnp.exp(m - mn)
                l = al * l + jnp.sum(p, axis=-1, keepdims=True)
                acc = al * acc + jnp.dot(p.astype(BF16), vj, preferred_element_type=F32)
                out.append((mn, l, acc))
            return tuple(out)

        init = tuple((jnp.full((t, 1), NEG, F32), jnp.zeros((t, 1), F32), jnp.zeros((t, LANES), F32))
                     for _ in range(2))
        carry = lax.fori_loop(0, i, lambda j, c: block(j, c, False), init)
        (m0, l0, a0), (m1, l1, a1) = block(i, carry, True)
        o_ref[...] = jnp.where(_head_lanes(t, 0), a0 / l0, a1 / l1)
        lse_ref[:, 0:1] = m0 + jnp.log(l0)
        lse_ref[:, 1:2] = m1 + jnp.log(l1)

    return pl.pallas_call(
        body, grid=(bsz, hp, nb), in_specs=[blk(2 * LANES), full(2 * LANES), full(LANES)],
        out_specs=[blk(LANES), col],
        out_shape=[jax.ShapeDtypeStruct((n, d), F32), jax.ShapeDtypeStruct((bsz, hp, seq, 2), F32)],
        name=name, compiler_params=_cp("parallel", "parallel", "arbitrary"),
    )(qa, ka, v)


def _flash_bwd(qa, ka, v, do, o, lse_row, bsz, seq, name):
    n, d = v.shape
    hp = d // LANES
    t = _tile(seq, ATTN_TILE)
    nb, blk, full, col, rows = _attn_specs(bsz, seq, t)
    scale = 1.0 / math.sqrt(HEAD_DIM)
    tn_ = (((0,), (0,)), ((), ()))

    def body(k_ref, v_ref, q_ref, do_ref, o_ref, lse_ref, dq_ref, dcc_ref, dk_ref, dv_ref, dck_ref, dqa, dl):
        j = pl.program_id(2)
        causal = lax.broadcasted_iota(jnp.int32, (t, t), 1) >= lax.broadcasted_iota(jnp.int32, (t, t), 0)
        heads = [_head_lanes(t, 0), _head_lanes(t, 1)]

        @pl.when(j == 0)
        def _():
            dqa[...] = jnp.zeros_like(dqa)
            ones = jnp.ones((8, LANES), F32)
            for ib in range(nb):
                rs = slice(ib * t, (ib + 1) * t)
                prod = do_ref[rs, :] * o_ref[rs, :]
                for hh in range(2):
                    dl[ib, hh:hh + 1, :] = lax.dot_general(ones, jnp.where(heads[hh], prod, 0.0), _NT, precision=HI,
                                                           preferred_element_type=F32)[0:1]

        vj = v_ref[...]

        def block(i, carry, masked):
            rs = pl.ds(pl.multiple_of(i * t, t), t)
            doi = do_ref[rs, :]
            dks, dvp = list(carry[:2]), carry[2]
            for hh in range(2):
                hs = slice(hh * LANES, (hh + 1) * LANES)
                kh, qi = k_ref[:, hs], q_ref[rs, hs]
                dom = jnp.where(heads[hh], doi, 0.0).astype(BF16)
                st = lax.dot_general(kh, qi, _NT, preferred_element_type=F32)
                if masked:
                    st = jnp.where(causal, st, NEG)
                pt = jnp.exp(st - lse_ref[i, hh:hh + 1, :])
                dvp = dvp + jnp.dot(pt.astype(BF16), dom, preferred_element_type=F32)
                dpt = lax.dot_general(vj, dom, _NT, preferred_element_type=F32)
                dsb = (pt * (dpt - dl[i, hh:hh + 1, :])).astype(BF16)
                dks[hh] = dks[hh] + jnp.dot(dsb, qi, preferred_element_type=F32)
                dqa[rs, hs] += lax.dot_general(dsb, kh, tn_, preferred_element_type=F32)
            return dks[0], dks[1], dvp

        zero = jnp.zeros((t, LANES), F32)
        carry = block(j, (zero, zero, zero), True)
        dk0, dk1, dvp = lax.fori_loop(j + 1, nb, lambda i, c: block(i, c, False), carry)
        dk_ref[...] = jnp.where(heads[0], dk0, pltpu.roll(dk1, HEAD_DIM, 1))
        dv_ref[...] = dvp
        dck_ref[:, 0:1] = -dk0[:, HEAD_DIM + 3:HEAD_DIM + 4]
        dck_ref[:, 1:2] = -dk1[:, HEAD_DIM + 3:HEAD_DIM + 4]

        @pl.when(j == nb - 1)
        def _():
            first = lax.broadcasted_iota(jnp.int32, (seq, LANES), 1) < HEAD_DIM
            dq_ref[...] = jnp.where(first, dqa[:, :LANES], pltpu.roll(dqa[:, LANES:], HEAD_DIM, 1)) * scale
            for hh in range(2):
                lo = hh * LANES + HEAD_DIM
                dcc_ref[:, hh:hh + 1] = dqa[:, lo:lo + 1]

    whole_col = pl.BlockSpec((None, None, seq, 2), lambda b, h, i: (b, h, 0, 0))
    cshape = jax.ShapeDtypeStruct((bsz, hp, seq, 2), F32)
    nd = jax.ShapeDtypeStruct((n, d), F32)
    return pl.pallas_call(
        body, grid=(bsz, hp, nb),
        in_specs=[blk(2 * LANES), blk(LANES), full(2 * LANES), full(LANES), full(LANES), rows],
        out_specs=[full(LANES), whole_col, blk(LANES), blk(LANES), col],
        out_shape=[nd, cshape, nd, nd, cshape],
        scratch_shapes=[pltpu.VMEM((seq, 2 * LANES), F32), pltpu.VMEM((nb, 2, t), F32)],
        name=name, compiler_params=_cp("parallel", "parallel", "arbitrary"),
    )(ka, v, qa, do, o, lse_row)


def _adamw(w, g, m, v, name):
    r, c = w.shape
    tr = r
    for cand in (512, 256, 128, 64, 32, 16, 8):
        if r % cand == 0 and r > cand and cand * c * 4 <= 4 * 1024 * 1024:
            tr = cand
            break

    def body(w_ref, g_ref, m_ref, v_ref, d_ref, m2_ref, v2_ref):
        gv = g_ref[...]
        m2 = ADAM_B1 * m_ref[...] + (1.0 - ADAM_B1) * gv
        v2 = ADAM_B2 * v_ref[...] + (1.0 - ADAM_B2) * jnp.square(gv)
        m_hat = m2 / (1.0 - ADAM_B1 ** ADAM_STEP)
        v_hat = v2 / (1.0 - ADAM_B2 ** ADAM_STEP)
        d_ref[...] = -ADAM_LR * (m_hat / (jnp.sqrt(v_hat) + ADAM_EPS) + ADAM_WD * w_ref[...])
        m2_ref[...] = m2
        v2_ref[...] = v2

    blk = pl.BlockSpec((tr, c), lambda i: (i, 0))
    shp = jax.ShapeDtypeStruct((r, c), F32)
    return pl.pallas_call(
        body, grid=(r // tr,), in_specs=[blk] * 4, out_specs=[blk] * 3, out_shape=[shp] * 3,
        name=name, compiler_params=_cp("parallel"),
    )(w, g, m, v)


def _adamw_nd(w, g, m, v, name):
    shape = w.shape
    two = (math.prod(shape[:-1]), shape[-1])
    d_, m_, v_ = _adamw(w.reshape(two), g.reshape(two), m.reshape(two), v.reshape(two), name)
    return g.reshape(shape), d_.reshape(shape), m_.reshape(shape), v_.reshape(shape)


def _place():
    x, y, c = lax.axis_index("x"), lax.axis_index("y"), lax.axis_index("c")
    chips = [(1 - x, y), (x, 1 - y), (1 - x, 1 - y)]
    return x, y, c, chips


def _gather_chips(shards):
    nt = len(shards)
    halves = [s.shape[0] // 2 for s in shards]

    def copies(ins, outs, send_sems, recv_sems):
        x, y, c, chips = _place()
        cps = []
        for t in range(nt):
            rows = pl.ds(c * halves[t], halves[t])
            for jj, (cx, cy) in enumerate(chips):
                cps.append(pltpu.make_async_remote_copy(
                    src_ref=ins[t].at[rows, :], dst_ref=outs[t].at[2 * x + y, rows, :], send_sem=send_sems.at[3 * t + jj],
                    recv_sem=recv_sems.at[3 * t + jj], device_id=(cx, cy, c), device_id_type=MESH))
        return cps

    def start(*refs):
        for cp in copies(*refs):
            cp.start()

    def wait(*refs):
        for cp in copies(*refs):
            cp.wait()

    return _Comm(shards, [jax.ShapeDtypeStruct((N_CHIPS,) + s.shape, s.dtype) for s in shards], 3 * nt, start, wait)


def _gather_sibling(shards, bufs, name):
    nt = len(shards)
    halves = [s.shape[0] // 2 for s in shards]

    def body(*refs):
        ins, outs = refs[:nt], refs[2 * nt:3 * nt]
        send_sems, recv_sems = refs[3 * nt:]
        x, y, c, chips = _place()
        sibling = (x, y, 1 - c)

        def copy(t, jj, hf):
            cx, cy = chips[jj]
            region = outs[t].at[2 * cx + cy, pl.ds(hf * halves[t], halves[t]), :]
            return pltpu.make_async_remote_copy(src_ref=region, dst_ref=region, send_sem=send_sems.at[4 * t + jj],
                                                recv_sem=recv_sems.at[4 * t + jj], device_id=sibling,
                                                device_id_type=MESH)

        def own(t):
            return pltpu.make_async_remote_copy(src_ref=ins[t], dst_ref=outs[t].at[2 * x + y],
                                                send_sem=send_sems.at[4 * t + 3], recv_sem=recv_sems.at[4 * t + 3],
                                                device_id=sibling, device_id_type=MESH)

        sends = [copy(t, jj, c) for t in range(nt) for jj in range(3)] + [own(t) for t in range(nt)]
        for cp in sends:
            cp.start()
        for t in range(nt):
            for jj in range(3):
                copy(t, jj, 1 - c).wait_recv()
            own(t).wait_recv()
        for cp in sends:
            cp.wait_send()

    return pl.pallas_call(
        body, in_specs=[ANY] * (2 * nt), out_specs=[ANY] * nt,
        out_shape=[jax.ShapeDtypeStruct(b.shape, b.dtype) for b in bufs],
        scratch_shapes=[pltpu.SemaphoreType.DMA((4 * nt,)), pltpu.SemaphoreType.DMA((4 * nt,))],
        input_output_aliases={nt + t: t for t in range(nt)}, name=name,
        compiler_params=pltpu.CompilerParams(has_side_effects=True),
    )(*shards, *bufs)


def _send_sibling_halves(grads, name):
    nt = len(grads)
    halves = [g.shape[1] // 2 for g in grads]

    def body(*refs):
        ins, outs = refs[:nt], refs[nt:2 * nt]
        send_sems, recv_sems = refs[2 * nt:]
        x, y, c, _ = _place()
        cps = []
        for t in range(nt):
            src = ins[t].at[:, pl.ds((1 - c) * halves[t], halves[t]), :]
            cps.append(pltpu.make_async_remote_copy(src_ref=src, dst_ref=outs[t], send_sem=send_sems.at[t],
                                                    recv_sem=recv_sems.at[t], device_id=(x, y, 1 - c),
                                                    device_id_type=MESH))
        for cp in cps:
            cp.start()
        for cp in cps:
            cp.wait()

    return pl.pallas_call(
        body, in_specs=[ANY] * nt, out_specs=[ANY] * nt,
        out_shape=[jax.ShapeDtypeStruct((N_CHIPS, h, g.shape[2]), g.dtype) for g, h in zip(grads, halves)],
        scratch_shapes=[pltpu.SemaphoreType.DMA((nt,)), pltpu.SemaphoreType.DMA((nt,))],
        name=name, compiler_params=pltpu.CompilerParams(has_side_effects=True),
    )(*grads)


def _add_sibling(g, r1, c_idx, out_dtype, name):
    s, r, w = g.shape
    rh = r // 2
    tr = rh
    for cand in (512, 256, 128, 64, 32, 16, 8):
        if rh % cand == 0 and cand * w * 4 <= 4 * 1024 * 1024:
            tr = cand
            break
    per = rh // tr

    def body(c_ref, g_ref, r_ref, o_ref):
        o_ref[...] = (g_ref[...] + r_ref[...]).astype(o_ref.dtype)

    return pl.pallas_call(
        body,
        grid_spec=pltpu.PrefetchScalarGridSpec(
            num_scalar_prefetch=1, grid=(s, per),
            in_specs=[pl.BlockSpec((None, tr, w), lambda a, b, c_ref: (a, c_ref[0] * per + b, 0)),
                      pl.BlockSpec((None, tr, w), lambda a, b, c_ref: (a, b, 0))],
            out_specs=pl.BlockSpec((None, tr, w), lambda a, b, c_ref: (a, b, 0))),
        out_shape=jax.ShapeDtypeStruct((s, rh, w), out_dtype), name=name,
        compiler_params=_cp("parallel", "parallel"),
    )(c_idx, g, r1)


def _chip_slots(parts):
    nt = len(parts)

    def copies(ins, outs, send_sems, recv_sems):
        x, y, c, chips = _place()
        return [pltpu.make_async_remote_copy(
            src_ref=ins[t].at[2 * cx + cy], dst_ref=outs[t].at[jj], send_sem=send_sems.at[3 * t + jj],
            recv_sem=recv_sems.at[3 * t + jj], device_id=(cx, cy, c), device_id_type=MESH)
            for t in range(nt) for jj, (cx, cy) in enumerate(chips)]

    def start(*refs):
        for cp in copies(*refs):
            cp.start()

    def wait(*refs):
        for cp in copies(*refs):
            cp.wait()

    return _Comm(parts, [jax.ShapeDtypeStruct((3,) + p.shape[1:], p.dtype) for p in parts], 3 * nt, start, wait)


def _add_chips(part, r2, place_idx, name):
    s, rh, w = part.shape
    tr = rh
    for cand in (512, 256, 128, 64, 32, 16, 8):
        if rh % cand == 0 and cand * w * 4 <= 4 * 1024 * 1024:
            tr = cand
            break
    per = rh // tr

    def body(b_ref, p_ref, r0_ref, r1_ref, r2_ref, o_ref):
        up = lambda ref: ref[...].astype(F32)
        o_ref[...] = ((up(p_ref) + up(r0_ref)) + up(r1_ref)) + up(r2_ref)

    return pl.pallas_call(
        body,
        grid_spec=pltpu.PrefetchScalarGridSpec(
            num_scalar_prefetch=1, grid=(per,),
            in_specs=[pl.BlockSpec((None, tr, w), lambda a, b_ref: (b_ref[0], a, 0))]
            + [pl.BlockSpec((None, tr, w), lambda a, b_ref, jj=jj: (jj, a, 0)) for jj in range(3)],
            out_specs=pl.BlockSpec((tr, w), lambda a, b_ref: (b_ref[1] * per + a, 0))),
        out_shape=jax.ShapeDtypeStruct((2 * rh, w), F32), name=name, compiler_params=_cp("parallel"),
    )(place_idx, part, r2, r2, r2)


def _swap_halves(bufs, name):
    nt = len(bufs)

    def body(*refs):
        outs = refs[nt:2 * nt]
        send_sems, recv_sems = refs[2 * nt:]
        x, y, c, _ = _place()

        def copy(t, hf):
            rh = outs[t].shape[0] // 2
            region = outs[t].at[pl.ds(hf * rh, rh), :]
            return pltpu.make_async_remote_copy(src_ref=region, dst_ref=region, send_sem=send_sems.at[t],
                                                recv_sem=recv_sems.at[t], device_id=(x, y, 1 - c),
                                                device_id_type=MESH)

        sends = [copy(t, c) for t in range(nt)]
        for cp in sends:
            cp.start()
        for t in range(nt):
            copy(t, 1 - c).wait_recv()
        for cp in sends:
            cp.wait_send()

    return pl.pallas_call(
        body, in_specs=[ANY] * nt, out_specs=[ANY] * nt,
        out_shape=[jax.ShapeDtypeStruct(b.shape, b.dtype) for b in bufs],
        scratch_shapes=[pltpu.SemaphoreType.DMA((nt,)), pltpu.SemaphoreType.DMA((nt,))],
        input_output_aliases={t: t for t in range(nt)}, name=name,
        compiler_params=pltpu.CompilerParams(has_side_effects=True),
    )(*bufs)


def _rs_begin(grads, wire, c_idx, tag):
    r1 = _send_sibling_halves(grads, name=f"rs_sibling_{tag}")
    return [_add_sibling(g, r, c_idx, wire[t], name=f"rs_add_sibling_{tag}_{t}")
            for t, (g, r) in enumerate(zip(grads, r1))]


def _rs_finish(parts, r2, chip_idx, tag):
    fin = [_add_chips(p, r, chip_idx, name=f"rs_add_chips_{tag}_{t}") for t, (p, r) in enumerate(zip(parts, r2))]
    return _swap_halves(fin, name=f"rs_swap_{tag}")


def _all_reduce_small(v, name):
    r, w = v.shape

    def body(v_ref, o_ref, buf, send_sems, recv_sems):
        x, y, c, _ = _place()
        me = 4 * x + 2 * y + c
        buf[me] = v_ref[...]
        cps = []
        for kk in range(1, 8):
            peer = (x ^ ((kk >> 2) & 1), y ^ ((kk >> 1) & 1), c ^ (kk & 1))
            cps.append(pltpu.make_async_remote_copy(src_ref=v_ref, dst_ref=buf.at[me], send_sem=send_sems.at[kk - 1],
                                                    recv_sem=recv_sems.at[kk - 1], device_id=peer, device_id_type=MESH))
        for cp in cps:
            cp.start()
        for kk in range(1, 8):
            pltpu.make_async_remote_copy(src_ref=v_ref, dst_ref=buf.at[me ^ kk], send_sem=send_sems.at[kk - 1],
                                         recv_sem=recv_sems.at[kk - 1], device_id=(x, y, c),
                                         device_id_type=MESH).wait_recv()
        for cp in cps:
            cp.wait_send()
        acc = buf[0]
        for dev in range(1, 8):
            acc = acc + buf[dev]
        o_ref[...] = acc

    vm = pl.BlockSpec(memory_space=pltpu.VMEM)
    return pl.pallas_call(
        body, in_specs=[vm], out_specs=vm, out_shape=jax.ShapeDtypeStruct((r, w), F32),
        scratch_shapes=[pltpu.VMEM((8, r, w), F32), pltpu.SemaphoreType.DMA((7,)), pltpu.SemaphoreType.DMA((7,))],
        name=name, compiler_params=pltpu.CompilerParams(has_side_effects=True),
    )(v)


def _to_shards(a, axis=-1):
    axis = axis % a.ndim
    shp = a.shape
    a = a.reshape(shp[:axis] + (N_CHIPS, shp[axis] // N_CHIPS) + shp[axis + 1:])
    return jnp.moveaxis(a, axis, 0).reshape(N_CHIPS, -1)


def _from_shards(s, shard_shape, axis=-1):
    axis = axis % len(shard_shape)
    a = jnp.moveaxis(s.reshape((N_CHIPS,) + tuple(shard_shape)), 0, axis)
    return a.reshape(tuple(shard_shape[:axis]) + (N_CHIPS * shard_shape[axis],) + tuple(shard_shape[axis + 1:]))


def _pack(vecs, rows):
    flat = jnp.concatenate([v.reshape(v.shape[0], -1) if v.ndim > 1 else v.reshape(1, -1) for v in vecs], axis=1)
    lead = flat.shape[0]
    flat = jnp.pad(flat, ((0, 0), (0, rows * LANES - flat.shape[1])))
    return flat.reshape(lead, rows, LANES)


def _pack_rows(sizes, mult):
    total = sum(sizes)
    rows = -(-total // LANES)
    return -(-rows // mult) * mult


def _unpack(flat, shapes):
    out, pos = [], 0
    for shp in shapes:
        sz = math.prod(shp)
        out.append(flat[..., pos:pos + sz].reshape(flat.shape[:-1] + tuple(shp)))
        pos += sz
    return out


def _row_layout(col, t):
    bsz, hp, seq, _ = col.shape
    return col.reshape(bsz, hp, seq // t, t, 2).transpose(0, 1, 2, 4, 3)


def _from_col_layout(col):
    bsz, hp, seq, _ = col.shape
    a = col.transpose(0, 2, 1, 3).reshape(bsz * seq, 2 * hp)
    return jnp.pad(a, ((0, 0), (0, LANES - 2 * hp)))


def kernel(x, norm_mix, norm_ffn, conv_w_in, conv_b_in, conv_dw, conv_dw_b, conv_ln_g, conv_ln_b, conv_w_out, conv_b_out, pool_w, pool_b, pool_scale, fox_w_in, fox_b_f, fox_q_gain, fox_k_gain, fox_w_o, ffn_w_up, ffn_dw, ffn_dw_b, ffn_w_down, loss_target, m_norm_mix, m_norm_ffn, m_conv_w_in, m_conv_b_in, m_conv_dw, m_conv_dw_b, m_conv_ln_g, m_conv_ln_b, m_conv_w_out, m_conv_b_out, m_pool_w, m_pool_b, m_pool_scale, m_fox_w_in, m_fox_b_f, m_fox_q_gain, m_fox_k_gain, m_fox_w_o, m_ffn_w_up, m_ffn_dw, m_ffn_dw_b, m_ffn_w_down, v_norm_mix, v_norm_ffn, v_conv_w_in, v_conv_b_in, v_conv_dw, v_conv_dw_b, v_conv_ln_g, v_conv_ln_b, v_conv_w_out, v_conv_b_out, v_pool_w, v_pool_b, v_pool_scale, v_fox_w_in, v_fox_b_f, v_fox_q_gain, v_fox_k_gain, v_fox_w_o, v_ffn_w_up, v_ffn_dw, v_ffn_dw_b, v_ffn_w_down):
    bsz, seq, d = x.shape
    n = bsz * seq
    depth = norm_mix.shape[0]
    n_conv, n_pool, n_fox = conv_w_in.shape[0], pool_w.shape[0], fox_w_in.shape[0]
    f2 = ffn_dw_b.shape[1]
    f = f2 // 2
    nh = d // HEAD_DIM
    hp = d // LANES
    ng = len(POOL_WINDOWS)
    gd = d // ng
    c_idx = lax.axis_index("c").astype(jnp.int32).reshape(1)
    chip_idx = jnp.stack([2 * lax.axis_index("x") + lax.axis_index("y"), lax.axis_index("c")]).astype(jnp.int32)

    small_shapes = [conv_b_in.shape, conv_dw.shape, conv_dw_b.shape, conv_ln_g.shape, conv_ln_b.shape,
                    conv_b_out.shape, pool_b.shape, ffn_dw.shape]
    small_rows = _pack_rows([math.prod(s) for s in small_shapes], 16)
    small = _pack([v.reshape(1, -1) for v in (conv_b_in, conv_dw, conv_dw_b, conv_ln_g, conv_ln_b, conv_b_out,
                                                pool_b, ffn_dw)], small_rows)[0]
    def layer_shards(i):
        kind, j = i % 3, i // 3
        shards = [ffn_w_up[i].astype(BF16), ffn_w_down[i].astype(BF16)]
        if kind == 0:
            shards += [conv_w_in[j].astype(BF16), conv_w_out[j].astype(BF16)]
        elif kind == 1:
            shards += [pool_w[j].reshape(ng * (gd // N_CHIPS), gd).astype(BF16)]
        else:
            shards += [fox_w_in[j].astype(BF16), fox_w_o[j].astype(BF16)]
        if i == 0:
            shards.append(small)
        return shards

    gathered = [None] * depth
    gathered[0] = _gather_sibling(layer_shards(0), _run_comm(_gather_chips(layer_shards(0)), name="gather_chips_l0"),
                                  name="gather_sibling_l0")
    small_all = gathered[0][-1].reshape(N_CHIPS, -1)
    sm = _unpack(small_all, small_shapes)
    axes = [-1] * 8
    b_in_f, dw_f, dw_b_f, ln_g_f, ln_b_f, b_out_f, pool_b_f, ffn_dw_f = [
        _from_shards(s_.reshape(N_CHIPS, -1), shp, ax) for s_, shp, ax in zip(sm, small_shapes, axes)]

    xs = x.reshape(n, d)
    tgt = loss_target.reshape(n, d)
    vec = lambda a: a.reshape(1, -1)

    saved = []
    cur = xs
    for i in range(depth):
        kind, j = i % 3, i // 3
        wts = gathered[i]
        sv = {"x_in": cur}
        if kind == 0:
            w_in, w_out = wts[2], wts[3].reshape(d, d)
            wcol = w_in.shape[2]
            h = _rms_fwd(cur, vec(norm_mix[i]), name=f"rms_mix_l{i}")
            pa = _mm(h, w_in, m=n, n=d, k=d, tk=d, tn=wcol, b_stk=wcol, b_s0=0, bias=vec(b_in_f[j, :d]),
                     name=f"conv_in_a_l{i}")
            pg = _mm(h, w_in, m=n, n=d, k=d, tk=d, tn=wcol, b_stk=wcol, b_s0=2, bias=vec(b_in_f[j, d:]),
                     name=f"conv_in_g_l{i}")
            taps = jnp.pad(dw_f[j], ((0, CONV_HALO - CONV_WIDTH), (0, 0)))
            u, s_ = _conf_fwd(pa, pg, taps, vec(dw_b_f[j]), vec(ln_g_f[j]), vec(ln_b_f[j]), seq, name=f"conf_fwd_l{i}")
            cur = _mm(s_, w_out, m=n, n=d, k=d, tk=d, bias=vec(b_out_f[j]), res=cur, name=f"conv_out_l{i}")
            sv.update(h=h, pa=pa, pg=pg, u=u, s=s_, taps=taps)
        elif kind == 1:
            pw = wts[2].reshape(N_CHIPS, ng, gd // N_CHIPS, gd).transpose(1, 0, 2, 3).reshape(ng, gd, gd)
            cur, p = _pool_fwd(cur, vec(norm_mix[i]), pw, vec(pool_b_f[j]), vec(pool_scale[j]), seq,
                               name=f"pool_fwd_l{i}")
            sv.update(p=p, pw=pw)
        else:
            w_in = wts[2].transpose(1, 0, 2).reshape(d, -1)
            w_qkv = w_in[:, :3 * d]
            w_f = jnp.pad(w_in[:, 3 * d:], ((0, 0), (0, LANES - nh)))
            w_o = wts[3].reshape(d, d)
            bf = jnp.pad(vec(fox_b_f[j]), ((0, 0), (0, LANES - nh)))
            qg, kg = jnp.tile(vec(fox_q_gain[j]), (1, nh)), jnp.tile(vec(fox_k_gain[j]), (1, nh))
            h = _rms_fwd(cur, vec(norm_mix[i]), name=f"rms_mix_l{i}")
            qkv = _mm(h, w_qkv, m=n, n=3 * d, k=d, tk=d, tn=d, name=f"fox_qkv_l{i}")
            fl = _mm(h, w_f, m=n, n=LANES, k=d, tk=d, name=f"fox_fl_l{i}")
            qa, ka, v = _fox_prep_fwd(qkv, fl, bf, qg, kg, seq, name=f"fox_prep_l{i}")
            o, lse = _flash_fwd(qa, ka, v, bsz, seq, name=f"fox_attn_l{i}")
            cur = _mm(o, w_o, m=n, n=d, k=d, tk=d, res=cur, name=f"fox_out_l{i}")
            sv.update(h=h, qkv=qkv, fl=fl, qa=qa, ka=ka, v=v, o=o, lse=lse, w_qkv=w_qkv, w_f=w_f, w_o=w_o, bf=bf,
                      qg=qg, kg=kg)
        w_up, w_down = wts[0], wts[1].reshape(f, d)
        ucol = w_up.shape[2]
        sv["x_mid"] = cur
        h2 = _rms_fwd(cur, vec(norm_ffn[i]), name=f"rms_ffn_l{i}")
        uv = _mm(h2, w_up, m=n, n=f, k=d, tk=d, tn=ucol, b_stk=ucol, b_s0=0, name=f"ffn_up_v_l{i}")
        ug = _mm(h2, w_up, m=n, n=f, k=d, tk=d, tn=ucol, b_stk=ucol, b_s0=2, name=f"ffn_up_g_l{i}")
        fdw, fdb = ffn_dw_f[i], ffn_dw_b[i]
        glu_args = (uv, ug, fdw[:, :f], fdw[:, f:], vec(fdb[:f]), vec(fdb[f:]), w_down, cur, seq)
        if i + 1 < depth:
            nxt_shards = layer_shards(i + 1)
            (a_, cur), landed = _ffn_glu_down(*glu_args, name=f"ffn_glu_down_l{i}", comm=_gather_chips(nxt_shards))
            gathered[i + 1] = _gather_sibling(nxt_shards, landed, name=f"gather_sibling_l{i + 1}")
        else:
            a_, cur = _ffn_glu_down(*glu_args, name=f"ffn_glu_down_l{i}")
        sv.update(h2=h2, uv=uv, ug=ug, a=a_)
        saved.append(sv)

    dy, loss_part = _loss(cur, tgt, name="loss")
    loss = lax.psum(loss_part[0, 0], ("x", "y", "c"))

    g_norm_mix, g_norm_ffn = [None] * depth, [None] * depth
    g_ffn_dw_b = [None] * depth
    g_up, g_down = [None] * depth, [None] * depth
    g_small = {}
    g_conv_in, g_conv_out = [None] * n_conv, [None] * n_conv
    g_pool_w = g_fox_in = g_fox_o = None
    g_pool_scale = g_bf = g_qg = g_kg = None
    part_small = {"b_in": [None] * n_conv, "dw": [None] * n_conv, "dw_b": [None] * n_conv, "ln_g": [None] * n_conv,
                  "ln_b": [None] * n_conv, "b_out": [None] * n_conv, "pool_b": None, "ffn_dw": [None] * depth}

    pending = []
    done = {}

    def take_pending():
        groups = list(pending)
        pending.clear()
        parts = [p_ for g_ in groups for p_ in g_[0]]
        return groups, (_chip_slots(parts) if parts else None)

    def finish_groups(groups, r2):
        pos = 0
        for parts, tag, sink in groups:
            sink(_rs_finish(parts, r2[pos:pos + len(parts)], chip_idx, tag))
            pos += len(parts)

    def carried(fn, args, name):
        groups, comm = take_pending()
        if comm is None:
            return fn(*args, name=name)
        outs, r2 = fn(*args, name=name, comm=comm)
        finish_groups(groups, r2)
        return outs

    dcur = dy
    for i in reversed(range(depth)):
        kind, j = i % 3, i // 3
        wts, sv = gathered[i], saved[i]
        w_up, w_down = wts[0], wts[1].reshape(f, d)
        ucol = w_up.shape[2]
        fdw, fdb = ffn_dw_f[i], ffn_dw_b[i]
        dw_down = _mm(sv["a"], dcur, m=f, n=d, k=n, tm=f // 2, tn=d, tk=2048, ta=True, name=f"ffn_down_dw_l{i}")
        duv, dug, dwv, dwg, dbv, dbg = carried(
            _ffn_glu_bwd, (sv["uv"], sv["ug"], dcur, w_down, fdw[:, :f], fdw[:, f:], vec(fdb[:f]), vec(fdb[f:]), seq),
            name=f"ffn_glu_bwd_l{i}")
        dw_up = _mm(sv["h2"], duv, m=d, n=f, k=n, tm=d, tn=ucol, tk=2048, ta=True, o_stk=ucol, o_s0=0,
                    o_slots=N_CHIPS, name=f"ffn_up_dw_v_l{i}")
        dw_up = _mm(sv["h2"], dug, m=d, n=f, k=n, tm=d, tn=ucol, tk=2048, ta=True, o_stk=ucol, o_s0=2,
                    o_slots=N_CHIPS, o_buf=dw_up, name=f"ffn_up_dw_g_l{i}")
        dh2 = _mm(duv, w_up, m=n, n=d, k=f, tn=d, tk=ucol, tb=True, b_stk=ucol, b_s0=0, name=f"ffn_up_dx_v_l{i}")
        dh2 = _mm(dug, w_up, m=n, n=d, k=f, tn=d, tk=ucol, tb=True, b_stk=ucol, b_s0=2, res=dh2,
                  name=f"ffn_up_dx_g_l{i}")
        dmid, g_norm_ffn[i] = _rms_bwd(sv["x_mid"], dh2, vec(norm_ffn[i]), dcur, name=f"rms_ffn_bwd_l{i}")
        part_small["ffn_dw"][i] = jnp.concatenate([dwv, dwg], axis=1)
        g_ffn_dw_b[i] = jnp.concatenate([dbv, dbg], axis=1)

        def ffn_sink(red, i=i):
            g_up[i], g_down[i] = red[0], red[1]

        pending.append((_rs_begin([dw_up, dw_down.reshape(N_CHIPS, f // N_CHIPS, d)], [BF16, BF16], c_idx,
                                  tag=f"ffn_l{i}"), f"ffn_l{i}", ffn_sink))

        if kind == 0:
            w_in, w_out = wts[2], wts[3].reshape(d, d)
            wcol = w_in.shape[2]
            ds = _mm(dmid, w_out, m=n, n=d, k=d, tk=d, tb=True, name=f"conv_out_dx_l{i}")
            dw_out = _mm(sv["s"], dmid, m=d, n=d, k=n, tm=d, tn=d, tk=2048, ta=True, name=f"conv_out_dw_l{i}")
            dpa, dpg, ddw, ddwb, dlng, dlnb, dba, dbg_ = carried(
                _conf_bwd, (sv["u"], ds, sv["pa"], sv["pg"], sv["taps"], vec(ln_g_f[j]), vec(ln_b_f[j]), seq),
                name=f"conf_bwd_l{i}")
            dw_in = _mm(sv["h"], dpa, m=d, n=d, k=n, tm=d, tn=wcol, tk=2048, ta=True, o_stk=wcol, o_s0=0,
                        o_slots=N_CHIPS, name=f"conv_in_dw_a_l{i}")
            dw_in = _mm(sv["h"], dpg, m=d, n=d, k=n, tm=d, tn=wcol, tk=2048, ta=True, o_stk=wcol, o_s0=2,
                        o_slots=N_CHIPS, o_buf=dw_in, name=f"conv_in_dw_g_l{i}")
            dh = _mm(dpa, w_in, m=n, n=d, k=d, tn=d, tk=wcol, tb=True, b_stk=wcol, b_s0=0, name=f"conv_in_dx_a_l{i}")
            dh = _mm(dpg, w_in, m=n, n=d, k=d, tn=d, tk=wcol, tb=True, b_stk=wcol, b_s0=2, res=dh,
                     name=f"conv_in_dx_g_l{i}")
            dcur, g_norm_mix[i], db_out = _rms_bwd(sv["x_in"], dh, vec(norm_mix[i]), dmid, name=f"rms_mix_bwd_l{i}",
                                                   colsum=True)
            part_small["b_in"][j] = jnp.concatenate([dba, dbg_], axis=1)
            part_small["dw"][j] = ddw[:CONV_WIDTH]
            part_small["dw_b"][j], part_small["ln_g"][j], part_small["ln_b"][j] = ddwb, dlng, dlnb
            part_small["b_out"][j] = db_out
            mix_grads = [dw_in, dw_out.reshape(N_CHIPS, d // N_CHIPS, d)]
        elif kind == 1:
            dcur, g_norm_mix[i], dpw, dpb, dpsc = _pool_bwd(dmid, sv["x_in"], sv["p"], vec(norm_mix[i]), sv["pw"],
                                                            vec(pool_b_f[j]), vec(pool_scale[j]), seq,
                                                            name=f"pool_bwd_l{i}")
            part_small["pool_b"] = dpb
            g_pool_scale = dpsc
            mix_grads = [dpw.reshape(ng, N_CHIPS, gd // N_CHIPS, gd).transpose(1, 0, 2, 3).reshape(N_CHIPS, gd, gd)]
        else:
            do = _mm(dmid, sv["w_o"], m=n, n=d, k=d, tk=d, tb=True, name=f"fox_out_dx_l{i}")
            dw_o = _mm(sv["o"], dmid, m=d, n=d, k=n, tm=d, tn=d, tk=2048, ta=True, name=f"fox_out_dw_l{i}")
            dq, dcc, dk, dv, dck = _flash_bwd(sv["qa"], sv["ka"], sv["v"], do, sv["o"],
                                              _row_layout(sv["lse"], _tile(seq, ATTN_TILE)), bsz, seq,
                                              name=f"fox_attn_bwd_l{i}")
            dqkv, dfl, dqg, dkg, dbf = carried(
                _fox_prep_bwd, (sv["qkv"], dq, dk, dv, _from_col_layout(dcc), _from_col_layout(dck), sv["fl"], sv["bf"],
                                sv["qg"], sv["kg"], seq), name=f"fox_prep_bwd_l{i}")
            dw_qkv = _mm(sv["h"], dqkv, m=d, n=3 * d, k=n, tm=d, tn=d, tk=2048, ta=True, name=f"fox_qkv_dw_l{i}")
            dw_f = _mm(sv["h"], dfl, m=d, n=LANES, k=n, tm=d, tk=2048, ta=True, name=f"fox_fl_dw_l{i}")
            dh = _mm(dqkv, sv["w_qkv"], m=n, n=d, k=3 * d, tn=d, tk=d, tb=True, name=f"fox_qkv_dx_l{i}")
            dh = _mm(dfl, sv["w_f"], m=n, n=d, k=LANES, tn=d, tb=True, res=dh, name=f"fox_fl_dx_l{i}")
            dcur, g_norm_mix[i] = _rms_bwd(sv["x_in"], dh, vec(norm_mix[i]), dmid, name=f"rms_mix_bwd_l{i}")
            g_bf = dbf[:, :nh]
            g_qg = dqg.reshape(nh, HEAD_DIM).sum(axis=0, keepdims=True)
            g_kg = dkg.reshape(nh, HEAD_DIM).sum(axis=0, keepdims=True)
            dw_in_full = jnp.concatenate([dw_qkv, dw_f[:, :nh]], axis=1)
            wshard = dw_in_full.shape[1] // N_CHIPS
            mix_grads = [dw_in_full.reshape(d, N_CHIPS, wshard).transpose(1, 0, 2),
                         dw_o.reshape(N_CHIPS, d // N_CHIPS, d)]
        wire = [BF16] * len(mix_grads)
        if i == 0:
            sm_parts = [jnp.concatenate(part_small["b_in"]), jnp.stack(part_small["dw"]),
                        jnp.concatenate(part_small["dw_b"]), jnp.concatenate(part_small["ln_g"]),
                        jnp.concatenate(part_small["ln_b"]), jnp.concatenate(part_small["b_out"]),
                        part_small["pool_b"].reshape(n_pool, ng, gd), jnp.stack(part_small["ffn_dw"])]
            mix_grads.append(_pack([_to_shards(p_) for p_ in sm_parts], small_rows))
            wire.append(F32)

        def mix_sink(red, i=i, kind=kind, j=j):
            if kind == 0:
                g_conv_in[j], g_conv_out[j] = red[0], red[1]
            elif kind == 1:
                done["pool_w"] = red[0]
            else:
                done["fox_in"], done["fox_o"] = red[0], red[1]
            if i == 0:
                done["small"] = red[-1].reshape(-1)

        pending.append((_rs_begin(mix_grads, wire, c_idx, tag=f"mix_l{i}"), f"mix_l{i}", mix_sink))

    groups, comm = take_pending()
    finish_groups(groups, _run_comm(comm, name="rs_chips_tail"))
    g_pool_w, g_fox_in, g_fox_o, g_small_flat = done["pool_w"], done["fox_in"], done["fox_o"], done["small"]

    grad_x = dcur.reshape(bsz, seq, d)

    rep_parts = [jnp.concatenate(g_norm_mix), jnp.concatenate(g_norm_ffn), g_pool_scale, g_bf, g_qg, g_kg,
                 jnp.concatenate(g_ffn_dw_b)]
    rep_shapes = [norm_mix.shape, norm_ffn.shape, pool_scale.shape, fox_b_f.shape, fox_q_gain.shape,
                  fox_k_gain.shape, ffn_dw_b.shape]
    rep_rows = _pack_rows([math.prod(s) for s in rep_shapes], 8)
    rep = _all_reduce_small(_pack([p_.reshape(1, -1) for p_ in rep_parts], rep_rows)[0], name="all_reduce_small")
    g_rep = _unpack(rep.reshape(-1), rep_shapes)
    g_sm = _unpack(g_small_flat, small_shapes)

    grads = {
        "norm_mix": g_rep[0], "norm_ffn": g_rep[1],
        "conv_w_in": jnp.stack(g_conv_in), "conv_b_in": g_sm[0], "conv_dw": g_sm[1], "conv_dw_b": g_sm[2],
        "conv_ln_g": g_sm[3], "conv_ln_b": g_sm[4], "conv_w_out": jnp.stack(g_conv_out), "conv_b_out": g_sm[5],
        "pool_w": g_pool_w.reshape(pool_w.shape), "pool_b": g_sm[6], "pool_scale": g_rep[2],
        "fox_w_in": g_fox_in.reshape(fox_w_in.shape), "fox_b_f": g_rep[3], "fox_q_gain": g_rep[4],
        "fox_k_gain": g_rep[5], "fox_w_o": g_fox_o.reshape(fox_w_o.shape),
        "ffn_w_up": jnp.stack(g_up), "ffn_dw": g_sm[7], "ffn_dw_b": g_rep[6], "ffn_w_down": jnp.stack(g_down),
    }
    weights = dict(norm_mix=norm_mix, norm_ffn=norm_ffn, conv_w_in=conv_w_in, conv_b_in=conv_b_in, conv_dw=conv_dw,
                   conv_dw_b=conv_dw_b, conv_ln_g=conv_ln_g, conv_ln_b=conv_ln_b, conv_w_out=conv_w_out,
                   conv_b_out=conv_b_out, pool_w=pool_w, pool_b=pool_b, pool_scale=pool_scale, fox_w_in=fox_w_in,
                   fox_b_f=fox_b_f, fox_q_gain=fox_q_gain, fox_k_gain=fox_k_gain, fox_w_o=fox_w_o, ffn_w_up=ffn_w_up,
                   ffn_dw=ffn_dw, ffn_dw_b=ffn_dw_b, ffn_w_down=ffn_w_down)
    m_in = dict(norm_mix=m_norm_mix, norm_ffn=m_norm_ffn, conv_w_in=m_conv_w_in, conv_b_in=m_conv_b_in,
                conv_dw=m_conv_dw, conv_dw_b=m_conv_dw_b, conv_ln_g=m_conv_ln_g, conv_ln_b=m_conv_ln_b,
                conv_w_out=m_conv_w_out, conv_b_out=m_conv_b_out, pool_w=m_pool_w, pool_b=m_pool_b,
                pool_scale=m_pool_scale, fox_w_in=m_fox_w_in, fox_b_f=m_fox_b_f, fox_q_gain=m_fox_q_gain,
                fox_k_gain=m_fox_k_gain, fox_w_o=m_fox_w_o, ffn_w_up=m_ffn_w_up, ffn_dw=m_ffn_dw,
                ffn_dw_b=m_ffn_dw_b, ffn_w_down=m_ffn_w_down)
    v_in = dict(norm_mix=v_norm_mix, norm_ffn=v_norm_ffn, conv_w_in=v_conv_w_in, conv_b_in=v_conv_b_in,
                conv_dw=v_conv_dw, conv_dw_b=v_conv_dw_b, conv_ln_g=v_conv_ln_g, conv_ln_b=v_conv_ln_b,
                conv_w_out=v_conv_w_out, conv_b_out=v_conv_b_out, pool_w=v_pool_w, pool_b=v_pool_b,
                pool_scale=v_pool_scale, fox_w_in=v_fox_w_in, fox_b_f=v_fox_b_f, fox_q_gain=v_fox_q_gain,
                fox_k_gain=v_fox_k_gain, fox_w_o=v_fox_w_o, ffn_w_up=v_ffn_w_up, ffn_dw=v_ffn_dw,
                ffn_dw_b=v_ffn_dw_b, ffn_w_down=v_ffn_w_down)
    names = list(weights)
    g_out, d_out, m_out, v_out = [], [], [], []
    for nm in names:
        g_, dl_, m_, v_ = _adamw_nd(weights[nm], grads[nm].reshape(weights[nm].shape), m_in[nm], v_in[nm],
                                    name=f"adamw_{nm}")
        g_out.append(g_)
        d_out.append(dl_)
        m_out.append(m_)
        v_out.append(v_)
    return (loss, grad_x, *g_out, *d_out, *m_out, *v_out)
```

```python
import math

import jax
import jax.numpy as jnp
import numpy as np
from jax import lax
from jax.experimental import pallas as pl
from jax.experimental.pallas import tpu as pltpu

F32 = jnp.float32
BF16 = jnp.bfloat16
HI = lax.Precision.HIGHEST
MESH = pl.DeviceIdType.MESH
ANY = pl.BlockSpec(memory_space=pl.ANY)

EPS = 1e-6
HEAD_DIM = 64
LANES = 128
POOL_WINDOWS = (2, 4, 8, 16)
CONV_WIDTH = 31
CONV_HALO = 32
FFN_HALO = 8
FFN_ROWS, FFN_COLS = 256, 1408
FFN_DOWN_ROWS, FFN_DOWN_COLS = 128, 2816
POOL_HALO = 16
N_CHIPS = 4
NEG = -1e30

ADAM_LR = 0.001
ADAM_B1 = 0.9
ADAM_B2 = 0.999
ADAM_EPS = 1e-08
ADAM_WD = 0.01
ADAM_STEP = 10

V7X_VMEM_LIMIT_BYTES = 56 * 1024 * 1024


def _cp(*sem):
    return pltpu.CompilerParams(dimension_semantics=sem or None, vmem_limit_bytes=V7X_VMEM_LIMIT_BYTES)


def _tile(n, pref):
    t = min(n, pref)
    assert n % t == 0, (n, pref)
    return t


def _sig(v):
    return jax.nn.sigmoid(v)


def _roll(v, shift):
    n = v.shape[0]
    shift = shift % n
    return v if shift == 0 else pltpu.roll(v, shift, 0)


SUBLANES = 8


CONV_ROWS = 64


def _conv_taps(rot_ref, w_ref, out_ref, tm, start_of):
    d = out_ref.shape[1]
    for lc in range(d // LANES):
        ls = slice(lc * LANES, (lc + 1) * LANES)

        for r0 in range(0, tm, CONV_ROWS):
            acc = None
            for sh in range(CONV_WIDTH):
                kk = CONV_WIDTH - 1 - sh
                lo = r0 + start_of(sh)
                term = w_ref[kk:kk + 1, ls] * rot_ref[sh % SUBLANES, lo:lo + CONV_ROWS, ls]
                acc = term if acc is None else acc + term
            out_ref[r0:r0 + CONV_ROWS, ls] = acc


def _tap_grads(rotz_ref, rotd_ref, dw_ref, tm, halo):
    d = dw_ref.shape[1]
    for lc in range(d // LANES):
        ls = slice(lc * LANES, (lc + 1) * LANES)

        acc = [None] * CONV_WIDTH
        for r0 in range(0, tm, CONV_ROWS):
            duc = rotd_ref[0, r0:r0 + CONV_ROWS, ls]
            for sh in range(CONV_WIDTH):
                lo = r0 + halo - (sh // SUBLANES) * SUBLANES
                prod = duc * rotz_ref[sh % SUBLANES, lo:lo + CONV_ROWS, ls]
                part = prod.reshape(CONV_ROWS // SUBLANES, SUBLANES, LANES).sum(axis=0)
                acc[sh] = part if acc[sh] is None else acc[sh] + part
        for sh in range(CONV_WIDTH):
            kk = CONV_WIDTH - 1 - sh
            dw_ref[kk:kk + 1, ls] += jnp.sum(acc[sh], axis=0, keepdims=True)


class _Comm:
    def __init__(self, ins, out_shapes, n_sems, start, wait):
        self.ins, self.out_shapes, self.n_sems, self.start, self.wait = list(ins), list(out_shapes), n_sems, start, wait


def _hosted(body, comm, *, grid, in_specs, out_specs, out_shape, scratch_shapes, sem, name, ins):
    if comm is None:
        return pl.pallas_call(body, grid=grid, in_specs=in_specs, out_specs=out_specs, out_shape=out_shape,
                              scratch_shapes=scratch_shapes, name=name, compiler_params=_cp(*sem))(*ins)
    n_in, n_out, n_scr = len(in_specs), len(out_specs), len(scratch_shapes)
    nci, nco = len(comm.ins), len(comm.out_shapes)

    def wrapped(*refs):
        pos = [0]

        def take(cnt):
            pos[0] += cnt
            return refs[pos[0] - cnt:pos[0]]

        r_in, c_in, r_out, c_out, r_scr = take(n_in), take(nci), take(n_out), take(nco), take(n_scr)
        send_sems, recv_sems = take(2)
        ids = [pl.program_id(ax) for ax in range(len(grid))]
        first, last = ids[0] == 0, ids[0] == grid[0] - 1
        for ax in range(1, len(grid)):
            first = jnp.logical_and(first, ids[ax] == 0)
            last = jnp.logical_and(last, ids[ax] == grid[ax] - 1)

        @pl.when(first)
        def _():
            comm.start(c_in, c_out, send_sems, recv_sems)

        body(*r_in, *r_out, *r_scr)

        @pl.when(last)
        def _():
            comm.wait(c_in, c_out, send_sems, recv_sems)

    outs = pl.pallas_call(
        wrapped, grid=grid, in_specs=list(in_specs) + [ANY] * nci, out_specs=list(out_specs) + [ANY] * nco,
        out_shape=list(out_shape) + comm.out_shapes,
        scratch_shapes=list(scratch_shapes) + [pltpu.SemaphoreType.DMA((comm.n_sems,))] * 2, name=name,
        compiler_params=pltpu.CompilerParams(dimension_semantics=sem, vmem_limit_bytes=V7X_VMEM_LIMIT_BYTES,
                                             has_side_effects=True),
    )(*ins, *comm.ins)
    return list(outs[:n_out]), list(outs[n_out:])


def _run_comm(comm, name):
    def body(*refs):
        nci, nco = len(comm.ins), len(comm.out_shapes)
        c_in, c_out, send_sems, recv_sems = refs[:nci], refs[nci:nci + nco], refs[-2], refs[-1]
        comm.start(c_in, c_out, send_sems, recv_sems)
        comm.wait(c_in, c_out, send_sems, recv_sems)

    return pl.pallas_call(
        body, in_specs=[ANY] * len(comm.ins), out_specs=[ANY] * len(comm.out_shapes), out_shape=comm.out_shapes,
        scratch_shapes=[pltpu.SemaphoreType.DMA((comm.n_sems,))] * 2, name=name,
        compiler_params=pltpu.CompilerParams(has_side_effects=True),
    )(*comm.ins)


def _mm(a, b, *, m, n, k, name, tm=1024, tn=1024, tk=512, ta=False, tb=False, b_stk=None, b_s0=0,
        o_stk=None, o_s0=0, o_slots=None, o_buf=None, bias=None, res=None, out_dtype=F32):
    tm, tn, tk = _tile(m, tm), _tile(n, tn), _tile(k, tk)
    gi, gj, gk = m // tm, n // tn, k // tk
    nsl = 1
    a_spec = pl.BlockSpec((tk, tm), lambda j, i, kk: (kk, i)) if ta else pl.BlockSpec((tm, tk), lambda j, i, kk: (i, kk))
    if b_stk is None:
        b_spec = pl.BlockSpec((tn, tk), lambda j, i, kk: (j, kk)) if tb else pl.BlockSpec((tk, tn), lambda j, i, kk: (kk, j))
    elif tb and tk > b_stk:
        assert tk % b_stk == 0 and b_s0 % (tk // b_stk) == 0 and not ta
        nsl = tk // b_stk
        b_spec = pl.BlockSpec((nsl, tn, b_stk), lambda j, i, kk: (b_s0 // nsl + kk, j, 0))
    elif tb:
        assert b_stk % tk == 0
        per = b_stk // tk
        b_spec = pl.BlockSpec((None, tn, tk), lambda j, i, kk: (b_s0 + kk // per, j, kk % per))
    else:
        assert b_stk % tn == 0
        per = b_stk // tn
        b_spec = pl.BlockSpec((None, tk, tn), lambda j, i, kk: (b_s0 + j // per, kk, j % per))
    ins, in_specs = [a, b], [a_spec, b_spec]
    if bias is not None:
        ins.append(bias)
        in_specs.append(pl.BlockSpec((1, tn), lambda j, i, kk: (0, j)))
    if res is not None:
        ins.append(res)
        in_specs.append(pl.BlockSpec((tm, tn), lambda j, i, kk: (i, j)))
    aliases = {}
    if o_stk is None:
        out_shape = jax.ShapeDtypeStruct((m, n), out_dtype)
        o_spec = pl.BlockSpec((tm, tn), lambda j, i, kk: (i, j))
    else:
        assert o_stk % tn == 0
        pero = o_stk // tn
        out_shape = jax.ShapeDtypeStruct((o_slots, m, o_stk), out_dtype)
        o_spec = pl.BlockSpec((None, tm, tn), lambda j, i, kk: (o_s0 + j // pero, i, j % pero))
        if o_buf is not None:
            aliases = {len(ins): 0}
            ins.append(o_buf)
            in_specs.append(ANY)
    has_bias, has_res, has_buf = bias is not None, res is not None, o_buf is not None
    dn = (((0 if ta else 1,), (1 if tb else 0,)), ((), ()))

    def body(*refs):
        a_ref, b_ref = refs[0], refs[1]
        pos = 2
        bias_ref = refs[pos] if has_bias else None
        pos += has_bias
        res_ref = refs[pos] if has_res else None
        pos += has_res + has_buf
        o_ref = refs[pos]
        if nsl == 1:
            p = lax.dot_general(a_ref[...].astype(BF16), b_ref[...].astype(BF16), dn, preferred_element_type=F32)
        else:
            p = sum(lax.dot_general(a_ref[:, s * b_stk:(s + 1) * b_stk].astype(BF16), b_ref[s].astype(BF16), dn,
                                    preferred_element_type=F32) for s in range(nsl))

        def finish(acc):
            if has_bias:
                acc = acc + bias_ref[...]
            if has_res:
                acc = acc + res_ref[...]
            o_ref[...] = acc.astype(o_ref.dtype)

        if gk == 1:
            finish(p)
        else:
            acc_ref = refs[pos + 1]
            kk = pl.program_id(2)

            @pl.when(kk == 0)
            def _():
                acc_ref[...] = p

            @pl.when(kk > 0)
            def _():
                acc_ref[...] += p

            @pl.when(kk == gk - 1)
            def _():
                finish(acc_ref[...])

    return pl.pallas_call(
        body, grid=(gj, gi, gk), in_specs=in_specs, out_specs=o_spec, out_shape=out_shape,
        scratch_shapes=[pltpu.VMEM((tm, tn), F32)] if gk > 1 else [],
        input_output_aliases=aliases, name=name,
        compiler_params=_cp("parallel", "parallel", "arbitrary"),
    )(*ins)


def _rms_fwd(x, g, name):
    n, d = x.shape
    tm = _tile(n, 512)

    def body(x_ref, g_ref, h_ref):
        xv = x_ref[...]
        r = lax.rsqrt(jnp.mean(xv * xv, axis=-1, keepdims=True) + EPS)
        h_ref[...] = (xv * r * g_ref[...]).astype(h_ref.dtype)

    return pl.pallas_call(
        body, grid=(n // tm,),
        in_specs=[pl.BlockSpec((tm, d), lambda i: (i, 0)), pl.BlockSpec((1, d), lambda i: (0, 0))],
        out_specs=pl.BlockSpec((tm, d), lambda i: (i, 0)),
        out_shape=jax.ShapeDtypeStruct((n, d), BF16), name=name, compiler_params=_cp("arbitrary"),
    )(x, g)


def _rms_bwd(x, dh, g, dres, name, colsum=False):
    n, d = x.shape
    tm = _tile(n, 512)

    def body(x_ref, dh_ref, g_ref, dres_ref, dx_ref, dg_ref, *rest):
        i = pl.program_id(0)
        xv, dhv = x_ref[...], dh_ref[...]
        r = lax.rsqrt(jnp.mean(xv * xv, axis=-1, keepdims=True) + EPS)
        xn = xv * r
        dxn = dhv * g_ref[...]
        dx_ref[...] = dres_ref[...] + r * (dxn - xn * jnp.mean(dxn * xn, axis=-1, keepdims=True))
        dg = jnp.sum(dhv * xn, axis=0, keepdims=True)

        @pl.when(i == 0)
        def _():
            dg_ref[...] = jnp.zeros_like(dg_ref)
            if colsum:
                rest[0][...] = jnp.zeros_like(rest[0])

        dg_ref[...] += dg
        if colsum:
            rest[0][...] += jnp.sum(dres_ref[...], axis=0, keepdims=True)

    row = pl.BlockSpec((tm, d), lambda i: (i, 0))
    vec = pl.BlockSpec((1, d), lambda i: (0, 0))
    out_shape = [jax.ShapeDtypeStruct((n, d), F32), jax.ShapeDtypeStruct((1, d), F32)]
    out_specs = [row, vec]
    if colsum:
        out_shape.append(jax.ShapeDtypeStruct((1, d), F32))
        out_specs.append(vec)
    return pl.pallas_call(
        body, grid=(n // tm,), in_specs=[row, row, vec, row], out_specs=out_specs, out_shape=out_shape,
        name=name, compiler_params=_cp("arbitrary"),
    )(x, dh, g, dres)


def _loss(y, tgt, name):
    n, d = y.shape
    tm = _tile(n, 512)

    def body(y_ref, t_ref, dy_ref, l_ref):
        i = pl.program_id(0)
        e = y_ref[...] - t_ref[...]
        dy_ref[...] = e / d
        part = 0.5 * jnp.sum(jnp.mean(e * e, axis=-1, keepdims=True), axis=0, keepdims=True)

        @pl.when(i == 0)
        def _():
            l_ref[...] = jnp.zeros_like(l_ref)

        l_ref[...] += part

    row = pl.BlockSpec((tm, d), lambda i: (i, 0))
    return pl.pallas_call(
        body, grid=(n // tm,), in_specs=[row, row],
        out_specs=[row, pl.BlockSpec((1, 1), lambda i: (0, 0))],
        out_shape=[jax.ShapeDtypeStruct((n, d), F32), jax.ShapeDtypeStruct((1, 1), F32)],
        name=name, compiler_params=_cp("arbitrary"),
    )(y, tgt)


def _ffn_specs(n, f, tm, tc, seq):
    hb = FFN_HALO
    cur = pl.BlockSpec((tm, tc), lambda j, i: (i, j))
    prev = pl.BlockSpec((hb, tc), lambda j, i: (jnp.maximum(i * (tm // hb) - 1, 0), j))
    nxt = pl.BlockSpec((hb, tc), lambda j, i: (jnp.minimum((i + 1) * (tm // hb), n // hb - 1), j))
    taps = pl.BlockSpec((3, tc), lambda j, i: (0, j))
    vec = pl.BlockSpec((1, tc), lambda j, i: (0, j))
    return cur, prev, nxt, taps, vec


def _ffn_glu_down(uv, ug, wv, wg, bv, bg, w_down, res, seq, name, comm=None):
    n, f = uv.shape
    d = w_down.shape[1]
    tm, tc = _tile(seq, FFN_DOWN_ROWS), _tile(f, FFN_DOWN_COLS)
    tps = seq // tm
    nj = f // tc
    hb = FFN_HALO
    cur = pl.BlockSpec((tm, tc), lambda i, j: (i, j))
    prev = pl.BlockSpec((hb, tc), lambda i, j: (jnp.maximum(i * (tm // hb) - 1, 0), j))
    taps = pl.BlockSpec((3, tc), lambda i, j: (0, j))
    vec = pl.BlockSpec((1, tc), lambda i, j: (0, j))
    wblk = pl.BlockSpec((tc, d), lambda i, j: (j, 0))
    row = pl.BlockSpec((tm, d), lambda i, j: (i, 0))

    def body(uvp, uvc, ugp, ugc, wv_ref, wg_ref, bv_ref, bg_ref, wd_ref, res_ref, a_ref, x_ref, acc_ref):
        first = (pl.program_id(0) % tps) == 0
        j = pl.program_id(1)

        def conv(p_ref, c_ref, w_ref, b_ref):
            xs = jnp.concatenate([jnp.where(first, 0.0, p_ref[...]), c_ref[...]], axis=0)
            w = w_ref[...]
            y = w[2:3] * xs + w[1:2] * _roll(xs, 1) + w[0:1] * _roll(xs, 2)
            return y[FFN_HALO:] + b_ref[...]

        val = conv(uvp, uvc, wv_ref, bv_ref)
        gate = conv(ugp, ugc, wg_ref, bg_ref)
        a = (gate * _sig(gate) * val).astype(BF16)
        a_ref[...] = a
        p = jnp.dot(a, wd_ref[...], preferred_element_type=F32)

        @pl.when(j == 0)
        def _():
            acc_ref[...] = res_ref[...] + p

        @pl.when(j > 0)
        def _():
            acc_ref[...] += p

        @pl.when(j == nj - 1)
        def _():
            x_ref[...] = acc_ref[...]

    return _hosted(
        body, comm, grid=(n // tm, nj), in_specs=[prev, cur, prev, cur, taps, taps, vec, vec, wblk, row],
        out_specs=[cur, row], out_shape=[jax.ShapeDtypeStruct((n, f), BF16), jax.ShapeDtypeStruct((n, d), F32)],
        scratch_shapes=[pltpu.VMEM((tm, d), F32)], sem=("parallel", "arbitrary"), name=name,
        ins=(uv, uv, ug, ug, wv, wg, bv, bg, w_down, res))


def _ffn_glu_bwd(uv, ug, dx, w_down, wv, wg, bv, bg, seq, name, comm=None):
    n, f = uv.shape
    d = dx.shape[1]
    tm, tc = _tile(seq, FFN_ROWS), _tile(f, FFN_COLS)
    tps = seq // tm
    hb = FFN_HALO
    ext = tm + hb
    cur, prev, nxt, taps, vec = _ffn_specs(n, f, tm, tc, seq)
    dx_cur = pl.BlockSpec((tm, d), lambda j, i: (i, 0))
    dx_nxt = pl.BlockSpec((hb, d), lambda j, i: (jnp.minimum((i + 1) * (tm // hb), n // hb - 1), 0))
    wblk = pl.BlockSpec((tc, d), lambda j, i: (j, 0))

    def body(uvp, uvc, uvn, ugp, ugc, ugn, dx_c, dx_n, wd_ref, wv_ref, wg_ref, bv_ref, bg_ref,
             duv_ref, dug_ref, dwv_ref, dwg_ref, dbv_ref, dbg_ref):
        i = pl.program_id(1)
        first = (i % tps) == 0
        last = (i % tps) == tps - 1
        dx_e = jnp.concatenate([dx_c[...], jnp.where(last, 0.0, dx_n[...])], axis=0).astype(BF16)
        da_e = lax.dot_general(dx_e, wd_ref[...], _NT, preferred_element_type=F32)

        def taps3(p_ref, c_ref, n_ref):
            xs = jnp.concatenate([jnp.where(first, 0.0, p_ref[...]), c_ref[...], n_ref[...]], axis=0)
            return xs, _roll(xs, 1), _roll(xs, 2)

        xv, xg = taps3(uvp, uvc, uvn), taps3(ugp, ugc, ugn)
        wv_, wg_ = wv_ref[...], wg_ref[...]

        def conv(xs, w, b_ref):
            return (w[2:3] * xs[0] + w[1:2] * xs[1] + w[0:1] * xs[2])[hb:] + b_ref[...]

        val, gate = conv(xv, wv_, bv_ref), conv(xg, wg_, bg_ref)
        sg = _sig(gate)
        dval = da_e * (gate * sg)
        dgate = da_e * val * (sg * (1.0 + gate * (1.0 - sg)))

        def conv_t(dv, w):
            return (w[2:3] * dv + w[1:2] * _roll(dv, ext - 1) + w[0:1] * _roll(dv, ext - 2))[:tm]

        duv_ref[...] = conv_t(dval, wv_).astype(duv_ref.dtype)
        dug_ref[...] = conv_t(dgate, wg_).astype(dug_ref.dtype)

        def tap_grads(d_own, xs):
            return jnp.concatenate(
                [jnp.sum(d_own * xs[2 - kk][hb:hb + tm], axis=0, keepdims=True) for kk in range(3)], axis=0)

        dv_own, dg_own = dval[:tm], dgate[:tm]

        @pl.when(i == 0)
        def _():
            for r in (dwv_ref, dwg_ref, dbv_ref, dbg_ref):
                r[...] = jnp.zeros_like(r)

        dwv_ref[...] += tap_grads(dv_own, xv)
        dwg_ref[...] += tap_grads(dg_own, xg)
        dbv_ref[...] += jnp.sum(dv_own, axis=0, keepdims=True)
        dbg_ref[...] += jnp.sum(dg_own, axis=0, keepdims=True)

    return _hosted(
        body, comm, grid=(f // tc, n // tm),
        in_specs=[prev, cur, nxt, prev, cur, nxt, dx_cur, dx_nxt, wblk, taps, taps, vec, vec],
        out_specs=[cur, cur, taps, taps, vec, vec],
        out_shape=[jax.ShapeDtypeStruct((n, f), BF16), jax.ShapeDtypeStruct((n, f), BF16),
                   jax.ShapeDtypeStruct((3, f), F32), jax.ShapeDtypeStruct((3, f), F32),
                   jax.ShapeDtypeStruct((1, f), F32), jax.ShapeDtypeStruct((1, f), F32)],
        scratch_shapes=[], sem=("parallel", "arbitrary"), name=name,
        ins=(uv, uv, uv, ug, ug, ug, dx, dx, w_down, wv, wg, bv, bg))


def _conf_specs(n, d, tm):
    hb = CONV_HALO
    cur = pl.BlockSpec((tm, d), lambda i: (i, 0))
    prev = pl.BlockSpec((hb, d), lambda i: (jnp.maximum(i * (tm // hb) - 1, 0), 0))
    nxt = pl.BlockSpec((hb, d), lambda i: (jnp.minimum((i + 1) * (tm // hb), n // hb - 1), 0))
    taps = pl.BlockSpec((CONV_HALO, d), lambda i: (0, 0))
    vec = pl.BlockSpec((1, d), lambda i: (0, 0))
    return cur, prev, nxt, taps, vec


def _conf_fwd(pa, pg, w, wb, lng, lnb, seq, name, comm=None):
    n, d = pa.shape
    tm = _tile(seq, 256)
    tps = seq // tm
    hb = CONV_HALO
    cur, prev, _, taps, vec = _conf_specs(n, d, tm)

    def body(pap, pac, pgp, pgc, w_ref, wb_ref, lng_ref, lnb_ref, u_ref, s_ref, rot_ref):
        first = (pl.program_id(0) % tps) == 0
        a = jnp.concatenate([jnp.where(first, 0.0, pap[...]), pac[...]], axis=0)
        g = jnp.concatenate([jnp.where(first, 0.0, pgp[...]), pgc[...]], axis=0)
        z = a * _sig(g)
        for b in range(SUBLANES):
            rot_ref[b] = _roll(z, b)
        _conv_taps(rot_ref, w_ref, u_ref, tm, lambda sh: hb - (sh // SUBLANES) * SUBLANES)
        u = u_ref[...] + wb_ref[...]
        mu = jnp.mean(u, axis=-1, keepdims=True)
        uc = u - mu
        var = jnp.mean(uc * uc, axis=-1, keepdims=True)
        ul = uc * lax.rsqrt(var + EPS) * lng_ref[...] + lnb_ref[...]
        u_ref[...] = u
        s_ref[...] = (ul * _sig(ul)).astype(s_ref.dtype)

    return _hosted(
        body, comm, grid=(n // tm,), in_specs=[prev, cur, prev, cur, taps, vec, vec, vec],
        out_specs=[cur, cur],
        out_shape=[jax.ShapeDtypeStruct((n, d), F32), jax.ShapeDtypeStruct((n, d), BF16)],
        scratch_shapes=[pltpu.VMEM((SUBLANES, tm + hb, d), F32)], sem=("arbitrary",), name=name,
        ins=(pa, pa, pg, pg, w, wb, lng, lnb))


def _conf_bwd(u, ds, pa, pg, w, lng, lnb, seq, name, comm=None):
    n, d = u.shape
    tm = _tile(seq, 256)
    tps = seq // tm
    hb = CONV_HALO
    ext = tm + hb
    cur, prev, nxt, taps, vec = _conf_specs(n, d, tm)

    def body(uc_ref, un_ref, dsc_ref, dsn_ref, pap, pac, pgp, pgc, w_ref, lng_ref, lnb_ref,
             dpa_ref, dpg_ref, dw_ref, dwb_ref, dlng_ref, dlnb_ref, dba_ref, dbg_ref, rotz_ref, rotd_ref, dz_ref):
        i = pl.program_id(0)
        first = (i % tps) == 0
        last = (i % tps) == tps - 1

        @pl.when(i == 0)
        def _():
            for r in (dw_ref, dwb_ref, dlng_ref, dlnb_ref, dba_ref, dbg_ref):
                r[...] = jnp.zeros_like(r)

        ue = jnp.concatenate([uc_ref[...], un_ref[...]], axis=0)
        dse = jnp.concatenate([dsc_ref[...], jnp.where(last, 0.0, dsn_ref[...])], axis=0)
        mu = jnp.mean(ue, axis=-1, keepdims=True)
        cen = ue - mu
        r = lax.rsqrt(jnp.mean(cen * cen, axis=-1, keepdims=True) + EPS)
        xn = cen * r
        ul = xn * lng_ref[...] + lnb_ref[...]
        sg = _sig(ul)
        dul = dse * (sg * (1.0 + ul * (1.0 - sg)))
        dun = dul * lng_ref[...]
        du = r * (dun - jnp.mean(dun, axis=-1, keepdims=True) - xn * jnp.mean(dun * xn, axis=-1, keepdims=True))
        dlng_ref[...] += jnp.sum((dul * xn)[:tm], axis=0, keepdims=True)
        dlnb_ref[...] += jnp.sum(dul[:tm], axis=0, keepdims=True)
        dwb_ref[...] += jnp.sum(du[:tm], axis=0, keepdims=True)
        for b in range(SUBLANES):
            rotd_ref[b] = _roll(du, ext - b)
        _conv_taps(rotd_ref, w_ref, dz_ref, tm, lambda sh: (sh // SUBLANES) * SUBLANES)
        dz = dz_ref[...]

        a = jnp.concatenate([jnp.where(first, 0.0, pap[...]), pac[...]], axis=0)
        g = jnp.concatenate([jnp.where(first, 0.0, pgp[...]), pgc[...]], axis=0)
        sgg = _sig(g)
        z = a * sgg
        for b in range(SUBLANES):
            rotz_ref[b] = _roll(z, b)
        _tap_grads(rotz_ref, rotd_ref, dw_ref, tm, hb)

        a_c, sg_c = a[hb:], sgg[hb:]
        da = dz * sg_c
        dg = dz * a_c * sg_c * (1.0 - sg_c)
        dpa_ref[...] = da.astype(dpa_ref.dtype)
        dpg_ref[...] = dg.astype(dpg_ref.dtype)
        dba_ref[...] += jnp.sum(da, axis=0, keepdims=True)
        dbg_ref[...] += jnp.sum(dg, axis=0, keepdims=True)

    vshape = jax.ShapeDtypeStruct((1, d), F32)
    return _hosted(
        body, comm, grid=(n // tm,),
        in_specs=[cur, nxt, cur, nxt, prev, cur, prev, cur, taps, vec, vec],
        out_specs=[cur, cur, taps, vec, vec, vec, vec, vec],
        out_shape=[jax.ShapeDtypeStruct((n, d), BF16), jax.ShapeDtypeStruct((n, d), BF16),
                   jax.ShapeDtypeStruct((CONV_HALO, d), F32), vshape, vshape, vshape, vshape, vshape],
        scratch_shapes=[pltpu.VMEM((SUBLANES, ext, d), F32), pltpu.VMEM((SUBLANES, ext, d), F32),
                        pltpu.VMEM((tm, d), F32)],
        sem=("arbitrary",), name=name, ins=(u, u, ds, ds, pa, pa, pg, pg, w, lng, lnb))


def _pool_specs(n, d, tm, gd):
    hb = POOL_HALO
    cur = pl.BlockSpec((tm, d), lambda i: (i, 0))
    prev = pl.BlockSpec((hb, d), lambda i: (jnp.maximum(i * (tm // hb) - 1, 0), 0))
    nxt = pl.BlockSpec((hb, d), lambda i: (jnp.minimum((i + 1) * (tm // hb), n // hb - 1), 0))
    wsp = pl.BlockSpec((len(POOL_WINDOWS), gd, gd), lambda i: (0, 0, 0))
    vec = pl.BlockSpec((1, d), lambda i: (0, 0))
    return cur, prev, nxt, wsp, vec


def _pool_fwd(x, g, w, b, sc, seq, name):
    n, d = x.shape
    gd = d // len(POOL_WINDOWS)
    tm = _tile(seq, 256)
    tps = seq // tm
    hb = POOL_HALO
    cur, prev, _, wsp, vec = _pool_specs(n, d, tm, gd)

    def body(xp, xc, g_ref, w_ref, b_ref, sc_ref, x1_ref, p_ref):
        i = pl.program_id(0)
        first = (i % tps) == 0
        xe = jnp.concatenate([jnp.where(first, 0.0, xp[...]), xc[...]], axis=0)
        r = lax.rsqrt(jnp.mean(xe * xe, axis=-1, keepdims=True) + EPS)
        h = xe * r * g_ref[...]
        t = ((i % tps) * tm + lax.broadcasted_iota(jnp.int32, (tm, 1), 0) + 1).astype(F32)
        ys = []
        for gi, win in enumerate(POOL_WINDOWS):
            hg = h[:, gi * gd:(gi + 1) * gd]
            s, sh = hg, 1
            while sh < win:
                s = s + _roll(s, sh)
                sh *= 2
            p = (s[hb:] / jnp.minimum(t, float(win)) - hg[hb:]).astype(BF16)
            p_ref[:, gi * gd:(gi + 1) * gd] = p
            ys.append(jnp.dot(p, w_ref[gi], preferred_element_type=F32))
        y = jnp.concatenate(ys, axis=1) + b_ref[...]
        x1_ref[...] = xc[...] + y * sc_ref[...]

    return pl.pallas_call(
        body, grid=(n // tm,), in_specs=[prev, cur, vec, wsp, vec, vec], out_specs=[cur, cur],
        out_shape=[jax.ShapeDtypeStruct((n, d), F32), jax.ShapeDtypeStruct((n, d), BF16)],
        name=name, compiler_params=_cp("arbitrary"),
    )(x, x, g, w, b, sc)


def _pool_bwd(dx1, x, p, g, w, b, sc, seq, name):
    n, d = x.shape
    ng = len(POOL_WINDOWS)
    gd = d // ng
    tm = _tile(seq, 256)
    tps = seq // tm
    hb = POOL_HALO
    ext = tm + hb
    cur, _, nxt, wsp, vec = _pool_specs(n, d, tm, gd)

    def body(dc_ref, dn_ref, x_ref, p_ref, g_ref, w_ref, b_ref, sc_ref, dx_ref, dg_ref, dw_ref, db_ref, dsc_ref):
        i = pl.program_id(0)
        last = (i % tps) == tps - 1

        @pl.when(i == 0)
        def _():
            for r_ in (dg_ref, dw_ref, db_ref, dsc_ref):
                r_[...] = jnp.zeros_like(r_)

        dxc = dc_ref[...]
        dxe = jnp.concatenate([dxc, jnp.where(last, 0.0, dn_ref[...])], axis=0)
        dyg = dxe * sc_ref[...]
        t = ((i % tps) * tm + lax.broadcasted_iota(jnp.int32, (ext, 1), 0) + 1).astype(F32)
        dhs = []
        for gi, win in enumerate(POOL_WINDOWS):
            sl = slice(gi * gd, (gi + 1) * gd)
            dyb = dyg[:, sl].astype(BF16)
            wg = w_ref[gi]
            dp = lax.dot_general(dyb, wg, (((1,), (1,)), ((), ())), preferred_element_type=F32)
            s, sh = dp / jnp.minimum(t, float(win)), 1
            while sh < win:
                s = s + _roll(s, ext - sh)
                sh *= 2
            dhs.append((s - dp)[:tm])
            pg = p_ref[:, sl]
            dw_ref[gi] += lax.dot_general(pg, dyb[:tm], (((0,), (0,)), ((), ())), preferred_element_type=F32)
            ypre = jnp.dot(pg, wg, preferred_element_type=F32) + b_ref[:, sl]
            dsc_ref[:, sl] += jnp.sum(dxc[:, sl] * ypre, axis=0, keepdims=True)
            db_ref[:, sl] += jnp.sum(dyg[:tm, sl], axis=0, keepdims=True)
        dh = jnp.concatenate(dhs, axis=1)
        xv = x_ref[...]
        r = lax.rsqrt(jnp.mean(xv * xv, axis=-1, keepdims=True) + EPS)
        xn = xv * r
        dxn = dh * g_ref[...]
        dx_ref[...] = dxc + r * (dxn - xn * jnp.mean(dxn * xn, axis=-1, keepdims=True))
        dg_ref[...] += jnp.sum(dh * xn, axis=0, keepdims=True)

    vshape = jax.ShapeDtypeStruct((1, d), F32)
    return pl.pallas_call(
        body, grid=(n // tm,), in_specs=[cur, nxt, cur, cur, vec, wsp, vec, vec],
        out_specs=[cur, vec, wsp, vec, vec],
        out_shape=[jax.ShapeDtypeStruct((n, d), F32), vshape, jax.ShapeDtypeStruct((ng, gd, gd), F32), vshape, vshape],
        name=name, compiler_params=_cp("arbitrary"),
    )(dx1, dx1, x, p, g, w, b, sc)


def _head_maps(d):
    hd = lax.broadcasted_iota(jnp.int32, (d, LANES), 0) // HEAD_DIM
    col = lax.broadcasted_iota(jnp.int32, (d, LANES), 1)
    gm = (hd == col).astype(BF16)
    hd_t = lax.broadcasted_iota(jnp.int32, (LANES, d), 1) // HEAD_DIM
    row = lax.broadcasted_iota(jnp.int32, (LANES, d), 0)
    gt = (hd_t == row).astype(BF16)
    return gm, gt


def _dot_split(v, onehot):
    hi = v.astype(BF16)
    lo = (v - hi.astype(F32)).astype(BF16)
    return jnp.dot(hi, onehot, preferred_element_type=F32) + jnp.dot(lo, onehot, preferred_element_type=F32)


def _bias_placement(nh):
    pq = np.zeros((3 * LANES, nh * HEAD_DIM), np.float32)
    pk = np.zeros((3 * LANES, nh * HEAD_DIM), np.float32)
    oq = np.zeros((1, nh * HEAD_DIM), np.float32)
    ok = np.zeros((1, nh * HEAD_DIM), np.float32)
    for h in range(nh):
        for piece in range(3):
            pq[piece * LANES + h, h * HEAD_DIM + piece] = 1.0
            pk[piece * LANES + h, h * HEAD_DIM + 3 + piece] = -1.0
            oq[0, h * HEAD_DIM + 3 + piece] = 1.0
            ok[0, h * HEAD_DIM + piece] = 1.0
    return jnp.asarray(pq, BF16), jnp.asarray(pk, BF16), jnp.asarray(oq), jnp.asarray(ok)


def _fox_prep_fwd(qkv, fl, bf, qg, kg, seq, name):
    n, d3 = qkv.shape
    d = d3 // 3
    nh = d // HEAD_DIM
    tm = _tile(seq, 256)
    tps = seq // tm
    scale = 1.0 / math.sqrt(HEAD_DIM)
    pq, pk, oq, ok = _bias_placement(nh)

    def body(qkv_ref, fl_ref, bf_ref, qg_ref, kg_ref, pq_ref, pk_ref, oq_ref, ok_ref, q_ref, k_ref, v_ref, carry):
        first = (pl.program_id(0) % tps) == 0
        gm, gt = _head_maps(d)

        def head_norm(xr, gain):
            r = lax.rsqrt(_dot_split(xr * xr, gm) / HEAD_DIM + EPS)
            return xr * _dot_split(r, gt) * gain

        qs = (head_norm(qkv_ref[:, :d], qg_ref[...]).astype(BF16).astype(F32) * scale).astype(BF16)
        kn = head_norm(qkv_ref[:, d:2 * d], kg_ref[...]).astype(BF16)
        v_ref[...] = qkv_ref[:, 2 * d:].astype(BF16)
        z = fl_ref[...] + bf_ref[...]
        logf = jnp.minimum(z, 0.0) - jnp.log1p(jnp.exp(-jnp.abs(z)))
        tri = (lax.broadcasted_iota(jnp.int32, (tm, tm), 0) >= lax.broadcasted_iota(jnp.int32, (tm, tm), 1)).astype(F32)

        @pl.when(first)
        def _():
            carry[...] = jnp.zeros_like(carry)

        c = jnp.dot(tri, logf, precision=HI, preferred_element_type=F32) + carry[...]
        carry[...] = c[tm - 1:tm, :]
        c1 = c.astype(BF16)
        r1 = c - c1.astype(F32)
        c2 = r1.astype(BF16)
        c3 = (r1 - c2.astype(F32)).astype(BF16)
        pieces = jnp.concatenate([c1, c2, c3], axis=1)
        eq = (jnp.dot(pieces, pq_ref[...], preferred_element_type=F32) + oq_ref[...]).astype(BF16)
        ek = (jnp.dot(pieces, pk_ref[...], preferred_element_type=F32) + ok_ref[...]).astype(BF16)
        for h in range(nh):
            lo, hi = h * HEAD_DIM, (h + 1) * HEAD_DIM
            q_ref[:, 2 * lo:2 * lo + HEAD_DIM] = qs[:, lo:hi]
            q_ref[:, 2 * lo + HEAD_DIM:2 * hi] = eq[:, lo:hi]
            k_ref[:, 2 * lo:2 * lo + HEAD_DIM] = kn[:, lo:hi]
            k_ref[:, 2 * lo + HEAD_DIM:2 * hi] = ek[:, lo:hi]

    row = lambda w: pl.BlockSpec((tm, w), lambda i: (i, 0))
    vec = lambda w: pl.BlockSpec((1, w), lambda i: (0, 0))
    full = lambda a: pl.BlockSpec(a.shape, lambda i: (0, 0))
    return pl.pallas_call(
        body, grid=(n // tm,),
        in_specs=[row(d3), row(LANES), vec(LANES), vec(d), vec(d), full(pq), full(pk), full(oq), full(ok)],
        out_specs=[row(2 * d), row(2 * d), row(d)],
        out_shape=[jax.ShapeDtypeStruct((n, 2 * d), BF16)] * 2 + [jax.ShapeDtypeStruct((n, d), BF16)],
        scratch_shapes=[pltpu.VMEM((1, LANES), F32)], name=name, compiler_params=_cp("arbitrary"),
    )(qkv, fl, bf, qg, kg, pq, pk, oq, ok)


def _fox_prep_bwd(qkv, dq, dk, dv, dc1, dc2, fl, bf, qg, kg, seq, name, comm=None):
    n, d3 = qkv.shape
    d = d3 // 3
    tm = _tile(seq, 256)
    tps = seq // tm
    nt = n // tm

    def body(qkv_ref, dq_ref, dk_ref, dv_ref, dc1_ref, dc2_ref, fl_ref, bf_ref, qg_ref, kg_ref,
             dqkv_ref, dfl_ref, dqg_ref, dkg_ref, dbf_ref, carry):
        i = pl.program_id(0)
        tile = nt - 1 - i
        last = (tile % tps) == tps - 1
        gm, gt = _head_maps(d)

        @pl.when(i == 0)
        def _():
            for r_ in (dqg_ref, dkg_ref, dbf_ref):
                r_[...] = jnp.zeros_like(r_)

        @pl.when(last)
        def _():
            carry[...] = jnp.zeros_like(carry)

        def head_norm_bwd(xr, dy, gain, dgain_ref):
            rf = _dot_split(lax.rsqrt(_dot_split(xr * xr, gm) / HEAD_DIM + EPS), gt)
            xn = xr * rf
            dgain_ref[...] += jnp.sum(dy * xn, axis=0, keepdims=True)
            dyg = dy * gain
            mean = _dot_split(dyg * xn, gm) / HEAD_DIM
            return rf * (dyg - xn * _dot_split(mean, gt))

        dqkv_ref[:, :d] = head_norm_bwd(qkv_ref[:, :d], dq_ref[...], qg_ref[...], dqg_ref).astype(BF16)
        dqkv_ref[:, d:2 * d] = head_norm_bwd(qkv_ref[:, d:2 * d], dk_ref[...], kg_ref[...], dkg_ref).astype(BF16)
        dqkv_ref[:, 2 * d:] = dv_ref[...].astype(BF16)

        dc = dc1_ref[...] + dc2_ref[...]
        tri = (lax.broadcasted_iota(jnp.int32, (tm, tm), 0) <= lax.broadcasted_iota(jnp.int32, (tm, tm), 1)).astype(F32)
        dlog = jnp.dot(tri, dc, precision=HI, preferred_element_type=F32) + carry[...]
        carry[...] = dlog[0:1, :]
        dfl = dlog * (1.0 - _sig(fl_ref[...] + bf_ref[...]))
        dfl_ref[...] = dfl.astype(BF16)
        dbf_ref[...] += jnp.sum(dfl, axis=0, keepdims=True)

    row = lambda w: pl.BlockSpec((tm, w), lambda i: (nt - 1 - i, 0))
    vec = lambda w: pl.BlockSpec((1, w), lambda i: (0, 0))
    return _hosted(
        body, comm, grid=(nt,),
        in_specs=[row(d3), row(d), row(d), row(d), row(LANES), row(LANES), row(LANES), vec(LANES), vec(d), vec(d)],
        out_specs=[row(d3), row(LANES), vec(d), vec(d), vec(LANES)],
        out_shape=[jax.ShapeDtypeStruct((n, d3), BF16), jax.ShapeDtypeStruct((n, LANES), BF16),
                   jax.ShapeDtypeStruct((1, d), F32), jax.ShapeDtypeStruct((1, d), F32),
                   jax.ShapeDtypeStruct((1, LANES), F32)],
        scratch_shapes=[pltpu.VMEM((1, LANES), F32)], sem=("arbitrary",), name=name,
        ins=(qkv, dq, dk, dv, dc1, dc2, fl, bf, qg, kg))


def _attn_specs(bsz, seq, t):
    nb = seq // t
    blk = lambda w: pl.BlockSpec((t, w), lambda b, h, i: (b * nb + i, h))
    full = lambda w: pl.BlockSpec((seq, w), lambda b, h, i: (b, h))
    col = pl.BlockSpec((None, None, t, 2), lambda b, h, i: (b, h, i, 0))
    rows = pl.BlockSpec((None, None, nb, 2, t), lambda b, h, i: (b, h, 0, 0, 0))
    return nb, blk, full, col, rows


_NT = (((1,), (1,)), ((), ()))
ATTN_TILE = 512


def _head_lanes(t, hh):
    lane = lax.broadcasted_iota(jnp.int32, (t, LANES), 1)
    return (lane < HEAD_DIM) if hh == 0 else (lane >= HEAD_DIM)


def _flash_fwd(qa, ka, v, bsz, seq, name):
    n, d = v.shape
    hp = d // LANES
    t = _tile(seq, ATTN_TILE)
    nb, blk, full, col, _ = _attn_specs(bsz, seq, t)

    def body(q_ref, k_ref, v_ref, o_ref, lse_ref):
        i = pl.program_id(2)
        causal = lax.broadcasted_iota(jnp.int32, (t, t), 0) >= lax.broadcasted_iota(jnp.int32, (t, t), 1)

        def block(j, carry, masked):
            rs = pl.ds(pl.multiple_of(j * t, t), t)
            vj = v_ref[rs, :]
            out = []
            for hh in range(2):
                m, l, acc = carry[hh]
                hs = slice(hh * LANES, (hh + 1) * LANES)
                sc = lax.dot_general(q_ref[:, hs], k_ref[rs, hs], _NT, preferred_element_type=F32)
                if masked:
                    sc = jnp.where(causal, sc, NEG)
                mn = jnp.maximum(m, jnp.max(sc, axis=-1, keepdims=True))
                p = jnp.exp(sc - mn)
                al = jnp.exp(m - mn)
                l = al * l + jnp.sum(p, axis=-1, keepdims=True)
                acc = al * acc + jnp.dot(p.astype(BF16), vj, preferred_element_type=F32)
                out.append((mn, l, acc))
            return tuple(out)

        init = tuple((jnp.full((t, 1), NEG, F32), jnp.zeros((t, 1), F32), jnp.zeros((t, LANES), F32))
                     for _ in range(2))
        carry = lax.fori_loop(0, i, lambda j, c: block(j, c, False), init)
        (m0, l0, a0), (m1, l1, a1) = block(i, carry, True)
        o_ref[...] = jnp.where(_head_lanes(t, 0), a0 / l0, a1 / l1)
        lse_ref[:, 0:1] = m0 + jnp.log(l0)
        lse_ref[:, 1:2] = m1 + jnp.log(l1)

    return pl.pallas_call(
        body, grid=(bsz, hp, nb), in_specs=[blk(2 * LANES), full(2 * LANES), full(LANES)],
        out_specs=[blk(LANES), col],
        out_shape=[jax.ShapeDtypeStruct((n, d), F32), jax.ShapeDtypeStruct((bsz, hp, seq, 2), F32)],
        name=name, compiler_params=_cp("parallel", "parallel", "arbitrary"),
    )(qa, ka, v)


def _flash_bwd(qa, ka, v, do, o, lse_row, bsz, seq, name):
    n, d = v.shape
    hp = d // LANES
    t = _tile(seq, ATTN_TILE)
    nb, blk, full, col, rows = _attn_specs(bsz, seq, t)
    scale = 1.0 / math.sqrt(HEAD_DIM)
    tn_ = (((0,), (0,)), ((), ()))

    def body(k_ref, v_ref, q_ref, do_ref, o_ref, lse_ref, dq_ref, dcc_ref, dk_ref, dv_ref, dck_ref, dqa, dl):
        j = pl.program_id(2)
        causal = lax.broadcasted_iota(jnp.int32, (t, t), 1) >= lax.broadcasted_iota(jnp.int32, (t, t), 0)
        heads = [_head_lanes(t, 0), _head_lanes(t, 1)]

        @pl.when(j == 0)
        def _():
            dqa[...] = jnp.zeros_like(dqa)
            ones = jnp.ones((8, LANES), F32)
            for ib in range(nb):
                rs = slice(ib * t, (ib + 1) * t)
                prod = do_ref[rs, :] * o_ref[rs, :]
                for hh in range(2):
                    dl[ib, hh:hh + 1, :] = lax.dot_general(ones, jnp.where(heads[hh], prod, 0.0), _NT, precision=HI,
                                                           preferred_element_type=F32)[0:1]

        vj = v_ref[...]

        def block(i, carry, masked):
            rs = pl.ds(pl.multiple_of(i * t, t), t)
            doi = do_ref[rs, :]
            dks, dvp = list(carry[:2]), carry[2]
            for hh in range(2):
                hs = slice(hh * LANES, (hh + 1) * LANES)
                kh, qi = k_ref[:, hs], q_ref[rs, hs]
                dom = jnp.where(heads[hh], doi, 0.0).astype(BF16)
                st = lax.dot_general(kh, qi, _NT, preferred_element_type=F32)
                if masked:
                    st = jnp.where(causal, st, NEG)
                pt = jnp.exp(st - lse_ref[i, hh:hh + 1, :])
                dvp = dvp + jnp.dot(pt.astype(BF16), dom, preferred_element_type=F32)
                dpt = lax.dot_general(vj, dom, _NT, preferred_element_type=F32)
                dsb = (pt * (dpt - dl[i, hh:hh + 1, :])).astype(BF16)
                dks[hh] = dks[hh] + jnp.dot(dsb, qi, preferred_element_type=F32)
                dqa[rs, hs] += lax.dot_general(dsb, kh, tn_, preferred_element_type=F32)
            return dks[0], dks[1], dvp

        zero = jnp.zeros((t, LANES), F32)
        carry = block(j, (zero, zero, zero), True)
        dk0, dk1, dvp = lax.fori_loop(j + 1, nb, lambda i, c: block(i, c, False), carry)
        dk_ref[...] = jnp.where(heads[0], dk0, pltpu.roll(dk1, HEAD_DIM, 1))
        dv_ref[...] = dvp
        dck_ref[:, 0:1] = -dk0[:, HEAD_DIM + 3:HEAD_DIM + 4]
        dck_ref[:, 1:2] = -dk1[:, HEAD_DIM + 3:HEAD_DIM + 4]

        @pl.when(j == nb - 1)
        def _():
            first = lax.broadcasted_iota(jnp.int32, (seq, LANES), 1) < HEAD_DIM
            dq_ref[...] = jnp.where(first, dqa[:, :LANES], pltpu.roll(dqa[:, LANES:], HEAD_DIM, 1)) * scale
            for hh in range(2):
                lo = hh * LANES + HEAD_DIM
                dcc_ref[:, hh:hh + 1] = dqa[:, lo:lo + 1]

    whole_col = pl.BlockSpec((None, None, seq, 2), lambda b, h, i: (b, h, 0, 0))
    cshape = jax.ShapeDtypeStruct((bsz, hp, seq, 2), F32)
    nd = jax.ShapeDtypeStruct((n, d), F32)
    return pl.pallas_call(
        body, grid=(bsz, hp, nb),
        in_specs=[blk(2 * LANES), blk(LANES), full(2 * LANES), full(LANES), full(LANES), rows],
        out_specs=[full(LANES), whole_col, blk(LANES), blk(LANES), col],
        out_shape=[nd, cshape, nd, nd, cshape],
        scratch_shapes=[pltpu.VMEM((seq, 2 * LANES), F32), pltpu.VMEM((nb, 2, t), F32)],
        name=name, compiler_params=_cp("parallel", "parallel", "arbitrary"),
    )(ka, v, qa, do, o, lse_row)


def _adamw(w, g, m, v, name):
    r, c = w.shape
    tr = r
    for cand in (512, 256, 128, 64, 32, 16, 8):
        if r % cand == 0 and r > cand and cand * c * 4 <= 4 * 1024 * 1024:
            tr = cand
            break

    def body(w_ref, g_ref, m_ref, v_ref, d_ref, m2_ref, v2_ref):
        gv = g_ref[...]
        m2 = ADAM_B1 * m_ref[...] + (1.0 - ADAM_B1) * gv
        v2 = ADAM_B2 * v_ref[...] + (1.0 - ADAM_B2) * jnp.square(gv)
        m_hat = m2 / (1.0 - ADAM_B1 ** ADAM_STEP)
        v_hat = v2 / (1.0 - ADAM_B2 ** ADAM_STEP)
        d_ref[...] = -ADAM_LR * (m_hat / (jnp.sqrt(v_hat) + ADAM_EPS) + ADAM_WD * w_ref[...])
        m2_ref[...] = m2
        v2_ref[...] = v2

    blk = pl.BlockSpec((tr, c), lambda i: (i, 0))
    shp = jax.ShapeDtypeStruct((r, c), F32)
    return pl.pallas_call(
        body, grid=(r // tr,), in_specs=[blk] * 4, out_specs=[blk] * 3, out_shape=[shp] * 3,
        name=name, compiler_params=_cp("parallel"),
    )(w, g, m, v)


def _adamw_nd(w, g, m, v, name):
    shape = w.shape
    two = (math.prod(shape[:-1]), shape[-1])
    d_, m_, v_ = _adamw(w.reshape(two), g.reshape(two), m.reshape(two), v.reshape(two), name)
    return g.reshape(shape), d_.reshape(shape), m_.reshape(shape), v_.reshape(shape)


def _place():
    x, y, c = lax.axis_index("x"), lax.axis_index("y"), lax.axis_index("c")
    chips = [(1 - x, y), (x, 1 - y), (1 - x, 1 - y)]
    return x, y, c, chips


def _gather_chips(shards):
    nt = len(shards)
    halves = [s.shape[0] // 2 for s in shards]

    def copies(ins, outs, send_sems, recv_sems):
        x, y, c, chips = _place()
        cps = []
        for t in range(nt):
            rows = pl.ds(c * halves[t], halves[t])
            for jj, (cx, cy) in enumerate(chips):
                cps.append(pltpu.make_async_remote_copy(
                    src_ref=ins[t].at[rows, :], dst_ref=outs[t].at[2 * x + y, rows, :], send_sem=send_sems.at[3 * t + jj],
                    recv_sem=recv_sems.at[3 * t + jj], device_id=(cx, cy, c), device_id_type=MESH))
        return cps

    def start(*refs):
        for cp in copies(*refs):
            cp.start()

    def wait(*refs):
        for cp in copies(*refs):
            cp.wait()

    return _Comm(shards, [jax.ShapeDtypeStruct((N_CHIPS,) + s.shape, s.dtype) for s in shards], 3 * nt, start, wait)


def _gather_sibling(shards, bufs, name):
    nt = len(shards)
    halves = [s.shape[0] // 2 for s in shards]

    def body(*refs):
        ins, outs = refs[:nt], refs[2 * nt:3 * nt]
        send_sems, recv_sems = refs[3 * nt:]
        x, y, c, chips = _place()
        sibling = (x, y, 1 - c)

        def copy(t, jj, hf):
            cx, cy = chips[jj]
            region = outs[t].at[2 * cx + cy, pl.ds(hf * halves[t], halves[t]), :]
            return pltpu.make_async_remote_copy(src_ref=region, dst_ref=region, send_sem=send_sems.at[4 * t + jj],
                                                recv_sem=recv_sems.at[4 * t + jj], device_id=sibling,
                                                device_id_type=MESH)

        def own(t):
            return pltpu.make_async_remote_copy(src_ref=ins[t], dst_ref=outs[t].at[2 * x + y],
                                                send_sem=send_sems.at[4 * t + 3], recv_sem=recv_sems.at[4 * t + 3],
                                                device_id=sibling, device_id_type=MESH)

        sends = [copy(t, jj, c) for t in range(nt) for jj in range(3)] + [own(t) for t in range(nt)]
        for cp in sends:
            cp.start()
        for t in range(nt):
            for jj in range(3):
                copy(t, jj, 1 - c).wait_recv()
            own(t).wait_recv()
        for cp in sends:
            cp.wait_send()

    return pl.pallas_call(
        body, in_specs=[ANY] * (2 * nt), out_specs=[ANY] * nt,
        out_shape=[jax.ShapeDtypeStruct(b.shape, b.dtype) for b in bufs],
        scratch_shapes=[pltpu.SemaphoreType.DMA((4 * nt,)), pltpu.SemaphoreType.DMA((4 * nt,))],
        input_output_aliases={nt + t: t for t in range(nt)}, name=name,
        compiler_params=pltpu.CompilerParams(has_side_effects=True),
    )(*shards, *bufs)


def _send_sibling_halves(grads, name):
    nt = len(grads)
    halves = [g.shape[1] // 2 for g in grads]

    def body(*refs):
        ins, outs = refs[:nt], refs[nt:2 * nt]
        send_sems, recv_sems = refs[2 * nt:]
        x, y, c, _ = _place()
        cps = []
        for t in range(nt):
            src = ins[t].at[:, pl.ds((1 - c) * halves[t], halves[t]), :]
            cps.append(pltpu.make_async_remote_copy(src_ref=src, dst_ref=outs[t], send_sem=send_sems.at[t],
                                                    recv_sem=recv_sems.at[t], device_id=(x, y, 1 - c),
                                                    device_id_type=MESH))
        for cp in cps:
            cp.start()
        for cp in cps:
            cp.wait()

    return pl.pallas_call(
        body, in_specs=[ANY] * nt, out_specs=[ANY] * nt,
        out_shape=[jax.ShapeDtypeStruct((N_CHIPS, h, g.shape[2]), g.dtype) for g, h in zip(grads, halves)],
        scratch_shapes=[pltpu.SemaphoreType.DMA((nt,)), pltpu.SemaphoreType.DMA((nt,))],
        name=name, compiler_params=pltpu.CompilerParams(has_side_effects=True),
    )(*grads)


def _add_sibling(g, r1, c_idx, out_dtype, name):
    s, r, w = g.shape
    rh = r // 2
    tr = rh
    for cand in (512, 256, 128, 64, 32, 16, 8):
        if rh % cand == 0 and cand * w * 4 <= 4 * 1024 * 1024:
            tr = cand
            break
    per = rh // tr

    def body(c_ref, g_ref, r_ref, o_ref):
        o_ref[...] = (g_ref[...] + r_ref[...]).astype(o_ref.dtype)

    return pl.pallas_call(
        body,
        grid_spec=pltpu.PrefetchScalarGridSpec(
            num_scalar_prefetch=1, grid=(s, per),
            in_specs=[pl.BlockSpec((None, tr, w), lambda a, b, c_ref: (a, c_ref[0] * per + b, 0)),
                      pl.BlockSpec((None, tr, w), lambda a, b, c_ref: (a, b, 0))],
            out_specs=pl.BlockSpec((None, tr, w), lambda a, b, c_ref: (a, b, 0))),
        out_shape=jax.ShapeDtypeStruct((s, rh, w), out_dtype), name=name,
        compiler_params=_cp("parallel", "parallel"),
    )(c_idx, g, r1)


def _chip_slots(parts):
    nt = len(parts)

    def copies(ins, outs, send_sems, recv_sems):
        x, y, c, chips = _place()
        return [pltpu.make_async_remote_copy(
            src_ref=ins[t].at[2 * cx + cy], dst_ref=outs[t].at[jj], send_sem=send_sems.at[3 * t + jj],
            recv_sem=recv_sems.at[3 * t + jj], device_id=(cx, cy, c), device_id_type=MESH)
            for t in range(nt) for jj, (cx, cy) in enumerate(chips)]

    def start(*refs):
        for cp in copies(*refs):
            cp.start()

    def wait(*refs):
        for cp in copies(*refs):
            cp.wait()

    return _Comm(parts, [jax.ShapeDtypeStruct((3,) + p.shape[1:], p.dtype) for p in parts], 3 * nt, start, wait)


def _add_chips(part, r2, place_idx, name):
    s, rh, w = part.shape
    tr = rh
    for cand in (512, 256, 128, 64, 32, 16, 8):
        if rh % cand == 0 and cand * w * 4 <= 4 * 1024 * 1024:
            tr = cand
            break
    per = rh // tr

    def body(b_ref, p_ref, r0_ref, r1_ref, r2_ref, o_ref):
        up = lambda ref: ref[...].astype(F32)
        o_ref[...] = ((up(p_ref) + up(r0_ref)) + up(r1_ref)) + up(r2_ref)

    return pl.pallas_call(
        body,
        grid_spec=pltpu.PrefetchScalarGridSpec(
            num_scalar_prefetch=1, grid=(per,),
            in_specs=[pl.BlockSpec((None, tr, w), lambda a, b_ref: (b_ref[0], a, 0))]
            + [pl.BlockSpec((None, tr, w), lambda a, b_ref, jj=jj: (jj, a, 0)) for jj in range(3)],
            out_specs=pl.BlockSpec((tr, w), lambda a, b_ref: (b_ref[1] * per + a, 0))),
        out_shape=jax.ShapeDtypeStruct((2 * rh, w), F32), name=name, compiler_params=_cp("parallel"),
    )(place_idx, part, r2, r2, r2)


def _swap_halves(bufs, name):
    nt = len(bufs)

    def body(*refs):
        outs = refs[nt:2 * nt]
        send_sems, recv_sems = refs[2 * nt:]
        x, y, c, _ = _place()

        def copy(t, hf):
            rh = outs[t].shape[0] // 2
            region = outs[t].at[pl.ds(hf * rh, rh), :]
            return pltpu.make_async_remote_copy(src_ref=region, dst_ref=region, send_sem=send_sems.at[t],
                                                recv_sem=recv_sems.at[t], device_id=(x, y, 1 - c),
                                                device_id_type=MESH)

        sends = [copy(t, c) for t in range(nt)]
        for cp in sends:
            cp.start()
        for t in range(nt):
            copy(t, 1 - c).wait_recv()
        for cp in sends:
            cp.wait_send()

    return pl.pallas_call(
        body, in_specs=[ANY] * nt, out_specs=[ANY] * nt,
        out_shape=[jax.ShapeDtypeStruct(b.shape, b.dtype) for b in bufs],
        scratch_shapes=[pltpu.SemaphoreType.DMA((nt,)), pltpu.SemaphoreType.DMA((nt,))],
        input_output_aliases={t: t for t in range(nt)}, name=name,
        compiler_params=pltpu.CompilerParams(has_side_effects=True),
    )(*bufs)


def _rs_begin(grads, wire, c_idx, tag):
    r1 = _send_sibling_halves(grads, name=f"rs_sibling_{tag}")
    return [_add_sibling(g, r, c_idx, wire[t], name=f"rs_add_sibling_{tag}_{t}")
            for t, (g, r) in enumerate(zip(grads, r1))]


def _rs_finish(parts, r2, chip_idx, tag):
    fin = [_add_chips(p, r, chip_idx, name=f"rs_add_chips_{tag}_{t}") for t, (p, r) in enumerate(zip(parts, r2))]
    return _swap_halves(fin, name=f"rs_swap_{tag}")


def _all_reduce_small(v, name):
    r, w = v.shape

    def body(v_ref, o_ref, buf, send_sems, recv_sems):
        x, y, c, _ = _place()
        me = 4 * x + 2 * y + c
        buf[me] = v_ref[...]
        cps = []
        for kk in range(1, 8):
            peer = (x ^ ((kk >> 2) & 1), y ^ ((kk >> 1) & 1), c ^ (kk & 1))
            cps.append(pltpu.make_async_remote_copy(src_ref=v_ref, dst_ref=buf.at[me], send_sem=send_sems.at[kk - 1],
                                                    recv_sem=recv_sems.at[kk - 1], device_id=peer, device_id_type=MESH))
        for cp in cps:
            cp.start()
        for kk in range(1, 8):
            pltpu.make_async_remote_copy(src_ref=v_ref, dst_ref=buf.at[me ^ kk], send_sem=send_sems.at[kk - 1],
                                         recv_sem=recv_sems.at[kk - 1], device_id=(x, y, c),
                                         device_id_type=MESH).wait_recv()
        for cp in cps:
            cp.wait_send()
        acc = buf[0]
        for dev in range(1, 8):
            acc = acc + buf[dev]
        o_ref[...] = acc

    vm = pl.BlockSpec(memory_space=pltpu.VMEM)
    return pl.pallas_call(
        body, in_specs=[vm], out_specs=vm, out_shape=jax.ShapeDtypeStruct((r, w), F32),
        scratch_shapes=[pltpu.VMEM((8, r, w), F32), pltpu.SemaphoreType.DMA((7,)), pltpu.SemaphoreType.DMA((7,))],
        name=name, compiler_params=pltpu.CompilerParams(has_side_effects=True),
    )(v)


def _to_shards(a, axis=-1):
    axis = axis % a.ndim
    shp = a.shape
    a = a.reshape(shp[:axis] + (N_CHIPS, shp[axis] // N_CHIPS) + shp[axis + 1:])
    return jnp.moveaxis(a, axis, 0).reshape(N_CHIPS, -1)


def _from_shards(s, shard_shape, axis=-1):
    axis = axis % len(shard_shape)
    a = jnp.moveaxis(s.reshape((N_CHIPS,) + tuple(shard_shape)), 0, axis)
    return a.reshape(tuple(shard_shape[:axis]) + (N_CHIPS * shard_shape[axis],) + tuple(shard_shape[axis + 1:]))


def _pack(vecs, rows):
    flat = jnp.concatenate([v.reshape(v.shape[0], -1) if v.ndim > 1 else v.reshape(1, -1) for v in vecs], axis=1)
    lead = flat.shape[0]
    flat = jnp.pad(flat, ((0, 0), (0, rows * LANES - flat.shape[1])))
    return flat.reshape(lead, rows, LANES)


def _pack_rows(sizes, mult):
    total = sum(sizes)
    rows = -(-total // LANES)
    return -(-rows // mult) * mult


def _unpack(flat, shapes):
    out, pos = [], 0
    for shp in shapes:
        sz = math.prod(shp)
        out.append(flat[..., pos:pos + sz].reshape(flat.shape[:-1] + tuple(shp)))
        pos += sz
    return out


def _row_layout(col, t):
    bsz, hp, seq, _ = col.shape
    return col.reshape(bsz, hp, seq // t, t, 2).transpose(0, 1, 2, 4, 3)


def _from_col_layout(col):
    bsz, hp, seq, _ = col.shape
    a = col.transpose(0, 2, 1, 3).reshape(bsz * seq, 2 * hp)
    return jnp.pad(a, ((0, 0), (0, LANES - 2 * hp)))


def kernel(x, norm_mix, norm_ffn, conv_w_in, conv_b_in, conv_dw, conv_dw_b, conv_ln_g, conv_ln_b, conv_w_out, conv_b_out, pool_w, pool_b, pool_scale, fox_w_in, fox_b_f, fox_q_gain, fox_k_gain, fox_w_o, ffn_w_up, ffn_dw, ffn_dw_b, ffn_w_down, loss_target, m_norm_mix, m_norm_ffn, m_conv_w_in, m_conv_b_in, m_conv_dw, m_conv_dw_b, m_conv_ln_g, m_conv_ln_b, m_conv_w_out, m_conv_b_out, m_pool_w, m_pool_b, m_pool_scale, m_fox_w_in, m_fox_b_f, m_fox_q_gain, m_fox_k_gain, m_fox_w_o, m_ffn_w_up, m_ffn_dw, m_ffn_dw_b, m_ffn_w_down, v_norm_mix, v_norm_ffn, v_conv_w_in, v_conv_b_in, v_conv_dw, v_conv_dw_b, v_conv_ln_g, v_conv_ln_b, v_conv_w_out, v_conv_b_out, v_pool_w, v_pool_b, v_pool_scale, v_fox_w_in, v_fox_b_f, v_fox_q_gain, v_fox_k_gain, v_fox_w_o, v_ffn_w_up, v_ffn_dw, v_ffn_dw_b, v_ffn_w_down):
    bsz, seq, d = x.shape
    n = bsz * seq
    depth = norm_mix.shape[0]
    n_conv, n_pool, n_fox = conv_w_in.shape[0], pool_w.shape[0], fox_w_in.shape[0]
    f2 = ffn_dw_b.shape[1]
    f = f2 // 2
    nh = d // HEAD_DIM
    hp = d // LANES
    ng = len(POOL_WINDOWS)
    gd = d // ng
    c_idx = lax.axis_index("c").astype(jnp.int32).reshape(1)
    chip_idx = jnp.stack([2 * lax.axis_index("x") + lax.axis_index("y"), lax.axis_index("c")]).astype(jnp.int32)

    small_shapes = [conv_b_in.shape, conv_dw.shape, conv_dw_b.shape, conv_ln_g.shape, conv_ln_b.shape,
                    conv_b_out.shape, pool_b.shape, ffn_dw.shape]
    small_rows = _pack_rows([math.prod(s) for s in small_shapes], 16)
    small = _pack([v.reshape(1, -1) for v in (conv_b_in, conv_dw, conv_dw_b, conv_ln_g, conv_ln_b, conv_b_out,
                                                pool_b, ffn_dw)], small_rows)[0]
    def layer_shards(i):
        kind, j = i % 3, i // 3
        shards = [ffn_w_up[i].astype(BF16), ffn_w_down[i].astype(BF16)]
        if kind == 0:
            shards += [conv_w_in[j].astype(BF16), conv_w_out[j].astype(BF16)]
        elif kind == 1:
            shards += [pool_w[j].reshape(ng * (gd // N_CHIPS), gd).astype(BF16)]
        else:
            shards += [fox_w_in[j].astype(BF16), fox_w_o[j].astype(BF16)]
        if i == 0:
            shards.append(small)
        return shards

    gathered = [None] * depth
    first_now = layer_shards(0)[2:]
    gathered[0] = [None, None] + list(_gather_sibling(
        first_now, _run_comm(_gather_chips(first_now), name="gather_chips_l0"), name="gather_sibling_l0"))
    small_all = gathered[0][-1].reshape(N_CHIPS, -1)
    sm = _unpack(small_all, small_shapes)
    axes = [-1] * 8
    b_in_f, dw_f, dw_b_f, ln_g_f, ln_b_f, b_out_f, pool_b_f, ffn_dw_f = [
        _from_shards(s_.reshape(N_CHIPS, -1), shp, ax) for s_, shp, ax in zip(sm, small_shapes, axes)]

    xs = x.reshape(n, d)
    tgt = loss_target.reshape(n, d)
    vec = lambda a: a.reshape(1, -1)

    saved = []
    cur = xs
    for i in range(depth):
        kind, j = i % 3, i // 3
        wts = gathered[i]
        sv = {"x_in": cur}
        if kind == 0:
            w_in, w_out = wts[2], wts[3].reshape(d, d)
            wcol = w_in.shape[2]
            h = _rms_fwd(cur, vec(norm_mix[i]), name=f"rms_mix_l{i}")
            pa = _mm(h, w_in, m=n, n=d, k=d, tk=d, tn=wcol, b_stk=wcol, b_s0=0, bias=vec(b_in_f[j, :d]),
                     name=f"conv_in_a_l{i}")
            pg = _mm(h, w_in, m=n, n=d, k=d, tk=d, tn=wcol, b_stk=wcol, b_s0=2, bias=vec(b_in_f[j, d:]),
                     name=f"conv_in_g_l{i}")
            taps = jnp.pad(dw_f[j], ((0, CONV_HALO - CONV_WIDTH), (0, 0)))
            conf_args = (pa, pg, taps, vec(dw_b_f[j]), vec(ln_g_f[j]), vec(ln_b_f[j]), seq)
            if i == 0:
                ffn_shards = layer_shards(0)[:2]
                (u, s_), landed = _conf_fwd(*conf_args, name=f"conf_fwd_l{i}", comm=_gather_chips(ffn_shards))
                gathered[0][:2] = _gather_sibling(ffn_shards, landed, name="gather_sibling_ffn_l0")
            else:
                u, s_ = _conf_fwd(*conf_args, name=f"conf_fwd_l{i}")
            cur = _mm(s_, w_out, m=n, n=d, k=d, tk=d, bias=vec(b_out_f[j]), res=cur, name=f"conv_out_l{i}")
            sv.update(h=h, pa=pa, pg=pg, u=u, s=s_, taps=taps)
        elif kind == 1:
            pw = wts[2].reshape(N_CHIPS, ng, gd // N_CHIPS, gd).transpose(1, 0, 2, 3).reshape(ng, gd, gd)
            cur, p = _pool_fwd(cur, vec(norm_mix[i]), pw, vec(pool_b_f[j]), vec(pool_scale[j]), seq,
                               name=f"pool_fwd_l{i}")
            sv.update(p=p, pw=pw)
        else:
            w_in = wts[2].transpose(1, 0, 2).reshape(d, -1)
            w_qkv = w_in[:, :3 * d]
            w_f = jnp.pad(w_in[:, 3 * d:], ((0, 0), (0, LANES - nh)))
            w_o = wts[3].reshape(d, d)
            bf = jnp.pad(vec(fox_b_f[j]), ((0, 0), (0, LANES - nh)))
            qg, kg = jnp.tile(vec(fox_q_gain[j]), (1, nh)), jnp.tile(vec(fox_k_gain[j]), (1, nh))
            h = _rms_fwd(cur, vec(norm_mix[i]), name=f"rms_mix_l{i}")
            qkv = _mm(h, w_qkv, m=n, n=3 * d, k=d, tk=d, tn=d, name=f"fox_qkv_l{i}")
            fl = _mm(h, w_f, m=n, n=LANES, k=d, tk=d, name=f"fox_fl_l{i}")
            qa, ka, v = _fox_prep_fwd(qkv, fl, bf, qg, kg, seq, name=f"fox_prep_l{i}")
            o, lse = _flash_fwd(qa, ka, v, bsz, seq, name=f"fox_attn_l{i}")
            cur = _mm(o, w_o, m=n, n=d, k=d, tk=d, res=cur, name=f"fox_out_l{i}")
            sv.update(h=h, qkv=qkv, fl=fl, qa=qa, ka=ka, v=v, o=o, lse=lse, w_qkv=w_qkv, w_f=w_f, w_o=w_o, bf=bf,
                      qg=qg, kg=kg)
        w_up, w_down = wts[0], wts[1].reshape(f, d)
        ucol = w_up.shape[2]
        sv["x_mid"] = cur
        h2 = _rms_fwd(cur, vec(norm_ffn[i]), name=f"rms_ffn_l{i}")
        uv = _mm(h2, w_up, m=n, n=f, k=d, tk=d, tn=ucol, b_stk=ucol, b_s0=0, name=f"ffn_up_v_l{i}")
        ug = _mm(h2, w_up, m=n, n=f, k=d, tk=d, tn=ucol, b_stk=ucol, b_s0=2, name=f"ffn_up_g_l{i}")
        fdw, fdb = ffn_dw_f[i], ffn_dw_b[i]
        glu_args = (uv, ug, fdw[:, :f], fdw[:, f:], vec(fdb[:f]), vec(fdb[f:]), w_down, cur, seq)
        if i + 1 < depth:
            nxt_shards = layer_shards(i + 1)
            (a_, cur), landed = _ffn_glu_down(*glu_args, name=f"ffn_glu_down_l{i}", comm=_gather_chips(nxt_shards))
            gathered[i + 1] = _gather_sibling(nxt_shards, landed, name=f"gather_sibling_l{i + 1}")
        else:
            a_, cur = _ffn_glu_down(*glu_args, name=f"ffn_glu_down_l{i}")
        sv.update(h2=h2, uv=uv, ug=ug, a=a_)
        saved.append(sv)

    dy, loss_part = _loss(cur, tgt, name="loss")
    loss = lax.psum(loss_part[0, 0], ("x", "y", "c"))

    g_norm_mix, g_norm_ffn = [None] * depth, [None] * depth
    g_ffn_dw_b = [None] * depth
    g_up, g_down = [None] * depth, [None] * depth
    g_small = {}
    g_conv_in, g_conv_out = [None] * n_conv, [None] * n_conv
    g_pool_w = g_fox_in = g_fox_o = None
    g_pool_scale = g_bf = g_qg = g_kg = None
    part_small = {"b_in": [None] * n_conv, "dw": [None] * n_conv, "dw_b": [None] * n_conv, "ln_g": [None] * n_conv,
                  "ln_b": [None] * n_conv, "b_out": [None] * n_conv, "pool_b": None, "ffn_dw": [None] * depth}

    pending = []
    done = {}

    def take_pending():
        groups = list(pending)
        pending.clear()
        parts = [p_ for g_ in groups for p_ in g_[0]]
        return groups, (_chip_slots(parts) if parts else None)

    def finish_groups(groups, r2):
        pos = 0
        for parts, tag, sink in groups:
            sink(_rs_finish(parts, r2[pos:pos + len(parts)], chip_idx, tag))
            pos += len(parts)

    def carried(fn, args, name):
        groups, comm = take_pending()
        if comm is None:
            return fn(*args, name=name)
        outs, r2 = fn(*args, name=name, comm=comm)
        finish_groups(groups, r2)
        return outs

    dcur = dy
    for i in reversed(range(depth)):
        kind, j = i % 3, i // 3
        wts, sv = gathered[i], saved[i]
        w_up, w_down = wts[0], wts[1].reshape(f, d)
        ucol = w_up.shape[2]
        fdw, fdb = ffn_dw_f[i], ffn_dw_b[i]
        dw_down = _mm(sv["a"], dcur, m=f, n=d, k=n, tm=f // 2, tn=d, tk=2048, ta=True, name=f"ffn_down_dw_l{i}")
        duv, dug, dwv, dwg, dbv, dbg = carried(
            _ffn_glu_bwd, (sv["uv"], sv["ug"], dcur, w_down, fdw[:, :f], fdw[:, f:], vec(fdb[:f]), vec(fdb[f:]), seq),
            name=f"ffn_glu_bwd_l{i}")
        dw_up = _mm(sv["h2"], duv, m=d, n=f, k=n, tm=d, tn=ucol, tk=2048, ta=True, o_stk=ucol, o_s0=0,
                    o_slots=N_CHIPS, name=f"ffn_up_dw_v_l{i}")
        dw_up = _mm(sv["h2"], dug, m=d, n=f, k=n, tm=d, tn=ucol, tk=2048, ta=True, o_stk=ucol, o_s0=2,
                    o_slots=N_CHIPS, o_buf=dw_up, name=f"ffn_up_dw_g_l{i}")
        dh2 = _mm(duv, w_up, m=n, n=d, k=f, tn=d, tk=f, tb=True, b_stk=ucol, b_s0=0, name=f"ffn_up_dx_v_l{i}")
        dh2 = _mm(dug, w_up, m=n, n=d, k=f, tn=d, tk=f, tb=True, b_stk=ucol, b_s0=2, res=dh2,
                  name=f"ffn_up_dx_g_l{i}")
        dmid, g_norm_ffn[i] = _rms_bwd(sv["x_mid"], dh2, vec(norm_ffn[i]), dcur, name=f"rms_ffn_bwd_l{i}")
        part_small["ffn_dw"][i] = jnp.concatenate([dwv, dwg], axis=1)
        g_ffn_dw_b[i] = jnp.concatenate([dbv, dbg], axis=1)

        def ffn_sink(red, i=i):
            g_up[i], g_down[i] = red[0], red[1]

        pending.append((_rs_begin([dw_up, dw_down.reshape(N_CHIPS, f // N_CHIPS, d)], [BF16, BF16], c_idx,
                                  tag=f"ffn_l{i}"), f"ffn_l{i}", ffn_sink))

        if kind == 0:
            w_in, w_out = wts[2], wts[3].reshape(d, d)
            wcol = w_in.shape[2]
            ds = _mm(dmid, w_out, m=n, n=d, k=d, tk=d, tb=True, name=f"conv_out_dx_l{i}")
            dw_out = _mm(sv["s"], dmid, m=d, n=d, k=n, tm=d, tn=d, tk=2048, ta=True, name=f"conv_out_dw_l{i}")
            dpa, dpg, ddw, ddwb, dlng, dlnb, dba, dbg_ = carried(
                _conf_bwd, (sv["u"], ds, sv["pa"], sv["pg"], sv["taps"], vec(ln_g_f[j]), vec(ln_b_f[j]), seq),
                name=f"conf_bwd_l{i}")
            dw_in = _mm(sv["h"], dpa, m=d, n=d, k=n, tm=d, tn=wcol, tk=2048, ta=True, o_stk=wcol, o_s0=0,
                        o_slots=N_CHIPS, name=f"conv_in_dw_a_l{i}")
            dw_in = _mm(sv["h"], dpg, m=d, n=d, k=n, tm=d, tn=wcol, tk=2048, ta=True, o_stk=wcol, o_s0=2,
                        o_slots=N_CHIPS, o_buf=dw_in, name=f"conv_in_dw_g_l{i}")
            dh = _mm(dpa, w_in, m=n, n=d, k=d, tn=d, tk=d, tb=True, b_stk=wcol, b_s0=0, name=f"conv_in_dx_a_l{i}")
            dh = _mm(dpg, w_in, m=n, n=d, k=d, tn=d, tk=d, tb=True, b_stk=wcol, b_s0=2, res=dh,
                     name=f"conv_in_dx_g_l{i}")
            dcur, g_norm_mix[i], db_out = _rms_bwd(sv["x_in"], dh, vec(norm_mix[i]), dmid, name=f"rms_mix_bwd_l{i}",
                                                   colsum=True)
            part_small["b_in"][j] = jnp.concatenate([dba, dbg_], axis=1)
            part_small["dw"][j] = ddw[:CONV_WIDTH]
            part_small["dw_b"][j], part_small["ln_g"][j], part_small["ln_b"][j] = ddwb, dlng, dlnb
            part_small["b_out"][j] = db_out
            mix_grads = [dw_in, dw_out.reshape(N_CHIPS, d // N_CHIPS, d)]
        elif kind == 1:
            dcur, g_norm_mix[i], dpw, dpb, dpsc = _pool_bwd(dmid, sv["x_in"], sv["p"], vec(norm_mix[i]), sv["pw"],
                                                            vec(pool_b_f[j]), vec(pool_scale[j]), seq,
                                                            name=f"pool_bwd_l{i}")
            part_small["pool_b"] = dpb
            g_pool_scale = dpsc
            mix_grads = [dpw.reshape(ng, N_CHIPS, gd // N_CHIPS, gd).transpose(1, 0, 2, 3).reshape(N_CHIPS, gd, gd)]
        else:
            do = _mm(dmid, sv["w_o"], m=n, n=d, k=d, tk=d, tb=True, name=f"fox_out_dx_l{i}")
            dw_o = _mm(sv["o"], dmid, m=d, n=d, k=n, tm=d, tn=d, tk=2048, ta=True, name=f"fox_out_dw_l{i}")
            dq, dcc, dk, dv, dck = _flash_bwd(sv["qa"], sv["ka"], sv["v"], do, sv["o"],
                                              _row_layout(sv["lse"], _tile(seq, ATTN_TILE)), bsz, seq,
                                              name=f"fox_attn_bwd_l{i}")
            dqkv, dfl, dqg, dkg, dbf = carried(
                _fox_prep_bwd, (sv["qkv"], dq, dk, dv, _from_col_layout(dcc), _from_col_layout(dck), sv["fl"], sv["bf"],
                                sv["qg"], sv["kg"], seq), name=f"fox_prep_bwd_l{i}")
            dw_qkv = _mm(sv["h"], dqkv, m=d, n=3 * d, k=n, tm=d, tn=d, tk=2048, ta=True, name=f"fox_qkv_dw_l{i}")
            dw_f = _mm(sv["h"], dfl, m=d, n=LANES, k=n, tm=d, tk=2048, ta=True, name=f"fox_fl_dw_l{i}")
            dh = _mm(dqkv, sv["w_qkv"], m=n, n=d, k=3 * d, tn=d, tk=d, tb=True, name=f"fox_qkv_dx_l{i}")
            dh = _mm(dfl, sv["w_f"], m=n, n=d, k=LANES, tn=d, tb=True, res=dh, name=f"fox_fl_dx_l{i}")
            dcur, g_norm_mix[i] = _rms_bwd(sv["x_in"], dh, vec(norm_mix[i]), dmid, name=f"rms_mix_bwd_l{i}")
            g_bf = dbf[:, :nh]
            g_qg = dqg.reshape(nh, HEAD_DIM).sum(axis=0, keepdims=True)
            g_kg = dkg.reshape(nh, HEAD_DIM).sum(axis=0, keepdims=True)
            dw_in_full = jnp.concatenate([dw_qkv, dw_f[:, :nh]], axis=1)
            wshard = dw_in_full.shape[1] // N_CHIPS
            mix_grads = [dw_in_full.reshape(d, N_CHIPS, wshard).transpose(1, 0, 2),
                         dw_o.reshape(N_CHIPS, d // N_CHIPS, d)]
        wire = [BF16] * len(mix_grads)
        if i == 0:
            sm_parts = [jnp.concatenate(part_small["b_in"]), jnp.stack(part_small["dw"]),
                        jnp.concatenate(part_small["dw_b"]), jnp.concatenate(part_small["ln_g"]),
                        jnp.concatenate(part_small["ln_b"]), jnp.concatenate(part_small["b_out"]),
                        part_small["pool_b"].reshape(n_pool, ng, gd), jnp.stack(part_small["ffn_dw"])]
            mix_grads.append(_pack([_to_shards(p_) for p_ in sm_parts], small_rows))
            wire.append(F32)

        def mix_sink(red, i=i, kind=kind, j=j):
            if kind == 0:
                g_conv_in[j], g_conv_out[j] = red[0], red[1]
            elif kind == 1:
                done["pool_w"] = red[0]
            else:
                done["fox_in"], done["fox_o"] = red[0], red[1]
            if i == 0:
                done["small"] = red[-1].reshape(-1)

        pending.append((_rs_begin(mix_grads, wire, c_idx, tag=f"mix_l{i}"), f"mix_l{i}", mix_sink))

    groups, comm = take_pending()
    finish_groups(groups, _run_comm(comm, name="rs_chips_tail"))
    g_pool_w, g_fox_in, g_fox_o, g_small_flat = done["pool_w"], done["fox_in"], done["fox_o"], done["small"]

    grad_x = dcur.reshape(bsz, seq, d)

    rep_parts = [jnp.concatenate(g_norm_mix), jnp.concatenate(g_norm_ffn), g_pool_scale, g_bf, g_qg, g_kg,
                 jnp.concatenate(g_ffn_dw_b)]
    rep_shapes = [norm_mix.shape, norm_ffn.shape, pool_scale.shape, fox_b_f.shape, fox_q_gain.shape,
                  fox_k_gain.shape, ffn_dw_b.shape]
    rep_rows = _pack_rows([math.prod(s) for s in rep_shapes], 8)
    rep = _all_reduce_small(_pack([p_.reshape(1, -1) for p_ in rep_parts], rep_rows)[0], name="all_reduce_small")
    g_rep = _unpack(rep.reshape(-1), rep_shapes)
    g_sm = _unpack(g_small_flat, small_shapes)

    grads = {
        "norm_mix": g_rep[0], "norm_ffn": g_rep[1],
        "conv_w_in": jnp.stack(g_conv_in), "conv_b_in": g_sm[0], "conv_dw": g_sm[1], "conv_dw_b": g_sm[2],
        "conv_ln_g": g_sm[3], "conv_ln_b": g_sm[4], "conv_w_out": jnp.stack(g_conv_out), "conv_b_out": g_sm[5],
        "pool_w": g_pool_w.reshape(pool_w.shape), "pool_b": g_sm[6], "pool_scale": g_rep[2],
        "fox_w_in": g_fox_in.reshape(fox_w_in.shape), "fox_b_f": g_rep[3], "fox_q_gain": g_rep[4],
        "fox_k_gain": g_rep[5], "fox_w_o": g_fox_o.reshape(fox_w_o.shape),
        "ffn_w_up": jnp.stack(g_up), "ffn_dw": g_sm[7], "ffn_dw_b": g_rep[6], "ffn_w_down": jnp.stack(g_down),
    }
    weights = dict(norm_mix=norm_mix, norm_ffn=norm_ffn, conv_w_in=conv_w_in, conv_b_in=conv_b_in, conv_dw=conv_dw,
                   conv_dw_b=conv_dw_b, conv_ln_g=conv_ln_g, conv_ln_b=conv_ln_b, conv_w_out=conv_w_out,
                   conv_b_out=conv_b_out, pool_w=pool_w, pool_b=pool_b, pool_scale=pool_scale, fox_w_in=fox_w_in,
                   fox_b_f=fox_b_f, fox_q_gain=fox_q_gain, fox_k_gain=fox_k_gain, fox_w_o=fox_w_o, ffn_w_up=ffn_w_up,
                   ffn_dw=ffn_dw, ffn_dw_b=ffn_dw_b, ffn_w_down=ffn_w_down)
    m_in = dict(norm_mix=m_norm_mix, norm_ffn=m_norm_ffn, conv_w_in=m_conv_w_in, conv_b_in=m_conv_b_in,
                conv_dw=m_conv_dw, conv_dw_b=m_conv_dw_b, conv_ln_g=m_conv_ln_g, conv_ln_b=m_conv_ln_b,
                conv_w_out=m_conv_w_out, conv_b_out=m_conv_b_out, pool_w=m_pool_w, pool_b=m_pool_b,
                pool_scale=m_pool_scale, fox_w_in=m_fox_w_in, fox_b_f=m_fox_b_f, fox_q_gain=m_fox_q_gain,
                fox_k_gain=m_fox_k_gain, fox_w_o=m_fox_w_o, ffn_w_up=m_ffn_w_up, ffn_dw=m_ffn_dw,
                ffn_dw_b=m_ffn_dw_b, ffn_w_down=m_ffn_w_down)
    v_in = dict(norm_mix=v_norm_mix, norm_ffn=v_norm_ffn, conv_w_in=v_conv_w_in, conv_b_in=v_conv_b_in,
                conv_dw=v_conv_dw, conv_dw_b=v_conv_dw_b, conv_ln_g=v_conv_ln_g, conv_ln_b=v_conv_ln_b,
                conv_w_out=v_conv_w_out, conv_b_out=v_conv_b_out, pool_w=v_pool_w, pool_b=v_pool_b,
                pool_scale=v_pool_scale, fox_w_in=v_fox_w_in, fox_b_f=v_fox_b_f, fox_q_gain=v_fox_q_gain,
                fox_k_gain=v_fox_k_gain, fox_w_o=v_fox_w_o, ffn_w_up=v_ffn_w_up, ffn_dw=v_ffn_dw,
                ffn_dw_b=v_ffn_dw_b, ffn_w_down=v_ffn_w_down)
    names = list(weights)
    g_out, d_out, m_out, v_out = [], [], [], []
    for nm in names:
        g_, dl_, m_, v_ = _adamw_nd(weights[nm], grads[nm].reshape(weights[nm].shape), m_in[nm], v_in[nm],
                                    name=f"adamw_{nm}")
        g_out.append(g_)
        d_out.append(dl_)
        m_out.append(m_)
        v_out.append(v_)
    return (loss, grad_x, *g_out, *d_out, *m_out, *v_out)
```

```python
import math

import jax
import jax.numpy as jnp
import numpy as np
from jax import lax
from jax.experimental import pallas as pl
from jax.experimental.pallas import tpu as pltpu

F32 = jnp.float32
BF16 = jnp.bfloat16
HI = lax.Precision.HIGHEST
MESH = pl.DeviceIdType.MESH
ANY = pl.BlockSpec(memory_space=pl.ANY)

EPS = 1e-6
HEAD_DIM = 64
LANES = 128
POOL_WINDOWS = (2, 4, 8, 16)
CONV_WIDTH = 31
CONV_HALO = 32
FFN_HALO = 8
FFN_ROWS, FFN_COLS = 256, 1408
FFN_DOWN_ROWS, FFN_DOWN_COLS = 128, 2816
POOL_HALO = 16
N_CHIPS = 4
NEG = -1e30

ADAM_LR = 0.001
ADAM_B1 = 0.9
ADAM_B2 = 0.999
ADAM_EPS = 1e-08
ADAM_WD = 0.01
ADAM_STEP = 10

V7X_VMEM_LIMIT_BYTES = 56 * 1024 * 1024


def _cp(*sem):
    return pltpu.CompilerParams(dimension_semantics=sem or None, vmem_limit_bytes=V7X_VMEM_LIMIT_BYTES)


def _tile(n, pref):
    t = min(n, pref)
    assert n % t == 0, (n, pref)
    return t


def _sig(v):
    return jax.nn.sigmoid(v)


def _roll(v, shift):
    n = v.shape[0]
    shift = shift % n
    return v if shift == 0 else pltpu.roll(v, shift, 0)


SUBLANES = 8


CONV_ROWS = 64


def _conv_taps(rot_ref, w_ref, out_ref, tm, start_of):
    d = out_ref.shape[1]
    for lc in range(d // LANES):
        ls = slice(lc * LANES, (lc + 1) * LANES)

        for r0 in range(0, tm, CONV_ROWS):
            acc = None
            for sh in range(CONV_WIDTH):
                kk = CONV_WIDTH - 1 - sh
                lo = r0 + start_of(sh)
                term = w_ref[kk:kk + 1, ls] * rot_ref[sh % SUBLANES, lo:lo + CONV_ROWS, ls]
                acc = term if acc is None else acc + term
            out_ref[r0:r0 + CONV_ROWS, ls] = acc


def _tap_grads(rotz_ref, rotd_ref, dw_ref, tm, halo):
    d = dw_ref.shape[1]
    for lc in range(d // LANES):
        ls = slice(lc * LANES, (lc + 1) * LANES)

        acc = [None] * CONV_WIDTH
        for r0 in range(0, tm, CONV_ROWS):
            duc = rotd_ref[0, r0:r0 + CONV_ROWS, ls]
            for sh in range(CONV_WIDTH):
                lo = r0 + halo - (sh // SUBLANES) * SUBLANES
                prod = duc * rotz_ref[sh % SUBLANES, lo:lo + CONV_ROWS, ls]
                part = prod.reshape(CONV_ROWS // SUBLANES, SUBLANES, LANES).sum(axis=0)
                acc[sh] = part if acc[sh] is None else acc[sh] + part
        for sh in range(CONV_WIDTH):
            kk = CONV_WIDTH - 1 - sh
            dw_ref[kk:kk + 1, ls] += jnp.sum(acc[sh], axis=0, keepdims=True)


class _Comm:
    def __init__(self, ins, out_shapes, n_sems, start, wait):
        self.ins, self.out_shapes, self.n_sems, self.start, self.wait = list(ins), list(out_shapes), n_sems, start, wait


def _hosted(body, comm, *, grid, in_specs, out_specs, out_shape, scratch_shapes, sem, name, ins):
    if comm is None:
        return pl.pallas_call(body, grid=grid, in_specs=in_specs, out_specs=out_specs, out_shape=out_shape,
                              scratch_shapes=scratch_shapes, name=name, compiler_params=_cp(*sem))(*ins)
    n_in, n_out, n_scr = len(in_specs), len(out_specs), len(scratch_shapes)
    nci, nco = len(comm.ins), len(comm.out_shapes)

    def wrapped(*refs):
        pos = [0]

        def take(cnt):
            pos[0] += cnt
            return refs[pos[0] - cnt:pos[0]]

        r_in, c_in, r_out, c_out, r_scr = take(n_in), take(nci), take(n_out), take(nco), take(n_scr)
        send_sems, recv_sems = take(2)
        ids = [pl.program_id(ax) for ax in range(len(grid))]
        first, last = ids[0] == 0, ids[0] == grid[0] - 1
        for ax in range(1, len(grid)):
            first = jnp.logical_and(first, ids[ax] == 0)
            last = jnp.logical_and(last, ids[ax] == grid[ax] - 1)

        @pl.when(first)
        def _():
            comm.start(c_in, c_out, send_sems, recv_sems)

        body(*r_in, *r_out, *r_scr)

        @pl.when(last)
        def _():
            comm.wait(c_in, c_out, send_sems, recv_sems)

    outs = pl.pallas_call(
        wrapped, grid=grid, in_specs=list(in_specs) + [ANY] * nci, out_specs=list(out_specs) + [ANY] * nco,
        out_shape=list(out_shape) + comm.out_shapes,
        scratch_shapes=list(scratch_shapes) + [pltpu.SemaphoreType.DMA((comm.n_sems,))] * 2, name=name,
        compiler_params=pltpu.CompilerParams(dimension_semantics=sem, vmem_limit_bytes=V7X_VMEM_LIMIT_BYTES,
                                             has_side_effects=True),
    )(*ins, *comm.ins)
    return list(outs[:n_out]), list(outs[n_out:])


def _run_comm(comm, name):
    def body(*refs):
        nci, nco = len(comm.ins), len(comm.out_shapes)
        c_in, c_out, send_sems, recv_sems = refs[:nci], refs[nci:nci + nco], refs[-2], refs[-1]
        comm.start(c_in, c_out, send_sems, recv_sems)
        comm.wait(c_in, c_out, send_sems, recv_sems)

    return pl.pallas_call(
        body, in_specs=[ANY] * len(comm.ins), out_specs=[ANY] * len(comm.out_shapes), out_shape=comm.out_shapes,
        scratch_shapes=[pltpu.SemaphoreType.DMA((comm.n_sems,))] * 2, name=name,
        compiler_params=pltpu.CompilerParams(has_side_effects=True),
    )(*comm.ins)


def _mm(a, b, *, m, n, k, name, tm=1024, tn=1024, tk=512, ta=False, tb=False, b_stk=None, b_s0=0,
        o_stk=None, o_s0=0, o_slots=None, o_buf=None, bias=None, res=None, out_dtype=F32):
    tm, tn, tk = _tile(m, tm), _tile(n, tn), _tile(k, tk)
    gi, gj, gk = m // tm, n // tn, k // tk
    nsl = 1
    a_spec = pl.BlockSpec((tk, tm), lambda j, i, kk: (kk, i)) if ta else pl.BlockSpec((tm, tk), lambda j, i, kk: (i, kk))
    if b_stk is None:
        b_spec = pl.BlockSpec((tn, tk), lambda j, i, kk: (j, kk)) if tb else pl.BlockSpec((tk, tn), lambda j, i, kk: (kk, j))
    elif tb and tk > b_stk:
        assert tk % b_stk == 0 and b_s0 % (tk // b_stk) == 0 and not ta
        nsl = tk // b_stk
        b_spec = pl.BlockSpec((nsl, tn, b_stk), lambda j, i, kk: (b_s0 // nsl + kk, j, 0))
    elif tb:
        assert b_stk % tk == 0
        per = b_stk // tk
        b_spec = pl.BlockSpec((None, tn, tk), lambda j, i, kk: (b_s0 + kk // per, j, kk % per))
    else:
        assert b_stk % tn == 0
        per = b_stk // tn
        b_spec = pl.BlockSpec((None, tk, tn), lambda j, i, kk: (b_s0 + j // per, kk, j % per))
    ins, in_specs = [a, b], [a_spec, b_spec]
    if bias is not None:
        ins.append(bias)
        in_specs.append(pl.BlockSpec((1, tn), lambda j, i, kk: (0, j)))
    if res is not None:
        ins.append(res)
        in_specs.append(pl.BlockSpec((tm, tn), lambda j, i, kk: (i, j)))
    aliases = {}
    if o_stk is None:
        out_shape = jax.ShapeDtypeStruct((m, n), out_dtype)
        o_spec = pl.BlockSpec((tm, tn), lambda j, i, kk: (i, j))
    else:
        assert o_stk % tn == 0
        pero = o_stk // tn
        out_shape = jax.ShapeDtypeStruct((o_slots, m, o_stk), out_dtype)
        o_spec = pl.BlockSpec((None, tm, tn), lambda j, i, kk: (o_s0 + j // pero, i, j % pero))
        if o_buf is not None:
            aliases = {len(ins): 0}
            ins.append(o_buf)
            in_specs.append(ANY)
    has_bias, has_res, has_buf = bias is not None, res is not None, o_buf is not None
    dn = (((0 if ta else 1,), (1 if tb else 0,)), ((), ()))

    def body(*refs):
        a_ref, b_ref = refs[0], refs[1]
        pos = 2
        bias_ref = refs[pos] if has_bias else None
        pos += has_bias
        res_ref = refs[pos] if has_res else None
        pos += has_res + has_buf
        o_ref = refs[pos]
        if nsl == 1:
            p = lax.dot_general(a_ref[...].astype(BF16), b_ref[...].astype(BF16), dn, preferred_element_type=F32)
        else:
            p = sum(lax.dot_general(a_ref[:, s * b_stk:(s + 1) * b_stk].astype(BF16), b_ref[s].astype(BF16), dn,
                                    preferred_element_type=F32) for s in range(nsl))

        def finish(acc):
            if has_bias:
                acc = acc + bias_ref[...]
            if has_res:
                acc = acc + res_ref[...]
            o_ref[...] = acc.astype(o_ref.dtype)

        if gk == 1:
            finish(p)
        else:
            acc_ref = refs[pos + 1]
            kk = pl.program_id(2)

            @pl.when(kk == 0)
            def _():
                acc_ref[...] = p

            @pl.when(kk > 0)
            def _():
                acc_ref[...] += p

            @pl.when(kk == gk - 1)
            def _():
                finish(acc_ref[...])

    return pl.pallas_call(
        body, grid=(gj, gi, gk), in_specs=in_specs, out_specs=o_spec, out_shape=out_shape,
        scratch_shapes=[pltpu.VMEM((tm, tn), F32)] if gk > 1 else [],
        input_output_aliases=aliases, name=name,
        compiler_params=_cp("parallel", "parallel", "arbitrary"),
    )(*ins)


def _rms_fwd(x, g, name):
    n, d = x.shape
    tm = _tile(n, 512)

    def body(x_ref, g_ref, h_ref):
        xv = x_ref[...]
        r = lax.rsqrt(jnp.mean(xv * xv, axis=-1, keepdims=True) + EPS)
        h_ref[...] = (xv * r * g_ref[...]).astype(h_ref.dtype)

    return pl.pallas_call(
        body, grid=(n // tm,),
        in_specs=[pl.BlockSpec((tm, d), lambda i: (i, 0)), pl.BlockSpec((1, d), lambda i: (0, 0))],
        out_specs=pl.BlockSpec((tm, d), lambda i: (i, 0)),
        out_shape=jax.ShapeDtypeStruct((n, d), BF16), name=name, compiler_params=_cp("arbitrary"),
    )(x, g)


def _rms_bwd(x, dh, g, dres, name, colsum=False):
    n, d = x.shape
    tm = _tile(n, 512)

    def body(x_ref, dh_ref, g_ref, dres_ref, dx_ref, dg_ref, *rest):
        i = pl.program_id(0)
        xv, dhv = x_ref[...], dh_ref[...]
        r = lax.rsqrt(jnp.mean(xv * xv, axis=-1, keepdims=True) + EPS)
        xn = xv * r
        dxn = dhv * g_ref[...]
        dx_ref[...] = dres_ref[...] + r * (dxn - xn * jnp.mean(dxn * xn, axis=-1, keepdims=True))
        dg = jnp.sum(dhv * xn, axis=0, keepdims=True)

        @pl.when(i == 0)
        def _():
            dg_ref[...] = jnp.zeros_like(dg_ref)
            if colsum:
                rest[0][...] = jnp.zeros_like(rest[0])

        dg_ref[...] += dg
        if colsum:
            rest[0][...] += jnp.sum(dres_ref[...], axis=0, keepdims=True)

    row = pl.BlockSpec((tm, d), lambda i: (i, 0))
    vec = pl.BlockSpec((1, d), lambda i: (0, 0))
    out_shape = [jax.ShapeDtypeStruct((n, d), F32), jax.ShapeDtypeStruct((1, d), F32)]
    out_specs = [row, vec]
    if colsum:
        out_shape.append(jax.ShapeDtypeStruct((1, d), F32))
        out_specs.append(vec)
    return pl.pallas_call(
        body, grid=(n // tm,), in_specs=[row, row, vec, row], out_specs=out_specs, out_shape=out_shape,
        name=name, compiler_params=_cp("arbitrary"),
    )(x, dh, g, dres)


def _loss(y, tgt, name):
    n, d = y.shape
    tm = _tile(n, 512)

    def body(y_ref, t_ref, dy_ref, l_ref):
        i = pl.program_id(0)
        e = y_ref[...] - t_ref[...]
        dy_ref[...] = e / d
        part = 0.5 * jnp.sum(jnp.mean(e * e, axis=-1, keepdims=True), axis=0, keepdims=True)

        @pl.when(i == 0)
        def _():
            l_ref[...] = jnp.zeros_like(l_ref)

        l_ref[...] += part

    row = pl.BlockSpec((tm, d), lambda i: (i, 0))
    return pl.pallas_call(
        body, grid=(n // tm,), in_specs=[row, row],
        out_specs=[row, pl.BlockSpec((1, 1), lambda i: (0, 0))],
        out_shape=[jax.ShapeDtypeStruct((n, d), F32), jax.ShapeDtypeStruct((1, 1), F32)],
        name=name, compiler_params=_cp("arbitrary"),
    )(y, tgt)


def _ffn_specs(n, f, tm, tc, seq):
    hb = FFN_HALO
    cur = pl.BlockSpec((tm, tc), lambda j, i: (i, j))
    prev = pl.BlockSpec((hb, tc), lambda j, i: (jnp.maximum(i * (tm // hb) - 1, 0), j))
    nxt = pl.BlockSpec((hb, tc), lambda j, i: (jnp.minimum((i + 1) * (tm // hb), n // hb - 1), j))
    taps = pl.BlockSpec((3, tc), lambda j, i: (0, j))
    vec = pl.BlockSpec((1, tc), lambda j, i: (0, j))
    return cur, prev, nxt, taps, vec


def _ffn_glu_down(uv, ug, wv, wg, bv, bg, w_down, res, seq, name, comm=None):
    n, f = uv.shape
    d = w_down.shape[1]
    tm, tc = _tile(seq, FFN_DOWN_ROWS), _tile(f, FFN_DOWN_COLS)
    tps = seq // tm
    nj = f // tc
    hb = FFN_HALO
    cur = pl.BlockSpec((tm, tc), lambda i, j: (i, j))
    prev = pl.BlockSpec((hb, tc), lambda i, j: (jnp.maximum(i * (tm // hb) - 1, 0), j))
    taps = pl.BlockSpec((3, tc), lambda i, j: (0, j))
    vec = pl.BlockSpec((1, tc), lambda i, j: (0, j))
    wblk = pl.BlockSpec((tc, d), lambda i, j: (j, 0))
    row = pl.BlockSpec((tm, d), lambda i, j: (i, 0))

    def body(uvp, uvc, ugp, ugc, wv_ref, wg_ref, bv_ref, bg_ref, wd_ref, res_ref, a_ref, x_ref, acc_ref):
        first = (pl.program_id(0) % tps) == 0
        j = pl.program_id(1)

        def conv(p_ref, c_ref, w_ref, b_ref):
            xs = jnp.concatenate([jnp.where(first, 0.0, p_ref[...]), c_ref[...]], axis=0)
            w = w_ref[...]
            y = w[2:3] * xs + w[1:2] * _roll(xs, 1) + w[0:1] * _roll(xs, 2)
            return y[FFN_HALO:] + b_ref[...]

        val = conv(uvp, uvc, wv_ref, bv_ref)
        gate = conv(ugp, ugc, wg_ref, bg_ref)
        a = (gate * _sig(gate) * val).astype(BF16)
        a_ref[...] = a
        p = jnp.dot(a, wd_ref[...], preferred_element_type=F32)

        @pl.when(j == 0)
        def _():
            acc_ref[...] = res_ref[...] + p

        @pl.when(j > 0)
        def _():
            acc_ref[...] += p

        @pl.when(j == nj - 1)
        def _():
            x_ref[...] = acc_ref[...]

    return _hosted(
        body, comm, grid=(n // tm, nj), in_specs=[prev, cur, prev, cur, taps, taps, vec, vec, wblk, row],
        out_specs=[cur, row], out_shape=[jax.ShapeDtypeStruct((n, f), BF16), jax.ShapeDtypeStruct((n, d), F32)],
        scratch_shapes=[pltpu.VMEM((tm, d), F32)], sem=("parallel", "arbitrary"), name=name,
        ins=(uv, uv, ug, ug, wv, wg, bv, bg, w_down, res))


def _ffn_glu_bwd(uv, ug, dx, w_down, wv, wg, bv, bg, seq, name, comm=None):
    n, f = uv.shape
    d = dx.shape[1]
    tm, tc = _tile(seq, FFN_ROWS), _tile(f, FFN_COLS)
    tps = seq // tm
    hb = FFN_HALO
    ext = tm + hb
    cur, prev, nxt, taps, vec = _ffn_specs(n, f, tm, tc, seq)
    dx_cur = pl.BlockSpec((tm, d), lambda j, i: (i, 0))
    dx_nxt = pl.BlockSpec((hb, d), lambda j, i: (jnp.minimum((i + 1) * (tm // hb), n // hb - 1), 0))
    wblk = pl.BlockSpec((tc, d), lambda j, i: (j, 0))

    def body(uvp, uvc, uvn, ugp, ugc, ugn, dx_c, dx_n, wd_ref, wv_ref, wg_ref, bv_ref, bg_ref,
             duv_ref, dug_ref, dwv_ref, dwg_ref, dbv_ref, dbg_ref):
        i = pl.program_id(1)
        first = (i % tps) == 0
        last = (i % tps) == tps - 1
        dx_e = jnp.concatenate([dx_c[...], jnp.where(last, 0.0, dx_n[...])], axis=0).astype(BF16)
        da_e = lax.dot_general(dx_e, wd_ref[...], _NT, preferred_element_type=F32)

        def taps3(p_ref, c_ref, n_ref):
            xs = jnp.concatenate([jnp.where(first, 0.0, p_ref[...]), c_ref[...], n_ref[...]], axis=0)
            return xs, _roll(xs, 1), _roll(xs, 2)

        xv, xg = taps3(uvp, uvc, uvn), taps3(ugp, ugc, ugn)
        wv_, wg_ = wv_ref[...], wg_ref[...]

        def conv(xs, w, b_ref):
            return (w[2:3] * xs[0] + w[1:2] * xs[1] + w[0:1] * xs[2])[hb:] + b_ref[...]

        val, gate = conv(xv, wv_, bv_ref), conv(xg, wg_, bg_ref)
        sg = _sig(gate)
        dval = da_e * (gate * sg)
        dgate = da_e * val * (sg * (1.0 + gate * (1.0 - sg)))

        def conv_t(dv, w):
            return (w[2:3] * dv + w[1:2] * _roll(dv, ext - 1) + w[0:1] * _roll(dv, ext - 2))[:tm]

        duv_ref[...] = conv_t(dval, wv_).astype(duv_ref.dtype)
        dug_ref[...] = conv_t(dgate, wg_).astype(dug_ref.dtype)

        def tap_grads(d_own, xs):
            return jnp.concatenate(
                [jnp.sum(d_own * xs[2 - kk][hb:hb + tm], axis=0, keepdims=True) for kk in range(3)], axis=0)

        dv_own, dg_own = dval[:tm], dgate[:tm]

        @pl.when(i == 0)
        def _():
            for r in (dwv_ref, dwg_ref, dbv_ref, dbg_ref):
                r[...] = jnp.zeros_like(r)

        dwv_ref[...] += tap_grads(dv_own, xv)
        dwg_ref[...] += tap_grads(dg_own, xg)
        dbv_ref[...] += jnp.sum(dv_own, axis=0, keepdims=True)
        dbg_ref[...] += jnp.sum(dg_own, axis=0, keepdims=True)

    return _hosted(
        body, comm, grid=(f // tc, n // tm),
        in_specs=[prev, cur, nxt, prev, cur, nxt, dx_cur, dx_nxt, wblk, taps, taps, vec, vec],
        out_specs=[cur, cur, taps, taps, vec, vec],
        out_shape=[jax.ShapeDtypeStruct((n, f), BF16), jax.ShapeDtypeStruct((n, f), BF16),
                   jax.ShapeDtypeStruct((3, f), F32), jax.ShapeDtypeStruct((3, f), F32),
                   jax.ShapeDtypeStruct((1, f), F32), jax.ShapeDtypeStruct((1, f), F32)],
        scratch_shapes=[], sem=("parallel", "arbitrary"), name=name,
        ins=(uv, uv, uv, ug, ug, ug, dx, dx, w_down, wv, wg, bv, bg))


def _conf_specs(n, d, tm):
    hb = CONV_HALO
    cur = pl.BlockSpec((tm, d), lambda i: (i, 0))
    prev = pl.BlockSpec((hb, d), lambda i: (jnp.maximum(i * (tm // hb) - 1, 0), 0))
    nxt = pl.BlockSpec((hb, d), lambda i: (jnp.minimum((i + 1) * (tm // hb), n // hb - 1), 0))
    taps = pl.BlockSpec((CONV_HALO, d), lambda i: (0, 0))
    vec = pl.BlockSpec((1, d), lambda i: (0, 0))
    return cur, prev, nxt, taps, vec


def _conf_fwd(pa, pg, w, wb, lng, lnb, seq, name, comm=None):
    n, d = pa.shape
    tm = _tile(seq, 256)
    tps = seq // tm
    hb = CONV_HALO
    cur, prev, _, taps, vec = _conf_specs(n, d, tm)

    def body(pap, pac, pgp, pgc, w_ref, wb_ref, lng_ref, lnb_ref, u_ref, s_ref, rot_ref):
        first = (pl.program_id(0) % tps) == 0
        a = jnp.concatenate([jnp.where(first, 0.0, pap[...]), pac[...]], axis=0)
        g = jnp.concatenate([jnp.where(first, 0.0, pgp[...]), pgc[...]], axis=0)
        z = a * _sig(g)
        for b in range(SUBLANES):
            rot_ref[b] = _roll(z, b)
        _conv_taps(rot_ref, w_ref, u_ref, tm, lambda sh: hb - (sh // SUBLANES) * SUBLANES)
        u = u_ref[...] + wb_ref[...]
        mu = jnp.mean(u, axis=-1, keepdims=True)
        uc = u - mu
        var = jnp.mean(uc * uc, axis=-1, keepdims=True)
        ul = uc * lax.rsqrt(var + EPS) * lng_ref[...] + lnb_ref[...]
        u_ref[...] = u
        s_ref[...] = (ul * _sig(ul)).astype(s_ref.dtype)

    return _hosted(
        body, comm, grid=(n // tm,), in_specs=[prev, cur, prev, cur, taps, vec, vec, vec],
        out_specs=[cur, cur],
        out_shape=[jax.ShapeDtypeStruct((n, d), F32), jax.ShapeDtypeStruct((n, d), BF16)],
        scratch_shapes=[pltpu.VMEM((SUBLANES, tm + hb, d), F32)], sem=("arbitrary",), name=name,
        ins=(pa, pa, pg, pg, w, wb, lng, lnb))


def _conf_bwd(u, ds, pa, pg, w, lng, lnb, seq, name, comm=None):
    n, d = u.shape
    tm = _tile(seq, 256)
    tps = seq // tm
    hb = CONV_HALO
    ext = tm + hb
    cur, prev, nxt, taps, vec = _conf_specs(n, d, tm)

    def body(uc_ref, un_ref, dsc_ref, dsn_ref, pap, pac, pgp, pgc, w_ref, lng_ref, lnb_ref,
             dpa_ref, dpg_ref, dw_ref, dwb_ref, dlng_ref, dlnb_ref, dba_ref, dbg_ref, rotz_ref, rotd_ref, dz_ref):
        i = pl.program_id(0)
        first = (i % tps) == 0
        last = (i % tps) == tps - 1

        @pl.when(i == 0)
        def _():
            for r in (dw_ref, dwb_ref, dlng_ref, dlnb_ref, dba_ref, dbg_ref):
                r[...] = jnp.zeros_like(r)

        ue = jnp.concatenate([uc_ref[...], un_ref[...]], axis=0)
        dse = jnp.concatenate([dsc_ref[...], jnp.where(last, 0.0, dsn_ref[...])], axis=0)
        mu = jnp.mean(ue, axis=-1, keepdims=True)
        cen = ue - mu
        r = lax.rsqrt(jnp.mean(cen * cen, axis=-1, keepdims=True) + EPS)
        xn = cen * r
        ul = xn * lng_ref[...] + lnb_ref[...]
        sg = _sig(ul)
        dul = dse * (sg * (1.0 + ul * (1.0 - sg)))
        dun = dul * lng_ref[...]
        du = r * (dun - jnp.mean(dun, axis=-1, keepdims=True) - xn * jnp.mean(dun * xn, axis=-1, keepdims=True))
        dlng_ref[...] += jnp.sum((dul * xn)[:tm], axis=0, keepdims=True)
        dlnb_ref[...] += jnp.sum(dul[:tm], axis=0, keepdims=True)
        dwb_ref[...] += jnp.sum(du[:tm], axis=0, keepdims=True)
        for b in range(SUBLANES):
            rotd_ref[b] = _roll(du, ext - b)
        _conv_taps(rotd_ref, w_ref, dz_ref, tm, lambda sh: (sh // SUBLANES) * SUBLANES)
        dz = dz_ref[...]

        a = jnp.concatenate([jnp.where(first, 0.0, pap[...]), pac[...]], axis=0)
        g = jnp.concatenate([jnp.where(first, 0.0, pgp[...]), pgc[...]], axis=0)
        sgg = _sig(g)
        z = a * sgg
        for b in range(SUBLANES):
            rotz_ref[b] = _roll(z, b)
        _tap_grads(rotz_ref, rotd_ref, dw_ref, tm, hb)

        a_c, sg_c = a[hb:], sgg[hb:]
        da = dz * sg_c
        dg = dz * a_c * sg_c * (1.0 - sg_c)
        dpa_ref[...] = da.astype(dpa_ref.dtype)
        dpg_ref[...] = dg.astype(dpg_ref.dtype)
        dba_ref[...] += jnp.sum(da, axis=0, keepdims=True)
        dbg_ref[...] += jnp.sum(dg, axis=0, keepdims=True)

    vshape = jax.ShapeDtypeStruct((1, d), F32)
    return _hosted(
        body, comm, grid=(n // tm,),
        in_specs=[cur, nxt, cur, nxt, prev, cur, prev, cur, taps, vec, vec],
        out_specs=[cur, cur, taps, vec, vec, vec, vec, vec],
        out_shape=[jax.ShapeDtypeStruct((n, d), BF16), jax.ShapeDtypeStruct((n, d), BF16),
                   jax.ShapeDtypeStruct((CONV_HALO, d), F32), vshape, vshape, vshape, vshape, vshape],
        scratch_shapes=[pltpu.VMEM((SUBLANES, ext, d), F32), pltpu.VMEM((SUBLANES, ext, d), F32),
                        pltpu.VMEM((tm, d), F32)],
        sem=("arbitrary",), name=name, ins=(u, u, ds, ds, pa, pa, pg, pg, w, lng, lnb))


def _pool_specs(n, d, tm, gd):
    hb = POOL_HALO
    cur = pl.BlockSpec((tm, d), lambda i: (i, 0))
    prev = pl.BlockSpec((hb, d), lambda i: (jnp.maximum(i * (tm // hb) - 1, 0), 0))
    nxt = pl.BlockSpec((hb, d), lambda i: (jnp.minimum((i + 1) * (tm // hb), n // hb - 1), 0))
    wsp = pl.BlockSpec((len(POOL_WINDOWS), gd, gd), lambda i: (0, 0, 0))
    vec = pl.BlockSpec((1, d), lambda i: (0, 0))
    return cur, prev, nxt, wsp, vec


def _pool_fwd(x, g, w, b, sc, seq, name):
    n, d = x.shape
    gd = d // len(POOL_WINDOWS)
    tm = _tile(seq, 256)
    tps = seq // tm
    hb = POOL_HALO
    cur, prev, _, wsp, vec = _pool_specs(n, d, tm, gd)

    def body(xp, xc, g_ref, w_ref, b_ref, sc_ref, x1_ref, p_ref):
        i = pl.program_id(0)
        first = (i % tps) == 0
        xe = jnp.concatenate([jnp.where(first, 0.0, xp[...]), xc[...]], axis=0)
        r = lax.rsqrt(jnp.mean(xe * xe, axis=-1, keepdims=True) + EPS)
        h = xe * r * g_ref[...]
        t = ((i % tps) * tm + lax.broadcasted_iota(jnp.int32, (tm, 1), 0) + 1).astype(F32)
        ys = []
        for gi, win in enumerate(POOL_WINDOWS):
            hg = h[:, gi * gd:(gi + 1) * gd]
            s, sh = hg, 1
            while sh < win:
                s = s + _roll(s, sh)
                sh *= 2
            p = (s[hb:] / jnp.minimum(t, float(win)) - hg[hb:]).astype(BF16)
            p_ref[:, gi * gd:(gi + 1) * gd] = p
            ys.append(jnp.dot(p, w_ref[gi], preferred_element_type=F32))
        y = jnp.concatenate(ys, axis=1) + b_ref[...]
        x1_ref[...] = xc[...] + y * sc_ref[...]

    return pl.pallas_call(
        body, grid=(n // tm,), in_specs=[prev, cur, vec, wsp, vec, vec], out_specs=[cur, cur],
        out_shape=[jax.ShapeDtypeStruct((n, d), F32), jax.ShapeDtypeStruct((n, d), BF16)],
        name=name, compiler_params=_cp("arbitrary"),
    )(x, x, g, w, b, sc)


def _pool_bwd(dx1, x, p, g, w, b, sc, seq, name):
    n, d = x.shape
    ng = len(POOL_WINDOWS)
    gd = d // ng
    tm = _tile(seq, 256)
    tps = seq // tm
    hb = POOL_HALO
    ext = tm + hb
    cur, _, nxt, wsp, vec = _pool_specs(n, d, tm, gd)

    def body(dc_ref, dn_ref, x_ref, p_ref, g_ref, w_ref, b_ref, sc_ref, dx_ref, dg_ref, dw_ref, db_ref, dsc_ref):
        i = pl.program_id(0)
        last = (i % tps) == tps - 1

        @pl.when(i == 0)
        def _():
            for r_ in (dg_ref, dw_ref, db_ref, dsc_ref):
                r_[...] = jnp.zeros_like(r_)

        dxc = dc_ref[...]
        dxe = jnp.concatenate([dxc, jnp.where(last, 0.0, dn_ref[...])], axis=0)
        dyg = dxe * sc_ref[...]
        t = ((i % tps) * tm + lax.broadcasted_iota(jnp.int32, (ext, 1), 0) + 1).astype(F32)
        dhs = []
        for gi, win in enumerate(POOL_WINDOWS):
            sl = slice(gi * gd, (gi + 1) * gd)
            dyb = dyg[:, sl].astype(BF16)
            wg = w_ref[gi]
            dp = lax.dot_general(dyb, wg, (((1,), (1,)), ((), ())), preferred_element_type=F32)
            s, sh = dp / jnp.minimum(t, float(win)), 1
            while sh < win:
                s = s + _roll(s, ext - sh)
                sh *= 2
            dhs.append((s - dp)[:tm])
            pg = p_ref[:, sl]
            dw_ref[gi] += lax.dot_general(pg, dyb[:tm], (((0,), (0,)), ((), ())), preferred_element_type=F32)
            ypre = jnp.dot(pg, wg, preferred_element_type=F32) + b_ref[:, sl]
            dsc_ref[:, sl] += jnp.sum(dxc[:, sl] * ypre, axis=0, keepdims=True)
            db_ref[:, sl] += jnp.sum(dyg[:tm, sl], axis=0, keepdims=True)
        dh = jnp.concatenate(dhs, axis=1)
        xv = x_ref[...]
        r = lax.rsqrt(jnp.mean(xv * xv, axis=-1, keepdims=True) + EPS)
        xn = xv * r
        dxn = dh * g_ref[...]
        dx_ref[...] = dxc + r * (dxn - xn * jnp.mean(dxn * xn, axis=-1, keepdims=True))
        dg_ref[...] += jnp.sum(dh * xn, axis=0, keepdims=True)

    vshape = jax.ShapeDtypeStruct((1, d), F32)
    return pl.pallas_call(
        body, grid=(n // tm,), in_specs=[cur, nxt, cur, cur, vec, wsp, vec, vec],
        out_specs=[cur, vec, wsp, vec, vec],
        out_shape=[jax.ShapeDtypeStruct((n, d), F32), vshape, jax.ShapeDtypeStruct((ng, gd, gd), F32), vshape, vshape],
        name=name, compiler_params=_cp("arbitrary"),
    )(dx1, dx1, x, p, g, w, b, sc)


def _head_maps(d):
    hd = lax.broadcasted_iota(jnp.int32, (d, LANES), 0) // HEAD_DIM
    col = lax.broadcasted_iota(jnp.int32, (d, LANES), 1)
    gm = (hd == col).astype(BF16)
    hd_t = lax.broadcasted_iota(jnp.int32, (LANES, d), 1) // HEAD_DIM
    row = lax.broadcasted_iota(jnp.int32, (LANES, d), 0)
    gt = (hd_t == row).astype(BF16)
    return gm, gt


def _dot_split(v, onehot):
    hi = v.astype(BF16)
    lo = (v - hi.astype(F32)).astype(BF16)
    return jnp.dot(hi, onehot, preferred_element_type=F32) + jnp.dot(lo, onehot, preferred_element_type=F32)


def _bias_placement(nh):
    pq = np.zeros((3 * LANES, nh * HEAD_DIM), np.float32)
    pk = np.zeros((3 * LANES, nh * HEAD_DIM), np.float32)
    oq = np.zeros((1, nh * HEAD_DIM), np.float32)
    ok = np.zeros((1, nh * HEAD_DIM), np.float32)
    for h in range(nh):
        for piece in range(3):
            pq[piece * LANES + h, h * HEAD_DIM + piece] = 1.0
            pk[piece * LANES + h, h * HEAD_DIM + 3 + piece] = -1.0
            oq[0, h * HEAD_DIM + 3 + piece] = 1.0
            ok[0, h * HEAD_DIM + piece] = 1.0
    return jnp.asarray(pq, BF16), jnp.asarray(pk, BF16), jnp.asarray(oq), jnp.asarray(ok)


def _fox_prep_fwd(qkv, fl, bf, qg, kg, seq, name):
    n, d3 = qkv.shape
    d = d3 // 3
    nh = d // HEAD_DIM
    tm = _tile(seq, 256)
    tps = seq // tm
    scale = 1.0 / math.sqrt(HEAD_DIM)
    pq, pk, oq, ok = _bias_placement(nh)

    def body(qkv_ref, fl_ref, bf_ref, qg_ref, kg_ref, pq_ref, pk_ref, oq_ref, ok_ref, q_ref, k_ref, v_ref, carry):
        first = (pl.program_id(0) % tps) == 0
        gm, gt = _head_maps(d)

        def head_norm(xr, gain):
            r = lax.rsqrt(_dot_split(xr * xr, gm) / HEAD_DIM + EPS)
            return xr * _dot_split(r, gt) * gain

        qs = (head_norm(qkv_ref[:, :d], qg_ref[...]).astype(BF16).astype(F32) * scale).astype(BF16)
        kn = head_norm(qkv_ref[:, d:2 * d], kg_ref[...]).astype(BF16)
        v_ref[...] = qkv_ref[:, 2 * d:].astype(BF16)
        z = fl_ref[...] + bf_ref[...]
        logf = jnp.minimum(z, 0.0) - jnp.log1p(jnp.exp(-jnp.abs(z)))
        tri = (lax.broadcasted_iota(jnp.int32, (tm, tm), 0) >= lax.broadcasted_iota(jnp.int32, (tm, tm), 1)).astype(F32)

        @pl.when(first)
        def _():
            carry[...] = jnp.zeros_like(carry)

        c = jnp.dot(tri, logf, precision=HI, preferred_element_type=F32) + carry[...]
        carry[...] = c[tm - 1:tm, :]
        c1 = c.astype(BF16)
        r1 = c - c1.astype(F32)
        c2 = r1.astype(BF16)
        c3 = (r1 - c2.astype(F32)).astype(BF16)
        pieces = jnp.concatenate([c1, c2, c3], axis=1)
        eq = (jnp.dot(pieces, pq_ref[...], preferred_element_type=F32) + oq_ref[...]).astype(BF16)
        ek = (jnp.dot(pieces, pk_ref[...], preferred_element_type=F32) + ok_ref[...]).astype(BF16)
        for h in range(nh):
            lo, hi = h * HEAD_DIM, (h + 1) * HEAD_DIM
            q_ref[:, 2 * lo:2 * lo + HEAD_DIM] = qs[:, lo:hi]
            q_ref[:, 2 * lo + HEAD_DIM:2 * hi] = eq[:, lo:hi]
            k_ref[:, 2 * lo:2 * lo + HEAD_DIM] = kn[:, lo:hi]
            k_ref[:, 2 * lo + HEAD_DIM:2 * hi] = ek[:, lo:hi]

    row = lambda w: pl.BlockSpec((tm, w), lambda i: (i, 0))
    vec = lambda w: pl.BlockSpec((1, w), lambda i: (0, 0))
    full = lambda a: pl.BlockSpec(a.shape, lambda i: (0, 0))
    return pl.pallas_call(
        body, grid=(n // tm,),
        in_specs=[row(d3), row(LANES), vec(LANES), vec(d), vec(d), full(pq), full(pk), full(oq), full(ok)],
        out_specs=[row(2 * d), row(2 * d), row(d)],
        out_shape=[jax.ShapeDtypeStruct((n, 2 * d), BF16)] * 2 + [jax.ShapeDtypeStruct((n, d), BF16)],
        scratch_shapes=[pltpu.VMEM((1, LANES), F32)], name=name, compiler_params=_cp("arbitrary"),
    )(qkv, fl, bf, qg, kg, pq, pk, oq, ok)


def _fox_prep_bwd(qkv, dq, dk, dv, dc1, dc2, fl, bf, qg, kg, seq, name, comm=None):
    n, d3 = qkv.shape
    d = d3 // 3
    tm = _tile(seq, 256)
    tps = seq // tm
    nt = n // tm

    def body(qkv_ref, dq_ref, dk_ref, dv_ref, dc1_ref, dc2_ref, fl_ref, bf_ref, qg_ref, kg_ref,
             dqkv_ref, dfl_ref, dqg_ref, dkg_ref, dbf_ref, carry):
        i = pl.program_id(0)
        tile = nt - 1 - i
        last = (tile % tps) == tps - 1
        gm, gt = _head_maps(d)

        @pl.when(i == 0)
        def _():
            for r_ in (dqg_ref, dkg_ref, dbf_ref):
                r_[...] = jnp.zeros_like(r_)

        @pl.when(last)
        def _():
            carry[...] = jnp.zeros_like(carry)

        def head_norm_bwd(xr, dy, gain, dgain_ref):
            rf = _dot_split(lax.rsqrt(_dot_split(xr * xr, gm) / HEAD_DIM + EPS), gt)
            xn = xr * rf
            dgain_ref[...] += jnp.sum(dy * xn, axis=0, keepdims=True)
            dyg = dy * gain
            mean = _dot_split(dyg * xn, gm) / HEAD_DIM
            return rf * (dyg - xn * _dot_split(mean, gt))

        dqkv_ref[:, :d] = head_norm_bwd(qkv_ref[:, :d], dq_ref[...], qg_ref[...], dqg_ref).astype(BF16)
        dqkv_ref[:, d:2 * d] = head_norm_bwd(qkv_ref[:, d:2 * d], dk_ref[...], kg_ref[...], dkg_ref).astype(BF16)
        dqkv_ref[:, 2 * d:] = dv_ref[...].astype(BF16)

        dc = dc1_ref[...] + dc2_ref[...]
        tri = (lax.broadcasted_iota(jnp.int32, (tm, tm), 0) <= lax.broadcasted_iota(jnp.int32, (tm, tm), 1)).astype(F32)
        dlog = jnp.dot(tri, dc, precision=HI, preferred_element_type=F32) + carry[...]
        carry[...] = dlog[0:1, :]
        dfl = dlog * (1.0 - _sig(fl_ref[...] + bf_ref[...]))
        dfl_ref[...] = dfl.astype(BF16)
        dbf_ref[...] += jnp.sum(dfl, axis=0, keepdims=True)

    row = lambda w: pl.BlockSpec((tm, w), lambda i: (nt - 1 - i, 0))
    vec = lambda w: pl.BlockSpec((1, w), lambda i: (0, 0))
    return _hosted(
        body, comm, grid=(nt,),
        in_specs=[row(d3), row(d), row(d), row(d), row(LANES), row(LANES), row(LANES), vec(LANES), vec(d), vec(d)],
        out_specs=[row(d3), row(LANES), vec(d), vec(d), vec(LANES)],
        out_shape=[jax.ShapeDtypeStruct((n, d3), BF16), jax.ShapeDtypeStruct((n, LANES), BF16),
                   jax.ShapeDtypeStruct((1, d), F32), jax.ShapeDtypeStruct((1, d), F32),
                   jax.ShapeDtypeStruct((1, LANES), F32)],
        scratch_shapes=[pltpu.VMEM((1, LANES), F32)], sem=("arbitrary",), name=name,
        ins=(qkv, dq, dk, dv, dc1, dc2, fl, bf, qg, kg))


def _attn_specs(bsz, seq, t):
    nb = seq // t
    blk = lambda w: pl.BlockSpec((t, w), lambda b, h, i: (b * nb + i, h))
    full = lambda w: pl.BlockSpec((seq, w), lambda b, h, i: (b, h))
    col = pl.BlockSpec((None, None, t, 2), lambda b, h, i: (b, h, i, 0))
    rows = pl.BlockSpec((None, None, nb, 2, t), lambda b, h, i: (b, h, 0, 0, 0))
    return nb, blk, full, col, rows


_NT = (((1,), (1,)), ((), ()))
ATTN_TILE = 512


def _head_lanes(t, hh):
    lane = lax.broadcasted_iota(jnp.int32, (t, LANES), 1)
    return (lane < HEAD_DIM) if hh == 0 else (lane >= HEAD_DIM)


def _flash_fwd(qa, ka, v, bsz, seq, name):
    n, d = v.shape
    hp = d // LANES
    t = _tile(seq, ATTN_TILE)
    nb, blk, full, col, _ = _attn_specs(bsz, seq, t)

    def body(q_ref, k_ref, v_ref, o_ref, lse_ref):
        i = pl.program_id(2)
        causal = lax.broadcasted_iota(jnp.int32, (t, t), 0) >= lax.broadcasted_iota(jnp.int32, (t, t), 1)

        def block(j, carry, masked):
            rs = pl.ds(pl.multiple_of(j * t, t), t)
            vj = v_ref[rs, :]
            out = []
            for hh in range(2):
                m, l, acc = carry[hh]
                hs = slice(hh * LANES, (hh + 1) * LANES)
                sc = lax.dot_general(q_ref[:, hs], k_ref[rs, hs], _NT, preferred_element_type=F32)
                if masked:
                    sc = jnp.where(causal, sc, NEG)
                mn = jnp.maximum(m, jnp.max(sc, axis=-1, keepdims=True))
                p = jnp.exp(sc - mn)
                al = jnp.exp(m - mn)
                l = al * l + jnp.sum(p, axis=-1, keepdims=True)
                acc = al * acc + jnp.dot(p.astype(BF16), vj, preferred_element_type=F32)
                out.append((mn, l, acc))
            return tuple(out)

        init = tuple((jnp.full((t, 1), NEG, F32), jnp.zeros((t, 1), F32), jnp.zeros((t, LANES), F32))
                     for _ in range(2))
        carry = lax.fori_loop(0, i, lambda j, c: block(j, c, False), init)
        (m0, l0, a0), (m1, l1, a1) = block(i, carry, True)
        o_ref[...] = jnp.where(_head_lanes(t, 0), a0 / l0, a1 / l1)
        lse_ref[:, 0:1] = m0 + jnp.log(l0)
        lse_ref[:, 1:2] = m1 + jnp.log(l1)

    return pl.pallas_call(
        body, grid=(bsz, hp, nb), in_specs=[blk(2 * LANES), full(2 * LANES), full(LANES)],
        out_specs=[blk(LANES), col],
        out_shape=[jax.ShapeDtypeStruct((n, d), F32), jax.ShapeDtypeStruct((bsz, hp, seq, 2), F32)],
        name=name, compiler_params=_cp("parallel", "parallel", "arbitrary"),
    )(qa, ka, v)


def _flash_bwd(qa, ka, v, do, o, lse_row, bsz, seq, name):
    n, d = v.shape
    hp = d // LANES
    t = _tile(seq, ATTN_TILE)
    nb, blk, full, col, rows = _attn_specs(bsz, seq, t)
    scale = 1.0 / math.sqrt(HEAD_DIM)
    tn_ = (((0,), (0,)), ((), ()))

    def body(k_ref, v_ref, q_ref, do_ref, o_ref, lse_ref, dq_ref, dcc_ref, dk_ref, dv_ref, dck_ref, dqa, dl):
        j = pl.program_id(2)
        causal = lax.broadcasted_iota(jnp.int32, (t, t), 1) >= lax.broadcasted_iota(jnp.int32, (t, t), 0)
        heads = [_head_lanes(t, 0), _head_lanes(t, 1)]

        @pl.when(j == 0)
        def _():
            dqa[...] = jnp.zeros_like(dqa)
            ones = jnp.ones((8, LANES), F32)
            for ib in range(nb):
                rs = slice(ib * t, (ib + 1) * t)
                prod = do_ref[rs, :] * o_ref[rs, :]
                for hh in range(2):
                    dl[ib, hh:hh + 1, :] = lax.dot_general(ones, jnp.where(heads[hh], prod, 0.0), _NT, precision=HI,
                                                           preferred_element_type=F32)[0:1]

        vj = v_ref[...]

        def block(i, carry, masked):
            rs = pl.ds(pl.multiple_of(i * t, t), t)
            doi = do_ref[rs, :]
            dks, dvp = list(carry[:2]), carry[2]
            for hh in range(2):
                hs = slice(hh * LANES, (hh + 1) * LANES)
                kh, qi = k_ref[:, hs], q_ref[rs, hs]
                dom = jnp.where(heads[hh], doi, 0.0).astype(BF16)
                st = lax.dot_general(kh, qi, _NT, preferred_element_type=F32)
                if masked:
                    st = jnp.where(causal, st, NEG)
                pt = jnp.exp(st - lse_ref[i, hh:hh + 1, :])
                dvp = dvp + jnp.dot(pt.astype(BF16), dom, preferred_element_type=F32)
                dpt = lax.dot_general(vj, dom, _NT, preferred_element_type=F32)
                dsb = (pt * (dpt - dl[i, hh:hh + 1, :])).astype(BF16)
                dks[hh] = dks[hh] + jnp.dot(dsb, qi, preferred_element_type=F32)
                dqa[rs, hs] += lax.dot_general(dsb, kh, tn_, preferred_element_type=F32)
            return dks[0], dks[1], dvp

        zero = jnp.zeros((t, LANES), F32)
        carry = block(j, (zero, zero, zero), True)
        dk0, dk1, dvp = lax.fori_loop(j + 1, nb, lambda i, c: block(i, c, False), carry)
        dk_ref[...] = jnp.where(heads[0], dk0, pltpu.roll(dk1, HEAD_DIM, 1))
        dv_ref[...] = dvp
        dck_ref[:, 0:1] = -dk0[:, HEAD_DIM + 3:HEAD_DIM + 4]
        dck_ref[:, 1:2] = -dk1[:, HEAD_DIM + 3:HEAD_DIM + 4]

        @pl.when(j == nb - 1)
        def _():
            first = lax.broadcasted_iota(jnp.int32, (seq, LANES), 1) < HEAD_DIM
            dq_ref[...] = jnp.where(first, dqa[:, :LANES], pltpu.roll(dqa[:, LANES:], HEAD_DIM, 1)) * scale
            for hh in range(2):
                lo = hh * LANES + HEAD_DIM
                dcc_ref[:, hh:hh + 1] = dqa[:, lo:lo + 1]

    whole_col = pl.BlockSpec((None, None, seq, 2), lambda b, h, i: (b, h, 0, 0))
    cshape = jax.ShapeDtypeStruct((bsz, hp, seq, 2), F32)
    nd = jax.ShapeDtypeStruct((n, d), F32)
    return pl.pallas_call(
        body, grid=(bsz, hp, nb),
        in_specs=[blk(2 * LANES), blk(LANES), full(2 * LANES), full(LANES), full(LANES), rows],
        out_specs=[full(LANES), whole_col, blk(LANES), blk(LANES), col],
        out_shape=[nd, cshape, nd, nd, cshape],
        scratch_shapes=[pltpu.VMEM((seq, 2 * LANES), F32), pltpu.VMEM((nb, 2, t), F32)],
        name=name, compiler_params=_cp("parallel", "parallel", "arbitrary"),
    )(ka, v, qa, do, o, lse_row)


def _adamw(w, g, m, v, name):
    r, c = w.shape
    tr = r
    for cand in (512, 256, 128, 64, 32, 16, 8):
        if r % cand == 0 and r > cand and cand * c * 4 <= 4 * 1024 * 1024:
            tr = cand
            break

    def body(w_ref, g_ref, m_ref, v_ref, d_ref, m2_ref, v2_ref):
        gv = g_ref[...]
        m2 = ADAM_B1 * m_ref[...] + (1.0 - ADAM_B1) * gv
        v2 = ADAM_B2 * v_ref[...] + (1.0 - ADAM_B2) * jnp.square(gv)
        m_hat = m2 / (1.0 - ADAM_B1 ** ADAM_STEP)
        v_hat = v2 / (1.0 - ADAM_B2 ** ADAM_STEP)
        d_ref[...] = -ADAM_LR * (m_hat / (jnp.sqrt(v_hat) + ADAM_EPS) + ADAM_WD * w_ref[...])
        m2_ref[...] = m2
        v2_ref[...] = v2

    blk = pl.BlockSpec((tr, c), lambda i: (i, 0))
    shp = jax.ShapeDtypeStruct((r, c), F32)
    return pl.pallas_call(
        body, grid=(r // tr,), in_specs=[blk] * 4, out_specs=[blk] * 3, out_shape=[shp] * 3,
        name=name, compiler_params=_cp("parallel"),
    )(w, g, m, v)


def _adamw_nd(w, g, m, v, name):
    shape = w.shape
    two = (math.prod(shape[:-1]), shape[-1])
    d_, m_, v_ = _adamw(w.reshape(two), g.reshape(two), m.reshape(two), v.reshape(two), name)
    return g.reshape(shape), d_.reshape(shape), m_.reshape(shape), v_.reshape(shape)


def _place():
    x, y, c = lax.axis_index("x"), lax.axis_index("y"), lax.axis_index("c")
    chips = [(1 - x, y), (x, 1 - y), (1 - x, 1 - y)]
    return x, y, c, chips


def _gather_chips(shards):
    nt = len(shards)
    halves = [s.shape[0] // 2 for s in shards]

    def copies(ins, outs, send_sems, recv_sems):
        x, y, c, chips = _place()
        cps = []
        for t in range(nt):
            rows = pl.ds(c * halves[t], halves[t])
            for jj, (cx, cy) in enumerate(chips):
                cps.append(pltpu.make_async_remote_copy(
                    src_ref=ins[t].at[rows, :], dst_ref=outs[t].at[2 * x + y, rows, :], send_sem=send_sems.at[3 * t + jj],
                    recv_sem=recv_sems.at[3 * t + jj], device_id=(cx, cy, c), device_id_type=MESH))
        return cps

    def start(*refs):
        for cp in copies(*refs):
            cp.start()

    def wait(*refs):
        for cp in copies(*refs):
            cp.wait()

    return _Comm(shards, [jax.ShapeDtypeStruct((N_CHIPS,) + s.shape, s.dtype) for s in shards], 3 * nt, start, wait)


def _gather_sibling(shards, bufs, name):
    nt = len(shards)
    halves = [s.shape[0] // 2 for s in shards]

    def body(*refs):
        ins, outs = refs[:nt], refs[2 * nt:3 * nt]
        send_sems, recv_sems = refs[3 * nt:]
        x, y, c, chips = _place()
        sibling = (x, y, 1 - c)

        def copy(t, jj, hf):
            cx, cy = chips[jj]
            region = outs[t].at[2 * cx + cy, pl.ds(hf * halves[t], halves[t]), :]
            return pltpu.make_async_remote_copy(src_ref=region, dst_ref=region, send_sem=send_sems.at[4 * t + jj],
                                                recv_sem=recv_sems.at[4 * t + jj], device_id=sibling,
                                                device_id_type=MESH)

        def own(t):
            return pltpu.make_async_remote_copy(src_ref=ins[t], dst_ref=outs[t].at[2 * x + y],
                                                send_sem=send_sems.at[4 * t + 3], recv_sem=recv_sems.at[4 * t + 3],
                                                device_id=sibling, device_id_type=MESH)

        sends = [copy(t, jj, c) for t in range(nt) for jj in range(3)] + [own(t) for t in range(nt)]
        for cp in sends:
            cp.start()
        for t in range(nt):
            for jj in range(3):
                copy(t, jj, 1 - c).wait_recv()
            own(t).wait_recv()
        for cp in sends:
            cp.wait_send()

    return pl.pallas_call(
        body, in_specs=[ANY] * (2 * nt), out_specs=[ANY] * nt,
        out_shape=[jax.ShapeDtypeStruct(b.shape, b.dtype) for b in bufs],
        scratch_shapes=[pltpu.SemaphoreType.DMA((4 * nt,)), pltpu.SemaphoreType.DMA((4 * nt,))],
        input_output_aliases={nt + t: t for t in range(nt)}, name=name,
        compiler_params=pltpu.CompilerParams(has_side_effects=True),
    )(*shards, *bufs)


N_PARTIALS = 7


def _scatter_partials(grads):
    nt = len(grads)
    halves = [g.shape[1] // 2 for g in grads]

    def copies(ins, outs, send_sems, recv_sems):
        x, y, c, chips = _place()
        cps = []

        def copy(t, kk, slot, core, to):
            rows = pl.ds(core * halves[t], halves[t])
            return pltpu.make_async_remote_copy(
                src_ref=ins[t].at[slot, rows, :], dst_ref=outs[t].at[kk], send_sem=send_sems.at[N_PARTIALS * t + kk],
                recv_sem=recv_sems.at[N_PARTIALS * t + kk], device_id=to, device_id_type=MESH)

        for t in range(nt):
            for jj, (cx, cy) in enumerate(chips):
                cps.append(copy(t, 2 * jj, 2 * cx + cy, c, (cx, cy, c)))
                cps.append(copy(t, 2 * jj + 1, 2 * cx + cy, 1 - c, (cx, cy, 1 - c)))
            cps.append(copy(t, N_PARTIALS - 1, 2 * x + y, 1 - c, (x, y, 1 - c)))
        return cps

    def start(*refs):
        for cp in copies(*refs):
            cp.start()

    def wait(*refs):
        for cp in copies(*refs):
            cp.wait()

    return _Comm(grads, [jax.ShapeDtypeStruct((N_PARTIALS, h, g.shape[2]), g.dtype) for g, h in zip(grads, halves)],
                 N_PARTIALS * nt, start, wait)


def _add_partials(g, got, place_idx, name):
    s, r, w = g.shape
    rh = r // 2
    tr = rh
    for cand in (256, 128, 64, 32, 16):
        if rh % cand == 0 and cand * w * 4 <= 2 * 1024 * 1024:
            tr = cand
            break
    per = rh // tr

    def body(b_ref, g_ref, *rest):
        o_ref = rest[-1]
        acc = g_ref[...].astype(F32)
        for r_ref in rest[:-1]:
            acc = acc + r_ref[...].astype(F32)
        o_ref[...] = acc

    return pl.pallas_call(
        body,
        grid_spec=pltpu.PrefetchScalarGridSpec(
            num_scalar_prefetch=1, grid=(per,),
            in_specs=[pl.BlockSpec((None, tr, w), lambda a, b_ref: (b_ref[0], b_ref[1] * per + a, 0))]
            + [pl.BlockSpec((None, tr, w), lambda a, b_ref, kk=kk: (kk, a, 0)) for kk in range(N_PARTIALS)],
            out_specs=pl.BlockSpec((tr, w), lambda a, b_ref: (b_ref[1] * per + a, 0))),
        out_shape=jax.ShapeDtypeStruct((r, w), F32), name=name, compiler_params=_cp("parallel"),
    )(place_idx, g, *([got] * N_PARTIALS))


def _swap_halves(bufs, name):
    nt = len(bufs)

    def body(*refs):
        outs = refs[nt:2 * nt]
        send_sems, recv_sems = refs[2 * nt:]
        x, y, c, _ = _place()

        def copy(t, hf):
            rh = outs[t].shape[0] // 2
            region = outs[t].at[pl.ds(hf * rh, rh), :]
            return pltpu.make_async_remote_copy(src_ref=region, dst_ref=region, send_sem=send_sems.at[t],
                                                recv_sem=recv_sems.at[t], device_id=(x, y, 1 - c),
                                                device_id_type=MESH)

        sends = [copy(t, c) for t in range(nt)]
        for cp in sends:
            cp.start()
        for t in range(nt):
            copy(t, 1 - c).wait_recv()
        for cp in sends:
            cp.wait_send()

    return pl.pallas_call(
        body, in_specs=[ANY] * nt, out_specs=[ANY] * nt,
        out_shape=[jax.ShapeDtypeStruct(b.shape, b.dtype) for b in bufs],
        scratch_shapes=[pltpu.SemaphoreType.DMA((nt,)), pltpu.SemaphoreType.DMA((nt,))],
        input_output_aliases={t: t for t in range(nt)}, name=name,
        compiler_params=pltpu.CompilerParams(has_side_effects=True),
    )(*bufs)


def _rs_finish(grads, got, place_idx, tag):
    fin = [_add_partials(g, r, place_idx, name=f"rs_add_{tag}_{t}") for t, (g, r) in enumerate(zip(grads, got))]
    return _swap_halves(fin, name=f"rs_swap_{tag}")


def _all_reduce_small(v, name):
    r, w = v.shape

    def body(v_ref, o_ref, buf, send_sems, recv_sems):
        x, y, c, _ = _place()
        me = 4 * x + 2 * y + c
        buf[me] = v_ref[...]
        cps = []
        for kk in range(1, 8):
            peer = (x ^ ((kk >> 2) & 1), y ^ ((kk >> 1) & 1), c ^ (kk & 1))
            cps.append(pltpu.make_async_remote_copy(src_ref=v_ref, dst_ref=buf.at[me], send_sem=send_sems.at[kk - 1],
                                                    recv_sem=recv_sems.at[kk - 1], device_id=peer, device_id_type=MESH))
        for cp in cps:
            cp.start()
        for kk in range(1, 8):
            pltpu.make_async_remote_copy(src_ref=v_ref, dst_ref=buf.at[me ^ kk], send_sem=send_sems.at[kk - 1],
                                         recv_sem=recv_sems.at[kk - 1], device_id=(x, y, c),
                                         device_id_type=MESH).wait_recv()
        for cp in cps:
            cp.wait_send()
        acc = buf[0]
        for dev in range(1, 8):
            acc = acc + buf[dev]
        o_ref[...] = acc

    vm = pl.BlockSpec(memory_space=pltpu.VMEM)
    return pl.pallas_call(
        body, in_specs=[vm], out_specs=vm, out_shape=jax.ShapeDtypeStruct((r, w), F32),
        scratch_shapes=[pltpu.VMEM((8, r, w), F32), pltpu.SemaphoreType.DMA((7,)), pltpu.SemaphoreType.DMA((7,))],
        name=name, compiler_params=pltpu.CompilerParams(has_side_effects=True),
    )(v)


def _to_shards(a, axis=-1):
    axis = axis % a.ndim
    shp = a.shape
    a = a.reshape(shp[:axis] + (N_CHIPS, shp[axis] // N_CHIPS) + shp[axis + 1:])
    return jnp.moveaxis(a, axis, 0).reshape(N_CHIPS, -1)


def _from_shards(s, shard_shape, axis=-1):
    axis = axis % len(shard_shape)
    a = jnp.moveaxis(s.reshape((N_CHIPS,) + tuple(shard_shape)), 0, axis)
    return a.reshape(tuple(shard_shape[:axis]) + (N_CHIPS * shard_shape[axis],) + tuple(shard_shape[axis + 1:]))


def _pack(vecs, rows):
    flat = jnp.concatenate([v.reshape(v.shape[0], -1) if v.ndim > 1 else v.reshape(1, -1) for v in vecs], axis=1)
    lead = flat.shape[0]
    flat = jnp.pad(flat, ((0, 0), (0, rows * LANES - flat.shape[1])))
    return flat.reshape(lead, rows, LANES)


def _pack_rows(sizes, mult):
    total = sum(sizes)
    rows = -(-total // LANES)
    return -(-rows // mult) * mult


def _unpack(flat, shapes):
    out, pos = [], 0
    for shp in shapes:
        sz = math.prod(shp)
        out.append(flat[..., pos:pos + sz].reshape(flat.shape[:-1] + tuple(shp)))
        pos += sz
    return out


def _row_layout(col, t):
    bsz, hp, seq, _ = col.shape
    return col.reshape(bsz, hp, seq // t, t, 2).transpose(0, 1, 2, 4, 3)


def _from_col_layout(col):
    bsz, hp, seq, _ = col.shape
    a = col.transpose(0, 2, 1, 3).reshape(bsz * seq, 2 * hp)
    return jnp.pad(a, ((0, 0), (0, LANES - 2 * hp)))


def kernel(x, norm_mix, norm_ffn, conv_w_in, conv_b_in, conv_dw, conv_dw_b, conv_ln_g, conv_ln_b, conv_w_out, conv_b_out, pool_w, pool_b, pool_scale, fox_w_in, fox_b_f, fox_q_gain, fox_k_gain, fox_w_o, ffn_w_up, ffn_dw, ffn_dw_b, ffn_w_down, loss_target, m_norm_mix, m_norm_ffn, m_conv_w_in, m_conv_b_in, m_conv_dw, m_conv_dw_b, m_conv_ln_g, m_conv_ln_b, m_conv_w_out, m_conv_b_out, m_pool_w, m_pool_b, m_pool_scale, m_fox_w_in, m_fox_b_f, m_fox_q_gain, m_fox_k_gain, m_fox_w_o, m_ffn_w_up, m_ffn_dw, m_ffn_dw_b, m_ffn_w_down, v_norm_mix, v_norm_ffn, v_conv_w_in, v_conv_b_in, v_conv_dw, v_conv_dw_b, v_conv_ln_g, v_conv_ln_b, v_conv_w_out, v_conv_b_out, v_pool_w, v_pool_b, v_pool_scale, v_fox_w_in, v_fox_b_f, v_fox_q_gain, v_fox_k_gain, v_fox_w_o, v_ffn_w_up, v_ffn_dw, v_ffn_dw_b, v_ffn_w_down):
    bsz, seq, d = x.shape
    n = bsz * seq
    depth = norm_mix.shape[0]
    n_conv, n_pool, n_fox = conv_w_in.shape[0], pool_w.shape[0], fox_w_in.shape[0]
    f2 = ffn_dw_b.shape[1]
    f = f2 // 2
    nh = d // HEAD_DIM
    hp = d // LANES
    ng = len(POOL_WINDOWS)
    gd = d // ng
    chip_idx = jnp.stack([2 * lax.axis_index("x") + lax.axis_index("y"), lax.axis_index("c")]).astype(jnp.int32)

    small_shapes = [conv_b_in.shape, conv_dw.shape, conv_dw_b.shape, conv_ln_g.shape, conv_ln_b.shape,
                    conv_b_out.shape, pool_b.shape, ffn_dw.shape]
    small_rows = _pack_rows([math.prod(s) for s in small_shapes], 16)
    small = _pack([v.reshape(1, -1) for v in (conv_b_in, conv_dw, conv_dw_b, conv_ln_g, conv_ln_b, conv_b_out,
                                                pool_b, ffn_dw)], small_rows)[0]
    def layer_shards(i):
        kind, j = i % 3, i // 3
        shards = [ffn_w_up[i].astype(BF16), ffn_w_down[i].astype(BF16)]
        if kind == 0:
            shards += [conv_w_in[j].astype(BF16), conv_w_out[j].astype(BF16)]
        elif kind == 1:
            shards += [pool_w[j].reshape(ng * (gd // N_CHIPS), gd).astype(BF16)]
        else:
            shards += [fox_w_in[j].astype(BF16), fox_w_o[j].astype(BF16)]
        if i == 0:
            shards.append(small)
        return shards

    gathered = [None] * depth
    first_now = layer_shards(0)[2:]
    gathered[0] = [None, None] + list(_gather_sibling(
        first_now, _run_comm(_gather_chips(first_now), name="gather_chips_l0"), name="gather_sibling_l0"))
    small_all = gathered[0][-1].reshape(N_CHIPS, -1)
    sm = _unpack(small_all, small_shapes)
    axes = [-1] * 8
    b_in_f, dw_f, dw_b_f, ln_g_f, ln_b_f, b_out_f, pool_b_f, ffn_dw_f = [
        _from_shards(s_.reshape(N_CHIPS, -1), shp, ax) for s_, shp, ax in zip(sm, small_shapes, axes)]

    xs = x.reshape(n, d)
    tgt = loss_target.reshape(n, d)
    vec = lambda a: a.reshape(1, -1)

    saved = []
    cur = xs
    for i in range(depth):
        kind, j = i % 3, i // 3
        wts = gathered[i]
        sv = {"x_in": cur}
        if kind == 0:
            w_in, w_out = wts[2], wts[3].reshape(d, d)
            wcol = w_in.shape[2]
            h = _rms_fwd(cur, vec(norm_mix[i]), name=f"rms_mix_l{i}")
            pa = _mm(h, w_in, m=n, n=d, k=d, tk=d, tn=wcol, b_stk=wcol, b_s0=0, bias=vec(b_in_f[j, :d]),
                     name=f"conv_in_a_l{i}")
            pg = _mm(h, w_in, m=n, n=d, k=d, tk=d, tn=wcol, b_stk=wcol, b_s0=2, bias=vec(b_in_f[j, d:]),
                     name=f"conv_in_g_l{i}")
            taps = jnp.pad(dw_f[j], ((0, CONV_HALO - CONV_WIDTH), (0, 0)))
            conf_args = (pa, pg, taps, vec(dw_b_f[j]), vec(ln_g_f[j]), vec(ln_b_f[j]), seq)
            if i == 0:
                ffn_shards = layer_shards(0)[:2]
                (u, s_), landed = _conf_fwd(*conf_args, name=f"conf_fwd_l{i}", comm=_gather_chips(ffn_shards))
                gathered[0][:2] = _gather_sibling(ffn_shards, landed, name="gather_sibling_ffn_l0")
            else:
                u, s_ = _conf_fwd(*conf_args, name=f"conf_fwd_l{i}")
            cur = _mm(s_, w_out, m=n, n=d, k=d, tk=d, bias=vec(b_out_f[j]), res=cur, name=f"conv_out_l{i}")
            sv.update(h=h, pa=pa, pg=pg, u=u, s=s_, taps=taps)
        elif kind == 1:
            pw = wts[2].reshape(N_CHIPS, ng, gd // N_CHIPS, gd).transpose(1, 0, 2, 3).reshape(ng, gd, gd)
            cur, p = _pool_fwd(cur, vec(norm_mix[i]), pw, vec(pool_b_f[j]), vec(pool_scale[j]), seq,
                               name=f"pool_fwd_l{i}")
            sv.update(p=p, pw=pw)
        else:
            w_in = wts[2].transpose(1, 0, 2).reshape(d, -1)
            w_qkv = w_in[:, :3 * d]
            w_f = jnp.pad(w_in[:, 3 * d:], ((0, 0), (0, LANES - nh)))
            w_o = wts[3].reshape(d, d)
            bf = jnp.pad(vec(fox_b_f[j]), ((0, 0), (0, LANES - nh)))
            qg, kg = jnp.tile(vec(fox_q_gain[j]), (1, nh)), jnp.tile(vec(fox_k_gain[j]), (1, nh))
            h = _rms_fwd(cur, vec(norm_mix[i]), name=f"rms_mix_l{i}")
            qkv = _mm(h, w_qkv, m=n, n=3 * d, k=d, tk=d, tn=d, name=f"fox_qkv_l{i}")
            fl = _mm(h, w_f, m=n, n=LANES, k=d, tk=d, name=f"fox_fl_l{i}")
            qa, ka, v = _fox_prep_fwd(qkv, fl, bf, qg, kg, seq, name=f"fox_prep_l{i}")
            o, lse = _flash_fwd(qa, ka, v, bsz, seq, name=f"fox_attn_l{i}")
            cur = _mm(o, w_o, m=n, n=d, k=d, tk=d, res=cur, name=f"fox_out_l{i}")
            sv.update(h=h, qkv=qkv, fl=fl, qa=qa, ka=ka, v=v, o=o, lse=lse, w_qkv=w_qkv, w_f=w_f, w_o=w_o, bf=bf,
                      qg=qg, kg=kg)
        w_up, w_down = wts[0], wts[1].reshape(f, d)
        ucol = w_up.shape[2]
        sv["x_mid"] = cur
        h2 = _rms_fwd(cur, vec(norm_ffn[i]), name=f"rms_ffn_l{i}")
        uv = _mm(h2, w_up, m=n, n=f, k=d, tk=d, tn=ucol, b_stk=ucol, b_s0=0, name=f"ffn_up_v_l{i}")
        ug = _mm(h2, w_up, m=n, n=f, k=d, tk=d, tn=ucol, b_stk=ucol, b_s0=2, name=f"ffn_up_g_l{i}")
        fdw, fdb = ffn_dw_f[i], ffn_dw_b[i]
        glu_args = (uv, ug, fdw[:, :f], fdw[:, f:], vec(fdb[:f]), vec(fdb[f:]), w_down, cur, seq)
        if i + 1 < depth:
            nxt_shards = layer_shards(i + 1)
            (a_, cur), landed = _ffn_glu_down(*glu_args, name=f"ffn_glu_down_l{i}", comm=_gather_chips(nxt_shards))
            gathered[i + 1] = _gather_sibling(nxt_shards, landed, name=f"gather_sibling_l{i + 1}")
        else:
            a_, cur = _ffn_glu_down(*glu_args, name=f"ffn_glu_down_l{i}")
        sv.update(h2=h2, uv=uv, ug=ug, a=a_)
        saved.append(sv)

    dy, loss_part = _loss(cur, tgt, name="loss")
    loss = lax.psum(loss_part[0, 0], ("x", "y", "c"))

    g_norm_mix, g_norm_ffn = [None] * depth, [None] * depth
    g_ffn_dw_b = [None] * depth
    g_up, g_down = [None] * depth, [None] * depth
    g_conv_in, g_conv_out = [None] * n_conv, [None] * n_conv
    g_pool_w = g_fox_in = g_fox_o = None
    g_pool_scale = g_bf = g_qg = g_kg = None
    part_small = {"b_in": [None] * n_conv, "dw": [None] * n_conv, "dw_b": [None] * n_conv, "ln_g": [None] * n_conv,
                  "ln_b": [None] * n_conv, "b_out": [None] * n_conv, "pool_b": None, "ffn_dw": [None] * depth}

    pending = []
    done = {}

    def take_pending():
        groups = list(pending)
        pending.clear()
        parts = [p_ for g_ in groups for p_ in g_[0]]
        return groups, (_scatter_partials(parts) if parts else None)

    def finish_groups(groups, r2):
        pos = 0
        for parts, tag, sink in groups:
            sink(_rs_finish(parts, r2[pos:pos + len(parts)], chip_idx, tag))
            pos += len(parts)

    def carried(fn, args, name):
        groups, comm = take_pending()
        if comm is None:
            return fn(*args, name=name)
        outs, r2 = fn(*args, name=name, comm=comm)
        finish_groups(groups, r2)
        return outs

    dcur = dy
    for i in reversed(range(depth)):
        kind, j = i % 3, i // 3
        wts, sv = gathered[i], saved[i]
        w_up, w_down = wts[0], wts[1].reshape(f, d)
        ucol = w_up.shape[2]
        fdw, fdb = ffn_dw_f[i], ffn_dw_b[i]
        dw_down = _mm(sv["a"], dcur, m=f, n=d, k=n, tm=f // 2, tn=d, tk=2048, ta=True, out_dtype=BF16, name=f"ffn_down_dw_l{i}")
        duv, dug, dwv, dwg, dbv, dbg = carried(
            _ffn_glu_bwd, (sv["uv"], sv["ug"], dcur, w_down, fdw[:, :f], fdw[:, f:], vec(fdb[:f]), vec(fdb[f:]), seq),
            name=f"ffn_glu_bwd_l{i}")
        dw_up = _mm(sv["h2"], duv, m=d, n=f, k=n, tm=d, tn=ucol, tk=2048, ta=True, out_dtype=BF16, o_stk=ucol, o_s0=0,
                    o_slots=N_CHIPS, name=f"ffn_up_dw_v_l{i}")
        dw_up = _mm(sv["h2"], dug, m=d, n=f, k=n, tm=d, tn=ucol, tk=2048, ta=True, out_dtype=BF16, o_stk=ucol, o_s0=2,
                    o_slots=N_CHIPS, o_buf=dw_up, name=f"ffn_up_dw_g_l{i}")
        dh2 = _mm(duv, w_up, m=n, n=d, k=f, tn=d, tk=f, tb=True, b_stk=ucol, b_s0=0, name=f"ffn_up_dx_v_l{i}")
        dh2 = _mm(dug, w_up, m=n, n=d, k=f, tn=d, tk=f, tb=True, b_stk=ucol, b_s0=2, res=dh2,
                  name=f"ffn_up_dx_g_l{i}")
        dmid, g_norm_ffn[i] = _rms_bwd(sv["x_mid"], dh2, vec(norm_ffn[i]), dcur, name=f"rms_ffn_bwd_l{i}")
        part_small["ffn_dw"][i] = jnp.concatenate([dwv, dwg], axis=1)
        g_ffn_dw_b[i] = jnp.concatenate([dbv, dbg], axis=1)

        def ffn_sink(red, i=i):
            g_up[i], g_down[i] = red[0], red[1]

        pending.append(([dw_up, dw_down.reshape(N_CHIPS, f // N_CHIPS, d)], f"ffn_l{i}", ffn_sink))

        if kind == 0:
            w_in, w_out = wts[2], wts[3].reshape(d, d)
            wcol = w_in.shape[2]
            ds = _mm(dmid, w_out, m=n, n=d, k=d, tk=d, tb=True, name=f"conv_out_dx_l{i}")
            dw_out = _mm(sv["s"], dmid, m=d, n=d, k=n, tm=d, tn=d, tk=2048, ta=True, out_dtype=BF16, name=f"conv_out_dw_l{i}")
            dpa, dpg, ddw, ddwb, dlng, dlnb, dba, dbg_ = carried(
                _conf_bwd, (sv["u"], ds, sv["pa"], sv["pg"], sv["taps"], vec(ln_g_f[j]), vec(ln_b_f[j]), seq),
                name=f"conf_bwd_l{i}")
            dw_in = _mm(sv["h"], dpa, m=d, n=d, k=n, tm=d, tn=wcol, tk=2048, ta=True, out_dtype=BF16, o_stk=wcol, o_s0=0,
                        o_slots=N_CHIPS, name=f"conv_in_dw_a_l{i}")
            dw_in = _mm(sv["h"], dpg, m=d, n=d, k=n, tm=d, tn=wcol, tk=2048, ta=True, out_dtype=BF16, o_stk=wcol, o_s0=2,
                        o_slots=N_CHIPS, o_buf=dw_in, name=f"conv_in_dw_g_l{i}")
            dh = _mm(dpa, w_in, m=n, n=d, k=d, tn=d, tk=d, tb=True, b_stk=wcol, b_s0=0, name=f"conv_in_dx_a_l{i}")
            dh = _mm(dpg, w_in, m=n, n=d, k=d, tn=d, tk=d, tb=True, b_stk=wcol, b_s0=2, res=dh,
                     name=f"conv_in_dx_g_l{i}")
            dcur, g_norm_mix[i], db_out = _rms_bwd(sv["x_in"], dh, vec(norm_mix[i]), dmid, name=f"rms_mix_bwd_l{i}",
                                                   colsum=True)
            part_small["b_in"][j] = jnp.concatenate([dba, dbg_], axis=1)
            part_small["dw"][j] = ddw[:CONV_WIDTH]
            part_small["dw_b"][j], part_small["ln_g"][j], part_small["ln_b"][j] = ddwb, dlng, dlnb
            part_small["b_out"][j] = db_out
            mix_grads = [dw_in, dw_out.reshape(N_CHIPS, d // N_CHIPS, d)]
        elif kind == 1:
            dcur, g_norm_mix[i], dpw, dpb, dpsc = _pool_bwd(dmid, sv["x_in"], sv["p"], vec(norm_mix[i]), sv["pw"],
                                                            vec(pool_b_f[j]), vec(pool_scale[j]), seq,
                                                            name=f"pool_bwd_l{i}")
            part_small["pool_b"] = dpb
            g_pool_scale = dpsc
            mix_grads = [dpw.reshape(ng, N_CHIPS, gd // N_CHIPS, gd).transpose(1, 0, 2, 3).reshape(N_CHIPS, gd, gd)
                         .astype(BF16)]
        else:
            do = _mm(dmid, sv["w_o"], m=n, n=d, k=d, tk=d, tb=True, name=f"fox_out_dx_l{i}")
            dw_o = _mm(sv["o"], dmid, m=d, n=d, k=n, tm=d, tn=d, tk=2048, ta=True, out_dtype=BF16, name=f"fox_out_dw_l{i}")
            dq, dcc, dk, dv, dck = _flash_bwd(sv["qa"], sv["ka"], sv["v"], do, sv["o"],
                                              _row_layout(sv["lse"], _tile(seq, ATTN_TILE)), bsz, seq,
                                              name=f"fox_attn_bwd_l{i}")
            dqkv, dfl, dqg, dkg, dbf = carried(
                _fox_prep_bwd, (sv["qkv"], dq, dk, dv, _from_col_layout(dcc), _from_col_layout(dck), sv["fl"], sv["bf"],
                                sv["qg"], sv["kg"], seq), name=f"fox_prep_bwd_l{i}")
            dw_qkv = _mm(sv["h"], dqkv, m=d, n=3 * d, k=n, tm=d, tn=d, tk=2048, ta=True, out_dtype=BF16, name=f"fox_qkv_dw_l{i}")
            dw_f = _mm(sv["h"], dfl, m=d, n=LANES, k=n, tm=d, tk=2048, ta=True, out_dtype=BF16, name=f"fox_fl_dw_l{i}")
            dh = _mm(dqkv, sv["w_qkv"], m=n, n=d, k=3 * d, tn=d, tk=d, tb=True, name=f"fox_qkv_dx_l{i}")
            dh = _mm(dfl, sv["w_f"], m=n, n=d, k=LANES, tn=d, tb=True, res=dh, name=f"fox_fl_dx_l{i}")
            dcur, g_norm_mix[i] = _rms_bwd(sv["x_in"], dh, vec(norm_mix[i]), dmid, name=f"rms_mix_bwd_l{i}")
            g_bf = dbf[:, :nh]
            g_qg = dqg.reshape(nh, HEAD_DIM).sum(axis=0, keepdims=True)
            g_kg = dkg.reshape(nh, HEAD_DIM).sum(axis=0, keepdims=True)
            dw_in_full = jnp.concatenate([dw_qkv, dw_f[:, :nh]], axis=1)
            wshard = dw_in_full.shape[1] // N_CHIPS
            mix_grads = [dw_in_full.reshape(d, N_CHIPS, wshard).transpose(1, 0, 2),
                         dw_o.reshape(N_CHIPS, d // N_CHIPS, d)]
        if i == 0:
            sm_parts = [jnp.concatenate(part_small["b_in"]), jnp.stack(part_small["dw"]),
                        jnp.concatenate(part_small["dw_b"]), jnp.concatenate(part_small["ln_g"]),
                        jnp.concatenate(part_small["ln_b"]), jnp.concatenate(part_small["b_out"]),
                        part_small["pool_b"].reshape(n_pool, ng, gd), jnp.stack(part_small["ffn_dw"])]
            mix_grads.append(_pack([_to_shards(p_) for p_ in sm_parts], small_rows))

        def mix_sink(red, i=i, kind=kind, j=j):
            if kind == 0:
                g_conv_in[j], g_conv_out[j] = red[0], red[1]
            elif kind == 1:
                done["pool_w"] = red[0]
            else:
                done["fox_in"], done["fox_o"] = red[0], red[1]
            if i == 0:
                done["small"] = red[-1].reshape(-1)

        pending.append((mix_grads, f"mix_l{i}", mix_sink))

    groups, comm = take_pending()
    finish_groups(groups, _run_comm(comm, name="rs_chips_tail"))
    g_pool_w, g_fox_in, g_fox_o, g_small_flat = done["pool_w"], done["fox_in"], done["fox_o"], done["small"]

    grad_x = dcur.reshape(bsz, seq, d)

    rep_parts = [jnp.concatenate(g_norm_mix), jnp.concatenate(g_norm_ffn), g_pool_scale, g_bf, g_qg, g_kg,
                 jnp.concatenate(g_ffn_dw_b)]
    rep_shapes = [norm_mix.shape, norm_ffn.shape, pool_scale.shape, fox_b_f.shape, fox_q_gain.shape,
                  fox_k_gain.shape, ffn_dw_b.shape]
    rep_rows = _pack_rows([math.prod(s) for s in rep_shapes], 8)
    rep = _all_reduce_small(_pack([p_.reshape(1, -1) for p_ in rep_parts], rep_rows)[0], name="all_reduce_small")
    g_rep = _unpack(rep.reshape(-1), rep_shapes)
    g_sm = _unpack(g_small_flat, small_shapes)

    grads = {
        "norm_mix": g_rep[0], "norm_ffn": g_rep[1],
        "conv_w_in": jnp.stack(g_conv_in), "conv_b_in": g_sm[0], "conv_dw": g_sm[1], "conv_dw_b": g_sm[2],
        "conv_ln_g": g_sm[3], "conv_ln_b": g_sm[4], "conv_w_out": jnp.stack(g_conv_out), "conv_b_out": g_sm[5],
        "pool_w": g_pool_w.reshape(pool_w.shape), "pool_b": g_sm[6], "pool_scale": g_rep[2],
        "fox_w_in": g_fox_in.reshape(fox_w_in.shape), "fox_b_f": g_rep[3], "fox_q_gain": g_rep[4],
        "fox_k_gain": g_rep[5], "fox_w_o": g_fox_o.reshape(fox_w_o.shape),
        "ffn_w_up": jnp.stack(g_up), "ffn_dw": g_sm[7], "ffn_dw_b": g_rep[6], "ffn_w_down": jnp.stack(g_down),
    }
    weights = dict(norm_mix=norm_mix, norm_ffn=norm_ffn, conv_w_in=conv_w_in, conv_b_in=conv_b_in, conv_dw=conv_dw,
                   conv_dw_b=conv_dw_b, conv_ln_g=conv_ln_g, conv_ln_b=conv_ln_b, conv_w_out=conv_w_out,
                   conv_b_out=conv_b_out, pool_w=pool_w, pool_b=pool_b, pool_scale=pool_scale, fox_w_in=fox_w_in,
                   fox_b_f=fox_b_f, fox_q_gain=fox_q_gain, fox_k_gain=fox_k_gain, fox_w_o=fox_w_o, ffn_w_up=ffn_w_up,
                   ffn_dw=ffn_dw, ffn_dw_b=ffn_dw_b, ffn_w_down=ffn_w_down)
    m_in = dict(norm_mix=m_norm_mix, norm_ffn=m_norm_ffn, conv_w_in=m_conv_w_in, conv_b_in=m_conv_b_in,
                conv_dw=m_conv_dw, conv_dw_b=m_conv_dw_b, conv_ln_g=m_conv_ln_g, conv_ln_b=m_conv_ln_b,
                conv_w_out=m_conv_w_out, conv_b_out=m_conv_b_out, pool_w=m_pool_w, pool_b=m_pool_b,
                pool_scale=m_pool_scale, fox_w_in=m_fox_w_in, fox_b_f=m_fox_b_f, fox_q_gain=m_fox_q_gain,
                fox_k_gain=m_fox_k_gain, fox_w_o=m_fox_w_o, ffn_w_up=m_ffn_w_up, ffn_dw=m_ffn_dw,
                ffn_dw_b=m_ffn_dw_b, ffn_w_down=m_ffn_w_down)
    v_in = dict(norm_mix=v_norm_mix, norm_ffn=v_norm_ffn, conv_w_in=v_conv_w_in, conv_b_in=v_conv_b_in,
                conv_dw=v_conv_dw, conv_dw_b=v_conv_dw_b, conv_ln_g=v_conv_ln_g, conv_ln_b=v_conv_ln_b,
                conv_w_out=v_conv_w_out, conv_b_out=v_conv_b_out, pool_w=v_pool_w, pool_b=v_pool_b,
                pool_scale=v_pool_scale, fox_w_in=v_fox_w_in, fox_b_f=v_fox_b_f, fox_q_gain=v_fox_q_gain,
                fox_k_gain=v_fox_k_gain, fox_w_o=v_fox_w_o, ffn_w_up=v_ffn_w_up, ffn_dw=v_ffn_dw,
                ffn_dw_b=v_ffn_dw_b, ffn_w_down=v_ffn_w_down)
    names = list(weights)
    g_out, d_out, m_out, v_out = [], [], [], []
    for nm in names:
        g_, dl_, m_, v_ = _adamw_nd(weights[nm], grads[nm].reshape(weights[nm].shape), m_in[nm], v_in[nm],
                                    name=f"adamw_{nm}")
        g_out.append(g_)
        d_out.append(dl_)
        m_out.append(m_)
        v_out.append(v_)
    return (loss, grad_x, *g_out, *d_out, *m_out, *v_out)
```

```python
import math

import jax
import jax.numpy as jnp
import numpy as np
from jax import lax
from jax.experimental import pallas as pl
from jax.experimental.pallas import tpu as pltpu

F32 = jnp.float32
BF16 = jnp.bfloat16
HI = lax.Precision.HIGHEST
MESH = pl.DeviceIdType.MESH
ANY = pl.BlockSpec(memory_space=pl.ANY)

EPS = 1e-6
HEAD_DIM = 64
LANES = 128
POOL_WINDOWS = (2, 4, 8, 16)
CONV_WIDTH = 31
CONV_HALO = 32
FFN_HALO = 8
FFN_ROWS, FFN_COLS = 256, 1408
FFN_DOWN_ROWS, FFN_DOWN_COLS = 128, 2816
POOL_HALO = 16
N_CHIPS = 4
NEG = -1e30

ADAM_LR = 0.001
ADAM_B1 = 0.9
ADAM_B2 = 0.999
ADAM_EPS = 1e-08
ADAM_WD = 0.01
ADAM_STEP = 10

V7X_VMEM_LIMIT_BYTES = 56 * 1024 * 1024


def _cp(*sem):
    return pltpu.CompilerParams(dimension_semantics=sem or None, vmem_limit_bytes=V7X_VMEM_LIMIT_BYTES)


def _tile(n, pref):
    t = min(n, pref)
    assert n % t == 0, (n, pref)
    return t


def _sig(v):
    return jax.nn.sigmoid(v)


def _roll(v, shift):
    n = v.shape[0]
    shift = shift % n
    return v if shift == 0 else pltpu.roll(v, shift, 0)


SUBLANES = 8


CONV_ROWS = 64


def _conv_taps(rot_ref, w_ref, out_ref, tm, start_of):
    d = out_ref.shape[1]
    for lc in range(d // LANES):
        ls = slice(lc * LANES, (lc + 1) * LANES)

        for r0 in range(0, tm, CONV_ROWS):
            acc = None
            for sh in range(CONV_WIDTH):
                kk = CONV_WIDTH - 1 - sh
                lo = r0 + start_of(sh)
                term = w_ref[kk:kk + 1, ls] * rot_ref[sh % SUBLANES, lo:lo + CONV_ROWS, ls]
                acc = term if acc is None else acc + term
            out_ref[r0:r0 + CONV_ROWS, ls] = acc


def _tap_grads(rotz_ref, rotd_ref, dw_ref, tm, halo):
    d = dw_ref.shape[1]
    for lc in range(d // LANES):
        ls = slice(lc * LANES, (lc + 1) * LANES)

        acc = [None] * CONV_WIDTH
        for r0 in range(0, tm, CONV_ROWS):
            duc = rotd_ref[0, r0:r0 + CONV_ROWS, ls]
            for sh in range(CONV_WIDTH):
                lo = r0 + halo - (sh // SUBLANES) * SUBLANES
                prod = duc * rotz_ref[sh % SUBLANES, lo:lo + CONV_ROWS, ls]
                part = prod.reshape(CONV_ROWS // SUBLANES, SUBLANES, LANES).sum(axis=0)
                acc[sh] = part if acc[sh] is None else acc[sh] + part
        for sh in range(CONV_WIDTH):
            kk = CONV_WIDTH - 1 - sh
            dw_ref[kk:kk + 1, ls] += jnp.sum(acc[sh], axis=0, keepdims=True)


class _Comm:
    def __init__(self, ins, out_shapes, n_sems, start, wait):
        self.ins, self.out_shapes, self.n_sems, self.start, self.wait = list(ins), list(out_shapes), n_sems, start, wait


def _hosted(body, comm, *, grid, in_specs, out_specs, out_shape, scratch_shapes, sem, name, ins):
    if comm is None:
        return pl.pallas_call(body, grid=grid, in_specs=in_specs, out_specs=out_specs, out_shape=out_shape,
                              scratch_shapes=scratch_shapes, name=name, compiler_params=_cp(*sem))(*ins)
    n_in, n_out, n_scr = len(in_specs), len(out_specs), len(scratch_shapes)
    nci, nco = len(comm.ins), len(comm.out_shapes)

    def wrapped(*refs):
        pos = [0]

        def take(cnt):
            pos[0] += cnt
            return refs[pos[0] - cnt:pos[0]]

        r_in, c_in, r_out, c_out, r_scr = take(n_in), take(nci), take(n_out), take(nco), take(n_scr)
        send_sems, recv_sems = take(2)
        ids = [pl.program_id(ax) for ax in range(len(grid))]
        first, last = ids[0] == 0, ids[0] == grid[0] - 1
        for ax in range(1, len(grid)):
            first = jnp.logical_and(first, ids[ax] == 0)
            last = jnp.logical_and(last, ids[ax] == grid[ax] - 1)

        @pl.when(first)
        def _():
            comm.start(c_in, c_out, send_sems, recv_sems)

        body(*r_in, *r_out, *r_scr)

        @pl.when(last)
        def _():
            comm.wait(c_in, c_out, send_sems, recv_sems)

    outs = pl.pallas_call(
        wrapped, grid=grid, in_specs=list(in_specs) + [ANY] * nci, out_specs=list(out_specs) + [ANY] * nco,
        out_shape=list(out_shape) + comm.out_shapes,
        scratch_shapes=list(scratch_shapes) + [pltpu.SemaphoreType.DMA((comm.n_sems,))] * 2, name=name,
        compiler_params=pltpu.CompilerParams(dimension_semantics=sem, vmem_limit_bytes=V7X_VMEM_LIMIT_BYTES,
                                             has_side_effects=True),
    )(*ins, *comm.ins)
    return list(outs[:n_out]), list(outs[n_out:])


def _run_comm(comm, name):
    def body(*refs):
        nci, nco = len(comm.ins), len(comm.out_shapes)
        c_in, c_out, send_sems, recv_sems = refs[:nci], refs[nci:nci + nco], refs[-2], refs[-1]
        comm.start(c_in, c_out, send_sems, recv_sems)
        comm.wait(c_in, c_out, send_sems, recv_sems)

    return pl.pallas_call(
        body, in_specs=[ANY] * len(comm.ins), out_specs=[ANY] * len(comm.out_shapes), out_shape=comm.out_shapes,
        scratch_shapes=[pltpu.SemaphoreType.DMA((comm.n_sems,))] * 2, name=name,
        compiler_params=pltpu.CompilerParams(has_side_effects=True),
    )(*comm.ins)


def _mm(a, b, *, m, n, k, name, tm=1024, tn=1024, tk=512, ta=False, tb=False, b_stk=None, b_s0=0,
        o_stk=None, o_s0=0, o_slots=None, o_buf=None, bias=None, res=None, out_dtype=F32):
    tm, tn, tk = _tile(m, tm), _tile(n, tn), _tile(k, tk)
    gi, gj, gk = m // tm, n // tn, k // tk
    nsl = 1
    a_spec = pl.BlockSpec((tk, tm), lambda j, i, kk: (kk, i)) if ta else pl.BlockSpec((tm, tk), lambda j, i, kk: (i, kk))
    if b_stk is None:
        b_spec = pl.BlockSpec((tn, tk), lambda j, i, kk: (j, kk)) if tb else pl.BlockSpec((tk, tn), lambda j, i, kk: (kk, j))
    elif tb and tk > b_stk:
        assert tk % b_stk == 0 and b_s0 % (tk // b_stk) == 0 and not ta
        nsl = tk // b_stk
        b_spec = pl.BlockSpec((nsl, tn, b_stk), lambda j, i, kk: (b_s0 // nsl + kk, j, 0))
    elif tb:
        assert b_stk % tk == 0
        per = b_stk // tk
        b_spec = pl.BlockSpec((None, tn, tk), lambda j, i, kk: (b_s0 + kk // per, j, kk % per))
    else:
        assert b_stk % tn == 0
        per = b_stk // tn
        b_spec = pl.BlockSpec((None, tk, tn), lambda j, i, kk: (b_s0 + j // per, kk, j % per))
    ins, in_specs = [a, b], [a_spec, b_spec]
    if bias is not None:
        ins.append(bias)
        in_specs.append(pl.BlockSpec((1, tn), lambda j, i, kk: (0, j)))
    if res is not None:
        ins.append(res)
        in_specs.append(pl.BlockSpec((tm, tn), lambda j, i, kk: (i, j)))
    aliases = {}
    if o_stk is None:
        out_shape = jax.ShapeDtypeStruct((m, n), out_dtype)
        o_spec = pl.BlockSpec((tm, tn), lambda j, i, kk: (i, j))
    else:
        assert o_stk % tn == 0
        pero = o_stk // tn
        out_shape = jax.ShapeDtypeStruct((o_slots, m, o_stk), out_dtype)
        o_spec = pl.BlockSpec((None, tm, tn), lambda j, i, kk: (o_s0 + j // pero, i, j % pero))
        if o_buf is not None:
            aliases = {len(ins): 0}
            ins.append(o_buf)
            in_specs.append(ANY)
    has_bias, has_res, has_buf = bias is not None, res is not None, o_buf is not None
    dn = (((0 if ta else 1,), (1 if tb else 0,)), ((), ()))

    def body(*refs):
        a_ref, b_ref = refs[0], refs[1]
        pos = 2
        bias_ref = refs[pos] if has_bias else None
        pos += has_bias
        res_ref = refs[pos] if has_res else None
        pos += has_res + has_buf
        o_ref = refs[pos]
        if nsl == 1:
            p = lax.dot_general(a_ref[...].astype(BF16), b_ref[...].astype(BF16), dn, preferred_element_type=F32)
        else:
            p = sum(lax.dot_general(a_ref[:, s * b_stk:(s + 1) * b_stk].astype(BF16), b_ref[s].astype(BF16), dn,
                                    preferred_element_type=F32) for s in range(nsl))

        def finish(acc):
            if has_bias:
                acc = acc + bias_ref[...]
            if has_res:
                acc = acc + res_ref[...]
            o_ref[...] = acc.astype(o_ref.dtype)

        if gk == 1:
            finish(p)
        else:
            acc_ref = refs[pos + 1]
            kk = pl.program_id(2)

            @pl.when(kk == 0)
            def _():
                acc_ref[...] = p

            @pl.when(kk > 0)
            def _():
                acc_ref[...] += p

            @pl.when(kk == gk - 1)
            def _():
                finish(acc_ref[...])

    return pl.pallas_call(
        body, grid=(gj, gi, gk), in_specs=in_specs, out_specs=o_spec, out_shape=out_shape,
        scratch_shapes=[pltpu.VMEM((tm, tn), F32)] if gk > 1 else [],
        input_output_aliases=aliases, name=name,
        compiler_params=_cp("parallel", "parallel", "arbitrary"),
    )(*ins)


def _rms_fwd(x, g, name):
    n, d = x.shape
    tm = _tile(n, 512)

    def body(x_ref, g_ref, h_ref):
        xv = x_ref[...]
        r = lax.rsqrt(jnp.mean(xv * xv, axis=-1, keepdims=True) + EPS)
        h_ref[...] = (xv * r * g_ref[...]).astype(h_ref.dtype)

    return pl.pallas_call(
        body, grid=(n // tm,),
        in_specs=[pl.BlockSpec((tm, d), lambda i: (i, 0)), pl.BlockSpec((1, d), lambda i: (0, 0))],
        out_specs=pl.BlockSpec((tm, d), lambda i: (i, 0)),
        out_shape=jax.ShapeDtypeStruct((n, d), BF16), name=name, compiler_params=_cp("arbitrary"),
    )(x, g)


def _rms_bwd(x, dh, g, dres, name, colsum=False):
    n, d = x.shape
    tm = _tile(n, 512)

    def body(x_ref, dh_ref, g_ref, dres_ref, dx_ref, dg_ref, *rest):
        i = pl.program_id(0)
        xv, dhv = x_ref[...], dh_ref[...]
        r = lax.rsqrt(jnp.mean(xv * xv, axis=-1, keepdims=True) + EPS)
        xn = xv * r
        dxn = dhv * g_ref[...]
        dx_ref[...] = dres_ref[...] + r * (dxn - xn * jnp.mean(dxn * xn, axis=-1, keepdims=True))
        dg = jnp.sum(dhv * xn, axis=0, keepdims=True)

        @pl.when(i == 0)
        def _():
            dg_ref[...] = jnp.zeros_like(dg_ref)
            if colsum:
                rest[0][...] = jnp.zeros_like(rest[0])

        dg_ref[...] += dg
        if colsum:
            rest[0][...] += jnp.sum(dres_ref[...], axis=0, keepdims=True)

    row = pl.BlockSpec((tm, d), lambda i: (i, 0))
    vec = pl.BlockSpec((1, d), lambda i: (0, 0))
    out_shape = [jax.ShapeDtypeStruct((n, d), F32), jax.ShapeDtypeStruct((1, d), F32)]
    out_specs = [row, vec]
    if colsum:
        out_shape.append(jax.ShapeDtypeStruct((1, d), F32))
        out_specs.append(vec)
    return pl.pallas_call(
        body, grid=(n // tm,), in_specs=[row, row, vec, row], out_specs=out_specs, out_shape=out_shape,
        name=name, compiler_params=_cp("arbitrary"),
    )(x, dh, g, dres)


def _loss(y, tgt, name):
    n, d = y.shape
    tm = _tile(n, 512)

    def body(y_ref, t_ref, dy_ref, l_ref):
        i = pl.program_id(0)
        e = y_ref[...] - t_ref[...]
        dy_ref[...] = e / d
        part = 0.5 * jnp.sum(jnp.mean(e * e, axis=-1, keepdims=True), axis=0, keepdims=True)

        @pl.when(i == 0)
        def _():
            l_ref[...] = jnp.zeros_like(l_ref)

        l_ref[...] += part

    row = pl.BlockSpec((tm, d), lambda i: (i, 0))
    return pl.pallas_call(
        body, grid=(n // tm,), in_specs=[row, row],
        out_specs=[row, pl.BlockSpec((1, 1), lambda i: (0, 0))],
        out_shape=[jax.ShapeDtypeStruct((n, d), F32), jax.ShapeDtypeStruct((1, 1), F32)],
        name=name, compiler_params=_cp("arbitrary"),
    )(y, tgt)


def _ffn_specs(n, f, tm, tc, seq):
    hb = FFN_HALO
    cur = pl.BlockSpec((tm, tc), lambda j, i: (i, j))
    prev = pl.BlockSpec((hb, tc), lambda j, i: (jnp.maximum(i * (tm // hb) - 1, 0), j))
    nxt = pl.BlockSpec((hb, tc), lambda j, i: (jnp.minimum((i + 1) * (tm // hb), n // hb - 1), j))
    taps = pl.BlockSpec((3, tc), lambda j, i: (0, j))
    vec = pl.BlockSpec((1, tc), lambda j, i: (0, j))
    return cur, prev, nxt, taps, vec


def _ffn_glu_down(uv, ug, wv, wg, bv, bg, w_down, res, seq, name, comm=None):
    n, f = uv.shape
    d = w_down.shape[1]
    tm, tc = _tile(seq, FFN_DOWN_ROWS), _tile(f, FFN_DOWN_COLS)
    tps = seq // tm
    nj = f // tc
    hb = FFN_HALO
    cur = pl.BlockSpec((tm, tc), lambda i, j: (i, j))
    prev = pl.BlockSpec((hb, tc), lambda i, j: (jnp.maximum(i * (tm // hb) - 1, 0), j))
    taps = pl.BlockSpec((3, tc), lambda i, j: (0, j))
    vec = pl.BlockSpec((1, tc), lambda i, j: (0, j))
    wblk = pl.BlockSpec((tc, d), lambda i, j: (j, 0))
    row = pl.BlockSpec((tm, d), lambda i, j: (i, 0))

    def body(uvp, uvc, ugp, ugc, wv_ref, wg_ref, bv_ref, bg_ref, wd_ref, res_ref, a_ref, x_ref, acc_ref):
        first = (pl.program_id(0) % tps) == 0
        j = pl.program_id(1)

        def conv(p_ref, c_ref, w_ref, b_ref):
            xs = jnp.concatenate([jnp.where(first, 0.0, p_ref[...]), c_ref[...]], axis=0)
            w = w_ref[...]
            y = w[2:3] * xs + w[1:2] * _roll(xs, 1) + w[0:1] * _roll(xs, 2)
            return y[FFN_HALO:] + b_ref[...]

        val = conv(uvp, uvc, wv_ref, bv_ref)
        gate = conv(ugp, ugc, wg_ref, bg_ref)
        a = (gate * _sig(gate) * val).astype(BF16)
        a_ref[...] = a
        p = jnp.dot(a, wd_ref[...], preferred_element_type=F32)

        @pl.when(j == 0)
        def _():
            acc_ref[...] = res_ref[...] + p

        @pl.when(j > 0)
        def _():
            acc_ref[...] += p

        @pl.when(j == nj - 1)
        def _():
            x_ref[...] = acc_ref[...]

    return _hosted(
        body, comm, grid=(n // tm, nj), in_specs=[prev, cur, prev, cur, taps, taps, vec, vec, wblk, row],
        out_specs=[cur, row], out_shape=[jax.ShapeDtypeStruct((n, f), BF16), jax.ShapeDtypeStruct((n, d), F32)],
        scratch_shapes=[pltpu.VMEM((tm, d), F32)], sem=("parallel", "arbitrary"), name=name,
        ins=(uv, uv, ug, ug, wv, wg, bv, bg, w_down, res))


def _ffn_glu_bwd(uv, ug, dx, w_down, wv, wg, bv, bg, seq, name, comm=None):
    n, f = uv.shape
    d = dx.shape[1]
    tm, tc = _tile(seq, FFN_ROWS), _tile(f, FFN_COLS)
    tps = seq // tm
    hb = FFN_HALO
    ext = tm + hb
    cur, prev, nxt, taps, vec = _ffn_specs(n, f, tm, tc, seq)
    dx_cur = pl.BlockSpec((tm, d), lambda j, i: (i, 0))
    dx_nxt = pl.BlockSpec((hb, d), lambda j, i: (jnp.minimum((i + 1) * (tm // hb), n // hb - 1), 0))
    wblk = pl.BlockSpec((tc, d), lambda j, i: (j, 0))

    def body(uvp, uvc, uvn, ugp, ugc, ugn, dx_c, dx_n, wd_ref, wv_ref, wg_ref, bv_ref, bg_ref,
             duv_ref, dug_ref, dwv_ref, dwg_ref, dbv_ref, dbg_ref):
        i = pl.program_id(1)
        first = (i % tps) == 0
        last = (i % tps) == tps - 1
        dx_e = jnp.concatenate([dx_c[...], jnp.where(last, 0.0, dx_n[...])], axis=0).astype(BF16)
        da_e = lax.dot_general(dx_e, wd_ref[...], _NT, preferred_element_type=F32)

        def taps3(p_ref, c_ref, n_ref):
            xs = jnp.concatenate([jnp.where(first, 0.0, p_ref[...]), c_ref[...], n_ref[...]], axis=0)
            return xs, _roll(xs, 1), _roll(xs, 2)

        xv, xg = taps3(uvp, uvc, uvn), taps3(ugp, ugc, ugn)
        wv_, wg_ = wv_ref[...], wg_ref[...]

        def conv(xs, w, b_ref):
            return (w[2:3] * xs[0] + w[1:2] * xs[1] + w[0:1] * xs[2])[hb:] + b_ref[...]

        val, gate = conv(xv, wv_, bv_ref), conv(xg, wg_, bg_ref)
        sg = _sig(gate)
        dval = da_e * (gate * sg)
        dgate = da_e * val * (sg * (1.0 + gate * (1.0 - sg)))

        def conv_t(dv, w):
            return (w[2:3] * dv + w[1:2] * _roll(dv, ext - 1) + w[0:1] * _roll(dv, ext - 2))[:tm]

        duv_ref[...] = conv_t(dval, wv_).astype(duv_ref.dtype)
        dug_ref[...] = conv_t(dgate, wg_).astype(dug_ref.dtype)

        def tap_grads(d_own, xs):
            return jnp.concatenate(
                [jnp.sum(d_own * xs[2 - kk][hb:hb + tm], axis=0, keepdims=True) for kk in range(3)], axis=0)

        dv_own, dg_own = dval[:tm], dgate[:tm]

        @pl.when(i == 0)
        def _():
            for r in (dwv_ref, dwg_ref, dbv_ref, dbg_ref):
                r[...] = jnp.zeros_like(r)

        dwv_ref[...] += tap_grads(dv_own, xv)
        dwg_ref[...] += tap_grads(dg_own, xg)
        dbv_ref[...] += jnp.sum(dv_own, axis=0, keepdims=True)
        dbg_ref[...] += jnp.sum(dg_own, axis=0, keepdims=True)

    return _hosted(
        body, comm, grid=(f // tc, n // tm),
        in_specs=[prev, cur, nxt, prev, cur, nxt, dx_cur, dx_nxt, wblk, taps, taps, vec, vec],
        out_specs=[cur, cur, taps, taps, vec, vec],
        out_shape=[jax.ShapeDtypeStruct((n, f), BF16), jax.ShapeDtypeStruct((n, f), BF16),
                   jax.ShapeDtypeStruct((3, f), F32), jax.ShapeDtypeStruct((3, f), F32),
                   jax.ShapeDtypeStruct((1, f), F32), jax.ShapeDtypeStruct((1, f), F32)],
        scratch_shapes=[], sem=("parallel", "arbitrary"), name=name,
        ins=(uv, uv, uv, ug, ug, ug, dx, dx, w_down, wv, wg, bv, bg))


def _conf_specs(n, d, tm):
    hb = CONV_HALO
    cur = pl.BlockSpec((tm, d), lambda i: (i, 0))
    prev = pl.BlockSpec((hb, d), lambda i: (jnp.maximum(i * (tm // hb) - 1, 0), 0))
    nxt = pl.BlockSpec((hb, d), lambda i: (jnp.minimum((i + 1) * (tm // hb), n // hb - 1), 0))
    taps = pl.BlockSpec((CONV_HALO, d), lambda i: (0, 0))
    vec = pl.BlockSpec((1, d), lambda i: (0, 0))
    return cur, prev, nxt, taps, vec


def _conf_fwd(pa, pg, w, wb, lng, lnb, seq, name, comm=None):
    n, d = pa.shape
    tm = _tile(seq, 256)
    tps = seq // tm
    hb = CONV_HALO
    cur, prev, _, taps, vec = _conf_specs(n, d, tm)

    def body(pap, pac, pgp, pgc, w_ref, wb_ref, lng_ref, lnb_ref, u_ref, s_ref, rot_ref):
        first = (pl.program_id(0) % tps) == 0
        a = jnp.concatenate([jnp.where(first, 0.0, pap[...]), pac[...]], axis=0)
        g = jnp.concatenate([jnp.where(first, 0.0, pgp[...]), pgc[...]], axis=0)
        z = a * _sig(g)
        for b in range(SUBLANES):
            rot_ref[b] = _roll(z, b)
        _conv_taps(rot_ref, w_ref, u_ref, tm, lambda sh: hb - (sh // SUBLANES) * SUBLANES)
        u = u_ref[...] + wb_ref[...]
        mu = jnp.mean(u, axis=-1, keepdims=True)
        uc = u - mu
        var = jnp.mean(uc * uc, axis=-1, keepdims=True)
        ul = uc * lax.rsqrt(var + EPS) * lng_ref[...] + lnb_ref[...]
        u_ref[...] = u
        s_ref[...] = (ul * _sig(ul)).astype(s_ref.dtype)

    return _hosted(
        body, comm, grid=(n // tm,), in_specs=[prev, cur, prev, cur, taps, vec, vec, vec],
        out_specs=[cur, cur],
        out_shape=[jax.ShapeDtypeStruct((n, d), F32), jax.ShapeDtypeStruct((n, d), BF16)],
        scratch_shapes=[pltpu.VMEM((SUBLANES, tm + hb, d), F32)], sem=("arbitrary",), name=name,
        ins=(pa, pa, pg, pg, w, wb, lng, lnb))


def _conf_bwd(u, ds, pa, pg, w, lng, lnb, seq, name, comm=None):
    n, d = u.shape
    tm = _tile(seq, 256)
    tps = seq // tm
    hb = CONV_HALO
    ext = tm + hb
    cur, prev, nxt, taps, vec = _conf_specs(n, d, tm)

    def body(uc_ref, un_ref, dsc_ref, dsn_ref, pap, pac, pgp, pgc, w_ref, lng_ref, lnb_ref,
             dpa_ref, dpg_ref, dw_ref, dwb_ref, dlng_ref, dlnb_ref, dba_ref, dbg_ref, rotz_ref, rotd_ref, dz_ref):
        i = pl.program_id(0)
        first = (i % tps) == 0
        last = (i % tps) == tps - 1

        @pl.when(i == 0)
        def _():
            for r in (dw_ref, dwb_ref, dlng_ref, dlnb_ref, dba_ref, dbg_ref):
                r[...] = jnp.zeros_like(r)

        ue = jnp.concatenate([uc_ref[...], un_ref[...]], axis=0)
        dse = jnp.concatenate([dsc_ref[...], jnp.where(last, 0.0, dsn_ref[...])], axis=0)
        mu = jnp.mean(ue, axis=-1, keepdims=True)
        cen = ue - mu
        r = lax.rsqrt(jnp.mean(cen * cen, axis=-1, keepdims=True) + EPS)
        xn = cen * r
        ul = xn * lng_ref[...] + lnb_ref[...]
        sg = _sig(ul)
        dul = dse * (sg * (1.0 + ul * (1.0 - sg)))
        dun = dul * lng_ref[...]
        du = r * (dun - jnp.mean(dun, axis=-1, keepdims=True) - xn * jnp.mean(dun * xn, axis=-1, keepdims=True))
        dlng_ref[...] += jnp.sum((dul * xn)[:tm], axis=0, keepdims=True)
        dlnb_ref[...] += jnp.sum(dul[:tm], axis=0, keepdims=True)
        dwb_ref[...] += jnp.sum(du[:tm], axis=0, keepdims=True)
        for b in range(SUBLANES):
            rotd_ref[b] = _roll(du, ext - b)
        _conv_taps(rotd_ref, w_ref, dz_ref, tm, lambda sh: (sh // SUBLANES) * SUBLANES)
        dz = dz_ref[...]

        a = jnp.concatenate([jnp.where(first, 0.0, pap[...]), pac[...]], axis=0)
        g = jnp.concatenate([jnp.where(first, 0.0, pgp[...]), pgc[...]], axis=0)
        sgg = _sig(g)
        z = a * sgg
        for b in range(SUBLANES):
            rotz_ref[b] = _roll(z, b)
        _tap_grads(rotz_ref, rotd_ref, dw_ref, tm, hb)

        a_c, sg_c = a[hb:], sgg[hb:]
        da = dz * sg_c
        dg = dz * a_c * sg_c * (1.0 - sg_c)
        dpa_ref[...] = da.astype(dpa_ref.dtype)
        dpg_ref[...] = dg.astype(dpg_ref.dtype)
        dba_ref[...] += jnp.sum(da, axis=0, keepdims=True)
        dbg_ref[...] += jnp.sum(dg, axis=0, keepdims=True)

    vshape = jax.ShapeDtypeStruct((1, d), F32)
    return _hosted(
        body, comm, grid=(n // tm,),
        in_specs=[cur, nxt, cur, nxt, prev, cur, prev, cur, taps, vec, vec],
        out_specs=[cur, cur, taps, vec, vec, vec, vec, vec],
        out_shape=[jax.ShapeDtypeStruct((n, d), BF16), jax.ShapeDtypeStruct((n, d), BF16),
                   jax.ShapeDtypeStruct((CONV_HALO, d), F32), vshape, vshape, vshape, vshape, vshape],
        scratch_shapes=[pltpu.VMEM((SUBLANES, ext, d), F32), pltpu.VMEM((SUBLANES, ext, d), F32),
                        pltpu.VMEM((tm, d), F32)],
        sem=("arbitrary",), name=name, ins=(u, u, ds, ds, pa, pa, pg, pg, w, lng, lnb))


def _pool_specs(n, d, tm, gd):
    hb = POOL_HALO
    cur = pl.BlockSpec((tm, d), lambda i: (i, 0))
    prev = pl.BlockSpec((hb, d), lambda i: (jnp.maximum(i * (tm // hb) - 1, 0), 0))
    nxt = pl.BlockSpec((hb, d), lambda i: (jnp.minimum((i + 1) * (tm // hb), n // hb - 1), 0))
    wsp = pl.BlockSpec((len(POOL_WINDOWS), gd, gd), lambda i: (0, 0, 0))
    vec = pl.BlockSpec((1, d), lambda i: (0, 0))
    return cur, prev, nxt, wsp, vec


def _pool_fwd(x, g, w, b, sc, seq, name):
    n, d = x.shape
    gd = d // len(POOL_WINDOWS)
    tm = _tile(seq, 256)
    tps = seq // tm
    hb = POOL_HALO
    cur, prev, _, wsp, vec = _pool_specs(n, d, tm, gd)

    def body(xp, xc, g_ref, w_ref, b_ref, sc_ref, x1_ref, p_ref):
        i = pl.program_id(0)
        first = (i % tps) == 0
        xe = jnp.concatenate([jnp.where(first, 0.0, xp[...]), xc[...]], axis=0)
        r = lax.rsqrt(jnp.mean(xe * xe, axis=-1, keepdims=True) + EPS)
        h = xe * r * g_ref[...]
        t = ((i % tps) * tm + lax.broadcasted_iota(jnp.int32, (tm, 1), 0) + 1).astype(F32)
        ys = []
        for gi, win in enumerate(POOL_WINDOWS):
            hg = h[:, gi * gd:(gi + 1) * gd]
            s, sh = hg, 1
            while sh < win:
                s = s + _roll(s, sh)
                sh *= 2
            p = (s[hb:] / jnp.minimum(t, float(win)) - hg[hb:]).astype(BF16)
            p_ref[:, gi * gd:(gi + 1) * gd] = p
            ys.append(jnp.dot(p, w_ref[gi], preferred_element_type=F32))
        y = jnp.concatenate(ys, axis=1) + b_ref[...]
        x1_ref[...] = xc[...] + y * sc_ref[...]

    return pl.pallas_call(
        body, grid=(n // tm,), in_specs=[prev, cur, vec, wsp, vec, vec], out_specs=[cur, cur],
        out_shape=[jax.ShapeDtypeStruct((n, d), F32), jax.ShapeDtypeStruct((n, d), BF16)],
        name=name, compiler_params=_cp("arbitrary"),
    )(x, x, g, w, b, sc)


def _pool_bwd(dx1, x, p, g, w, b, sc, seq, name):
    n, d = x.shape
    ng = len(POOL_WINDOWS)
    gd = d // ng
    tm = _tile(seq, 256)
    tps = seq // tm
    hb = POOL_HALO
    ext = tm + hb
    cur, _, nxt, wsp, vec = _pool_specs(n, d, tm, gd)

    def body(dc_ref, dn_ref, x_ref, p_ref, g_ref, w_ref, b_ref, sc_ref, dx_ref, dg_ref, dw_ref, db_ref, dsc_ref):
        i = pl.program_id(0)
        last = (i % tps) == tps - 1

        @pl.when(i == 0)
        def _():
            for r_ in (dg_ref, dw_ref, db_ref, dsc_ref):
                r_[...] = jnp.zeros_like(r_)

        dxc = dc_ref[...]
        dxe = jnp.concatenate([dxc, jnp.where(last, 0.0, dn_ref[...])], axis=0)
        dyg = dxe * sc_ref[...]
        t = ((i % tps) * tm + lax.broadcasted_iota(jnp.int32, (ext, 1), 0) + 1).astype(F32)
        dhs = []
        for gi, win in enumerate(POOL_WINDOWS):
            sl = slice(gi * gd, (gi + 1) * gd)
            dyb = dyg[:, sl].astype(BF16)
            wg = w_ref[gi]
            dp = lax.dot_general(dyb, wg, (((1,), (1,)), ((), ())), preferred_element_type=F32)
            s, sh = dp / jnp.minimum(t, float(win)), 1
            while sh < win:
                s = s + _roll(s, ext - sh)
                sh *= 2
            dhs.append((s - dp)[:tm])
            pg = p_ref[:, sl]
            dw_ref[gi] += lax.dot_general(pg, dyb[:tm], (((0,), (0,)), ((), ())), preferred_element_type=F32)
            ypre = jnp.dot(pg, wg, preferred_element_type=F32) + b_ref[:, sl]
            dsc_ref[:, sl] += jnp.sum(dxc[:, sl] * ypre, axis=0, keepdims=True)
            db_ref[:, sl] += jnp.sum(dyg[:tm, sl], axis=0, keepdims=True)
        dh = jnp.concatenate(dhs, axis=1)
        xv = x_ref[...]
        r = lax.rsqrt(jnp.mean(xv * xv, axis=-1, keepdims=True) + EPS)
        xn = xv * r
        dxn = dh * g_ref[...]
        dx_ref[...] = dxc + r * (dxn - xn * jnp.mean(dxn * xn, axis=-1, keepdims=True))
        dg_ref[...] += jnp.sum(dh * xn, axis=0, keepdims=True)

    vshape = jax.ShapeDtypeStruct((1, d), F32)
    return pl.pallas_call(
        body, grid=(n // tm,), in_specs=[cur, nxt, cur, cur, vec, wsp, vec, vec],
        out_specs=[cur, vec, wsp, vec, vec],
        out_shape=[jax.ShapeDtypeStruct((n, d), F32), vshape, jax.ShapeDtypeStruct((ng, gd, gd), F32), vshape, vshape],
        name=name, compiler_params=_cp("arbitrary"),
    )(dx1, dx1, x, p, g, w, b, sc)


def _head_maps(d):
    hd = lax.broadcasted_iota(jnp.int32, (d, LANES), 0) // HEAD_DIM
    col = lax.broadcasted_iota(jnp.int32, (d, LANES), 1)
    gm = (hd == col).astype(BF16)
    hd_t = lax.broadcasted_iota(jnp.int32, (LANES, d), 1) // HEAD_DIM
    row = lax.broadcasted_iota(jnp.int32, (LANES, d), 0)
    gt = (hd_t == row).astype(BF16)
    return gm, gt


def _dot_split(v, onehot):
    hi = v.astype(BF16)
    lo = (v - hi.astype(F32)).astype(BF16)
    return jnp.dot(hi, onehot, preferred_element_type=F32) + jnp.dot(lo, onehot, preferred_element_type=F32)


def _bias_placement(nh):
    pq = np.zeros((3 * LANES, nh * HEAD_DIM), np.float32)
    pk = np.zeros((3 * LANES, nh * HEAD_DIM), np.float32)
    oq = np.zeros((1, nh * HEAD_DIM), np.float32)
    ok = np.zeros((1, nh * HEAD_DIM), np.float32)
    for h in range(nh):
        for piece in range(3):
            pq[piece * LANES + h, h * HEAD_DIM + piece] = 1.0
            pk[piece * LANES + h, h * HEAD_DIM + 3 + piece] = -1.0
            oq[0, h * HEAD_DIM + 3 + piece] = 1.0
            ok[0, h * HEAD_DIM + piece] = 1.0
    return jnp.asarray(pq, BF16), jnp.asarray(pk, BF16), jnp.asarray(oq), jnp.asarray(ok)


def _fox_prep_fwd(qkv, fl, bf, qg, kg, seq, name):
    n, d3 = qkv.shape
    d = d3 // 3
    nh = d // HEAD_DIM
    tm = _tile(seq, 256)
    tps = seq // tm
    scale = 1.0 / math.sqrt(HEAD_DIM)
    pq, pk, oq, ok = _bias_placement(nh)

    def body(qkv_ref, fl_ref, bf_ref, qg_ref, kg_ref, pq_ref, pk_ref, oq_ref, ok_ref, q_ref, k_ref, v_ref, carry):
        first = (pl.program_id(0) % tps) == 0
        gm, gt = _head_maps(d)

        def head_norm(xr, gain):
            r = lax.rsqrt(_dot_split(xr * xr, gm) / HEAD_DIM + EPS)
            return xr * _dot_split(r, gt) * gain

        qs = (head_norm(qkv_ref[:, :d], qg_ref[...]).astype(BF16).astype(F32) * scale).astype(BF16)
        kn = head_norm(qkv_ref[:, d:2 * d], kg_ref[...]).astype(BF16)
        v_ref[...] = qkv_ref[:, 2 * d:].astype(BF16)
        z = fl_ref[...] + bf_ref[...]
        logf = jnp.minimum(z, 0.0) - jnp.log1p(jnp.exp(-jnp.abs(z)))
        tri = (lax.broadcasted_iota(jnp.int32, (tm, tm), 0) >= lax.broadcasted_iota(jnp.int32, (tm, tm), 1)).astype(F32)

        @pl.when(first)
        def _():
            carry[...] = jnp.zeros_like(carry)

        c = jnp.dot(tri, logf, precision=HI, preferred_element_type=F32) + carry[...]
        carry[...] = c[tm - 1:tm, :]
        c1 = c.astype(BF16)
        r1 = c - c1.astype(F32)
        c2 = r1.astype(BF16)
        c3 = (r1 - c2.astype(F32)).astype(BF16)
        pieces = jnp.concatenate([c1, c2, c3], axis=1)
        eq = (jnp.dot(pieces, pq_ref[...], preferred_element_type=F32) + oq_ref[...]).astype(BF16)
        ek = (jnp.dot(pieces, pk_ref[...], preferred_element_type=F32) + ok_ref[...]).astype(BF16)
        for h in range(nh):
            lo, hi = h * HEAD_DIM, (h + 1) * HEAD_DIM
            q_ref[:, 2 * lo:2 * lo + HEAD_DIM] = qs[:, lo:hi]
            q_ref[:, 2 * lo + HEAD_DIM:2 * hi] = eq[:, lo:hi]
            k_ref[:, 2 * lo:2 * lo + HEAD_DIM] = kn[:, lo:hi]
            k_ref[:, 2 * lo + HEAD_DIM:2 * hi] = ek[:, lo:hi]

    row = lambda w: pl.BlockSpec((tm, w), lambda i: (i, 0))
    vec = lambda w: pl.BlockSpec((1, w), lambda i: (0, 0))
    full = lambda a: pl.BlockSpec(a.shape, lambda i: (0, 0))
    return pl.pallas_call(
        body, grid=(n // tm,),
        in_specs=[row(d3), row(LANES), vec(LANES), vec(d), vec(d), full(pq), full(pk), full(oq), full(ok)],
        out_specs=[row(2 * d), row(2 * d), row(d)],
        out_shape=[jax.ShapeDtypeStruct((n, 2 * d), BF16)] * 2 + [jax.ShapeDtypeStruct((n, d), BF16)],
        scratch_shapes=[pltpu.VMEM((1, LANES), F32)], name=name, compiler_params=_cp("arbitrary"),
    )(qkv, fl, bf, qg, kg, pq, pk, oq, ok)


def _fox_prep_bwd(qkv, dq, dk, dv, dc1, dc2, fl, bf, qg, kg, seq, name, comm=None):
    n, d3 = qkv.shape
    d = d3 // 3
    tm = _tile(seq, 256)
    tps = seq // tm
    nt = n // tm

    def body(qkv_ref, dq_ref, dk_ref, dv_ref, dc1_ref, dc2_ref, fl_ref, bf_ref, qg_ref, kg_ref,
             dqkv_ref, dfl_ref, dqg_ref, dkg_ref, dbf_ref, carry):
        i = pl.program_id(0)
        tile = nt - 1 - i
        last = (tile % tps) == tps - 1
        gm, gt = _head_maps(d)

        @pl.when(i == 0)
        def _():
            for r_ in (dqg_ref, dkg_ref, dbf_ref):
                r_[...] = jnp.zeros_like(r_)

        @pl.when(last)
        def _():
            carry[...] = jnp.zeros_like(carry)

        def head_norm_bwd(xr, dy, gain, dgain_ref):
            rf = _dot_split(lax.rsqrt(_dot_split(xr * xr, gm) / HEAD_DIM + EPS), gt)
            xn = xr * rf
            dgain_ref[...] += jnp.sum(dy * xn, axis=0, keepdims=True)
            dyg = dy * gain
            mean = _dot_split(dyg * xn, gm) / HEAD_DIM
            return rf * (dyg - xn * _dot_split(mean, gt))

        dqkv_ref[:, :d] = head_norm_bwd(qkv_ref[:, :d], dq_ref[...], qg_ref[...], dqg_ref).astype(BF16)
        dqkv_ref[:, d:2 * d] = head_norm_bwd(qkv_ref[:, d:2 * d], dk_ref[...], kg_ref[...], dkg_ref).astype(BF16)
        dqkv_ref[:, 2 * d:] = dv_ref[...].astype(BF16)

        dc = dc1_ref[...] + dc2_ref[...]
        tri = (lax.broadcasted_iota(jnp.int32, (tm, tm), 0) <= lax.broadcasted_iota(jnp.int32, (tm, tm), 1)).astype(F32)
        dlog = jnp.dot(tri, dc, precision=HI, preferred_element_type=F32) + carry[...]
        carry[...] = dlog[0:1, :]
        dfl = dlog * (1.0 - _sig(fl_ref[...] + bf_ref[...]))
        dfl_ref[...] = dfl.astype(BF16)
        dbf_ref[...] += jnp.sum(dfl, axis=0, keepdims=True)

    row = lambda w: pl.BlockSpec((tm, w), lambda i: (nt - 1 - i, 0))
    vec = lambda w: pl.BlockSpec((1, w), lambda i: (0, 0))
    return _hosted(
        body, comm, grid=(nt,),
        in_specs=[row(d3), row(d), row(d), row(d), row(LANES), row(LANES), row(LANES), vec(LANES), vec(d), vec(d)],
        out_specs=[row(d3), row(LANES), vec(d), vec(d), vec(LANES)],
        out_shape=[jax.ShapeDtypeStruct((n, d3), BF16), jax.ShapeDtypeStruct((n, LANES), BF16),
                   jax.ShapeDtypeStruct((1, d), F32), jax.ShapeDtypeStruct((1, d), F32),
                   jax.ShapeDtypeStruct((1, LANES), F32)],
        scratch_shapes=[pltpu.VMEM((1, LANES), F32)], sem=("arbitrary",), name=name,
        ins=(qkv, dq, dk, dv, dc1, dc2, fl, bf, qg, kg))


def _attn_specs(bsz, seq, t):
    nb = seq // t
    blk = lambda w: pl.BlockSpec((t, w), lambda b, h, i: (b * nb + i, h))
    full = lambda w: pl.BlockSpec((seq, w), lambda b, h, i: (b, h))
    col = pl.BlockSpec((None, None, t, 2), lambda b, h, i: (b, h, i, 0))
    rows = pl.BlockSpec((None, None, nb, 2, t), lambda b, h, i: (b, h, 0, 0, 0))
    return nb, blk, full, col, rows


_NT = (((1,), (1,)), ((), ()))
ATTN_TILE = 512


def _head_lanes(t, hh):
    lane = lax.broadcasted_iota(jnp.int32, (t, LANES), 1)
    return (lane < HEAD_DIM) if hh == 0 else (lane >= HEAD_DIM)


def _flash_fwd(qa, ka, v, bsz, seq, name):
    n, d = v.shape
    hp = d // LANES
    t = _tile(seq, ATTN_TILE)
    nb, blk, full, col, _ = _attn_specs(bsz, seq, t)

    def body(q_ref, k_ref, v_ref, o_ref, lse_ref):
        i = pl.program_id(2)
        causal = lax.broadcasted_iota(jnp.int32, (t, t), 0) >= lax.broadcasted_iota(jnp.int32, (t, t), 1)

        def block(j, carry, masked):
            rs = pl.ds(pl.multiple_of(j * t, t), t)
            vj = v_ref[rs, :]
            out = []
            for hh in range(2):
                m, l, acc = carry[hh]
                hs = slice(hh * LANES, (hh + 1) * LANES)
                sc = lax.dot_general(q_ref[:, hs], k_ref[rs, hs], _NT, preferred_element_type=F32)
                if masked:
                    sc = jnp.where(causal, sc, NEG)
                mn = jnp.maximum(m, jnp.max(sc, axis=-1, keepdims=True))
                p = jnp.exp(sc - mn)
                al = jnp.exp(m - mn)
                l = al * l + jnp.sum(p, axis=-1, keepdims=True)
                acc = al * acc + jnp.dot(p.astype(BF16), vj, preferred_element_type=F32)
                out.append((mn, l, acc))
            return tuple(out)

        init = tuple((jnp.full((t, 1), NEG, F32), jnp.zeros((t, 1), F32), jnp.zeros((t, LANES), F32))
                     for _ in range(2))
        carry = lax.fori_loop(0, i, lambda j, c: block(j, c, False), init)
        (m0, l0, a0), (m1, l1, a1) = block(i, carry, True)
        o_ref[...] = jnp.where(_head_lanes(t, 0), a0 / l0, a1 / l1)
        lse_ref[:, 0:1] = m0 + jnp.log(l0)
        lse_ref[:, 1:2] = m1 + jnp.log(l1)

    return pl.pallas_call(
        body, grid=(bsz, hp, nb), in_specs=[blk(2 * LANES), full(2 * LANES), full(LANES)],
        out_specs=[blk(LANES), col],
        out_shape=[jax.ShapeDtypeStruct((n, d), F32), jax.ShapeDtypeStruct((bsz, hp, seq, 2), F32)],
        name=name, compiler_params=_cp("parallel", "parallel", "arbitrary"),
    )(qa, ka, v)


def _flash_bwd(qa, ka, v, do, o, lse_row, bsz, seq, name):
    n, d = v.shape
    hp = d // LANES
    t = _tile(seq, ATTN_TILE)
    nb, blk, full, col, rows = _attn_specs(bsz, seq, t)
    scale = 1.0 / math.sqrt(HEAD_DIM)
    tn_ = (((0,), (0,)), ((), ()))

    def body(k_ref, v_ref, q_ref, do_ref, o_ref, lse_ref, dq_ref, dcc_ref, dk_ref, dv_ref, dck_ref, dqa, dl):
        j = pl.program_id(2)
        causal = lax.broadcasted_iota(jnp.int32, (t, t), 1) >= lax.broadcasted_iota(jnp.int32, (t, t), 0)
        heads = [_head_lanes(t, 0), _head_lanes(t, 1)]

        @pl.when(j == 0)
        def _():
            dqa[...] = jnp.zeros_like(dqa)
            ones = jnp.ones((8, LANES), F32)
            for ib in range(nb):
                rs = slice(ib * t, (ib + 1) * t)
                prod = do_ref[rs, :] * o_ref[rs, :]
                for hh in range(2):
                    dl[ib, hh:hh + 1, :] = lax.dot_general(ones, jnp.where(heads[hh], prod, 0.0), _NT, precision=HI,
                                                           preferred_element_type=F32)[0:1]

        vj = v_ref[...]

        def block(i, carry, masked):
            rs = pl.ds(pl.multiple_of(i * t, t), t)
            doi = do_ref[rs, :]
            dks, dvp = list(carry[:2]), carry[2]
            for hh in range(2):
                hs = slice(hh * LANES, (hh + 1) * LANES)
                kh, qi = k_ref[:, hs], q_ref[rs, hs]
                dom = jnp.where(heads[hh], doi, 0.0).astype(BF16)
                st = lax.dot_general(kh, qi, _NT, preferred_element_type=F32)
                if masked:
                    st = jnp.where(causal, st, NEG)
                pt = jnp.exp(st - lse_ref[i, hh:hh + 1, :])
                dvp = dvp + jnp.dot(pt.astype(BF16), dom, preferred_element_type=F32)
                dpt = lax.dot_general(vj, dom, _NT, preferred_element_type=F32)
                dsb = (pt * (dpt - dl[i, hh:hh + 1, :])).astype(BF16)
                dks[hh] = dks[hh] + jnp.dot(dsb, qi, preferred_element_type=F32)
                dqa[rs, hs] += lax.dot_general(dsb, kh, tn_, preferred_element_type=F32)
            return dks[0], dks[1], dvp

        zero = jnp.zeros((t, LANES), F32)
        carry = block(j, (zero, zero, zero), True)
        dk0, dk1, dvp = lax.fori_loop(j + 1, nb, lambda i, c: block(i, c, False), carry)
        dk_ref[...] = jnp.where(heads[0], dk0, pltpu.roll(dk1, HEAD_DIM, 1))
        dv_ref[...] = dvp
        dck_ref[:, 0:1] = -dk0[:, HEAD_DIM + 3:HEAD_DIM + 4]
        dck_ref[:, 1:2] = -dk1[:, HEAD_DIM + 3:HEAD_DIM + 4]

        @pl.when(j == nb - 1)
        def _():
            first = lax.broadcasted_iota(jnp.int32, (seq, LANES), 1) < HEAD_DIM
            dq_ref[...] = jnp.where(first, dqa[:, :LANES], pltpu.roll(dqa[:, LANES:], HEAD_DIM, 1)) * scale
            for hh in range(2):
                lo = hh * LANES + HEAD_DIM
                dcc_ref[:, hh:hh + 1] = dqa[:, lo:lo + 1]

    whole_col = pl.BlockSpec((None, None, seq, 2), lambda b, h, i: (b, h, 0, 0))
    cshape = jax.ShapeDtypeStruct((bsz, hp, seq, 2), F32)
    nd = jax.ShapeDtypeStruct((n, d), F32)
    return pl.pallas_call(
        body, grid=(bsz, hp, nb),
        in_specs=[blk(2 * LANES), blk(LANES), full(2 * LANES), full(LANES), full(LANES), rows],
        out_specs=[full(LANES), whole_col, blk(LANES), blk(LANES), col],
        out_shape=[nd, cshape, nd, nd, cshape],
        scratch_shapes=[pltpu.VMEM((seq, 2 * LANES), F32), pltpu.VMEM((nb, 2, t), F32)],
        name=name, compiler_params=_cp("parallel", "parallel", "arbitrary"),
    )(ka, v, qa, do, o, lse_row)


def _adamw(w, g, m, v, name):
    r, c = w.shape
    tr = r
    for cand in (512, 256, 128, 64, 32, 16, 8):
        if r % cand == 0 and r > cand and cand * c * 4 <= 4 * 1024 * 1024:
            tr = cand
            break

    def body(w_ref, g_ref, m_ref, v_ref, d_ref, m2_ref, v2_ref):
        gv = g_ref[...]
        m2 = ADAM_B1 * m_ref[...] + (1.0 - ADAM_B1) * gv
        v2 = ADAM_B2 * v_ref[...] + (1.0 - ADAM_B2) * jnp.square(gv)
        m_hat = m2 / (1.0 - ADAM_B1 ** ADAM_STEP)
        v_hat = v2 / (1.0 - ADAM_B2 ** ADAM_STEP)
        d_ref[...] = -ADAM_LR * (m_hat / (jnp.sqrt(v_hat) + ADAM_EPS) + ADAM_WD * w_ref[...])
        m2_ref[...] = m2
        v2_ref[...] = v2

    blk = pl.BlockSpec((tr, c), lambda i: (i, 0))
    shp = jax.ShapeDtypeStruct((r, c), F32)
    return pl.pallas_call(
        body, grid=(r // tr,), in_specs=[blk] * 4, out_specs=[blk] * 3, out_shape=[shp] * 3,
        name=name, compiler_params=_cp("parallel"),
    )(w, g, m, v)


def _adamw_nd(w, g, m, v, name):
    shape = w.shape
    two = (math.prod(shape[:-1]), shape[-1])
    d_, m_, v_ = _adamw(w.reshape(two), g.reshape(two), m.reshape(two), v.reshape(two), name)
    return g.reshape(shape), d_.reshape(shape), m_.reshape(shape), v_.reshape(shape)


def _place():
    x, y, c = lax.axis_index("x"), lax.axis_index("y"), lax.axis_index("c")
    chips = [(1 - x, y), (x, 1 - y), (1 - x, 1 - y)]
    return x, y, c, chips


def _gather_chips(shards):
    nt = len(shards)
    halves = [s.shape[0] // 2 for s in shards]

    def copies(ins, outs, send_sems, recv_sems):
        x, y, c, chips = _place()
        cps = []
        for t in range(nt):
            rows = pl.ds(c * halves[t], halves[t])
            for jj, (cx, cy) in enumerate(chips):
                cps.append(pltpu.make_async_remote_copy(
                    src_ref=ins[t].at[rows, :], dst_ref=outs[t].at[2 * x + y, rows, :], send_sem=send_sems.at[3 * t + jj],
                    recv_sem=recv_sems.at[3 * t + jj], device_id=(cx, cy, c), device_id_type=MESH))
        return cps

    def start(*refs):
        for cp in copies(*refs):
            cp.start()

    def wait(*refs):
        for cp in copies(*refs):
            cp.wait()

    return _Comm(shards, [jax.ShapeDtypeStruct((N_CHIPS,) + s.shape, s.dtype) for s in shards], 3 * nt, start, wait)


def _gather_sibling(shards, bufs, name):
    nt = len(shards)
    halves = [s.shape[0] // 2 for s in shards]

    def body(*refs):
        ins, outs = refs[:nt], refs[2 * nt:3 * nt]
        send_sems, recv_sems = refs[3 * nt:]
        x, y, c, chips = _place()
        sibling = (x, y, 1 - c)

        def copy(t, jj, hf):
            cx, cy = chips[jj]
            region = outs[t].at[2 * cx + cy, pl.ds(hf * halves[t], halves[t]), :]
            return pltpu.make_async_remote_copy(src_ref=region, dst_ref=region, send_sem=send_sems.at[4 * t + jj],
                                                recv_sem=recv_sems.at[4 * t + jj], device_id=sibling,
                                                device_id_type=MESH)

        def own(t):
            return pltpu.make_async_remote_copy(src_ref=ins[t], dst_ref=outs[t].at[2 * x + y],
                                                send_sem=send_sems.at[4 * t + 3], recv_sem=recv_sems.at[4 * t + 3],
                                                device_id=sibling, device_id_type=MESH)

        sends = [copy(t, jj, c) for t in range(nt) for jj in range(3)] + [own(t) for t in range(nt)]
        for cp in sends:
            cp.start()
        for t in range(nt):
            for jj in range(3):
                copy(t, jj, 1 - c).wait_recv()
            own(t).wait_recv()
        for cp in sends:
            cp.wait_send()

    return pl.pallas_call(
        body, in_specs=[ANY] * (2 * nt), out_specs=[ANY] * nt,
        out_shape=[jax.ShapeDtypeStruct(b.shape, b.dtype) for b in bufs],
        scratch_shapes=[pltpu.SemaphoreType.DMA((4 * nt,)), pltpu.SemaphoreType.DMA((4 * nt,))],
        input_output_aliases={nt + t: t for t in range(nt)}, name=name,
        compiler_params=pltpu.CompilerParams(has_side_effects=True),
    )(*shards, *bufs)


N_PARTIALS = 7


def _scatter_partials(grads):
    nt = len(grads)
    halves = [g.shape[1] // 2 for g in grads]

    def copies(ins, outs, send_sems, recv_sems):
        x, y, c, chips = _place()
        cps = []

        def copy(t, kk, slot, core, to):
            rows = pl.ds(core * halves[t], halves[t])
            return pltpu.make_async_remote_copy(
                src_ref=ins[t].at[slot, rows, :], dst_ref=outs[t].at[kk], send_sem=send_sems.at[N_PARTIALS * t + kk],
                recv_sem=recv_sems.at[N_PARTIALS * t + kk], device_id=to, device_id_type=MESH)

        for t in range(nt):
            for jj, (cx, cy) in enumerate(chips):
                cps.append(copy(t, 2 * jj, 2 * cx + cy, c, (cx, cy, c)))
                cps.append(copy(t, 2 * jj + 1, 2 * cx + cy, 1 - c, (cx, cy, 1 - c)))
            cps.append(copy(t, N_PARTIALS - 1, 2 * x + y, 1 - c, (x, y, 1 - c)))
        return cps

    def start(*refs):
        for cp in copies(*refs):
            cp.start()

    def wait(*refs):
        for cp in copies(*refs):
            cp.wait()

    return _Comm(grads, [jax.ShapeDtypeStruct((N_PARTIALS, h, g.shape[2]), g.dtype) for g, h in zip(grads, halves)],
                 N_PARTIALS * nt, start, wait)


def _add_partials(g, got, place_idx, name):
    s, r, w = g.shape
    rh = r // 2
    tr = rh
    for cand in (256, 128, 64, 32, 16):
        if rh % cand == 0 and cand * w * 4 <= 2 * 1024 * 1024:
            tr = cand
            break
    per = rh // tr

    def body(b_ref, g_ref, *rest):
        o_ref = rest[-1]
        acc = g_ref[...].astype(F32)
        for r_ref in rest[:-1]:
            acc = acc + r_ref[...].astype(F32)
        o_ref[...] = acc

    return pl.pallas_call(
        body,
        grid_spec=pltpu.PrefetchScalarGridSpec(
            num_scalar_prefetch=1, grid=(per,),
            in_specs=[pl.BlockSpec((None, tr, w), lambda a, b_ref: (b_ref[0], b_ref[1] * per + a, 0))]
            + [pl.BlockSpec((None, tr, w), lambda a, b_ref, kk=kk: (kk, a, 0)) for kk in range(N_PARTIALS)],
            out_specs=pl.BlockSpec((tr, w), lambda a, b_ref: (b_ref[1] * per + a, 0))),
        out_shape=jax.ShapeDtypeStruct((r, w), F32), name=name, compiler_params=_cp("parallel"),
    )(place_idx, g, *([got] * N_PARTIALS))


def _swap_halves(bufs, name):
    nt = len(bufs)

    def body(*refs):
        outs = refs[nt:2 * nt]
        send_sems, recv_sems = refs[2 * nt:]
        x, y, c, _ = _place()

        def copy(t, hf):
            rh = outs[t].shape[0] // 2
            region = outs[t].at[pl.ds(hf * rh, rh), :]
            return pltpu.make_async_remote_copy(src_ref=region, dst_ref=region, send_sem=send_sems.at[t],
                                                recv_sem=recv_sems.at[t], device_id=(x, y, 1 - c),
                                                device_id_type=MESH)

        sends = [copy(t, c) for t in range(nt)]
        for cp in sends:
            cp.start()
        for t in range(nt):
            copy(t, 1 - c).wait_recv()
        for cp in sends:
            cp.wait_send()

    return pl.pallas_call(
        body, in_specs=[ANY] * nt, out_specs=[ANY] * nt,
        out_shape=[jax.ShapeDtypeStruct(b.shape, b.dtype) for b in bufs],
        scratch_shapes=[pltpu.SemaphoreType.DMA((nt,)), pltpu.SemaphoreType.DMA((nt,))],
        input_output_aliases={t: t for t in range(nt)}, name=name,
        compiler_params=pltpu.CompilerParams(has_side_effects=True),
    )(*bufs)


def _rs_finish(grads, got, place_idx, tag):
    fin = [_add_partials(g, r, place_idx, name=f"rs_add_{tag}_{t}") for t, (g, r) in enumerate(zip(grads, got))]
    return _swap_halves(fin, name=f"rs_swap_{tag}")


def _all_reduce_small(v, name):
    r, w = v.shape

    def body(v_ref, o_ref, buf, send_sems, recv_sems):
        x, y, c, _ = _place()
        me = 4 * x + 2 * y + c
        buf[me] = v_ref[...]
        cps = []
        for kk in range(1, 8):
            peer = (x ^ ((kk >> 2) & 1), y ^ ((kk >> 1) & 1), c ^ (kk & 1))
            cps.append(pltpu.make_async_remote_copy(src_ref=v_ref, dst_ref=buf.at[me], send_sem=send_sems.at[kk - 1],
                                                    recv_sem=recv_sems.at[kk - 1], device_id=peer, device_id_type=MESH))
        for cp in cps:
            cp.start()
        for kk in range(1, 8):
            pltpu.make_async_remote_copy(src_ref=v_ref, dst_ref=buf.at[me ^ kk], send_sem=send_sems.at[kk - 1],
                                         recv_sem=recv_sems.at[kk - 1], device_id=(x, y, c),
                                         device_id_type=MESH).wait_recv()
        for cp in cps:
            cp.wait_send()
        acc = buf[0]
        for dev in range(1, 8):
            acc = acc + buf[dev]
        o_ref[...] = acc

    vm = pl.BlockSpec(memory_space=pltpu.VMEM)
    return pl.pallas_call(
        body, in_specs=[vm], out_specs=vm, out_shape=jax.ShapeDtypeStruct((r, w), F32),
        scratch_shapes=[pltpu.VMEM((8, r, w), F32), pltpu.SemaphoreType.DMA((7,)), pltpu.SemaphoreType.DMA((7,))],
        name=name, compiler_params=pltpu.CompilerParams(has_side_effects=True),
    )(v)


def _to_shards(a, axis=-1):
    axis = axis % a.ndim
    shp = a.shape
    a = a.reshape(shp[:axis] + (N_CHIPS, shp[axis] // N_CHIPS) + shp[axis + 1:])
    return jnp.moveaxis(a, axis, 0).reshape(N_CHIPS, -1)


def _from_shards(s, shard_shape, axis=-1):
    axis = axis % len(shard_shape)
    a = jnp.moveaxis(s.reshape((N_CHIPS,) + tuple(shard_shape)), 0, axis)
    return a.reshape(tuple(shard_shape[:axis]) + (N_CHIPS * shard_shape[axis],) + tuple(shard_shape[axis + 1:]))


def _pack(vecs, rows):
    flat = jnp.concatenate([v.reshape(v.shape[0], -1) if v.ndim > 1 else v.reshape(1, -1) for v in vecs], axis=1)
    lead = flat.shape[0]
    flat = jnp.pad(flat, ((0, 0), (0, rows * LANES - flat.shape[1])))
    return flat.reshape(lead, rows, LANES)


def _pack_rows(sizes, mult):
    total = sum(sizes)
    rows = -(-total // LANES)
    return -(-rows // mult) * mult


def _unpack(flat, shapes):
    out, pos = [], 0
    for shp in shapes:
        sz = math.prod(shp)
        out.append(flat[..., pos:pos + sz].reshape(flat.shape[:-1] + tuple(shp)))
        pos += sz
    return out


def _row_layout(col, t):
    bsz, hp, seq, _ = col.shape
    return col.reshape(bsz, hp, seq // t, t, 2).transpose(0, 1, 2, 4, 3)


def _from_col_layout(col):
    bsz, hp, seq, _ = col.shape
    a = col.transpose(0, 2, 1, 3).reshape(bsz * seq, 2 * hp)
    return jnp.pad(a, ((0, 0), (0, LANES - 2 * hp)))


def kernel(x, norm_mix, norm_ffn, conv_w_in, conv_b_in, conv_dw, conv_dw_b, conv_ln_g, conv_ln_b, conv_w_out, conv_b_out, pool_w, pool_b, pool_scale, fox_w_in, fox_b_f, fox_q_gain, fox_k_gain, fox_w_o, ffn_w_up, ffn_dw, ffn_dw_b, ffn_w_down, loss_target, m_norm_mix, m_norm_ffn, m_conv_w_in, m_conv_b_in, m_conv_dw, m_conv_dw_b, m_conv_ln_g, m_conv_ln_b, m_conv_w_out, m_conv_b_out, m_pool_w, m_pool_b, m_pool_scale, m_fox_w_in, m_fox_b_f, m_fox_q_gain, m_fox_k_gain, m_fox_w_o, m_ffn_w_up, m_ffn_dw, m_ffn_dw_b, m_ffn_w_down, v_norm_mix, v_norm_ffn, v_conv_w_in, v_conv_b_in, v_conv_dw, v_conv_dw_b, v_conv_ln_g, v_conv_ln_b, v_conv_w_out, v_conv_b_out, v_pool_w, v_pool_b, v_pool_scale, v_fox_w_in, v_fox_b_f, v_fox_q_gain, v_fox_k_gain, v_fox_w_o, v_ffn_w_up, v_ffn_dw, v_ffn_dw_b, v_ffn_w_down):
    bsz, seq, d = x.shape
    n = bsz * seq
    depth = norm_mix.shape[0]
    n_conv, n_pool, n_fox = conv_w_in.shape[0], pool_w.shape[0], fox_w_in.shape[0]
    f2 = ffn_dw_b.shape[1]
    f = f2 // 2
    nh = d // HEAD_DIM
    hp = d // LANES
    ng = len(POOL_WINDOWS)
    gd = d // ng
    chip_idx = jnp.stack([2 * lax.axis_index("x") + lax.axis_index("y"), lax.axis_index("c")]).astype(jnp.int32)

    small_shapes = [conv_b_in.shape, conv_dw.shape, conv_dw_b.shape, conv_ln_g.shape, conv_ln_b.shape,
                    conv_b_out.shape, pool_b.shape, ffn_dw.shape]
    small_rows = _pack_rows([math.prod(s) for s in small_shapes], 16)
    small = _pack([v.reshape(1, -1) for v in (conv_b_in, conv_dw, conv_dw_b, conv_ln_g, conv_ln_b, conv_b_out,
                                                pool_b, ffn_dw)], small_rows)[0]
    def layer_shards(i):
        kind, j = i % 3, i // 3
        shards = [ffn_w_up[i].astype(BF16), ffn_w_down[i].astype(BF16)]
        if kind == 0:
            shards += [conv_w_in[j].astype(BF16), conv_w_out[j].astype(BF16)]
        elif kind == 1:
            shards += [pool_w[j].reshape(ng * (gd // N_CHIPS), gd).astype(BF16)]
        else:
            shards += [fox_w_in[j].astype(BF16), fox_w_o[j].astype(BF16)]
        if i == 0:
            shards.append(small)
        return shards

    gathered = [None] * depth
    first_now = layer_shards(0)[2:]
    gathered[0] = [None, None] + list(_gather_sibling(
        first_now, _run_comm(_gather_chips(first_now), name="gather_chips_l0"), name="gather_sibling_l0"))
    small_all = gathered[0][-1].reshape(N_CHIPS, -1)
    sm = _unpack(small_all, small_shapes)
    axes = [-1] * 8
    b_in_f, dw_f, dw_b_f, ln_g_f, ln_b_f, b_out_f, pool_b_f, ffn_dw_f = [
        _from_shards(s_.reshape(N_CHIPS, -1), shp, ax) for s_, shp, ax in zip(sm, small_shapes, axes)]

    xs = x.reshape(n, d)
    tgt = loss_target.reshape(n, d)
    vec = lambda a: a.reshape(1, -1)

    saved = []
    cur = xs
    for i in range(depth):
        kind, j = i % 3, i // 3
        wts = gathered[i]
        sv = {"x_in": cur}
        if kind == 0:
            w_in, w_out = wts[2], wts[3].reshape(d, d)
            wcol = w_in.shape[2]
            h = _rms_fwd(cur, vec(norm_mix[i]), name=f"rms_mix_l{i}")
            pa = _mm(h, w_in, m=n, n=d, k=d, tk=d, tn=wcol, b_stk=wcol, b_s0=0, bias=vec(b_in_f[j, :d]),
                     name=f"conv_in_a_l{i}")
            pg = _mm(h, w_in, m=n, n=d, k=d, tk=d, tn=wcol, b_stk=wcol, b_s0=2, bias=vec(b_in_f[j, d:]),
                     name=f"conv_in_g_l{i}")
            taps = jnp.pad(dw_f[j], ((0, CONV_HALO - CONV_WIDTH), (0, 0)))
            conf_args = (pa, pg, taps, vec(dw_b_f[j]), vec(ln_g_f[j]), vec(ln_b_f[j]), seq)
            if i == 0:
                ffn_shards = layer_shards(0)[:2]
                (u, s_), landed = _conf_fwd(*conf_args, name=f"conf_fwd_l{i}", comm=_gather_chips(ffn_shards))
                gathered[0][:2] = _gather_sibling(ffn_shards, landed, name="gather_sibling_ffn_l0")
            else:
                u, s_ = _conf_fwd(*conf_args, name=f"conf_fwd_l{i}")
            cur = _mm(s_, w_out, m=n, n=d, k=d, tk=d, bias=vec(b_out_f[j]), res=cur, name=f"conv_out_l{i}")
            sv.update(h=h, pa=pa, pg=pg, u=u, s=s_, taps=taps)
        elif kind == 1:
            pw = wts[2].reshape(N_CHIPS, ng, gd // N_CHIPS, gd).transpose(1, 0, 2, 3).reshape(ng, gd, gd)
            cur, p = _pool_fwd(cur, vec(norm_mix[i]), pw, vec(pool_b_f[j]), vec(pool_scale[j]), seq,
                               name=f"pool_fwd_l{i}")
            sv.update(p=p, pw=pw)
        else:
            w_in = wts[2].transpose(1, 0, 2).reshape(d, -1)
            w_qkv = w_in[:, :3 * d]
            w_f = jnp.pad(w_in[:, 3 * d:], ((0, 0), (0, LANES - nh)))
            w_o = wts[3].reshape(d, d)
            bf = jnp.pad(vec(fox_b_f[j]), ((0, 0), (0, LANES - nh)))
            qg, kg = jnp.tile(vec(fox_q_gain[j]), (1, nh)), jnp.tile(vec(fox_k_gain[j]), (1, nh))
            h = _rms_fwd(cur, vec(norm_mix[i]), name=f"rms_mix_l{i}")
            qkv = _mm(h, w_qkv, m=n, n=3 * d, k=d, tk=d, tn=d, name=f"fox_qkv_l{i}")
            fl = _mm(h, w_f, m=n, n=LANES, k=d, tk=d, name=f"fox_fl_l{i}")
            qa, ka, v = _fox_prep_fwd(qkv, fl, bf, qg, kg, seq, name=f"fox_prep_l{i}")
            o, lse = _flash_fwd(qa, ka, v, bsz, seq, name=f"fox_attn_l{i}")
            cur = _mm(o, w_o, m=n, n=d, k=d, tk=d, res=cur, name=f"fox_out_l{i}")
            sv.update(h=h, qkv=qkv, fl=fl, qa=qa, ka=ka, v=v, o=o, lse=lse, w_qkv=w_qkv, w_f=w_f, w_o=w_o, bf=bf,
                      qg=qg, kg=kg)
        w_up, w_down = wts[0], wts[1].reshape(f, d)
        ucol = w_up.shape[2]
        sv["x_mid"] = cur
        h2 = _rms_fwd(cur, vec(norm_ffn[i]), name=f"rms_ffn_l{i}")
        uv = _mm(h2, w_up, m=n, n=f, k=d, tk=d, tn=ucol, b_stk=ucol, b_s0=0, name=f"ffn_up_v_l{i}")
        ug = _mm(h2, w_up, m=n, n=f, k=d, tk=d, tn=ucol, b_stk=ucol, b_s0=2, name=f"ffn_up_g_l{i}")
        fdw, fdb = ffn_dw_f[i], ffn_dw_b[i]
        glu_args = (uv, ug, fdw[:, :f], fdw[:, f:], vec(fdb[:f]), vec(fdb[f:]), w_down, cur, seq)
        if i + 1 < depth:
            nxt_shards = layer_shards(i + 1)
            (a_, cur), landed = _ffn_glu_down(*glu_args, name=f"ffn_glu_down_l{i}", comm=_gather_chips(nxt_shards))
            gathered[i + 1] = _gather_sibling(nxt_shards, landed, name=f"gather_sibling_l{i + 1}")
        else:
            a_, cur = _ffn_glu_down(*glu_args, name=f"ffn_glu_down_l{i}")
        sv.update(h2=h2, uv=uv, ug=ug, a=a_)
        saved.append(sv)

    dy, loss_part = _loss(cur, tgt, name="loss")
    loss = lax.psum(loss_part[0, 0], ("x", "y", "c"))

    g_norm_mix, g_norm_ffn = [None] * depth, [None] * depth
    g_ffn_dw_b = [None] * depth
    g_up, g_down = [None] * depth, [None] * depth
    g_conv_in, g_conv_out = [None] * n_conv, [None] * n_conv
    g_pool_w = g_fox_in = g_fox_o = None
    g_pool_scale = g_bf = g_qg = g_kg = None
    part_small = {"b_in": [None] * n_conv, "dw": [None] * n_conv, "dw_b": [None] * n_conv, "ln_g": [None] * n_conv,
                  "ln_b": [None] * n_conv, "b_out": [None] * n_conv, "pool_b": None, "ffn_dw": [None] * depth}

    pending = []
    done = {}

    def take_pending():
        groups = list(pending)
        pending.clear()
        parts = [p_ for g_ in groups for p_ in g_[0]]
        return groups, (_scatter_partials(parts) if parts else None)

    def finish_groups(groups, r2):
        pos = 0
        for parts, tag, sink in groups:
            sink(_rs_finish(parts, r2[pos:pos + len(parts)], chip_idx, tag))
            pos += len(parts)

    def carried(fn, args, name):
        groups, comm = take_pending()
        if comm is None:
            return fn(*args, name=name)
        outs, r2 = fn(*args, name=name, comm=comm)
        finish_groups(groups, r2)
        return outs

    dcur = dy
    for i in reversed(range(depth)):
        kind, j = i % 3, i // 3
        wts, sv = gathered[i], saved[i]
        w_up, w_down = wts[0], wts[1].reshape(f, d)
        ucol = w_up.shape[2]
        fdw, fdb = ffn_dw_f[i], ffn_dw_b[i]
        dw_down = _mm(sv["a"], dcur, m=f, n=d, k=n, tm=f // 2, tn=d, tk=2048, ta=True, out_dtype=BF16, name=f"ffn_down_dw_l{i}")
        duv, dug, dwv, dwg, dbv, dbg = carried(
            _ffn_glu_bwd, (sv["uv"], sv["ug"], dcur, w_down, fdw[:, :f], fdw[:, f:], vec(fdb[:f]), vec(fdb[f:]), seq),
            name=f"ffn_glu_bwd_l{i}")
        dw_up = _mm(sv["h2"], duv, m=d, n=f, k=n, tm=d, tn=ucol, tk=2048, ta=True, out_dtype=BF16, o_stk=ucol, o_s0=0,
                    o_slots=N_CHIPS, name=f"ffn_up_dw_v_l{i}")
        dw_up = _mm(sv["h2"], dug, m=d, n=f, k=n, tm=d, tn=ucol, tk=2048, ta=True, out_dtype=BF16, o_stk=ucol, o_s0=2,
                    o_slots=N_CHIPS, o_buf=dw_up, name=f"ffn_up_dw_g_l{i}")
        dh2 = _mm(duv, w_up, m=n, n=d, k=f, tn=d, tk=f, tb=True, b_stk=ucol, b_s0=0, name=f"ffn_up_dx_v_l{i}")
        dh2 = _mm(dug, w_up, m=n, n=d, k=f, tn=d, tk=f, tb=True, b_stk=ucol, b_s0=2, res=dh2,
                  name=f"ffn_up_dx_g_l{i}")
        dmid, g_norm_ffn[i] = _rms_bwd(sv["x_mid"], dh2, vec(norm_ffn[i]), dcur, name=f"rms_ffn_bwd_l{i}")
        part_small["ffn_dw"][i] = jnp.concatenate([dwv, dwg], axis=1)
        g_ffn_dw_b[i] = jnp.concatenate([dbv, dbg], axis=1)

        def ffn_sink(red, i=i):
            g_up[i], g_down[i] = red[0], red[1]

        pending.append(([dw_up, dw_down.reshape(N_CHIPS, f // N_CHIPS, d)], f"ffn_l{i}", ffn_sink))

        if kind == 0:
            w_in, w_out = wts[2], wts[3].reshape(d, d)
            wcol = w_in.shape[2]
            ds = _mm(dmid, w_out, m=n, n=d, k=d, tk=d, tb=True, name=f"conv_out_dx_l{i}")
            dw_out = _mm(sv["s"], dmid, m=d, n=d, k=n, tm=d, tn=d, tk=2048, ta=True, out_dtype=BF16, name=f"conv_out_dw_l{i}")
            dpa, dpg, ddw, ddwb, dlng, dlnb, dba, dbg_ = carried(
                _conf_bwd, (sv["u"], ds, sv["pa"], sv["pg"], sv["taps"], vec(ln_g_f[j]), vec(ln_b_f[j]), seq),
                name=f"conf_bwd_l{i}")
            dw_in = _mm(sv["h"], dpa, m=d, n=d, k=n, tm=d, tn=wcol, tk=2048, ta=True, out_dtype=BF16, o_stk=wcol, o_s0=0,
                        o_slots=N_CHIPS, name=f"conv_in_dw_a_l{i}")
            dw_in = _mm(sv["h"], dpg, m=d, n=d, k=n, tm=d, tn=wcol, tk=2048, ta=True, out_dtype=BF16, o_stk=wcol, o_s0=2,
                        o_slots=N_CHIPS, o_buf=dw_in, name=f"conv_in_dw_g_l{i}")
            dh = _mm(dpa, w_in, m=n, n=d, k=d, tn=d, tk=d, tb=True, b_stk=wcol, b_s0=0, name=f"conv_in_dx_a_l{i}")
            dh = _mm(dpg, w_in, m=n, n=d, k=d, tn=d, tk=d, tb=True, b_stk=wcol, b_s0=2, res=dh,
                     name=f"conv_in_dx_g_l{i}")
            dcur, g_norm_mix[i], db_out = _rms_bwd(sv["x_in"], dh, vec(norm_mix[i]), dmid, name=f"rms_mix_bwd_l{i}",
                                                   colsum=True)
            part_small["b_in"][j] = jnp.concatenate([dba, dbg_], axis=1)
            part_small["dw"][j] = ddw[:CONV_WIDTH]
            part_small["dw_b"][j], part_small["ln_g"][j], part_small["ln_b"][j] = ddwb, dlng, dlnb
            part_small["b_out"][j] = db_out
            mix_grads = [dw_in, dw_out.reshape(N_CHIPS, d // N_CHIPS, d)]
        elif kind == 1:
            dcur, g_norm_mix[i], dpw, dpb, dpsc = _pool_bwd(dmid, sv["x_in"], sv["p"], vec(norm_mix[i]), sv["pw"],
                                                            vec(pool_b_f[j]), vec(pool_scale[j]), seq,
                                                            name=f"pool_bwd_l{i}")
            part_small["pool_b"] = dpb
            g_pool_scale = dpsc
            mix_grads = [dpw.reshape(ng, N_CHIPS, gd // N_CHIPS, gd).transpose(1, 0, 2, 3).reshape(N_CHIPS, gd, gd)
                         .astype(BF16)]
        else:
            do = _mm(dmid, sv["w_o"], m=n, n=d, k=d, tk=d, tb=True, name=f"fox_out_dx_l{i}")
            dw_o = _mm(sv["o"], dmid, m=d, n=d, k=n, tm=d, tn=d, tk=2048, ta=True, out_dtype=BF16, name=f"fox_out_dw_l{i}")
            dq, dcc, dk, dv, dck = _flash_bwd(sv["qa"], sv["ka"], sv["v"], do, sv["o"],
                                              _row_layout(sv["lse"], _tile(seq, ATTN_TILE)), bsz, seq,
                                              name=f"fox_attn_bwd_l{i}")
            dqkv, dfl, dqg, dkg, dbf = _fox_prep_bwd(sv["qkv"], dq, dk, dv, _from_col_layout(dcc), _from_col_layout(dck),
                                                     sv["fl"], sv["bf"], sv["qg"], sv["kg"], seq,
                                                     name=f"fox_prep_bwd_l{i}")
            dw_qkv = _mm(sv["h"], dqkv, m=d, n=3 * d, k=n, tm=d, tn=d, tk=2048, ta=True, out_dtype=BF16, name=f"fox_qkv_dw_l{i}")
            dw_f = _mm(sv["h"], dfl, m=d, n=LANES, k=n, tm=d, tk=2048, ta=True, out_dtype=BF16, name=f"fox_fl_dw_l{i}")
            dh = _mm(dqkv, sv["w_qkv"], m=n, n=d, k=3 * d, tn=d, tk=d, tb=True, name=f"fox_qkv_dx_l{i}")
            dh = _mm(dfl, sv["w_f"], m=n, n=d, k=LANES, tn=d, tb=True, res=dh, name=f"fox_fl_dx_l{i}")
            dcur, g_norm_mix[i] = _rms_bwd(sv["x_in"], dh, vec(norm_mix[i]), dmid, name=f"rms_mix_bwd_l{i}")
            g_bf = dbf[:, :nh]
            g_qg = dqg.reshape(nh, HEAD_DIM).sum(axis=0, keepdims=True)
            g_kg = dkg.reshape(nh, HEAD_DIM).sum(axis=0, keepdims=True)
            dw_in_full = jnp.concatenate([dw_qkv, dw_f[:, :nh]], axis=1)
            wshard = dw_in_full.shape[1] // N_CHIPS
            mix_grads = [dw_in_full.reshape(d, N_CHIPS, wshard).transpose(1, 0, 2),
                         dw_o.reshape(N_CHIPS, d // N_CHIPS, d)]
        if i == 0:
            sm_parts = [jnp.concatenate(part_small["b_in"]), jnp.stack(part_small["dw"]),
                        jnp.concatenate(part_small["dw_b"]), jnp.concatenate(part_small["ln_g"]),
                        jnp.concatenate(part_small["ln_b"]), jnp.concatenate(part_small["b_out"]),
                        part_small["pool_b"].reshape(n_pool, ng, gd), jnp.stack(part_small["ffn_dw"])]
            mix_grads.append(_pack([_to_shards(p_) for p_ in sm_parts], small_rows))

        def mix_sink(red, i=i, kind=kind, j=j):
            if kind == 0:
                g_conv_in[j], g_conv_out[j] = red[0], red[1]
            elif kind == 1:
                done["pool_w"] = red[0]
            else:
                done["fox_in"], done["fox_o"] = red[0], red[1]
            if i == 0:
                done["small"] = red[-1].reshape(-1)

        pending.append((mix_grads, f"mix_l{i}", mix_sink))

    groups, comm = take_pending()
    finish_groups(groups, _run_comm(comm, name="rs_chips_tail"))
    g_pool_w, g_fox_in, g_fox_o, g_small_flat = done["pool_w"], done["fox_in"], done["fox_o"], done["small"]

    grad_x = dcur.reshape(bsz, seq, d)

    rep_parts = [jnp.concatenate(g_norm_mix), jnp.concatenate(g_norm_ffn), g_pool_scale, g_bf, g_qg, g_kg,
                 jnp.concatenate(g_ffn_dw_b)]
    rep_shapes = [norm_mix.shape, norm_ffn.shape, pool_scale.shape, fox_b_f.shape, fox_q_gain.shape,
                  fox_k_gain.shape, ffn_dw_b.shape]
    rep_rows = _pack_rows([math.prod(s) for s in rep_shapes], 8)
    rep = _all_reduce_small(_pack([p_.reshape(1, -1) for p_ in rep_parts], rep_rows)[0], name="all_reduce_small")
    g_rep = _unpack(rep.reshape(-1), rep_shapes)
    g_sm = _unpack(g_small_flat, small_shapes)

    grads = {
        "norm_mix": g_rep[0], "norm_ffn": g_rep[1],
        "conv_w_in": jnp.stack(g_conv_in), "conv_b_in": g_sm[0], "conv_dw": g_sm[1], "conv_dw_b": g_sm[2],
        "conv_ln_g": g_sm[3], "conv_ln_b": g_sm[4], "conv_w_out": jnp.stack(g_conv_out), "conv_b_out": g_sm[5],
        "pool_w": g_pool_w.reshape(pool_w.shape), "pool_b": g_sm[6], "pool_scale": g_rep[2],
        "fox_w_in": g_fox_in.reshape(fox_w_in.shape), "fox_b_f": g_rep[3], "fox_q_gain": g_rep[4],
        "fox_k_gain": g_rep[5], "fox_w_o": g_fox_o.reshape(fox_w_o.shape),
        "ffn_w_up": jnp.stack(g_up), "ffn_dw": g_sm[7], "ffn_dw_b": g_rep[6], "ffn_w_down": jnp.stack(g_down),
    }
    weights = dict(norm_mix=norm_mix, norm_ffn=norm_ffn, conv_w_in=conv_w_in, conv_b_in=conv_b_in, conv_dw=conv_dw,
                   conv_dw_b=conv_dw_b, conv_ln_g=conv_ln_g, conv_ln_b=conv_ln_b, conv_w_out=conv_w_out,
                   conv_b_out=conv_b_out, pool_w=pool_w, pool_b=pool_b, pool_scale=pool_scale, fox_w_in=fox_w_in,
                   fox_b_f=fox_b_f, fox_q_gain=fox_q_gain, fox_k_gain=fox_k_gain, fox_w_o=fox_w_o, ffn_w_up=ffn_w_up,
                   ffn_dw=ffn_dw, ffn_dw_b=ffn_dw_b, ffn_w_down=ffn_w_down)
    m_in = dict(norm_mix=m_norm_mix, norm_ffn=m_norm_ffn, conv_w_in=m_conv_w_in, conv_b_in=m_conv_b_in,
                conv_dw=m_conv_dw, conv_dw_b=m_conv_dw_b, conv_ln_g=m_conv_ln_g, conv_ln_b=m_conv_ln_b,
                conv_w_out=m_conv_w_out, conv_b_out=m_conv_b_out, pool_w=m_pool_w, pool_b=m_pool_b,
                pool_scale=m_pool_scale, fox_w_in=m_fox_w_in, fox_b_f=m_fox_b_f, fox_q_gain=m_fox_q_gain,
                fox_k_gain=m_fox_k_gain, fox_w_o=m_fox_w_o, ffn_w_up=m_ffn_w_up, ffn_dw=m_ffn_dw,
                ffn_dw_b=m_ffn_dw_b, ffn_w_down=m_ffn_w_down)
    v_in = dict(norm_mix=v_norm_mix, norm_ffn=v_norm_ffn, conv_w_in=v_conv_w_in, conv_b_in=v_conv_b_in,
                conv_dw=v_conv_dw, conv_dw_b=v_conv_dw_b, conv_ln_g=v_conv_ln_g, conv_ln_b=v_conv_ln_b,
                conv_w_out=v_conv_w_out, conv_b_out=v_conv_b_out, pool_w=v_pool_w, pool_b=v_pool_b,
                pool_scale=v_pool_scale, fox_w_in=v_fox_w_in, fox_b_f=v_fox_b_f, fox_q_gain=v_fox_q_gain,
                fox_k_gain=v_fox_k_gain, fox_w_o=v_fox_w_o, ffn_w_up=v_ffn_w_up, ffn_dw=v_ffn_dw,
                ffn_dw_b=v_ffn_dw_b, ffn_w_down=v_ffn_w_down)
    names = list(weights)
    g_out, d_out, m_out, v_out = [], [], [], []
    for nm in names:
        g_, dl_, m_, v_ = _adamw_nd(weights[nm], grads[nm].reshape(weights[nm].shape), m_in[nm], v_in[nm],
                                    name=f"adamw_{nm}")
        g_out.append(g_)
        d_out.append(dl_)
        m_out.append(m_)
        v_out.append(v_)
    return (loss, grad_x, *g_out, *d_out, *m_out, *v_out)
```

```python
import math

import jax
import jax.numpy as jnp
import numpy as np
from jax import lax
from jax.experimental import pallas as pl
from jax.experimental.pallas import tpu as pltpu

F32 = jnp.float32
BF16 = jnp.bfloat16
HI = lax.Precision.HIGHEST
MESH = pl.DeviceIdType.MESH
ANY = pl.BlockSpec(memory_space=pl.ANY)

EPS = 1e-6
HEAD_DIM = 64
LANES = 128
POOL_WINDOWS = (2, 4, 8, 16)
CONV_WIDTH = 31
CONV_HALO = 32
FFN_HALO = 8
FFN_ROWS, FFN_COLS = 256, 1408
FFN_DOWN_ROWS, FFN_DOWN_COLS = 256, 2816
POOL_HALO = 16
N_CHIPS = 4
NEG = -1e30

ADAM_LR = 0.001
ADAM_B1 = 0.9
ADAM_B2 = 0.999
ADAM_EPS = 1e-08
ADAM_WD = 0.01
ADAM_STEP = 10

V7X_VMEM_LIMIT_BYTES = 56 * 1024 * 1024


def _cp(*sem):
    return pltpu.CompilerParams(dimension_semantics=sem or None, vmem_limit_bytes=V7X_VMEM_LIMIT_BYTES)


def _tile(n, pref):
    t = min(n, pref)
    assert n % t == 0, (n, pref)
    return t


def _sig(v):
    return jax.nn.sigmoid(v)


def _roll(v, shift):
    n = v.shape[0]
    shift = shift % n
    return v if shift == 0 else pltpu.roll(v, shift, 0)


SUBLANES = 8


CONV_ROWS = 64


def _conv_taps(rot_ref, w_ref, out_ref, tm, start_of):
    d = out_ref.shape[1]
    for lc in range(d // LANES):
        ls = slice(lc * LANES, (lc + 1) * LANES)

        for r0 in range(0, tm, CONV_ROWS):
            acc = None
            for sh in range(CONV_WIDTH):
                kk = CONV_WIDTH - 1 - sh
                lo = r0 + start_of(sh)
                term = w_ref[kk:kk + 1, ls] * rot_ref[sh % SUBLANES, lo:lo + CONV_ROWS, ls]
                acc = term if acc is None else acc + term
            out_ref[r0:r0 + CONV_ROWS, ls] = acc


def _tap_grads(rotz_ref, rotd_ref, dw_ref, tm, halo):
    d = dw_ref.shape[1]
    for lc in range(d // LANES):
        ls = slice(lc * LANES, (lc + 1) * LANES)

        acc = [None] * CONV_WIDTH
        for r0 in range(0, tm, CONV_ROWS):
            duc = rotd_ref[0, r0:r0 + CONV_ROWS, ls]
            for sh in range(CONV_WIDTH):
                lo = r0 + halo - (sh // SUBLANES) * SUBLANES
                prod = duc * rotz_ref[sh % SUBLANES, lo:lo + CONV_ROWS, ls]
                part = prod.reshape(CONV_ROWS // SUBLANES, SUBLANES, LANES).sum(axis=0)
                acc[sh] = part if acc[sh] is None else acc[sh] + part
        for sh in range(CONV_WIDTH):
            kk = CONV_WIDTH - 1 - sh
            dw_ref[kk:kk + 1, ls] += jnp.sum(acc[sh], axis=0, keepdims=True)


class _Comm:
    def __init__(self, ins, out_shapes, n_sems, start, wait):
        self.ins, self.out_shapes, self.n_sems, self.start, self.wait = list(ins), list(out_shapes), n_sems, start, wait


def _hosted(body, comm, *, grid, in_specs, out_specs, out_shape, scratch_shapes, sem, name, ins):
    if comm is None:
        return pl.pallas_call(body, grid=grid, in_specs=in_specs, out_specs=out_specs, out_shape=out_shape,
                              scratch_shapes=scratch_shapes, name=name, compiler_params=_cp(*sem))(*ins)
    n_in, n_out, n_scr = len(in_specs), len(out_specs), len(scratch_shapes)
    nci, nco = len(comm.ins), len(comm.out_shapes)

    def wrapped(*refs):
        pos = [0]

        def take(cnt):
            pos[0] += cnt
            return refs[pos[0] - cnt:pos[0]]

        r_in, c_in, r_out, c_out, r_scr = take(n_in), take(nci), take(n_out), take(nco), take(n_scr)
        send_sems, recv_sems = take(2)
        ids = [pl.program_id(ax) for ax in range(len(grid))]
        first, last = ids[0] == 0, ids[0] == grid[0] - 1
        for ax in range(1, len(grid)):
            first = jnp.logical_and(first, ids[ax] == 0)
            last = jnp.logical_and(last, ids[ax] == grid[ax] - 1)

        @pl.when(first)
        def _():
            comm.start(c_in, c_out, send_sems, recv_sems)

        body(*r_in, *r_out, *r_scr)

        @pl.when(last)
        def _():
            comm.wait(c_in, c_out, send_sems, recv_sems)

    outs = pl.pallas_call(
        wrapped, grid=grid, in_specs=list(in_specs) + [ANY] * nci, out_specs=list(out_specs) + [ANY] * nco,
        out_shape=list(out_shape) + comm.out_shapes,
        scratch_shapes=list(scratch_shapes) + [pltpu.SemaphoreType.DMA((comm.n_sems,))] * 2, name=name,
        compiler_params=pltpu.CompilerParams(dimension_semantics=sem, vmem_limit_bytes=V7X_VMEM_LIMIT_BYTES,
                                             has_side_effects=True),
    )(*ins, *comm.ins)
    return list(outs[:n_out]), list(outs[n_out:])


def _run_comm(comm, name):
    def body(*refs):
        nci, nco = len(comm.ins), len(comm.out_shapes)
        c_in, c_out, send_sems, recv_sems = refs[:nci], refs[nci:nci + nco], refs[-2], refs[-1]
        comm.start(c_in, c_out, send_sems, recv_sems)
        comm.wait(c_in, c_out, send_sems, recv_sems)

    return pl.pallas_call(
        body, in_specs=[ANY] * len(comm.ins), out_specs=[ANY] * len(comm.out_shapes), out_shape=comm.out_shapes,
        scratch_shapes=[pltpu.SemaphoreType.DMA((comm.n_sems,))] * 2, name=name,
        compiler_params=pltpu.CompilerParams(has_side_effects=True),
    )(*comm.ins)


def _mm(a, b, *, m, n, k, name, tm=1024, tn=1024, tk=512, ta=False, tb=False, b_stk=None, b_s0=0,
        o_stk=None, o_s0=0, o_slots=None, o_buf=None, bias=None, res=None, out_dtype=F32):
    tm, tn, tk = _tile(m, tm), _tile(n, tn), _tile(k, tk)
    gi, gj, gk = m // tm, n // tn, k // tk
    nsl = 1
    a_spec = pl.BlockSpec((tk, tm), lambda j, i, kk: (kk, i)) if ta else pl.BlockSpec((tm, tk), lambda j, i, kk: (i, kk))
    if b_stk is None:
        b_spec = pl.BlockSpec((tn, tk), lambda j, i, kk: (j, kk)) if tb else pl.BlockSpec((tk, tn), lambda j, i, kk: (kk, j))
    elif tb and tk > b_stk:
        assert tk % b_stk == 0 and b_s0 % (tk // b_stk) == 0 and not ta
        nsl = tk // b_stk
        b_spec = pl.BlockSpec((nsl, tn, b_stk), lambda j, i, kk: (b_s0 // nsl + kk, j, 0))
    elif tb:
        assert b_stk % tk == 0
        per = b_stk // tk
        b_spec = pl.BlockSpec((None, tn, tk), lambda j, i, kk: (b_s0 + kk // per, j, kk % per))
    else:
        assert b_stk % tn == 0
        per = b_stk // tn
        b_spec = pl.BlockSpec((None, tk, tn), lambda j, i, kk: (b_s0 + j // per, kk, j % per))
    ins, in_specs = [a, b], [a_spec, b_spec]
    if bias is not None:
        ins.append(bias)
        in_specs.append(pl.BlockSpec((1, tn), lambda j, i, kk: (0, j)))
    if res is not None:
        ins.append(res)
        in_specs.append(pl.BlockSpec((tm, tn), lambda j, i, kk: (i, j)))
    aliases = {}
    if o_stk is None:
        out_shape = jax.ShapeDtypeStruct((m, n), out_dtype)
        o_spec = pl.BlockSpec((tm, tn), lambda j, i, kk: (i, j))
    else:
        assert o_stk % tn == 0
        pero = o_stk // tn
        out_shape = jax.ShapeDtypeStruct((o_slots, m, o_stk), out_dtype)
        o_spec = pl.BlockSpec((None, tm, tn), lambda j, i, kk: (o_s0 + j // pero, i, j % pero))
        if o_buf is not None:
            aliases = {len(ins): 0}
            ins.append(o_buf)
            in_specs.append(ANY)
    has_bias, has_res, has_buf = bias is not None, res is not None, o_buf is not None
    dn = (((0 if ta else 1,), (1 if tb else 0,)), ((), ()))

    def body(*refs):
        a_ref, b_ref = refs[0], refs[1]
        pos = 2
        bias_ref = refs[pos] if has_bias else None
        pos += has_bias
        res_ref = refs[pos] if has_res else None
        pos += has_res + has_buf
        o_ref = refs[pos]
        if nsl == 1:
            p = lax.dot_general(a_ref[...].astype(BF16), b_ref[...].astype(BF16), dn, preferred_element_type=F32)
        else:
            p = sum(lax.dot_general(a_ref[:, s * b_stk:(s + 1) * b_stk].astype(BF16), b_ref[s].astype(BF16), dn,
                                    preferred_element_type=F32) for s in range(nsl))

        def finish(acc):
            if has_bias:
                acc = acc + bias_ref[...]
            if has_res:
                acc = acc + res_ref[...]
            o_ref[...] = acc.astype(o_ref.dtype)

        if gk == 1:
            finish(p)
        else:
            acc_ref = refs[pos + 1]
            kk = pl.program_id(2)

            @pl.when(kk == 0)
            def _():
                acc_ref[...] = p

            @pl.when(kk > 0)
            def _():
                acc_ref[...] += p

            @pl.when(kk == gk - 1)
            def _():
                finish(acc_ref[...])

    return pl.pallas_call(
        body, grid=(gj, gi, gk), in_specs=in_specs, out_specs=o_spec, out_shape=out_shape,
        scratch_shapes=[pltpu.VMEM((tm, tn), F32)] if gk > 1 else [],
        input_output_aliases=aliases, name=name,
        compiler_params=_cp("parallel", "parallel", "arbitrary"),
    )(*ins)


def _rms_fwd(x, g, name):
    n, d = x.shape
    tm = _tile(n, 512)

    def body(x_ref, g_ref, h_ref):
        xv = x_ref[...]
        r = lax.rsqrt(jnp.mean(xv * xv, axis=-1, keepdims=True) + EPS)
        h_ref[...] = (xv * r * g_ref[...]).astype(h_ref.dtype)

    return pl.pallas_call(
        body, grid=(n // tm,),
        in_specs=[pl.BlockSpec((tm, d), lambda i: (i, 0)), pl.BlockSpec((1, d), lambda i: (0, 0))],
        out_specs=pl.BlockSpec((tm, d), lambda i: (i, 0)),
        out_shape=jax.ShapeDtypeStruct((n, d), BF16), name=name, compiler_params=_cp("arbitrary"),
    )(x, g)


def _rms_bwd(x, dh, g, dres, name, colsum=False):
    n, d = x.shape
    tm = _tile(n, 512)

    def body(x_ref, dh_ref, g_ref, dres_ref, dx_ref, dg_ref, *rest):
        i = pl.program_id(0)
        xv, dhv = x_ref[...], dh_ref[...]
        r = lax.rsqrt(jnp.mean(xv * xv, axis=-1, keepdims=True) + EPS)
        xn = xv * r
        dxn = dhv * g_ref[...]
        dx_ref[...] = dres_ref[...] + r * (dxn - xn * jnp.mean(dxn * xn, axis=-1, keepdims=True))
        dg = jnp.sum(dhv * xn, axis=0, keepdims=True)

        @pl.when(i == 0)
        def _():
            dg_ref[...] = jnp.zeros_like(dg_ref)
            if colsum:
                rest[0][...] = jnp.zeros_like(rest[0])

        dg_ref[...] += dg
        if colsum:
            rest[0][...] += jnp.sum(dres_ref[...], axis=0, keepdims=True)

    row = pl.BlockSpec((tm, d), lambda i: (i, 0))
    vec = pl.BlockSpec((1, d), lambda i: (0, 0))
    out_shape = [jax.ShapeDtypeStruct((n, d), F32), jax.ShapeDtypeStruct((1, d), F32)]
    out_specs = [row, vec]
    if colsum:
        out_shape.append(jax.ShapeDtypeStruct((1, d), F32))
        out_specs.append(vec)
    return pl.pallas_call(
        body, grid=(n // tm,), in_specs=[row, row, vec, row], out_specs=out_specs, out_shape=out_shape,
        name=name, compiler_params=_cp("arbitrary"),
    )(x, dh, g, dres)


def _loss(y, tgt, name):
    n, d = y.shape
    tm = _tile(n, 512)

    def body(y_ref, t_ref, dy_ref, l_ref):
        i = pl.program_id(0)
        e = y_ref[...] - t_ref[...]
        dy_ref[...] = e / d
        part = 0.5 * jnp.sum(jnp.mean(e * e, axis=-1, keepdims=True), axis=0, keepdims=True)

        @pl.when(i == 0)
        def _():
            l_ref[...] = jnp.zeros_like(l_ref)

        l_ref[...] += part

    row = pl.BlockSpec((tm, d), lambda i: (i, 0))
    return pl.pallas_call(
        body, grid=(n // tm,), in_specs=[row, row],
        out_specs=[row, pl.BlockSpec((1, 1), lambda i: (0, 0))],
        out_shape=[jax.ShapeDtypeStruct((n, d), F32), jax.ShapeDtypeStruct((1, 1), F32)],
        name=name, compiler_params=_cp("arbitrary"),
    )(y, tgt)


def _ffn_specs(n, f, tm, tc, seq):
    hb = FFN_HALO
    cur = pl.BlockSpec((tm, tc), lambda j, i: (i, j))
    prev = pl.BlockSpec((hb, tc), lambda j, i: (jnp.maximum(i * (tm // hb) - 1, 0), j))
    nxt = pl.BlockSpec((hb, tc), lambda j, i: (jnp.minimum((i + 1) * (tm // hb), n // hb - 1), j))
    taps = pl.BlockSpec((3, tc), lambda j, i: (0, j))
    vec = pl.BlockSpec((1, tc), lambda j, i: (0, j))
    return cur, prev, nxt, taps, vec


def _ffn_glu_down(uv, ug, wv, wg, bv, bg, w_down, res, seq, name, comm=None):
    n, f = uv.shape
    d = w_down.shape[1]
    tm, tc = _tile(seq, FFN_DOWN_ROWS), _tile(f, FFN_DOWN_COLS)
    tps = seq // tm
    nj = f // tc
    hb = FFN_HALO
    cur = pl.BlockSpec((tm, tc), lambda i, j: (i, j))
    prev = pl.BlockSpec((hb, tc), lambda i, j: (jnp.maximum(i * (tm // hb) - 1, 0), j))
    taps = pl.BlockSpec((3, tc), lambda i, j: (0, j))
    vec = pl.BlockSpec((1, tc), lambda i, j: (0, j))
    wblk = pl.BlockSpec((tc, d), lambda i, j: (j, 0))
    row = pl.BlockSpec((tm, d), lambda i, j: (i, 0))

    def body(uvp, uvc, ugp, ugc, wv_ref, wg_ref, bv_ref, bg_ref, wd_ref, res_ref, a_ref, x_ref, acc_ref):
        first = (pl.program_id(0) % tps) == 0
        j = pl.program_id(1)

        def conv(p_ref, c_ref, w_ref, b_ref):
            xs = jnp.concatenate([jnp.where(first, 0.0, p_ref[...]), c_ref[...]], axis=0)
            w = w_ref[...]
            y = w[2:3] * xs + w[1:2] * _roll(xs, 1) + w[0:1] * _roll(xs, 2)
            return y[FFN_HALO:] + b_ref[...]

        val = conv(uvp, uvc, wv_ref, bv_ref)
        gate = conv(ugp, ugc, wg_ref, bg_ref)
        a = (gate * _sig(gate) * val).astype(BF16)
        a_ref[...] = a
        p = jnp.dot(a, wd_ref[...], preferred_element_type=F32)

        @pl.when(j == 0)
        def _():
            acc_ref[...] = res_ref[...] + p

        @pl.when(j > 0)
        def _():
            acc_ref[...] += p

        @pl.when(j == nj - 1)
        def _():
            x_ref[...] = acc_ref[...]

    return _hosted(
        body, comm, grid=(n // tm, nj), in_specs=[prev, cur, prev, cur, taps, taps, vec, vec, wblk, row],
        out_specs=[cur, row], out_shape=[jax.ShapeDtypeStruct((n, f), BF16), jax.ShapeDtypeStruct((n, d), F32)],
        scratch_shapes=[pltpu.VMEM((tm, d), F32)], sem=("parallel", "arbitrary"), name=name,
        ins=(uv, uv, ug, ug, wv, wg, bv, bg, w_down, res))


def _ffn_glu_bwd(uv, ug, dx, w_down, wv, wg, bv, bg, seq, name, comm=None):
    n, f = uv.shape
    d = dx.shape[1]
    tm, tc = _tile(seq, FFN_ROWS), _tile(f, FFN_COLS)
    tps = seq // tm
    hb = FFN_HALO
    ext = tm + hb
    cur, prev, nxt, taps, vec = _ffn_specs(n, f, tm, tc, seq)
    dx_cur = pl.BlockSpec((tm, d), lambda j, i: (i, 0))
    dx_nxt = pl.BlockSpec((hb, d), lambda j, i: (jnp.minimum((i + 1) * (tm // hb), n // hb - 1), 0))
    wblk = pl.BlockSpec((tc, d), lambda j, i: (j, 0))

    def body(uvp, uvc, uvn, ugp, ugc, ugn, dx_c, dx_n, wd_ref, wv_ref, wg_ref, bv_ref, bg_ref,
             duv_ref, dug_ref, dwv_ref, dwg_ref, dbv_ref, dbg_ref):
        i = pl.program_id(1)
        first = (i % tps) == 0
        last = (i % tps) == tps - 1
        dx_e = jnp.concatenate([dx_c[...], jnp.where(last, 0.0, dx_n[...])], axis=0).astype(BF16)
        da_e = lax.dot_general(dx_e, wd_ref[...], _NT, preferred_element_type=F32)

        def taps3(p_ref, c_ref, n_ref):
            xs = jnp.concatenate([jnp.where(first, 0.0, p_ref[...]), c_ref[...], n_ref[...]], axis=0)
            return xs, _roll(xs, 1), _roll(xs, 2)

        xv, xg = taps3(uvp, uvc, uvn), taps3(ugp, ugc, ugn)
        wv_, wg_ = wv_ref[...], wg_ref[...]

        def conv(xs, w, b_ref):
            return (w[2:3] * xs[0] + w[1:2] * xs[1] + w[0:1] * xs[2])[hb:] + b_ref[...]

        val, gate = conv(xv, wv_, bv_ref), conv(xg, wg_, bg_ref)
        sg = _sig(gate)
        dval = da_e * (gate * sg)
        dgate = da_e * val * (sg * (1.0 + gate * (1.0 - sg)))

        def conv_t(dv, w):
            return (w[2:3] * dv + w[1:2] * _roll(dv, ext - 1) + w[0:1] * _roll(dv, ext - 2))[:tm]

        duv_ref[...] = conv_t(dval, wv_).astype(duv_ref.dtype)
        dug_ref[...] = conv_t(dgate, wg_).astype(dug_ref.dtype)

        def tap_grads(d_own, xs):
            return jnp.concatenate(
                [jnp.sum(d_own * xs[2 - kk][hb:hb + tm], axis=0, keepdims=True) for kk in range(3)], axis=0)

        dv_own, dg_own = dval[:tm], dgate[:tm]

        @pl.when(i == 0)
        def _():
            for r in (dwv_ref, dwg_ref, dbv_ref, dbg_ref):
                r[...] = jnp.zeros_like(r)

        dwv_ref[...] += tap_grads(dv_own, xv)
        dwg_ref[...] += tap_grads(dg_own, xg)
        dbv_ref[...] += jnp.sum(dv_own, axis=0, keepdims=True)
        dbg_ref[...] += jnp.sum(dg_own, axis=0, keepdims=True)

    return _hosted(
        body, comm, grid=(f // tc, n // tm),
        in_specs=[prev, cur, nxt, prev, cur, nxt, dx_cur, dx_nxt, wblk, taps, taps, vec, vec],
        out_specs=[cur, cur, taps, taps, vec, vec],
        out_shape=[jax.ShapeDtypeStruct((n, f), BF16), jax.ShapeDtypeStruct((n, f), BF16),
                   jax.ShapeDtypeStruct((3, f), F32), jax.ShapeDtypeStruct((3, f), F32),
                   jax.ShapeDtypeStruct((1, f), F32), jax.ShapeDtypeStruct((1, f), F32)],
        scratch_shapes=[], sem=("parallel", "arbitrary"), name=name,
        ins=(uv, uv, uv, ug, ug, ug, dx, dx, w_down, wv, wg, bv, bg))


def _conf_specs(n, d, tm):
    hb = CONV_HALO
    cur = pl.BlockSpec((tm, d), lambda i: (i, 0))
    prev = pl.BlockSpec((hb, d), lambda i: (jnp.maximum(i * (tm // hb) - 1, 0), 0))
    nxt = pl.BlockSpec((hb, d), lambda i: (jnp.minimum((i + 1) * (tm // hb), n // hb - 1), 0))
    taps = pl.BlockSpec((CONV_HALO, d), lambda i: (0, 0))
    vec = pl.BlockSpec((1, d), lambda i: (0, 0))
    return cur, prev, nxt, taps, vec


def _conf_fwd(pa, pg, w, wb, lng, lnb, seq, name, comm=None):
    n, d = pa.shape
    tm = _tile(seq, 256)
    tps = seq // tm
    hb = CONV_HALO
    cur, prev, _, taps, vec = _conf_specs(n, d, tm)

    def body(pap, pac, pgp, pgc, w_ref, wb_ref, lng_ref, lnb_ref, u_ref, s_ref, rot_ref):
        first = (pl.program_id(0) % tps) == 0
        a = jnp.concatenate([jnp.where(first, 0.0, pap[...]), pac[...]], axis=0)
        g = jnp.concatenate([jnp.where(first, 0.0, pgp[...]), pgc[...]], axis=0)
        z = a * _sig(g)
        for b in range(SUBLANES):
            rot_ref[b] = _roll(z, b)
        _conv_taps(rot_ref, w_ref, u_ref, tm, lambda sh: hb - (sh // SUBLANES) * SUBLANES)
        u = u_ref[...] + wb_ref[...]
        mu = jnp.mean(u, axis=-1, keepdims=True)
        uc = u - mu
        var = jnp.mean(uc * uc, axis=-1, keepdims=True)
        ul = uc * lax.rsqrt(var + EPS) * lng_ref[...] + lnb_ref[...]
        u_ref[...] = u
        s_ref[...] = (ul * _sig(ul)).astype(s_ref.dtype)

    return _hosted(
        body, comm, grid=(n // tm,), in_specs=[prev, cur, prev, cur, taps, vec, vec, vec],
        out_specs=[cur, cur],
        out_shape=[jax.ShapeDtypeStruct((n, d), F32), jax.ShapeDtypeStruct((n, d), BF16)],
        scratch_shapes=[pltpu.VMEM((SUBLANES, tm + hb, d), F32)], sem=("arbitrary",), name=name,
        ins=(pa, pa, pg, pg, w, wb, lng, lnb))


def _conf_bwd(u, ds, pa, pg, w, lng, lnb, seq, name, comm=None):
    n, d = u.shape
    tm = _tile(seq, 256)
    tps = seq // tm
    hb = CONV_HALO
    ext = tm + hb
    cur, prev, nxt, taps, vec = _conf_specs(n, d, tm)

    def body(uc_ref, un_ref, dsc_ref, dsn_ref, pap, pac, pgp, pgc, w_ref, lng_ref, lnb_ref,
             dpa_ref, dpg_ref, dw_ref, dwb_ref, dlng_ref, dlnb_ref, dba_ref, dbg_ref, rotz_ref, rotd_ref, dz_ref):
        i = pl.program_id(0)
        first = (i % tps) == 0
        last = (i % tps) == tps - 1

        @pl.when(i == 0)
        def _():
            for r in (dw_ref, dwb_ref, dlng_ref, dlnb_ref, dba_ref, dbg_ref):
                r[...] = jnp.zeros_like(r)

        ue = jnp.concatenate([uc_ref[...], un_ref[...]], axis=0)
        dse = jnp.concatenate([dsc_ref[...], jnp.where(last, 0.0, dsn_ref[...])], axis=0)
        mu = jnp.mean(ue, axis=-1, keepdims=True)
        cen = ue - mu
        r = lax.rsqrt(jnp.mean(cen * cen, axis=-1, keepdims=True) + EPS)
        xn = cen * r
        ul = xn * lng_ref[...] + lnb_ref[...]
        sg = _sig(ul)
        dul = dse * (sg * (1.0 + ul * (1.0 - sg)))
        dun = dul * lng_ref[...]
        du = r * (dun - jnp.mean(dun, axis=-1, keepdims=True) - xn * jnp.mean(dun * xn, axis=-1, keepdims=True))
        dlng_ref[...] += jnp.sum((dul * xn)[:tm], axis=0, keepdims=True)
        dlnb_ref[...] += jnp.sum(dul[:tm], axis=0, keepdims=True)
        dwb_ref[...] += jnp.sum(du[:tm], axis=0, keepdims=True)
        for b in range(SUBLANES):
            rotd_ref[b] = _roll(du, ext - b)
        _conv_taps(rotd_ref, w_ref, dz_ref, tm, lambda sh: (sh // SUBLANES) * SUBLANES)
        dz = dz_ref[...]

        a = jnp.concatenate([jnp.where(first, 0.0, pap[...]), pac[...]], axis=0)
        g = jnp.concatenate([jnp.where(first, 0.0, pgp[...]), pgc[...]], axis=0)
        sgg = _sig(g)
        z = a * sgg
        for b in range(SUBLANES):
            rotz_ref[b] = _roll(z, b)
        _tap_grads(rotz_ref, rotd_ref, dw_ref, tm, hb)

        a_c, sg_c = a[hb:], sgg[hb:]
        da = dz * sg_c
        dg = dz * a_c * sg_c * (1.0 - sg_c)
        dpa_ref[...] = da.astype(dpa_ref.dtype)
        dpg_ref[...] = dg.astype(dpg_ref.dtype)
        dba_ref[...] += jnp.sum(da, axis=0, keepdims=True)
        dbg_ref[...] += jnp.sum(dg, axis=0, keepdims=True)

    vshape = jax.ShapeDtypeStruct((1, d), F32)
    return _hosted(
        body, comm, grid=(n // tm,),
        in_specs=[cur, nxt, cur, nxt, prev, cur, prev, cur, taps, vec, vec],
        out_specs=[cur, cur, taps, vec, vec, vec, vec, vec],
        out_shape=[jax.ShapeDtypeStruct((n, d), BF16), jax.ShapeDtypeStruct((n, d), BF16),
                   jax.ShapeDtypeStruct((CONV_HALO, d), F32), vshape, vshape, vshape, vshape, vshape],
        scratch_shapes=[pltpu.VMEM((SUBLANES, ext, d), F32), pltpu.VMEM((SUBLANES, ext, d), F32),
                        pltpu.VMEM((tm, d), F32)],
        sem=("arbitrary",), name=name, ins=(u, u, ds, ds, pa, pa, pg, pg, w, lng, lnb))


def _pool_specs(n, d, tm, gd):
    hb = POOL_HALO
    cur = pl.BlockSpec((tm, d), lambda i: (i, 0))
    prev = pl.BlockSpec((hb, d), lambda i: (jnp.maximum(i * (tm // hb) - 1, 0), 0))
    nxt = pl.BlockSpec((hb, d), lambda i: (jnp.minimum((i + 1) * (tm // hb), n // hb - 1), 0))
    wsp = pl.BlockSpec((len(POOL_WINDOWS), gd, gd), lambda i: (0, 0, 0))
    vec = pl.BlockSpec((1, d), lambda i: (0, 0))
    return cur, prev, nxt, wsp, vec


def _pool_fwd(x, g, w, b, sc, seq, name):
    n, d = x.shape
    gd = d // len(POOL_WINDOWS)
    tm = _tile(seq, 256)
    tps = seq // tm
    hb = POOL_HALO
    cur, prev, _, wsp, vec = _pool_specs(n, d, tm, gd)

    def body(xp, xc, g_ref, w_ref, b_ref, sc_ref, x1_ref, p_ref):
        i = pl.program_id(0)
        first = (i % tps) == 0
        xe = jnp.concatenate([jnp.where(first, 0.0, xp[...]), xc[...]], axis=0)
        r = lax.rsqrt(jnp.mean(xe * xe, axis=-1, keepdims=True) + EPS)
        h = xe * r * g_ref[...]
        t = ((i % tps) * tm + lax.broadcasted_iota(jnp.int32, (tm, 1), 0) + 1).astype(F32)
        ys = []
        for gi, win in enumerate(POOL_WINDOWS):
            hg = h[:, gi * gd:(gi + 1) * gd]
            s, sh = hg, 1
            while sh < win:
                s = s + _roll(s, sh)
                sh *= 2
            p = (s[hb:] / jnp.minimum(t, float(win)) - hg[hb:]).astype(BF16)
            p_ref[:, gi * gd:(gi + 1) * gd] = p
            ys.append(jnp.dot(p, w_ref[gi], preferred_element_type=F32))
        y = jnp.concatenate(ys, axis=1) + b_ref[...]
        x1_ref[...] = xc[...] + y * sc_ref[...]

    return pl.pallas_call(
        body, grid=(n // tm,), in_specs=[prev, cur, vec, wsp, vec, vec], out_specs=[cur, cur],
        out_shape=[jax.ShapeDtypeStruct((n, d), F32), jax.ShapeDtypeStruct((n, d), BF16)],
        name=name, compiler_params=_cp("arbitrary"),
    )(x, x, g, w, b, sc)


def _pool_bwd(dx1, x, p, g, w, b, sc, seq, name):
    n, d = x.shape
    ng = len(POOL_WINDOWS)
    gd = d // ng
    tm = _tile(seq, 256)
    tps = seq // tm
    hb = POOL_HALO
    ext = tm + hb
    cur, _, nxt, wsp, vec = _pool_specs(n, d, tm, gd)

    def body(dc_ref, dn_ref, x_ref, p_ref, g_ref, w_ref, b_ref, sc_ref, dx_ref, dg_ref, dw_ref, db_ref, dsc_ref):
        i = pl.program_id(0)
        last = (i % tps) == tps - 1

        @pl.when(i == 0)
        def _():
            for r_ in (dg_ref, dw_ref, db_ref, dsc_ref):
                r_[...] = jnp.zeros_like(r_)

        dxc = dc_ref[...]
        dxe = jnp.concatenate([dxc, jnp.where(last, 0.0, dn_ref[...])], axis=0)
        dyg = dxe * sc_ref[...]
        t = ((i % tps) * tm + lax.broadcasted_iota(jnp.int32, (ext, 1), 0) + 1).astype(F32)
        dhs = []
        for gi, win in enumerate(POOL_WINDOWS):
            sl = slice(gi * gd, (gi + 1) * gd)
            dyb = dyg[:, sl].astype(BF16)
            wg = w_ref[gi]
            dp = lax.dot_general(dyb, wg, (((1,), (1,)), ((), ())), preferred_element_type=F32)
            s, sh = dp / jnp.minimum(t, float(win)), 1
            while sh < win:
                s = s + _roll(s, ext - sh)
                sh *= 2
            dhs.append((s - dp)[:tm])
            pg = p_ref[:, sl]
            dw_ref[gi] += lax.dot_general(pg, dyb[:tm], (((0,), (0,)), ((), ())), preferred_element_type=F32)
            ypre = jnp.dot(pg, wg, preferred_element_type=F32) + b_ref[:, sl]
            dsc_ref[:, sl] += jnp.sum(dxc[:, sl] * ypre, axis=0, keepdims=True)
            db_ref[:, sl] += jnp.sum(dyg[:tm, sl], axis=0, keepdims=True)
        dh = jnp.concatenate(dhs, axis=1)
        xv = x_ref[...]
        r = lax.rsqrt(jnp.mean(xv * xv, axis=-1, keepdims=True) + EPS)
        xn = xv * r
        dxn = dh * g_ref[...]
        dx_ref[...] = dxc + r * (dxn - xn * jnp.mean(dxn * xn, axis=-1, keepdims=True))
        dg_ref[...] += jnp.sum(dh * xn, axis=0, keepdims=True)

    vshape = jax.ShapeDtypeStruct((1, d), F32)
    return pl.pallas_call(
        body, grid=(n // tm,), in_specs=[cur, nxt, cur, cur, vec, wsp, vec, vec],
        out_specs=[cur, vec, wsp, vec, vec],
        out_shape=[jax.ShapeDtypeStruct((n, d), F32), vshape, jax.ShapeDtypeStruct((ng, gd, gd), F32), vshape, vshape],
        name=name, compiler_params=_cp("arbitrary"),
    )(dx1, dx1, x, p, g, w, b, sc)


def _head_maps(d):
    hd = lax.broadcasted_iota(jnp.int32, (d, LANES), 0) // HEAD_DIM
    col = lax.broadcasted_iota(jnp.int32, (d, LANES), 1)
    gm = (hd == col).astype(BF16)
    hd_t = lax.broadcasted_iota(jnp.int32, (LANES, d), 1) // HEAD_DIM
    row = lax.broadcasted_iota(jnp.int32, (LANES, d), 0)
    gt = (hd_t == row).astype(BF16)
    return gm, gt


def _dot_split(v, onehot):
    hi = v.astype(BF16)
    lo = (v - hi.astype(F32)).astype(BF16)
    return jnp.dot(hi, onehot, preferred_element_type=F32) + jnp.dot(lo, onehot, preferred_element_type=F32)


def _bias_placement(nh):
    pq = np.zeros((3 * LANES, nh * HEAD_DIM), np.float32)
    pk = np.zeros((3 * LANES, nh * HEAD_DIM), np.float32)
    oq = np.zeros((1, nh * HEAD_DIM), np.float32)
    ok = np.zeros((1, nh * HEAD_DIM), np.float32)
    for h in range(nh):
        for piece in range(3):
            pq[piece * LANES + h, h * HEAD_DIM + piece] = 1.0
            pk[piece * LANES + h, h * HEAD_DIM + 3 + piece] = -1.0
            oq[0, h * HEAD_DIM + 3 + piece] = 1.0
            ok[0, h * HEAD_DIM + piece] = 1.0
    return jnp.asarray(pq, BF16), jnp.asarray(pk, BF16), jnp.asarray(oq), jnp.asarray(ok)


def _fox_prep_fwd(qkv, fl, bf, qg, kg, seq, name):
    n, d3 = qkv.shape
    d = d3 // 3
    nh = d // HEAD_DIM
    tm = _tile(seq, 256)
    tps = seq // tm
    scale = 1.0 / math.sqrt(HEAD_DIM)
    pq, pk, oq, ok = _bias_placement(nh)

    def body(qkv_ref, fl_ref, bf_ref, qg_ref, kg_ref, pq_ref, pk_ref, oq_ref, ok_ref, q_ref, k_ref, v_ref, carry):
        first = (pl.program_id(0) % tps) == 0
        gm, gt = _head_maps(d)

        def head_norm(xr, gain):
            r = lax.rsqrt(_dot_split(xr * xr, gm) / HEAD_DIM + EPS)
            return xr * _dot_split(r, gt) * gain

        qs = (head_norm(qkv_ref[:, :d], qg_ref[...]).astype(BF16).astype(F32) * scale).astype(BF16)
        kn = head_norm(qkv_ref[:, d:2 * d], kg_ref[...]).astype(BF16)
        v_ref[...] = qkv_ref[:, 2 * d:].astype(BF16)
        z = fl_ref[...] + bf_ref[...]
        logf = jnp.minimum(z, 0.0) - jnp.log1p(jnp.exp(-jnp.abs(z)))
        tri = (lax.broadcasted_iota(jnp.int32, (tm, tm), 0) >= lax.broadcasted_iota(jnp.int32, (tm, tm), 1)).astype(F32)

        @pl.when(first)
        def _():
            carry[...] = jnp.zeros_like(carry)

        c = jnp.dot(tri, logf, precision=HI, preferred_element_type=F32) + carry[...]
        carry[...] = c[tm - 1:tm, :]
        c1 = c.astype(BF16)
        r1 = c - c1.astype(F32)
        c2 = r1.astype(BF16)
        c3 = (r1 - c2.astype(F32)).astype(BF16)
        pieces = jnp.concatenate([c1, c2, c3], axis=1)
        eq = (jnp.dot(pieces, pq_ref[...], preferred_element_type=F32) + oq_ref[...]).astype(BF16)
        ek = (jnp.dot(pieces, pk_ref[...], preferred_element_type=F32) + ok_ref[...]).astype(BF16)
        for h in range(nh):
            lo, hi = h * HEAD_DIM, (h + 1) * HEAD_DIM
            q_ref[:, 2 * lo:2 * lo + HEAD_DIM] = qs[:, lo:hi]
            q_ref[:, 2 * lo + HEAD_DIM:2 * hi] = eq[:, lo:hi]
            k_ref[:, 2 * lo:2 * lo + HEAD_DIM] = kn[:, lo:hi]
            k_ref[:, 2 * lo + HEAD_DIM:2 * hi] = ek[:, lo:hi]

    row = lambda w: pl.BlockSpec((tm, w), lambda i: (i, 0))
    vec = lambda w: pl.BlockSpec((1, w), lambda i: (0, 0))
    full = lambda a: pl.BlockSpec(a.shape, lambda i: (0, 0))
    return pl.pallas_call(
        body, grid=(n // tm,),
        in_specs=[row(d3), row(LANES), vec(LANES), vec(d), vec(d), full(pq), full(pk), full(oq), full(ok)],
        out_specs=[row(2 * d), row(2 * d), row(d)],
        out_shape=[jax.ShapeDtypeStruct((n, 2 * d), BF16)] * 2 + [jax.ShapeDtypeStruct((n, d), BF16)],
        scratch_shapes=[pltpu.VMEM((1, LANES), F32)], name=name, compiler_params=_cp("arbitrary"),
    )(qkv, fl, bf, qg, kg, pq, pk, oq, ok)


def _fox_prep_bwd(qkv, dq, dk, dv, dc1, dc2, fl, bf, qg, kg, seq, name, comm=None):
    n, d3 = qkv.shape
    d = d3 // 3
    tm = _tile(seq, 256)
    tps = seq // tm
    nt = n // tm

    def body(qkv_ref, dq_ref, dk_ref, dv_ref, dc1_ref, dc2_ref, fl_ref, bf_ref, qg_ref, kg_ref,
             dqkv_ref, dfl_ref, dqg_ref, dkg_ref, dbf_ref, carry):
        i = pl.program_id(0)
        tile = nt - 1 - i
        last = (tile % tps) == tps - 1
        gm, gt = _head_maps(d)

        @pl.when(i == 0)
        def _():
            for r_ in (dqg_ref, dkg_ref, dbf_ref):
                r_[...] = jnp.zeros_like(r_)

        @pl.when(last)
        def _():
            carry[...] = jnp.zeros_like(carry)

        def head_norm_bwd(xr, dy, gain, dgain_ref):
            rf = _dot_split(lax.rsqrt(_dot_split(xr * xr, gm) / HEAD_DIM + EPS), gt)
            xn = xr * rf
            dgain_ref[...] += jnp.sum(dy * xn, axis=0, keepdims=True)
            dyg = dy * gain
            mean = _dot_split(dyg * xn, gm) / HEAD_DIM
            return rf * (dyg - xn * _dot_split(mean, gt))

        dqkv_ref[:, :d] = head_norm_bwd(qkv_ref[:, :d], dq_ref[...], qg_ref[...], dqg_ref).astype(BF16)
        dqkv_ref[:, d:2 * d] = head_norm_bwd(qkv_ref[:, d:2 * d], dk_ref[...], kg_ref[...], dkg_ref).astype(BF16)
        dqkv_ref[:, 2 * d:] = dv_ref[...].astype(BF16)

        dc = dc1_ref[...] + dc2_ref[...]
        tri = (lax.broadcasted_iota(jnp.int32, (tm, tm), 0) <= lax.broadcasted_iota(jnp.int32, (tm, tm), 1)).astype(F32)
        dlog = jnp.dot(tri, dc, precision=HI, preferred_element_type=F32) + carry[...]
        carry[...] = dlog[0:1, :]
        dfl = dlog * (1.0 - _sig(fl_ref[...] + bf_ref[...]))
        dfl_ref[...] = dfl.astype(BF16)
        dbf_ref[...] += jnp.sum(dfl, axis=0, keepdims=True)

    row = lambda w: pl.BlockSpec((tm, w), lambda i: (nt - 1 - i, 0))
    vec = lambda w: pl.BlockSpec((1, w), lambda i: (0, 0))
    return _hosted(
        body, comm, grid=(nt,),
        in_specs=[row(d3), row(d), row(d), row(d), row(LANES), row(LANES), row(LANES), vec(LANES), vec(d), vec(d)],
        out_specs=[row(d3), row(LANES), vec(d), vec(d), vec(LANES)],
        out_shape=[jax.ShapeDtypeStruct((n, d3), BF16), jax.ShapeDtypeStruct((n, LANES), BF16),
                   jax.ShapeDtypeStruct((1, d), F32), jax.ShapeDtypeStruct((1, d), F32),
                   jax.ShapeDtypeStruct((1, LANES), F32)],
        scratch_shapes=[pltpu.VMEM((1, LANES), F32)], sem=("arbitrary",), name=name,
        ins=(qkv, dq, dk, dv, dc1, dc2, fl, bf, qg, kg))


def _attn_specs(bsz, seq, t):
    nb = seq // t
    blk = lambda w: pl.BlockSpec((t, w), lambda b, h, i: (b * nb + i, h))
    full = lambda w: pl.BlockSpec((seq, w), lambda b, h, i: (b, h))
    col = pl.BlockSpec((None, None, t, 2), lambda b, h, i: (b, h, i, 0))
    rows = pl.BlockSpec((None, None, nb, 2, t), lambda b, h, i: (b, h, 0, 0, 0))
    return nb, blk, full, col, rows


_NT = (((1,), (1,)), ((), ()))
ATTN_TILE = 512


def _head_lanes(t, hh):
    lane = lax.broadcasted_iota(jnp.int32, (t, LANES), 1)
    return (lane < HEAD_DIM) if hh == 0 else (lane >= HEAD_DIM)


def _flash_fwd(qa, ka, v, bsz, seq, name):
    n, d = v.shape
    hp = d // LANES
    t = _tile(seq, ATTN_TILE)
    nb, blk, full, col, _ = _attn_specs(bsz, seq, t)

    def body(q_ref, k_ref, v_ref, o_ref, lse_ref):
        i = pl.program_id(2)
        causal = lax.broadcasted_iota(jnp.int32, (t, t), 0) >= lax.broadcasted_iota(jnp.int32, (t, t), 1)

        def block(j, carry, masked):
            rs = pl.ds(pl.multiple_of(j * t, t), t)
            vj = v_ref[rs, :]
            out = []
            for hh in range(2):
                m, l, acc = carry[hh]
                hs = slice(hh * LANES, (hh + 1) * LANES)
                sc = lax.dot_general(q_ref[:, hs], k_ref[rs, hs], _NT, preferred_element_type=F32)
                if masked:
                    sc = jnp.where(causal, sc, NEG)
                mn = jnp.maximum(m, jnp.max(sc, axis=-1, keepdims=True))
                p = jnp.exp(sc - mn)
                al = jnp.exp(m - mn)
                l = al * l + jnp.sum(p, axis=-1, keepdims=True)
                acc = al * acc + jnp.dot(p.astype(BF16), vj, preferred_element_type=F32)
                out.append((mn, l, acc))
            return tuple(out)

        init = tuple((jnp.full((t, 1), NEG, F32), jnp.zeros((t, 1), F32), jnp.zeros((t, LANES), F32))
                     for _ in range(2))
        carry = lax.fori_loop(0, i, lambda j, c: block(j, c, False), init)
        (m0, l0, a0), (m1, l1, a1) = block(i, carry, True)
        o_ref[...] = jnp.where(_head_lanes(t, 0), a0 / l0, a1 / l1)
        lse_ref[:, 0:1] = m0 + jnp.log(l0)
        lse_ref[:, 1:2] = m1 + jnp.log(l1)

    return pl.pallas_call(
        body, grid=(bsz, hp, nb), in_specs=[blk(2 * LANES), full(2 * LANES), full(LANES)],
        out_specs=[blk(LANES), col],
        out_shape=[jax.ShapeDtypeStruct((n, d), F32), jax.ShapeDtypeStruct((bsz, hp, seq, 2), F32)],
        name=name, compiler_params=_cp("parallel", "parallel", "arbitrary"),
    )(qa, ka, v)


def _flash_bwd(qa, ka, v, do, o, lse_row, bsz, seq, name):
    n, d = v.shape
    hp = d // LANES
    t = _tile(seq, ATTN_TILE)
    nb, blk, full, col, rows = _attn_specs(bsz, seq, t)
    scale = 1.0 / math.sqrt(HEAD_DIM)
    tn_ = (((0,), (0,)), ((), ()))

    def body(k_ref, v_ref, q_ref, do_ref, o_ref, lse_ref, dq_ref, dcc_ref, dk_ref, dv_ref, dck_ref, dqa, dl):
        j = pl.program_id(2)
        causal = lax.broadcasted_iota(jnp.int32, (t, t), 1) >= lax.broadcasted_iota(jnp.int32, (t, t), 0)
        heads = [_head_lanes(t, 0), _head_lanes(t, 1)]

        @pl.when(j == 0)
        def _():
            dqa[...] = jnp.zeros_like(dqa)
            ones = jnp.ones((8, LANES), F32)
            for ib in range(nb):
                rs = slice(ib * t, (ib + 1) * t)
                prod = do_ref[rs, :] * o_ref[rs, :]
                for hh in range(2):
                    dl[ib, hh:hh + 1, :] = lax.dot_general(ones, jnp.where(heads[hh], prod, 0.0), _NT, precision=HI,
                                                           preferred_element_type=F32)[0:1]

        vj = v_ref[...]

        def block(i, carry, masked):
            rs = pl.ds(pl.multiple_of(i * t, t), t)
            doi = do_ref[rs, :]
            dks, dvp = list(carry[:2]), carry[2]
            for hh in range(2):
                hs = slice(hh * LANES, (hh + 1) * LANES)
                kh, qi = k_ref[:, hs], q_ref[rs, hs]
                dom = jnp.where(heads[hh], doi, 0.0).astype(BF16)
                st = lax.dot_general(kh, qi, _NT, preferred_element_type=F32)
                if masked:
                    st = jnp.where(causal, st, NEG)
                pt = jnp.exp(st - lse_ref[i, hh:hh + 1, :])
                dvp = dvp + jnp.dot(pt.astype(BF16), dom, preferred_element_type=F32)
                dpt = lax.dot_general(vj, dom, _NT, preferred_element_type=F32)
                dsb = (pt * (dpt - dl[i, hh:hh + 1, :])).astype(BF16)
                dks[hh] = dks[hh] + jnp.dot(dsb, qi, preferred_element_type=F32)
                dqa[rs, hs] += lax.dot_general(dsb, kh, tn_, preferred_element_type=F32)
            return dks[0], dks[1], dvp

        zero = jnp.zeros((t, LANES), F32)
        carry = block(j, (zero, zero, zero), True)
        dk0, dk1, dvp = lax.fori_loop(j + 1, nb, lambda i, c: block(i, c, False), carry)
        dk_ref[...] = jnp.where(heads[0], dk0, pltpu.roll(dk1, HEAD_DIM, 1))
        dv_ref[...] = dvp
        dck_ref[:, 0:1] = -dk0[:, HEAD_DIM + 3:HEAD_DIM + 4]
        dck_ref[:, 1:2] = -dk1[:, HEAD_DIM + 3:HEAD_DIM + 4]

        @pl.when(j == nb - 1)
        def _():
            first = lax.broadcasted_iota(jnp.int32, (seq, LANES), 1) < HEAD_DIM
            dq_ref[...] = jnp.where(first, dqa[:, :LANES], pltpu.roll(dqa[:, LANES:], HEAD_DIM, 1)) * scale
            for hh in range(2):
                lo = hh * LANES + HEAD_DIM
                dcc_ref[:, hh:hh + 1] = dqa[:, lo:lo + 1]

    whole_col = pl.BlockSpec((None, None, seq, 2), lambda b, h, i: (b, h, 0, 0))
    cshape = jax.ShapeDtypeStruct((bsz, hp, seq, 2), F32)
    nd = jax.ShapeDtypeStruct((n, d), F32)
    return pl.pallas_call(
        body, grid=(bsz, hp, nb),
        in_specs=[blk(2 * LANES), blk(LANES), full(2 * LANES), full(LANES), full(LANES), rows],
        out_specs=[full(LANES), whole_col, blk(LANES), blk(LANES), col],
        out_shape=[nd, cshape, nd, nd, cshape],
        scratch_shapes=[pltpu.VMEM((seq, 2 * LANES), F32), pltpu.VMEM((nb, 2, t), F32)],
        name=name, compiler_params=_cp("parallel", "parallel", "arbitrary"),
    )(ka, v, qa, do, o, lse_row)


def _adamw(w, g, m, v, name):
    r, c = w.shape
    tr = r
    for cand in (512, 256, 128, 64, 32, 16, 8):
        if r % cand == 0 and r > cand and cand * c * 4 <= 4 * 1024 * 1024:
            tr = cand
            break

    def body(w_ref, g_ref, m_ref, v_ref, d_ref, m2_ref, v2_ref):
        gv = g_ref[...]
        m2 = ADAM_B1 * m_ref[...] + (1.0 - ADAM_B1) * gv
        v2 = ADAM_B2 * v_ref[...] + (1.0 - ADAM_B2) * jnp.square(gv)
        m_hat = m2 / (1.0 - ADAM_B1 ** ADAM_STEP)
        v_hat = v2 / (1.0 - ADAM_B2 ** ADAM_STEP)
        d_ref[...] = -ADAM_LR * (m_hat / (jnp.sqrt(v_hat) + ADAM_EPS) + ADAM_WD * w_ref[...])
        m2_ref[...] = m2
        v2_ref[...] = v2

    blk = pl.BlockSpec((tr, c), lambda i: (i, 0))
    shp = jax.ShapeDtypeStruct((r, c), F32)
    return pl.pallas_call(
        body, grid=(r // tr,), in_specs=[blk] * 4, out_specs=[blk] * 3, out_shape=[shp] * 3,
        name=name, compiler_params=_cp("parallel"),
    )(w, g, m, v)


def _adamw_nd(w, g, m, v, name):
    shape = w.shape
    two = (math.prod(shape[:-1]), shape[-1])
    d_, m_, v_ = _adamw(w.reshape(two), g.reshape(two), m.reshape(two), v.reshape(two), name)
    return g.reshape(shape), d_.reshape(shape), m_.reshape(shape), v_.reshape(shape)


def _place():
    x, y, c = lax.axis_index("x"), lax.axis_index("y"), lax.axis_index("c")
    chips = [(1 - x, y), (x, 1 - y), (1 - x, 1 - y)]
    return x, y, c, chips


def _gather_chips(shards):
    nt = len(shards)
    halves = [s.shape[0] // 2 for s in shards]

    def copies(ins, outs, send_sems, recv_sems):
        x, y, c, chips = _place()
        cps = []
        for t in range(nt):
            rows = pl.ds(c * halves[t], halves[t])
            for jj, (cx, cy) in enumerate(chips):
                cps.append(pltpu.make_async_remote_copy(
                    src_ref=ins[t].at[rows, :], dst_ref=outs[t].at[2 * x + y, rows, :], send_sem=send_sems.at[3 * t + jj],
                    recv_sem=recv_sems.at[3 * t + jj], device_id=(cx, cy, c), device_id_type=MESH))
        return cps

    def start(*refs):
        for cp in copies(*refs):
            cp.start()

    def wait(*refs):
        for cp in copies(*refs):
            cp.wait()

    return _Comm(shards, [jax.ShapeDtypeStruct((N_CHIPS,) + s.shape, s.dtype) for s in shards], 3 * nt, start, wait)


def _gather_sibling(shards, bufs, name):
    nt = len(shards)
    halves = [s.shape[0] // 2 for s in shards]

    def body(*refs):
        ins, outs = refs[:nt], refs[2 * nt:3 * nt]
        send_sems, recv_sems = refs[3 * nt:]
        x, y, c, chips = _place()
        sibling = (x, y, 1 - c)

        def copy(t, jj, hf):
            cx, cy = chips[jj]
            region = outs[t].at[2 * cx + cy, pl.ds(hf * halves[t], halves[t]), :]
            return pltpu.make_async_remote_copy(src_ref=region, dst_ref=region, send_sem=send_sems.at[4 * t + jj],
                                                recv_sem=recv_sems.at[4 * t + jj], device_id=sibling,
                                                device_id_type=MESH)

        def own(t):
            return pltpu.make_async_remote_copy(src_ref=ins[t], dst_ref=outs[t].at[2 * x + y],
                                                send_sem=send_sems.at[4 * t + 3], recv_sem=recv_sems.at[4 * t + 3],
                                                device_id=sibling, device_id_type=MESH)

        sends = [copy(t, jj, c) for t in range(nt) for jj in range(3)] + [own(t) for t in range(nt)]
        for cp in sends:
            cp.start()
        for t in range(nt):
            for jj in range(3):
                copy(t, jj, 1 - c).wait_recv()
            own(t).wait_recv()
        for cp in sends:
            cp.wait_send()

    return pl.pallas_call(
        body, in_specs=[ANY] * (2 * nt), out_specs=[ANY] * nt,
        out_shape=[jax.ShapeDtypeStruct(b.shape, b.dtype) for b in bufs],
        scratch_shapes=[pltpu.SemaphoreType.DMA((4 * nt,)), pltpu.SemaphoreType.DMA((4 * nt,))],
        input_output_aliases={nt + t: t for t in range(nt)}, name=name,
        compiler_params=pltpu.CompilerParams(has_side_effects=True),
    )(*shards, *bufs)


N_PARTIALS = 7


def _scatter_partials(grads):
    nt = len(grads)
    halves = [g.shape[1] // 2 for g in grads]

    def copies(ins, outs, send_sems, recv_sems):
        x, y, c, chips = _place()
        cps = []

        def copy(t, kk, slot, core, to):
            rows = pl.ds(core * halves[t], halves[t])
            return pltpu.make_async_remote_copy(
                src_ref=ins[t].at[slot, rows, :], dst_ref=outs[t].at[kk], send_sem=send_sems.at[N_PARTIALS * t + kk],
                recv_sem=recv_sems.at[N_PARTIALS * t + kk], device_id=to, device_id_type=MESH)

        for t in range(nt):
            for jj, (cx, cy) in enumerate(chips):
                cps.append(copy(t, 2 * jj, 2 * cx + cy, c, (cx, cy, c)))
                cps.append(copy(t, 2 * jj + 1, 2 * cx + cy, 1 - c, (cx, cy, 1 - c)))
            cps.append(copy(t, N_PARTIALS - 1, 2 * x + y, 1 - c, (x, y, 1 - c)))
        return cps

    def start(*refs):
        for cp in copies(*refs):
            cp.start()

    def wait(*refs):
        for cp in copies(*refs):
            cp.wait()

    return _Comm(grads, [jax.ShapeDtypeStruct((N_PARTIALS, h, g.shape[2]), g.dtype) for g, h in zip(grads, halves)],
                 N_PARTIALS * nt, start, wait)


def _add_partials(g, got, place_idx, name):
    s, r, w = g.shape
    rh = r // 2
    tr = rh
    for cand in (256, 128, 64, 32, 16):
        if rh % cand == 0 and cand * w * 4 <= 2 * 1024 * 1024:
            tr = cand
            break
    per = rh // tr

    def body(b_ref, g_ref, *rest):
        o_ref = rest[-1]
        acc = g_ref[...].astype(F32)
        for r_ref in rest[:-1]:
            acc = acc + r_ref[...].astype(F32)
        o_ref[...] = acc

    return pl.pallas_call(
        body,
        grid_spec=pltpu.PrefetchScalarGridSpec(
            num_scalar_prefetch=1, grid=(per,),
            in_specs=[pl.BlockSpec((None, tr, w), lambda a, b_ref: (b_ref[0], b_ref[1] * per + a, 0))]
            + [pl.BlockSpec((None, tr, w), lambda a, b_ref, kk=kk: (kk, a, 0)) for kk in range(N_PARTIALS)],
            out_specs=pl.BlockSpec((tr, w), lambda a, b_ref: (b_ref[1] * per + a, 0))),
        out_shape=jax.ShapeDtypeStruct((r, w), F32), name=name, compiler_params=_cp("parallel"),
    )(place_idx, g, *([got] * N_PARTIALS))


def _swap_halves(bufs, name):
    nt = len(bufs)

    def body(*refs):
        outs = refs[nt:2 * nt]
        send_sems, recv_sems = refs[2 * nt:]
        x, y, c, _ = _place()

        def copy(t, hf):
            rh = outs[t].shape[0] // 2
            region = outs[t].at[pl.ds(hf * rh, rh), :]
            return pltpu.make_async_remote_copy(src_ref=region, dst_ref=region, send_sem=send_sems.at[t],
                                                recv_sem=recv_sems.at[t], device_id=(x, y, 1 - c),
                                                device_id_type=MESH)

        sends = [copy(t, c) for t in range(nt)]
        for cp in sends:
            cp.start()
        for t in range(nt):
            copy(t, 1 - c).wait_recv()
        for cp in sends:
            cp.wait_send()

    return pl.pallas_call(
        body, in_specs=[ANY] * nt, out_specs=[ANY] * nt,
        out_shape=[jax.ShapeDtypeStruct(b.shape, b.dtype) for b in bufs],
        scratch_shapes=[pltpu.SemaphoreType.DMA((nt,)), pltpu.SemaphoreType.DMA((nt,))],
        input_output_aliases={t: t for t in range(nt)}, name=name,
        compiler_params=pltpu.CompilerParams(has_side_effects=True),
    )(*bufs)


def _rs_finish(grads, got, place_idx, tag):
    fin = [_add_partials(g, r, place_idx, name=f"rs_add_{tag}_{t}") for t, (g, r) in enumerate(zip(grads, got))]
    return _swap_halves(fin, name=f"rs_swap_{tag}")


def _all_reduce_small(v, name):
    r, w = v.shape

    def body(v_ref, o_ref, buf, send_sems, recv_sems):
        x, y, c, _ = _place()
        me = 4 * x + 2 * y + c
        buf[me] = v_ref[...]
        cps = []
        for kk in range(1, 8):
            peer = (x ^ ((kk >> 2) & 1), y ^ ((kk >> 1) & 1), c ^ (kk & 1))
            cps.append(pltpu.make_async_remote_copy(src_ref=v_ref, dst_ref=buf.at[me], send_sem=send_sems.at[kk - 1],
                                                    recv_sem=recv_sems.at[kk - 1], device_id=peer, device_id_type=MESH))
        for cp in cps:
            cp.start()
        for kk in range(1, 8):
            pltpu.make_async_remote_copy(src_ref=v_ref, dst_ref=buf.at[me ^ kk], send_sem=send_sems.at[kk - 1],
                                         recv_sem=recv_sems.at[kk - 1], device_id=(x, y, c),
                                         device_id_type=MESH).wait_recv()
        for cp in cps:
            cp.wait_send()
        acc = buf[0]
        for dev in range(1, 8):
            acc = acc + buf[dev]
        o_ref[...] = acc

    vm = pl.BlockSpec(memory_space=pltpu.VMEM)
    return pl.pallas_call(
        body, in_specs=[vm], out_specs=vm, out_shape=jax.ShapeDtypeStruct((r, w), F32),
        scratch_shapes=[pltpu.VMEM((8, r, w), F32), pltpu.SemaphoreType.DMA((7,)), pltpu.SemaphoreType.DMA((7,))],
        name=name, compiler_params=pltpu.CompilerParams(has_side_effects=True),
    )(v)


def _to_shards(a, axis=-1):
    axis = axis % a.ndim
    shp = a.shape
    a = a.reshape(shp[:axis] + (N_CHIPS, shp[axis] // N_CHIPS) + shp[axis + 1:])
    return jnp.moveaxis(a, axis, 0).reshape(N_CHIPS, -1)


def _from_shards(s, shard_shape, axis=-1):
    axis = axis % len(shard_shape)
    a = jnp.moveaxis(s.reshape((N_CHIPS,) + tuple(shard_shape)), 0, axis)
    return a.reshape(tuple(shard_shape[:axis]) + (N_CHIPS * shard_shape[axis],) + tuple(shard_shape[axis + 1:]))


def _pack(vecs, rows):
    flat = jnp.concatenate([v.reshape(v.shape[0], -1) if v.ndim > 1 else v.reshape(1, -1) for v in vecs], axis=1)
    lead = flat.shape[0]
    flat = jnp.pad(flat, ((0, 0), (0, rows * LANES - flat.shape[1])))
    return flat.reshape(lead, rows, LANES)


def _pack_rows(sizes, mult):
    total = sum(sizes)
    rows = -(-total // LANES)
    return -(-rows // mult) * mult


def _unpack(flat, shapes):
    out, pos = [], 0
    for shp in shapes:
        sz = math.prod(shp)
        out.append(flat[..., pos:pos + sz].reshape(flat.shape[:-1] + tuple(shp)))
        pos += sz
    return out


def _row_layout(col, t):
    bsz, hp, seq, _ = col.shape
    return col.reshape(bsz, hp, seq // t, t, 2).transpose(0, 1, 2, 4, 3)


def _from_col_layout(col):
    bsz, hp, seq, _ = col.shape
    a = col.transpose(0, 2, 1, 3).reshape(bsz * seq, 2 * hp)
    return jnp.pad(a, ((0, 0), (0, LANES - 2 * hp)))


def kernel(x, norm_mix, norm_ffn, conv_w_in, conv_b_in, conv_dw, conv_dw_b, conv_ln_g, conv_ln_b, conv_w_out, conv_b_out, pool_w, pool_b, pool_scale, fox_w_in, fox_b_f, fox_q_gain, fox_k_gain, fox_w_o, ffn_w_up, ffn_dw, ffn_dw_b, ffn_w_down, loss_target, m_norm_mix, m_norm_ffn, m_conv_w_in, m_conv_b_in, m_conv_dw, m_conv_dw_b, m_conv_ln_g, m_conv_ln_b, m_conv_w_out, m_conv_b_out, m_pool_w, m_pool_b, m_pool_scale, m_fox_w_in, m_fox_b_f, m_fox_q_gain, m_fox_k_gain, m_fox_w_o, m_ffn_w_up, m_ffn_dw, m_ffn_dw_b, m_ffn_w_down, v_norm_mix, v_norm_ffn, v_conv_w_in, v_conv_b_in, v_conv_dw, v_conv_dw_b, v_conv_ln_g, v_conv_ln_b, v_conv_w_out, v_conv_b_out, v_pool_w, v_pool_b, v_pool_scale, v_fox_w_in, v_fox_b_f, v_fox_q_gain, v_fox_k_gain, v_fox_w_o, v_ffn_w_up, v_ffn_dw, v_ffn_dw_b, v_ffn_w_down):
    bsz, seq, d = x.shape
    n = bsz * seq
    depth = norm_mix.shape[0]
    n_conv, n_pool, n_fox = conv_w_in.shape[0], pool_w.shape[0], fox_w_in.shape[0]
    f2 = ffn_dw_b.shape[1]
    f = f2 // 2
    nh = d // HEAD_DIM
    hp = d // LANES
    ng = len(POOL_WINDOWS)
    gd = d // ng
    chip_idx = jnp.stack([2 * lax.axis_index("x") + lax.axis_index("y"), lax.axis_index("c")]).astype(jnp.int32)

    small_shapes = [conv_b_in.shape, conv_dw.shape, conv_dw_b.shape, conv_ln_g.shape, conv_ln_b.shape,
                    conv_b_out.shape, pool_b.shape, ffn_dw.shape]
    small_rows = _pack_rows([math.prod(s) for s in small_shapes], 16)
    small = _pack([v.reshape(1, -1) for v in (conv_b_in, conv_dw, conv_dw_b, conv_ln_g, conv_ln_b, conv_b_out,
                                                pool_b, ffn_dw)], small_rows)[0]
    def layer_shards(i):
        kind, j = i % 3, i // 3
        shards = [ffn_w_up[i].astype(BF16), ffn_w_down[i].astype(BF16)]
        if kind == 0:
            shards += [conv_w_in[j].astype(BF16), conv_w_out[j].astype(BF16)]
        elif kind == 1:
            shards += [pool_w[j].reshape(ng * (gd // N_CHIPS), gd).astype(BF16)]
        else:
            shards += [fox_w_in[j].astype(BF16), fox_w_o[j].astype(BF16)]
        if i == 0:
            shards.append(small)
        return shards

    gathered = [None] * depth
    first_now = layer_shards(0)[2:]
    gathered[0] = [None, None] + list(_gather_sibling(
        first_now, _run_comm(_gather_chips(first_now), name="gather_chips_l0"), name="gather_sibling_l0"))
    small_all = gathered[0][-1].reshape(N_CHIPS, -1)
    sm = _unpack(small_all, small_shapes)
    axes = [-1] * 8
    b_in_f, dw_f, dw_b_f, ln_g_f, ln_b_f, b_out_f, pool_b_f, ffn_dw_f = [
        _from_shards(s_.reshape(N_CHIPS, -1), shp, ax) for s_, shp, ax in zip(sm, small_shapes, axes)]

    xs = x.reshape(n, d)
    tgt = loss_target.reshape(n, d)
    vec = lambda a: a.reshape(1, -1)

    saved = []
    cur = xs
    for i in range(depth):
        kind, j = i % 3, i // 3
        wts = gathered[i]
        sv = {"x_in": cur}
        if kind == 0:
            w_in, w_out = wts[2], wts[3].reshape(d, d)
            wcol = w_in.shape[2]
            h = _rms_fwd(cur, vec(norm_mix[i]), name=f"rms_mix_l{i}")
            pa = _mm(h, w_in, m=n, n=d, k=d, tk=d, tn=wcol, b_stk=wcol, b_s0=0, bias=vec(b_in_f[j, :d]),
                     name=f"conv_in_a_l{i}")
            pg = _mm(h, w_in, m=n, n=d, k=d, tk=d, tn=wcol, b_stk=wcol, b_s0=2, bias=vec(b_in_f[j, d:]),
                     name=f"conv_in_g_l{i}")
            taps = jnp.pad(dw_f[j], ((0, CONV_HALO - CONV_WIDTH), (0, 0)))
            conf_args = (pa, pg, taps, vec(dw_b_f[j]), vec(ln_g_f[j]), vec(ln_b_f[j]), seq)
            if i == 0:
                ffn_shards = layer_shards(0)[:2]
                (u, s_), landed = _conf_fwd(*conf_args, name=f"conf_fwd_l{i}", comm=_gather_chips(ffn_shards))
                gathered[0][:2] = _gather_sibling(ffn_shards, landed, name="gather_sibling_ffn_l0")
            else:
                u, s_ = _conf_fwd(*conf_args, name=f"conf_fwd_l{i}")
            cur = _mm(s_, w_out, m=n, n=d, k=d, tk=d, bias=vec(b_out_f[j]), res=cur, name=f"conv_out_l{i}")
            sv.update(h=h, pa=pa, pg=pg, u=u, s=s_, taps=taps)
        elif kind == 1:
            pw = wts[2].reshape(N_CHIPS, ng, gd // N_CHIPS, gd).transpose(1, 0, 2, 3).reshape(ng, gd, gd)
            cur, p = _pool_fwd(cur, vec(norm_mix[i]), pw, vec(pool_b_f[j]), vec(pool_scale[j]), seq,
                               name=f"pool_fwd_l{i}")
            sv.update(p=p, pw=pw)
        else:
            w_in = wts[2].transpose(1, 0, 2).reshape(d, -1)
            w_qkv = w_in[:, :3 * d]
            w_f = jnp.pad(w_in[:, 3 * d:], ((0, 0), (0, LANES - nh)))
            w_o = wts[3].reshape(d, d)
            bf = jnp.pad(vec(fox_b_f[j]), ((0, 0), (0, LANES - nh)))
            qg, kg = jnp.tile(vec(fox_q_gain[j]), (1, nh)), jnp.tile(vec(fox_k_gain[j]), (1, nh))
            h = _rms_fwd(cur, vec(norm_mix[i]), name=f"rms_mix_l{i}")
            qkv = _mm(h, w_qkv, m=n, n=3 * d, k=d, tk=d, tn=d, name=f"fox_qkv_l{i}")
            fl = _mm(h, w_f, m=n, n=LANES, k=d, tk=d, name=f"fox_fl_l{i}")
            qa, ka, v = _fox_prep_fwd(qkv, fl, bf, qg, kg, seq, name=f"fox_prep_l{i}")
            o, lse = _flash_fwd(qa, ka, v, bsz, seq, name=f"fox_attn_l{i}")
            cur = _mm(o, w_o, m=n, n=d, k=d, tk=d, res=cur, name=f"fox_out_l{i}")
            sv.update(h=h, qkv=qkv, fl=fl, qa=qa, ka=ka, v=v, o=o, lse=lse, w_qkv=w_qkv, w_f=w_f, w_o=w_o, bf=bf,
                      qg=qg, kg=kg)
        w_up, w_down = wts[0], wts[1].reshape(f, d)
        ucol = w_up.shape[2]
        sv["x_mid"] = cur
        h2 = _rms_fwd(cur, vec(norm_ffn[i]), name=f"rms_ffn_l{i}")
        uv = _mm(h2, w_up, m=n, n=f, k=d, tk=d, tn=ucol, b_stk=ucol, b_s0=0, name=f"ffn_up_v_l{i}")
        ug = _mm(h2, w_up, m=n, n=f, k=d, tk=d, tn=ucol, b_stk=ucol, b_s0=2, name=f"ffn_up_g_l{i}")
        fdw, fdb = ffn_dw_f[i], ffn_dw_b[i]
        glu_args = (uv, ug, fdw[:, :f], fdw[:, f:], vec(fdb[:f]), vec(fdb[f:]), w_down, cur, seq)
        if i + 1 < depth:
            nxt_shards = layer_shards(i + 1)
            (a_, cur), landed = _ffn_glu_down(*glu_args, name=f"ffn_glu_down_l{i}", comm=_gather_chips(nxt_shards))
            gathered[i + 1] = _gather_sibling(nxt_shards, landed, name=f"gather_sibling_l{i + 1}")
        else:
            a_, cur = _ffn_glu_down(*glu_args, name=f"ffn_glu_down_l{i}")
        sv.update(h2=h2, uv=uv, ug=ug, a=a_)
        saved.append(sv)

    dy, loss_part = _loss(cur, tgt, name="loss")
    loss = lax.psum(loss_part[0, 0], ("x", "y", "c"))

    g_norm_mix, g_norm_ffn = [None] * depth, [None] * depth
    g_ffn_dw_b = [None] * depth
    g_up, g_down = [None] * depth, [None] * depth
    g_conv_in, g_conv_out = [None] * n_conv, [None] * n_conv
    g_pool_w = g_fox_in = g_fox_o = None
    g_pool_scale = g_bf = g_qg = g_kg = None
    part_small = {"b_in": [None] * n_conv, "dw": [None] * n_conv, "dw_b": [None] * n_conv, "ln_g": [None] * n_conv,
                  "ln_b": [None] * n_conv, "b_out": [None] * n_conv, "pool_b": None, "ffn_dw": [None] * depth}

    pending = []
    done = {}

    def take_pending():
        groups = list(pending)
        pending.clear()
        parts = [p_ for g_ in groups for p_ in g_[0]]
        return groups, (_scatter_partials(parts) if parts else None)

    def finish_groups(groups, r2):
        pos = 0
        for parts, tag, sink in groups:
            sink(_rs_finish(parts, r2[pos:pos + len(parts)], chip_idx, tag))
            pos += len(parts)

    def carried(fn, args, name):
        groups, comm = take_pending()
        if comm is None:
            return fn(*args, name=name)
        outs, r2 = fn(*args, name=name, comm=comm)
        finish_groups(groups, r2)
        return outs

    dcur = dy
    for i in reversed(range(depth)):
        kind, j = i % 3, i // 3
        wts, sv = gathered[i], saved[i]
        w_up, w_down = wts[0], wts[1].reshape(f, d)
        ucol = w_up.shape[2]
        fdw, fdb = ffn_dw_f[i], ffn_dw_b[i]
        dw_down = _mm(sv["a"], dcur, m=f, n=d, k=n, tm=f // 2, tn=d, tk=2048, ta=True, out_dtype=BF16, name=f"ffn_down_dw_l{i}")
        duv, dug, dwv, dwg, dbv, dbg = carried(
            _ffn_glu_bwd, (sv["uv"], sv["ug"], dcur, w_down, fdw[:, :f], fdw[:, f:], vec(fdb[:f]), vec(fdb[f:]), seq),
            name=f"ffn_glu_bwd_l{i}")
        dw_up = _mm(sv["h2"], duv, m=d, n=f, k=n, tm=d, tn=ucol, tk=2048, ta=True, out_dtype=BF16, o_stk=ucol, o_s0=0,
                    o_slots=N_CHIPS, name=f"ffn_up_dw_v_l{i}")
        dw_up = _mm(sv["h2"], dug, m=d, n=f, k=n, tm=d, tn=ucol, tk=2048, ta=True, out_dtype=BF16, o_stk=ucol, o_s0=2,
                    o_slots=N_CHIPS, o_buf=dw_up, name=f"ffn_up_dw_g_l{i}")
        dh2 = _mm(duv, w_up, m=n, n=d, k=f, tn=d, tk=f, tb=True, b_stk=ucol, b_s0=0, name=f"ffn_up_dx_v_l{i}")
        dh2 = _mm(dug, w_up, m=n, n=d, k=f, tn=d, tk=f, tb=True, b_stk=ucol, b_s0=2, res=dh2,
                  name=f"ffn_up_dx_g_l{i}")
        dmid, g_norm_ffn[i] = _rms_bwd(sv["x_mid"], dh2, vec(norm_ffn[i]), dcur, name=f"rms_ffn_bwd_l{i}")
        part_small["ffn_dw"][i] = jnp.concatenate([dwv, dwg], axis=1)
        g_ffn_dw_b[i] = jnp.concatenate([dbv, dbg], axis=1)

        def ffn_sink(red, i=i):
            g_up[i], g_down[i] = red[0], red[1]

        pending.append(([dw_up, dw_down.reshape(N_CHIPS, f // N_CHIPS, d)], f"ffn_l{i}", ffn_sink))

        if kind == 0:
            w_in, w_out = wts[2], wts[3].reshape(d, d)
            wcol = w_in.shape[2]
            ds = _mm(dmid, w_out, m=n, n=d, k=d, tk=d, tb=True, name=f"conv_out_dx_l{i}")
            dw_out = _mm(sv["s"], dmid, m=d, n=d, k=n, tm=d, tn=d, tk=2048, ta=True, out_dtype=BF16, name=f"conv_out_dw_l{i}")
            dpa, dpg, ddw, ddwb, dlng, dlnb, dba, dbg_ = carried(
                _conf_bwd, (sv["u"], ds, sv["pa"], sv["pg"], sv["taps"], vec(ln_g_f[j]), vec(ln_b_f[j]), seq),
                name=f"conf_bwd_l{i}")
            dw_in = _mm(sv["h"], dpa, m=d, n=d, k=n, tm=d, tn=wcol, tk=2048, ta=True, out_dtype=BF16, o_stk=wcol, o_s0=0,
                        o_slots=N_CHIPS, name=f"conv_in_dw_a_l{i}")
            dw_in = _mm(sv["h"], dpg, m=d, n=d, k=n, tm=d, tn=wcol, tk=2048, ta=True, out_dtype=BF16, o_stk=wcol, o_s0=2,
                        o_slots=N_CHIPS, o_buf=dw_in, name=f"conv_in_dw_g_l{i}")
            dh = _mm(dpa, w_in, m=n, n=d, k=d, tn=d, tk=d, tb=True, b_stk=wcol, b_s0=0, name=f"conv_in_dx_a_l{i}")
            dh = _mm(dpg, w_in, m=n, n=d, k=d, tn=d, tk=d, tb=True, b_stk=wcol, b_s0=2, res=dh,
                     name=f"conv_in_dx_g_l{i}")
            dcur, g_norm_mix[i], db_out = _rms_bwd(sv["x_in"], dh, vec(norm_mix[i]), dmid, name=f"rms_mix_bwd_l{i}",
                                                   colsum=True)
            part_small["b_in"][j] = jnp.concatenate([dba, dbg_], axis=1)
            part_small["dw"][j] = ddw[:CONV_WIDTH]
            part_small["dw_b"][j], part_small["ln_g"][j], part_small["ln_b"][j] = ddwb, dlng, dlnb
            part_small["b_out"][j] = db_out
            mix_grads = [dw_in, dw_out.reshape(N_CHIPS, d // N_CHIPS, d)]
        elif kind == 1:
            dcur, g_norm_mix[i], dpw, dpb, dpsc = _pool_bwd(dmid, sv["x_in"], sv["p"], vec(norm_mix[i]), sv["pw"],
                                                            vec(pool_b_f[j]), vec(pool_scale[j]), seq,
                                                            name=f"pool_bwd_l{i}")
            part_small["pool_b"] = dpb
            g_pool_scale = dpsc
            mix_grads = [dpw.reshape(ng, N_CHIPS, gd // N_CHIPS, gd).transpose(1, 0, 2, 3).reshape(N_CHIPS, gd, gd)
                         .astype(BF16)]
        else:
            do = _mm(dmid, sv["w_o"], m=n, n=d, k=d, tk=d, tb=True, name=f"fox_out_dx_l{i}")
            dw_o = _mm(sv["o"], dmid, m=d, n=d, k=n, tm=d, tn=d, tk=2048, ta=True, out_dtype=BF16, name=f"fox_out_dw_l{i}")
            dq, dcc, dk, dv, dck = _flash_bwd(sv["qa"], sv["ka"], sv["v"], do, sv["o"],
                                              _row_layout(sv["lse"], _tile(seq, ATTN_TILE)), bsz, seq,
                                              name=f"fox_attn_bwd_l{i}")
            dqkv, dfl, dqg, dkg, dbf = _fox_prep_bwd(sv["qkv"], dq, dk, dv, _from_col_layout(dcc), _from_col_layout(dck),
                                                     sv["fl"], sv["bf"], sv["qg"], sv["kg"], seq,
                                                     name=f"fox_prep_bwd_l{i}")
            dw_qkv = _mm(sv["h"], dqkv, m=d, n=3 * d, k=n, tm=d, tn=d, tk=2048, ta=True, out_dtype=BF16, name=f"fox_qkv_dw_l{i}")
            dw_f = _mm(sv["h"], dfl, m=d, n=LANES, k=n, tm=d, tk=2048, ta=True, out_dtype=BF16, name=f"fox_fl_dw_l{i}")
            dh = _mm(dqkv, sv["w_qkv"], m=n, n=d, k=3 * d, tn=d, tk=d, tb=True, name=f"fox_qkv_dx_l{i}")
            dh = _mm(dfl, sv["w_f"], m=n, n=d, k=LANES, tn=d, tb=True, res=dh, name=f"fox_fl_dx_l{i}")
            dcur, g_norm_mix[i] = _rms_bwd(sv["x_in"], dh, vec(norm_mix[i]), dmid, name=f"rms_mix_bwd_l{i}")
            g_bf = dbf[:, :nh]
            g_qg = dqg.reshape(nh, HEAD_DIM).sum(axis=0, keepdims=True)
            g_kg = dkg.reshape(nh, HEAD_DIM).sum(axis=0, keepdims=True)
            dw_in_full = jnp.concatenate([dw_qkv, dw_f[:, :nh]], axis=1)
            wshard = dw_in_full.shape[1] // N_CHIPS
            mix_grads = [dw_in_full.reshape(d, N_CHIPS, wshard).transpose(1, 0, 2),
                         dw_o.reshape(N_CHIPS, d // N_CHIPS, d)]
        if i == 0:
            sm_parts = [jnp.concatenate(part_small["b_in"]), jnp.stack(part_small["dw"]),
                        jnp.concatenate(part_small["dw_b"]), jnp.concatenate(part_small["ln_g"]),
                        jnp.concatenate(part_small["ln_b"]), jnp.concatenate(part_small["b_out"]),
                        part_small["pool_b"].reshape(n_pool, ng, gd), jnp.stack(part_small["ffn_dw"])]
            mix_grads.append(_pack([_to_shards(p_) for p_ in sm_parts], small_rows))

        def mix_sink(red, i=i, kind=kind, j=j):
            if kind == 0:
                g_conv_in[j], g_conv_out[j] = red[0], red[1]
            elif kind == 1:
                done["pool_w"] = red[0]
            else:
                done["fox_in"], done["fox_o"] = red[0], red[1]
            if i == 0:
                done["small"] = red[-1].reshape(-1)

        pending.append((mix_grads, f"mix_l{i}", mix_sink))

    groups, comm = take_pending()
    finish_groups(groups, _run_comm(comm, name="rs_chips_tail"))
    g_pool_w, g_fox_in, g_fox_o, g_small_flat = done["pool_w"], done["fox_in"], done["fox_o"], done["small"]

    grad_x = dcur.reshape(bsz, seq, d)

    rep_parts = [jnp.concatenate(g_norm_mix), jnp.concatenate(g_norm_ffn), g_pool_scale, g_bf, g_qg, g_kg,
                 jnp.concatenate(g_ffn_dw_b)]
    rep_shapes = [norm_mix.shape, norm_ffn.shape, pool_scale.shape, fox_b_f.shape, fox_q_gain.shape,
                  fox_k_gain.shape, ffn_dw_b.shape]
    rep_rows = _pack_rows([math.prod(s) for s in rep_shapes], 8)
    rep = _all_reduce_small(_pack([p_.reshape(1, -1) for p_ in rep_parts], rep_rows)[0], name="all_reduce_small")
    g_rep = _unpack(rep.reshape(-1), rep_shapes)
    g_sm = _unpack(g_small_flat, small_shapes)

    grads = {
        "norm_mix": g_rep[0], "norm_ffn": g_rep[1],
        "conv_w_in": jnp.stack(g_conv_in), "conv_b_in": g_sm[0], "conv_dw": g_sm[1], "conv_dw_b": g_sm[2],
        "conv_ln_g": g_sm[3], "conv_ln_b": g_sm[4], "conv_w_out": jnp.stack(g_conv_out), "conv_b_out": g_sm[5],
        "pool_w": g_pool_w.reshape(pool_w.shape), "pool_b": g_sm[6], "pool_scale": g_rep[2],
        "fox_w_in": g_fox_in.reshape(fox_w_in.shape), "fox_b_f": g_rep[3], "fox_q_gain": g_rep[4],
        "fox_k_gain": g_rep[5], "fox_w_o": g_fox_o.reshape(fox_w_o.shape),
        "ffn_w_up": jnp.stack(g_up), "ffn_dw": g_sm[7], "ffn_dw_b": g_rep[6], "ffn_w_down": jnp.stack(g_down),
    }
    weights = dict(norm_mix=norm_mix, norm_ffn=norm_ffn, conv_w_in=conv_w_in, conv_b_in=conv_b_in, conv_dw=conv_dw,
                   conv_dw_b=conv_dw_b, conv_ln_g=conv_ln_g, conv_ln_b=conv_ln_b, conv_w_out=conv_w_out,
                   conv_b_out=conv_b_out, pool_w=pool_w, pool_b=pool_b, pool_scale=pool_scale, fox_w_in=fox_w_in,
                   fox_b_f=fox_b_f, fox_q_gain=fox_q_gain, fox_k_gain=fox_k_gain, fox_w_o=fox_w_o, ffn_w_up=ffn_w_up,
                   ffn_dw=ffn_dw, ffn_dw_b=ffn_dw_b, ffn_w_down=ffn_w_down)
    m_in = dict(norm_mix=m_norm_mix, norm_ffn=m_norm_ffn, conv_w_in=m_conv_w_in, conv_b_in=m_conv_b_in,
                conv_dw=m_conv_dw, conv_dw_b=m_conv_dw_b, conv_ln_g=m_conv_ln_g, conv_ln_b=m_conv_ln_b,
                conv_w_out=m_conv_w_out, conv_b_out=m_conv_b_out, pool_w=m_pool_w, pool_b=m_pool_b,
                pool_scale=m_pool_scale, fox_w_in=m_fox_w_in, fox_b_f=m_fox_b_f, fox_q_gain=m_fox_q_gain,
                fox_k_gain=m_fox_k_gain, fox_w_o=m_fox_w_o, ffn_w_up=m_ffn_w_up, ffn_dw=m_ffn_dw,
                ffn_dw_b=m_ffn_dw_b, ffn_w_down=m_ffn_w_down)
    v_in = dict(norm_mix=v_norm_mix, norm_ffn=v_norm_ffn, conv_w_in=v_conv_w_in, conv_b_in=v_conv_b_in,
                conv_dw=v_conv_dw, conv_dw_b=v_conv_dw_b, conv_ln_g=v_conv_ln_g, conv_ln_b=v_conv_ln_b,
                conv_w_out=v_conv_w_out, conv_b_out=v_conv_b_out, pool_w=v_pool_w, pool_b=v_pool_b,
                pool_scale=v_pool_scale, fox_w_in=v_fox_w_in, fox_b_f=v_fox_b_f, fox_q_gain=v_fox_q_gain,
                fox_k_gain=v_fox_k_gain, fox_w_o=v_fox_w_o, ffn_w_up=v_ffn_w_up, ffn_dw=v_ffn_dw,
                ffn_dw_b=v_ffn_dw_b, ffn_w_down=v_ffn_w_down)
    names = list(weights)
    g_out, d_out, m_out, v_out = [], [], [], []
    for nm in names:
        g_, dl_, m_, v_ = _adamw_nd(weights[nm], grads[nm].reshape(weights[nm].shape), m_in[nm], v_in[nm],
                                    name=f"adamw_{nm}")
        g_out.append(g_)
        d_out.append(dl_)
        m_out.append(m_)
        v_out.append(v_)
    return (loss, grad_x, *g_out, *d_out, *m_out, *v_out)
```

```python
import math

import jax
import jax.numpy as jnp
import numpy as np
from jax import lax
from jax.experimental import pallas as pl
from jax.experimental.pallas import tpu as pltpu

F32 = jnp.float32
BF16 = jnp.bfloat16
HI = lax.Precision.HIGHEST
MESH = pl.DeviceIdType.MESH
ANY = pl.BlockSpec(memory_space=pl.ANY)

EPS = 1e-6
HEAD_DIM = 64
LANES = 128
POOL_WINDOWS = (2, 4, 8, 16)
CONV_WIDTH = 31
CONV_HALO = 32
FFN_HALO = 8
FFN_ROWS, FFN_COLS = 512, 1408
FFN_DOWN_ROWS, FFN_DOWN_COLS = 256, 2816
POOL_HALO = 16
N_CHIPS = 4
NEG = -1e30

ADAM_LR = 0.001
ADAM_B1 = 0.9
ADAM_B2 = 0.999
ADAM_EPS = 1e-08
ADAM_WD = 0.01
ADAM_STEP = 10

V7X_VMEM_LIMIT_BYTES = 56 * 1024 * 1024


def _cp(*sem):
    return pltpu.CompilerParams(dimension_semantics=sem or None, vmem_limit_bytes=V7X_VMEM_LIMIT_BYTES)


def _tile(n, pref):
    t = min(n, pref)
    assert n % t == 0, (n, pref)
    return t


def _sig(v):
    return jax.nn.sigmoid(v)


def _roll(v, shift):
    n = v.shape[0]
    shift = shift % n
    return v if shift == 0 else pltpu.roll(v, shift, 0)


SUBLANES = 8


CONV_ROWS = 64


def _conv_taps(rot_ref, w_ref, out_ref, tm, start_of):
    d = out_ref.shape[1]
    for lc in range(d // LANES):
        ls = slice(lc * LANES, (lc + 1) * LANES)

        for r0 in range(0, tm, CONV_ROWS):
            acc = None
            for sh in range(CONV_WIDTH):
                kk = CONV_WIDTH - 1 - sh
                lo = r0 + start_of(sh)
                term = w_ref[kk:kk + 1, ls] * rot_ref[sh % SUBLANES, lo:lo + CONV_ROWS, ls]
                acc = term if acc is None else acc + term
            out_ref[r0:r0 + CONV_ROWS, ls] = acc


def _tap_grads(rotz_ref, rotd_ref, dw_ref, tm, halo):
    d = dw_ref.shape[1]
    for lc in range(d // LANES):
        ls = slice(lc * LANES, (lc + 1) * LANES)

        acc = [None] * CONV_WIDTH
        for r0 in range(0, tm, CONV_ROWS):
            duc = rotd_ref[0, r0:r0 + CONV_ROWS, ls]
            for sh in range(CONV_WIDTH):
                lo = r0 + halo - (sh // SUBLANES) * SUBLANES
                prod = duc * rotz_ref[sh % SUBLANES, lo:lo + CONV_ROWS, ls]
                part = prod.reshape(CONV_ROWS // SUBLANES, SUBLANES, LANES).sum(axis=0)
                acc[sh] = part if acc[sh] is None else acc[sh] + part
        for sh in range(CONV_WIDTH):
            kk = CONV_WIDTH - 1 - sh
            dw_ref[kk:kk + 1, ls] += jnp.sum(acc[sh], axis=0, keepdims=True)


class _Comm:
    def __init__(self, ins, out_shapes, n_sems, start, wait):
        self.ins, self.out_shapes, self.n_sems, self.start, self.wait = list(ins), list(out_shapes), n_sems, start, wait


def _hosted(body, comm, *, grid, in_specs, out_specs, out_shape, scratch_shapes, sem, name, ins):
    if comm is None:
        return pl.pallas_call(body, grid=grid, in_specs=in_specs, out_specs=out_specs, out_shape=out_shape,
                              scratch_shapes=scratch_shapes, name=name, compiler_params=_cp(*sem))(*ins)
    n_in, n_out, n_scr = len(in_specs), len(out_specs), len(scratch_shapes)
    nci, nco = len(comm.ins), len(comm.out_shapes)

    def wrapped(*refs):
        pos = [0]

        def take(cnt):
            pos[0] += cnt
            return refs[pos[0] - cnt:pos[0]]

        r_in, c_in, r_out, c_out, r_scr = take(n_in), take(nci), take(n_out), take(nco), take(n_scr)
        send_sems, recv_sems = take(2)
        ids = [pl.program_id(ax) for ax in range(len(grid))]
        first, last = ids[0] == 0, ids[0] == grid[0] - 1
        for ax in range(1, len(grid)):
            first = jnp.logical_and(first, ids[ax] == 0)
            last = jnp.logical_and(last, ids[ax] == grid[ax] - 1)

        @pl.when(first)
        def _():
            comm.start(c_in, c_out, send_sems, recv_sems)

        body(*r_in, *r_out, *r_scr)

        @pl.when(last)
        def _():
            comm.wait(c_in, c_out, send_sems, recv_sems)

    outs = pl.pallas_call(
        wrapped, grid=grid, in_specs=list(in_specs) + [ANY] * nci, out_specs=list(out_specs) + [ANY] * nco,
        out_shape=list(out_shape) + comm.out_shapes,
        scratch_shapes=list(scratch_shapes) + [pltpu.SemaphoreType.DMA((comm.n_sems,))] * 2, name=name,
        compiler_params=pltpu.CompilerParams(dimension_semantics=sem, vmem_limit_bytes=V7X_VMEM_LIMIT_BYTES,
                                             has_side_effects=True),
    )(*ins, *comm.ins)
    return list(outs[:n_out]), list(outs[n_out:])


def _run_comm(comm, name):
    def body(*refs):
        nci, nco = len(comm.ins), len(comm.out_shapes)
        c_in, c_out, send_sems, recv_sems = refs[:nci], refs[nci:nci + nco], refs[-2], refs[-1]
        comm.start(c_in, c_out, send_sems, recv_sems)
        comm.wait(c_in, c_out, send_sems, recv_sems)

    return pl.pallas_call(
        body, in_specs=[ANY] * len(comm.ins), out_specs=[ANY] * len(comm.out_shapes), out_shape=comm.out_shapes,
        scratch_shapes=[pltpu.SemaphoreType.DMA((comm.n_sems,))] * 2, name=name,
        compiler_params=pltpu.CompilerParams(has_side_effects=True),
    )(*comm.ins)


def _mm(a, b, *, m, n, k, name, tm=1024, tn=1024, tk=512, ta=False, tb=False, b_stk=None, b_s0=0,
        o_stk=None, o_s0=0, o_slots=None, o_buf=None, bias=None, res=None, out_dtype=F32):
    tm, tn, tk = _tile(m, tm), _tile(n, tn), _tile(k, tk)
    gi, gj, gk = m // tm, n // tn, k // tk
    nsl = 1
    a_spec = pl.BlockSpec((tk, tm), lambda j, i, kk: (kk, i)) if ta else pl.BlockSpec((tm, tk), lambda j, i, kk: (i, kk))
    if b_stk is None:
        b_spec = pl.BlockSpec((tn, tk), lambda j, i, kk: (j, kk)) if tb else pl.BlockSpec((tk, tn), lambda j, i, kk: (kk, j))
    elif tb and tk > b_stk:
        assert tk % b_stk == 0 and b_s0 % (tk // b_stk) == 0 and not ta
        nsl = tk // b_stk
        b_spec = pl.BlockSpec((nsl, tn, b_stk), lambda j, i, kk: (b_s0 // nsl + kk, j, 0))
    elif tb:
        assert b_stk % tk == 0
        per = b_stk // tk
        b_spec = pl.BlockSpec((None, tn, tk), lambda j, i, kk: (b_s0 + kk // per, j, kk % per))
    else:
        assert b_stk % tn == 0
        per = b_stk // tn
        b_spec = pl.BlockSpec((None, tk, tn), lambda j, i, kk: (b_s0 + j // per, kk, j % per))
    ins, in_specs = [a, b], [a_spec, b_spec]
    if bias is not None:
        ins.append(bias)
        in_specs.append(pl.BlockSpec((1, tn), lambda j, i, kk: (0, j)))
    if res is not None:
        ins.append(res)
        in_specs.append(pl.BlockSpec((tm, tn), lambda j, i, kk: (i, j)))
    aliases = {}
    if o_stk is None:
        out_shape = jax.ShapeDtypeStruct((m, n), out_dtype)
        o_spec = pl.BlockSpec((tm, tn), lambda j, i, kk: (i, j))
    else:
        assert o_stk % tn == 0
        pero = o_stk // tn
        out_shape = jax.ShapeDtypeStruct((o_slots, m, o_stk), out_dtype)
        o_spec = pl.BlockSpec((None, tm, tn), lambda j, i, kk: (o_s0 + j // pero, i, j % pero))
        if o_buf is not None:
            aliases = {len(ins): 0}
            ins.append(o_buf)
            in_specs.append(ANY)
    has_bias, has_res, has_buf = bias is not None, res is not None, o_buf is not None
    dn = (((0 if ta else 1,), (1 if tb else 0,)), ((), ()))

    def body(*refs):
        a_ref, b_ref = refs[0], refs[1]
        pos = 2
        bias_ref = refs[pos] if has_bias else None
        pos += has_bias
        res_ref = refs[pos] if has_res else None
        pos += has_res + has_buf
        o_ref = refs[pos]
        if nsl == 1:
            p = lax.dot_general(a_ref[...].astype(BF16), b_ref[...].astype(BF16), dn, preferred_element_type=F32)
        else:
            p = sum(lax.dot_general(a_ref[:, s * b_stk:(s + 1) * b_stk].astype(BF16), b_ref[s].astype(BF16), dn,
                                    preferred_element_type=F32) for s in range(nsl))

        def finish(acc):
            if has_bias:
                acc = acc + bias_ref[...]
            if has_res:
                acc = acc + res_ref[...]
            o_ref[...] = acc.astype(o_ref.dtype)

        if gk == 1:
            finish(p)
        else:
            acc_ref = refs[pos + 1]
            kk = pl.program_id(2)

            @pl.when(kk == 0)
            def _():
                acc_ref[...] = p

            @pl.when(kk > 0)
            def _():
                acc_ref[...] += p

            @pl.when(kk == gk - 1)
            def _():
                finish(acc_ref[...])

    return pl.pallas_call(
        body, grid=(gj, gi, gk), in_specs=in_specs, out_specs=o_spec, out_shape=out_shape,
        scratch_shapes=[pltpu.VMEM((tm, tn), F32)] if gk > 1 else [],
        input_output_aliases=aliases, name=name,
        compiler_params=_cp("parallel", "parallel", "arbitrary"),
    )(*ins)


def _rms_fwd(x, g, name):
    n, d = x.shape
    tm = _tile(n, 512)

    def body(x_ref, g_ref, h_ref):
        xv = x_ref[...]
        r = lax.rsqrt(jnp.mean(xv * xv, axis=-1, keepdims=True) + EPS)
        h_ref[...] = (xv * r * g_ref[...]).astype(h_ref.dtype)

    return pl.pallas_call(
        body, grid=(n // tm,),
        in_specs=[pl.BlockSpec((tm, d), lambda i: (i, 0)), pl.BlockSpec((1, d), lambda i: (0, 0))],
        out_specs=pl.BlockSpec((tm, d), lambda i: (i, 0)),
        out_shape=jax.ShapeDtypeStruct((n, d), BF16), name=name, compiler_params=_cp("arbitrary"),
    )(x, g)


def _rms_bwd(x, dh, g, dres, name, colsum=False):
    n, d = x.shape
    tm = _tile(n, 512)

    def body(x_ref, dh_ref, g_ref, dres_ref, dx_ref, dg_ref, *rest):
        i = pl.program_id(0)
        xv, dhv = x_ref[...], dh_ref[...]
        r = lax.rsqrt(jnp.mean(xv * xv, axis=-1, keepdims=True) + EPS)
        xn = xv * r
        dxn = dhv * g_ref[...]
        dx_ref[...] = dres_ref[...] + r * (dxn - xn * jnp.mean(dxn * xn, axis=-1, keepdims=True))
        dg = jnp.sum(dhv * xn, axis=0, keepdims=True)

        @pl.when(i == 0)
        def _():
            dg_ref[...] = jnp.zeros_like(dg_ref)
            if colsum:
                rest[0][...] = jnp.zeros_like(rest[0])

        dg_ref[...] += dg
        if colsum:
            rest[0][...] += jnp.sum(dres_ref[...], axis=0, keepdims=True)

    row = pl.BlockSpec((tm, d), lambda i: (i, 0))
    vec = pl.BlockSpec((1, d), lambda i: (0, 0))
    out_shape = [jax.ShapeDtypeStruct((n, d), F32), jax.ShapeDtypeStruct((1, d), F32)]
    out_specs = [row, vec]
    if colsum:
        out_shape.append(jax.ShapeDtypeStruct((1, d), F32))
        out_specs.append(vec)
    return pl.pallas_call(
        body, grid=(n // tm,), in_specs=[row, row, vec, row], out_specs=out_specs, out_shape=out_shape,
        name=name, compiler_params=_cp("arbitrary"),
    )(x, dh, g, dres)


def _loss(y, tgt, name):
    n, d = y.shape
    tm = _tile(n, 512)

    def body(y_ref, t_ref, dy_ref, l_ref):
        i = pl.program_id(0)
        e = y_ref[...] - t_ref[...]
        dy_ref[...] = e / d
        part = 0.5 * jnp.sum(jnp.mean(e * e, axis=-1, keepdims=True), axis=0, keepdims=True)

        @pl.when(i == 0)
        def _():
            l_ref[...] = jnp.zeros_like(l_ref)

        l_ref[...] += part

    row = pl.BlockSpec((tm, d), lambda i: (i, 0))
    return pl.pallas_call(
        body, grid=(n // tm,), in_specs=[row, row],
        out_specs=[row, pl.BlockSpec((1, 1), lambda i: (0, 0))],
        out_shape=[jax.ShapeDtypeStruct((n, d), F32), jax.ShapeDtypeStruct((1, 1), F32)],
        name=name, compiler_params=_cp("arbitrary"),
    )(y, tgt)


def _ffn_specs(n, f, tm, tc, seq):
    hb = FFN_HALO
    cur = pl.BlockSpec((tm, tc), lambda j, i: (i, j))
    prev = pl.BlockSpec((hb, tc), lambda j, i: (jnp.maximum(i * (tm // hb) - 1, 0), j))
    nxt = pl.BlockSpec((hb, tc), lambda j, i: (jnp.minimum((i + 1) * (tm // hb), n // hb - 1), j))
    taps = pl.BlockSpec((3, tc), lambda j, i: (0, j))
    vec = pl.BlockSpec((1, tc), lambda j, i: (0, j))
    return cur, prev, nxt, taps, vec


def _ffn_glu_down(uv, ug, wv, wg, bv, bg, w_down, res, seq, name, comm=None):
    n, f = uv.shape
    d = w_down.shape[1]
    tm, tc = _tile(seq, FFN_DOWN_ROWS), _tile(f, FFN_DOWN_COLS)
    tps = seq // tm
    nj = f // tc
    hb = FFN_HALO
    cur = pl.BlockSpec((tm, tc), lambda i, j: (i, j))
    prev = pl.BlockSpec((hb, tc), lambda i, j: (jnp.maximum(i * (tm // hb) - 1, 0), j))
    taps = pl.BlockSpec((3, tc), lambda i, j: (0, j))
    vec = pl.BlockSpec((1, tc), lambda i, j: (0, j))
    wblk = pl.BlockSpec((tc, d), lambda i, j: (j, 0))
    row = pl.BlockSpec((tm, d), lambda i, j: (i, 0))

    def body(uvp, uvc, ugp, ugc, wv_ref, wg_ref, bv_ref, bg_ref, wd_ref, res_ref, a_ref, x_ref, acc_ref):
        first = (pl.program_id(0) % tps) == 0
        j = pl.program_id(1)

        def conv(p_ref, c_ref, w_ref, b_ref):
            xs = jnp.concatenate([jnp.where(first, 0.0, p_ref[...]), c_ref[...]], axis=0)
            w = w_ref[...]
            y = w[2:3] * xs + w[1:2] * _roll(xs, 1) + w[0:1] * _roll(xs, 2)
            return y[FFN_HALO:] + b_ref[...]

        val = conv(uvp, uvc, wv_ref, bv_ref)
        gate = conv(ugp, ugc, wg_ref, bg_ref)
        a = (gate * _sig(gate) * val).astype(BF16)
        a_ref[...] = a
        p = jnp.dot(a, wd_ref[...], preferred_element_type=F32)

        @pl.when(j == 0)
        def _():
            acc_ref[...] = res_ref[...] + p

        @pl.when(j > 0)
        def _():
            acc_ref[...] += p

        @pl.when(j == nj - 1)
        def _():
            x_ref[...] = acc_ref[...]

    return _hosted(
        body, comm, grid=(n // tm, nj), in_specs=[prev, cur, prev, cur, taps, taps, vec, vec, wblk, row],
        out_specs=[cur, row], out_shape=[jax.ShapeDtypeStruct((n, f), BF16), jax.ShapeDtypeStruct((n, d), F32)],
        scratch_shapes=[pltpu.VMEM((tm, d), F32)], sem=("parallel", "arbitrary"), name=name,
        ins=(uv, uv, ug, ug, wv, wg, bv, bg, w_down, res))


def _ffn_glu_bwd(uv, ug, dx, w_down, wv, wg, bv, bg, seq, name, comm=None):
    n, f = uv.shape
    d = dx.shape[1]
    tm, tc = _tile(seq, FFN_ROWS), _tile(f, FFN_COLS)
    tps = seq // tm
    hb = FFN_HALO
    ext = tm + hb
    cur, prev, nxt, taps, vec = _ffn_specs(n, f, tm, tc, seq)
    dx_cur = pl.BlockSpec((tm, d), lambda j, i: (i, 0))
    dx_nxt = pl.BlockSpec((hb, d), lambda j, i: (jnp.minimum((i + 1) * (tm // hb), n // hb - 1), 0))
    wblk = pl.BlockSpec((tc, d), lambda j, i: (j, 0))

    def body(uvp, uvc, uvn, ugp, ugc, ugn, dx_c, dx_n, wd_ref, wv_ref, wg_ref, bv_ref, bg_ref,
             duv_ref, dug_ref, dwv_ref, dwg_ref, dbv_ref, dbg_ref):
        i = pl.program_id(1)
        first = (i % tps) == 0
        last = (i % tps) == tps - 1
        dx_e = jnp.concatenate([dx_c[...], jnp.where(last, 0.0, dx_n[...])], axis=0).astype(BF16)
        da_e = lax.dot_general(dx_e, wd_ref[...], _NT, preferred_element_type=F32)

        def taps3(p_ref, c_ref, n_ref):
            xs = jnp.concatenate([jnp.where(first, 0.0, p_ref[...]), c_ref[...], n_ref[...]], axis=0)
            return xs, _roll(xs, 1), _roll(xs, 2)

        xv, xg = taps3(uvp, uvc, uvn), taps3(ugp, ugc, ugn)
        wv_, wg_ = wv_ref[...], wg_ref[...]

        def conv(xs, w, b_ref):
            return (w[2:3] * xs[0] + w[1:2] * xs[1] + w[0:1] * xs[2])[hb:] + b_ref[...]

        val, gate = conv(xv, wv_, bv_ref), conv(xg, wg_, bg_ref)
        sg = _sig(gate)
        dval = da_e * (gate * sg)
        dgate = da_e * val * (sg * (1.0 + gate * (1.0 - sg)))

        def conv_t(dv, w):
            return (w[2:3] * dv + w[1:2] * _roll(dv, ext - 1) + w[0:1] * _roll(dv, ext - 2))[:tm]

        duv_ref[...] = conv_t(dval, wv_).astype(duv_ref.dtype)
        dug_ref[...] = conv_t(dgate, wg_).astype(dug_ref.dtype)

        def tap_grads(d_own, xs):
            return jnp.concatenate(
                [jnp.sum(d_own * xs[2 - kk][hb:hb + tm], axis=0, keepdims=True) for kk in range(3)], axis=0)

        dv_own, dg_own = dval[:tm], dgate[:tm]

        @pl.when(i == 0)
        def _():
            for r in (dwv_ref, dwg_ref, dbv_ref, dbg_ref):
                r[...] = jnp.zeros_like(r)

        dwv_ref[...] += tap_grads(dv_own, xv)
        dwg_ref[...] += tap_grads(dg_own, xg)
        dbv_ref[...] += jnp.sum(dv_own, axis=0, keepdims=True)
        dbg_ref[...] += jnp.sum(dg_own, axis=0, keepdims=True)

    return _hosted(
        body, comm, grid=(f // tc, n // tm),
        in_specs=[prev, cur, nxt, prev, cur, nxt, dx_cur, dx_nxt, wblk, taps, taps, vec, vec],
        out_specs=[cur, cur, taps, taps, vec, vec],
        out_shape=[jax.ShapeDtypeStruct((n, f), BF16), jax.ShapeDtypeStruct((n, f), BF16),
                   jax.ShapeDtypeStruct((3, f), F32), jax.ShapeDtypeStruct((3, f), F32),
                   jax.ShapeDtypeStruct((1, f), F32), jax.ShapeDtypeStruct((1, f), F32)],
        scratch_shapes=[], sem=("parallel", "arbitrary"), name=name,
        ins=(uv, uv, uv, ug, ug, ug, dx, dx, w_down, wv, wg, bv, bg))


def _conf_specs(n, d, tm):
    hb = CONV_HALO
    cur = pl.BlockSpec((tm, d), lambda i: (i, 0))
    prev = pl.BlockSpec((hb, d), lambda i: (jnp.maximum(i * (tm // hb) - 1, 0), 0))
    nxt = pl.BlockSpec((hb, d), lambda i: (jnp.minimum((i + 1) * (tm // hb), n // hb - 1), 0))
    taps = pl.BlockSpec((CONV_HALO, d), lambda i: (0, 0))
    vec = pl.BlockSpec((1, d), lambda i: (0, 0))
    return cur, prev, nxt, taps, vec


def _conf_fwd(pa, pg, w, wb, lng, lnb, seq, name, comm=None):
    n, d = pa.shape
    tm = _tile(seq, 256)
    tps = seq // tm
    hb = CONV_HALO
    cur, prev, _, taps, vec = _conf_specs(n, d, tm)

    def body(pap, pac, pgp, pgc, w_ref, wb_ref, lng_ref, lnb_ref, u_ref, s_ref, rot_ref):
        first = (pl.program_id(0) % tps) == 0
        a = jnp.concatenate([jnp.where(first, 0.0, pap[...]), pac[...]], axis=0)
        g = jnp.concatenate([jnp.where(first, 0.0, pgp[...]), pgc[...]], axis=0)
        z = a * _sig(g)
        for b in range(SUBLANES):
            rot_ref[b] = _roll(z, b)
        _conv_taps(rot_ref, w_ref, u_ref, tm, lambda sh: hb - (sh // SUBLANES) * SUBLANES)
        u = u_ref[...] + wb_ref[...]
        mu = jnp.mean(u, axis=-1, keepdims=True)
        uc = u - mu
        var = jnp.mean(uc * uc, axis=-1, keepdims=True)
        ul = uc * lax.rsqrt(var + EPS) * lng_ref[...] + lnb_ref[...]
        u_ref[...] = u
        s_ref[...] = (ul * _sig(ul)).astype(s_ref.dtype)

    return _hosted(
        body, comm, grid=(n // tm,), in_specs=[prev, cur, prev, cur, taps, vec, vec, vec],
        out_specs=[cur, cur],
        out_shape=[jax.ShapeDtypeStruct((n, d), F32), jax.ShapeDtypeStruct((n, d), BF16)],
        scratch_shapes=[pltpu.VMEM((SUBLANES, tm + hb, d), F32)], sem=("arbitrary",), name=name,
        ins=(pa, pa, pg, pg, w, wb, lng, lnb))


def _conf_bwd(u, ds, pa, pg, w, lng, lnb, seq, name, comm=None):
    n, d = u.shape
    tm = _tile(seq, 256)
    tps = seq // tm
    hb = CONV_HALO
    ext = tm + hb
    cur, prev, nxt, taps, vec = _conf_specs(n, d, tm)

    def body(uc_ref, un_ref, dsc_ref, dsn_ref, pap, pac, pgp, pgc, w_ref, lng_ref, lnb_ref,
             dpa_ref, dpg_ref, dw_ref, dwb_ref, dlng_ref, dlnb_ref, dba_ref, dbg_ref, rotz_ref, rotd_ref, dz_ref):
        i = pl.program_id(0)
        first = (i % tps) == 0
        last = (i % tps) == tps - 1

        @pl.when(i == 0)
        def _():
            for r in (dw_ref, dwb_ref, dlng_ref, dlnb_ref, dba_ref, dbg_ref):
                r[...] = jnp.zeros_like(r)

        ue = jnp.concatenate([uc_ref[...], un_ref[...]], axis=0)
        dse = jnp.concatenate([dsc_ref[...], jnp.where(last, 0.0, dsn_ref[...])], axis=0)
        mu = jnp.mean(ue, axis=-1, keepdims=True)
        cen = ue - mu
        r = lax.rsqrt(jnp.mean(cen * cen, axis=-1, keepdims=True) + EPS)
        xn = cen * r
        ul = xn * lng_ref[...] + lnb_ref[...]
        sg = _sig(ul)
        dul = dse * (sg * (1.0 + ul * (1.0 - sg)))
        dun = dul * lng_ref[...]
        du = r * (dun - jnp.mean(dun, axis=-1, keepdims=True) - xn * jnp.mean(dun * xn, axis=-1, keepdims=True))
        dlng_ref[...] += jnp.sum((dul * xn)[:tm], axis=0, keepdims=True)
        dlnb_ref[...] += jnp.sum(dul[:tm], axis=0, keepdims=True)
        dwb_ref[...] += jnp.sum(du[:tm], axis=0, keepdims=True)
        for b in range(SUBLANES):
            rotd_ref[b] = _roll(du, ext - b)
        _conv_taps(rotd_ref, w_ref, dz_ref, tm, lambda sh: (sh // SUBLANES) * SUBLANES)
        dz = dz_ref[...]

        a = jnp.concatenate([jnp.where(first, 0.0, pap[...]), pac[...]], axis=0)
        g = jnp.concatenate([jnp.where(first, 0.0, pgp[...]), pgc[...]], axis=0)
        sgg = _sig(g)
        z = a * sgg
        for b in range(SUBLANES):
            rotz_ref[b] = _roll(z, b)
        _tap_grads(rotz_ref, rotd_ref, dw_ref, tm, hb)

        a_c, sg_c = a[hb:], sgg[hb:]
        da = dz * sg_c
        dg = dz * a_c * sg_c * (1.0 - sg_c)
        dpa_ref[...] = da.astype(dpa_ref.dtype)
        dpg_ref[...] = dg.astype(dpg_ref.dtype)
        dba_ref[...] += jnp.sum(da, axis=0, keepdims=True)
        dbg_ref[...] += jnp.sum(dg, axis=0, keepdims=True)

    vshape = jax.ShapeDtypeStruct((1, d), F32)
    return _hosted(
        body, comm, grid=(n // tm,),
        in_specs=[cur, nxt, cur, nxt, prev, cur, prev, cur, taps, vec, vec],
        out_specs=[cur, cur, taps, vec, vec, vec, vec, vec],
        out_shape=[jax.ShapeDtypeStruct((n, d), BF16), jax.ShapeDtypeStruct((n, d), BF16),
                   jax.ShapeDtypeStruct((CONV_HALO, d), F32), vshape, vshape, vshape, vshape, vshape],
        scratch_shapes=[pltpu.VMEM((SUBLANES, ext, d), F32), pltpu.VMEM((SUBLANES, ext, d), F32),
                        pltpu.VMEM((tm, d), F32)],
        sem=("arbitrary",), name=name, ins=(u, u, ds, ds, pa, pa, pg, pg, w, lng, lnb))


def _pool_specs(n, d, tm, gd):
    hb = POOL_HALO
    cur = pl.BlockSpec((tm, d), lambda i: (i, 0))
    prev = pl.BlockSpec((hb, d), lambda i: (jnp.maximum(i * (tm // hb) - 1, 0), 0))
    nxt = pl.BlockSpec((hb, d), lambda i: (jnp.minimum((i + 1) * (tm // hb), n // hb - 1), 0))
    wsp = pl.BlockSpec((len(POOL_WINDOWS), gd, gd), lambda i: (0, 0, 0))
    vec = pl.BlockSpec((1, d), lambda i: (0, 0))
    return cur, prev, nxt, wsp, vec


def _pool_fwd(x, g, w, b, sc, seq, name):
    n, d = x.shape
    gd = d // len(POOL_WINDOWS)
    tm = _tile(seq, 256)
    tps = seq // tm
    hb = POOL_HALO
    cur, prev, _, wsp, vec = _pool_specs(n, d, tm, gd)

    def body(xp, xc, g_ref, w_ref, b_ref, sc_ref, x1_ref, p_ref):
        i = pl.program_id(0)
        first = (i % tps) == 0
        xe = jnp.concatenate([jnp.where(first, 0.0, xp[...]), xc[...]], axis=0)
        r = lax.rsqrt(jnp.mean(xe * xe, axis=-1, keepdims=True) + EPS)
        h = xe * r * g_ref[...]
        t = ((i % tps) * tm + lax.broadcasted_iota(jnp.int32, (tm, 1), 0) + 1).astype(F32)
        ys = []
        for gi, win in enumerate(POOL_WINDOWS):
            hg = h[:, gi * gd:(gi + 1) * gd]
            s, sh = hg, 1
            while sh < win:
                s = s + _roll(s, sh)
                sh *= 2
            p = (s[hb:] / jnp.minimum(t, float(win)) - hg[hb:]).astype(BF16)
            p_ref[:, gi * gd:(gi + 1) * gd] = p
            ys.append(jnp.dot(p, w_ref[gi], preferred_element_type=F32))
        y = jnp.concatenate(ys, axis=1) + b_ref[...]
        x1_ref[...] = xc[...] + y * sc_ref[...]

    return pl.pallas_call(
        body, grid=(n // tm,), in_specs=[prev, cur, vec, wsp, vec, vec], out_specs=[cur, cur],
        out_shape=[jax.ShapeDtypeStruct((n, d), F32), jax.ShapeDtypeStruct((n, d), BF16)],
        name=name, compiler_params=_cp("arbitrary"),
    )(x, x, g, w, b, sc)


def _pool_bwd(dx1, x, p, g, w, b, sc, seq, name):
    n, d = x.shape
    ng = len(POOL_WINDOWS)
    gd = d // ng
    tm = _tile(seq, 256)
    tps = seq // tm
    hb = POOL_HALO
    ext = tm + hb
    cur, _, nxt, wsp, vec = _pool_specs(n, d, tm, gd)

    def body(dc_ref, dn_ref, x_ref, p_ref, g_ref, w_ref, b_ref, sc_ref, dx_ref, dg_ref, dw_ref, db_ref, dsc_ref):
        i = pl.program_id(0)
        last = (i % tps) == tps - 1

        @pl.when(i == 0)
        def _():
            for r_ in (dg_ref, dw_ref, db_ref, dsc_ref):
                r_[...] = jnp.zeros_like(r_)

        dxc = dc_ref[...]
        dxe = jnp.concatenate([dxc, jnp.where(last, 0.0, dn_ref[...])], axis=0)
        dyg = dxe * sc_ref[...]
        t = ((i % tps) * tm + lax.broadcasted_iota(jnp.int32, (ext, 1), 0) + 1).astype(F32)
        dhs = []
        for gi, win in enumerate(POOL_WINDOWS):
            sl = slice(gi * gd, (gi + 1) * gd)
            dyb = dyg[:, sl].astype(BF16)
            wg = w_ref[gi]
            dp = lax.dot_general(dyb, wg, (((1,), (1,)), ((), ())), preferred_element_type=F32)
            s, sh = dp / jnp.minimum(t, float(win)), 1
            while sh < win:
                s = s + _roll(s, ext - sh)
                sh *= 2
            dhs.append((s - dp)[:tm])
            pg = p_ref[:, sl]
            dw_ref[gi] += lax.dot_general(pg, dyb[:tm], (((0,), (0,)), ((), ())), preferred_element_type=F32)
            ypre = jnp.dot(pg, wg, preferred_element_type=F32) + b_ref[:, sl]
            dsc_ref[:, sl] += jnp.sum(dxc[:, sl] * ypre, axis=0, keepdims=True)
            db_ref[:, sl] += jnp.sum(dyg[:tm, sl], axis=0, keepdims=True)
        dh = jnp.concatenate(dhs, axis=1)
        xv = x_ref[...]
        r = lax.rsqrt(jnp.mean(xv * xv, axis=-1, keepdims=True) + EPS)
        xn = xv * r
        dxn = dh * g_ref[...]
        dx_ref[...] = dxc + r * (dxn - xn * jnp.mean(dxn * xn, axis=-1, keepdims=True))
        dg_ref[...] += jnp.sum(dh * xn, axis=0, keepdims=True)

    vshape = jax.ShapeDtypeStruct((1, d), F32)
    return pl.pallas_call(
        body, grid=(n // tm,), in_specs=[cur, nxt, cur, cur, vec, wsp, vec, vec],
        out_specs=[cur, vec, wsp, vec, vec],
        out_shape=[jax.ShapeDtypeStruct((n, d), F32), vshape, jax.ShapeDtypeStruct((ng, gd, gd), F32), vshape, vshape],
        name=name, compiler_params=_cp("arbitrary"),
    )(dx1, dx1, x, p, g, w, b, sc)


def _head_maps(d):
    hd = lax.broadcasted_iota(jnp.int32, (d, LANES), 0) // HEAD_DIM
    col = lax.broadcasted_iota(jnp.int32, (d, LANES), 1)
    gm = (hd == col).astype(BF16)
    hd_t = lax.broadcasted_iota(jnp.int32, (LANES, d), 1) // HEAD_DIM
    row = lax.broadcasted_iota(jnp.int32, (LANES, d), 0)
    gt = (hd_t == row).astype(BF16)
    return gm, gt


def _dot_split(v, onehot):
    hi = v.astype(BF16)
    lo = (v - hi.astype(F32)).astype(BF16)
    return jnp.dot(hi, onehot, preferred_element_type=F32) + jnp.dot(lo, onehot, preferred_element_type=F32)


def _bias_placement(nh):
    pq = np.zeros((3 * LANES, nh * HEAD_DIM), np.float32)
    pk = np.zeros((3 * LANES, nh * HEAD_DIM), np.float32)
    oq = np.zeros((1, nh * HEAD_DIM), np.float32)
    ok = np.zeros((1, nh * HEAD_DIM), np.float32)
    for h in range(nh):
        for piece in range(3):
            pq[piece * LANES + h, h * HEAD_DIM + piece] = 1.0
            pk[piece * LANES + h, h * HEAD_DIM + 3 + piece] = -1.0
            oq[0, h * HEAD_DIM + 3 + piece] = 1.0
            ok[0, h * HEAD_DIM + piece] = 1.0
    return jnp.asarray(pq, BF16), jnp.asarray(pk, BF16), jnp.asarray(oq), jnp.asarray(ok)


def _fox_prep_fwd(qkv, fl, bf, qg, kg, seq, name):
    n, d3 = qkv.shape
    d = d3 // 3
    nh = d // HEAD_DIM
    tm = _tile(seq, 256)
    tps = seq // tm
    scale = 1.0 / math.sqrt(HEAD_DIM)
    pq, pk, oq, ok = _bias_placement(nh)

    def body(qkv_ref, fl_ref, bf_ref, qg_ref, kg_ref, pq_ref, pk_ref, oq_ref, ok_ref, q_ref, k_ref, v_ref, carry):
        first = (pl.program_id(0) % tps) == 0
        gm, gt = _head_maps(d)

        def head_norm(xr, gain):
            r = lax.rsqrt(_dot_split(xr * xr, gm) / HEAD_DIM + EPS)
            return xr * _dot_split(r, gt) * gain

        qs = (head_norm(qkv_ref[:, :d], qg_ref[...]).astype(BF16).astype(F32) * scale).astype(BF16)
        kn = head_norm(qkv_ref[:, d:2 * d], kg_ref[...]).astype(BF16)
        v_ref[...] = qkv_ref[:, 2 * d:].astype(BF16)
        z = fl_ref[...] + bf_ref[...]
        logf = jnp.minimum(z, 0.0) - jnp.log1p(jnp.exp(-jnp.abs(z)))
        tri = (lax.broadcasted_iota(jnp.int32, (tm, tm), 0) >= lax.broadcasted_iota(jnp.int32, (tm, tm), 1)).astype(F32)

        @pl.when(first)
        def _():
            carry[...] = jnp.zeros_like(carry)

        c = jnp.dot(tri, logf, precision=HI, preferred_element_type=F32) + carry[...]
        carry[...] = c[tm - 1:tm, :]
        c1 = c.astype(BF16)
        r1 = c - c1.astype(F32)
        c2 = r1.astype(BF16)
        c3 = (r1 - c2.astype(F32)).astype(BF16)
        pieces = jnp.concatenate([c1, c2, c3], axis=1)
        eq = (jnp.dot(pieces, pq_ref[...], preferred_element_type=F32) + oq_ref[...]).astype(BF16)
        ek = (jnp.dot(pieces, pk_ref[...], preferred_element_type=F32) + ok_ref[...]).astype(BF16)
        for h in range(nh):
            lo, hi = h * HEAD_DIM, (h + 1) * HEAD_DIM
            q_ref[:, 2 * lo:2 * lo + HEAD_DIM] = qs[:, lo:hi]
            q_ref[:, 2 * lo + HEAD_DIM:2 * hi] = eq[:, lo:hi]
            k_ref[:, 2 * lo:2 * lo + HEAD_DIM] = kn[:, lo:hi]
            k_ref[:, 2 * lo + HEAD_DIM:2 * hi] = ek[:, lo:hi]

    row = lambda w: pl.BlockSpec((tm, w), lambda i: (i, 0))
    vec = lambda w: pl.BlockSpec((1, w), lambda i: (0, 0))
    full = lambda a: pl.BlockSpec(a.shape, lambda i: (0, 0))
    return pl.pallas_call(
        body, grid=(n // tm,),
        in_specs=[row(d3), row(LANES), vec(LANES), vec(d), vec(d), full(pq), full(pk), full(oq), full(ok)],
        out_specs=[row(2 * d), row(2 * d), row(d)],
        out_shape=[jax.ShapeDtypeStruct((n, 2 * d), BF16)] * 2 + [jax.ShapeDtypeStruct((n, d), BF16)],
        scratch_shapes=[pltpu.VMEM((1, LANES), F32)], name=name, compiler_params=_cp("arbitrary"),
    )(qkv, fl, bf, qg, kg, pq, pk, oq, ok)


def _fox_prep_bwd(qkv, dq, dk, dv, dc1, dc2, fl, bf, qg, kg, seq, name, comm=None):
    n, d3 = qkv.shape
    d = d3 // 3
    tm = _tile(seq, 256)
    tps = seq // tm
    nt = n // tm

    def body(qkv_ref, dq_ref, dk_ref, dv_ref, dc1_ref, dc2_ref, fl_ref, bf_ref, qg_ref, kg_ref,
             dqkv_ref, dfl_ref, dqg_ref, dkg_ref, dbf_ref, carry):
        i = pl.program_id(0)
        tile = nt - 1 - i
        last = (tile % tps) == tps - 1
        gm, gt = _head_maps(d)

        @pl.when(i == 0)
        def _():
            for r_ in (dqg_ref, dkg_ref, dbf_ref):
                r_[...] = jnp.zeros_like(r_)

        @pl.when(last)
        def _():
            carry[...] = jnp.zeros_like(carry)

        def head_norm_bwd(xr, dy, gain, dgain_ref):
            rf = _dot_split(lax.rsqrt(_dot_split(xr * xr, gm) / HEAD_DIM + EPS), gt)
            xn = xr * rf
            dgain_ref[...] += jnp.sum(dy * xn, axis=0, keepdims=True)
            dyg = dy * gain
            mean = _dot_split(dyg * xn, gm) / HEAD_DIM
            return rf * (dyg - xn * _dot_split(mean, gt))

        dqkv_ref[:, :d] = head_norm_bwd(qkv_ref[:, :d], dq_ref[...], qg_ref[...], dqg_ref).astype(BF16)
        dqkv_ref[:, d:2 * d] = head_norm_bwd(qkv_ref[:, d:2 * d], dk_ref[...], kg_ref[...], dkg_ref).astype(BF16)
        dqkv_ref[:, 2 * d:] = dv_ref[...].astype(BF16)

        dc = dc1_ref[...] + dc2_ref[...]
        tri = (lax.broadcasted_iota(jnp.int32, (tm, tm), 0) <= lax.broadcasted_iota(jnp.int32, (tm, tm), 1)).astype(F32)
        dlog = jnp.dot(tri, dc, precision=HI, preferred_element_type=F32) + carry[...]
        carry[...] = dlog[0:1, :]
        dfl = dlog * (1.0 - _sig(fl_ref[...] + bf_ref[...]))
        dfl_ref[...] = dfl.astype(BF16)
        dbf_ref[...] += jnp.sum(dfl, axis=0, keepdims=True)

    row = lambda w: pl.BlockSpec((tm, w), lambda i: (nt - 1 - i, 0))
    vec = lambda w: pl.BlockSpec((1, w), lambda i: (0, 0))
    return _hosted(
        body, comm, grid=(nt,),
        in_specs=[row(d3), row(d), row(d), row(d), row(LANES), row(LANES), row(LANES), vec(LANES), vec(d), vec(d)],
        out_specs=[row(d3), row(LANES), vec(d), vec(d), vec(LANES)],
        out_shape=[jax.ShapeDtypeStruct((n, d3), BF16), jax.ShapeDtypeStruct((n, LANES), BF16),
                   jax.ShapeDtypeStruct((1, d), F32), jax.ShapeDtypeStruct((1, d), F32),
                   jax.ShapeDtypeStruct((1, LANES), F32)],
        scratch_shapes=[pltpu.VMEM((1, LANES), F32)], sem=("arbitrary",), name=name,
        ins=(qkv, dq, dk, dv, dc1, dc2, fl, bf, qg, kg))


def _attn_specs(bsz, seq, t):
    nb = seq // t
    blk = lambda w: pl.BlockSpec((t, w), lambda b, h, i: (b * nb + i, h))
    full = lambda w: pl.BlockSpec((seq, w), lambda b, h, i: (b, h))
    col = pl.BlockSpec((None, None, t, 2), lambda b, h, i: (b, h, i, 0))
    rows = pl.BlockSpec((None, None, nb, 2, t), lambda b, h, i: (b, h, 0, 0, 0))
    return nb, blk, full, col, rows


_NT = (((1,), (1,)), ((), ()))
ATTN_TILE = 512


def _head_lanes(t, hh):
    lane = lax.broadcasted_iota(jnp.int32, (t, LANES), 1)
    return (lane < HEAD_DIM) if hh == 0 else (lane >= HEAD_DIM)


def _flash_fwd(qa, ka, v, bsz, seq, name):
    n, d = v.shape
    hp = d // LANES
    t = _tile(seq, ATTN_TILE)
    nb, blk, full, col, _ = _attn_specs(bsz, seq, t)

    def body(q_ref, k_ref, v_ref, o_ref, lse_ref):
        i = pl.program_id(2)
        causal = lax.broadcasted_iota(jnp.int32, (t, t), 0) >= lax.broadcasted_iota(jnp.int32, (t, t), 1)

        def block(j, carry, masked):
            rs = pl.ds(pl.multiple_of(j * t, t), t)
            vj = v_ref[rs, :]
            out = []
            for hh in range(2):
                m, l, acc = carry[hh]
                hs = slice(hh * LANES, (hh + 1) * LANES)
                sc = lax.dot_general(q_ref[:, hs], k_ref[rs, hs], _NT, preferred_element_type=F32)
                if masked:
                    sc = jnp.where(causal, sc, NEG)
                mn = jnp.maximum(m, jnp.max(sc, axis=-1, keepdims=True))
                p = jnp.exp(sc - mn)
                al = jnp.exp(m - mn)
                l = al * l + jnp.sum(p, axis=-1, keepdims=True)
                acc = al * acc + jnp.dot(p.astype(BF16), vj, preferred_element_type=F32)
                out.append((mn, l, acc))
            return tuple(out)

        init = tuple((jnp.full((t, 1), NEG, F32), jnp.zeros((t, 1), F32), jnp.zeros((t, LANES), F32))
                     for _ in range(2))
        carry = lax.fori_loop(0, i, lambda j, c: block(j, c, False), init)
        (m0, l0, a0), (m1, l1, a1) = block(i, carry, True)
        o_ref[...] = jnp.where(_head_lanes(t, 0), a0 / l0, a1 / l1)
        lse_ref[:, 0:1] = m0 + jnp.log(l0)
        lse_ref[:, 1:2] = m1 + jnp.log(l1)

    return pl.pallas_call(
        body, grid=(bsz, hp, nb), in_specs=[blk(2 * LANES), full(2 * LANES), full(LANES)],
        out_specs=[blk(LANES), col],
        out_shape=[jax.ShapeDtypeStruct((n, d), F32), jax.ShapeDtypeStruct((bsz, hp, seq, 2), F32)],
        name=name, compiler_params=_cp("parallel", "parallel", "arbitrary"),
    )(qa, ka, v)


def _flash_bwd(qa, ka, v, do, o, lse_row, bsz, seq, name):
    n, d = v.shape
    hp = d // LANES
    t = _tile(seq, ATTN_TILE)
    nb, blk, full, col, rows = _attn_specs(bsz, seq, t)
    scale = 1.0 / math.sqrt(HEAD_DIM)
    tn_ = (((0,), (0,)), ((), ()))

    def body(k_ref, v_ref, q_ref, do_ref, o_ref, lse_ref, dq_ref, dcc_ref, dk_ref, dv_ref, dck_ref, dqa, dl):
        j = pl.program_id(2)
        causal = lax.broadcasted_iota(jnp.int32, (t, t), 1) >= lax.broadcasted_iota(jnp.int32, (t, t), 0)
        heads = [_head_lanes(t, 0), _head_lanes(t, 1)]

        @pl.when(j == 0)
        def _():
            dqa[...] = jnp.zeros_like(dqa)
            ones = jnp.ones((8, LANES), F32)
            for ib in range(nb):
                rs = slice(ib * t, (ib + 1) * t)
                prod = do_ref[rs, :] * o_ref[rs, :]
                for hh in range(2):
                    dl[ib, hh:hh + 1, :] = lax.dot_general(ones, jnp.where(heads[hh], prod, 0.0), _NT, precision=HI,
                                                           preferred_element_type=F32)[0:1]

        vj = v_ref[...]

        def block(i, carry, masked):
            rs = pl.ds(pl.multiple_of(i * t, t), t)
            doi = do_ref[rs, :]
            dks, dvp = list(carry[:2]), carry[2]
            for hh in range(2):
                hs = slice(hh * LANES, (hh + 1) * LANES)
                kh, qi = k_ref[:, hs], q_ref[rs, hs]
                dom = jnp.where(heads[hh], doi, 0.0).astype(BF16)
                st = lax.dot_general(kh, qi, _NT, preferred_element_type=F32)
                if masked:
                    st = jnp.where(causal, st, NEG)
                pt = jnp.exp(st - lse_ref[i, hh:hh + 1, :])
                dvp = dvp + jnp.dot(pt.astype(BF16), dom, preferred_element_type=F32)
                dpt = lax.dot_general(vj, dom, _NT, preferred_element_type=F32)
                dsb = (pt * (dpt - dl[i, hh:hh + 1, :])).astype(BF16)
                dks[hh] = dks[hh] + jnp.dot(dsb, qi, preferred_element_type=F32)
                dqa[rs, hs] += lax.dot_general(dsb, kh, tn_, preferred_element_type=F32)
            return dks[0], dks[1], dvp

        zero = jnp.zeros((t, LANES), F32)
        carry = block(j, (zero, zero, zero), True)
        dk0, dk1, dvp = lax.fori_loop(j + 1, nb, lambda i, c: block(i, c, False), carry)
        dk_ref[...] = jnp.where(heads[0], dk0, pltpu.roll(dk1, HEAD_DIM, 1))
        dv_ref[...] = dvp
        dck_ref[:, 0:1] = -dk0[:, HEAD_DIM + 3:HEAD_DIM + 4]
        dck_ref[:, 1:2] = -dk1[:, HEAD_DIM + 3:HEAD_DIM + 4]

        @pl.when(j == nb - 1)
        def _():
            first = lax.broadcasted_iota(jnp.int32, (seq, LANES), 1) < HEAD_DIM
            dq_ref[...] = jnp.where(first, dqa[:, :LANES], pltpu.roll(dqa[:, LANES:], HEAD_DIM, 1)) * scale
            for hh in range(2):
                lo = hh * LANES + HEAD_DIM
                dcc_ref[:, hh:hh + 1] = dqa[:, lo:lo + 1]

    whole_col = pl.BlockSpec((None, None, seq, 2), lambda b, h, i: (b, h, 0, 0))
    cshape = jax.ShapeDtypeStruct((bsz, hp, seq, 2), F32)
    nd = jax.ShapeDtypeStruct((n, d), F32)
    return pl.pallas_call(
        body, grid=(bsz, hp, nb),
        in_specs=[blk(2 * LANES), blk(LANES), full(2 * LANES), full(LANES), full(LANES), rows],
        out_specs=[full(LANES), whole_col, blk(LANES), blk(LANES), col],
        out_shape=[nd, cshape, nd, nd, cshape],
        scratch_shapes=[pltpu.VMEM((seq, 2 * LANES), F32), pltpu.VMEM((nb, 2, t), F32)],
        name=name, compiler_params=_cp("parallel", "parallel", "arbitrary"),
    )(ka, v, qa, do, o, lse_row)


def _adamw(w, g, m, v, name):
    r, c = w.shape
    tr = r
    for cand in (512, 256, 128, 64, 32, 16, 8):
        if r % cand == 0 and r > cand and cand * c * 4 <= 4 * 1024 * 1024:
            tr = cand
            break

    def body(w_ref, g_ref, m_ref, v_ref, d_ref, m2_ref, v2_ref):
        gv = g_ref[...]
        m2 = ADAM_B1 * m_ref[...] + (1.0 - ADAM_B1) * gv
        v2 = ADAM_B2 * v_ref[...] + (1.0 - ADAM_B2) * jnp.square(gv)
        m_hat = m2 / (1.0 - ADAM_B1 ** ADAM_STEP)
        v_hat = v2 / (1.0 - ADAM_B2 ** ADAM_STEP)
        d_ref[...] = -ADAM_LR * (m_hat / (jnp.sqrt(v_hat) + ADAM_EPS) + ADAM_WD * w_ref[...])
        m2_ref[...] = m2
        v2_ref[...] = v2

    blk = pl.BlockSpec((tr, c), lambda i: (i, 0))
    shp = jax.ShapeDtypeStruct((r, c), F32)
    return pl.pallas_call(
        body, grid=(r // tr,), in_specs=[blk] * 4, out_specs=[blk] * 3, out_shape=[shp] * 3,
        name=name, compiler_params=_cp("parallel"),
    )(w, g, m, v)


def _adamw_nd(w, g, m, v, name):
    shape = w.shape
    two = (math.prod(shape[:-1]), shape[-1])
    d_, m_, v_ = _adamw(w.reshape(two), g.reshape(two), m.reshape(two), v.reshape(two), name)
    return g.reshape(shape), d_.reshape(shape), m_.reshape(shape), v_.reshape(shape)


def _place():
    x, y, c = lax.axis_index("x"), lax.axis_index("y"), lax.axis_index("c")
    chips = [(1 - x, y), (x, 1 - y), (1 - x, 1 - y)]
    return x, y, c, chips


def _gather_chips(shards):
    nt = len(shards)
    halves = [s.shape[0] // 2 for s in shards]

    def copies(ins, outs, send_sems, recv_sems):
        x, y, c, chips = _place()
        cps = []
        for t in range(nt):
            rows = pl.ds(c * halves[t], halves[t])
            for jj, (cx, cy) in enumerate(chips):
                cps.append(pltpu.make_async_remote_copy(
                    src_ref=ins[t].at[rows, :], dst_ref=outs[t].at[2 * x + y, rows, :], send_sem=send_sems.at[3 * t + jj],
                    recv_sem=recv_sems.at[3 * t + jj], device_id=(cx, cy, c), device_id_type=MESH))
        return cps

    def start(*refs):
        for cp in copies(*refs):
            cp.start()

    def wait(*refs):
        for cp in copies(*refs):
            cp.wait()

    return _Comm(shards, [jax.ShapeDtypeStruct((N_CHIPS,) + s.shape, s.dtype) for s in shards], 3 * nt, start, wait)


def _gather_sibling(shards, bufs, name):
    nt = len(shards)
    halves = [s.shape[0] // 2 for s in shards]

    def body(*refs):
        ins, outs = refs[:nt], refs[2 * nt:3 * nt]
        send_sems, recv_sems = refs[3 * nt:]
        x, y, c, chips = _place()
        sibling = (x, y, 1 - c)

        def copy(t, jj, hf):
            cx, cy = chips[jj]
            region = outs[t].at[2 * cx + cy, pl.ds(hf * halves[t], halves[t]), :]
            return pltpu.make_async_remote_copy(src_ref=region, dst_ref=region, send_sem=send_sems.at[4 * t + jj],
                                                recv_sem=recv_sems.at[4 * t + jj], device_id=sibling,
                                                device_id_type=MESH)

        def own(t):
            return pltpu.make_async_remote_copy(src_ref=ins[t], dst_ref=outs[t].at[2 * x + y],
                                                send_sem=send_sems.at[4 * t + 3], recv_sem=recv_sems.at[4 * t + 3],
                                                device_id=sibling, device_id_type=MESH)

        sends = [copy(t, jj, c) for t in range(nt) for jj in range(3)] + [own(t) for t in range(nt)]
        for cp in sends:
            cp.start()
        for t in range(nt):
            for jj in range(3):
                copy(t, jj, 1 - c).wait_recv()
            own(t).wait_recv()
        for cp in sends:
            cp.wait_send()

    return pl.pallas_call(
        body, in_specs=[ANY] * (2 * nt), out_specs=[ANY] * nt,
        out_shape=[jax.ShapeDtypeStruct(b.shape, b.dtype) for b in bufs],
        scratch_shapes=[pltpu.SemaphoreType.DMA((4 * nt,)), pltpu.SemaphoreType.DMA((4 * nt,))],
        input_output_aliases={nt + t: t for t in range(nt)}, name=name,
        compiler_params=pltpu.CompilerParams(has_side_effects=True),
    )(*shards, *bufs)


N_PARTIALS = 7


def _scatter_partials(grads):
    nt = len(grads)
    halves = [g.shape[1] // 2 for g in grads]

    def copies(ins, outs, send_sems, recv_sems):
        x, y, c, chips = _place()
        cps = []

        def copy(t, kk, slot, core, to):
            rows = pl.ds(core * halves[t], halves[t])
            return pltpu.make_async_remote_copy(
                src_ref=ins[t].at[slot, rows, :], dst_ref=outs[t].at[kk], send_sem=send_sems.at[N_PARTIALS * t + kk],
                recv_sem=recv_sems.at[N_PARTIALS * t + kk], device_id=to, device_id_type=MESH)

        for t in range(nt):
            for jj, (cx, cy) in enumerate(chips):
                cps.append(copy(t, 2 * jj, 2 * cx + cy, c, (cx, cy, c)))
                cps.append(copy(t, 2 * jj + 1, 2 * cx + cy, 1 - c, (cx, cy, 1 - c)))
            cps.append(copy(t, N_PARTIALS - 1, 2 * x + y, 1 - c, (x, y, 1 - c)))
        return cps

    def start(*refs):
        for cp in copies(*refs):
            cp.start()

    def wait(*refs):
        for cp in copies(*refs):
            cp.wait()

    return _Comm(grads, [jax.ShapeDtypeStruct((N_PARTIALS, h, g.shape[2]), g.dtype) for g, h in zip(grads, halves)],
                 N_PARTIALS * nt, start, wait)


def _add_partials(g, got, place_idx, name):
    s, r, w = g.shape
    rh = r // 2
    tr = rh
    for cand in (256, 128, 64, 32, 16):
        if rh % cand == 0 and cand * w * 4 <= 2 * 1024 * 1024:
            tr = cand
            break
    per = rh // tr

    def body(b_ref, g_ref, *rest):
        o_ref = rest[-1]
        acc = g_ref[...].astype(F32)
        for r_ref in rest[:-1]:
            acc = acc + r_ref[...].astype(F32)
        o_ref[...] = acc

    return pl.pallas_call(
        body,
        grid_spec=pltpu.PrefetchScalarGridSpec(
            num_scalar_prefetch=1, grid=(per,),
            in_specs=[pl.BlockSpec((None, tr, w), lambda a, b_ref: (b_ref[0], b_ref[1] * per + a, 0))]
            + [pl.BlockSpec((None, tr, w), lambda a, b_ref, kk=kk: (kk, a, 0)) for kk in range(N_PARTIALS)],
            out_specs=pl.BlockSpec((tr, w), lambda a, b_ref: (b_ref[1] * per + a, 0))),
        out_shape=jax.ShapeDtypeStruct((r, w), F32), name=name, compiler_params=_cp("parallel"),
    )(place_idx, g, *([got] * N_PARTIALS))


def _swap_halves(bufs, name):
    nt = len(bufs)

    def body(*refs):
        outs = refs[nt:2 * nt]
        send_sems, recv_sems = refs[2 * nt:]
        x, y, c, _ = _place()

        def copy(t, hf):
            rh = outs[t].shape[0] // 2
            region = outs[t].at[pl.ds(hf * rh, rh), :]
            return pltpu.make_async_remote_copy(src_ref=region, dst_ref=region, send_sem=send_sems.at[t],
                                                recv_sem=recv_sems.at[t], device_id=(x, y, 1 - c),
                                                device_id_type=MESH)

        sends = [copy(t, c) for t in range(nt)]
        for cp in sends:
            cp.start()
        for t in range(nt):
            copy(t, 1 - c).wait_recv()
        for cp in sends:
            cp.wait_send()

    return pl.pallas_call(
        body, in_specs=[ANY] * nt, out_specs=[ANY] * nt,
        out_shape=[jax.ShapeDtypeStruct(b.shape, b.dtype) for b in bufs],
        scratch_shapes=[pltpu.SemaphoreType.DMA((nt,)), pltpu.SemaphoreType.DMA((nt,))],
        input_output_aliases={t: t for t in range(nt)}, name=name,
        compiler_params=pltpu.CompilerParams(has_side_effects=True),
    )(*bufs)


def _rs_finish(grads, got, place_idx, tag):
    fin = [_add_partials(g, r, place_idx, name=f"rs_add_{tag}_{t}") for t, (g, r) in enumerate(zip(grads, got))]
    return _swap_halves(fin, name=f"rs_swap_{tag}")


def _all_reduce_small(v, name):
    r, w = v.shape

    def body(v_ref, o_ref, buf, send_sems, recv_sems):
        x, y, c, _ = _place()
        me = 4 * x + 2 * y + c
        buf[me] = v_ref[...]
        cps = []
        for kk in range(1, 8):
            peer = (x ^ ((kk >> 2) & 1), y ^ ((kk >> 1) & 1), c ^ (kk & 1))
            cps.append(pltpu.make_async_remote_copy(src_ref=v_ref, dst_ref=buf.at[me], send_sem=send_sems.at[kk - 1],
                                                    recv_sem=recv_sems.at[kk - 1], device_id=peer, device_id_type=MESH))
        for cp in cps:
            cp.start()
        for kk in range(1, 8):
            pltpu.make_async_remote_copy(src_ref=v_ref, dst_ref=buf.at[me ^ kk], send_sem=send_sems.at[kk - 1],
                                         recv_sem=recv_sems.at[kk - 1], device_id=(x, y, c),
                                         device_id_type=MESH).wait_recv()
        for cp in cps:
            cp.wait_send()
        acc = buf[0]
        for dev in range(1, 8):
            acc = acc + buf[dev]
        o_ref[...] = acc

    vm = pl.BlockSpec(memory_space=pltpu.VMEM)
    return pl.pallas_call(
        body, in_specs=[vm], out_specs=vm, out_shape=jax.ShapeDtypeStruct((r, w), F32),
        scratch_shapes=[pltpu.VMEM((8, r, w), F32), pltpu.SemaphoreType.DMA((7,)), pltpu.SemaphoreType.DMA((7,))],
        name=name, compiler_params=pltpu.CompilerParams(has_side_effects=True),
    )(v)


def _to_shards(a, axis=-1):
    axis = axis % a.ndim
    shp = a.shape
    a = a.reshape(shp[:axis] + (N_CHIPS, shp[axis] // N_CHIPS) + shp[axis + 1:])
    return jnp.moveaxis(a, axis, 0).reshape(N_CHIPS, -1)


def _from_shards(s, shard_shape, axis=-1):
    axis = axis % len(shard_shape)
    a = jnp.moveaxis(s.reshape((N_CHIPS,) + tuple(shard_shape)), 0, axis)
    return a.reshape(tuple(shard_shape[:axis]) + (N_CHIPS * shard_shape[axis],) + tuple(shard_shape[axis + 1:]))


def _pack(vecs, rows):
    flat = jnp.concatenate([v.reshape(v.shape[0], -1) if v.ndim > 1 else v.reshape(1, -1) for v in vecs], axis=1)
    lead = flat.shape[0]
    flat = jnp.pad(flat, ((0, 0), (0, rows * LANES - flat.shape[1])))
    return flat.reshape(lead, rows, LANES)


def _pack_rows(sizes, mult):
    total = sum(sizes)
    rows = -(-total // LANES)
    return -(-rows // mult) * mult


def _unpack(flat, shapes):
    out, pos = [], 0
    for shp in shapes:
        sz = math.prod(shp)
        out.append(flat[..., pos:pos + sz].reshape(flat.shape[:-1] + tuple(shp)))
        pos += sz
    return out


def _row_layout(col, t):
    bsz, hp, seq, _ = col.shape
    return col.reshape(bsz, hp, seq // t, t, 2).transpose(0, 1, 2, 4, 3)


def _from_col_layout(col):
    bsz, hp, seq, _ = col.shape
    a = col.transpose(0, 2, 1, 3).reshape(bsz * seq, 2 * hp)
    return jnp.pad(a, ((0, 0), (0, LANES - 2 * hp)))


def kernel(x, norm_mix, norm_ffn, conv_w_in, conv_b_in, conv_dw, conv_dw_b, conv_ln_g, conv_ln_b, conv_w_out, conv_b_out, pool_w, pool_b, pool_scale, fox_w_in, fox_b_f, fox_q_gain, fox_k_gain, fox_w_o, ffn_w_up, ffn_dw, ffn_dw_b, ffn_w_down, loss_target, m_norm_mix, m_norm_ffn, m_conv_w_in, m_conv_b_in, m_conv_dw, m_conv_dw_b, m_conv_ln_g, m_conv_ln_b, m_conv_w_out, m_conv_b_out, m_pool_w, m_pool_b, m_pool_scale, m_fox_w_in, m_fox_b_f, m_fox_q_gain, m_fox_k_gain, m_fox_w_o, m_ffn_w_up, m_ffn_dw, m_ffn_dw_b, m_ffn_w_down, v_norm_mix, v_norm_ffn, v_conv_w_in, v_conv_b_in, v_conv_dw, v_conv_dw_b, v_conv_ln_g, v_conv_ln_b, v_conv_w_out, v_conv_b_out, v_pool_w, v_pool_b, v_pool_scale, v_fox_w_in, v_fox_b_f, v_fox_q_gain, v_fox_k_gain, v_fox_w_o, v_ffn_w_up, v_ffn_dw, v_ffn_dw_b, v_ffn_w_down):
    bsz, seq, d = x.shape
    n = bsz * seq
    depth = norm_mix.shape[0]
    n_conv, n_pool, n_fox = conv_w_in.shape[0], pool_w.shape[0], fox_w_in.shape[0]
    f2 = ffn_dw_b.shape[1]
    f = f2 // 2
    nh = d // HEAD_DIM
    hp = d // LANES
    ng = len(POOL_WINDOWS)
    gd = d // ng
    chip_idx = jnp.stack([2 * lax.axis_index("x") + lax.axis_index("y"), lax.axis_index("c")]).astype(jnp.int32)

    small_shapes = [conv_b_in.shape, conv_dw.shape, conv_dw_b.shape, conv_ln_g.shape, conv_ln_b.shape,
                    conv_b_out.shape, pool_b.shape, ffn_dw.shape]
    small_rows = _pack_rows([math.prod(s) for s in small_shapes], 16)
    small = _pack([v.reshape(1, -1) for v in (conv_b_in, conv_dw, conv_dw_b, conv_ln_g, conv_ln_b, conv_b_out,
                                                pool_b, ffn_dw)], small_rows)[0]
    def layer_shards(i):
        kind, j = i % 3, i // 3
        shards = [ffn_w_up[i].astype(BF16), ffn_w_down[i].astype(BF16)]
        if kind == 0:
            shards += [conv_w_in[j].astype(BF16), conv_w_out[j].astype(BF16)]
        elif kind == 1:
            shards += [pool_w[j].reshape(ng * (gd // N_CHIPS), gd).astype(BF16)]
        else:
            shards += [fox_w_in[j].astype(BF16), fox_w_o[j].astype(BF16)]
        if i == 0:
            shards.append(small)
        return shards

    gathered = [None] * depth
    first_now = layer_shards(0)[2:]
    gathered[0] = [None, None] + list(_gather_sibling(
        first_now, _run_comm(_gather_chips(first_now), name="gather_chips_l0"), name="gather_sibling_l0"))
    small_all = gathered[0][-1].reshape(N_CHIPS, -1)
    sm = _unpack(small_all, small_shapes)
    axes = [-1] * 8
    b_in_f, dw_f, dw_b_f, ln_g_f, ln_b_f, b_out_f, pool_b_f, ffn_dw_f = [
        _from_shards(s_.reshape(N_CHIPS, -1), shp, ax) for s_, shp, ax in zip(sm, small_shapes, axes)]

    xs = x.reshape(n, d)
    tgt = loss_target.reshape(n, d)
    vec = lambda a: a.reshape(1, -1)

    saved = []
    cur = xs
    for i in range(depth):
        kind, j = i % 3, i // 3
        wts = gathered[i]
        sv = {"x_in": cur}
        if kind == 0:
            w_in, w_out = wts[2], wts[3].reshape(d, d)
            wcol = w_in.shape[2]
            h = _rms_fwd(cur, vec(norm_mix[i]), name=f"rms_mix_l{i}")
            pa = _mm(h, w_in, m=n, n=d, k=d, tk=d, tn=wcol, b_stk=wcol, b_s0=0, bias=vec(b_in_f[j, :d]),
                     name=f"conv_in_a_l{i}")
            pg = _mm(h, w_in, m=n, n=d, k=d, tk=d, tn=wcol, b_stk=wcol, b_s0=2, bias=vec(b_in_f[j, d:]),
                     name=f"conv_in_g_l{i}")
            taps = jnp.pad(dw_f[j], ((0, CONV_HALO - CONV_WIDTH), (0, 0)))
            conf_args = (pa, pg, taps, vec(dw_b_f[j]), vec(ln_g_f[j]), vec(ln_b_f[j]), seq)
            if i == 0:
                ffn_shards = layer_shards(0)[:2]
                (u, s_), landed = _conf_fwd(*conf_args, name=f"conf_fwd_l{i}", comm=_gather_chips(ffn_shards))
                gathered[0][:2] = _gather_sibling(ffn_shards, landed, name="gather_sibling_ffn_l0")
            else:
                u, s_ = _conf_fwd(*conf_args, name=f"conf_fwd_l{i}")
            cur = _mm(s_, w_out, m=n, n=d, k=d, tk=d, bias=vec(b_out_f[j]), res=cur, name=f"conv_out_l{i}")
            sv.update(h=h, pa=pa, pg=pg, u=u, s=s_, taps=taps)
        elif kind == 1:
            pw = wts[2].reshape(N_CHIPS, ng, gd // N_CHIPS, gd).transpose(1, 0, 2, 3).reshape(ng, gd, gd)
            cur, p = _pool_fwd(cur, vec(norm_mix[i]), pw, vec(pool_b_f[j]), vec(pool_scale[j]), seq,
                               name=f"pool_fwd_l{i}")
            sv.update(p=p, pw=pw)
        else:
            w_in = wts[2].transpose(1, 0, 2).reshape(d, -1)
            w_qkv = w_in[:, :3 * d]
            w_f = jnp.pad(w_in[:, 3 * d:], ((0, 0), (0, LANES - nh)))
            w_o = wts[3].reshape(d, d)
            bf = jnp.pad(vec(fox_b_f[j]), ((0, 0), (0, LANES - nh)))
            qg, kg = jnp.tile(vec(fox_q_gain[j]), (1, nh)), jnp.tile(vec(fox_k_gain[j]), (1, nh))
            h = _rms_fwd(cur, vec(norm_mix[i]), name=f"rms_mix_l{i}")
            qkv = _mm(h, w_qkv, m=n, n=3 * d, k=d, tk=d, tn=d, name=f"fox_qkv_l{i}")
            fl = _mm(h, w_f, m=n, n=LANES, k=d, tk=d, name=f"fox_fl_l{i}")
            qa, ka, v = _fox_prep_fwd(qkv, fl, bf, qg, kg, seq, name=f"fox_prep_l{i}")
            o, lse = _flash_fwd(qa, ka, v, bsz, seq, name=f"fox_attn_l{i}")
            cur = _mm(o, w_o, m=n, n=d, k=d, tk=d, res=cur, name=f"fox_out_l{i}")
            sv.update(h=h, qkv=qkv, fl=fl, qa=qa, ka=ka, v=v, o=o, lse=lse, w_qkv=w_qkv, w_f=w_f, w_o=w_o, bf=bf,
                      qg=qg, kg=kg)
        w_up, w_down = wts[0], wts[1].reshape(f, d)
        ucol = w_up.shape[2]
        sv["x_mid"] = cur
        h2 = _rms_fwd(cur, vec(norm_ffn[i]), name=f"rms_ffn_l{i}")
        uv = _mm(h2, w_up, m=n, n=f, k=d, tk=d, tn=ucol, b_stk=ucol, b_s0=0, name=f"ffn_up_v_l{i}")
        ug = _mm(h2, w_up, m=n, n=f, k=d, tk=d, tn=ucol, b_stk=ucol, b_s0=2, name=f"ffn_up_g_l{i}")
        fdw, fdb = ffn_dw_f[i], ffn_dw_b[i]
        glu_args = (uv, ug, fdw[:, :f], fdw[:, f:], vec(fdb[:f]), vec(fdb[f:]), w_down, cur, seq)
        if i + 1 < depth:
            nxt_shards = layer_shards(i + 1)
            (a_, cur), landed = _ffn_glu_down(*glu_args, name=f"ffn_glu_down_l{i}", comm=_gather_chips(nxt_shards))
            gathered[i + 1] = _gather_sibling(nxt_shards, landed, name=f"gather_sibling_l{i + 1}")
        else:
            a_, cur = _ffn_glu_down(*glu_args, name=f"ffn_glu_down_l{i}")
        sv.update(h2=h2, uv=uv, ug=ug, a=a_)
        saved.append(sv)

    dy, loss_part = _loss(cur, tgt, name="loss")
    loss = lax.psum(loss_part[0, 0], ("x", "y", "c"))

    g_norm_mix, g_norm_ffn = [None] * depth, [None] * depth
    g_ffn_dw_b = [None] * depth
    g_up, g_down = [None] * depth, [None] * depth
    g_conv_in, g_conv_out = [None] * n_conv, [None] * n_conv
    g_pool_w = g_fox_in = g_fox_o = None
    g_pool_scale = g_bf = g_qg = g_kg = None
    part_small = {"b_in": [None] * n_conv, "dw": [None] * n_conv, "dw_b": [None] * n_conv, "ln_g": [None] * n_conv,
                  "ln_b": [None] * n_conv, "b_out": [None] * n_conv, "pool_b": None, "ffn_dw": [None] * depth}

    pending = []
    done = {}

    def take_pending():
        groups = list(pending)
        pending.clear()
        parts = [p_ for g_ in groups for p_ in g_[0]]
        return groups, (_scatter_partials(parts) if parts else None)

    def finish_groups(groups, r2):
        pos = 0
        for parts, tag, sink in groups:
            sink(_rs_finish(parts, r2[pos:pos + len(parts)], chip_idx, tag))
            pos += len(parts)

    def carried(fn, args, name):
        groups, comm = take_pending()
        if comm is None:
            return fn(*args, name=name)
        outs, r2 = fn(*args, name=name, comm=comm)
        finish_groups(groups, r2)
        return outs

    dcur = dy
    for i in reversed(range(depth)):
        kind, j = i % 3, i // 3
        wts, sv = gathered[i], saved[i]
        w_up, w_down = wts[0], wts[1].reshape(f, d)
        ucol = w_up.shape[2]
        fdw, fdb = ffn_dw_f[i], ffn_dw_b[i]
        dw_down = _mm(sv["a"], dcur, m=f, n=d, k=n, tm=f // 2, tn=d, tk=2048, ta=True, out_dtype=BF16, name=f"ffn_down_dw_l{i}")
        duv, dug, dwv, dwg, dbv, dbg = carried(
            _ffn_glu_bwd, (sv["uv"], sv["ug"], dcur, w_down, fdw[:, :f], fdw[:, f:], vec(fdb[:f]), vec(fdb[f:]), seq),
            name=f"ffn_glu_bwd_l{i}")
        dw_up = _mm(sv["h2"], duv, m=d, n=f, k=n, tm=d, tn=ucol, tk=2048, ta=True, out_dtype=BF16, o_stk=ucol, o_s0=0,
                    o_slots=N_CHIPS, name=f"ffn_up_dw_v_l{i}")
        dw_up = _mm(sv["h2"], dug, m=d, n=f, k=n, tm=d, tn=ucol, tk=2048, ta=True, out_dtype=BF16, o_stk=ucol, o_s0=2,
                    o_slots=N_CHIPS, o_buf=dw_up, name=f"ffn_up_dw_g_l{i}")
        dh2 = _mm(duv, w_up, m=n, n=d, k=f, tn=d, tk=f, tb=True, b_stk=ucol, b_s0=0, name=f"ffn_up_dx_v_l{i}")
        dh2 = _mm(dug, w_up, m=n, n=d, k=f, tn=d, tk=f, tb=True, b_stk=ucol, b_s0=2, res=dh2,
                  name=f"ffn_up_dx_g_l{i}")
        dmid, g_norm_ffn[i] = _rms_bwd(sv["x_mid"], dh2, vec(norm_ffn[i]), dcur, name=f"rms_ffn_bwd_l{i}")
        part_small["ffn_dw"][i] = jnp.concatenate([dwv, dwg], axis=1)
        g_ffn_dw_b[i] = jnp.concatenate([dbv, dbg], axis=1)

        def ffn_sink(red, i=i):
            g_up[i], g_down[i] = red[0], red[1]

        pending.append(([dw_up, dw_down.reshape(N_CHIPS, f // N_CHIPS, d)], f"ffn_l{i}", ffn_sink))

        if kind == 0:
            w_in, w_out = wts[2], wts[3].reshape(d, d)
            wcol = w_in.shape[2]
            ds = _mm(dmid, w_out, m=n, n=d, k=d, tk=d, tb=True, name=f"conv_out_dx_l{i}")
            dw_out = _mm(sv["s"], dmid, m=d, n=d, k=n, tm=d, tn=d, tk=2048, ta=True, out_dtype=BF16, name=f"conv_out_dw_l{i}")
            dpa, dpg, ddw, ddwb, dlng, dlnb, dba, dbg_ = carried(
                _conf_bwd, (sv["u"], ds, sv["pa"], sv["pg"], sv["taps"], vec(ln_g_f[j]), vec(ln_b_f[j]), seq),
                name=f"conf_bwd_l{i}")
            dw_in = _mm(sv["h"], dpa, m=d, n=d, k=n, tm=d, tn=wcol, tk=2048, ta=True, out_dtype=BF16, o_stk=wcol, o_s0=0,
                        o_slots=N_CHIPS, name=f"conv_in_dw_a_l{i}")
            dw_in = _mm(sv["h"], dpg, m=d, n=d, k=n, tm=d, tn=wcol, tk=2048, ta=True, out_dtype=BF16, o_stk=wcol, o_s0=2,
                        o_slots=N_CHIPS, o_buf=dw_in, name=f"conv_in_dw_g_l{i}")
            dh = _mm(dpa, w_in, m=n, n=d, k=d, tn=d, tk=d, tb=True, b_stk=wcol, b_s0=0, name=f"conv_in_dx_a_l{i}")
            dh = _mm(dpg, w_in, m=n, n=d, k=d, tn=d, tk=d, tb=True, b_stk=wcol, b_s0=2, res=dh,
                     name=f"conv_in_dx_g_l{i}")
            dcur, g_norm_mix[i], db_out = _rms_bwd(sv["x_in"], dh, vec(norm_mix[i]), dmid, name=f"rms_mix_bwd_l{i}",
                                                   colsum=True)
            part_small["b_in"][j] = jnp.concatenate([dba, dbg_], axis=1)
            part_small["dw"][j] = ddw[:CONV_WIDTH]
            part_small["dw_b"][j], part_small["ln_g"][j], part_small["ln_b"][j] = ddwb, dlng, dlnb
            part_small["b_out"][j] = db_out
            mix_grads = [dw_in, dw_out.reshape(N_CHIPS, d // N_CHIPS, d)]
        elif kind == 1:
            dcur, g_norm_mix[i], dpw, dpb, dpsc = _pool_bwd(dmid, sv["x_in"], sv["p"], vec(norm_mix[i]), sv["pw"],
                                                            vec(pool_b_f[j]), vec(pool_scale[j]), seq,
                                                            name=f"pool_bwd_l{i}")
            part_small["pool_b"] = dpb
            g_pool_scale = dpsc
            mix_grads = [dpw.reshape(ng, N_CHIPS, gd // N_CHIPS, gd).transpose(1, 0, 2, 3).reshape(N_CHIPS, gd, gd)
                         .astype(BF16)]
        else:
            do = _mm(dmid, sv["w_o"], m=n, n=d, k=d, tk=d, tb=True, name=f"fox_out_dx_l{i}")
            dw_o = _mm(sv["o"], dmid, m=d, n=d, k=n, tm=d, tn=d, tk=2048, ta=True, out_dtype=BF16, name=f"fox_out_dw_l{i}")
            dq, dcc, dk, dv, dck = _flash_bwd(sv["qa"], sv["ka"], sv["v"], do, sv["o"],
                                              _row_layout(sv["lse"], _tile(seq, ATTN_TILE)), bsz, seq,
                                              name=f"fox_attn_bwd_l{i}")
            dqkv, dfl, dqg, dkg, dbf = _fox_prep_bwd(sv["qkv"], dq, dk, dv, _from_col_layout(dcc), _from_col_layout(dck),
                                                     sv["fl"], sv["bf"], sv["qg"], sv["kg"], seq,
                                                     name=f"fox_prep_bwd_l{i}")
            dw_qkv = _mm(sv["h"], dqkv, m=d, n=3 * d, k=n, tm=d, tn=d, tk=2048, ta=True, out_dtype=BF16, name=f"fox_qkv_dw_l{i}")
            dw_f = _mm(sv["h"], dfl, m=d, n=LANES, k=n, tm=d, tk=2048, ta=True, out_dtype=BF16, name=f"fox_fl_dw_l{i}")
            dh = _mm(dqkv, sv["w_qkv"], m=n, n=d, k=3 * d, tn=d, tk=d, tb=True, name=f"fox_qkv_dx_l{i}")
            dh = _mm(dfl, sv["w_f"], m=n, n=d, k=LANES, tn=d, tb=True, res=dh, name=f"fox_fl_dx_l{i}")
            dcur, g_norm_mix[i] = _rms_bwd(sv["x_in"], dh, vec(norm_mix[i]), dmid, name=f"rms_mix_bwd_l{i}")
            g_bf = dbf[:, :nh]
            g_qg = dqg.reshape(nh, HEAD_DIM).sum(axis=0, keepdims=True)
            g_kg = dkg.reshape(nh, HEAD_DIM).sum(axis=0, keepdims=True)
            dw_in_full = jnp.concatenate([dw_qkv, dw_f[:, :nh]], axis=1)
            wshard = dw_in_full.shape[1] // N_CHIPS
            mix_grads = [dw_in_full.reshape(d, N_CHIPS, wshard).transpose(1, 0, 2),
                         dw_o.reshape(N_CHIPS, d // N_CHIPS, d)]
        if i == 0:
            sm_parts = [jnp.concatenate(part_small["b_in"]), jnp.stack(part_small["dw"]),
                        jnp.concatenate(part_small["dw_b"]), jnp.concatenate(part_small["ln_g"]),
                        jnp.concatenate(part_small["ln_b"]), jnp.concatenate(part_small["b_out"]),
                        part_small["pool_b"].reshape(n_pool, ng, gd), jnp.stack(part_small["ffn_dw"])]
            mix_grads.append(_pack([_to_shards(p_) for p_ in sm_parts], small_rows))

        def mix_sink(red, i=i, kind=kind, j=j):
            if kind == 0:
                g_conv_in[j], g_conv_out[j] = red[0], red[1]
            elif kind == 1:
                done["pool_w"] = red[0]
            else:
                done["fox_in"], done["fox_o"] = red[0], red[1]
            if i == 0:
                done["small"] = red[-1].reshape(-1)

        pending.append((mix_grads, f"mix_l{i}", mix_sink))

    groups, comm = take_pending()
    finish_groups(groups, _run_comm(comm, name="rs_chips_tail"))
    g_pool_w, g_fox_in, g_fox_o, g_small_flat = done["pool_w"], done["fox_in"], done["fox_o"], done["small"]

    grad_x = dcur.reshape(bsz, seq, d)

    rep_parts = [jnp.concatenate(g_norm_mix), jnp.concatenate(g_norm_ffn), g_pool_scale, g_bf, g_qg, g_kg,
                 jnp.concatenate(g_ffn_dw_b)]
    rep_shapes = [norm_mix.shape, norm_ffn.shape, pool_scale.shape, fox_b_f.shape, fox_q_gain.shape,
                  fox_k_gain.shape, ffn_dw_b.shape]
    rep_rows = _pack_rows([math.prod(s) for s in rep_shapes], 8)
    rep = _all_reduce_small(_pack([p_.reshape(1, -1) for p_ in rep_parts], rep_rows)[0], name="all_reduce_small")
    g_rep = _unpack(rep.reshape(-1), rep_shapes)
    g_sm = _unpack(g_small_flat, small_shapes)

    grads = {
        "norm_mix": g_rep[0], "norm_ffn": g_rep[1],
        "conv_w_in": jnp.stack(g_conv_in), "conv_b_in": g_sm[0], "conv_dw": g_sm[1], "conv_dw_b": g_sm[2],
        "conv_ln_g": g_sm[3], "conv_ln_b": g_sm[4], "conv_w_out": jnp.stack(g_conv_out), "conv_b_out": g_sm[5],
        "pool_w": g_pool_w.reshape(pool_w.shape), "pool_b": g_sm[6], "pool_scale": g_rep[2],
        "fox_w_in": g_fox_in.reshape(fox_w_in.shape), "fox_b_f": g_rep[3], "fox_q_gain": g_rep[4],
        "fox_k_gain": g_rep[5], "fox_w_o": g_fox_o.reshape(fox_w_o.shape),
        "ffn_w_up": jnp.stack(g_up), "ffn_dw": g_sm[7], "ffn_dw_b": g_rep[6], "ffn_w_down": jnp.stack(g_down),
    }
    weights = dict(norm_mix=norm_mix, norm_ffn=norm_ffn, conv_w_in=conv_w_in, conv_b_in=conv_b_in, conv_dw=conv_dw,
                   conv_dw_b=conv_dw_b, conv_ln_g=conv_ln_g, conv_ln_b=conv_ln_b, conv_w_out=conv_w_out,
                   conv_b_out=conv_b_out, pool_w=pool_w, pool_b=pool_b, pool_scale=pool_scale, fox_w_in=fox_w_in,
                   fox_b_f=fox_b_f, fox_q_gain=fox_q_gain, fox_k_gain=fox_k_gain, fox_w_o=fox_w_o, ffn_w_up=ffn_w_up,
                   ffn_dw=ffn_dw, ffn_dw_b=ffn_dw_b, ffn_w_down=ffn_w_down)
    m_in = dict(norm_mix=m_norm_mix, norm_ffn=m_norm_ffn, conv_w_in=m_conv_w_in, conv_b_in=m_conv_b_in,
                conv_dw=m_conv_dw, conv_dw_b=m_conv_dw_b, conv_ln_g=m_conv_ln_g, conv_ln_b=m_conv_ln_b,
                conv_w_out=m_conv_w_out, conv_b_out=m_conv_b_out, pool_w=m_pool_w, pool_b=m_pool_b,
                pool_scale=m_pool_scale, fox_w_in=m_fox_w_in, fox_b_f=m_fox_b_f, fox_q_gain=m_fox_q_gain,
                fox_k_gain=m_fox_k_gain, fox_w_o=m_fox_w_o, ffn_w_up=m_ffn_w_up, ffn_dw=m_ffn_dw,
                ffn_dw_b=m_ffn_dw_b, ffn_w_down=m_ffn_w_down)
    v_in = dict(norm_mix=v_norm_mix, norm_ffn=v_norm_ffn, conv_w_in=v_conv_w_in, conv_b_in=v_conv_b_in,
                conv_dw=v_conv_dw, conv_dw_b=v_conv_dw_b, conv_ln_g=v_conv_ln_g, conv_ln_b=v_conv_ln_b,
                conv_w_out=v_conv_w_out, conv_b_out=v_conv_b_out, pool_w=v_pool_w, pool_b=v_pool_b,
                pool_scale=v_pool_scale, fox_w_in=v_fox_w_in, fox_b_f=v_fox_b_f, fox_q_gain=v_fox_q_gain,
                fox_k_gain=v_fox_k_gain, fox_w_o=v_fox_w_o, ffn_w_up=v_ffn_w_up, ffn_dw=v_ffn_dw,
                ffn_dw_b=v_ffn_dw_b, ffn_w_down=v_ffn_w_down)
    names = list(weights)
    g_out, d_out, m_out, v_out = [], [], [], []
    for nm in names:
        g_, dl_, m_, v_ = _adamw_nd(weights[nm], grads[nm].reshape(weights[nm].shape), m_in[nm], v_in[nm],
                                    name=f"adamw_{nm}")
        g_out.append(g_)
        d_out.append(dl_)
        m_out.append(m_)
        v_out.append(v_)
    return (loss, grad_x, *g_out, *d_out, *m_out, *v_out)
```

```python
import math

import jax
import jax.numpy as jnp
import numpy as np
from jax import lax
from jax.experimental import pallas as pl
from jax.experimental.pallas import tpu as pltpu

F32 = jnp.float32
BF16 = jnp.bfloat16
HI = lax.Precision.HIGHEST
MESH = pl.DeviceIdType.MESH
ANY = pl.BlockSpec(memory_space=pl.ANY)

EPS = 1e-6
HEAD_DIM = 64
LANES = 128
POOL_WINDOWS = (2, 4, 8, 16)
CONV_WIDTH = 31
CONV_HALO = 32
FFN_HALO = 8
FFN_ROWS, FFN_COLS = 256, 2816
FFN_DOWN_ROWS, FFN_DOWN_COLS = 256, 2816
POOL_HALO = 16
N_CHIPS = 4
NEG = -1e30

ADAM_LR = 0.001
ADAM_B1 = 0.9
ADAM_B2 = 0.999
ADAM_EPS = 1e-08
ADAM_WD = 0.01
ADAM_STEP = 10

V7X_VMEM_LIMIT_BYTES = 56 * 1024 * 1024


def _cp(*sem):
    return pltpu.CompilerParams(dimension_semantics=sem or None, vmem_limit_bytes=V7X_VMEM_LIMIT_BYTES)


def _tile(n, pref):
    t = min(n, pref)
    assert n % t == 0, (n, pref)
    return t


def _sig(v):
    return jax.nn.sigmoid(v)


def _roll(v, shift):
    n = v.shape[0]
    shift = shift % n
    return v if shift == 0 else pltpu.roll(v, shift, 0)


SUBLANES = 8


CONV_ROWS = 64


def _conv_taps(rot_ref, w_ref, out_ref, tm, start_of):
    d = out_ref.shape[1]
    for lc in range(d // LANES):
        ls = slice(lc * LANES, (lc + 1) * LANES)

        for r0 in range(0, tm, CONV_ROWS):
            acc = None
            for sh in range(CONV_WIDTH):
                kk = CONV_WIDTH - 1 - sh
                lo = r0 + start_of(sh)
                term = w_ref[kk:kk + 1, ls] * rot_ref[sh % SUBLANES, lo:lo + CONV_ROWS, ls]
                acc = term if acc is None else acc + term
            out_ref[r0:r0 + CONV_ROWS, ls] = acc


def _tap_grads(rotz_ref, rotd_ref, dw_ref, tm, halo):
    d = dw_ref.shape[1]
    for lc in range(d // LANES):
        ls = slice(lc * LANES, (lc + 1) * LANES)

        acc = [None] * CONV_WIDTH
        for r0 in range(0, tm, CONV_ROWS):
            duc = rotd_ref[0, r0:r0 + CONV_ROWS, ls]
            for sh in range(CONV_WIDTH):
                lo = r0 + halo - (sh // SUBLANES) * SUBLANES
                prod = duc * rotz_ref[sh % SUBLANES, lo:lo + CONV_ROWS, ls]
                part = prod.reshape(CONV_ROWS // SUBLANES, SUBLANES, LANES).sum(axis=0)
                acc[sh] = part if acc[sh] is None else acc[sh] + part
        for sh in range(CONV_WIDTH):
            kk = CONV_WIDTH - 1 - sh
            dw_ref[kk:kk + 1, ls] += jnp.sum(acc[sh], axis=0, keepdims=True)


class _Comm:
    def __init__(self, ins, out_shapes, n_sems, start, wait):
        self.ins, self.out_shapes, self.n_sems, self.start, self.wait = list(ins), list(out_shapes), n_sems, start, wait


def _hosted(body, comm, *, grid, in_specs, out_specs, out_shape, scratch_shapes, sem, name, ins):
    if comm is None:
        return pl.pallas_call(body, grid=grid, in_specs=in_specs, out_specs=out_specs, out_shape=out_shape,
                              scratch_shapes=scratch_shapes, name=name, compiler_params=_cp(*sem))(*ins)
    n_in, n_out, n_scr = len(in_specs), len(out_specs), len(scratch_shapes)
    nci, nco = len(comm.ins), len(comm.out_shapes)

    def wrapped(*refs):
        pos = [0]

        def take(cnt):
            pos[0] += cnt
            return refs[pos[0] - cnt:pos[0]]

        r_in, c_in, r_out, c_out, r_scr = take(n_in), take(nci), take(n_out), take(nco), take(n_scr)
        send_sems, recv_sems = take(2)
        ids = [pl.program_id(ax) for ax in range(len(grid))]
        first, last = ids[0] == 0, ids[0] == grid[0] - 1
        for ax in range(1, len(grid)):
            first = jnp.logical_and(first, ids[ax] == 0)
            last = jnp.logical_and(last, ids[ax] == grid[ax] - 1)

        @pl.when(first)
        def _():
            comm.start(c_in, c_out, send_sems, recv_sems)

        body(*r_in, *r_out, *r_scr)

        @pl.when(last)
        def _():
            comm.wait(c_in, c_out, send_sems, recv_sems)

    outs = pl.pallas_call(
        wrapped, grid=grid, in_specs=list(in_specs) + [ANY] * nci, out_specs=list(out_specs) + [ANY] * nco,
        out_shape=list(out_shape) + comm.out_shapes,
        scratch_shapes=list(scratch_shapes) + [pltpu.SemaphoreType.DMA((comm.n_sems,))] * 2, name=name,
        compiler_params=pltpu.CompilerParams(dimension_semantics=sem, vmem_limit_bytes=V7X_VMEM_LIMIT_BYTES,
                                             has_side_effects=True),
    )(*ins, *comm.ins)
    return list(outs[:n_out]), list(outs[n_out:])


def _run_comm(comm, name):
    def body(*refs):
        nci, nco = len(comm.ins), len(comm.out_shapes)
        c_in, c_out, send_sems, recv_sems = refs[:nci], refs[nci:nci + nco], refs[-2], refs[-1]
        comm.start(c_in, c_out, send_sems, recv_sems)
        comm.wait(c_in, c_out, send_sems, recv_sems)

    return pl.pallas_call(
        body, in_specs=[ANY] * len(comm.ins), out_specs=[ANY] * len(comm.out_shapes), out_shape=comm.out_shapes,
        scratch_shapes=[pltpu.SemaphoreType.DMA((comm.n_sems,))] * 2, name=name,
        compiler_params=pltpu.CompilerParams(has_side_effects=True),
    )(*comm.ins)


def _mm(a, b, *, m, n, k, name, tm=1024, tn=1024, tk=512, ta=False, tb=False, b_stk=None, b_s0=0,
        o_stk=None, o_s0=0, o_slots=None, o_buf=None, bias=None, res=None, out_dtype=F32):
    tm, tn, tk = _tile(m, tm), _tile(n, tn), _tile(k, tk)
    gi, gj, gk = m // tm, n // tn, k // tk
    nsl = 1
    a_spec = pl.BlockSpec((tk, tm), lambda j, i, kk: (kk, i)) if ta else pl.BlockSpec((tm, tk), lambda j, i, kk: (i, kk))
    if b_stk is None:
        b_spec = pl.BlockSpec((tn, tk), lambda j, i, kk: (j, kk)) if tb else pl.BlockSpec((tk, tn), lambda j, i, kk: (kk, j))
    elif tb and tk > b_stk:
        assert tk % b_stk == 0 and b_s0 % (tk // b_stk) == 0 and not ta
        nsl = tk // b_stk
        b_spec = pl.BlockSpec((nsl, tn, b_stk), lambda j, i, kk: (b_s0 // nsl + kk, j, 0))
    elif tb:
        assert b_stk % tk == 0
        per = b_stk // tk
        b_spec = pl.BlockSpec((None, tn, tk), lambda j, i, kk: (b_s0 + kk // per, j, kk % per))
    else:
        assert b_stk % tn == 0
        per = b_stk // tn
        b_spec = pl.BlockSpec((None, tk, tn), lambda j, i, kk: (b_s0 + j // per, kk, j % per))
    ins, in_specs = [a, b], [a_spec, b_spec]
    if bias is not None:
        ins.append(bias)
        in_specs.append(pl.BlockSpec((1, tn), lambda j, i, kk: (0, j)))
    if res is not None:
        ins.append(res)
        in_specs.append(pl.BlockSpec((tm, tn), lambda j, i, kk: (i, j)))
    aliases = {}
    if o_stk is None:
        out_shape = jax.ShapeDtypeStruct((m, n), out_dtype)
        o_spec = pl.BlockSpec((tm, tn), lambda j, i, kk: (i, j))
    else:
        assert o_stk % tn == 0
        pero = o_stk // tn
        out_shape = jax.ShapeDtypeStruct((o_slots, m, o_stk), out_dtype)
        o_spec = pl.BlockSpec((None, tm, tn), lambda j, i, kk: (o_s0 + j // pero, i, j % pero))
        if o_buf is not None:
            aliases = {len(ins): 0}
            ins.append(o_buf)
            in_specs.append(ANY)
    has_bias, has_res, has_buf = bias is not None, res is not None, o_buf is not None
    dn = (((0 if ta else 1,), (1 if tb else 0,)), ((), ()))

    def body(*refs):
        a_ref, b_ref = refs[0], refs[1]
        pos = 2
        bias_ref = refs[pos] if has_bias else None
        pos += has_bias
        res_ref = refs[pos] if has_res else None
        pos += has_res + has_buf
        o_ref = refs[pos]
        if nsl == 1:
            p = lax.dot_general(a_ref[...].astype(BF16), b_ref[...].astype(BF16), dn, preferred_element_type=F32)
        else:
            p = sum(lax.dot_general(a_ref[:, s * b_stk:(s + 1) * b_stk].astype(BF16), b_ref[s].astype(BF16), dn,
                                    preferred_element_type=F32) for s in range(nsl))

        def finish(acc):
            if has_bias:
                acc = acc + bias_ref[...]
            if has_res:
                acc = acc + res_ref[...]
            o_ref[...] = acc.astype(o_ref.dtype)

        if gk == 1:
            finish(p)
        else:
            acc_ref = refs[pos + 1]
            kk = pl.program_id(2)

            @pl.when(kk == 0)
            def _():
                acc_ref[...] = p

            @pl.when(kk > 0)
            def _():
                acc_ref[...] += p

            @pl.when(kk == gk - 1)
            def _():
                finish(acc_ref[...])

    return pl.pallas_call(
        body, grid=(gj, gi, gk), in_specs=in_specs, out_specs=o_spec, out_shape=out_shape,
        scratch_shapes=[pltpu.VMEM((tm, tn), F32)] if gk > 1 else [],
        input_output_aliases=aliases, name=name,
        compiler_params=_cp("parallel", "parallel", "arbitrary"),
    )(*ins)


def _rms_fwd(x, g, name):
    n, d = x.shape
    tm = _tile(n, 512)

    def body(x_ref, g_ref, h_ref):
        xv = x_ref[...]
        r = lax.rsqrt(jnp.mean(xv * xv, axis=-1, keepdims=True) + EPS)
        h_ref[...] = (xv * r * g_ref[...]).astype(h_ref.dtype)

    return pl.pallas_call(
        body, grid=(n // tm,),
        in_specs=[pl.BlockSpec((tm, d), lambda i: (i, 0)), pl.BlockSpec((1, d), lambda i: (0, 0))],
        out_specs=pl.BlockSpec((tm, d), lambda i: (i, 0)),
        out_shape=jax.ShapeDtypeStruct((n, d), BF16), name=name, compiler_params=_cp("arbitrary"),
    )(x, g)


def _rms_bwd(x, dh, g, dres, name, colsum=False):
    n, d = x.shape
    tm = _tile(n, 512)

    def body(x_ref, dh_ref, g_ref, dres_ref, dx_ref, dg_ref, *rest):
        i = pl.program_id(0)
        xv, dhv = x_ref[...], dh_ref[...]
        r = lax.rsqrt(jnp.mean(xv * xv, axis=-1, keepdims=True) + EPS)
        xn = xv * r
        dxn = dhv * g_ref[...]
        dx_ref[...] = dres_ref[...] + r * (dxn - xn * jnp.mean(dxn * xn, axis=-1, keepdims=True))
        dg = jnp.sum(dhv * xn, axis=0, keepdims=True)

        @pl.when(i == 0)
        def _():
            dg_ref[...] = jnp.zeros_like(dg_ref)
            if colsum:
                rest[0][...] = jnp.zeros_like(rest[0])

        dg_ref[...] += dg
        if colsum:
            rest[0][...] += jnp.sum(dres_ref[...], axis=0, keepdims=True)

    row = pl.BlockSpec((tm, d), lambda i: (i, 0))
    vec = pl.BlockSpec((1, d), lambda i: (0, 0))
    out_shape = [jax.ShapeDtypeStruct((n, d), F32), jax.ShapeDtypeStruct((1, d), F32)]
    out_specs = [row, vec]
    if colsum:
        out_shape.append(jax.ShapeDtypeStruct((1, d), F32))
        out_specs.append(vec)
    return pl.pallas_call(
        body, grid=(n // tm,), in_specs=[row, row, vec, row], out_specs=out_specs, out_shape=out_shape,
        name=name, compiler_params=_cp("arbitrary"),
    )(x, dh, g, dres)


def _loss(y, tgt, name):
    n, d = y.shape
    tm = _tile(n, 512)

    def body(y_ref, t_ref, dy_ref, l_ref):
        i = pl.program_id(0)
        e = y_ref[...] - t_ref[...]
        dy_ref[...] = e / d
        part = 0.5 * jnp.sum(jnp.mean(e * e, axis=-1, keepdims=True), axis=0, keepdims=True)

        @pl.when(i == 0)
        def _():
            l_ref[...] = jnp.zeros_like(l_ref)

        l_ref[...] += part

    row = pl.BlockSpec((tm, d), lambda i: (i, 0))
    return pl.pallas_call(
        body, grid=(n // tm,), in_specs=[row, row],
        out_specs=[row, pl.BlockSpec((1, 1), lambda i: (0, 0))],
        out_shape=[jax.ShapeDtypeStruct((n, d), F32), jax.ShapeDtypeStruct((1, 1), F32)],
        name=name, compiler_params=_cp("arbitrary"),
    )(y, tgt)


def _ffn_specs(n, f, tm, tc, seq):
    hb = FFN_HALO
    cur = pl.BlockSpec((tm, tc), lambda j, i: (i, j))
    prev = pl.BlockSpec((hb, tc), lambda j, i: (jnp.maximum(i * (tm // hb) - 1, 0), j))
    nxt = pl.BlockSpec((hb, tc), lambda j, i: (jnp.minimum((i + 1) * (tm // hb), n // hb - 1), j))
    taps = pl.BlockSpec((3, tc), lambda j, i: (0, j))
    vec = pl.BlockSpec((1, tc), lambda j, i: (0, j))
    return cur, prev, nxt, taps, vec


def _ffn_glu_down(uv, ug, wv, wg, bv, bg, w_down, res, seq, name, comm=None):
    n, f = uv.shape
    d = w_down.shape[1]
    tm, tc = _tile(seq, FFN_DOWN_ROWS), _tile(f, FFN_DOWN_COLS)
    tps = seq // tm
    nj = f // tc
    hb = FFN_HALO
    cur = pl.BlockSpec((tm, tc), lambda i, j: (i, j))
    prev = pl.BlockSpec((hb, tc), lambda i, j: (jnp.maximum(i * (tm // hb) - 1, 0), j))
    taps = pl.BlockSpec((3, tc), lambda i, j: (0, j))
    vec = pl.BlockSpec((1, tc), lambda i, j: (0, j))
    wblk = pl.BlockSpec((tc, d), lambda i, j: (j, 0))
    row = pl.BlockSpec((tm, d), lambda i, j: (i, 0))

    def body(uvp, uvc, ugp, ugc, wv_ref, wg_ref, bv_ref, bg_ref, wd_ref, res_ref, a_ref, x_ref, acc_ref):
        first = (pl.program_id(0) % tps) == 0
        j = pl.program_id(1)

        def conv(p_ref, c_ref, w_ref, b_ref):
            xs = jnp.concatenate([jnp.where(first, 0.0, p_ref[...]), c_ref[...]], axis=0)
            w = w_ref[...]
            y = w[2:3] * xs + w[1:2] * _roll(xs, 1) + w[0:1] * _roll(xs, 2)
            return y[FFN_HALO:] + b_ref[...]

        val = conv(uvp, uvc, wv_ref, bv_ref)
        gate = conv(ugp, ugc, wg_ref, bg_ref)
        a = (gate * _sig(gate) * val).astype(BF16)
        a_ref[...] = a
        p = jnp.dot(a, wd_ref[...], preferred_element_type=F32)

        @pl.when(j == 0)
        def _():
            acc_ref[...] = res_ref[...] + p

        @pl.when(j > 0)
        def _():
            acc_ref[...] += p

        @pl.when(j == nj - 1)
        def _():
            x_ref[...] = acc_ref[...]

    return _hosted(
        body, comm, grid=(n // tm, nj), in_specs=[prev, cur, prev, cur, taps, taps, vec, vec, wblk, row],
        out_specs=[cur, row], out_shape=[jax.ShapeDtypeStruct((n, f), BF16), jax.ShapeDtypeStruct((n, d), F32)],
        scratch_shapes=[pltpu.VMEM((tm, d), F32)], sem=("parallel", "arbitrary"), name=name,
        ins=(uv, uv, ug, ug, wv, wg, bv, bg, w_down, res))


def _ffn_glu_bwd(uv, ug, dx, w_down, wv, wg, bv, bg, seq, name, comm=None):
    n, f = uv.shape
    d = dx.shape[1]
    tm, tc = _tile(seq, FFN_ROWS), _tile(f, FFN_COLS)
    tps = seq // tm
    hb = FFN_HALO
    ext = tm + hb
    cur, prev, nxt, taps, vec = _ffn_specs(n, f, tm, tc, seq)
    dx_cur = pl.BlockSpec((tm, d), lambda j, i: (i, 0))
    dx_nxt = pl.BlockSpec((hb, d), lambda j, i: (jnp.minimum((i + 1) * (tm // hb), n // hb - 1), 0))
    wblk = pl.BlockSpec((tc, d), lambda j, i: (j, 0))

    def body(uvp, uvc, uvn, ugp, ugc, ugn, dx_c, dx_n, wd_ref, wv_ref, wg_ref, bv_ref, bg_ref,
             duv_ref, dug_ref, dwv_ref, dwg_ref, dbv_ref, dbg_ref):
        i = pl.program_id(1)
        first = (i % tps) == 0
        last = (i % tps) == tps - 1
        dx_e = jnp.concatenate([dx_c[...], jnp.where(last, 0.0, dx_n[...])], axis=0).astype(BF16)
        da_e = lax.dot_general(dx_e, wd_ref[...], _NT, preferred_element_type=F32)

        def taps3(p_ref, c_ref, n_ref):
            xs = jnp.concatenate([jnp.where(first, 0.0, p_ref[...]), c_ref[...], n_ref[...]], axis=0)
            return xs, _roll(xs, 1), _roll(xs, 2)

        xv, xg = taps3(uvp, uvc, uvn), taps3(ugp, ugc, ugn)
        wv_, wg_ = wv_ref[...], wg_ref[...]

        def conv(xs, w, b_ref):
            return (w[2:3] * xs[0] + w[1:2] * xs[1] + w[0:1] * xs[2])[hb:] + b_ref[...]

        val, gate = conv(xv, wv_, bv_ref), conv(xg, wg_, bg_ref)
        sg = _sig(gate)
        dval = da_e * (gate * sg)
        dgate = da_e * val * (sg * (1.0 + gate * (1.0 - sg)))

        def conv_t(dv, w):
            return (w[2:3] * dv + w[1:2] * _roll(dv, ext - 1) + w[0:1] * _roll(dv, ext - 2))[:tm]

        duv_ref[...] = conv_t(dval, wv_).astype(duv_ref.dtype)
        dug_ref[...] = conv_t(dgate, wg_).astype(dug_ref.dtype)

        def tap_grads(d_own, xs):
            return jnp.concatenate(
                [jnp.sum(d_own * xs[2 - kk][hb:hb + tm], axis=0, keepdims=True) for kk in range(3)], axis=0)

        dv_own, dg_own = dval[:tm], dgate[:tm]

        @pl.when(i == 0)
        def _():
            for r in (dwv_ref, dwg_ref, dbv_ref, dbg_ref):
                r[...] = jnp.zeros_like(r)

        dwv_ref[...] += tap_grads(dv_own, xv)
        dwg_ref[...] += tap_grads(dg_own, xg)
        dbv_ref[...] += jnp.sum(dv_own, axis=0, keepdims=True)
        dbg_ref[...] += jnp.sum(dg_own, axis=0, keepdims=True)

    return _hosted(
        body, comm, grid=(f // tc, n // tm),
        in_specs=[prev, cur, nxt, prev, cur, nxt, dx_cur, dx_nxt, wblk, taps, taps, vec, vec],
        out_specs=[cur, cur, taps, taps, vec, vec],
        out_shape=[jax.ShapeDtypeStruct((n, f), BF16), jax.ShapeDtypeStruct((n, f), BF16),
                   jax.ShapeDtypeStruct((3, f), F32), jax.ShapeDtypeStruct((3, f), F32),
                   jax.ShapeDtypeStruct((1, f), F32), jax.ShapeDtypeStruct((1, f), F32)],
        scratch_shapes=[], sem=("parallel", "arbitrary"), name=name,
        ins=(uv, uv, uv, ug, ug, ug, dx, dx, w_down, wv, wg, bv, bg))


def _conf_specs(n, d, tm):
    hb = CONV_HALO
    cur = pl.BlockSpec((tm, d), lambda i: (i, 0))
    prev = pl.BlockSpec((hb, d), lambda i: (jnp.maximum(i * (tm // hb) - 1, 0), 0))
    nxt = pl.BlockSpec((hb, d), lambda i: (jnp.minimum((i + 1) * (tm // hb), n // hb - 1), 0))
    taps = pl.BlockSpec((CONV_HALO, d), lambda i: (0, 0))
    vec = pl.BlockSpec((1, d), lambda i: (0, 0))
    return cur, prev, nxt, taps, vec


def _conf_fwd(pa, pg, w, wb, lng, lnb, seq, name, comm=None):
    n, d = pa.shape
    tm = _tile(seq, 256)
    tps = seq // tm
    hb = CONV_HALO
    cur, prev, _, taps, vec = _conf_specs(n, d, tm)

    def body(pap, pac, pgp, pgc, w_ref, wb_ref, lng_ref, lnb_ref, u_ref, s_ref, rot_ref):
        first = (pl.program_id(0) % tps) == 0
        a = jnp.concatenate([jnp.where(first, 0.0, pap[...]), pac[...]], axis=0)
        g = jnp.concatenate([jnp.where(first, 0.0, pgp[...]), pgc[...]], axis=0)
        z = a * _sig(g)
        for b in range(SUBLANES):
            rot_ref[b] = _roll(z, b)
        _conv_taps(rot_ref, w_ref, u_ref, tm, lambda sh: hb - (sh // SUBLANES) * SUBLANES)
        u = u_ref[...] + wb_ref[...]
        mu = jnp.mean(u, axis=-1, keepdims=True)
        uc = u - mu
        var = jnp.mean(uc * uc, axis=-1, keepdims=True)
        ul = uc * lax.rsqrt(var + EPS) * lng_ref[...] + lnb_ref[...]
        u_ref[...] = u
        s_ref[...] = (ul * _sig(ul)).astype(s_ref.dtype)

    return _hosted(
        body, comm, grid=(n // tm,), in_specs=[prev, cur, prev, cur, taps, vec, vec, vec],
        out_specs=[cur, cur],
        out_shape=[jax.ShapeDtypeStruct((n, d), F32), jax.ShapeDtypeStruct((n, d), BF16)],
        scratch_shapes=[pltpu.VMEM((SUBLANES, tm + hb, d), F32)], sem=("arbitrary",), name=name,
        ins=(pa, pa, pg, pg, w, wb, lng, lnb))


def _conf_bwd(u, ds, pa, pg, w, lng, lnb, seq, name, comm=None):
    n, d = u.shape
    tm = _tile(seq, 256)
    tps = seq // tm
    hb = CONV_HALO
    ext = tm + hb
    cur, prev, nxt, taps, vec = _conf_specs(n, d, tm)

    def body(uc_ref, un_ref, dsc_ref, dsn_ref, pap, pac, pgp, pgc, w_ref, lng_ref, lnb_ref,
             dpa_ref, dpg_ref, dw_ref, dwb_ref, dlng_ref, dlnb_ref, dba_ref, dbg_ref, rotz_ref, rotd_ref, dz_ref):
        i = pl.program_id(0)
        first = (i % tps) == 0
        last = (i % tps) == tps - 1

        @pl.when(i == 0)
        def _():
            for r in (dw_ref, dwb_ref, dlng_ref, dlnb_ref, dba_ref, dbg_ref):
                r[...] = jnp.zeros_like(r)

        ue = jnp.concatenate([uc_ref[...], un_ref[...]], axis=0)
        dse = jnp.concatenate([dsc_ref[...], jnp.where(last, 0.0, dsn_ref[...])], axis=0)
        mu = jnp.mean(ue, axis=-1, keepdims=True)
        cen = ue - mu
        r = lax.rsqrt(jnp.mean(cen * cen, axis=-1, keepdims=True) + EPS)
        xn = cen * r
        ul = xn * lng_ref[...] + lnb_ref[...]
        sg = _sig(ul)
        dul = dse * (sg * (1.0 + ul * (1.0 - sg)))
        dun = dul * lng_ref[...]
        du = r * (dun - jnp.mean(dun, axis=-1, keepdims=True) - xn * jnp.mean(dun * xn, axis=-1, keepdims=True))
        dlng_ref[...] += jnp.sum((dul * xn)[:tm], axis=0, keepdims=True)
        dlnb_ref[...] += jnp.sum(dul[:tm], axis=0, keepdims=True)
        dwb_ref[...] += jnp.sum(du[:tm], axis=0, keepdims=True)
        for b in range(SUBLANES):
            rotd_ref[b] = _roll(du, ext - b)
        _conv_taps(rotd_ref, w_ref, dz_ref, tm, lambda sh: (sh // SUBLANES) * SUBLANES)
        dz = dz_ref[...]

        a = jnp.concatenate([jnp.where(first, 0.0, pap[...]), pac[...]], axis=0)
        g = jnp.concatenate([jnp.where(first, 0.0, pgp[...]), pgc[...]], axis=0)
        sgg = _sig(g)
        z = a * sgg
        for b in range(SUBLANES):
            rotz_ref[b] = _roll(z, b)
        _tap_grads(rotz_ref, rotd_ref, dw_ref, tm, hb)

        a_c, sg_c = a[hb:], sgg[hb:]
        da = dz * sg_c
        dg = dz * a_c * sg_c * (1.0 - sg_c)
        dpa_ref[...] = da.astype(dpa_ref.dtype)
        dpg_ref[...] = dg.astype(dpg_ref.dtype)
        dba_ref[...] += jnp.sum(da, axis=0, keepdims=True)
        dbg_ref[...] += jnp.sum(dg, axis=0, keepdims=True)

    vshape = jax.ShapeDtypeStruct((1, d), F32)
    return _hosted(
        body, comm, grid=(n // tm,),
        in_specs=[cur, nxt, cur, nxt, prev, cur, prev, cur, taps, vec, vec],
        out_specs=[cur, cur, taps, vec, vec, vec, vec, vec],
        out_shape=[jax.ShapeDtypeStruct((n, d), BF16), jax.ShapeDtypeStruct((n, d), BF16),
                   jax.ShapeDtypeStruct((CONV_HALO, d), F32), vshape, vshape, vshape, vshape, vshape],
        scratch_shapes=[pltpu.VMEM((SUBLANES, ext, d), F32), pltpu.VMEM((SUBLANES, ext, d), F32),
                        pltpu.VMEM((tm, d), F32)],
        sem=("arbitrary",), name=name, ins=(u, u, ds, ds, pa, pa, pg, pg, w, lng, lnb))


def _pool_specs(n, d, tm, gd):
    hb = POOL_HALO
    cur = pl.BlockSpec((tm, d), lambda i: (i, 0))
    prev = pl.BlockSpec((hb, d), lambda i: (jnp.maximum(i * (tm // hb) - 1, 0), 0))
    nxt = pl.BlockSpec((hb, d), lambda i: (jnp.minimum((i + 1) * (tm // hb), n // hb - 1), 0))
    wsp = pl.BlockSpec((len(POOL_WINDOWS), gd, gd), lambda i: (0, 0, 0))
    vec = pl.BlockSpec((1, d), lambda i: (0, 0))
    return cur, prev, nxt, wsp, vec


def _pool_fwd(x, g, w, b, sc, seq, name):
    n, d = x.shape
    gd = d // len(POOL_WINDOWS)
    tm = _tile(seq, 256)
    tps = seq // tm
    hb = POOL_HALO
    cur, prev, _, wsp, vec = _pool_specs(n, d, tm, gd)

    def body(xp, xc, g_ref, w_ref, b_ref, sc_ref, x1_ref, p_ref):
        i = pl.program_id(0)
        first = (i % tps) == 0
        xe = jnp.concatenate([jnp.where(first, 0.0, xp[...]), xc[...]], axis=0)
        r = lax.rsqrt(jnp.mean(xe * xe, axis=-1, keepdims=True) + EPS)
        h = xe * r * g_ref[...]
        t = ((i % tps) * tm + lax.broadcasted_iota(jnp.int32, (tm, 1), 0) + 1).astype(F32)
        ys = []
        for gi, win in enumerate(POOL_WINDOWS):
            hg = h[:, gi * gd:(gi + 1) * gd]
            s, sh = hg, 1
            while sh < win:
                s = s + _roll(s, sh)
                sh *= 2
            p = (s[hb:] / jnp.minimum(t, float(win)) - hg[hb:]).astype(BF16)
            p_ref[:, gi * gd:(gi + 1) * gd] = p
            ys.append(jnp.dot(p, w_ref[gi], preferred_element_type=F32))
        y = jnp.concatenate(ys, axis=1) + b_ref[...]
        x1_ref[...] = xc[...] + y * sc_ref[...]

    return pl.pallas_call(
        body, grid=(n // tm,), in_specs=[prev, cur, vec, wsp, vec, vec], out_specs=[cur, cur],
        out_shape=[jax.ShapeDtypeStruct((n, d), F32), jax.ShapeDtypeStruct((n, d), BF16)],
        name=name, compiler_params=_cp("arbitrary"),
    )(x, x, g, w, b, sc)


def _pool_bwd(dx1, x, p, g, w, b, sc, seq, name):
    n, d = x.shape
    ng = len(POOL_WINDOWS)
    gd = d // ng
    tm = _tile(seq, 256)
    tps = seq // tm
    hb = POOL_HALO
    ext = tm + hb
    cur, _, nxt, wsp, vec = _pool_specs(n, d, tm, gd)

    def body(dc_ref, dn_ref, x_ref, p_ref, g_ref, w_ref, b_ref, sc_ref, dx_ref, dg_ref, dw_ref, db_ref, dsc_ref):
        i = pl.program_id(0)
        last = (i % tps) == tps - 1

        @pl.when(i == 0)
        def _():
            for r_ in (dg_ref, dw_ref, db_ref, dsc_ref):
                r_[...] = jnp.zeros_like(r_)

        dxc = dc_ref[...]
        dxe = jnp.concatenate([dxc, jnp.where(last, 0.0, dn_ref[...])], axis=0)
        dyg = dxe * sc_ref[...]
        t = ((i % tps) * tm + lax.broadcasted_iota(jnp.int32, (ext, 1), 0) + 1).astype(F32)
        dhs = []
        for gi, win in enumerate(POOL_WINDOWS):
            sl = slice(gi * gd, (gi + 1) * gd)
            dyb = dyg[:, sl].astype(BF16)
            wg = w_ref[gi]
            dp = lax.dot_general(dyb, wg, (((1,), (1,)), ((), ())), preferred_element_type=F32)
            s, sh = dp / jnp.minimum(t, float(win)), 1
            while sh < win:
                s = s + _roll(s, ext - sh)
                sh *= 2
            dhs.append((s - dp)[:tm])
            pg = p_ref[:, sl]
            dw_ref[gi] += lax.dot_general(pg, dyb[:tm], (((0,), (0,)), ((), ())), preferred_element_type=F32)
            ypre = jnp.dot(pg, wg, preferred_element_type=F32) + b_ref[:, sl]
            dsc_ref[:, sl] += jnp.sum(dxc[:, sl] * ypre, axis=0, keepdims=True)
            db_ref[:, sl] += jnp.sum(dyg[:tm, sl], axis=0, keepdims=True)
        dh = jnp.concatenate(dhs, axis=1)
        xv = x_ref[...]
        r = lax.rsqrt(jnp.mean(xv * xv, axis=-1, keepdims=True) + EPS)
        xn = xv * r
        dxn = dh * g_ref[...]
        dx_ref[...] = dxc + r * (dxn - xn * jnp.mean(dxn * xn, axis=-1, keepdims=True))
        dg_ref[...] += jnp.sum(dh * xn, axis=0, keepdims=True)

    vshape = jax.ShapeDtypeStruct((1, d), F32)
    return pl.pallas_call(
        body, grid=(n // tm,), in_specs=[cur, nxt, cur, cur, vec, wsp, vec, vec],
        out_specs=[cur, vec, wsp, vec, vec],
        out_shape=[jax.ShapeDtypeStruct((n, d), F32), vshape, jax.ShapeDtypeStruct((ng, gd, gd), F32), vshape, vshape],
        name=name, compiler_params=_cp("arbitrary"),
    )(dx1, dx1, x, p, g, w, b, sc)


def _head_maps(d):
    hd = lax.broadcasted_iota(jnp.int32, (d, LANES), 0) // HEAD_DIM
    col = lax.broadcasted_iota(jnp.int32, (d, LANES), 1)
    gm = (hd == col).astype(BF16)
    hd_t = lax.broadcasted_iota(jnp.int32, (LANES, d), 1) // HEAD_DIM
    row = lax.broadcasted_iota(jnp.int32, (LANES, d), 0)
    gt = (hd_t == row).astype(BF16)
    return gm, gt


def _dot_split(v, onehot):
    hi = v.astype(BF16)
    lo = (v - hi.astype(F32)).astype(BF16)
    return jnp.dot(hi, onehot, preferred_element_type=F32) + jnp.dot(lo, onehot, preferred_element_type=F32)


def _bias_placement(nh):
    pq = np.zeros((3 * LANES, nh * HEAD_DIM), np.float32)
    pk = np.zeros((3 * LANES, nh * HEAD_DIM), np.float32)
    oq = np.zeros((1, nh * HEAD_DIM), np.float32)
    ok = np.zeros((1, nh * HEAD_DIM), np.float32)
    for h in range(nh):
        for piece in range(3):
            pq[piece * LANES + h, h * HEAD_DIM + piece] = 1.0
            pk[piece * LANES + h, h * HEAD_DIM + 3 + piece] = -1.0
            oq[0, h * HEAD_DIM + 3 + piece] = 1.0
            ok[0, h * HEAD_DIM + piece] = 1.0
    return jnp.asarray(pq, BF16), jnp.asarray(pk, BF16), jnp.asarray(oq), jnp.asarray(ok)


def _fox_prep_fwd(qkv, fl, bf, qg, kg, seq, name):
    n, d3 = qkv.shape
    d = d3 // 3
    nh = d // HEAD_DIM
    tm = _tile(seq, 256)
    tps = seq // tm
    scale = 1.0 / math.sqrt(HEAD_DIM)
    pq, pk, oq, ok = _bias_placement(nh)

    def body(qkv_ref, fl_ref, bf_ref, qg_ref, kg_ref, pq_ref, pk_ref, oq_ref, ok_ref, q_ref, k_ref, v_ref, carry):
        first = (pl.program_id(0) % tps) == 0
        gm, gt = _head_maps(d)

        def head_norm(xr, gain):
            r = lax.rsqrt(_dot_split(xr * xr, gm) / HEAD_DIM + EPS)
            return xr * _dot_split(r, gt) * gain

        qs = (head_norm(qkv_ref[:, :d], qg_ref[...]).astype(BF16).astype(F32) * scale).astype(BF16)
        kn = head_norm(qkv_ref[:, d:2 * d], kg_ref[...]).astype(BF16)
        v_ref[...] = qkv_ref[:, 2 * d:].astype(BF16)
        z = fl_ref[...] + bf_ref[...]
        logf = jnp.minimum(z, 0.0) - jnp.log1p(jnp.exp(-jnp.abs(z)))
        tri = (lax.broadcasted_iota(jnp.int32, (tm, tm), 0) >= lax.broadcasted_iota(jnp.int32, (tm, tm), 1)).astype(F32)

        @pl.when(first)
        def _():
            carry[...] = jnp.zeros_like(carry)

        c = jnp.dot(tri, logf, precision=HI, preferred_element_type=F32) + carry[...]
        carry[...] = c[tm - 1:tm, :]
        c1 = c.astype(BF16)
        r1 = c - c1.astype(F32)
        c2 = r1.astype(BF16)
        c3 = (r1 - c2.astype(F32)).astype(BF16)
        pieces = jnp.concatenate([c1, c2, c3], axis=1)
        eq = (jnp.dot(pieces, pq_ref[...], preferred_element_type=F32) + oq_ref[...]).astype(BF16)
        ek = (jnp.dot(pieces, pk_ref[...], preferred_element_type=F32) + ok_ref[...]).astype(BF16)
        for h in range(nh):
            lo, hi = h * HEAD_DIM, (h + 1) * HEAD_DIM
            q_ref[:, 2 * lo:2 * lo + HEAD_DIM] = qs[:, lo:hi]
            q_ref[:, 2 * lo + HEAD_DIM:2 * hi] = eq[:, lo:hi]
            k_ref[:, 2 * lo:2 * lo + HEAD_DIM] = kn[:, lo:hi]
            k_ref[:, 2 * lo + HEAD_DIM:2 * hi] = ek[:, lo:hi]

    row = lambda w: pl.BlockSpec((tm, w), lambda i: (i, 0))
    vec = lambda w: pl.BlockSpec((1, w), lambda i: (0, 0))
    full = lambda a: pl.BlockSpec(a.shape, lambda i: (0, 0))
    return pl.pallas_call(
        body, grid=(n // tm,),
        in_specs=[row(d3), row(LANES), vec(LANES), vec(d), vec(d), full(pq), full(pk), full(oq), full(ok)],
        out_specs=[row(2 * d), row(2 * d), row(d)],
        out_shape=[jax.ShapeDtypeStruct((n, 2 * d), BF16)] * 2 + [jax.ShapeDtypeStruct((n, d), BF16)],
        scratch_shapes=[pltpu.VMEM((1, LANES), F32)], name=name, compiler_params=_cp("arbitrary"),
    )(qkv, fl, bf, qg, kg, pq, pk, oq, ok)


def _fox_prep_bwd(qkv, dq, dk, dv, dc1, dc2, fl, bf, qg, kg, seq, name, comm=None):
    n, d3 = qkv.shape
    d = d3 // 3
    tm = _tile(seq, 256)
    tps = seq // tm
    nt = n // tm

    def body(qkv_ref, dq_ref, dk_ref, dv_ref, dc1_ref, dc2_ref, fl_ref, bf_ref, qg_ref, kg_ref,
             dqkv_ref, dfl_ref, dqg_ref, dkg_ref, dbf_ref, carry):
        i = pl.program_id(0)
        tile = nt - 1 - i
        last = (tile % tps) == tps - 1
        gm, gt = _head_maps(d)

        @pl.when(i == 0)
        def _():
            for r_ in (dqg_ref, dkg_ref, dbf_ref):
                r_[...] = jnp.zeros_like(r_)

        @pl.when(last)
        def _():
            carry[...] = jnp.zeros_like(carry)

        def head_norm_bwd(xr, dy, gain, dgain_ref):
            rf = _dot_split(lax.rsqrt(_dot_split(xr * xr, gm) / HEAD_DIM + EPS), gt)
            xn = xr * rf
            dgain_ref[...] += jnp.sum(dy * xn, axis=0, keepdims=True)
            dyg = dy * gain
            mean = _dot_split(dyg * xn, gm) / HEAD_DIM
            return rf * (dyg - xn * _dot_split(mean, gt))

        dqkv_ref[:, :d] = head_norm_bwd(qkv_ref[:, :d], dq_ref[...], qg_ref[...], dqg_ref).astype(BF16)
        dqkv_ref[:, d:2 * d] = head_norm_bwd(qkv_ref[:, d:2 * d], dk_ref[...], kg_ref[...], dkg_ref).astype(BF16)
        dqkv_ref[:, 2 * d:] = dv_ref[...].astype(BF16)

        dc = dc1_ref[...] + dc2_ref[...]
        tri = (lax.broadcasted_iota(jnp.int32, (tm, tm), 0) <= lax.broadcasted_iota(jnp.int32, (tm, tm), 1)).astype(F32)
        dlog = jnp.dot(tri, dc, precision=HI, preferred_element_type=F32) + carry[...]
        carry[...] = dlog[0:1, :]
        dfl = dlog * (1.0 - _sig(fl_ref[...] + bf_ref[...]))
        dfl_ref[...] = dfl.astype(BF16)
        dbf_ref[...] += jnp.sum(dfl, axis=0, keepdims=True)

    row = lambda w: pl.BlockSpec((tm, w), lambda i: (nt - 1 - i, 0))
    vec = lambda w: pl.BlockSpec((1, w), lambda i: (0, 0))
    return _hosted(
        body, comm, grid=(nt,),
        in_specs=[row(d3), row(d), row(d), row(d), row(LANES), row(LANES), row(LANES), vec(LANES), vec(d), vec(d)],
        out_specs=[row(d3), row(LANES), vec(d), vec(d), vec(LANES)],
        out_shape=[jax.ShapeDtypeStruct((n, d3), BF16), jax.ShapeDtypeStruct((n, LANES), BF16),
                   jax.ShapeDtypeStruct((1, d), F32), jax.ShapeDtypeStruct((1, d), F32),
                   jax.ShapeDtypeStruct((1, LANES), F32)],
        scratch_shapes=[pltpu.VMEM((1, LANES), F32)], sem=("arbitrary",), name=name,
        ins=(qkv, dq, dk, dv, dc1, dc2, fl, bf, qg, kg))


def _attn_specs(bsz, seq, t):
    nb = seq // t
    blk = lambda w: pl.BlockSpec((t, w), lambda b, h, i: (b * nb + i, h))
    full = lambda w: pl.BlockSpec((seq, w), lambda b, h, i: (b, h))
    col = pl.BlockSpec((None, None, t, 2), lambda b, h, i: (b, h, i, 0))
    rows = pl.BlockSpec((None, None, nb, 2, t), lambda b, h, i: (b, h, 0, 0, 0))
    return nb, blk, full, col, rows


_NT = (((1,), (1,)), ((), ()))
ATTN_TILE = 512


def _head_lanes(t, hh):
    lane = lax.broadcasted_iota(jnp.int32, (t, LANES), 1)
    return (lane < HEAD_DIM) if hh == 0 else (lane >= HEAD_DIM)


def _flash_fwd(qa, ka, v, bsz, seq, name):
    n, d = v.shape
    hp = d // LANES
    t = _tile(seq, ATTN_TILE)
    nb, blk, full, col, _ = _attn_specs(bsz, seq, t)

    def body(q_ref, k_ref, v_ref, o_ref, lse_ref):
        i = pl.program_id(2)
        causal = lax.broadcasted_iota(jnp.int32, (t, t), 0) >= lax.broadcasted_iota(jnp.int32, (t, t), 1)

        def block(j, carry, masked):
            rs = pl.ds(pl.multiple_of(j * t, t), t)
            vj = v_ref[rs, :]
            out = []
            for hh in range(2):
                m, l, acc = carry[hh]
                hs = slice(hh * LANES, (hh + 1) * LANES)
                sc = lax.dot_general(q_ref[:, hs], k_ref[rs, hs], _NT, preferred_element_type=F32)
                if masked:
                    sc = jnp.where(causal, sc, NEG)
                mn = jnp.maximum(m, jnp.max(sc, axis=-1, keepdims=True))
                p = jnp.exp(sc - mn)
                al = jnp.exp(m - mn)
                l = al * l + jnp.sum(p, axis=-1, keepdims=True)
                acc = al * acc + jnp.dot(p.astype(BF16), vj, preferred_element_type=F32)
                out.append((mn, l, acc))
            return tuple(out)

        init = tuple((jnp.full((t, 1), NEG, F32), jnp.zeros((t, 1), F32), jnp.zeros((t, LANES), F32))
                     for _ in range(2))
        carry = lax.fori_loop(0, i, lambda j, c: block(j, c, False), init)
        (m0, l0, a0), (m1, l1, a1) = block(i, carry, True)
        o_ref[...] = jnp.where(_head_lanes(t, 0), a0 / l0, a1 / l1)
        lse_ref[:, 0:1] = m0 + jnp.log(l0)
        lse_ref[:, 1:2] = m1 + jnp.log(l1)

    return pl.pallas_call(
        body, grid=(bsz, hp, nb), in_specs=[blk(2 * LANES), full(2 * LANES), full(LANES)],
        out_specs=[blk(LANES), col],
        out_shape=[jax.ShapeDtypeStruct((n, d), F32), jax.ShapeDtypeStruct((bsz, hp, seq, 2), F32)],
        name=name, compiler_params=_cp("parallel", "parallel", "arbitrary"),
    )(qa, ka, v)


def _flash_bwd(qa, ka, v, do, o, lse_row, bsz, seq, name):
    n, d = v.shape
    hp = d // LANES
    t = _tile(seq, ATTN_TILE)
    nb, blk, full, col, rows = _attn_specs(bsz, seq, t)
    scale = 1.0 / math.sqrt(HEAD_DIM)
    tn_ = (((0,), (0,)), ((), ()))

    def body(k_ref, v_ref, q_ref, do_ref, o_ref, lse_ref, dq_ref, dcc_ref, dk_ref, dv_ref, dck_ref, dqa, dl):
        j = pl.program_id(2)
        causal = lax.broadcasted_iota(jnp.int32, (t, t), 1) >= lax.broadcasted_iota(jnp.int32, (t, t), 0)
        heads = [_head_lanes(t, 0), _head_lanes(t, 1)]

        @pl.when(j == 0)
        def _():
            dqa[...] = jnp.zeros_like(dqa)
            ones = jnp.ones((8, LANES), F32)
            for ib in range(nb):
                rs = slice(ib * t, (ib + 1) * t)
                prod = do_ref[rs, :] * o_ref[rs, :]
                for hh in range(2):
                    dl[ib, hh:hh + 1, :] = lax.dot_general(ones, jnp.where(heads[hh], prod, 0.0), _NT, precision=HI,
                                                           preferred_element_type=F32)[0:1]

        vj = v_ref[...]

        def block(i, carry, masked):
            rs = pl.ds(pl.multiple_of(i * t, t), t)
            doi = do_ref[rs, :]
            dks, dvp = list(carry[:2]), carry[2]
            for hh in range(2):
                hs = slice(hh * LANES, (hh + 1) * LANES)
                kh, qi = k_ref[:, hs], q_ref[rs, hs]
                dom = jnp.where(heads[hh], doi, 0.0).astype(BF16)
                st = lax.dot_general(kh, qi, _NT, preferred_element_type=F32)
                if masked:
                    st = jnp.where(causal, st, NEG)
                pt = jnp.exp(st - lse_ref[i, hh:hh + 1, :])
                dvp = dvp + jnp.dot(pt.astype(BF16), dom, preferred_element_type=F32)
                dpt = lax.dot_general(vj, dom, _NT, preferred_element_type=F32)
                dsb = (pt * (dpt - dl[i, hh:hh + 1, :])).astype(BF16)
                dks[hh] = dks[hh] + jnp.dot(dsb, qi, preferred_element_type=F32)
                dqa[rs, hs] += lax.dot_general(dsb, kh, tn_, preferred_element_type=F32)
            return dks[0], dks[1], dvp

        zero = jnp.zeros((t, LANES), F32)
        carry = block(j, (zero, zero, zero), True)
        dk0, dk1, dvp = lax.fori_loop(j + 1, nb, lambda i, c: block(i, c, False), carry)
        dk_ref[...] = jnp.where(heads[0], dk0, pltpu.roll(dk1, HEAD_DIM, 1))
        dv_ref[...] = dvp
        dck_ref[:, 0:1] = -dk0[:, HEAD_DIM + 3:HEAD_DIM + 4]
        dck_ref[:, 1:2] = -dk1[:, HEAD_DIM + 3:HEAD_DIM + 4]

        @pl.when(j == nb - 1)
        def _():
            first = lax.broadcasted_iota(jnp.int32, (seq, LANES), 1) < HEAD_DIM
            dq_ref[...] = jnp.where(first, dqa[:, :LANES], pltpu.roll(dqa[:, LANES:], HEAD_DIM, 1)) * scale
            for hh in range(2):
                lo = hh * LANES + HEAD_DIM
                dcc_ref[:, hh:hh + 1] = dqa[:, lo:lo + 1]

    whole_col = pl.BlockSpec((None, None, seq, 2), lambda b, h, i: (b, h, 0, 0))
    cshape = jax.ShapeDtypeStruct((bsz, hp, seq, 2), F32)
    nd = jax.ShapeDtypeStruct((n, d), F32)
    return pl.pallas_call(
        body, grid=(bsz, hp, nb),
        in_specs=[blk(2 * LANES), blk(LANES), full(2 * LANES), full(LANES), full(LANES), rows],
        out_specs=[full(LANES), whole_col, blk(LANES), blk(LANES), col],
        out_shape=[nd, cshape, nd, nd, cshape],
        scratch_shapes=[pltpu.VMEM((seq, 2 * LANES), F32), pltpu.VMEM((nb, 2, t), F32)],
        name=name, compiler_params=_cp("parallel", "parallel", "arbitrary"),
    )(ka, v, qa, do, o, lse_row)


def _adamw(w, g, m, v, name):
    r, c = w.shape
    tr = r
    for cand in (512, 256, 128, 64, 32, 16, 8):
        if r % cand == 0 and r > cand and cand * c * 4 <= 4 * 1024 * 1024:
            tr = cand
            break

    def body(w_ref, g_ref, m_ref, v_ref, d_ref, m2_ref, v2_ref):
        gv = g_ref[...]
        m2 = ADAM_B1 * m_ref[...] + (1.0 - ADAM_B1) * gv
        v2 = ADAM_B2 * v_ref[...] + (1.0 - ADAM_B2) * jnp.square(gv)
        m_hat = m2 / (1.0 - ADAM_B1 ** ADAM_STEP)
        v_hat = v2 / (1.0 - ADAM_B2 ** ADAM_STEP)
        d_ref[...] = -ADAM_LR * (m_hat / (jnp.sqrt(v_hat) + ADAM_EPS) + ADAM_WD * w_ref[...])
        m2_ref[...] = m2
        v2_ref[...] = v2

    blk = pl.BlockSpec((tr, c), lambda i: (i, 0))
    shp = jax.ShapeDtypeStruct((r, c), F32)
    return pl.pallas_call(
        body, grid=(r // tr,), in_specs=[blk] * 4, out_specs=[blk] * 3, out_shape=[shp] * 3,
        name=name, compiler_params=_cp("parallel"),
    )(w, g, m, v)


def _adamw_nd(w, g, m, v, name):
    shape = w.shape
    two = (math.prod(shape[:-1]), shape[-1])
    d_, m_, v_ = _adamw(w.reshape(two), g.reshape(two), m.reshape(two), v.reshape(two), name)
    return g.reshape(shape), d_.reshape(shape), m_.reshape(shape), v_.reshape(shape)


def _place():
    x, y, c = lax.axis_index("x"), lax.axis_index("y"), lax.axis_index("c")
    chips = [(1 - x, y), (x, 1 - y), (1 - x, 1 - y)]
    return x, y, c, chips


def _gather_chips(shards):
    nt = len(shards)
    halves = [s.shape[0] // 2 for s in shards]

    def copies(ins, outs, send_sems, recv_sems):
        x, y, c, chips = _place()
        cps = []
        for t in range(nt):
            rows = pl.ds(c * halves[t], halves[t])
            for jj, (cx, cy) in enumerate(chips):
                cps.append(pltpu.make_async_remote_copy(
                    src_ref=ins[t].at[rows, :], dst_ref=outs[t].at[2 * x + y, rows, :], send_sem=send_sems.at[3 * t + jj],
                    recv_sem=recv_sems.at[3 * t + jj], device_id=(cx, cy, c), device_id_type=MESH))
        return cps

    def start(*refs):
        for cp in copies(*refs):
            cp.start()

    def wait(*refs):
        for cp in copies(*refs):
            cp.wait()

    return _Comm(shards, [jax.ShapeDtypeStruct((N_CHIPS,) + s.shape, s.dtype) for s in shards], 3 * nt, start, wait)


def _gather_sibling(shards, bufs, name):
    nt = len(shards)
    halves = [s.shape[0] // 2 for s in shards]

    def body(*refs):
        ins, outs = refs[:nt], refs[2 * nt:3 * nt]
        send_sems, recv_sems = refs[3 * nt:]
        x, y, c, chips = _place()
        sibling = (x, y, 1 - c)

        def copy(t, jj, hf):
            cx, cy = chips[jj]
            region = outs[t].at[2 * cx + cy, pl.ds(hf * halves[t], halves[t]), :]
            return pltpu.make_async_remote_copy(src_ref=region, dst_ref=region, send_sem=send_sems.at[4 * t + jj],
                                                recv_sem=recv_sems.at[4 * t + jj], device_id=sibling,
                                                device_id_type=MESH)

        def own(t):
            return pltpu.make_async_remote_copy(src_ref=ins[t], dst_ref=outs[t].at[2 * x + y],
                                                send_sem=send_sems.at[4 * t + 3], recv_sem=recv_sems.at[4 * t + 3],
                                                device_id=sibling, device_id_type=MESH)

        sends = [copy(t, jj, c) for t in range(nt) for jj in range(3)] + [own(t) for t in range(nt)]
        for cp in sends:
            cp.start()
        for t in range(nt):
            for jj in range(3):
                copy(t, jj, 1 - c).wait_recv()
            own(t).wait_recv()
        for cp in sends:
            cp.wait_send()

    return pl.pallas_call(
        body, in_specs=[ANY] * (2 * nt), out_specs=[ANY] * nt,
        out_shape=[jax.ShapeDtypeStruct(b.shape, b.dtype) for b in bufs],
        scratch_shapes=[pltpu.SemaphoreType.DMA((4 * nt,)), pltpu.SemaphoreType.DMA((4 * nt,))],
        input_output_aliases={nt + t: t for t in range(nt)}, name=name,
        compiler_params=pltpu.CompilerParams(has_side_effects=True),
    )(*shards, *bufs)


N_PARTIALS = 7


def _scatter_partials(grads):
    nt = len(grads)
    halves = [g.shape[1] // 2 for g in grads]

    def copies(ins, outs, send_sems, recv_sems):
        x, y, c, chips = _place()
        cps = []

        def copy(t, kk, slot, core, to):
            rows = pl.ds(core * halves[t], halves[t])
            return pltpu.make_async_remote_copy(
                src_ref=ins[t].at[slot, rows, :], dst_ref=outs[t].at[kk], send_sem=send_sems.at[N_PARTIALS * t + kk],
                recv_sem=recv_sems.at[N_PARTIALS * t + kk], device_id=to, device_id_type=MESH)

        for t in range(nt):
            for jj, (cx, cy) in enumerate(chips):
                cps.append(copy(t, 2 * jj, 2 * cx + cy, c, (cx, cy, c)))
                cps.append(copy(t, 2 * jj + 1, 2 * cx + cy, 1 - c, (cx, cy, 1 - c)))
            cps.append(copy(t, N_PARTIALS - 1, 2 * x + y, 1 - c, (x, y, 1 - c)))
        return cps

    def start(*refs):
        for cp in copies(*refs):
            cp.start()

    def wait(*refs):
        for cp in copies(*refs):
            cp.wait()

    return _Comm(grads, [jax.ShapeDtypeStruct((N_PARTIALS, h, g.shape[2]), g.dtype) for g, h in zip(grads, halves)],
                 N_PARTIALS * nt, start, wait)


def _add_partials(g, got, place_idx, name):
    s, r, w = g.shape
    rh = r // 2
    tr = rh
    for cand in (256, 128, 64, 32, 16):
        if rh % cand == 0 and cand * w * 4 <= 2 * 1024 * 1024:
            tr = cand
            break
    per = rh // tr

    def body(b_ref, g_ref, *rest):
        o_ref = rest[-1]
        acc = g_ref[...].astype(F32)
        for r_ref in rest[:-1]:
            acc = acc + r_ref[...].astype(F32)
        o_ref[...] = acc

    return pl.pallas_call(
        body,
        grid_spec=pltpu.PrefetchScalarGridSpec(
            num_scalar_prefetch=1, grid=(per,),
            in_specs=[pl.BlockSpec((None, tr, w), lambda a, b_ref: (b_ref[0], b_ref[1] * per + a, 0))]
            + [pl.BlockSpec((None, tr, w), lambda a, b_ref, kk=kk: (kk, a, 0)) for kk in range(N_PARTIALS)],
            out_specs=pl.BlockSpec((tr, w), lambda a, b_ref: (b_ref[1] * per + a, 0))),
        out_shape=jax.ShapeDtypeStruct((r, w), F32), name=name, compiler_params=_cp("parallel"),
    )(place_idx, g, *([got] * N_PARTIALS))


def _swap_halves(bufs, name):
    nt = len(bufs)

    def body(*refs):
        outs = refs[nt:2 * nt]
        send_sems, recv_sems = refs[2 * nt:]
        x, y, c, _ = _place()

        def copy(t, hf):
            rh = outs[t].shape[0] // 2
            region = outs[t].at[pl.ds(hf * rh, rh), :]
            return pltpu.make_async_remote_copy(src_ref=region, dst_ref=region, send_sem=send_sems.at[t],
                                                recv_sem=recv_sems.at[t], device_id=(x, y, 1 - c),
                                                device_id_type=MESH)

        sends = [copy(t, c) for t in range(nt)]
        for cp in sends:
            cp.start()
        for t in range(nt):
            copy(t, 1 - c).wait_recv()
        for cp in sends:
            cp.wait_send()

    return pl.pallas_call(
        body, in_specs=[ANY] * nt, out_specs=[ANY] * nt,
        out_shape=[jax.ShapeDtypeStruct(b.shape, b.dtype) for b in bufs],
        scratch_shapes=[pltpu.SemaphoreType.DMA((nt,)), pltpu.SemaphoreType.DMA((nt,))],
        input_output_aliases={t: t for t in range(nt)}, name=name,
        compiler_params=pltpu.CompilerParams(has_side_effects=True),
    )(*bufs)


def _rs_finish(grads, got, place_idx, tag):
    fin = [_add_partials(g, r, place_idx, name=f"rs_add_{tag}_{t}") for t, (g, r) in enumerate(zip(grads, got))]
    return _swap_halves(fin, name=f"rs_swap_{tag}")


def _all_reduce_small(v, name):
    r, w = v.shape

    def body(v_ref, o_ref, buf, send_sems, recv_sems):
        x, y, c, _ = _place()
        me = 4 * x + 2 * y + c
        buf[me] = v_ref[...]
        cps = []
        for kk in range(1, 8):
            peer = (x ^ ((kk >> 2) & 1), y ^ ((kk >> 1) & 1), c ^ (kk & 1))
            cps.append(pltpu.make_async_remote_copy(src_ref=v_ref, dst_ref=buf.at[me], send_sem=send_sems.at[kk - 1],
                                                    recv_sem=recv_sems.at[kk - 1], device_id=peer, device_id_type=MESH))
        for cp in cps:
            cp.start()
        for kk in range(1, 8):
            pltpu.make_async_remote_copy(src_ref=v_ref, dst_ref=buf.at[me ^ kk], send_sem=send_sems.at[kk - 1],
                                         recv_sem=recv_sems.at[kk - 1], device_id=(x, y, c),
                                         device_id_type=MESH).wait_recv()
        for cp in cps:
            cp.wait_send()
        acc = buf[0]
        for dev in range(1, 8):
            acc = acc + buf[dev]
        o_ref[...] = acc

    vm = pl.BlockSpec(memory_space=pltpu.VMEM)
    return pl.pallas_call(
        body, in_specs=[vm], out_specs=vm, out_shape=jax.ShapeDtypeStruct((r, w), F32),
        scratch_shapes=[pltpu.VMEM((8, r, w), F32), pltpu.SemaphoreType.DMA((7,)), pltpu.SemaphoreType.DMA((7,))],
        name=name, compiler_params=pltpu.CompilerParams(has_side_effects=True),
    )(v)


def _to_shards(a, axis=-1):
    axis = axis % a.ndim
    shp = a.shape
    a = a.reshape(shp[:axis] + (N_CHIPS, shp[axis] // N_CHIPS) + shp[axis + 1:])
    return jnp.moveaxis(a, axis, 0).reshape(N_CHIPS, -1)


def _from_shards(s, shard_shape, axis=-1):
    axis = axis % len(shard_shape)
    a = jnp.moveaxis(s.reshape((N_CHIPS,) + tuple(shard_shape)), 0, axis)
    return a.reshape(tuple(shard_shape[:axis]) + (N_CHIPS * shard_shape[axis],) + tuple(shard_shape[axis + 1:]))


def _pack(vecs, rows):
    flat = jnp.concatenate([v.reshape(v.shape[0], -1) if v.ndim > 1 else v.reshape(1, -1) for v in vecs], axis=1)
    lead = flat.shape[0]
    flat = jnp.pad(flat, ((0, 0), (0, rows * LANES - flat.shape[1])))
    return flat.reshape(lead, rows, LANES)


def _pack_rows(sizes, mult):
    total = sum(sizes)
    rows = -(-total // LANES)
    return -(-rows // mult) * mult


def _unpack(flat, shapes):
    out, pos = [], 0
    for shp in shapes:
        sz = math.prod(shp)
        out.append(flat[..., pos:pos + sz].reshape(flat.shape[:-1] + tuple(shp)))
        pos += sz
    return out


def _row_layout(col, t):
    bsz, hp, seq, _ = col.shape
    return col.reshape(bsz, hp, seq // t, t, 2).transpose(0, 1, 2, 4, 3)


def _from_col_layout(col):
    bsz, hp, seq, _ = col.shape
    a = col.transpose(0, 2, 1, 3).reshape(bsz * seq, 2 * hp)
    return jnp.pad(a, ((0, 0), (0, LANES - 2 * hp)))


def kernel(x, norm_mix, norm_ffn, conv_w_in, conv_b_in, conv_dw, conv_dw_b, conv_ln_g, conv_ln_b, conv_w_out, conv_b_out, pool_w, pool_b, pool_scale, fox_w_in, fox_b_f, fox_q_gain, fox_k_gain, fox_w_o, ffn_w_up, ffn_dw, ffn_dw_b, ffn_w_down, loss_target, m_norm_mix, m_norm_ffn, m_conv_w_in, m_conv_b_in, m_conv_dw, m_conv_dw_b, m_conv_ln_g, m_conv_ln_b, m_conv_w_out, m_conv_b_out, m_pool_w, m_pool_b, m_pool_scale, m_fox_w_in, m_fox_b_f, m_fox_q_gain, m_fox_k_gain, m_fox_w_o, m_ffn_w_up, m_ffn_dw, m_ffn_dw_b, m_ffn_w_down, v_norm_mix, v_norm_ffn, v_conv_w_in, v_conv_b_in, v_conv_dw, v_conv_dw_b, v_conv_ln_g, v_conv_ln_b, v_conv_w_out, v_conv_b_out, v_pool_w, v_pool_b, v_pool_scale, v_fox_w_in, v_fox_b_f, v_fox_q_gain, v_fox_k_gain, v_fox_w_o, v_ffn_w_up, v_ffn_dw, v_ffn_dw_b, v_ffn_w_down):
    bsz, seq, d = x.shape
    n = bsz * seq
    depth = norm_mix.shape[0]
    n_conv, n_pool, n_fox = conv_w_in.shape[0], pool_w.shape[0], fox_w_in.shape[0]
    f2 = ffn_dw_b.shape[1]
    f = f2 // 2
    nh = d // HEAD_DIM
    hp = d // LANES
    ng = len(POOL_WINDOWS)
    gd = d // ng
    chip_idx = jnp.stack([2 * lax.axis_index("x") + lax.axis_index("y"), lax.axis_index("c")]).astype(jnp.int32)

    small_shapes = [conv_b_in.shape, conv_dw.shape, conv_dw_b.shape, conv_ln_g.shape, conv_ln_b.shape,
                    conv_b_out.shape, pool_b.shape, ffn_dw.shape]
    small_rows = _pack_rows([math.prod(s) for s in small_shapes], 16)
    small = _pack([v.reshape(1, -1) for v in (conv_b_in, conv_dw, conv_dw_b, conv_ln_g, conv_ln_b, conv_b_out,
                                                pool_b, ffn_dw)], small_rows)[0]
    def layer_shards(i):
        kind, j = i % 3, i // 3
        shards = [ffn_w_up[i].astype(BF16), ffn_w_down[i].astype(BF16)]
        if kind == 0:
            shards += [conv_w_in[j].astype(BF16), conv_w_out[j].astype(BF16)]
        elif kind == 1:
            shards += [pool_w[j].reshape(ng * (gd // N_CHIPS), gd).astype(BF16)]
        else:
            shards += [fox_w_in[j].astype(BF16), fox_w_o[j].astype(BF16)]
        if i == 0:
            shards.append(small)
        return shards

    gathered = [None] * depth
    first_now = layer_shards(0)[2:]
    gathered[0] = [None, None] + list(_gather_sibling(
        first_now, _run_comm(_gather_chips(first_now), name="gather_chips_l0"), name="gather_sibling_l0"))
    small_all = gathered[0][-1].reshape(N_CHIPS, -1)
    sm = _unpack(small_all, small_shapes)
    axes = [-1] * 8
    b_in_f, dw_f, dw_b_f, ln_g_f, ln_b_f, b_out_f, pool_b_f, ffn_dw_f = [
        _from_shards(s_.reshape(N_CHIPS, -1), shp, ax) for s_, shp, ax in zip(sm, small_shapes, axes)]

    xs = x.reshape(n, d)
    tgt = loss_target.reshape(n, d)
    vec = lambda a: a.reshape(1, -1)

    saved = []
    cur = xs
    for i in range(depth):
        kind, j = i % 3, i // 3
        wts = gathered[i]
        sv = {"x_in": cur}
        if kind == 0:
            w_in, w_out = wts[2], wts[3].reshape(d, d)
            wcol = w_in.shape[2]
            h = _rms_fwd(cur, vec(norm_mix[i]), name=f"rms_mix_l{i}")
            pa = _mm(h, w_in, m=n, n=d, k=d, tk=d, tn=wcol, b_stk=wcol, b_s0=0, bias=vec(b_in_f[j, :d]),
                     name=f"conv_in_a_l{i}")
            pg = _mm(h, w_in, m=n, n=d, k=d, tk=d, tn=wcol, b_stk=wcol, b_s0=2, bias=vec(b_in_f[j, d:]),
                     name=f"conv_in_g_l{i}")
            taps = jnp.pad(dw_f[j], ((0, CONV_HALO - CONV_WIDTH), (0, 0)))
            conf_args = (pa, pg, taps, vec(dw_b_f[j]), vec(ln_g_f[j]), vec(ln_b_f[j]), seq)
            if i == 0:
                ffn_shards = layer_shards(0)[:2]
                (u, s_), landed = _conf_fwd(*conf_args, name=f"conf_fwd_l{i}", comm=_gather_chips(ffn_shards))
                gathered[0][:2] = _gather_sibling(ffn_shards, landed, name="gather_sibling_ffn_l0")
            else:
                u, s_ = _conf_fwd(*conf_args, name=f"conf_fwd_l{i}")
            cur = _mm(s_, w_out, m=n, n=d, k=d, tk=d, bias=vec(b_out_f[j]), res=cur, name=f"conv_out_l{i}")
            sv.update(h=h, pa=pa, pg=pg, u=u, s=s_, taps=taps)
        elif kind == 1:
            pw = wts[2].reshape(N_CHIPS, ng, gd // N_CHIPS, gd).transpose(1, 0, 2, 3).reshape(ng, gd, gd)
            cur, p = _pool_fwd(cur, vec(norm_mix[i]), pw, vec(pool_b_f[j]), vec(pool_scale[j]), seq,
                               name=f"pool_fwd_l{i}")
            sv.update(p=p, pw=pw)
        else:
            w_in = wts[2].transpose(1, 0, 2).reshape(d, -1)
            w_qkv = w_in[:, :3 * d]
            w_f = jnp.pad(w_in[:, 3 * d:], ((0, 0), (0, LANES - nh)))
            w_o = wts[3].reshape(d, d)
            bf = jnp.pad(vec(fox_b_f[j]), ((0, 0), (0, LANES - nh)))
            qg, kg = jnp.tile(vec(fox_q_gain[j]), (1, nh)), jnp.tile(vec(fox_k_gain[j]), (1, nh))
            h = _rms_fwd(cur, vec(norm_mix[i]), name=f"rms_mix_l{i}")
            qkv = _mm(h, w_qkv, m=n, n=3 * d, k=d, tk=d, tn=d, name=f"fox_qkv_l{i}")
            fl = _mm(h, w_f, m=n, n=LANES, k=d, tk=d, name=f"fox_fl_l{i}")
            qa, ka, v = _fox_prep_fwd(qkv, fl, bf, qg, kg, seq, name=f"fox_prep_l{i}")
            o, lse = _flash_fwd(qa, ka, v, bsz, seq, name=f"fox_attn_l{i}")
            cur = _mm(o, w_o, m=n, n=d, k=d, tk=d, res=cur, name=f"fox_out_l{i}")
            sv.update(h=h, qkv=qkv, fl=fl, qa=qa, ka=ka, v=v, o=o, lse=lse, w_qkv=w_qkv, w_f=w_f, w_o=w_o, bf=bf,
                      qg=qg, kg=kg)
        w_up, w_down = wts[0], wts[1].reshape(f, d)
        ucol = w_up.shape[2]
        sv["x_mid"] = cur
        h2 = _rms_fwd(cur, vec(norm_ffn[i]), name=f"rms_ffn_l{i}")
        uv = _mm(h2, w_up, m=n, n=f, k=d, tk=d, tn=ucol, b_stk=ucol, b_s0=0, name=f"ffn_up_v_l{i}")
        ug = _mm(h2, w_up, m=n, n=f, k=d, tk=d, tn=ucol, b_stk=ucol, b_s0=2, name=f"ffn_up_g_l{i}")
        fdw, fdb = ffn_dw_f[i], ffn_dw_b[i]
        glu_args = (uv, ug, fdw[:, :f], fdw[:, f:], vec(fdb[:f]), vec(fdb[f:]), w_down, cur, seq)
        if i + 1 < depth:
            nxt_shards = layer_shards(i + 1)
            (a_, cur), landed = _ffn_glu_down(*glu_args, name=f"ffn_glu_down_l{i}", comm=_gather_chips(nxt_shards))
            gathered[i + 1] = _gather_sibling(nxt_shards, landed, name=f"gather_sibling_l{i + 1}")
        else:
            a_, cur = _ffn_glu_down(*glu_args, name=f"ffn_glu_down_l{i}")
        sv.update(h2=h2, uv=uv, ug=ug, a=a_)
        saved.append(sv)

    dy, loss_part = _loss(cur, tgt, name="loss")
    loss = lax.psum(loss_part[0, 0], ("x", "y", "c"))

    g_norm_mix, g_norm_ffn = [None] * depth, [None] * depth
    g_ffn_dw_b = [None] * depth
    g_up, g_down = [None] * depth, [None] * depth
    g_conv_in, g_conv_out = [None] * n_conv, [None] * n_conv
    g_pool_w = g_fox_in = g_fox_o = None
    g_pool_scale = g_bf = g_qg = g_kg = None
    part_small = {"b_in": [None] * n_conv, "dw": [None] * n_conv, "dw_b": [None] * n_conv, "ln_g": [None] * n_conv,
                  "ln_b": [None] * n_conv, "b_out": [None] * n_conv, "pool_b": None, "ffn_dw": [None] * depth}

    pending = []
    done = {}

    def take_pending():
        groups = list(pending)
        pending.clear()
        parts = [p_ for g_ in groups for p_ in g_[0]]
        return groups, (_scatter_partials(parts) if parts else None)

    def finish_groups(groups, r2):
        pos = 0
        for parts, tag, sink in groups:
            sink(_rs_finish(parts, r2[pos:pos + len(parts)], chip_idx, tag))
            pos += len(parts)

    def carried(fn, args, name):
        groups, comm = take_pending()
        if comm is None:
            return fn(*args, name=name)
        outs, r2 = fn(*args, name=name, comm=comm)
        finish_groups(groups, r2)
        return outs

    dcur = dy
    for i in reversed(range(depth)):
        kind, j = i % 3, i // 3
        wts, sv = gathered[i], saved[i]
        w_up, w_down = wts[0], wts[1].reshape(f, d)
        ucol = w_up.shape[2]
        fdw, fdb = ffn_dw_f[i], ffn_dw_b[i]
        dw_down = _mm(sv["a"], dcur, m=f, n=d, k=n, tm=f // 2, tn=d, tk=2048, ta=True, out_dtype=BF16, name=f"ffn_down_dw_l{i}")
        duv, dug, dwv, dwg, dbv, dbg = carried(
            _ffn_glu_bwd, (sv["uv"], sv["ug"], dcur, w_down, fdw[:, :f], fdw[:, f:], vec(fdb[:f]), vec(fdb[f:]), seq),
            name=f"ffn_glu_bwd_l{i}")
        dw_up = _mm(sv["h2"], duv, m=d, n=f, k=n, tm=d, tn=ucol, tk=2048, ta=True, out_dtype=BF16, o_stk=ucol, o_s0=0,
                    o_slots=N_CHIPS, name=f"ffn_up_dw_v_l{i}")
        dw_up = _mm(sv["h2"], dug, m=d, n=f, k=n, tm=d, tn=ucol, tk=2048, ta=True, out_dtype=BF16, o_stk=ucol, o_s0=2,
                    o_slots=N_CHIPS, o_buf=dw_up, name=f"ffn_up_dw_g_l{i}")
        dh2 = _mm(duv, w_up, m=n, n=d, k=f, tn=d, tk=f, tb=True, b_stk=ucol, b_s0=0, name=f"ffn_up_dx_v_l{i}")
        dh2 = _mm(dug, w_up, m=n, n=d, k=f, tn=d, tk=f, tb=True, b_stk=ucol, b_s0=2, res=dh2,
                  name=f"ffn_up_dx_g_l{i}")
        dmid, g_norm_ffn[i] = _rms_bwd(sv["x_mid"], dh2, vec(norm_ffn[i]), dcur, name=f"rms_ffn_bwd_l{i}")
        part_small["ffn_dw"][i] = jnp.concatenate([dwv, dwg], axis=1)
        g_ffn_dw_b[i] = jnp.concatenate([dbv, dbg], axis=1)

        def ffn_sink(red, i=i):
            g_up[i], g_down[i] = red[0], red[1]

        pending.append(([dw_up, dw_down.reshape(N_CHIPS, f // N_CHIPS, d)], f"ffn_l{i}", ffn_sink))

        if kind == 0:
            w_in, w_out = wts[2], wts[3].reshape(d, d)
            wcol = w_in.shape[2]
            ds = _mm(dmid, w_out, m=n, n=d, k=d, tk=d, tb=True, name=f"conv_out_dx_l{i}")
            dw_out = _mm(sv["s"], dmid, m=d, n=d, k=n, tm=d, tn=d, tk=2048, ta=True, out_dtype=BF16, name=f"conv_out_dw_l{i}")
            dpa, dpg, ddw, ddwb, dlng, dlnb, dba, dbg_ = carried(
                _conf_bwd, (sv["u"], ds, sv["pa"], sv["pg"], sv["taps"], vec(ln_g_f[j]), vec(ln_b_f[j]), seq),
                name=f"conf_bwd_l{i}")
            dw_in = _mm(sv["h"], dpa, m=d, n=d, k=n, tm=d, tn=wcol, tk=2048, ta=True, out_dtype=BF16, o_stk=wcol, o_s0=0,
                        o_slots=N_CHIPS, name=f"conv_in_dw_a_l{i}")
            dw_in = _mm(sv["h"], dpg, m=d, n=d, k=n, tm=d, tn=wcol, tk=2048, ta=True, out_dtype=BF16, o_stk=wcol, o_s0=2,
                        o_slots=N_CHIPS, o_buf=dw_in, name=f"conv_in_dw_g_l{i}")
            dh = _mm(dpa, w_in, m=n, n=d, k=d, tn=d, tk=d, tb=True, b_stk=wcol, b_s0=0, name=f"conv_in_dx_a_l{i}")
            dh = _mm(dpg, w_in, m=n, n=d, k=d, tn=d, tk=d, tb=True, b_stk=wcol, b_s0=2, res=dh,
                     name=f"conv_in_dx_g_l{i}")
            dcur, g_norm_mix[i], db_out = _rms_bwd(sv["x_in"], dh, vec(norm_mix[i]), dmid, name=f"rms_mix_bwd_l{i}",
                                                   colsum=True)
            part_small["b_in"][j] = jnp.concatenate([dba, dbg_], axis=1)
            part_small["dw"][j] = ddw[:CONV_WIDTH]
            part_small["dw_b"][j], part_small["ln_g"][j], part_small["ln_b"][j] = ddwb, dlng, dlnb
            part_small["b_out"][j] = db_out
            mix_grads = [dw_in, dw_out.reshape(N_CHIPS, d // N_CHIPS, d)]
        elif kind == 1:
            dcur, g_norm_mix[i], dpw, dpb, dpsc = _pool_bwd(dmid, sv["x_in"], sv["p"], vec(norm_mix[i]), sv["pw"],
                                                            vec(pool_b_f[j]), vec(pool_scale[j]), seq,
                                                            name=f"pool_bwd_l{i}")
            part_small["pool_b"] = dpb
            g_pool_scale = dpsc
            mix_grads = [dpw.reshape(ng, N_CHIPS, gd // N_CHIPS, gd).transpose(1, 0, 2, 3).reshape(N_CHIPS, gd, gd)
                         .astype(BF16)]
        else:
            do = _mm(dmid, sv["w_o"], m=n, n=d, k=d, tk=d, tb=True, name=f"fox_out_dx_l{i}")
            dw_o = _mm(sv["o"], dmid, m=d, n=d, k=n, tm=d, tn=d, tk=2048, ta=True, out_dtype=BF16, name=f"fox_out_dw_l{i}")
            dq, dcc, dk, dv, dck = _flash_bwd(sv["qa"], sv["ka"], sv["v"], do, sv["o"],
                                              _row_layout(sv["lse"], _tile(seq, ATTN_TILE)), bsz, seq,
                                              name=f"fox_attn_bwd_l{i}")
            dqkv, dfl, dqg, dkg, dbf = _fox_prep_bwd(sv["qkv"], dq, dk, dv, _from_col_layout(dcc), _from_col_layout(dck),
                                                     sv["fl"], sv["bf"], sv["qg"], sv["kg"], seq,
                                                     name=f"fox_prep_bwd_l{i}")
            dw_qkv = _mm(sv["h"], dqkv, m=d, n=3 * d, k=n, tm=d, tn=d, tk=2048, ta=True, out_dtype=BF16, name=f"fox_qkv_dw_l{i}")
            dw_f = _mm(sv["h"], dfl, m=d, n=LANES, k=n, tm=d, tk=2048, ta=True, out_dtype=BF16, name=f"fox_fl_dw_l{i}")
            dh = _mm(dqkv, sv["w_qkv"], m=n, n=d, k=3 * d, tn=d, tk=d, tb=True, name=f"fox_qkv_dx_l{i}")
            dh = _mm(dfl, sv["w_f"], m=n, n=d, k=LANES, tn=d, tb=True, res=dh, name=f"fox_fl_dx_l{i}")
            dcur, g_norm_mix[i] = _rms_bwd(sv["x_in"], dh, vec(norm_mix[i]), dmid, name=f"rms_mix_bwd_l{i}")
            g_bf = dbf[:, :nh]
            g_qg = dqg.reshape(nh, HEAD_DIM).sum(axis=0, keepdims=True)
            g_kg = dkg.reshape(nh, HEAD_DIM).sum(axis=0, keepdims=True)
            dw_in_full = jnp.concatenate([dw_qkv, dw_f[:, :nh]], axis=1)
            wshard = dw_in_full.shape[1] // N_CHIPS
            mix_grads = [dw_in_full.reshape(d, N_CHIPS, wshard).transpose(1, 0, 2),
                         dw_o.reshape(N_CHIPS, d // N_CHIPS, d)]
        if i == 0:
            sm_parts = [jnp.concatenate(part_small["b_in"]), jnp.stack(part_small["dw"]),
                        jnp.concatenate(part_small["dw_b"]), jnp.concatenate(part_small["ln_g"]),
                        jnp.concatenate(part_small["ln_b"]), jnp.concatenate(part_small["b_out"]),
                        part_small["pool_b"].reshape(n_pool, ng, gd), jnp.stack(part_small["ffn_dw"])]
            mix_grads.append(_pack([_to_shards(p_) for p_ in sm_parts], small_rows))

        def mix_sink(red, i=i, kind=kind, j=j):
            if kind == 0:
                g_conv_in[j], g_conv_out[j] = red[0], red[1]
            elif kind == 1:
                done["pool_w"] = red[0]
            else:
                done["fox_in"], done["fox_o"] = red[0], red[1]
            if i == 0:
                done["small"] = red[-1].reshape(-1)

        pending.append((mix_grads, f"mix_l{i}", mix_sink))

    groups, comm = take_pending()
    finish_groups(groups, _run_comm(comm, name="rs_chips_tail"))
    g_pool_w, g_fox_in, g_fox_o, g_small_flat = done["pool_w"], done["fox_in"], done["fox_o"], done["small"]

    grad_x = dcur.reshape(bsz, seq, d)

    rep_parts = [jnp.concatenate(g_norm_mix), jnp.concatenate(g_norm_ffn), g_pool_scale, g_bf, g_qg, g_kg,
                 jnp.concatenate(g_ffn_dw_b)]
    rep_shapes = [norm_mix.shape, norm_ffn.shape, pool_scale.shape, fox_b_f.shape, fox_q_gain.shape,
                  fox_k_gain.shape, ffn_dw_b.shape]
    rep_rows = _pack_rows([math.prod(s) for s in rep_shapes], 8)
    rep = _all_reduce_small(_pack([p_.reshape(1, -1) for p_ in rep_parts], rep_rows)[0], name="all_reduce_small")
    g_rep = _unpack(rep.reshape(-1), rep_shapes)
    g_sm = _unpack(g_small_flat, small_shapes)

    grads = {
        "norm_mix": g_rep[0], "norm_ffn": g_rep[1],
        "conv_w_in": jnp.stack(g_conv_in), "conv_b_in": g_sm[0], "conv_dw": g_sm[1], "conv_dw_b": g_sm[2],
        "conv_ln_g": g_sm[3], "conv_ln_b": g_sm[4], "conv_w_out": jnp.stack(g_conv_out), "conv_b_out": g_sm[5],
        "pool_w": g_pool_w.reshape(pool_w.shape), "pool_b": g_sm[6], "pool_scale": g_rep[2],
        "fox_w_in": g_fox_in.reshape(fox_w_in.shape), "fox_b_f": g_rep[3], "fox_q_gain": g_rep[4],
        "fox_k_gain": g_rep[5], "fox_w_o": g_fox_o.reshape(fox_w_o.shape),
        "ffn_w_up": jnp.stack(g_up), "ffn_dw": g_sm[7], "ffn_dw_b": g_rep[6], "ffn_w_down": jnp.stack(g_down),
    }
    weights = dict(norm_mix=norm_mix, norm_ffn=norm_ffn, conv_w_in=conv_w_in, conv_b_in=conv_b_in, conv_dw=conv_dw,
                   conv_dw_b=conv_dw_b, conv_ln_g=conv_ln_g, conv_ln_b=conv_ln_b, conv_w_out=conv_w_out,
                   conv_b_out=conv_b_out, pool_w=pool_w, pool_b=pool_b, pool_scale=pool_scale, fox_w_in=fox_w_in,
                   fox_b_f=fox_b_f, fox_q_gain=fox_q_gain, fox_k_gain=fox_k_gain, fox_w_o=fox_w_o, ffn_w_up=ffn_w_up,
                   ffn_dw=ffn_dw, ffn_dw_b=ffn_dw_b, ffn_w_down=ffn_w_down)
    m_in = dict(norm_mix=m_norm_mix, norm_ffn=m_norm_ffn, conv_w_in=m_conv_w_in, conv_b_in=m_conv_b_in,
                conv_dw=m_conv_dw, conv_dw_b=m_conv_dw_b, conv_ln_g=m_conv_ln_g, conv_ln_b=m_conv_ln_b,
                conv_w_out=m_conv_w_out, conv_b_out=m_conv_b_out, pool_w=m_pool_w, pool_b=m_pool_b,
                pool_scale=m_pool_scale, fox_w_in=m_fox_w_in, fox_b_f=m_fox_b_f, fox_q_gain=m_fox_q_gain,
                fox_k_gain=m_fox_k_gain, fox_w_o=m_fox_w_o, ffn_w_up=m_ffn_w_up, ffn_dw=m_ffn_dw,
                ffn_dw_b=m_ffn_dw_b, ffn_w_down=m_ffn_w_down)
    v_in = dict(norm_mix=v_norm_mix, norm_ffn=v_norm_ffn, conv_w_in=v_conv_w_in, conv_b_in=v_conv_b_in,
                conv_dw=v_conv_dw, conv_dw_b=v_conv_dw_b, conv_ln_g=v_conv_ln_g, conv_ln_b=v_conv_ln_b,
                conv_w_out=v_conv_w_out, conv_b_out=v_conv_b_out, pool_w=v_pool_w, pool_b=v_pool_b,
                pool_scale=v_pool_scale, fox_w_in=v_fox_w_in, fox_b_f=v_fox_b_f, fox_q_gain=v_fox_q_gain,
                fox_k_gain=v_fox_k_gain, fox_w_o=v_fox_w_o, ffn_w_up=v_ffn_w_up, ffn_dw=v_ffn_dw,
                ffn_dw_b=v_ffn_dw_b, ffn_w_down=v_ffn_w_down)
    names = list(weights)
    g_out, d_out, m_out, v_out = [], [], [], []
    for nm in names:
        g_, dl_, m_, v_ = _adamw_nd(weights[nm], grads[nm].reshape(weights[nm].shape), m_in[nm], v_in[nm],
                                    name=f"adamw_{nm}")
        g_out.append(g_)
        d_out.append(dl_)
        m_out.append(m_)
        v_out.append(v_)
    return (loss, grad_x, *g_out, *d_out, *m_out, *v_out)
```
